```python
import jax, jax.numpy as jnp
from jax import lax
import numpy as np


D_MODEL = 1024
BATCH = 8
SEQ = 8192
DEPTH = 2

GRID_W = 64
CTX_LEN = 256
N_MOD = 9
NA_HEADS = 8
HEAD_DIM = 64
NA_WIDTH = NA_HEADS * HEAD_DIM
NA_KH = 8
NA_KW = 16
POOL_GROUPS = 4
POOL_CH = 128
POOL_WIDTH = POOL_GROUPS * POOL_CH
POOL_WINDOWS = (2, 4, 8, 16)
MIX_WIDTH = NA_WIDTH + POOL_WIDTH
IN_WIDTH = 3 * NA_WIDTH + POOL_WIDTH
D_FF = 2816
ROPE_THETA = 10000.0
ROPE_PAIRS = HEAD_DIM // 4
RMS_EPS = 1e-6
NEG_INF = -1e30

kernel_name = 'hybrid_na_pool_macaron_dit_block'


def rms_norm(x, g):
    xf = x.astype(jnp.float32)
    y = xf * lax.rsqrt(jnp.mean(xf * xf, axis=-1, keepdims=True) + RMS_EPS)
    return (y * g.astype(jnp.float32)).astype(x.dtype)


def modulate(h, shift, scale):
    return h * (1.0 + scale) + shift


def mod_vectors(cvec, w_mod, b_mod):
    m = jax.nn.silu(cvec) @ w_mod + b_mod
    return jnp.split(m, N_MOD, axis=-1)


def swiglu(h, w_gate_up, w_down):
    gate, up = jnp.split(h @ w_gate_up, 2, axis=-1)
    return (jax.nn.silu(gate) * up) @ w_down


def sandwich_ffn(x, w_gate_up, w_down, g_pre, g_post, shift, scale, gate):
    h = modulate(rms_norm(x, g_pre), shift, scale)
    return x + 0.5 * gate * rms_norm(swiglu(h, w_gate_up, w_down), g_post)


def axial_rope(x, rows):
    seq = rows * GRID_W
    t = jnp.arange(seq)
    inv = ROPE_THETA ** (-jnp.arange(ROPE_PAIRS, dtype=jnp.float32) / ROPE_PAIRS)

    def rot(xa, pos):
        ang = pos.astype(jnp.float32)[:, None] * inv
        cos = jnp.cos(ang)[:, None, :]
        sin = jnp.sin(ang)[:, None, :]
        x1, x2 = jnp.split(xa, 2, axis=-1)
        return jnp.concatenate([x1 * cos - x2 * sin, x2 * cos + x1 * sin], axis=-1)

    xr, xc = jnp.split(x.astype(jnp.float32), 2, axis=-1)
    return jnp.concatenate([rot(xr, t // GRID_W), rot(xc, t % GRID_W)], axis=-1).astype(x.dtype)


def neighbourhood_attention(q, k, v, kc, vc, rpb, rows):
    b, s, h, dh = q.shape
    kh = min(NA_KH, rows)
    scale = dh ** -0.5
    q = q.reshape(b, rows, GRID_W, h, dh)
    k = k.reshape(b, rows, GRID_W, h, dh)
    v = v.reshape(b, rows, GRID_W, h, dh)
    r = np.arange(rows)
    row_start = np.clip(r - kh // 2, 0, rows - kh)
    row_idx = row_start[:, None] + np.arange(kh)[None, :]
    k_blk = k[:, row_idx]
    v_blk = v[:, row_idx]
    j = np.arange(GRID_W)
    col_start = np.clip(j - NA_KW // 2, 0, GRID_W - NA_KW)
    col_valid = (j[None, :] >= col_start[:, None]) & (j[None, :] < col_start[:, None] + NA_KW)
    dr = row_idx - r[:, None]
    dc = np.clip(j[None, :] - j[:, None] + NA_KW - 1, 0, 2 * NA_KW - 2)
    bias = rpb[:, dr + NA_KH - 1]
    bias = bias[..., dc]
    bias = jnp.transpose(bias, (0, 1, 3, 2, 4)).astype(jnp.float32)
    s_loc = jnp.einsum('brqhd,brkchd->bhrqkc', q, k_blk, preferred_element_type=jnp.float32) * scale
    s_loc = jnp.where(col_valid[:, None, :], s_loc + bias, NEG_INF)
    s_ctx = jnp.einsum('brqhd,bkhd->bhrqk', q, kc, preferred_element_type=jnp.float32) * scale
    n_loc = kh * GRID_W
    scores = jnp.concatenate([s_loc.reshape(b, h, rows, GRID_W, n_loc), s_ctx], axis=-1)
    p = jax.nn.softmax(scores, axis=-1).astype(v.dtype)
    p_loc = p[..., :n_loc].reshape(b, h, rows, GRID_W, kh, GRID_W)
    p_ctx = p[..., n_loc:]
    o = jnp.einsum('bhrqkc,brkchd->brqhd', p_loc, v_blk) + jnp.einsum('bhrqk,bkhd->brqhd', p_ctx, vc)
    return o.reshape(b, s, h * dh)


def context_attention(q, k, v):
    b, l, h, dh = q.shape
    sc = jnp.einsum('bqhd,bkhd->bhqk', q, k, preferred_element_type=jnp.float32) * dh ** -0.5
    p = jax.nn.softmax(sc, axis=-1).astype(v.dtype)
    return jnp.einsum('bhqk,bkhd->bqhd', p, v).reshape(b, l, h * dh)


def pool_mix(u, w_pool, pool_scale):
    length = u.shape[-2]
    ug = u.reshape(u.shape[:-1] + (POOL_GROUPS, POOL_CH))
    uf = ug.astype(jnp.float32)
    cs = jnp.cumsum(uf, axis=-3)
    pad = [(0, 0)] * cs.ndim
    pad[-3] = (1, 0)
    cs = jnp.pad(cs, pad)
    t = np.arange(length)
    outs = []
    for g, w in enumerate(POOL_WINDOWS):
        lo = np.clip(t - w // 2, 0, length)
        hi = np.clip(t - w // 2 + w, 0, length)
        cnt = (hi - lo).astype(np.float32)[:, None]
        csg = cs[..., g, :]
        outs.append((jnp.take(csg, hi, axis=-2) - jnp.take(csg, lo, axis=-2)) / cnt)
    pooled = jnp.stack(outs, axis=-2)
    d = (pooled - uf).astype(u.dtype)
    y = jnp.einsum('...gc,gcd->...gd', d, w_pool) * pool_scale.reshape(POOL_GROUPS, POOL_CH)
    return y.reshape(u.shape)


def token_mix(hx, hc, w_in, w_out, rpb, w_pool, pool_scale, with_ctx_out):
    b, s, _ = hx.shape
    rows = s // GRID_W
    l = hc.shape[1]
    qx, kx, vx, ux = jnp.split(hx @ w_in, [NA_WIDTH, 2 * NA_WIDTH, 3 * NA_WIDTH], axis=-1)
    if with_ctx_out:
        qc, kc, vc, uc = jnp.split(hc @ w_in, [NA_WIDTH, 2 * NA_WIDTH, 3 * NA_WIDTH], axis=-1)
    else:
        kc, vc = jnp.split(hc @ w_in[:, NA_WIDTH:3 * NA_WIDTH], 2, axis=-1)
    heads = lambda t, n: t.reshape(b, n, NA_HEADS, HEAD_DIM)
    qx = axial_rope(heads(qx, s), rows)
    kx = axial_rope(heads(kx, s), rows)
    kc_h, vc_h = heads(kc, l), heads(vc, l)
    na_x = neighbourhood_attention(qx, kx, heads(vx, s), kc_h, vc_h, rpb, rows)
    pool_x = pool_mix(ux.reshape(b, rows, GRID_W, POOL_WIDTH), w_pool, pool_scale).reshape(b, s, POOL_WIDTH)
    out_x = jnp.concatenate([na_x, pool_x], axis=-1) @ w_out
    if with_ctx_out:
        na_c = context_attention(heads(qc, l), kc_h, vc_h)
        pool_c = pool_mix(uc, w_pool, pool_scale)
        out_c = jnp.concatenate([na_c, pool_c], axis=-1) @ w_out
        return out_x, out_c
    return out_x, None


def _fwd_setup_inputs(seed: int = 0) -> dict:
    key = jax.random.key(seed)
    ks = jax.random.split(key, 14)
    nrm = jax.random.normal
    f32 = jnp.float32
    return {
        'x': nrm(ks[0], (BATCH, SEQ, D_MODEL), f32),
        'c': nrm(ks[1], (BATCH, D_MODEL), f32),
        'ctx': nrm(ks[2], (BATCH, CTX_LEN, D_MODEL), f32),
        'c_ctx': nrm(ks[3], (D_MODEL,), f32),
        'w_mod': nrm(ks[4], (DEPTH, D_MODEL, N_MOD * D_MODEL), f32) * (0.5 * D_MODEL ** -0.5),
        'b_mod': nrm(ks[5], (DEPTH, N_MOD * D_MODEL), f32) * 0.01,
        'norm_g': 1.0 + 0.05 * nrm(ks[6], (DEPTH, 6, D_MODEL), f32),
        'w_ffn_gate_up': nrm(ks[7], (DEPTH, 2, D_MODEL, 2 * D_FF), f32) * D_MODEL ** -0.5,
        'w_ffn_down': nrm(ks[8], (DEPTH, 2, D_FF, D_MODEL), f32) * D_FF ** -0.5,
        'w_in': nrm(ks[9], (DEPTH, D_MODEL, IN_WIDTH), f32) * D_MODEL ** -0.5,
        'w_out': nrm(ks[10], (DEPTH, MIX_WIDTH, D_MODEL), f32) * MIX_WIDTH ** -0.5,
        'na_rpb': nrm(ks[11], (DEPTH, NA_HEADS, 2 * NA_KH - 1, 2 * NA_KW - 1), f32) * 0.1,
        'w_pool': nrm(ks[12], (DEPTH, POOL_GROUPS, POOL_CH, POOL_CH), f32) * POOL_CH ** -0.5,
        'pool_scale': 1.0 + 0.05 * nrm(ks[13], (DEPTH, POOL_WIDTH), f32),
    }


def _fwd_reference(x, c, ctx, c_ctx, w_mod, b_mod, norm_g, w_ffn_gate_up, w_ffn_down, w_in, w_out, na_rpb, w_pool, pool_scale):
    for l in range(DEPTH):
        last = l == DEPTH - 1
        mx = [m[:, None, :] for m in mod_vectors(c, w_mod[l], b_mod[l])]
        mc = mod_vectors(c_ctx, w_mod[l], b_mod[l])
        g = norm_g[l]
        x = sandwich_ffn(x, w_ffn_gate_up[l, 0], w_ffn_down[l, 0], g[0], g[1], mx[0], mx[1], mx[2])
        ctx = sandwich_ffn(ctx, w_ffn_gate_up[l, 0], w_ffn_down[l, 0], g[0], g[1], mc[0], mc[1], mc[2])
        hx = modulate(rms_norm(x, g[2]), mx[3], mx[4])
        hc = modulate(rms_norm(ctx, g[2]), mc[3], mc[4])
        out_x, out_c = token_mix(hx, hc, w_in[l], w_out[l], na_rpb[l], w_pool[l], pool_scale[l], not last)
        x = x + mx[5] * rms_norm(out_x, g[3])
        x = sandwich_ffn(x, w_ffn_gate_up[l, 1], w_ffn_down[l, 1], g[4], g[5], mx[6], mx[7], mx[8])
        if not last:
            ctx = ctx + mc[5] * rms_norm(out_c, g[3])
            ctx = sandwich_ffn(ctx, w_ffn_gate_up[l, 1], w_ffn_down[l, 1], g[4], g[5], mc[6], mc[7], mc[8])
    return x


import jax as _jax
import jax.numpy as _jnp

TWIN_FORMAT = 'train_step'
FWD_PARAMS = ['x', 'c', 'ctx', 'c_ctx', 'w_mod', 'b_mod', 'norm_g', 'w_ffn_gate_up', 'w_ffn_down', 'w_in', 'w_out', 'na_rpb', 'w_pool', 'pool_scale']
TWIN_WEIGHTS = ['c_ctx', 'w_mod', 'b_mod', 'norm_g', 'w_ffn_gate_up', 'w_ffn_down', 'w_in', 'w_out', 'na_rpb', 'w_pool', 'pool_scale']
TWIN_DIFF_INPUT = 'x'
TWIN_INPUTS = ['x', 'c', 'ctx', 'c_ctx', 'w_mod', 'b_mod', 'norm_g', 'w_ffn_gate_up', 'w_ffn_down', 'w_in', 'w_out', 'na_rpb', 'w_pool', 'pool_scale', 'loss_target', 'm_c_ctx', 'm_w_mod', 'm_b_mod', 'm_norm_g', 'm_w_ffn_gate_up', 'm_w_ffn_down', 'm_w_in', 'm_w_out', 'm_na_rpb', 'm_w_pool', 'm_pool_scale', 'v_c_ctx', 'v_w_mod', 'v_b_mod', 'v_norm_g', 'v_w_ffn_gate_up', 'v_w_ffn_down', 'v_w_in', 'v_w_out', 'v_na_rpb', 'v_w_pool', 'v_pool_scale']
TWIN_OUTPUTS = ['loss', 'grad_x', 'grad_c_ctx', 'grad_w_mod', 'grad_b_mod', 'grad_norm_g', 'grad_w_ffn_gate_up', 'grad_w_ffn_down', 'grad_w_in', 'grad_w_out', 'grad_na_rpb', 'grad_w_pool', 'grad_pool_scale', 'delta_c_ctx', 'delta_w_mod', 'delta_b_mod', 'delta_norm_g', 'delta_w_ffn_gate_up', 'delta_w_ffn_down', 'delta_w_in', 'delta_w_out', 'delta_na_rpb', 'delta_w_pool', 'delta_pool_scale', 'new_m_c_ctx', 'new_m_w_mod', 'new_m_b_mod', 'new_m_norm_g', 'new_m_w_ffn_gate_up', 'new_m_w_ffn_down', 'new_m_w_in', 'new_m_w_out', 'new_m_na_rpb', 'new_m_w_pool', 'new_m_pool_scale', 'new_v_c_ctx', 'new_v_w_mod', 'new_v_b_mod', 'new_v_norm_g', 'new_v_w_ffn_gate_up', 'new_v_w_ffn_down', 'new_v_w_in', 'new_v_w_out', 'new_v_na_rpb', 'new_v_w_pool', 'new_v_pool_scale']
TWIN_LEAF_KINDS = {'loss': 'loss', 'grad_x': 'grad_x', 'grad_c_ctx': 'grad_w', 'grad_w_mod': 'grad_w', 'grad_b_mod': 'grad_w', 'grad_norm_g': 'grad_w', 'grad_w_ffn_gate_up': 'grad_w', 'grad_w_ffn_down': 'grad_w', 'grad_w_in': 'grad_w', 'grad_w_out': 'grad_w', 'grad_na_rpb': 'grad_w', 'grad_w_pool': 'grad_w', 'grad_pool_scale': 'grad_w', 'delta_c_ctx': 'delta_w', 'delta_w_mod': 'delta_w', 'delta_b_mod': 'delta_w', 'delta_norm_g': 'delta_w', 'delta_w_ffn_gate_up': 'delta_w', 'delta_w_ffn_down': 'delta_w', 'delta_w_in': 'delta_w', 'delta_w_out': 'delta_w', 'delta_na_rpb': 'delta_w', 'delta_w_pool': 'delta_w', 'delta_pool_scale': 'delta_w', 'new_m_c_ctx': 'new_m', 'new_m_w_mod': 'new_m', 'new_m_b_mod': 'new_m', 'new_m_norm_g': 'new_m', 'new_m_w_ffn_gate_up': 'new_m', 'new_m_w_ffn_down': 'new_m', 'new_m_w_in': 'new_m', 'new_m_w_out': 'new_m', 'new_m_na_rpb': 'new_m', 'new_m_w_pool': 'new_m', 'new_m_pool_scale': 'new_m', 'new_v_c_ctx': 'new_v', 'new_v_w_mod': 'new_v', 'new_v_b_mod': 'new_v', 'new_v_norm_g': 'new_v', 'new_v_w_ffn_gate_up': 'new_v', 'new_v_w_ffn_down': 'new_v', 'new_v_w_in': 'new_v', 'new_v_w_out': 'new_v', 'new_v_na_rpb': 'new_v', 'new_v_w_pool': 'new_v', 'new_v_pool_scale': 'new_v'}


def _forward(args):
    return _fwd_reference(*[args[k] for k in FWD_PARAMS])


def _output_shape():
    def fwd():
        inp = _fwd_setup_inputs(0)
        return _fwd_reference(*[inp[k] for k in FWD_PARAMS])
    out = _jax.eval_shape(fwd)
    return out.shape, out.dtype

N_MICROBATCH = 1
ADAM_LR = 0.001
ADAM_B1 = 0.9
ADAM_B2 = 0.999
ADAM_EPS = 1e-08
ADAM_WD = 0.01
ADAM_STEP = 10
PER_EXAMPLE_BATCH_AXIS = {'x': 0, 'c': 0, 'ctx': 0, 'loss_target': 0}
SHARED_INPUTS = []
_WEIGHT_DTYPES = {'c_ctx': _jnp.float32, 'w_mod': _jnp.float32, 'b_mod': _jnp.float32, 'norm_g': _jnp.float32, 'w_ffn_gate_up': _jnp.float32, 'w_ffn_down': _jnp.float32, 'w_in': _jnp.float32, 'w_out': _jnp.float32, 'na_rpb': _jnp.float32, 'w_pool': _jnp.float32, 'pool_scale': _jnp.float32}
MOMENT_SCALE = {'c_ctx': 1.824031e-01, 'w_mod': 1.490144e+00, 'b_mod': 3.261525e+00, 'norm_g': 2.844164e+00, 'w_ffn_gate_up': 4.484650e-02, 'w_ffn_down': 7.924602e-02, 'w_in': 1.865706e-01, 'w_out': 2.997748e-01, 'na_rpb': 5.303365e-03, 'w_pool': 3.246670e-01, 'pool_scale': 3.710635e-01}


def _to_microbatches(a, axis):
    t = _jnp.moveaxis(a, axis, 0)
    t = t.reshape((N_MICROBATCH, t.shape[0] // N_MICROBATCH) + t.shape[1:])
    return _jnp.moveaxis(t, 1, axis + 1)


def setup_inputs(seed: int = 0) -> dict:
    inp = _fwd_setup_inputs(seed)
    key = _jax.random.fold_in(_jax.random.key(seed), 7919)
    shape, _ = _output_shape()
    out = dict(inp)
    out["loss_target"] = _jax.random.normal(_jax.random.fold_in(key, 0), shape, _jnp.float32)
    for i, name in enumerate(TWIN_WEIGHTS):
        w = inp[name].astype(_jnp.float32)
        if MOMENT_SCALE is None:
            s = _jnp.sqrt(_jnp.mean(_jnp.square(w)) + 1e-30)
        else:
            s = MOMENT_SCALE[name]
        km, kv = _jax.random.split(_jax.random.fold_in(key, i + 1))
        out[name] = w
        out["m_" + name] = s * _jax.random.normal(km, w.shape, _jnp.float32)
        out["v_" + name] = (s * s) * _jax.random.uniform(kv, w.shape, _jnp.float32, 0.5, 1.5)
    if N_MICROBATCH > 1:
        for name, axis in PER_EXAMPLE_BATCH_AXIS.items():
            out[name] = _to_microbatches(out[name], axis)
    return {'x': out['x'], 'c': out['c'], 'ctx': out['ctx'], 'c_ctx': out['c_ctx'], 'w_mod': out['w_mod'], 'b_mod': out['b_mod'], 'norm_g': out['norm_g'], 'w_ffn_gate_up': out['w_ffn_gate_up'], 'w_ffn_down': out['w_ffn_down'], 'w_in': out['w_in'], 'w_out': out['w_out'], 'na_rpb': out['na_rpb'], 'w_pool': out['w_pool'], 'pool_scale': out['pool_scale'], 'loss_target': out['loss_target'], 'm_c_ctx': out['m_c_ctx'], 'm_w_mod': out['m_w_mod'], 'm_b_mod': out['m_b_mod'], 'm_norm_g': out['m_norm_g'], 'm_w_ffn_gate_up': out['m_w_ffn_gate_up'], 'm_w_ffn_down': out['m_w_ffn_down'], 'm_w_in': out['m_w_in'], 'm_w_out': out['m_w_out'], 'm_na_rpb': out['m_na_rpb'], 'm_w_pool': out['m_w_pool'], 'm_pool_scale': out['m_pool_scale'], 'v_c_ctx': out['v_c_ctx'], 'v_w_mod': out['v_w_mod'], 'v_b_mod': out['v_b_mod'], 'v_norm_g': out['v_norm_g'], 'v_w_ffn_gate_up': out['v_w_ffn_gate_up'], 'v_w_ffn_down': out['v_w_ffn_down'], 'v_w_in': out['v_w_in'], 'v_w_out': out['v_w_out'], 'v_na_rpb': out['v_na_rpb'], 'v_w_pool': out['v_w_pool'], 'v_pool_scale': out['v_pool_scale']}


def _loss(weights, diff, rest, loss_target):
    with _jax.named_scope("forward"):
        args = {**rest, TWIN_DIFF_INPUT: diff, **{k: w.astype(_WEIGHT_DTYPES[k]) for k, w in weights.items()}}
        y = _forward(args)
    with _jax.named_scope("loss_head"):
        err = _jnp.square(y.astype(_jnp.float32) - loss_target)
        return 0.5 * _jnp.sum(_jnp.mean(err, axis=-1)) if err.ndim else 0.5 * err


def _adamw(w, g, m, v):
    m = ADAM_B1 * m + (1.0 - ADAM_B1) * g
    v = ADAM_B2 * v + (1.0 - ADAM_B2) * _jnp.square(g)
    m_hat = m / (1.0 - ADAM_B1 ** ADAM_STEP)
    v_hat = v / (1.0 - ADAM_B2 ** ADAM_STEP)
    delta = -ADAM_LR * (m_hat / (_jnp.sqrt(v_hat) + ADAM_EPS) + ADAM_WD * w)
    return delta, m, v


def reference(x, c, ctx, c_ctx, w_mod, b_mod, norm_g, w_ffn_gate_up, w_ffn_down, w_in, w_out, na_rpb, w_pool, pool_scale, loss_target, m_c_ctx, m_w_mod, m_b_mod, m_norm_g, m_w_ffn_gate_up, m_w_ffn_down, m_w_in, m_w_out, m_na_rpb, m_w_pool, m_pool_scale, v_c_ctx, v_w_mod, v_b_mod, v_norm_g, v_w_ffn_gate_up, v_w_ffn_down, v_w_in, v_w_out, v_na_rpb, v_w_pool, v_pool_scale):
    given = dict(x=x, c=c, ctx=ctx, c_ctx=c_ctx, w_mod=w_mod, b_mod=b_mod, norm_g=norm_g, w_ffn_gate_up=w_ffn_gate_up, w_ffn_down=w_ffn_down, w_in=w_in, w_out=w_out, na_rpb=na_rpb, w_pool=w_pool, pool_scale=pool_scale, loss_target=loss_target, m_c_ctx=m_c_ctx, m_w_mod=m_w_mod, m_b_mod=m_b_mod, m_norm_g=m_norm_g, m_w_ffn_gate_up=m_w_ffn_gate_up, m_w_ffn_down=m_w_ffn_down, m_w_in=m_w_in, m_w_out=m_w_out, m_na_rpb=m_na_rpb, m_w_pool=m_w_pool, m_pool_scale=m_pool_scale, v_c_ctx=v_c_ctx, v_w_mod=v_w_mod, v_b_mod=v_b_mod, v_norm_g=v_norm_g, v_w_ffn_gate_up=v_w_ffn_gate_up, v_w_ffn_down=v_w_ffn_down, v_w_in=v_w_in, v_w_out=v_w_out, v_na_rpb=v_na_rpb, v_w_pool=v_w_pool, v_pool_scale=v_pool_scale)
    weights = {n: given[n] for n in TWIN_WEIGHTS}
    shared = {n: given[n] for n in SHARED_INPUTS}
    per_example = {n: given[n] for n in ['x', 'c', 'ctx']}
    grad_fn = _jax.value_and_grad(_loss, argnums=(0, 1))

    def one_microbatch(ex, loss_target):
        ex = dict(ex)
        diff = ex.pop(TWIN_DIFF_INPUT)
        return grad_fn(weights, diff, {**shared, **ex}, loss_target)

    if N_MICROBATCH == 1:
        loss, (grad_w, grad_x) = one_microbatch(per_example, given["loss_target"])
    else:
        def body(carry, xs):
            loss_sum, grad_sum = carry
            l_k, (gw_k, gx_k) = one_microbatch(xs[0], xs[1])
            with _jax.named_scope("update"):
                return (loss_sum + l_k, _jax.tree.map(_jnp.add, grad_sum, gw_k)), gx_k

        init = (_jnp.zeros((), _jnp.float32), _jax.tree.map(_jnp.zeros_like, weights))
        (loss, grad_w), grad_x = _jax.lax.scan(body, init, (per_example, given["loss_target"]))
    with _jax.named_scope("update"):
        delta_w, new_m, new_v = {}, {}, {}
        for n in TWIN_WEIGHTS:
            delta_w[n], new_m[n], new_v[n] = _adamw(weights[n], grad_w[n], given["m_" + n], given["v_" + n])
    return (loss, grad_x, *[grad_w[n] for n in TWIN_WEIGHTS], *[delta_w[n] for n in TWIN_WEIGHTS],
            *[new_m[n] for n in TWIN_WEIGHTS], *[new_v[n] for n in TWIN_WEIGHTS])
```

```python
import functools
import math

import numpy as np
import jax
import jax.numpy as jnp
from jax import lax
from jax.experimental import pallas as pl
from jax.experimental.pallas import tpu as pltpu

F32 = jnp.float32
BF16 = jnp.bfloat16

N_DEV = 8
DEPTH = 2
GRID_W = 64
N_MOD = 9
NA_HEADS = 8
HEAD_DIM = 64
NA_WIDTH = NA_HEADS * HEAD_DIM
NA_KH = 8
NA_KW = 16
POOL_GROUPS = 4
POOL_CH = 128
POOL_WIDTH = POOL_GROUPS * POOL_CH
POOL_WINDOWS = (2, 4, 8, 16)
ROPE_THETA = 10000.0
ROPE_PAIRS = HEAD_DIM // 4
RMS_EPS = 1e-6
NEG_INF = -1e30
ATT_SCALE = HEAD_DIM ** -0.5

ADAM_LR = 0.001
ADAM_B1 = 0.9
ADAM_B2 = 0.999
ADAM_EPS = 1e-08
ADAM_WD = 0.01
ADAM_STEP = 10

TM = 256
LANES = 128
HEADS_PER_BLOCK = LANES // HEAD_DIM
N_HEAD_BLOCKS = NA_WIDTH // LANES
VMEM_LIMIT = 48 * 1024 * 1024
HIGHEST = lax.Precision.HIGHEST
MESH = pl.DeviceIdType.MESH

NN = (((1,), (0,)), ((), ()))
NT = (((1,), (1,)), ((), ()))
TN = (((0,), (0,)), ((), ()))


def _params(*sem):
    return pltpu.CompilerParams(dimension_semantics=sem, vmem_limit_bytes=VMEM_LIMIT)


def _pick(n, cands):
    for t in cands:
        if n % t == 0:
            return t
    raise ValueError(f"no tile for {n} among {cands}")


def _dot(a, b, dn=NN, precision=None):
    return lax.dot_general(a, b, dn, preferred_element_type=F32, precision=precision)


def _silu(x):
    return x * jax.nn.sigmoid(x)


def _dsilu(x):
    s = jax.nn.sigmoid(x)
    return s * (1.0 + x * (1.0 - s))


def _peer(mask):
    x, y, c = lax.axis_index("x"), lax.axis_index("y"), lax.axis_index("c")
    px = 1 - x if mask & 4 else x
    py = 1 - y if mask & 2 else y
    pc = 1 - c if mask & 1 else c
    return (px, py, pc), 4 * px + 2 * py + pc


def all_gather(v, name):
    def body(v_ref, o_ref, send_sems, recv_sems, local_sem):
        _, me = _peer(0)
        mine = pltpu.make_async_copy(v_ref, o_ref.at[me], local_sem)
        mine.start()
        sends = []
        for mask in range(1, N_DEV):
            peer, _ = _peer(mask)
            cp = pltpu.make_async_remote_copy(
                src_ref=v_ref, dst_ref=o_ref.at[me], send_sem=send_sems.at[mask - 1],
                recv_sem=recv_sems.at[mask - 1], device_id=peer, device_id_type=MESH)
            cp.start()
            sends.append(cp)
        for mask in range(1, N_DEV):
            peer, pid = _peer(mask)
            pltpu.make_async_remote_copy(
                src_ref=v_ref, dst_ref=o_ref.at[pid], send_sem=send_sems.at[mask - 1],
                recv_sem=recv_sems.at[mask - 1], device_id=peer, device_id_type=MESH).wait_recv()
        for cp in sends:
            cp.wait_send()
        mine.wait()

    return pl.pallas_call(
        body, name=name,
        out_shape=jax.ShapeDtypeStruct((N_DEV,) + v.shape, v.dtype),
        in_specs=[pl.BlockSpec(memory_space=pl.ANY)],
        out_specs=pl.BlockSpec(memory_space=pl.ANY),
        scratch_shapes=[pltpu.SemaphoreType.DMA((N_DEV - 1,)), pltpu.SemaphoreType.DMA((N_DEV - 1,)),
                        pltpu.SemaphoreType.DMA],
    )(v)


def all_to_all(v, name):
    def body(v_ref, o_ref, send_sems, recv_sems, local_sem):
        _, me = _peer(0)
        mine = pltpu.make_async_copy(v_ref.at[me], o_ref.at[me], local_sem)
        mine.start()
        sends = []
        for mask in range(1, N_DEV):
            peer, pid = _peer(mask)
            cp = pltpu.make_async_remote_copy(
                src_ref=v_ref.at[pid], dst_ref=o_ref.at[me], send_sem=send_sems.at[mask - 1],
                recv_sem=recv_sems.at[mask - 1], device_id=peer, device_id_type=MESH)
            cp.start()
            sends.append(cp)
        for mask in range(1, N_DEV):
            peer, pid = _peer(mask)
            pltpu.make_async_remote_copy(
                src_ref=v_ref.at[pid], dst_ref=o_ref.at[pid], send_sem=send_sems.at[mask - 1],
                recv_sem=recv_sems.at[mask - 1], device_id=peer, device_id_type=MESH).wait_recv()
        for cp in sends:
            cp.wait_send()
        mine.wait()

    return pl.pallas_call(
        body, name=name,
        out_shape=jax.ShapeDtypeStruct(v.shape, v.dtype),
        in_specs=[pl.BlockSpec(memory_space=pl.ANY)],
        out_specs=pl.BlockSpec(memory_space=pl.ANY),
        scratch_shapes=[pltpu.SemaphoreType.DMA((N_DEV - 1,)), pltpu.SemaphoreType.DMA((N_DEV - 1,)),
                        pltpu.SemaphoreType.DMA],
    )(v)


def matmul(a, b, mode, name, out_dtype=F32, tm=None, tn=None, tk=None):
    if mode == "nn":
        (m, k), n = a.shape, b.shape[1]
    elif mode == "nt":
        (m, k), n = a.shape, b.shape[0]
    else:
        (k, m), n = a.shape, b.shape[1]
    tm = tm or _pick(m, (384, 512, 256, 128))
    tn = tn or _pick(n, (1024, 1408, 512, 256, 128))
    tk = tk or _pick(k, (1024, 1408, 768, 512, 256, 128))
    nk = k // tk
    dn = {"nn": NN, "nt": NT, "tn": TN}[mode]

    def body(a_ref, b_ref, o_ref, acc_ref):
        kk = pl.program_id(2)

        @pl.when(kk == 0)
        def _():
            acc_ref[...] = jnp.zeros_like(acc_ref)

        acc_ref[...] += _dot(a_ref[...], b_ref[...], dn)

        @pl.when(kk == nk - 1)
        def _():
            o_ref[...] = acc_ref[...].astype(out_dtype)

    if mode == "nn":
        a_spec = pl.BlockSpec((tm, tk), lambda i, j, kk: (i, kk))
        b_spec = pl.BlockSpec((tk, tn), lambda i, j, kk: (kk, j))
    elif mode == "nt":
        a_spec = pl.BlockSpec((tm, tk), lambda i, j, kk: (i, kk))
        b_spec = pl.BlockSpec((tn, tk), lambda i, j, kk: (j, kk))
    else:
        a_spec = pl.BlockSpec((tk, tm), lambda i, j, kk: (kk, i))
        b_spec = pl.BlockSpec((tk, tn), lambda i, j, kk: (kk, j))
    return pl.pallas_call(
        body, name=name,
        out_shape=jax.ShapeDtypeStruct((m, n), out_dtype),
        grid=(m // tm, n // tn, nk),
        in_specs=[a_spec, b_spec],
        out_specs=pl.BlockSpec((tm, tn), lambda i, j, kk: (i, j)),
        scratch_shapes=[pltpu.VMEM((tm, tn), F32)],
        compiler_params=_params("parallel", "parallel", "arbitrary"),
    )(a, b)


def gate_up(hb, wg, wu, name):
    n, d = hb.shape
    f = wg.shape[1]
    tm = _pick(n, (384, 512, 256))
    tn = _pick(f, (1408, 512, 256, 128))

    def body(h_ref, wg_ref, wu_ref, g_ref, u_ref, a_ref):
        h = h_ref[...]
        g = _dot(h, wg_ref[...])
        u = _dot(h, wu_ref[...])
        g_ref[...] = g.astype(BF16)
        u_ref[...] = u.astype(BF16)
        a_ref[...] = (_silu(g) * u).astype(BF16)

    out = jax.ShapeDtypeStruct((n, f), BF16)
    tile = pl.BlockSpec((tm, tn), lambda i, j: (i, j))
    return pl.pallas_call(
        body, name=name, out_shape=(out, out, out), grid=(n // tm, f // tn),
        in_specs=[pl.BlockSpec((tm, d), lambda i, j: (i, 0)),
                  pl.BlockSpec((d, tn), lambda i, j: (0, j)),
                  pl.BlockSpec((d, tn), lambda i, j: (0, j))],
        out_specs=(tile, tile, tile),
        compiler_params=_params("parallel", "parallel"),
    )(hb, wg, wu)


def swiglu_bwd(g, u, da, name):
    n, f = g.shape

    def body(g_ref, u_ref, da_ref, o_ref):
        gg = g_ref[...].astype(F32)
        uu = u_ref[...].astype(F32)
        da_ = da_ref[...]
        o_ref[:, :f] = (da_ * uu * _dsilu(gg)).astype(BF16)
        o_ref[:, f:] = (da_ * _silu(gg)).astype(BF16)

    tile = pl.BlockSpec((TM, f), lambda i: (i, 0))
    return pl.pallas_call(
        body, name=name, out_shape=jax.ShapeDtypeStruct((n, 2 * f), BF16), grid=(n // TM,),
        in_specs=[tile, tile, tile],
        out_specs=pl.BlockSpec((TM, 2 * f), lambda i: (i, 0)),
        compiler_params=_params("parallel"),
    )(g, u, da)


def _mod_spec(nx_tiles, d):
    return pl.BlockSpec((1, N_MOD, d), lambda i: (i // nx_tiles, 0, 0))


def _rms(xf):
    return lax.rsqrt(jnp.mean(xf * xf, axis=-1, keepdims=True) + RMS_EPS)


def norm_mod(x, g, mod, k_shift, k_scale, nx_tiles, name):
    n, d = x.shape

    def body(x_ref, g_ref, mod_ref, o_ref):
        xf = x_ref[...]
        nrm = xf * _rms(xf) * g_ref[...]
        shift = mod_ref[0, k_shift:k_shift + 1, :]
        scale = mod_ref[0, k_scale:k_scale + 1, :]
        o_ref[...] = (nrm * (1.0 + scale) + shift).astype(BF16)

    tile = pl.BlockSpec((TM, d), lambda i: (i, 0))
    return pl.pallas_call(
        body, name=name, out_shape=jax.ShapeDtypeStruct((n, d), BF16), grid=(n // TM,),
        in_specs=[tile, pl.BlockSpec((1, d), lambda i: (0, 0)), _mod_spec(nx_tiles, d)],
        out_specs=tile, compiler_params=_params("parallel"),
    )(x, g, mod)


def post_fwd(x, f, g, mod, k_gate, coef, nx_tiles, name):
    n, d = x.shape

    def body(x_ref, f_ref, g_ref, mod_ref, o_ref):
        ff = f_ref[...]
        y = ff * _rms(ff) * g_ref[...]
        o_ref[...] = x_ref[...] + coef * mod_ref[0, k_gate:k_gate + 1, :] * y

    tile = pl.BlockSpec((TM, d), lambda i: (i, 0))
    return pl.pallas_call(
        body, name=name, out_shape=jax.ShapeDtypeStruct((n, d), F32), grid=(n // TM,),
        in_specs=[tile, tile, pl.BlockSpec((1, d), lambda i: (0, 0)), _mod_spec(nx_tiles, d)],
        out_specs=tile, compiler_params=_params("parallel"),
    )(x, f, g, mod)


def _red_spec(nx_tiles, d):
    return pl.BlockSpec((1, 8, d), lambda i: (i // nx_tiles, 0, 0))


def _red_store(red_ref, i, nx_tiles, rows):
    @pl.when(i % nx_tiles == 0)
    def _():
        red_ref[...] = jnp.zeros_like(red_ref)

    for r, val in enumerate(rows):
        red_ref[0, r:r + 1, :] += val


def post_bwd(f, dxo, g, mod, k_gate, coef, nx_tiles, name):
    n, d = f.shape
    groups = -(-(n // TM) // nx_tiles)

    def body(f_ref, dxo_ref, g_ref, mod_ref, df_ref, red_ref):
        i = pl.program_id(0)
        ff = f_ref[...]
        dxo_ = dxo_ref[...]
        gg = g_ref[...]
        r = _rms(ff)
        fn = ff * r
        dy = (coef * mod_ref[0, k_gate:k_gate + 1, :]) * dxo_
        dfn = dy * gg
        df = r * (dfn - fn * jnp.mean(dfn * fn, axis=-1, keepdims=True))
        df_ref[...] = df.astype(BF16)
        _red_store(red_ref, i, nx_tiles, [
            jnp.sum(coef * (fn * gg) * dxo_, axis=0, keepdims=True),
            jnp.sum(dy * fn, axis=0, keepdims=True)])

    tile = pl.BlockSpec((TM, d), lambda i: (i, 0))
    return pl.pallas_call(
        body, name=name,
        out_shape=(jax.ShapeDtypeStruct((n, d), BF16), jax.ShapeDtypeStruct((groups, 8, d), F32)),
        grid=(n // TM,),
        in_specs=[tile, tile, pl.BlockSpec((1, d), lambda i: (0, 0)), _mod_spec(nx_tiles, d)],
        out_specs=(tile, _red_spec(nx_tiles, d)),
        compiler_params=_params("arbitrary"),
    )(f, dxo, g, mod)


def pre_bwd(x, dh, dxo, g, mod, k_shift, k_scale, nx_tiles, name):
    n, d = x.shape
    groups = -(-(n // TM) // nx_tiles)

    def body(x_ref, dh_ref, dxo_ref, g_ref, mod_ref, dx_ref, red_ref):
        i = pl.program_id(0)
        xf = x_ref[...]
        dh_ = dh_ref[...]
        gg = g_ref[...]
        r = _rms(xf)
        xhat = xf * r
        dn = dh_ * (1.0 + mod_ref[0, k_scale:k_scale + 1, :])
        dxhat = dn * gg
        dx_ref[...] = dxo_ref[...] + r * (dxhat - xhat * jnp.mean(dxhat * xhat, axis=-1, keepdims=True))
        _red_store(red_ref, i, nx_tiles, [
            jnp.sum(dh_, axis=0, keepdims=True),
            jnp.sum(dh_ * (xhat * gg), axis=0, keepdims=True),
            jnp.sum(dn * xhat, axis=0, keepdims=True)])

    tile = pl.BlockSpec((TM, d), lambda i: (i, 0))
    return pl.pallas_call(
        body, name=name,
        out_shape=(jax.ShapeDtypeStruct((n, d), F32), jax.ShapeDtypeStruct((groups, 8, d), F32)),
        grid=(n // TM,),
        in_specs=[tile, tile, tile, pl.BlockSpec((1, d), lambda i: (0, 0)), _mod_spec(nx_tiles, d)],
        out_specs=(tile, _red_spec(nx_tiles, d)),
        compiler_params=_params("arbitrary"),
    )(x, dh, dxo, g, mod)


def _rope_tables(s, n):
    t = jnp.arange(n)
    lane = jnp.arange(LANES)
    dd = lane % HEAD_DIM
    inv = ROPE_THETA ** (-(dd % ROPE_PAIRS).astype(F32) / ROPE_PAIRS)
    pos = jnp.where(dd[None, :] < HEAD_DIM // 2, (t // GRID_W)[:, None], (t % GRID_W)[:, None]).astype(F32)
    ang = pos * inv[None, :]
    live = (t < s)[:, None]
    first = ((dd % (2 * ROPE_PAIRS)) < ROPE_PAIRS)[None, :]
    cos = jnp.where(live, jnp.cos(ang), 1.0)
    sin = jnp.where(live, jnp.sin(ang), 0.0)
    sa = jnp.where(first, -sin, 0.0)
    sb = jnp.where(first, 0.0, sin)
    return cos.astype(F32), sa.astype(F32), sb.astype(F32)


def _rope(xv, cos, sa, sb):
    return (xv * cos + pltpu.roll(xv, LANES - ROPE_PAIRS, 1) * sa + pltpu.roll(xv, ROPE_PAIRS, 1) * sb)


def qkv_post(qkvu, tables, name):
    n = qkvu.shape[0]
    w = NA_WIDTH

    def body(q_ref, k_ref, v_ref, c_ref, sa_ref, sb_ref, qo_ref, ko_ref, vo_ref):
        cos, sa, sb = c_ref[...], sa_ref[...], sb_ref[...]
        for b in range(N_HEAD_BLOCKS):
            sl = slice(b * LANES, (b + 1) * LANES)
            qo_ref[:, sl] = _rope(q_ref[:, sl], cos, sa, sb).astype(BF16)
            ko_ref[:, sl] = _rope(k_ref[:, sl], cos, sa, sb).astype(BF16)
        vo_ref[...] = v_ref[...].astype(BF16)

    tab = pl.BlockSpec((TM, LANES), lambda i: (i, 0))
    out = jax.ShapeDtypeStruct((n, w), BF16)
    o_spec = pl.BlockSpec((TM, w), lambda i: (i, 0))
    return pl.pallas_call(
        body, name=name, out_shape=(out, out, out), grid=(n // TM,),
        in_specs=[pl.BlockSpec((TM, w), lambda i: (i, 0)), pl.BlockSpec((TM, w), lambda i: (i, 1)),
                  pl.BlockSpec((TM, w), lambda i: (i, 2)), tab, tab, tab],
        out_specs=(o_spec, o_spec, o_spec), compiler_params=_params("parallel"),
    )(qkvu, qkvu, qkvu, *tables)


def qkv_bwd(dq, dk, dv, du, tables, name):
    n = dq.shape[0]
    w = NA_WIDTH

    def body(dq_ref, dk_ref, dv_ref, du_ref, c_ref, sa_ref, sb_ref, o_ref):
        cos, sa, sb = c_ref[...], -sa_ref[...], -sb_ref[...]
        for b in range(N_HEAD_BLOCKS):
            sl = slice(b * LANES, (b + 1) * LANES)
            o_ref[:, b * LANES:(b + 1) * LANES] = _rope(dq_ref[:, sl], cos, sa, sb).astype(BF16)
            o_ref[:, w + b * LANES:w + (b + 1) * LANES] = _rope(dk_ref[:, sl], cos, sa, sb).astype(BF16)
        o_ref[:, 2 * w:3 * w] = dv_ref[...].astype(BF16)
        o_ref[:, 3 * w:] = du_ref[...].astype(BF16)

    tab = pl.BlockSpec((TM, LANES), lambda i: (i, 0))
    tile = pl.BlockSpec((TM, w), lambda i: (i, 0))
    return pl.pallas_call(
        body, name=name, out_shape=jax.ShapeDtypeStruct((n, 4 * w), BF16), grid=(n // TM,),
        in_specs=[tile, tile, tile, tile, tab, tab, tab],
        out_specs=pl.BlockSpec((TM, 4 * w), lambda i: (i, 0)), compiler_params=_params("parallel"),
    )(dq, dk, dv, du, *tables)


def _na_consts():
    j = np.arange(GRID_W)
    col_start = np.clip(j - NA_KW // 2, 0, GRID_W - NA_KW)
    valid = (j[None, :] >= col_start[:, None]) & (j[None, :] < col_start[:, None] + NA_KW)
    dc = np.clip(j[None, :] - j[:, None] + NA_KW - 1, 0, 2 * NA_KW - 2)
    onehot = np.zeros((LANES, GRID_W, GRID_W), np.float32)
    for d in range(2 * NA_KW - 1):
        onehot[d] = ((dc == d) & valid).astype(np.float32)
    negmask = np.where(valid, 0.0, NEG_INF).astype(np.float32)
    return onehot.reshape(LANES, GRID_W * GRID_W), np.tile(negmask, (1, NA_KH))


def bias_tables(rpb, name):
    onehot, negmask = _na_consts()
    nj = 2 * NA_KH - 1
    rows = NA_HEADS * nj
    a = jnp.pad(rpb.reshape(rows, 2 * NA_KW - 1), ((0, 0), (0, LANES - (2 * NA_KW - 1))))

    def body(a_ref, e_ref, o_ref):
        o_ref[...] = _dot(a_ref[...], e_ref[...], precision=HIGHEST)

    t = pl.pallas_call(
        body, name=name, out_shape=jax.ShapeDtypeStruct((rows, GRID_W * GRID_W), F32),
        in_specs=[pl.BlockSpec(memory_space=pltpu.VMEM)] * 2,
        out_specs=pl.BlockSpec(memory_space=pltpu.VMEM),
    )(a, jnp.asarray(onehot))
    t = t.reshape(NA_HEADS, nj, GRID_W, GRID_W)
    tb = jnp.stack([t[:, j0:j0 + NA_KH] for j0 in range(NA_KH)])
    tb = tb.transpose(0, 1, 3, 2, 4).reshape(NA_KH, NA_HEADS, GRID_W, NA_KH * GRID_W)
    return tb + jnp.asarray(negmask)[None, None]


def bias_tables_bwd(dtb, name):
    onehot, _ = _na_consts()
    nj = 2 * NA_KH - 1
    d5 = dtb.reshape(NA_KH, NA_HEADS, GRID_W, NA_KH, GRID_W).transpose(0, 3, 1, 2, 4)
    d2 = d5.reshape(NA_KH * NA_KH * NA_HEADS, GRID_W * GRID_W)

    def body(d_ref, e_ref, o_ref):
        r = _dot(d_ref[...], e_ref[...], NT, precision=HIGHEST)
        for j in range(nj):
            acc = jnp.zeros((NA_HEADS, LANES), F32)
            for j0 in range(NA_KH):
                kk = j - j0
                if 0 <= kk < NA_KH:
                    base = (j0 * NA_KH + kk) * NA_HEADS
                    acc = acc + r[base:base + NA_HEADS, :]
            o_ref[j] = acc

    out = pl.pallas_call(
        body, name=name, out_shape=jax.ShapeDtypeStruct((nj, NA_HEADS, LANES), F32),
        in_specs=[pl.BlockSpec(memory_space=pltpu.VMEM)] * 2,
        out_specs=pl.BlockSpec(memory_space=pltpu.VMEM),
        compiler_params=pltpu.CompilerParams(vmem_limit_bytes=VMEM_LIMIT),
    )(d2, jnp.asarray(onehot))
    return out[:, :, :2 * NA_KW - 1].transpose(1, 0, 2)


def _head_masks():
    lane = lax.broadcasted_iota(jnp.int32, (1, LANES), 1)
    return [(lane >= h * HEAD_DIM) & (lane < (h + 1) * HEAD_DIM) for h in range(HEADS_PER_BLOCK)]


def _row_window(r, rows):
    rs = jnp.clip(r - NA_KH // 2, 0, rows - NA_KH)
    return rs - r + NA_KH - 1, pl.multiple_of(rs * GRID_W, GRID_W)


NA_ROWS_PER_STEP = 2
SLAB = NA_KH * GRID_W


def na_fwd(q, k, v, kc, vc, tb, name):
    s = q.shape[0]
    l = kc.shape[0]
    rows = s // GRID_W
    rr = NA_ROWS_PER_STEP

    def body(q_ref, k_ref, v_ref, kc_ref, vc_ref, tb_ref, o_ref, lse_ref):
        rb = pl.program_id(1)
        masks = _head_masks()
        kcb, vcb = kc_ref[...], vc_ref[...]
        for t in range(rr):
            j0, off = _row_window(rb * rr + t, rows)
            kslab = k_ref[pl.ds(off, SLAB), :]
            vslab = v_ref[pl.ds(off, SLAB), :]
            qt = q_ref[t * GRID_W:(t + 1) * GRID_W, :]
            o_acc = jnp.zeros((GRID_W, LANES), F32)
            lse_acc = jnp.zeros((GRID_W, LANES), F32)
            for h in range(HEADS_PER_BLOCK):
                qh = jnp.where(masks[h], qt, jnp.zeros_like(qt))
                s_loc = _dot(qh, kslab, NT) * ATT_SCALE + tb_ref[j0, h]
                s_ctx = _dot(qh, kcb, NT) * ATT_SCALE
                m = jnp.maximum(jnp.max(s_loc, axis=-1, keepdims=True), jnp.max(s_ctx, axis=-1, keepdims=True))
                p_loc = jnp.exp(s_loc - m)
                p_ctx = jnp.exp(s_ctx - m)
                den = jnp.sum(p_loc, axis=-1, keepdims=True) + jnp.sum(p_ctx, axis=-1, keepdims=True)
                o = (_dot(p_loc.astype(BF16), vslab) + _dot(p_ctx.astype(BF16), vcb)) / den
                o_acc = jnp.where(masks[h], o, o_acc)
                lse_acc = jnp.where(masks[h], m + jnp.log(den), lse_acc)
            o_ref[t * GRID_W:(t + 1) * GRID_W, :] = o_acc.astype(BF16)
            lse_ref[0, t * GRID_W:(t + 1) * GRID_W, :] = lse_acc

    tile = pl.BlockSpec((rr * GRID_W, LANES), lambda hb, rb: (rb, hb))
    whole = pl.BlockSpec((s, LANES), lambda hb, rb: (0, hb))
    ctx = pl.BlockSpec((l, LANES), lambda hb, rb: (0, hb))
    return pl.pallas_call(
        body, name=name,
        out_shape=(jax.ShapeDtypeStruct((s, NA_WIDTH), BF16), jax.ShapeDtypeStruct((N_HEAD_BLOCKS, s, LANES), F32)),
        grid=(N_HEAD_BLOCKS, rows // rr),
        in_specs=[tile, whole, whole, ctx, ctx,
                  pl.BlockSpec((NA_KH, HEADS_PER_BLOCK, GRID_W, SLAB), lambda hb, rb: (0, hb, 0, 0))],
        out_specs=(tile, pl.BlockSpec((1, rr * GRID_W, LANES), lambda hb, rb: (hb, rb, 0))),
        compiler_params=_params("parallel", "arbitrary"),
    )(q, k, v, kc, vc, tb)


def na_bwd(q, k, v, kc, vc, tb, o, dmix, lse, name):
    s = q.shape[0]
    l = kc.shape[0]
    rows = s // GRID_W
    rr = NA_ROWS_PER_STEP

    def body(q_ref, k_ref, v_ref, kc_ref, vc_ref, tb_ref, o_ref, do_ref, lse_ref,
             dq_ref, dk_ref, dv_ref, dkc_ref, dvc_ref, dtb_ref):
        rb = pl.program_id(1)

        @pl.when(rb == 0)
        def _():
            dk_ref[...] = jnp.zeros_like(dk_ref)
            dv_ref[...] = jnp.zeros_like(dv_ref)
            dkc_ref[...] = jnp.zeros_like(dkc_ref)
            dvc_ref[...] = jnp.zeros_like(dvc_ref)
            dtb_ref[...] = jnp.zeros_like(dtb_ref)

        masks = _head_masks()
        kcb, vcb = kc_ref[...], vc_ref[...]
        for t in range(rr):
            j0, off = _row_window(rb * rr + t, rows)
            sl = slice(t * GRID_W, (t + 1) * GRID_W)
            kslab = k_ref[pl.ds(off, SLAB), :]
            vslab = v_ref[pl.ds(off, SLAB), :]
            qt = q_ref[sl, :]
            do_f = do_ref[sl, :]
            dd = do_f * o_ref[sl, :].astype(F32)
            do_b = do_f.astype(BF16)
            lse_t = lse_ref[0, sl, :]
            dq_acc = jnp.zeros((GRID_W, LANES), F32)
            dk_acc = jnp.zeros((SLAB, LANES), F32)
            dv_acc = jnp.zeros((SLAB, LANES), F32)
            dkc_acc = jnp.zeros((l, LANES), F32)
            dvc_acc = jnp.zeros((l, LANES), F32)
            for h in range(HEADS_PER_BLOCK):
                qh = jnp.where(masks[h], qt, jnp.zeros_like(qt))
                doh = jnp.where(masks[h], do_b, jnp.zeros_like(do_b))
                delta = jnp.sum(jnp.where(masks[h], dd, 0.0), axis=-1, keepdims=True)
                lse_h = lse_t[:, h * HEAD_DIM:h * HEAD_DIM + 1]
                p_loc = jnp.exp(_dot(qh, kslab, NT) * ATT_SCALE + tb_ref[j0, h] - lse_h)
                p_ctx = jnp.exp(_dot(qh, kcb, NT) * ATT_SCALE - lse_h)
                ds_loc = p_loc * (_dot(doh, vslab, NT) - delta)
                ds_ctx = p_ctx * (_dot(doh, vcb, NT) - delta)
                dtb_ref[j0, h] += ds_loc
                dsl_b = ds_loc.astype(BF16)
                dsc_b = ds_ctx.astype(BF16)
                dq_h = (_dot(dsl_b, kslab) + _dot(dsc_b, kcb)) * ATT_SCALE
                dq_acc = jnp.where(masks[h], dq_h, dq_acc)
                dk_acc = dk_acc + _dot(dsl_b, qh, TN)
                dv_acc = dv_acc + _dot(p_loc.astype(BF16), doh, TN)
                dkc_acc = dkc_acc + _dot(dsc_b, qh, TN)
                dvc_acc = dvc_acc + _dot(p_ctx.astype(BF16), doh, TN)
            dq_ref[sl, :] = dq_acc
            dk_ref[pl.ds(off, SLAB), :] += dk_acc * ATT_SCALE
            dv_ref[pl.ds(off, SLAB), :] += dv_acc
            dkc_ref[...] += dkc_acc * ATT_SCALE
            dvc_ref[...] += dvc_acc

    tile = pl.BlockSpec((rr * GRID_W, LANES), lambda hb, rb: (rb, hb))
    whole = pl.BlockSpec((s, LANES), lambda hb, rb: (0, hb))
    ctx = pl.BlockSpec((l, LANES), lambda hb, rb: (0, hb))
    tbs = pl.BlockSpec((NA_KH, HEADS_PER_BLOCK, GRID_W, SLAB), lambda hb, rb: (0, hb, 0, 0))
    return pl.pallas_call(
        body, name=name,
        out_shape=(jax.ShapeDtypeStruct((s, NA_WIDTH), F32), jax.ShapeDtypeStruct((s, NA_WIDTH), F32),
                   jax.ShapeDtypeStruct((s, NA_WIDTH), F32), jax.ShapeDtypeStruct((l, NA_WIDTH), F32),
                   jax.ShapeDtypeStruct((l, NA_WIDTH), F32),
                   jax.ShapeDtypeStruct((NA_KH, NA_HEADS, GRID_W, SLAB), F32)),
        grid=(N_HEAD_BLOCKS, rows // rr),
        in_specs=[tile, whole, whole, ctx, ctx, tbs, tile, tile,
                  pl.BlockSpec((1, rr * GRID_W, LANES), lambda hb, rb: (hb, rb, 0))],
        out_specs=(tile, whole, whole, ctx, ctx, tbs),
        compiler_params=_params("parallel", "arbitrary"),
    )(q, k, v, kc, vc, tb, o, dmix, lse)


def ctx_attn_fwd(q, k, v, name):
    l = q.shape[0]

    def body(q_ref, k_ref, v_ref, o_ref, lse_ref):
        masks = _head_masks()
        qt, kb, vb = q_ref[...], k_ref[...], v_ref[...]
        o_acc = jnp.zeros((l, LANES), F32)
        lse_acc = jnp.zeros((l, LANES), F32)
        for h in range(HEADS_PER_BLOCK):
            qh = jnp.where(masks[h], qt, jnp.zeros_like(qt))
            sc = _dot(qh, kb, NT) * ATT_SCALE
            m = jnp.max(sc, axis=-1, keepdims=True)
            p = jnp.exp(sc - m)
            den = jnp.sum(p, axis=-1, keepdims=True)
            o_acc = jnp.where(masks[h], _dot(p.astype(BF16), vb) / den, o_acc)
            lse_acc = jnp.where(masks[h], m + jnp.log(den), lse_acc)
        o_ref[...] = o_acc.astype(BF16)
        lse_ref[0] = lse_acc

    blk = pl.BlockSpec((l, LANES), lambda hb: (0, hb))
    return pl.pallas_call(
        body, name=name,
        out_shape=(jax.ShapeDtypeStruct((l, NA_WIDTH), BF16), jax.ShapeDtypeStruct((N_HEAD_BLOCKS, l, LANES), F32)),
        grid=(N_HEAD_BLOCKS,), in_specs=[blk, blk, blk],
        out_specs=(blk, pl.BlockSpec((1, l, LANES), lambda hb: (hb, 0, 0))),
        compiler_params=_params("parallel"),
    )(q, k, v)


def ctx_attn_bwd(q, k, v, o, do, lse, dk_in, dv_in, name):
    l = q.shape[0]

    def body(q_ref, k_ref, v_ref, o_ref, do_ref, lse_ref, dki_ref, dvi_ref, dq_ref, dk_ref, dv_ref):
        masks = _head_masks()
        qt, kb, vb = q_ref[...], k_ref[...], v_ref[...]
        do_f = do_ref[...]
        dd = do_f * o_ref[...].astype(F32)
        do_b = do_f.astype(BF16)
        lse_t = lse_ref[0]
        dq_acc = jnp.zeros((l, LANES), F32)
        dk_acc = jnp.zeros((l, LANES), F32)
        dv_acc = jnp.zeros((l, LANES), F32)
        for h in range(HEADS_PER_BLOCK):
            qh = jnp.where(masks[h], qt, jnp.zeros_like(qt))
            doh = jnp.where(masks[h], do_b, jnp.zeros_like(do_b))
            delta = jnp.sum(jnp.where(masks[h], dd, 0.0), axis=-1, keepdims=True)
            p = jnp.exp(_dot(qh, kb, NT) * ATT_SCALE - lse_t[:, h * HEAD_DIM:h * HEAD_DIM + 1])
            ds = (p * (_dot(doh, vb, NT) - delta)).astype(BF16)
            dq_acc = jnp.where(masks[h], _dot(ds, kb) * ATT_SCALE, dq_acc)
            dk_acc = dk_acc + _dot(ds, qh, TN)
            dv_acc = dv_acc + _dot(p.astype(BF16), doh, TN)
        dq_ref[...] = dq_acc
        dk_ref[...] = dki_ref[...] + dk_acc * ATT_SCALE
        dv_ref[...] = dvi_ref[...] + dv_acc

    blk = pl.BlockSpec((l, LANES), lambda hb: (0, hb))
    out = jax.ShapeDtypeStruct((l, NA_WIDTH), F32)
    return pl.pallas_call(
        body, name=name, out_shape=(out, out, out), grid=(N_HEAD_BLOCKS,),
        in_specs=[blk, blk, blk, blk, blk, pl.BlockSpec((1, l, LANES), lambda hb: (hb, 0, 0)), blk, blk],
        out_specs=(blk, blk, blk), compiler_params=_params("parallel"),
    )(q, k, v, o, do, lse, dk_in, dv_in)


def _pool_consts(l):
    assert l == TM
    mem = np.zeros((2, POOL_GROUPS, TM, TM), np.float32)
    inv = np.zeros((2, POOL_GROUPS, TM, LANES), np.float32)
    for which, length in ((0, GRID_W), (1, l)):
        t = np.arange(length)
        for g, w in enumerate(POOL_WINDOWS):
            lo = np.clip(t - w // 2, 0, length)
            hi = np.clip(t - w // 2 + w, 0, length)
            blockm = ((t[None, :] >= lo[:, None]) & (t[None, :] < hi[:, None])).astype(np.float32)
            cnt = (hi - lo).astype(np.float32)
            for b in range(TM // length):
                mem[which, g, b * length:(b + 1) * length, b * length:(b + 1) * length] = blockm
                inv[which, g, b * length:(b + 1) * length, :] = (1.0 / cnt)[:, None]
    return mem, np.ascontiguousarray(mem.transpose(0, 1, 3, 2)), inv


def _split_dot(m01, val):
    hi = val.astype(BF16)
    lo = (val - hi.astype(F32)).astype(BF16)
    return _dot(m01, hi) + _dot(m01, lo)


def pool_fwd(qkvu, mem, inv, wp, scale, nx_tiles, name):
    n = qkvu.shape[0]

    def body(u_ref, m_ref, i_ref, wp_ref, s_ref, o_ref):
        for g in range(POOL_GROUPS):
            sl = slice(g * POOL_CH, (g + 1) * POOL_CH)
            ug = u_ref[:, sl]
            dg = _split_dot(m_ref[0, g], ug) * i_ref[0, g] - ug
            o_ref[:, sl] = (_dot(dg.astype(BF16), wp_ref[g]) * s_ref[:, sl]).astype(BF16)

    grp = lambda i: (i // nx_tiles, 0, 0, 0)
    return pl.pallas_call(
        body, name=name, out_shape=jax.ShapeDtypeStruct((n, POOL_WIDTH), BF16), grid=(n // TM,),
        in_specs=[pl.BlockSpec((TM, POOL_WIDTH), lambda i: (i, 3)),
                  pl.BlockSpec((1, POOL_GROUPS, TM, TM), grp),
                  pl.BlockSpec((1, POOL_GROUPS, TM, LANES), grp),
                  pl.BlockSpec((POOL_GROUPS, POOL_CH, POOL_CH), lambda i: (0, 0, 0)),
                  pl.BlockSpec((1, POOL_WIDTH), lambda i: (0, 0))],
        out_specs=pl.BlockSpec((TM, POOL_WIDTH), lambda i: (i, 0)),
        compiler_params=_params("parallel"),
    )(qkvu, mem, inv, wp, scale)


def pool_bwd(dmix, qkvu, mem, mem_t, inv, wp, scale, nx_tiles, name):
    n = qkvu.shape[0]

    def body(dy_ref, u_ref, m_ref, mt_ref, i_ref, wp_ref, s_ref, du_ref, dwp_ref, dsc_ref):
        @pl.when(pl.program_id(0) == 0)
        def _():
            dwp_ref[...] = jnp.zeros_like(dwp_ref)
            dsc_ref[...] = jnp.zeros_like(dsc_ref)

        for g in range(POOL_GROUPS):
            sl = slice(g * POOL_CH, (g + 1) * POOL_CH)
            ug = u_ref[:, sl]
            dy = dy_ref[:, sl]
            dg = (_split_dot(m_ref[0, g], ug) * i_ref[0, g] - ug).astype(BF16)
            z = _dot(dg, wp_ref[g])
            dsc_ref[0:1, sl] += jnp.sum(dy * z, axis=0, keepdims=True)
            dz = (dy * s_ref[:, sl]).astype(BF16)
            dwp_ref[g] += _dot(dg, dz, TN)
            dd = _dot(dz, wp_ref[g], NT)
            du_ref[:, sl] = _split_dot(mt_ref[0, g], dd * i_ref[0, g]) - dd

    grp = lambda i: (i // nx_tiles, 0, 0, 0)
    return pl.pallas_call(
        body, name=name,
        out_shape=(jax.ShapeDtypeStruct((n, POOL_WIDTH), F32),
                   jax.ShapeDtypeStruct((POOL_GROUPS, POOL_CH, POOL_CH), F32),
                   jax.ShapeDtypeStruct((8, POOL_WIDTH), F32)),
        grid=(n // TM,),
        in_specs=[pl.BlockSpec((TM, POOL_WIDTH), lambda i: (i, 1)),
                  pl.BlockSpec((TM, POOL_WIDTH), lambda i: (i, 3)),
                  pl.BlockSpec((1, POOL_GROUPS, TM, TM), grp),
                  pl.BlockSpec((1, POOL_GROUPS, TM, TM), grp),
                  pl.BlockSpec((1, POOL_GROUPS, TM, LANES), grp),
                  pl.BlockSpec((POOL_GROUPS, POOL_CH, POOL_CH), lambda i: (0, 0, 0)),
                  pl.BlockSpec((1, POOL_WIDTH), lambda i: (0, 0))],
        out_specs=(pl.BlockSpec((TM, POOL_WIDTH), lambda i: (i, 0)),
                   pl.BlockSpec((POOL_GROUPS, POOL_CH, POOL_CH), lambda i: (0, 0, 0)),
                   pl.BlockSpec((8, POOL_WIDTH), lambda i: (0, 0))),
        compiler_params=_params("arbitrary"),
    )(dmix, qkvu, mem, mem_t, inv, wp, scale)


MOD_ROWS = 16


def mod_fwd(cvecs, w, b, name):
    _, d = cvecs.shape
    cl = w.shape[2]
    tc = _pick(cl, (384, 128))

    def body(c_ref, w_ref, b_ref, o_ref):
        a = _silu(c_ref[...]).astype(BF16)
        o_ref[0] = _dot(a, w_ref[0].astype(BF16)) + b_ref[0]

    return pl.pallas_call(
        body, name=name, out_shape=jax.ShapeDtypeStruct((DEPTH, MOD_ROWS, cl), F32),
        grid=(DEPTH, cl // tc),
        in_specs=[pl.BlockSpec((MOD_ROWS, d), lambda li, j: (0, 0)),
                  pl.BlockSpec((1, d, tc), lambda li, j: (li, 0, j)),
                  pl.BlockSpec((1, 1, tc), lambda li, j: (li, 0, j))],
        out_specs=pl.BlockSpec((1, MOD_ROWS, tc), lambda li, j: (li, 0, j)),
        compiler_params=_params("parallel", "parallel"),
    )(cvecs, w, b)


def mod_bwd(cvecs, dm, w, name):
    _, d = cvecs.shape
    cl = w.shape[2]
    tc = _pick(cl, (384, 128))

    def body(c_ref, dm_ref, w_ref, dw_ref, da_ref):
        @pl.when((pl.program_id(0) == 0) & (pl.program_id(1) == 0))
        def _():
            da_ref[...] = jnp.zeros_like(da_ref)

        a = _silu(c_ref[...]).astype(BF16)
        dmb = dm_ref[0].astype(BF16)
        dw_ref[0] = _dot(a, dmb, TN)
        da_ref[...] += _dot(dmb, w_ref[0].astype(BF16), NT)

    return pl.pallas_call(
        body, name=name,
        out_shape=(jax.ShapeDtypeStruct((DEPTH, d, cl), F32), jax.ShapeDtypeStruct((MOD_ROWS, d), F32)),
        grid=(DEPTH, cl // tc),
        in_specs=[pl.BlockSpec((MOD_ROWS, d), lambda li, j: (0, 0)),
                  pl.BlockSpec((1, MOD_ROWS, tc), lambda li, j: (li, 0, j)),
                  pl.BlockSpec((1, d, tc), lambda li, j: (li, 0, j))],
        out_specs=(pl.BlockSpec((1, d, tc), lambda li, j: (li, 0, j)),
                   pl.BlockSpec((MOD_ROWS, d), lambda li, j: (0, 0))),
        compiler_params=_params("arbitrary", "arbitrary"),
    )(cvecs, dm, w)


def loss_head(y, target, name):
    s, d = y.shape
    nt = s // TM

    def body(y_ref, t_ref, l_ref, dy_ref, acc_ref):
        i = pl.program_id(0)

        @pl.when(i == 0)
        def _():
            acc_ref[...] = jnp.zeros_like(acc_ref)

        e = y_ref[...] - t_ref[...]
        dy_ref[...] = e * (1.0 / d)
        acc_ref[...] += jnp.sum(e * e, axis=0, keepdims=True)

        @pl.when(i == nt - 1)
        def _():
            l_ref[...] = jnp.sum(acc_ref[...], axis=1, keepdims=True) * (0.5 / d)

    tile = pl.BlockSpec((TM, d), lambda i: (i, 0))
    return pl.pallas_call(
        body, name=name,
        out_shape=(jax.ShapeDtypeStruct((1, 1), F32), jax.ShapeDtypeStruct((s, d), F32)),
        grid=(nt,), in_specs=[tile, tile],
        out_specs=(pl.BlockSpec((1, 1), lambda i: (0, 0)), tile),
        scratch_shapes=[pltpu.VMEM((1, d), F32)],
        compiler_params=_params("arbitrary"),
    )(y, target)


def sum_devices(v, name):
    _, r, c = v.shape
    tr = _pick(r, (64, 8))

    def body(v_ref, o_ref):
        acc = v_ref[0]
        for p in range(1, N_DEV):
            acc = acc + v_ref[p]
        o_ref[...] = acc

    return pl.pallas_call(
        body, name=name, out_shape=jax.ShapeDtypeStruct((r, c), F32), grid=(r // tr,),
        in_specs=[pl.BlockSpec((N_DEV, tr, c), lambda i: (0, i, 0))],
        out_specs=pl.BlockSpec((tr, c), lambda i: (i, 0)),
        compiler_params=_params("parallel"),
    )(v)


def cctx_grad(parts, c_ctx, name):
    d = c_ctx.shape[1]

    def body(p_ref, c_ref, o_ref):
        acc = p_ref[0]
        for p in range(1, N_DEV):
            acc = acc + p_ref[p]
        o_ref[...] = acc[8:9, :] * _dsilu(c_ref[...])

    return pl.pallas_call(
        body, name=name, out_shape=jax.ShapeDtypeStruct((1, d), F32),
        in_specs=[pl.BlockSpec(memory_space=pltpu.VMEM)] * 2,
        out_specs=pl.BlockSpec(memory_space=pltpu.VMEM),
    )(parts, c_ctx)


def _adam_math(w, g, m, v):
    m2 = ADAM_B1 * m + (1.0 - ADAM_B1) * g
    v2 = ADAM_B2 * v + (1.0 - ADAM_B2) * (g * g)
    m_hat = m2 / (1.0 - ADAM_B1 ** ADAM_STEP)
    v_hat = v2 / (1.0 - ADAM_B2 ** ADAM_STEP)
    delta = -ADAM_LR * (m_hat / (jnp.sqrt(v_hat) + ADAM_EPS) + ADAM_WD * w)
    return delta, m2, v2


def adamw(w, g, m, v, name):
    r, c = w.shape
    tr = _pick(r, (256, 128, 64, 32, 16, 8, r))

    def body(w_ref, g_ref, m_ref, v_ref, d_ref, m2_ref, v2_ref):
        d_ref[...], m2_ref[...], v2_ref[...] = _adam_math(w_ref[...], g_ref[...], m_ref[...], v_ref[...])

    tile = pl.BlockSpec((tr, c), lambda i: (i, 0))
    out = jax.ShapeDtypeStruct((r, c), F32)
    return pl.pallas_call(
        body, name=name, out_shape=(out, out, out), grid=(r // tr,),
        in_specs=[tile] * 4, out_specs=(tile, tile, tile), compiler_params=_params("parallel"),
    )(w, g, m, v)


def reduce_adamw(own, recv, w, m, v, name):
    r, c = own.shape
    tr = _pick(r, (192, 128, 64, 8))

    def body(own_ref, recv_ref, w_ref, m_ref, v_ref, g_ref, d_ref, m2_ref, v2_ref):
        _, me = _peer(0)
        acc = jnp.zeros((tr, c), F32)
        for p in range(N_DEV):
            acc = acc + jnp.where(me == p, own_ref[...], recv_ref[p].astype(F32))
        g_ref[...] = acc
        d_ref[...], m2_ref[...], v2_ref[...] = _adam_math(w_ref[...], acc, m_ref[...], v_ref[...])

    tile = pl.BlockSpec((tr, c), lambda i: (i, 0))
    out = jax.ShapeDtypeStruct((r, c), F32)
    return pl.pallas_call(
        body, name=name, out_shape=(out, out, out, out), grid=(r // tr,),
        in_specs=[tile, pl.BlockSpec((N_DEV, tr, c), lambda i: (0, i, 0)), tile, tile, tile],
        out_specs=(tile, tile, tile, tile), compiler_params=_params("parallel"),
    )(own, recv, w, m, v)


def _pack_shards(gu, dn, wi, wo, d):
    return jnp.concatenate([gu.reshape(-1, d), dn.reshape(-1, d), wi.reshape(-1, d), wo.reshape(-1, d)], axis=0)


def _pack_full(gu, dn, wi, wo, d):
    f2 = gu.shape[-1]
    f = dn.shape[2]
    a = gu.reshape(DEPTH, 2, d, N_DEV, f2 // N_DEV).transpose(3, 0, 1, 2, 4).reshape(N_DEV, -1, d)
    b = dn.reshape(DEPTH, 2, N_DEV, f // N_DEV, d).transpose(2, 0, 1, 3, 4).reshape(N_DEV, -1, d)
    e = wi.reshape(DEPTH, d, N_DEV, wi.shape[-1] // N_DEV).transpose(2, 0, 1, 3).reshape(N_DEV, -1, d)
    o = wo.reshape(DEPTH, N_DEV, wo.shape[1] // N_DEV, d).transpose(1, 0, 2, 3).reshape(N_DEV, -1, d)
    return jnp.concatenate([a, b, e, o], axis=1)


def _unpack_gathered(pk, shapes, d):
    (gu_s, dn_s, wi_s, wo_s) = shapes
    r0 = math.prod(gu_s) // d
    r1 = r0 + math.prod(dn_s) // d
    r2 = r1 + math.prod(wi_s) // d
    gu = pk[:, :r0].reshape((N_DEV,) + gu_s).transpose(1, 2, 3, 0, 4).reshape(gu_s[:3] + (N_DEV * gu_s[3],))
    dn = pk[:, r0:r1].reshape((N_DEV,) + dn_s).transpose(1, 2, 0, 3, 4).reshape(dn_s[:2] + (N_DEV * dn_s[2], d))
    wi = pk[:, r1:r2].reshape((N_DEV,) + wi_s).transpose(1, 2, 0, 3).reshape(wi_s[:2] + (N_DEV * wi_s[2],))
    wo = pk[:, r2:].reshape((N_DEV,) + wo_s).transpose(1, 0, 2, 3).reshape((wo_s[0], N_DEV * wo_s[1], d))
    return gu, dn, wi, wo


def _unpack_shards(pk, shapes, d):
    (gu_s, dn_s, wi_s, wo_s) = shapes
    r0 = math.prod(gu_s) // d
    r1 = r0 + math.prod(dn_s) // d
    r2 = r1 + math.prod(wi_s) // d
    return pk[:r0].reshape(gu_s), pk[r0:r1].reshape(dn_s), pk[r1:r2].reshape(wi_s), pk[r2:].reshape(wo_s)


def kernel(x, c, ctx, c_ctx, w_mod, b_mod, norm_g, w_ffn_gate_up, w_ffn_down, w_in, w_out, na_rpb, w_pool, pool_scale, loss_target, m_c_ctx, m_w_mod, m_b_mod, m_norm_g, m_w_ffn_gate_up, m_w_ffn_down, m_w_in, m_w_out, m_na_rpb, m_w_pool, m_pool_scale, v_c_ctx, v_w_mod, v_b_mod, v_norm_g, v_w_ffn_gate_up, v_w_ffn_down, v_w_in, v_w_out, v_na_rpb, v_w_pool, v_pool_scale):
    s, d = x.shape[1], x.shape[2]
    l = ctx.shape[1]
    n = s + l
    nx = s // TM
    f = w_ffn_down.shape[2] * N_DEV
    cl = w_mod.shape[2]
    dl = norm_g.shape[2]
    me = 4 * lax.axis_index("x") + 2 * lax.axis_index("y") + lax.axis_index("c")

    c_all = all_gather(c, "gather_c").reshape(N_DEV, d)
    cvecs = jnp.concatenate([c_all, c_ctx[None, :], jnp.zeros((MOD_ROWS - N_DEV - 1, d), F32)], axis=0)
    b_loc = lax.dynamic_slice(b_mod, (0, me * cl), (DEPTH, cl)).reshape(DEPTH, 1, cl)
    mod_loc = mod_fwd(cvecs, w_mod, b_loc, "mod_fwd")
    mod_all = all_gather(mod_loc.reshape(DEPTH * MOD_ROWS, cl), "gather_mod")
    mod_all = mod_all.reshape(N_DEV, DEPTH, MOD_ROWS, cl).transpose(1, 2, 0, 3).reshape(DEPTH, MOD_ROWS, N_DEV * cl)
    mine = lax.dynamic_slice(mod_all, (0, me, 0), (DEPTH, 1, N_DEV * cl))
    mods = jnp.concatenate([mine, mod_all[:, N_DEV:N_DEV + 1]], axis=1).reshape(DEPTH, 2, N_MOD, d)

    shard_shapes = (w_ffn_gate_up.shape, w_ffn_down.shape, w_in.shape, w_out.shape)
    w_pack = _pack_shards(w_ffn_gate_up, w_ffn_down, w_in, w_out, d)
    gathered = all_gather(w_pack.astype(BF16), "gather_weights")
    wgu, wdn, win, wout = _unpack_gathered(gathered, shard_shapes, d)
    wg, wu = wgu[..., :f], wgu[..., f:]
    wp_b = w_pool.astype(BF16)

    tables = _rope_tables(s, n)
    mem_np, mem_t_np, inv_np = _pool_consts(l)
    mem, mem_t, inv = jnp.asarray(mem_np, BF16), jnp.asarray(mem_t_np, BF16), jnp.asarray(inv_np)
    g_all = all_gather(norm_g.reshape(DEPTH * 6, dl), "gather_norm_g")
    g_full = g_all.reshape(N_DEV, DEPTH, 6, dl).transpose(1, 2, 0, 3).reshape(DEPTH, 6, 1, N_DEV * dl)

    def ffn_fwd(xin, li, i, tag):
        ks = 6 * i
        hb = norm_mod(xin, g_full[li, 4 * i], mods[li], ks, ks + 1, nx, tag + "_norm")
        gg, uu, aa = gate_up(hb, wg[li, i], wu[li, i], tag + "_gate_up")
        ff = matmul(aa, wdn[li, i], "nn", tag + "_down")
        xo = post_fwd(xin, ff, g_full[li, 4 * i + 1], mods[li], ks + 2, 0.5, nx, tag + "_post")
        return xo, (xin, hb, gg, uu, aa, ff)

    def ffn_bwd(dxo, saved, li, i, tag):
        xin, hb, gg, uu, aa, ff = saved
        ks = 6 * i
        dff, red1 = post_bwd(ff, dxo, g_full[li, 4 * i + 1], mods[li], ks + 2, 0.5, nx, tag + "_post_bwd")
        da = matmul(dff, wdn[li, i], "nt", tag + "_da")
        d_wdn = matmul(aa, dff, "tn", tag + "_dwdown")
        dgu = swiglu_bwd(gg, uu, da, tag + "_swiglu_bwd")
        dh = matmul(dgu, wgu[li, i], "nt", tag + "_dh")
        d_wgu = matmul(hb, dgu, "tn", tag + "_dwgu")
        dx, red2 = pre_bwd(xin, dh, dxo, g_full[li, 4 * i], mods[li], ks, ks + 1, nx, tag + "_pre_bwd")
        return dx, d_wgu, d_wdn, red1, red2

    def mix_fwd(xin, li, tag):
        last = li == DEPTH - 1
        hb = norm_mod(xin, g_full[li, 2], mods[li], 3, 4, nx, tag + "_norm")
        qkvu = matmul(hb, win[li], "nn", tag + "_in", tn=_pick(win.shape[-1], (2048, 1024, 512)))
        q, k, v = qkv_post(qkvu, tables, tag + "_rope")
        tb = bias_tables(na_rpb[li], tag + "_bias")
        qx, kx, vx, kc, vc = q[:s], k[:s], v[:s], k[s:], v[s:]
        na_x, lse = na_fwd(qx, kx, vx, kc, vc, tb, tag + "_na")
        py = pool_fwd(qkvu, mem, inv, wp_b[li], pool_scale[li][None, :], nx, tag + "_pool")
        if last:
            na_c = lse_c = None
            mix = jnp.concatenate([na_x, py[:s]], axis=1)
            xres = xin[:s]
        else:
            na_c, lse_c = ctx_attn_fwd(q[s:], kc, vc, tag + "_ctx_attn")
            mix = jnp.concatenate([jnp.concatenate([na_x, na_c], axis=0), py], axis=1)
            xres = xin
        fm = matmul(mix, wout[li], "nn", tag + "_out")
        xo = post_fwd(xres, fm, g_full[li, 3], mods[li], 5, 1.0, nx, tag + "_post")
        return xo, (xin, hb, qkvu, q, qx, kx, vx, kc, vc, tb, na_x, lse, na_c, lse_c, mix, fm)

    def mix_bwd(dxo, saved, li, tag):
        last = li == DEPTH - 1
        xin, hb, qkvu, q, qx, kx, vx, kc, vc, tb, na_x, lse, na_c, lse_c, mix, fm = saved
        dfm, red1 = post_bwd(fm, dxo, g_full[li, 3], mods[li], 5, 1.0, nx, tag + "_post_bwd")
        dmix = matmul(dfm, wout[li], "nt", tag + "_dmix")
        d_wout = matmul(mix, dfm, "tn", tag + "_dwout")
        if last:
            dmix = jnp.concatenate([dmix, jnp.zeros((l, dmix.shape[1]), F32)], axis=0)
            dxo = jnp.concatenate([dxo, jnp.zeros((l, d), F32)], axis=0)
        du, d_wp, d_ps = pool_bwd(dmix, qkvu, mem, mem_t, inv, wp_b[li], pool_scale[li][None, :], nx, tag + "_pool_bwd")
        dqx, dkx, dvx, dkc, dvc, dtb = na_bwd(qx, kx, vx, kc, vc, tb, na_x, dmix, lse, tag + "_na_bwd")
        if last:
            dqc = jnp.zeros((l, NA_WIDTH), F32)
        else:
            dqc, dkc, dvc = ctx_attn_bwd(q[s:], kc, vc, na_c, dmix[s:, :NA_WIDTH], lse_c, dkc, dvc, tag + "_ctx_attn_bwd")
        d_rpb = bias_tables_bwd(dtb, tag + "_bias_bwd")
        dqkvu = qkv_bwd(jnp.concatenate([dqx, dqc], axis=0), jnp.concatenate([dkx, dkc], axis=0),
                        jnp.concatenate([dvx, dvc], axis=0), du, tables, tag + "_rope_bwd")
        dh = matmul(dqkvu, win[li], "nt", tag + "_dh", tk=_pick(win.shape[-1], (2048, 1024, 512)))
        d_win = matmul(hb, dqkvu, "tn", tag + "_dwin")
        dx, red2 = pre_bwd(xin, dh, dxo, g_full[li, 2], mods[li], 3, 4, nx, tag + "_pre_bwd")
        return dx, d_win, d_wout, d_rpb, d_wp, d_ps[0], red1, red2

    xcur = jnp.concatenate([x[0], ctx[0]], axis=0)
    saved = []
    for li in range(DEPTH):
        xcur, sa = ffn_fwd(xcur, li, 0, f"l{li}_ffn0")
        xcur, sm = mix_fwd(xcur, li, f"l{li}_mix")
        xcur, sb = ffn_fwd(xcur, li, 1, f"l{li}_ffn1")
        saved.append((sa, sm, sb))

    loss_local, dcur = loss_head(xcur, loss_target[0], "loss")
    loss = lax.psum(loss_local[0, 0], ("x", "y", "c"))

    def grp2(red):
        if red.shape[0] == 2:
            return red
        return jnp.concatenate([red, jnp.zeros_like(red)], axis=0)

    d_wgu, d_wdn, d_win, d_wout = [], [], [], []
    d_rpb, d_wp, d_ps, d_mod, d_g = [], [], [], [], []
    for li in reversed(range(DEPTH)):
        sa, sm, sb = saved[li]
        dcur, gu1, dn1, rb1, rb2 = ffn_bwd(dcur, sb, li, 1, f"l{li}_ffn1")
        dcur, gwi, gwo, grpb, gwp, gps, rm1, rm2 = mix_bwd(dcur, sm, li, f"l{li}_mix")
        dcur, gu0, dn0, ra1, ra2 = ffn_bwd(dcur, sa, li, 0, f"l{li}_ffn0")
        ra1, ra2, rm1, rm2, rb1, rb2 = (grp2(t) for t in (ra1, ra2, rm1, rm2, rb1, rb2))
        d_wgu.insert(0, jnp.stack([gu0, gu1]))
        d_wdn.insert(0, jnp.stack([dn0, dn1]))
        d_win.insert(0, gwi)
        d_wout.insert(0, gwo)
        d_rpb.insert(0, grpb)
        d_wp.insert(0, gwp)
        d_ps.insert(0, gps)
        d_mod.insert(0, jnp.stack([ra2[:, 0], ra2[:, 1], ra1[:, 0], rm2[:, 0], rm2[:, 1], rm1[:, 0],
                                   rb2[:, 0], rb2[:, 1], rb1[:, 0]], axis=1))
        d_g.insert(0, jnp.stack([t[0] + t[1] for t in (ra2[:, 2], ra1[:, 1], rm2[:, 2], rm1[:, 1], rb2[:, 2], rb1[:, 1])]))
    grad_x = dcur[:s][None]

    small_parts = [jnp.stack(d_mod).reshape(-1, d), jnp.stack(d_g).reshape(-1, d),
                   jnp.stack(d_wp).reshape(-1, d), jnp.stack(d_ps).reshape(-1, d)]
    rpb_flat = jnp.stack(d_rpb).reshape(-1)
    rpb_rows = -(-rpb_flat.shape[0] // d)
    small_parts.append(jnp.pad(rpb_flat, (0, rpb_rows * d - rpb_flat.shape[0])).reshape(rpb_rows, d))
    offs = np.cumsum([0] + [p.shape[0] for p in small_parts])
    pad_rows = -(-offs[-1] // 64) * 64 - offs[-1]
    small = jnp.concatenate(small_parts + [jnp.zeros((pad_rows, d), F32)], axis=0)
    small_all = all_gather(small, "gather_small_grads")
    small_sum = sum_devices(small_all, "sum_small_grads")

    n_mod_rows = DEPTH * 2 * N_MOD
    dmod_all = small_all[:, :n_mod_rows].reshape(N_DEV, DEPTH, 2, N_MOD * d)
    dmod_sum = small_sum[:n_mod_rows].reshape(DEPTH, 2, N_MOD * d)
    dm_rows = jnp.concatenate([dmod_all[:, :, 0].transpose(1, 0, 2), dmod_sum[:, 1:2],
                               jnp.zeros((DEPTH, MOD_ROWS - N_DEV - 1, N_MOD * d), F32)], axis=1)
    grad_b_mod = dmod_sum[:, 0] + dmod_sum[:, 1]
    dm_loc = lax.dynamic_slice(dm_rows, (0, 0, me * cl), (DEPTH, MOD_ROWS, cl))
    grad_w_mod, da_part = mod_bwd(cvecs, dm_loc, w_mod, "mod_bwd")
    da_all = all_gather(da_part, "gather_dcvec")
    grad_c_ctx = cctx_grad(da_all, c_ctx[None, :], "c_ctx_grad")[0]

    grad_norm_full = small_sum[offs[1]:offs[2]].reshape(DEPTH, 6, d)
    grad_norm_g = lax.dynamic_slice(grad_norm_full, (0, 0, me * dl), (DEPTH, 6, dl))
    grad_w_pool = small_sum[offs[2]:offs[3]].reshape(w_pool.shape)
    grad_pool_scale = small_sum[offs[3]:offs[4]].reshape(pool_scale.shape)
    grad_na_rpb = small_sum[offs[4]:offs[5]].reshape(-1)[:rpb_flat.shape[0]].reshape(na_rpb.shape)

    g_pack = _pack_full(jnp.stack(d_wgu), jnp.stack(d_wdn), jnp.stack(d_win), jnp.stack(d_wout), d)
    own = lax.dynamic_slice(g_pack, (me, 0, 0), (1,) + g_pack.shape[1:])[0]
    recv = all_to_all(g_pack.astype(BF16), "exchange_grads")
    m_pack = _pack_shards(m_w_ffn_gate_up, m_w_ffn_down, m_w_in, m_w_out, d)
    v_pack = _pack_shards(v_w_ffn_gate_up, v_w_ffn_down, v_w_in, v_w_out, d)
    big = reduce_adamw(own, recv, w_pack, m_pack, v_pack, "reduce_adamw")
    (g_gu, g_dn, g_wi, g_wo), (dl_gu, dl_dn, dl_wi, dl_wo), (nm_gu, nm_dn, nm_wi, nm_wo), (nv_gu, nv_dn, nv_wi, nv_wo) = (
        _unpack_shards(t, shard_shapes, d) for t in big)

    def small_adam(w, g, m, v, name):
        shp = w.shape
        cols = shp[-1]
        outs = adamw(w.reshape(-1, cols), g.reshape(-1, cols), m.reshape(-1, cols), v.reshape(-1, cols), name)
        return tuple(t.reshape(shp) for t in outs)

    a_cc = small_adam(c_ctx, grad_c_ctx, m_c_ctx, v_c_ctx, "adam_c_ctx")
    a_wm = small_adam(w_mod, grad_w_mod, m_w_mod, v_w_mod, "adam_w_mod")
    a_bm = small_adam(b_mod, grad_b_mod, m_b_mod, v_b_mod, "adam_b_mod")
    a_ng = small_adam(norm_g, grad_norm_g, m_norm_g, v_norm_g, "adam_norm_g")
    a_rp = small_adam(na_rpb, grad_na_rpb, m_na_rpb, v_na_rpb, "adam_na_rpb")
    a_wp = small_adam(w_pool, grad_w_pool, m_w_pool, v_w_pool, "adam_w_pool")
    a_ps = small_adam(pool_scale, grad_pool_scale, m_pool_scale, v_pool_scale, "adam_pool_scale")

    grads = (grad_c_ctx, grad_w_mod, grad_b_mod, grad_norm_g, g_gu, g_dn, g_wi, g_wo, grad_na_rpb, grad_w_pool, grad_pool_scale)
    deltas = (a_cc[0], a_wm[0], a_bm[0], a_ng[0], dl_gu, dl_dn, dl_wi, dl_wo, a_rp[0], a_wp[0], a_ps[0])
    new_m = (a_cc[1], a_wm[1], a_bm[1], a_ng[1], nm_gu, nm_dn, nm_wi, nm_wo, a_rp[1], a_wp[1], a_ps[1])
    new_v = (a_cc[2], a_wm[2], a_bm[2], a_ng[2], nv_gu, nv_dn, nv_wi, nv_wo, a_rp[2], a_wp[2], a_ps[2])
    return (loss, grad_x, *grads, *deltas, *new_m, *new_v)
```

```python
import functools
import math

import numpy as np
import jax
import jax.numpy as jnp
from jax import lax
from jax.experimental import pallas as pl
from jax.experimental.pallas import tpu as pltpu

F32 = jnp.float32
BF16 = jnp.bfloat16

N_DEV = 8
DEPTH = 2
GRID_W = 64
N_MOD = 9
NA_HEADS = 8
HEAD_DIM = 64
NA_WIDTH = NA_HEADS * HEAD_DIM
NA_KH = 8
NA_KW = 16
POOL_GROUPS = 4
POOL_CH = 128
POOL_WIDTH = POOL_GROUPS * POOL_CH
POOL_WINDOWS = (2, 4, 8, 16)
ROPE_THETA = 10000.0
ROPE_PAIRS = HEAD_DIM // 4
RMS_EPS = 1e-6
NEG_INF = -1e30
ATT_SCALE = HEAD_DIM ** -0.5

ADAM_LR = 0.001
ADAM_B1 = 0.9
ADAM_B2 = 0.999
ADAM_EPS = 1e-08
ADAM_WD = 0.01
ADAM_STEP = 10

TM = 256
LANES = 128
HEADS_PER_BLOCK = LANES // HEAD_DIM
N_HEAD_BLOCKS = NA_WIDTH // LANES
IN_BLOCK = 2 * LANES
N_QKV_BLOCKS = 3 * NA_WIDTH // IN_BLOCK
VMEM_LIMIT = 48 * 1024 * 1024
HIGHEST = lax.Precision.HIGHEST
MESH = pl.DeviceIdType.MESH
ANY = pl.BlockSpec(memory_space=pl.ANY)

NN = (((1,), (0,)), ((), ()))
NT = (((1,), (1,)), ((), ()))
TN = (((0,), (0,)), ((), ()))


def _pick(n, cands):
    for t in cands:
        if n % t == 0:
            return t
    raise ValueError(f"no tile for {n} among {cands}")


def _dot(a, b, dn=NN, precision=None):
    return lax.dot_general(a, b, dn, preferred_element_type=F32, precision=precision)


def _silu(x):
    return x * jax.nn.sigmoid(x)


def _dsilu(x):
    s = jax.nn.sigmoid(x)
    return s * (1.0 + x * (1.0 - s))


def _peer(mask):
    x, y, c = lax.axis_index("x"), lax.axis_index("y"), lax.axis_index("c")
    px = 1 - x if mask & 4 else x
    py = 1 - y if mask & 2 else y
    pc = 1 - c if mask & 1 else c
    return (px, py, pc), 4 * px + 2 * py + pc


class Exchange:
    def __init__(self, gathers=(), a2as=()):
        self.gathers = list(gathers)
        self.a2as = list(a2as)
        self.n_jobs = len(self.gathers) + len(self.a2as)

    def inputs(self):
        out = list(self.gathers)
        for v, buf, _ in self.a2as:
            out += [v, buf]
        return out

    def out_shapes(self):
        shapes = [jax.ShapeDtypeStruct((N_DEV,) + v.shape, v.dtype) for v in self.gathers]
        shapes += [jax.ShapeDtypeStruct(buf.shape, buf.dtype) for _, buf, _ in self.a2as]
        return shapes

    def aliases(self, n_in, n_out):
        ng = len(self.gathers)
        return {n_in + ng + 2 * k + 1: n_out + ng + k for k in range(len(self.a2as))}

    def scratch(self):
        per = N_DEV - 1
        return [pltpu.SemaphoreType.DMA((per * self.n_jobs,)), pltpu.SemaphoreType.DMA((per * self.n_jobs,)),
                pltpu.SemaphoreType.DMA((self.n_jobs,))]

    def _copies(self, in_refs, out_refs, sems, with_recvs):
        send_sems, recv_sems, local_sems = sems
        _, me = _peer(0)
        ng = len(self.gathers)
        local, sends, recvs = [], [], []
        for job in range(self.n_jobs):
            if job < ng:
                src_of = lambda pid, r=in_refs[job]: r
                dst_of = lambda pid, r=out_refs[job]: r.at[pid]
            else:
                k = job - ng
                stage = self.a2as[k][2]
                src_of = lambda pid, r=in_refs[ng + 2 * k]: r.at[pid]
                dst_of = lambda pid, r=out_refs[job], st=stage: r.at[pid, st]
            local.append(pltpu.make_async_copy(src_of(me), dst_of(me), local_sems.at[job]))
            for mask in range(1, N_DEV):
                peer, pid = _peer(mask)
                idx = job * (N_DEV - 1) + mask - 1
                sends.append(pltpu.make_async_remote_copy(
                    src_ref=src_of(pid), dst_ref=dst_of(me), send_sem=send_sems.at[idx],
                    recv_sem=recv_sems.at[idx], device_id=peer, device_id_type=MESH))
                if with_recvs:
                    recvs.append(pltpu.make_async_remote_copy(
                        src_ref=src_of(pid), dst_ref=dst_of(pid), send_sem=send_sems.at[idx],
                        recv_sem=recv_sems.at[idx], device_id=peer, device_id_type=MESH))
        return local, sends, recvs

    def start(self, in_refs, out_refs, sems):
        local, sends, _ = self._copies(in_refs, out_refs, sems, False)
        for cp in local + sends:
            cp.start()

    def wait(self, in_refs, out_refs, sems):
        local, sends, recvs = self._copies(in_refs, out_refs, sems, True)
        for cp in recvs:
            cp.wait_recv()
        for cp in sends:
            cp.wait_send()
        for cp in local:
            cp.wait()


def carrier_call(core, *, name, grid, in_specs, out_specs, out_shape, inputs, scratch_shapes=(), aliases=None,
                 exchange=None):
    aliases = dict(aliases or {})
    n_in, n_out, n_sc = len(in_specs), len(out_specs), len(scratch_shapes)
    sem = ("arbitrary",) * len(grid)
    params = pltpu.CompilerParams(dimension_semantics=sem, vmem_limit_bytes=VMEM_LIMIT)
    if exchange is None or exchange.n_jobs == 0:
        outs = pl.pallas_call(core, name=name, grid=grid, in_specs=list(in_specs), out_specs=tuple(out_specs),
                              out_shape=tuple(out_shape), scratch_shapes=list(scratch_shapes),
                              input_output_aliases=aliases, compiler_params=params)(*inputs)
        return list(outs), []
    x_in = exchange.inputs()
    x_out = exchange.out_shapes()
    aliases.update(exchange.aliases(n_in, n_out))

    def body(*refs):
        a = n_in + len(x_in)
        b = a + n_out + len(x_out)
        core_in, job_in = refs[:n_in], refs[n_in:a]
        core_out, job_out = refs[a:a + n_out], refs[a + n_out:b]
        core_sc, job_sc = refs[b:b + n_sc], refs[b + n_sc:]
        first = functools.reduce(lambda p, q: p & q, [pl.program_id(ax) == 0 for ax in range(len(grid))])
        last = functools.reduce(lambda p, q: p & q, [pl.program_id(ax) == g - 1 for ax, g in enumerate(grid)])

        @pl.when(first)
        def _():
            exchange.start(job_in, job_out, job_sc)

        core(*core_in, *core_out, *core_sc)

        @pl.when(last)
        def _():
            exchange.wait(job_in, job_out, job_sc)

    outs = pl.pallas_call(
        body, name=name, grid=grid, in_specs=list(in_specs) + [ANY] * len(x_in),
        out_specs=tuple(out_specs) + (ANY,) * len(x_out), out_shape=tuple(out_shape) + tuple(x_out),
        scratch_shapes=list(scratch_shapes) + exchange.scratch(), input_output_aliases=aliases,
        compiler_params=params)(*inputs, *x_in)
    return list(outs[:n_out]), list(outs[n_out:])


def exchange_only(exchange, name):
    def body(*refs):
        n_in, n_out = len(exchange.inputs()), len(exchange.out_shapes())
        job_in, job_out, sems = refs[:n_in], refs[n_in:n_in + n_out], refs[n_in + n_out:]
        exchange.start(job_in, job_out, sems)
        exchange.wait(job_in, job_out, sems)

    x_in = exchange.inputs()
    outs = pl.pallas_call(
        body, name=name, in_specs=[ANY] * len(x_in), out_specs=(ANY,) * len(exchange.out_shapes()),
        out_shape=tuple(exchange.out_shapes()), scratch_shapes=exchange.scratch(),
        input_output_aliases=exchange.aliases(0, 0))(*x_in)
    return list(outs)


def all_gather(v, name):
    return exchange_only(Exchange(gathers=[v]), name)[0]


def _rms(xf):
    return lax.rsqrt(jnp.mean(xf * xf, axis=-1, keepdims=True) + RMS_EPS)


def _is_ctx(i, tm, s):
    return (i * tm + lax.broadcasted_iota(jnp.int32, (tm, 1), 0)) >= s


def _mod_rows(mod_ref, k, is_ctx):
    return jnp.where(is_ctx, mod_ref[1, k:k + 1, :], mod_ref[0, k:k + 1, :])


def _norm_mod(xf, g, mod_ref, k_shift, k_scale, is_ctx):
    nrm = xf * _rms(xf) * g
    return (nrm * (1.0 + _mod_rows(mod_ref, k_scale, is_ctx)) + _mod_rows(mod_ref, k_shift, is_ctx)).astype(BF16)


def _post(xf, ff, g, mod_ref, k_gate, coef, is_ctx):
    return xf + coef * _mod_rows(mod_ref, k_gate, is_ctx) * (ff * _rms(ff) * g)


def _red_add(red_ref, first, is_ctx, rows):
    @pl.when(first)
    def _():
        red_ref[...] = jnp.zeros_like(red_ref)

    for r, val in enumerate(rows):
        tot = jnp.sum(val, axis=0, keepdims=True)
        ctx = jnp.sum(jnp.where(is_ctx, val, 0.0), axis=0, keepdims=True)
        red_ref[0, r:r + 1, :] += tot - ctx
        red_ref[1, r:r + 1, :] += ctx


def _pre_bwd(xf, dh, dxo, g, mod_ref, k_scale, is_ctx):
    r = _rms(xf)
    xhat = xf * r
    dn = dh * (1.0 + _mod_rows(mod_ref, k_scale, is_ctx))
    dxhat = dn * g
    dx = dxo + r * (dxhat - xhat * jnp.mean(dxhat * xhat, axis=-1, keepdims=True))
    return dx, [dh, dh * (xhat * g), dn * xhat]


def _vec_spec(d):
    return pl.BlockSpec((1, d), lambda *_: (0, 0))


def _mod_whole(d):
    return pl.BlockSpec((2, N_MOD, d), lambda *_: (0, 0, 0))


def _red_whole(d):
    return pl.BlockSpec((2, 8, d), lambda *_: (0, 0, 0))


def _token_tile(n):
    return _pick(n, (768, 640, 512, 384, 256))


def ffn_up(x, g, mod, wgu, s, k0, name, exchange=None):
    n, d = x.shape
    fq = wgu.shape[-1]
    tm = _token_tile(n)

    def core(x_ref, g_ref, mod_ref, w_ref, hb_ref, gu_ref, a_ref, hb_s):
        i, k = pl.program_id(0), pl.program_id(1)

        @pl.when(k == 0)
        def _():
            hb_s[...] = _norm_mod(x_ref[...], g_ref[...], mod_ref, k0, k0 + 1, _is_ctx(i, tm, s))
            hb_ref[...] = hb_s[...]

        h = hb_s[...]
        gg = _dot(h, w_ref[0, 0])
        uu = _dot(h, w_ref[1, 0])
        gu_ref[0, 0] = gg.astype(BF16)
        gu_ref[1, 0] = uu.astype(BF16)
        a_ref[0] = (_silu(gg) * uu).astype(BF16)

    outs, xo = carrier_call(
        core, name=name, grid=(n // tm, 4),
        in_specs=[pl.BlockSpec((tm, d), lambda i, k: (i, 0)), _vec_spec(d), _mod_whole(d),
                  pl.BlockSpec((2, 1, d, fq), lambda i, k: (0, k, 0, 0))],
        out_specs=[pl.BlockSpec((tm, d), lambda i, k: (i, 0)),
                   pl.BlockSpec((2, 1, tm, fq), lambda i, k: (0, k, i, 0)),
                   pl.BlockSpec((1, tm, fq), lambda i, k: (k, i, 0))],
        out_shape=[jax.ShapeDtypeStruct((n, d), BF16), jax.ShapeDtypeStruct((2, 4, n, fq), BF16),
                   jax.ShapeDtypeStruct((4, n, fq), BF16)],
        scratch_shapes=[pltpu.VMEM((tm, d), BF16)], inputs=[x, g, mod, wgu], exchange=exchange)
    return outs, xo


def ffn_down(a4, wd4, x, g, mod, s, k_gate, name, exchange=None):
    n, d = x.shape
    fq = wd4.shape[1]
    tm = _token_tile(n)

    def core(a_ref, w_ref, x_ref, g_ref, mod_ref, f_ref, xo_ref, acc):
        i, k = pl.program_id(0), pl.program_id(1)

        @pl.when(k == 0)
        def _():
            acc[...] = jnp.zeros_like(acc)

        acc[...] += _dot(a_ref[0], w_ref[0])

        @pl.when(k == 3)
        def _():
            ff = acc[...]
            f_ref[...] = ff
            xo_ref[...] = _post(x_ref[...], ff, g_ref[...], mod_ref, k_gate, 0.5, _is_ctx(i, tm, s))

    tile = pl.BlockSpec((tm, d), lambda i, k: (i, 0))
    outs, xo = carrier_call(
        core, name=name, grid=(n // tm, 4),
        in_specs=[pl.BlockSpec((1, tm, fq), lambda i, k: (k, i, 0)), pl.BlockSpec((1, fq, d), lambda i, k: (k, 0, 0)),
                  tile, _vec_spec(d), _mod_whole(d)],
        out_specs=[tile, tile],
        out_shape=[jax.ShapeDtypeStruct((n, d), F32), jax.ShapeDtypeStruct((n, d), F32)],
        scratch_shapes=[pltpu.VMEM((tm, d), F32)], inputs=[a4, wd4, x, g, mod], exchange=exchange)
    return outs, xo


def ffn_da(df, wd4, gu, name, exchange=None):
    n, d = df.shape
    fq = wd4.shape[1]
    tm = _token_tile(n)

    def core(df_ref, w_ref, gu_ref, o_ref):
        da = _dot(df_ref[...], w_ref[0], NT)
        gg = gu_ref[0, 0].astype(F32)
        uu = gu_ref[1, 0].astype(F32)
        o_ref[0, 0] = (da * uu * _dsilu(gg)).astype(BF16)
        o_ref[1, 0] = (da * _silu(gg)).astype(BF16)

    gu_spec = pl.BlockSpec((2, 1, tm, fq), lambda i, k: (0, k, i, 0))
    outs, xo = carrier_call(
        core, name=name, grid=(n // tm, 4),
        in_specs=[pl.BlockSpec((tm, d), lambda i, k: (i, 0)), pl.BlockSpec((1, fq, d), lambda i, k: (k, 0, 0)), gu_spec],
        out_specs=[gu_spec], out_shape=[jax.ShapeDtypeStruct(gu.shape, BF16)],
        inputs=[df, wd4, gu], exchange=exchange)
    return outs[0], xo


def ffn_dh(dgu, wgu, x, dxo, g, mod, s, k0, name, exchange=None):
    n, d = x.shape
    fq = wgu.shape[-1]
    tm = _token_tile(n)

    def core(dgu_ref, w_ref, x_ref, dxo_ref, g_ref, mod_ref, dx_ref, red_ref, acc):
        i, k = pl.program_id(0), pl.program_id(1)

        @pl.when(k == 0)
        def _():
            acc[...] = jnp.zeros_like(acc)

        acc[...] += _dot(dgu_ref[0, 0], w_ref[0, 0], NT) + _dot(dgu_ref[1, 0], w_ref[1, 0], NT)

        @pl.when(k == 3)
        def _():
            is_ctx = _is_ctx(i, tm, s)
            dx, sums = _pre_bwd(x_ref[...], acc[...], dxo_ref[...], g_ref[...], mod_ref, k0 + 1, is_ctx)
            dx_ref[...] = dx
            _red_add(red_ref, i == 0, is_ctx, sums)

    tile = pl.BlockSpec((tm, d), lambda i, k: (i, 0))
    outs, xo = carrier_call(
        core, name=name, grid=(n // tm, 4),
        in_specs=[pl.BlockSpec((2, 1, tm, fq), lambda i, k: (0, k, i, 0)),
                  pl.BlockSpec((2, 1, d, fq), lambda i, k: (0, k, 0, 0)), tile, tile, _vec_spec(d), _mod_whole(d)],
        out_specs=[tile, _red_whole(d)],
        out_shape=[jax.ShapeDtypeStruct((n, d), F32), jax.ShapeDtypeStruct((2, 8, d), F32)],
        scratch_shapes=[pltpu.VMEM((tm, d), F32)], inputs=[dgu, wgu, x, dxo, g, mod], exchange=exchange)
    return outs, xo


def grad_weight(a, b, name, a_lead=None, b_lead=None, b_cols=None):
    n = a.shape[-2]
    ka = a.shape[-1]
    kb = b_cols or b.shape[-1]
    nj = a_lead or b_lead or (b.shape[-1] // b_cols)
    tk = _pick(n, (1408, 1024, 768, 640, 512, 256))
    nk = n // tk

    def core(a_ref, b_ref, o_ref, acc):
        kk = pl.program_id(1)

        @pl.when(kk == 0)
        def _():
            acc[...] = jnp.zeros_like(acc)

        av = a_ref[0] if a_lead else a_ref[...]
        bv = b_ref[0] if b_lead else b_ref[...]
        acc[...] += _dot(av, bv, TN)

        @pl.when(kk == nk - 1)
        def _():
            o_ref[0] = acc[...].astype(BF16)

    a_spec = (pl.BlockSpec((1, tk, ka), lambda j, kk: (j, kk, 0)) if a_lead
              else pl.BlockSpec((tk, ka), lambda j, kk: (kk, 0)))
    if b_lead:
        b_spec = pl.BlockSpec((1, tk, kb), lambda j, kk: (j, kk, 0))
    elif b_cols:
        b_spec = pl.BlockSpec((tk, kb), lambda j, kk: (kk, j))
    else:
        b_spec = pl.BlockSpec((tk, kb), lambda j, kk: (kk, 0))
    outs, _ = carrier_call(
        core, name=name, grid=(nj, nk), in_specs=[a_spec, b_spec],
        out_specs=[pl.BlockSpec((1, ka, kb), lambda j, kk: (j, 0, 0))],
        out_shape=[jax.ShapeDtypeStruct((nj, ka, kb), BF16)],
        scratch_shapes=[pltpu.VMEM((ka, kb), F32)], inputs=[a, b])
    return outs[0]


def post_bwd(f, dxo, g, mod, k_gate, coef, s, name):
    n, d = f.shape

    def core(f_ref, dxo_ref, g_ref, mod_ref, df_ref, red_ref):
        i = pl.program_id(0)
        is_ctx = _is_ctx(i, TM, s)
        ff = f_ref[...]
        dxo_ = dxo_ref[...]
        gg = g_ref[...]
        r = _rms(ff)
        fn = ff * r
        dy = (coef * _mod_rows(mod_ref, k_gate, is_ctx)) * dxo_
        dfn = dy * gg
        df_ref[...] = (r * (dfn - fn * jnp.mean(dfn * fn, axis=-1, keepdims=True))).astype(BF16)
        _red_add(red_ref, i == 0, is_ctx, [coef * (fn * gg) * dxo_, dy * fn])

    tile = pl.BlockSpec((TM, d), lambda i: (i, 0))
    outs, _ = carrier_call(
        core, name=name, grid=(n // TM,), in_specs=[tile, tile, _vec_spec(d), _mod_whole(d)],
        out_specs=[tile, _red_whole(d)],
        out_shape=[jax.ShapeDtypeStruct((n, d), BF16), jax.ShapeDtypeStruct((2, 8, d), F32)],
        inputs=[f, dxo, g, mod])
    return outs


def matmul_nt(a, b, name):
    m, k = a.shape
    n = b.shape[0]
    tm = _token_tile(m)

    def core(a_ref, b_ref, o_ref):
        o_ref[...] = _dot(a_ref[...], b_ref[...], NT)

    outs, _ = carrier_call(
        core, name=name, grid=(m // tm,),
        in_specs=[pl.BlockSpec((tm, k), lambda i: (i, 0)), pl.BlockSpec((n, k), lambda i: (0, 0))],
        out_specs=[pl.BlockSpec((tm, n), lambda i: (i, 0))], out_shape=[jax.ShapeDtypeStruct((m, n), F32)],
        inputs=[a, b])
    return outs[0]


def _rope_tables(s, n):
    t = jnp.arange(n)
    lane = jnp.arange(LANES)
    dd = lane % HEAD_DIM
    inv = ROPE_THETA ** (-(dd % ROPE_PAIRS).astype(F32) / ROPE_PAIRS)
    pos = jnp.where(dd[None, :] < HEAD_DIM // 2, (t // GRID_W)[:, None], (t % GRID_W)[:, None]).astype(F32)
    ang = pos * inv[None, :]
    live = (t < s)[:, None]
    first = ((dd % (2 * ROPE_PAIRS)) < ROPE_PAIRS)[None, :]
    cos = jnp.where(live, jnp.cos(ang), 1.0)
    sin = jnp.where(live, jnp.sin(ang), 0.0)
    sa = jnp.where(first, -sin, 0.0)
    sb = jnp.where(first, 0.0, sin)
    return cos.astype(F32), sa.astype(F32), sb.astype(F32)


def _rope(xv, cos, sa, sb):
    return (xv * cos + pltpu.roll(xv, LANES - ROPE_PAIRS, 1) * sa + pltpu.roll(xv, ROPE_PAIRS, 1) * sb)


def mix_in(x, g, mod, win8, tables, s, name):
    n, d = x.shape
    tm = _token_tile(n)
    nb = win8.shape[0]

    def core(x_ref, g_ref, mod_ref, w_ref, c_ref, sa_ref, sb_ref, hb_ref, qkv_ref, u_ref, hb_s):
        i, j = pl.program_id(0), pl.program_id(1)

        @pl.when(j == 0)
        def _():
            hb_s[...] = _norm_mod(x_ref[...], g_ref[...], mod_ref, 3, 4, _is_ctx(i, tm, s))
            hb_ref[...] = hb_s[...]

        y = _dot(hb_s[...], w_ref[0])

        @pl.when(j < 2 * NA_WIDTH // IN_BLOCK)
        def _():
            cos, sa, sb = c_ref[...], sa_ref[...], sb_ref[...]
            for b in range(IN_BLOCK // LANES):
                sl = slice(b * LANES, (b + 1) * LANES)
                qkv_ref[:, sl] = _rope(y[:, sl], cos, sa, sb).astype(BF16)

        @pl.when((j >= 2 * NA_WIDTH // IN_BLOCK) & (j < N_QKV_BLOCKS))
        def _():
            qkv_ref[...] = y.astype(BF16)

        @pl.when(j >= N_QKV_BLOCKS)
        def _():
            u_ref[...] = y

    tab = pl.BlockSpec((tm, LANES), lambda i, j: (i, 0))
    outs, _ = carrier_call(
        core, name=name, grid=(n // tm, nb),
        in_specs=[pl.BlockSpec((tm, d), lambda i, j: (i, 0)), _vec_spec(d), _mod_whole(d),
                  pl.BlockSpec((1, d, IN_BLOCK), lambda i, j: (j, 0, 0)), tab, tab, tab],
        out_specs=[pl.BlockSpec((tm, d), lambda i, j: (i, 0)),
                   pl.BlockSpec((tm, IN_BLOCK), lambda i, j: (i, jnp.minimum(j, N_QKV_BLOCKS - 1))),
                   pl.BlockSpec((tm, IN_BLOCK), lambda i, j: (i, jnp.maximum(j - N_QKV_BLOCKS, 0)))],
        out_shape=[jax.ShapeDtypeStruct((n, d), BF16), jax.ShapeDtypeStruct((n, 3 * NA_WIDTH), BF16),
                   jax.ShapeDtypeStruct((n, POOL_WIDTH), F32)],
        scratch_shapes=[pltpu.VMEM((tm, d), BF16)], inputs=[x, g, mod, win8, *tables])
    return outs


def qkv_bwd(dq, dk, dv, du, tables, name):
    n = dq.shape[0]
    w = NA_WIDTH

    def core(dq_ref, dk_ref, dv_ref, du_ref, c_ref, sa_ref, sb_ref, o_ref):
        cos, sa, sb = c_ref[...], -sa_ref[...], -sb_ref[...]
        for b in range(N_HEAD_BLOCKS):
            sl = slice(b * LANES, (b + 1) * LANES)
            o_ref[:, b * LANES:(b + 1) * LANES] = _rope(dq_ref[:, sl], cos, sa, sb).astype(BF16)
            o_ref[:, w + b * LANES:w + (b + 1) * LANES] = _rope(dk_ref[:, sl], cos, sa, sb).astype(BF16)
        o_ref[:, 2 * w:3 * w] = dv_ref[...].astype(BF16)
        o_ref[:, 3 * w:] = du_ref[...].astype(BF16)

    tab = pl.BlockSpec((TM, LANES), lambda i: (i, 0))
    tile = pl.BlockSpec((TM, w), lambda i: (i, 0))
    outs, _ = carrier_call(
        core, name=name, grid=(n // TM,), in_specs=[tile, tile, tile, tile, tab, tab, tab],
        out_specs=[pl.BlockSpec((TM, 4 * w), lambda i: (i, 0))],
        out_shape=[jax.ShapeDtypeStruct((n, 4 * w), BF16)], inputs=[dq, dk, dv, du, *tables])
    return outs[0]


def mix_out(na, py, wout, x, g, mod, s, name):
    n, d = x.shape
    tm = _token_tile(n)
    half = na.shape[1]

    def core(na_ref, py_ref, w_ref, x_ref, g_ref, mod_ref, f_ref, xo_ref):
        i = pl.program_id(0)
        ff = _dot(na_ref[...], w_ref[:half, :]) + _dot(py_ref[...], w_ref[half:, :])
        f_ref[...] = ff
        xo_ref[...] = _post(x_ref[...], ff, g_ref[...], mod_ref, 5, 1.0, _is_ctx(i, tm, s))

    tile = pl.BlockSpec((tm, d), lambda i: (i, 0))
    htile = pl.BlockSpec((tm, half), lambda i: (i, 0))
    outs, _ = carrier_call(
        core, name=name, grid=(n // tm,),
        in_specs=[htile, htile, pl.BlockSpec((2 * half, d), lambda i: (0, 0)), tile, _vec_spec(d), _mod_whole(d)],
        out_specs=[tile, tile],
        out_shape=[jax.ShapeDtypeStruct((n, d), F32), jax.ShapeDtypeStruct((n, d), F32)],
        inputs=[na, py, wout, x, g, mod])
    return outs


def grad_wout(na, py, dfm, name):
    n, half = na.shape
    d = dfm.shape[1]
    tk = _pick(n, (1408, 1024, 768, 640, 512, 256))
    nk = n // tk

    def core(na_ref, py_ref, b_ref, o_ref, acc):
        hh, kk = pl.program_id(0), pl.program_id(1)

        @pl.when(kk == 0)
        def _():
            acc[...] = jnp.zeros_like(acc)

        @pl.when(hh == 0)
        def _():
            acc[...] += _dot(na_ref[...], b_ref[...], TN)

        @pl.when(hh == 1)
        def _():
            acc[...] += _dot(py_ref[...], b_ref[...], TN)

        @pl.when(kk == nk - 1)
        def _():
            o_ref[0] = acc[...].astype(BF16)

    htile = pl.BlockSpec((tk, half), lambda hh, kk: (kk, 0))
    outs, _ = carrier_call(
        core, name=name, grid=(2, nk), in_specs=[htile, htile, pl.BlockSpec((tk, d), lambda hh, kk: (kk, 0))],
        out_specs=[pl.BlockSpec((1, half, d), lambda hh, kk: (hh, 0, 0))],
        out_shape=[jax.ShapeDtypeStruct((2, half, d), BF16)],
        scratch_shapes=[pltpu.VMEM((half, d), F32)], inputs=[na, py, dfm])
    return outs[0]


def mix_dh(dqkvu, win8, x, dxo, g, mod, s, name, exchange=None):
    n, d = x.shape
    tm = _token_tile(n)
    nb = win8.shape[0]

    def core(dq_ref, w_ref, x_ref, dxo_ref, g_ref, mod_ref, dx_ref, red_ref, acc):
        i, j = pl.program_id(0), pl.program_id(1)

        @pl.when(j == 0)
        def _():
            acc[...] = jnp.zeros_like(acc)

        acc[...] += _dot(dq_ref[...], w_ref[0], NT)

        @pl.when(j == nb - 1)
        def _():
            is_ctx = _is_ctx(i, tm, s)
            dx, sums = _pre_bwd(x_ref[...], acc[...], dxo_ref[...], g_ref[...], mod_ref, 4, is_ctx)
            dx_ref[...] = dx
            _red_add(red_ref, i == 0, is_ctx, sums)

    tile = pl.BlockSpec((tm, d), lambda i, j: (i, 0))
    outs, xo = carrier_call(
        core, name=name, grid=(n // tm, nb),
        in_specs=[pl.BlockSpec((tm, IN_BLOCK), lambda i, j: (i, j)),
                  pl.BlockSpec((1, d, IN_BLOCK), lambda i, j: (j, 0, 0)), tile, tile, _vec_spec(d), _mod_whole(d)],
        out_specs=[tile, _red_whole(d)],
        out_shape=[jax.ShapeDtypeStruct((n, d), F32), jax.ShapeDtypeStruct((2, 8, d), F32)],
        scratch_shapes=[pltpu.VMEM((tm, d), F32)], inputs=[dqkvu, win8, x, dxo, g, mod], exchange=exchange)
    return outs, xo


def _na_consts():
    j = np.arange(GRID_W)
    col_start = np.clip(j - NA_KW // 2, 0, GRID_W - NA_KW)
    valid = (j[None, :] >= col_start[:, None]) & (j[None, :] < col_start[:, None] + NA_KW)
    dc = np.clip(j[None, :] - j[:, None] + NA_KW - 1, 0, 2 * NA_KW - 2)
    onehot = np.zeros((LANES, GRID_W, GRID_W), np.float32)
    for d in range(2 * NA_KW - 1):
        onehot[d] = ((dc == d) & valid).astype(np.float32)
    negmask = np.where(valid, 0.0, NEG_INF).astype(np.float32)
    return onehot.reshape(LANES, GRID_W * GRID_W), np.tile(negmask, (1, NA_KH))


def bias_tables(rpb, name):
    onehot, negmask = _na_consts()
    nj = 2 * NA_KH - 1
    rows = NA_HEADS * nj
    a = jnp.pad(rpb.reshape(rows, 2 * NA_KW - 1), ((0, 0), (0, LANES - (2 * NA_KW - 1))))

    def body(a_ref, e_ref, o_ref):
        o_ref[...] = _dot(a_ref[...], e_ref[...], precision=HIGHEST)

    t = pl.pallas_call(
        body, name=name, out_shape=jax.ShapeDtypeStruct((rows, GRID_W * GRID_W), F32),
        in_specs=[pl.BlockSpec(memory_space=pltpu.VMEM)] * 2,
        out_specs=pl.BlockSpec(memory_space=pltpu.VMEM),
    )(a, jnp.asarray(onehot))
    t = t.reshape(NA_HEADS, nj, GRID_W, GRID_W)
    tb = jnp.stack([t[:, j0:j0 + NA_KH] for j0 in range(NA_KH)])
    tb = tb.transpose(0, 1, 3, 2, 4).reshape(NA_KH, NA_HEADS, GRID_W, NA_KH * GRID_W)
    return tb + jnp.asarray(negmask)[None, None]


def bias_tables_bwd(dtb, name):
    onehot, _ = _na_consts()
    nj = 2 * NA_KH - 1
    d5 = dtb.reshape(NA_KH, NA_HEADS, GRID_W, NA_KH, GRID_W).transpose(0, 3, 1, 2, 4)
    d2 = d5.reshape(NA_KH * NA_KH * NA_HEADS, GRID_W * GRID_W)

    def body(d_ref, e_ref, o_ref):
        r = _dot(d_ref[...], e_ref[...], NT, precision=HIGHEST)
        for j in range(nj):
            acc = jnp.zeros((NA_HEADS, LANES), F32)
            for j0 in range(NA_KH):
                kk = j - j0
                if 0 <= kk < NA_KH:
                    base = (j0 * NA_KH + kk) * NA_HEADS
                    acc = acc + r[base:base + NA_HEADS, :]
            o_ref[j] = acc

    out = pl.pallas_call(
        body, name=name, out_shape=jax.ShapeDtypeStruct((nj, NA_HEADS, LANES), F32),
        in_specs=[pl.BlockSpec(memory_space=pltpu.VMEM)] * 2,
        out_specs=pl.BlockSpec(memory_space=pltpu.VMEM),
        compiler_params=pltpu.CompilerParams(vmem_limit_bytes=VMEM_LIMIT),
    )(d2, jnp.asarray(onehot))
    return out[:, :, :2 * NA_KW - 1].transpose(1, 0, 2)


def _head_masks():
    lane = lax.broadcasted_iota(jnp.int32, (1, LANES), 1)
    return [(lane >= h * HEAD_DIM) & (lane < (h + 1) * HEAD_DIM) for h in range(HEADS_PER_BLOCK)]


def _row_window(r, rows):
    rs = jnp.clip(r - NA_KH // 2, 0, rows - NA_KH)
    return rs - r + NA_KH - 1, pl.multiple_of(rs * GRID_W, GRID_W)


NA_ROWS_PER_STEP = 2
NA_STEP = NA_ROWS_PER_STEP * GRID_W
SLAB = NA_KH * GRID_W
K_COL = N_HEAD_BLOCKS
V_COL = 2 * N_HEAD_BLOCKS


def na_fwd(qkv, tb, s, name, exchange=None):
    n = qkv.shape[0]
    l = n - s
    rows = s // GRID_W
    rr = NA_ROWS_PER_STEP
    x_steps = rows // rr

    def core(q_ref, k_ref, v_ref, kc_ref, vc_ref, tb_ref, o_ref, lse_ref):
        rb = pl.program_id(1)

        @pl.when(rb >= x_steps)
        def _():
            o_ref[...] = jnp.zeros_like(o_ref)
            lse_ref[...] = jnp.zeros_like(lse_ref)

        @pl.when(rb < x_steps)
        def _():
            masks = _head_masks()
            kcb, vcb = kc_ref[...], vc_ref[...]
            for t in range(rr):
                j0, off = _row_window(rb * rr + t, rows)
                kslab = k_ref[pl.ds(off, SLAB), :]
                vslab = v_ref[pl.ds(off, SLAB), :]
                qt = q_ref[t * GRID_W:(t + 1) * GRID_W, :]
                o_acc = jnp.zeros((GRID_W, LANES), F32)
                lse_acc = jnp.zeros((GRID_W, LANES), F32)
                for h in range(HEADS_PER_BLOCK):
                    qh = jnp.where(masks[h], qt, jnp.zeros_like(qt))
                    s_loc = _dot(qh, kslab, NT) * ATT_SCALE + tb_ref[j0, h]
                    s_ctx = _dot(qh, kcb, NT) * ATT_SCALE
                    m = jnp.maximum(jnp.max(s_loc, axis=-1, keepdims=True), jnp.max(s_ctx, axis=-1, keepdims=True))
                    p_loc = jnp.exp(s_loc - m)
                    p_ctx = jnp.exp(s_ctx - m)
                    den = jnp.sum(p_loc, axis=-1, keepdims=True) + jnp.sum(p_ctx, axis=-1, keepdims=True)
                    o = (_dot(p_loc.astype(BF16), vslab) + _dot(p_ctx.astype(BF16), vcb)) / den
                    o_acc = jnp.where(masks[h], o, o_acc)
                    lse_acc = jnp.where(masks[h], m + jnp.log(den), lse_acc)
                o_ref[t * GRID_W:(t + 1) * GRID_W, :] = o_acc.astype(BF16)
                lse_ref[0, t * GRID_W:(t + 1) * GRID_W, :] = lse_acc

    cb = s // l
    outs, xo = carrier_call(
        core, name=name, grid=(N_HEAD_BLOCKS, n // NA_STEP),
        in_specs=[pl.BlockSpec((NA_STEP, LANES), lambda hb, rb: (jnp.minimum(rb, x_steps - 1), hb)),
                  pl.BlockSpec((s, LANES), lambda hb, rb: (0, K_COL + hb)),
                  pl.BlockSpec((s, LANES), lambda hb, rb: (0, V_COL + hb)),
                  pl.BlockSpec((l, LANES), lambda hb, rb: (cb, K_COL + hb)),
                  pl.BlockSpec((l, LANES), lambda hb, rb: (cb, V_COL + hb)),
                  pl.BlockSpec((NA_KH, HEADS_PER_BLOCK, GRID_W, SLAB), lambda hb, rb: (0, hb, 0, 0))],
        out_specs=[pl.BlockSpec((NA_STEP, LANES), lambda hb, rb: (rb, hb)),
                   pl.BlockSpec((1, NA_STEP, LANES), lambda hb, rb: (hb, rb, 0))],
        out_shape=[jax.ShapeDtypeStruct((n, NA_WIDTH), BF16), jax.ShapeDtypeStruct((N_HEAD_BLOCKS, n, LANES), F32)],
        inputs=[qkv, qkv, qkv, qkv, qkv, tb], exchange=exchange)
    return outs, xo


def na_bwd(qkv, tb, o, dmix, lse, s, name, exchange=None):
    n = qkv.shape[0]
    l = n - s
    rows = s // GRID_W
    rr = NA_ROWS_PER_STEP
    x_steps = rows // rr

    def core(q_ref, k_ref, v_ref, kc_ref, vc_ref, tb_ref, o_ref, do_ref, lse_ref, dq_ref, dk_ref, dv_ref, dtb_ref):
        rb = pl.program_id(1)

        @pl.when(rb == 0)
        def _():
            dk_ref[...] = jnp.zeros_like(dk_ref)
            dv_ref[...] = jnp.zeros_like(dv_ref)
            dtb_ref[...] = jnp.zeros_like(dtb_ref)

        @pl.when(rb >= x_steps)
        def _():
            dq_ref[...] = jnp.zeros_like(dq_ref)

        @pl.when(rb < x_steps)
        def _():
            masks = _head_masks()
            kcb, vcb = kc_ref[...], vc_ref[...]
            for t in range(rr):
                j0, off = _row_window(rb * rr + t, rows)
                sl = slice(t * GRID_W, (t + 1) * GRID_W)
                kslab = k_ref[pl.ds(off, SLAB), :]
                vslab = v_ref[pl.ds(off, SLAB), :]
                qt = q_ref[sl, :]
                do_f = do_ref[sl, :]
                dd = do_f * o_ref[sl, :].astype(F32)
                do_b = do_f.astype(BF16)
                lse_t = lse_ref[0, sl, :]
                dq_acc = jnp.zeros((GRID_W, LANES), F32)
                dk_acc = jnp.zeros((SLAB, LANES), F32)
                dv_acc = jnp.zeros((SLAB, LANES), F32)
                dkc_acc = jnp.zeros((l, LANES), F32)
                dvc_acc = jnp.zeros((l, LANES), F32)
                for h in range(HEADS_PER_BLOCK):
                    qh = jnp.where(masks[h], qt, jnp.zeros_like(qt))
                    doh = jnp.where(masks[h], do_b, jnp.zeros_like(do_b))
                    delta = jnp.sum(jnp.where(masks[h], dd, 0.0), axis=-1, keepdims=True)
                    lse_h = lse_t[:, h * HEAD_DIM:h * HEAD_DIM + 1]
                    p_loc = jnp.exp(_dot(qh, kslab, NT) * ATT_SCALE + tb_ref[j0, h] - lse_h)
                    p_ctx = jnp.exp(_dot(qh, kcb, NT) * ATT_SCALE - lse_h)
                    ds_loc = p_loc * (_dot(doh, vslab, NT) - delta)
                    ds_ctx = p_ctx * (_dot(doh, vcb, NT) - delta)
                    dtb_ref[j0, h] += ds_loc
                    dsl_b = ds_loc.astype(BF16)
                    dsc_b = ds_ctx.astype(BF16)
                    dq_h = (_dot(dsl_b, kslab) + _dot(dsc_b, kcb)) * ATT_SCALE
                    dq_acc = jnp.where(masks[h], dq_h, dq_acc)
                    dk_acc = dk_acc + _dot(dsl_b, qh, TN)
                    dv_acc = dv_acc + _dot(p_loc.astype(BF16), doh, TN)
                    dkc_acc = dkc_acc + _dot(dsc_b, qh, TN)
                    dvc_acc = dvc_acc + _dot(p_ctx.astype(BF16), doh, TN)
                dq_ref[sl, :] = dq_acc
                dk_ref[pl.ds(off, SLAB), :] += dk_acc * ATT_SCALE
                dv_ref[pl.ds(off, SLAB), :] += dv_acc
                dk_ref[s:, :] += dkc_acc * ATT_SCALE
                dv_ref[s:, :] += dvc_acc

    cb = s // l
    clamp = lambda hb, rb: (jnp.minimum(rb, x_steps - 1), hb)
    tile_in = pl.BlockSpec((NA_STEP, LANES), clamp)
    whole_out = pl.BlockSpec((n, LANES), lambda hb, rb: (0, hb))
    tbs = pl.BlockSpec((NA_KH, HEADS_PER_BLOCK, GRID_W, SLAB), lambda hb, rb: (0, hb, 0, 0))
    f32n = jax.ShapeDtypeStruct((n, NA_WIDTH), F32)
    outs, xo = carrier_call(
        core, name=name, grid=(N_HEAD_BLOCKS, n // NA_STEP),
        in_specs=[tile_in,
                  pl.BlockSpec((s, LANES), lambda hb, rb: (0, K_COL + hb)),
                  pl.BlockSpec((s, LANES), lambda hb, rb: (0, V_COL + hb)),
                  pl.BlockSpec((l, LANES), lambda hb, rb: (cb, K_COL + hb)),
                  pl.BlockSpec((l, LANES), lambda hb, rb: (cb, V_COL + hb)),
                  tbs, tile_in, tile_in,
                  pl.BlockSpec((1, NA_STEP, LANES), lambda hb, rb: (hb, jnp.minimum(rb, x_steps - 1), 0))],
        out_specs=[pl.BlockSpec((NA_STEP, LANES), lambda hb, rb: (rb, hb)), whole_out, whole_out, tbs],
        out_shape=[f32n, f32n, f32n, jax.ShapeDtypeStruct((NA_KH, NA_HEADS, GRID_W, SLAB), F32)],
        inputs=[qkv, qkv, qkv, qkv, qkv, tb, o, dmix, lse], exchange=exchange)
    return outs, xo


def ctx_attn_fwd(qkv, na, s, name):
    n = qkv.shape[0]
    l = n - s
    cb = s // l

    def core(q_ref, k_ref, v_ref, na_in, o_ref, lse_ref):
        masks = _head_masks()
        qt, kb, vb = q_ref[...], k_ref[...], v_ref[...]
        o_acc = jnp.zeros((l, LANES), F32)
        lse_acc = jnp.zeros((l, LANES), F32)
        for h in range(HEADS_PER_BLOCK):
            qh = jnp.where(masks[h], qt, jnp.zeros_like(qt))
            sc = _dot(qh, kb, NT) * ATT_SCALE
            m = jnp.max(sc, axis=-1, keepdims=True)
            p = jnp.exp(sc - m)
            den = jnp.sum(p, axis=-1, keepdims=True)
            o_acc = jnp.where(masks[h], _dot(p.astype(BF16), vb) / den, o_acc)
            lse_acc = jnp.where(masks[h], m + jnp.log(den), lse_acc)
        o_ref[...] = o_acc.astype(BF16)
        lse_ref[0] = lse_acc

    outs, _ = carrier_call(
        core, name=name, grid=(N_HEAD_BLOCKS,),
        in_specs=[pl.BlockSpec((l, LANES), lambda hb: (cb, hb)), pl.BlockSpec((l, LANES), lambda hb: (cb, K_COL + hb)),
                  pl.BlockSpec((l, LANES), lambda hb: (cb, V_COL + hb)), ANY],
        out_specs=[pl.BlockSpec((l, LANES), lambda hb: (cb, hb)), pl.BlockSpec((1, l, LANES), lambda hb: (hb, 0, 0))],
        out_shape=[jax.ShapeDtypeStruct(na.shape, BF16), jax.ShapeDtypeStruct((N_HEAD_BLOCKS, l, LANES), F32)],
        inputs=[qkv, qkv, qkv, na], aliases={3: 0})
    return outs


def ctx_attn_bwd(qkv, na, dmix, lse, dq, dk, dv, s, name):
    n = qkv.shape[0]
    l = n - s
    cb = s // l

    def core(q_ref, k_ref, v_ref, o_ref, do_ref, lse_ref, dq_in, dk_in, dv_in, dq_ref, dk_ref, dv_ref):
        masks = _head_masks()
        qt, kb, vb = q_ref[...], k_ref[...], v_ref[...]
        do_f = do_ref[...]
        dd = do_f * o_ref[...].astype(F32)
        do_b = do_f.astype(BF16)
        lse_t = lse_ref[0]
        dq_acc = jnp.zeros((l, LANES), F32)
        dk_acc = jnp.zeros((l, LANES), F32)
        dv_acc = jnp.zeros((l, LANES), F32)
        for h in range(HEADS_PER_BLOCK):
            qh = jnp.where(masks[h], qt, jnp.zeros_like(qt))
            doh = jnp.where(masks[h], do_b, jnp.zeros_like(do_b))
            delta = jnp.sum(jnp.where(masks[h], dd, 0.0), axis=-1, keepdims=True)
            p = jnp.exp(_dot(qh, kb, NT) * ATT_SCALE - lse_t[:, h * HEAD_DIM:h * HEAD_DIM + 1])
            ds = (p * (_dot(doh, vb, NT) - delta)).astype(BF16)
            dq_acc = jnp.where(masks[h], _dot(ds, kb) * ATT_SCALE, dq_acc)
            dk_acc = dk_acc + _dot(ds, qh, TN)
            dv_acc = dv_acc + _dot(p.astype(BF16), doh, TN)
        dq_ref[...] = dq_acc
        dk_ref[...] = dk_in[...] + dk_acc * ATT_SCALE
        dv_ref[...] = dv_in[...] + dv_acc

    blk = pl.BlockSpec((l, LANES), lambda hb: (cb, hb))
    f32n = jax.ShapeDtypeStruct((n, NA_WIDTH), F32)
    outs, _ = carrier_call(
        core, name=name, grid=(N_HEAD_BLOCKS,),
        in_specs=[blk, pl.BlockSpec((l, LANES), lambda hb: (cb, K_COL + hb)),
                  pl.BlockSpec((l, LANES), lambda hb: (cb, V_COL + hb)), blk, blk,
                  pl.BlockSpec((1, l, LANES), lambda hb: (hb, 0, 0)), ANY, blk, blk],
        out_specs=[blk, blk, blk], out_shape=[f32n, f32n, f32n],
        inputs=[qkv, qkv, qkv, na, dmix, lse, dq, dk, dv], aliases={6: 0, 7: 1, 8: 2})
    return outs


def _pool_consts(l):
    assert l == TM
    mem = np.zeros((2, POOL_GROUPS, TM, TM), np.float32)
    inv = np.zeros((2, POOL_GROUPS, TM, LANES), np.float32)
    for which, length in ((0, GRID_W), (1, l)):
        t = np.arange(length)
        for g, w in enumerate(POOL_WINDOWS):
            lo = np.clip(t - w // 2, 0, length)
            hi = np.clip(t - w // 2 + w, 0, length)
            blockm = ((t[None, :] >= lo[:, None]) & (t[None, :] < hi[:, None])).astype(np.float32)
            cnt = (hi - lo).astype(np.float32)
            for b in range(TM // length):
                mem[which, g, b * length:(b + 1) * length, b * length:(b + 1) * length] = blockm
                inv[which, g, b * length:(b + 1) * length, :] = (1.0 / cnt)[:, None]
    return mem, np.ascontiguousarray(mem.transpose(0, 1, 3, 2)), inv


def _split_dot(m01, val):
    hi = val.astype(BF16)
    lo = (val - hi.astype(F32)).astype(BF16)
    return _dot(m01, hi) + _dot(m01, lo)


def pool_fwd(u, mem, inv, wp, scale, nx_tiles, name):
    n = u.shape[0]

    def core(u_ref, m_ref, i_ref, wp_ref, s_ref, o_ref):
        for g in range(POOL_GROUPS):
            sl = slice(g * POOL_CH, (g + 1) * POOL_CH)
            ug = u_ref[:, sl]
            dg = _split_dot(m_ref[0, g], ug) * i_ref[0, g] - ug
            o_ref[:, sl] = (_dot(dg.astype(BF16), wp_ref[g]) * s_ref[:, sl]).astype(BF16)

    grp = lambda i: (i // nx_tiles, 0, 0, 0)
    outs, _ = carrier_call(
        core, name=name, grid=(n // TM,),
        in_specs=[pl.BlockSpec((TM, POOL_WIDTH), lambda i: (i, 0)),
                  pl.BlockSpec((1, POOL_GROUPS, TM, TM), grp),
                  pl.BlockSpec((1, POOL_GROUPS, TM, LANES), grp),
                  pl.BlockSpec((POOL_GROUPS, POOL_CH, POOL_CH), lambda i: (0, 0, 0)),
                  pl.BlockSpec((1, POOL_WIDTH), lambda i: (0, 0))],
        out_specs=[pl.BlockSpec((TM, POOL_WIDTH), lambda i: (i, 0))],
        out_shape=[jax.ShapeDtypeStruct((n, POOL_WIDTH), BF16)], inputs=[u, mem, inv, wp, scale])
    return outs[0]


def pool_bwd(dmix, u, mem, mem_t, inv, wp, scale, nx_tiles, name):
    n = u.shape[0]

    def core(dy_ref, u_ref, m_ref, mt_ref, i_ref, wp_ref, s_ref, du_ref, dwp_ref, dsc_ref):
        @pl.when(pl.program_id(0) == 0)
        def _():
            dwp_ref[...] = jnp.zeros_like(dwp_ref)
            dsc_ref[...] = jnp.zeros_like(dsc_ref)

        for g in range(POOL_GROUPS):
            sl = slice(g * POOL_CH, (g + 1) * POOL_CH)
            ug = u_ref[:, sl]
            dy = dy_ref[:, sl]
            dg = (_split_dot(m_ref[0, g], ug) * i_ref[0, g] - ug).astype(BF16)
            z = _dot(dg, wp_ref[g])
            dsc_ref[0:1, sl] += jnp.sum(dy * z, axis=0, keepdims=True)
            dz = (dy * s_ref[:, sl]).astype(BF16)
            dwp_ref[g] += _dot(dg, dz, TN)
            dd = _dot(dz, wp_ref[g], NT)
            du_ref[:, sl] = _split_dot(mt_ref[0, g], dd * i_ref[0, g]) - dd

    grp = lambda i: (i // nx_tiles, 0, 0, 0)
    outs, _ = carrier_call(
        core, name=name, grid=(n // TM,),
        in_specs=[pl.BlockSpec((TM, POOL_WIDTH), lambda i: (i, 1)),
                  pl.BlockSpec((TM, POOL_WIDTH), lambda i: (i, 0)),
                  pl.BlockSpec((1, POOL_GROUPS, TM, TM), grp),
                  pl.BlockSpec((1, POOL_GROUPS, TM, TM), grp),
                  pl.BlockSpec((1, POOL_GROUPS, TM, LANES), grp),
                  pl.BlockSpec((POOL_GROUPS, POOL_CH, POOL_CH), lambda i: (0, 0, 0)),
                  pl.BlockSpec((1, POOL_WIDTH), lambda i: (0, 0))],
        out_specs=[pl.BlockSpec((TM, POOL_WIDTH), lambda i: (i, 0)),
                   pl.BlockSpec((POOL_GROUPS, POOL_CH, POOL_CH), lambda i: (0, 0, 0)),
                   pl.BlockSpec((8, POOL_WIDTH), lambda i: (0, 0))],
        out_shape=[jax.ShapeDtypeStruct((n, POOL_WIDTH), F32),
                   jax.ShapeDtypeStruct((POOL_GROUPS, POOL_CH, POOL_CH), F32),
                   jax.ShapeDtypeStruct((8, POOL_WIDTH), F32)],
        inputs=[dmix, u, mem, mem_t, inv, wp, scale])
    return outs


MOD_ROWS = 16


def mod_fwd(cvecs, w, b, name):
    _, d = cvecs.shape
    cl = w.shape[2]
    tc = _pick(cl, (384, 128))

    def core(c_ref, w_ref, b_ref, o_ref):
        a = _silu(c_ref[...]).astype(BF16)
        o_ref[0] = _dot(a, w_ref[0].astype(BF16)) + b_ref[0]

    outs, _ = carrier_call(
        core, name=name, grid=(DEPTH, cl // tc),
        in_specs=[pl.BlockSpec((MOD_ROWS, d), lambda li, j: (0, 0)),
                  pl.BlockSpec((1, d, tc), lambda li, j: (li, 0, j)),
                  pl.BlockSpec((1, 1, tc), lambda li, j: (li, 0, j))],
        out_specs=[pl.BlockSpec((1, MOD_ROWS, tc), lambda li, j: (li, 0, j))],
        out_shape=[jax.ShapeDtypeStruct((DEPTH, MOD_ROWS, cl), F32)], inputs=[cvecs, w, b])
    return outs[0]


def mod_bwd(cvecs, dm, w, name):
    _, d = cvecs.shape
    cl = w.shape[2]
    tc = _pick(cl, (384, 128))

    def core(c_ref, dm_ref, w_ref, dw_ref, da_ref):
        @pl.when((pl.program_id(0) == 0) & (pl.program_id(1) == 0))
        def _():
            da_ref[...] = jnp.zeros_like(da_ref)

        a = _silu(c_ref[...]).astype(BF16)
        dmb = dm_ref[0].astype(BF16)
        dw_ref[0] = _dot(a, dmb, TN)
        da_ref[...] += _dot(dmb, w_ref[0].astype(BF16), NT)

    outs, _ = carrier_call(
        core, name=name, grid=(DEPTH, cl // tc),
        in_specs=[pl.BlockSpec((MOD_ROWS, d), lambda li, j: (0, 0)),
                  pl.BlockSpec((1, MOD_ROWS, tc), lambda li, j: (li, 0, j)),
                  pl.BlockSpec((1, d, tc), lambda li, j: (li, 0, j))],
        out_specs=[pl.BlockSpec((1, d, tc), lambda li, j: (li, 0, j)),
                   pl.BlockSpec((MOD_ROWS, d), lambda li, j: (0, 0))],
        out_shape=[jax.ShapeDtypeStruct((DEPTH, d, cl), F32), jax.ShapeDtypeStruct((MOD_ROWS, d), F32)],
        inputs=[cvecs, dm, w])
    return outs


def loss_head(y, target, name):
    n, d = y.shape
    s = target.shape[0]
    nt, nx = n // TM, s // TM

    def core(y_ref, t_ref, l_ref, dy_ref, acc_ref):
        i = pl.program_id(0)

        @pl.when(i == 0)
        def _():
            acc_ref[...] = jnp.zeros_like(acc_ref)

        @pl.when(i < nx)
        def _():
            e = y_ref[...] - t_ref[...]
            dy_ref[...] = e * (1.0 / d)
            acc_ref[...] += jnp.sum(e * e, axis=0, keepdims=True)

        @pl.when(i >= nx)
        def _():
            dy_ref[...] = jnp.zeros_like(dy_ref)

        @pl.when(i == nt - 1)
        def _():
            l_ref[...] = jnp.sum(acc_ref[...], axis=1, keepdims=True) * (0.5 / d)

    tile = pl.BlockSpec((TM, d), lambda i: (i, 0))
    outs, _ = carrier_call(
        core, name=name, grid=(nt,),
        in_specs=[tile, pl.BlockSpec((TM, d), lambda i: (jnp.minimum(i, nx - 1), 0))],
        out_specs=[pl.BlockSpec((1, 1), lambda i: (0, 0)), tile],
        out_shape=[jax.ShapeDtypeStruct((1, 1), F32), jax.ShapeDtypeStruct((n, d), F32)],
        scratch_shapes=[pltpu.VMEM((1, d), F32)], inputs=[y, target])
    return outs


def sum_devices(v, name):
    _, r, c = v.shape
    tr = _pick(r, (64, 8))

    def core(v_ref, o_ref):
        acc = v_ref[0]
        for p in range(1, N_DEV):
            acc = acc + v_ref[p]
        o_ref[...] = acc

    outs, _ = carrier_call(
        core, name=name, grid=(r // tr,), in_specs=[pl.BlockSpec((N_DEV, tr, c), lambda i: (0, i, 0))],
        out_specs=[pl.BlockSpec((tr, c), lambda i: (i, 0))], out_shape=[jax.ShapeDtypeStruct((r, c), F32)],
        inputs=[v])
    return outs[0]


def cctx_grad(parts, c_ctx, name):
    d = c_ctx.shape[1]

    def body(p_ref, c_ref, o_ref):
        acc = p_ref[0]
        for p in range(1, N_DEV):
            acc = acc + p_ref[p]
        o_ref[...] = acc[8:9, :] * _dsilu(c_ref[...])

    return pl.pallas_call(
        body, name=name, out_shape=jax.ShapeDtypeStruct((1, d), F32),
        in_specs=[pl.BlockSpec(memory_space=pltpu.VMEM)] * 2,
        out_specs=pl.BlockSpec(memory_space=pltpu.VMEM),
    )(parts, c_ctx)


def _adam_math(w, g, m, v):
    m2 = ADAM_B1 * m + (1.0 - ADAM_B1) * g
    v2 = ADAM_B2 * v + (1.0 - ADAM_B2) * (g * g)
    m_hat = m2 / (1.0 - ADAM_B1 ** ADAM_STEP)
    v_hat = v2 / (1.0 - ADAM_B2 ** ADAM_STEP)
    delta = -ADAM_LR * (m_hat / (jnp.sqrt(v_hat) + ADAM_EPS) + ADAM_WD * w)
    return delta, m2, v2


def adamw(w, g, m, v, name):
    r, c = w.shape
    tr = _pick(r, (256, 128, 64, 32, 16, 8, r))

    def core(w_ref, g_ref, m_ref, v_ref, d_ref, m2_ref, v2_ref):
        d_ref[...], m2_ref[...], v2_ref[...] = _adam_math(w_ref[...], g_ref[...], m_ref[...], v_ref[...])

    tile = pl.BlockSpec((tr, c), lambda i: (i, 0))
    out = jax.ShapeDtypeStruct((r, c), F32)
    outs, _ = carrier_call(core, name=name, grid=(r // tr,), in_specs=[tile] * 4, out_specs=[tile] * 3,
                           out_shape=[out] * 3, inputs=[w, g, m, v])
    return outs


def reduce_adamw(recv, w, m, v, name):
    r, c = w.shape
    tr = _pick(r, (256, 128, 64, 8))

    def core(recv_ref, w_ref, m_ref, v_ref, g_ref, d_ref, m2_ref, v2_ref):
        acc = recv_ref[0].astype(F32)
        for p in range(1, N_DEV):
            acc = acc + recv_ref[p].astype(F32)
        g_ref[...] = acc
        d_ref[...], m2_ref[...], v2_ref[...] = _adam_math(w_ref[...], acc, m_ref[...], v_ref[...])

    tile = pl.BlockSpec((tr, c), lambda i: (i, 0))
    out = jax.ShapeDtypeStruct((r, c), F32)
    outs, _ = carrier_call(
        core, name=name, grid=(r // tr,),
        in_specs=[pl.BlockSpec((N_DEV, tr, c), lambda i: (0, i, 0)), tile, tile, tile],
        out_specs=[tile] * 4, out_shape=[out] * 4, inputs=[recv, w, m, v])
    return outs


def kernel(x, c, ctx, c_ctx, w_mod, b_mod, norm_g, w_ffn_gate_up, w_ffn_down, w_in, w_out, na_rpb, w_pool, pool_scale, loss_target, m_c_ctx, m_w_mod, m_b_mod, m_norm_g, m_w_ffn_gate_up, m_w_ffn_down, m_w_in, m_w_out, m_na_rpb, m_w_pool, m_pool_scale, v_c_ctx, v_w_mod, v_b_mod, v_norm_g, v_w_ffn_gate_up, v_w_ffn_down, v_w_in, v_w_out, v_na_rpb, v_w_pool, v_pool_scale):
    s, d = x.shape[1], x.shape[2]
    l = ctx.shape[1]
    n = s + l
    nx = s // TM
    fq = w_ffn_gate_up.shape[-1]
    fr = w_ffn_down.shape[2]
    cl = w_mod.shape[2]
    dl = norm_g.shape[2]
    me = 4 * lax.axis_index("x") + 2 * lax.axis_index("y") + lax.axis_index("c")

    c_all = all_gather(c, "gather_c").reshape(N_DEV, d)
    cvecs = jnp.concatenate([c_all, c_ctx[None, :], jnp.zeros((MOD_ROWS - N_DEV - 1, d), F32)], axis=0)
    b_loc = lax.dynamic_slice(b_mod, (0, me * cl), (DEPTH, cl)).reshape(DEPTH, 1, cl)
    mod_loc = mod_fwd(cvecs, w_mod, b_loc, "mod_fwd")
    mod_all = all_gather(mod_loc.reshape(DEPTH * MOD_ROWS, cl), "gather_mod")
    mod_all = mod_all.reshape(N_DEV, DEPTH, MOD_ROWS, cl).transpose(1, 2, 0, 3).reshape(DEPTH, MOD_ROWS, N_DEV * cl)
    mine = lax.dynamic_slice(mod_all, (0, me, 0), (DEPTH, 1, N_DEV * cl))
    mods = jnp.concatenate([mine, mod_all[:, N_DEV:N_DEV + 1]], axis=1).reshape(DEPTH, 2, N_MOD, d)

    gu_b = w_ffn_gate_up.astype(BF16)
    dn_b = w_ffn_down.astype(BF16)
    wi_b = w_in.astype(BF16)
    wo_b = w_out.astype(BF16)
    wp_b = w_pool.astype(BF16)

    def ffn_shards(li, i):
        return [gu_b[li, i], dn_b[li, i]]

    def mix_shards(li):
        return [wi_b[li], wo_b[li]]

    def as_ffn_weights(gathered):
        return gathered[0].reshape(2, 4, d, fq), gathered[1].reshape(4, 2 * fr, d)

    def as_mix_weights(gathered):
        return gathered[0], gathered[1].reshape(N_DEV * wo_b.shape[1], d)

    tables = _rope_tables(s, n)
    mem_np, mem_t_np, inv_np = _pool_consts(l)
    mem, mem_t, inv = jnp.asarray(mem_np, BF16), jnp.asarray(mem_t_np, BF16), jnp.asarray(inv_np)
    first = exchange_only(Exchange(gathers=[norm_g.reshape(DEPTH * 6, dl)] + ffn_shards(0, 0)), "gather_first")
    g_full = first[0].reshape(N_DEV, DEPTH, 6, dl).transpose(1, 2, 0, 3).reshape(DEPTH, 6, 1, N_DEV * dl)

    weights = {("ffn", 0, 0): as_ffn_weights(first[1:])}
    saved = {}
    xcur = jnp.concatenate([x[0], ctx[0]], axis=0)
    for li in range(DEPTH):
        last = li == DEPTH - 1
        for i in range(2):
            tag = f"l{li}_ffn{i}"
            wgu, wd4 = weights[("ffn", li, i)]
            if i == 0:
                ex_up, ex_dn = Exchange(gathers=mix_shards(li)), None
            elif not last:
                ex_up, ex_dn = Exchange(gathers=[gu_b[li + 1, 0]]), Exchange(gathers=[dn_b[li + 1, 0]])
            else:
                ex_up = ex_dn = None
            (hb, gu, a4), got_up = ffn_up(xcur, g_full[li, 4 * i], mods[li], wgu, s, 6 * i, tag + "_up", ex_up)
            (ff, xnext), got_dn = ffn_down(a4, wd4, xcur, g_full[li, 4 * i + 1], mods[li], s, 6 * i + 2, tag + "_down", ex_dn)
            saved[("ffn", li, i)] = (xcur, hb, gu, a4, ff)
            xcur = xnext
            if i == 0:
                weights[("mix", li)] = as_mix_weights(got_up)
            elif not last:
                weights[("ffn", li + 1, 0)] = as_ffn_weights(got_up + got_dn)
            if i == 0:
                tag = f"l{li}_mix"
                win8, wout = weights[("mix", li)]
                hb, qkv, u = mix_in(xcur, g_full[li, 2], mods[li], win8, tables, s, tag + "_in")
                tb = bias_tables(na_rpb[li], tag + "_bias")
                (na, lse), got = na_fwd(qkv, tb, s, tag + "_na", Exchange(gathers=ffn_shards(li, 1)))
                weights[("ffn", li, 1)] = as_ffn_weights(got)
                lse_c = None
                if not last:
                    na, lse_c = ctx_attn_fwd(qkv, na, s, tag + "_ctx_attn")
                py = pool_fwd(u, mem, inv, wp_b[li], pool_scale[li][None, :], nx, tag + "_pool")
                fm, xnext = mix_out(na, py, wout, xcur, g_full[li, 3], mods[li], s, tag + "_out")
                saved[("mix", li)] = (xcur, hb, qkv, u, tb, na, lse, lse_c, py, fm)
                xcur = xnext

    loss_local, dcur = loss_head(xcur, loss_target[0], "loss")
    loss = lax.psum(loss_local[0, 0], ("x", "y", "c"))

    recv = {"gu": lax.empty((N_DEV, 2 * DEPTH, d, fq), BF16), "dn": lax.empty((N_DEV, 2 * DEPTH, fr, d), BF16),
            "wi": lax.empty((N_DEV, DEPTH, d, IN_BLOCK), BF16), "wo": lax.empty((N_DEV, DEPTH, wo_b.shape[1], d), BF16)}
    pending = []

    def take(keys):
        nonlocal pending
        jobs = [(gr, recv[key], st) for key, gr, st in pending if key in keys]
        order = [key for key, _, _ in pending if key in keys]
        pending = [p for p in pending if p[0] not in keys]
        return Exchange(a2as=jobs), order

    def put(order, bufs):
        for key, buf in zip(order, bufs):
            recv[key] = buf

    d_rpb, d_wp, d_ps, d_mod, d_g = [], [], [], [], []
    for li in reversed(range(DEPTH)):
        reds = {}
        for i in (1, 0):
            tag = f"l{li}_ffn{i}"
            xin, hb, gu, a4, ff = saved[("ffn", li, i)]
            wgu, wd4 = weights[("ffn", li, i)]
            dff, red1 = post_bwd(ff, dcur, g_full[li, 4 * i + 1], mods[li], 6 * i + 2, 0.5, s, tag + "_post_bwd")
            ex, order = take(("gu", "wi", "wo"))
            dgu, bufs = ffn_da(dff, wd4, gu, tag + "_da", ex)
            put(order, bufs)
            ex, order = take(("dn",))
            (dcur, red2), bufs = ffn_dh(dgu, wgu, xin, dcur, g_full[li, 4 * i], mods[li], s, 6 * i, tag + "_dh", ex)
            put(order, bufs)
            g_dn = grad_weight(a4, dff, tag + "_dwdown", a_lead=4).reshape(N_DEV, fr, d)
            g_gu = grad_weight(hb, dgu.reshape(N_DEV, n, fq), tag + "_dwgu", b_lead=N_DEV)
            pending += [("gu", g_gu, 2 * li + i), ("dn", g_dn, 2 * li + i)]
            reds[i] = (red1, red2)
            if i == 1:
                tag = f"l{li}_mix"
                xin, hb, qkv, u, tb, na, lse, lse_c, py, fm = saved[("mix", li)]
                win8, wout = weights[("mix", li)]
                dfm, redm1 = post_bwd(fm, dcur, g_full[li, 3], mods[li], 5, 1.0, s, tag + "_post_bwd")
                dmix = matmul_nt(dfm, wout, tag + "_dmix")
                g_wo = grad_wout(na, py, dfm, tag + "_dwout").reshape(N_DEV, wo_b.shape[1], d)
                du, gwp, gps = pool_bwd(dmix, u, mem, mem_t, inv, wp_b[li], pool_scale[li][None, :], nx, tag + "_pool_bwd")
                ex, order = take(("gu", "dn"))
                (dq, dk, dv, dtb), bufs = na_bwd(qkv, tb, na, dmix, lse, s, tag + "_na_bwd", ex)
                put(order, bufs)
                if li != DEPTH - 1:
                    dq, dk, dv = ctx_attn_bwd(qkv, na, dmix, lse_c, dq, dk, dv, s, tag + "_ctx_attn_bwd")
                grpb = bias_tables_bwd(dtb, tag + "_bias_bwd")
                dqkvu = qkv_bwd(dq, dk, dv, du, tables, tag + "_rope_bwd")
                (dcur, redm2), _ = mix_dh(dqkvu, win8, xin, dcur, g_full[li, 2], mods[li], s, tag + "_dh")
                g_wi = grad_weight(hb, dqkvu, tag + "_dwin", b_cols=IN_BLOCK)
                pending += [("wi", g_wi, li), ("wo", g_wo, li)]
                d_rpb.insert(0, grpb)
                d_wp.insert(0, gwp)
                d_ps.insert(0, gps[0])
        (ra1, ra2), (rb1, rb2) = reds[0], reds[1]
        d_mod.insert(0, jnp.stack([ra2[:, 0], ra2[:, 1], ra1[:, 0], redm2[:, 0], redm2[:, 1], redm1[:, 0],
                                   rb2[:, 0], rb2[:, 1], rb1[:, 0]], axis=1))
        d_g.insert(0, jnp.stack([t[0] + t[1] for t in (ra2[:, 2], ra1[:, 1], redm2[:, 2], redm1[:, 1], rb2[:, 2], rb1[:, 1])]))
    grad_x = dcur[:s][None]

    def pad8(t):
        t = t.reshape(-1, d) if t.size % d == 0 else jnp.pad(t.reshape(-1), (0, -t.size % d)).reshape(-1, d)
        return jnp.pad(t, ((0, -t.shape[0] % 8), (0, 0)))

    small_parts = [pad8(jnp.stack(d_mod)), pad8(jnp.stack(d_g)), pad8(jnp.stack(d_wp)), pad8(jnp.stack(d_ps)),
                   pad8(jnp.stack(d_rpb))]
    offs = np.cumsum([0] + [p.shape[0] for p in small_parts])
    small_parts.append(jnp.zeros((-offs[-1] % 64, d), F32))
    small = jnp.concatenate(small_parts, axis=0)
    ex, order = take(("gu", "dn", "wi", "wo"))
    ex.gathers = [small]
    ex.n_jobs += 1
    bufs = exchange_only(ex, "exchange_last")
    small_all = bufs[0]
    put(order, bufs[1:])
    small_sum = sum_devices(small_all, "sum_small_grads")

    n_mod_rows = DEPTH * 2 * N_MOD
    dmod_all = small_all[:, :n_mod_rows].reshape(N_DEV, DEPTH, 2, N_MOD * d)
    dmod_sum = small_sum[:n_mod_rows].reshape(DEPTH, 2, N_MOD * d)
    dm_rows = jnp.concatenate([dmod_all[:, :, 0].transpose(1, 0, 2), dmod_sum[:, 1:2],
                               jnp.zeros((DEPTH, MOD_ROWS - N_DEV - 1, N_MOD * d), F32)], axis=1)
    grad_b_mod = dmod_sum[:, 0] + dmod_sum[:, 1]
    dm_loc = lax.dynamic_slice(dm_rows, (0, 0, me * cl), (DEPTH, MOD_ROWS, cl))
    grad_w_mod, da_part = mod_bwd(cvecs, dm_loc, w_mod, "mod_bwd")
    da_all = all_gather(da_part, "gather_dcvec")
    grad_c_ctx = cctx_grad(da_all, c_ctx[None, :], "c_ctx_grad")[0]

    grad_norm_full = small_sum[offs[1]:offs[1] + DEPTH * 6].reshape(DEPTH, 6, d)
    grad_norm_g = lax.dynamic_slice(grad_norm_full, (0, 0, me * dl), (DEPTH, 6, dl))
    grad_w_pool = small_sum[offs[2]:offs[2] + w_pool.size // d].reshape(w_pool.shape)
    grad_pool_scale = small_sum[offs[3]:offs[3] + pool_scale.size // d].reshape(pool_scale.shape)
    grad_na_rpb = small_sum[offs[4]:offs[5]].reshape(-1)[:na_rpb.size].reshape(na_rpb.shape)

    def big_adam(key, w, m, v, name):
        shp = w.shape
        cols = shp[-1]
        outs = reduce_adamw(recv[key].reshape(N_DEV, -1, cols), w.reshape(-1, cols), m.reshape(-1, cols),
                            v.reshape(-1, cols), name)
        return tuple(t.reshape(shp) for t in outs)

    def small_adam(w, g, m, v, name):
        shp = w.shape
        cols = shp[-1]
        outs = adamw(w.reshape(-1, cols), g.reshape(-1, cols), m.reshape(-1, cols), v.reshape(-1, cols), name)
        return tuple(t.reshape(shp) for t in outs)

    b_gu = big_adam("gu", w_ffn_gate_up, m_w_ffn_gate_up, v_w_ffn_gate_up, "adam_gate_up")
    b_dn = big_adam("dn", w_ffn_down, m_w_ffn_down, v_w_ffn_down, "adam_down")
    b_wi = big_adam("wi", w_in, m_w_in, v_w_in, "adam_w_in")
    b_wo = big_adam("wo", w_out, m_w_out, v_w_out, "adam_w_out")
    a_cc = small_adam(c_ctx, grad_c_ctx, m_c_ctx, v_c_ctx, "adam_c_ctx")
    a_wm = small_adam(w_mod, grad_w_mod, m_w_mod, v_w_mod, "adam_w_mod")
    a_bm = small_adam(b_mod, grad_b_mod, m_b_mod, v_b_mod, "adam_b_mod")
    a_ng = small_adam(norm_g, grad_norm_g, m_norm_g, v_norm_g, "adam_norm_g")
    a_rp = small_adam(na_rpb, grad_na_rpb, m_na_rpb, v_na_rpb, "adam_na_rpb")
    a_wp = small_adam(w_pool, grad_w_pool, m_w_pool, v_w_pool, "adam_w_pool")
    a_ps = small_adam(pool_scale, grad_pool_scale, m_pool_scale, v_pool_scale, "adam_pool_scale")

    grads = (grad_c_ctx, grad_w_mod, grad_b_mod, grad_norm_g, b_gu[0], b_dn[0], b_wi[0], b_wo[0], grad_na_rpb, grad_w_pool, grad_pool_scale)
    deltas = (a_cc[0], a_wm[0], a_bm[0], a_ng[0], b_gu[1], b_dn[1], b_wi[1], b_wo[1], a_rp[0], a_wp[0], a_ps[0])
    new_m = (a_cc[1], a_wm[1], a_bm[1], a_ng[1], b_gu[2], b_dn[2], b_wi[2], b_wo[2], a_rp[1], a_wp[1], a_ps[1])
    new_v = (a_cc[2], a_wm[2], a_bm[2], a_ng[2], b_gu[3], b_dn[3], b_wi[3], b_wo[3], a_rp[2], a_wp[2], a_ps[2])
    return (loss, grad_x, *grads, *deltas, *new_m, *new_v)
```

```python
import functools
import math

import numpy as np
import jax
import jax.numpy as jnp
from jax import lax
from jax.experimental import pallas as pl
from jax.experimental.pallas import tpu as pltpu

F32 = jnp.float32
BF16 = jnp.bfloat16

N_DEV = 8
DEPTH = 2
GRID_W = 64
N_MOD = 9
NA_HEADS = 8
HEAD_DIM = 64
NA_WIDTH = NA_HEADS * HEAD_DIM
NA_KH = 8
NA_KW = 16
POOL_GROUPS = 4
POOL_CH = 128
POOL_WIDTH = POOL_GROUPS * POOL_CH
POOL_WINDOWS = (2, 4, 8, 16)
ROPE_THETA = 10000.0
ROPE_PAIRS = HEAD_DIM // 4
RMS_EPS = 1e-6
NEG_INF = -1e30
ATT_SCALE = HEAD_DIM ** -0.5

ADAM_LR = 0.001
ADAM_B1 = 0.9
ADAM_B2 = 0.999
ADAM_EPS = 1e-08
ADAM_WD = 0.01
ADAM_STEP = 10

TM = 256
LANES = 128
HEADS_PER_BLOCK = LANES // HEAD_DIM
N_HEAD_BLOCKS = NA_WIDTH // LANES
IN_BLOCK = 2 * LANES
N_QKV_BLOCKS = 3 * NA_WIDTH // IN_BLOCK
VMEM_LIMIT = 48 * 1024 * 1024
HIGHEST = lax.Precision.HIGHEST
MESH = pl.DeviceIdType.MESH
ANY = pl.BlockSpec(memory_space=pl.ANY)

NN = (((1,), (0,)), ((), ()))
NT = (((1,), (1,)), ((), ()))
TN = (((0,), (0,)), ((), ()))


def _pick(n, cands):
    for t in cands:
        if n % t == 0:
            return t
    raise ValueError(f"no tile for {n} among {cands}")


def _dot(a, b, dn=NN, precision=None):
    return lax.dot_general(a, b, dn, preferred_element_type=F32, precision=precision)


def _silu(x):
    return x * jax.nn.sigmoid(x)


def _dsilu(x):
    s = jax.nn.sigmoid(x)
    return s * (1.0 + x * (1.0 - s))


def _peer(mask):
    x, y, c = lax.axis_index("x"), lax.axis_index("y"), lax.axis_index("c")
    px = 1 - x if mask & 4 else x
    py = 1 - y if mask & 2 else y
    pc = 1 - c if mask & 1 else c
    return (px, py, pc), 4 * px + 2 * py + pc


class Exchange:
    def __init__(self, gathers=(), a2as=()):
        self.gathers = list(gathers)
        self.a2as = list(a2as)
        self.n_jobs = len(self.gathers) + len(self.a2as)

    def inputs(self):
        out = list(self.gathers)
        for v, buf, _ in self.a2as:
            out += [v, buf]
        return out

    def out_shapes(self):
        shapes = [jax.ShapeDtypeStruct((N_DEV,) + v.shape, v.dtype) for v in self.gathers]
        shapes += [jax.ShapeDtypeStruct(buf.shape, buf.dtype) for _, buf, _ in self.a2as]
        return shapes

    def aliases(self, n_in, n_out):
        ng = len(self.gathers)
        return {n_in + ng + 2 * k + 1: n_out + ng + k for k in range(len(self.a2as))}

    def scratch(self):
        per = N_DEV - 1
        return [pltpu.SemaphoreType.DMA((per * self.n_jobs,)), pltpu.SemaphoreType.DMA((per * self.n_jobs,)),
                pltpu.SemaphoreType.DMA((self.n_jobs,))]

    def _copies(self, in_refs, out_refs, sems, with_recvs):
        send_sems, recv_sems, local_sems = sems
        _, me = _peer(0)
        ng = len(self.gathers)
        local, sends, recvs = [], [], []
        for job in range(self.n_jobs):
            if job < ng:
                src_of = lambda pid, r=in_refs[job]: r
                dst_of = lambda pid, r=out_refs[job]: r.at[pid]
            else:
                k = job - ng
                stage = self.a2as[k][2]
                src_of = lambda pid, r=in_refs[ng + 2 * k]: r.at[pid]
                dst_of = lambda pid, r=out_refs[job], st=stage: r.at[pid, st]
            local.append(pltpu.make_async_copy(src_of(me), dst_of(me), local_sems.at[job]))
            for mask in range(1, N_DEV):
                peer, pid = _peer(mask)
                idx = job * (N_DEV - 1) + mask - 1
                sends.append(pltpu.make_async_remote_copy(
                    src_ref=src_of(pid), dst_ref=dst_of(me), send_sem=send_sems.at[idx],
                    recv_sem=recv_sems.at[idx], device_id=peer, device_id_type=MESH))
                if with_recvs:
                    recvs.append(pltpu.make_async_remote_copy(
                        src_ref=src_of(pid), dst_ref=dst_of(pid), send_sem=send_sems.at[idx],
                        recv_sem=recv_sems.at[idx], device_id=peer, device_id_type=MESH))
        return local, sends, recvs

    def start(self, in_refs, out_refs, sems):
        local, sends, _ = self._copies(in_refs, out_refs, sems, False)
        for cp in local + sends:
            cp.start()

    def wait(self, in_refs, out_refs, sems):
        local, sends, recvs = self._copies(in_refs, out_refs, sems, True)
        for cp in recvs:
            cp.wait_recv()
        for cp in sends:
            cp.wait_send()
        for cp in local:
            cp.wait()


def carrier_call(core, *, name, grid, in_specs, out_specs, out_shape, inputs, scratch_shapes=(), aliases=None,
                 exchange=None):
    aliases = dict(aliases or {})
    n_in, n_out, n_sc = len(in_specs), len(out_specs), len(scratch_shapes)
    sem = ("arbitrary",) * len(grid)
    params = pltpu.CompilerParams(dimension_semantics=sem, vmem_limit_bytes=VMEM_LIMIT)
    if exchange is None or exchange.n_jobs == 0:
        outs = pl.pallas_call(core, name=name, grid=grid, in_specs=list(in_specs), out_specs=tuple(out_specs),
                              out_shape=tuple(out_shape), scratch_shapes=list(scratch_shapes),
                              input_output_aliases=aliases, compiler_params=params)(*inputs)
        return list(outs), []
    x_in = exchange.inputs()
    x_out = exchange.out_shapes()
    aliases.update(exchange.aliases(n_in, n_out))

    def body(*refs):
        a = n_in + len(x_in)
        b = a + n_out + len(x_out)
        core_in, job_in = refs[:n_in], refs[n_in:a]
        core_out, job_out = refs[a:a + n_out], refs[a + n_out:b]
        core_sc, job_sc = refs[b:b + n_sc], refs[b + n_sc:]
        first = functools.reduce(lambda p, q: p & q, [pl.program_id(ax) == 0 for ax in range(len(grid))])
        last = functools.reduce(lambda p, q: p & q, [pl.program_id(ax) == g - 1 for ax, g in enumerate(grid)])

        @pl.when(first)
        def _():
            exchange.start(job_in, job_out, job_sc)

        core(*core_in, *core_out, *core_sc)

        @pl.when(last)
        def _():
            exchange.wait(job_in, job_out, job_sc)

    outs = pl.pallas_call(
        body, name=name, grid=grid, in_specs=list(in_specs) + [ANY] * len(x_in),
        out_specs=tuple(out_specs) + (ANY,) * len(x_out), out_shape=tuple(out_shape) + tuple(x_out),
        scratch_shapes=list(scratch_shapes) + exchange.scratch(), input_output_aliases=aliases,
        compiler_params=params)(*inputs, *x_in)
    return list(outs[:n_out]), list(outs[n_out:])


def exchange_only(exchange, name):
    def body(*refs):
        n_in, n_out = len(exchange.inputs()), len(exchange.out_shapes())
        job_in, job_out, sems = refs[:n_in], refs[n_in:n_in + n_out], refs[n_in + n_out:]
        exchange.start(job_in, job_out, sems)
        exchange.wait(job_in, job_out, sems)

    x_in = exchange.inputs()
    outs = pl.pallas_call(
        body, name=name, in_specs=[ANY] * len(x_in), out_specs=(ANY,) * len(exchange.out_shapes()),
        out_shape=tuple(exchange.out_shapes()), scratch_shapes=exchange.scratch(),
        input_output_aliases=exchange.aliases(0, 0))(*x_in)
    return list(outs)


def all_gather(v, name):
    return exchange_only(Exchange(gathers=[v]), name)[0]


def _rms(xf):
    return lax.rsqrt(jnp.mean(xf * xf, axis=-1, keepdims=True) + RMS_EPS)


def _is_ctx(i, tm, s):
    return (i * tm + lax.broadcasted_iota(jnp.int32, (tm, 1), 0)) >= s


def _mod_rows(mod_ref, k, is_ctx):
    return jnp.where(is_ctx, mod_ref[1, k:k + 1, :], mod_ref[0, k:k + 1, :])


def _norm_mod(xf, g, mod_ref, k_shift, k_scale, is_ctx):
    nrm = xf * _rms(xf) * g
    return (nrm * (1.0 + _mod_rows(mod_ref, k_scale, is_ctx)) + _mod_rows(mod_ref, k_shift, is_ctx)).astype(BF16)


def _post(xf, ff, g, mod_ref, k_gate, coef, is_ctx):
    return xf + coef * _mod_rows(mod_ref, k_gate, is_ctx) * (ff * _rms(ff) * g)


def _red_add(red_ref, first, is_ctx, rows):
    @pl.when(first)
    def _():
        red_ref[...] = jnp.zeros_like(red_ref)

    for r, val in enumerate(rows):
        tot = jnp.sum(val, axis=0, keepdims=True)
        ctx = jnp.sum(jnp.where(is_ctx, val, 0.0), axis=0, keepdims=True)
        red_ref[0, r:r + 1, :] += tot - ctx
        red_ref[1, r:r + 1, :] += ctx


def _pre_bwd(xf, dh, dxo, g, mod_ref, k_scale, is_ctx):
    r = _rms(xf)
    xhat = xf * r
    dn = dh * (1.0 + _mod_rows(mod_ref, k_scale, is_ctx))
    dxhat = dn * g
    dx = dxo + r * (dxhat - xhat * jnp.mean(dxhat * xhat, axis=-1, keepdims=True))
    return dx, [dh, dh * (xhat * g), dn * xhat]


def _vec_spec(d):
    return pl.BlockSpec((1, d), lambda *_: (0, 0))


def _mod_whole(d):
    return pl.BlockSpec((2, N_MOD, d), lambda *_: (0, 0, 0))


def _red_whole(d):
    return pl.BlockSpec((2, 8, d), lambda *_: (0, 0, 0))


def _token_tile(n):
    return _pick(n, (768, 640, 512, 384, 256))


def ffn_up(x, g, mod, wgu, s, k0, name, exchange=None):
    n, d = x.shape
    fq = wgu.shape[-1]
    tm = _token_tile(n)

    def core(x_ref, g_ref, mod_ref, w_ref, hb_ref, gu_ref, a_ref, hb_s):
        i, k = pl.program_id(0), pl.program_id(1)

        @pl.when(k == 0)
        def _():
            hb_s[...] = _norm_mod(x_ref[...], g_ref[...], mod_ref, k0, k0 + 1, _is_ctx(i, tm, s))
            hb_ref[...] = hb_s[...]

        h = hb_s[...]
        gg = _dot(h, w_ref[0, 0])
        uu = _dot(h, w_ref[1, 0])
        gu_ref[0, 0] = gg.astype(BF16)
        gu_ref[1, 0] = uu.astype(BF16)
        a_ref[0] = (_silu(gg) * uu).astype(BF16)

    outs, xo = carrier_call(
        core, name=name, grid=(n // tm, 4),
        in_specs=[pl.BlockSpec((tm, d), lambda i, k: (i, 0)), _vec_spec(d), _mod_whole(d),
                  pl.BlockSpec((2, 1, d, fq), lambda i, k: (0, k, 0, 0))],
        out_specs=[pl.BlockSpec((tm, d), lambda i, k: (i, 0)),
                   pl.BlockSpec((2, 1, tm, fq), lambda i, k: (0, k, i, 0)),
                   pl.BlockSpec((1, tm, fq), lambda i, k: (k, i, 0))],
        out_shape=[jax.ShapeDtypeStruct((n, d), BF16), jax.ShapeDtypeStruct((2, 4, n, fq), BF16),
                   jax.ShapeDtypeStruct((4, n, fq), BF16)],
        scratch_shapes=[pltpu.VMEM((tm, d), BF16)], inputs=[x, g, mod, wgu], exchange=exchange)
    return outs, xo


def ffn_down(a4, wd4, x, g, mod, s, k_gate, name, exchange=None):
    n, d = x.shape
    fq = wd4.shape[1]
    tm = _token_tile(n)

    def core(a_ref, w_ref, x_ref, g_ref, mod_ref, f_ref, xo_ref, acc):
        i, k = pl.program_id(0), pl.program_id(1)

        @pl.when(k == 0)
        def _():
            acc[...] = jnp.zeros_like(acc)

        acc[...] += _dot(a_ref[0], w_ref[0])

        @pl.when(k == 3)
        def _():
            ff = acc[...]
            f_ref[...] = ff
            xo_ref[...] = _post(x_ref[...], ff, g_ref[...], mod_ref, k_gate, 0.5, _is_ctx(i, tm, s))

    tile = pl.BlockSpec((tm, d), lambda i, k: (i, 0))
    outs, xo = carrier_call(
        core, name=name, grid=(n // tm, 4),
        in_specs=[pl.BlockSpec((1, tm, fq), lambda i, k: (k, i, 0)), pl.BlockSpec((1, fq, d), lambda i, k: (k, 0, 0)),
                  tile, _vec_spec(d), _mod_whole(d)],
        out_specs=[tile, tile],
        out_shape=[jax.ShapeDtypeStruct((n, d), F32), jax.ShapeDtypeStruct((n, d), F32)],
        scratch_shapes=[pltpu.VMEM((tm, d), F32)], inputs=[a4, wd4, x, g, mod], exchange=exchange)
    return outs, xo


def ffn_da(df, wd4, gu, name, exchange=None):
    n, d = df.shape
    fq = wd4.shape[1]
    tm = _token_tile(n)

    def core(df_ref, w_ref, gu_ref, o_ref):
        da = _dot(df_ref[...], w_ref[0], NT)
        gg = gu_ref[0, 0].astype(F32)
        uu = gu_ref[1, 0].astype(F32)
        o_ref[0, 0] = (da * uu * _dsilu(gg)).astype(BF16)
        o_ref[1, 0] = (da * _silu(gg)).astype(BF16)

    gu_spec = pl.BlockSpec((2, 1, tm, fq), lambda i, k: (0, k, i, 0))
    outs, xo = carrier_call(
        core, name=name, grid=(n // tm, 4),
        in_specs=[pl.BlockSpec((tm, d), lambda i, k: (i, 0)), pl.BlockSpec((1, fq, d), lambda i, k: (k, 0, 0)), gu_spec],
        out_specs=[gu_spec], out_shape=[jax.ShapeDtypeStruct(gu.shape, BF16)],
        inputs=[df, wd4, gu], exchange=exchange)
    return outs[0], xo


def ffn_dh(dgu, wgu, x, dxo, g, mod, s, k0, name, exchange=None):
    n, d = x.shape
    fq = wgu.shape[-1]
    tm = _token_tile(n)

    def core(dgu_ref, w_ref, x_ref, dxo_ref, g_ref, mod_ref, dx_ref, red_ref, acc):
        i, k = pl.program_id(0), pl.program_id(1)

        @pl.when(k == 0)
        def _():
            acc[...] = jnp.zeros_like(acc)

        acc[...] += _dot(dgu_ref[0, 0], w_ref[0, 0], NT) + _dot(dgu_ref[1, 0], w_ref[1, 0], NT)

        @pl.when(k == 3)
        def _():
            is_ctx = _is_ctx(i, tm, s)
            dx, sums = _pre_bwd(x_ref[...], acc[...], dxo_ref[...], g_ref[...], mod_ref, k0 + 1, is_ctx)
            dx_ref[...] = dx
            _red_add(red_ref, i == 0, is_ctx, sums)

    tile = pl.BlockSpec((tm, d), lambda i, k: (i, 0))
    outs, xo = carrier_call(
        core, name=name, grid=(n // tm, 4),
        in_specs=[pl.BlockSpec((2, 1, tm, fq), lambda i, k: (0, k, i, 0)),
                  pl.BlockSpec((2, 1, d, fq), lambda i, k: (0, k, 0, 0)), tile, tile, _vec_spec(d), _mod_whole(d)],
        out_specs=[tile, _red_whole(d)],
        out_shape=[jax.ShapeDtypeStruct((n, d), F32), jax.ShapeDtypeStruct((2, 8, d), F32)],
        scratch_shapes=[pltpu.VMEM((tm, d), F32)], inputs=[dgu, wgu, x, dxo, g, mod], exchange=exchange)
    return outs, xo


def grad_weight(a, b, name, a_lead=None, b_lead=None, b_cols=None):
    n = a.shape[-2]
    ka = a.shape[-1]
    kb = b_cols or b.shape[-1]
    nj = a_lead or b_lead or (b.shape[-1] // b_cols)
    tk = _pick(n, (1408, 1024, 768, 640, 512, 256))
    nk = n // tk

    def core(a_ref, b_ref, o_ref, acc):
        kk = pl.program_id(1)

        @pl.when(kk == 0)
        def _():
            acc[...] = jnp.zeros_like(acc)

        av = a_ref[0] if a_lead else a_ref[...]
        bv = b_ref[0] if b_lead else b_ref[...]
        acc[...] += _dot(av, bv, TN)

        @pl.when(kk == nk - 1)
        def _():
            o_ref[0] = acc[...].astype(BF16)

    a_spec = (pl.BlockSpec((1, tk, ka), lambda j, kk: (j, kk, 0)) if a_lead
              else pl.BlockSpec((tk, ka), lambda j, kk: (kk, 0)))
    if b_lead:
        b_spec = pl.BlockSpec((1, tk, kb), lambda j, kk: (j, kk, 0))
    elif b_cols:
        b_spec = pl.BlockSpec((tk, kb), lambda j, kk: (kk, j))
    else:
        b_spec = pl.BlockSpec((tk, kb), lambda j, kk: (kk, 0))
    outs, _ = carrier_call(
        core, name=name, grid=(nj, nk), in_specs=[a_spec, b_spec],
        out_specs=[pl.BlockSpec((1, ka, kb), lambda j, kk: (j, 0, 0))],
        out_shape=[jax.ShapeDtypeStruct((nj, ka, kb), BF16)],
        scratch_shapes=[pltpu.VMEM((ka, kb), F32)], inputs=[a, b])
    return outs[0]


def post_bwd(f, dxo, g, mod, k_gate, coef, s, name):
    n, d = f.shape

    def core(f_ref, dxo_ref, g_ref, mod_ref, df_ref, red_ref):
        i = pl.program_id(0)
        is_ctx = _is_ctx(i, TM, s)
        ff = f_ref[...]
        dxo_ = dxo_ref[...]
        gg = g_ref[...]
        r = _rms(ff)
        fn = ff * r
        dy = (coef * _mod_rows(mod_ref, k_gate, is_ctx)) * dxo_
        dfn = dy * gg
        df_ref[...] = (r * (dfn - fn * jnp.mean(dfn * fn, axis=-1, keepdims=True))).astype(BF16)
        _red_add(red_ref, i == 0, is_ctx, [coef * (fn * gg) * dxo_, dy * fn])

    tile = pl.BlockSpec((TM, d), lambda i: (i, 0))
    outs, _ = carrier_call(
        core, name=name, grid=(n // TM,), in_specs=[tile, tile, _vec_spec(d), _mod_whole(d)],
        out_specs=[tile, _red_whole(d)],
        out_shape=[jax.ShapeDtypeStruct((n, d), BF16), jax.ShapeDtypeStruct((2, 8, d), F32)],
        inputs=[f, dxo, g, mod])
    return outs


def matmul_nt(a, b, name):
    m, k = a.shape
    n = b.shape[0]
    tm = _token_tile(m)

    def core(a_ref, b_ref, o_ref):
        o_ref[...] = _dot(a_ref[...], b_ref[...], NT)

    outs, _ = carrier_call(
        core, name=name, grid=(m // tm,),
        in_specs=[pl.BlockSpec((tm, k), lambda i: (i, 0)), pl.BlockSpec((n, k), lambda i: (0, 0))],
        out_specs=[pl.BlockSpec((tm, n), lambda i: (i, 0))], out_shape=[jax.ShapeDtypeStruct((m, n), F32)],
        inputs=[a, b])
    return outs[0]


def _rope_tables(s, n):
    t = jnp.arange(n)
    lane = jnp.arange(LANES)
    dd = lane % HEAD_DIM
    inv = ROPE_THETA ** (-(dd % ROPE_PAIRS).astype(F32) / ROPE_PAIRS)
    pos = jnp.where(dd[None, :] < HEAD_DIM // 2, (t // GRID_W)[:, None], (t % GRID_W)[:, None]).astype(F32)
    ang = pos * inv[None, :]
    live = (t < s)[:, None]
    first = ((dd % (2 * ROPE_PAIRS)) < ROPE_PAIRS)[None, :]
    cos = jnp.where(live, jnp.cos(ang), 1.0)
    sin = jnp.where(live, jnp.sin(ang), 0.0)
    sa = jnp.where(first, -sin, 0.0)
    sb = jnp.where(first, 0.0, sin)
    return cos.astype(F32), sa.astype(F32), sb.astype(F32)


def _rope(xv, cos, sa, sb):
    return (xv * cos + pltpu.roll(xv, LANES - ROPE_PAIRS, 1) * sa + pltpu.roll(xv, ROPE_PAIRS, 1) * sb)


def mix_in(x, g, mod, win8, tables, s, name):
    n, d = x.shape
    tm = _token_tile(n)
    nb = win8.shape[0]
    n_rope = 2 * NA_WIDTH // IN_BLOCK

    def core(x_ref, g_ref, mod_ref, w_ref, c_ref, sa_ref, sb_ref, hb_ref, qkv_ref, u_ref):
        i = pl.program_id(0)
        hb = _norm_mod(x_ref[...], g_ref[...], mod_ref, 3, 4, _is_ctx(i, tm, s))
        hb_ref[...] = hb
        cos, sa, sb = c_ref[...], sa_ref[...], sb_ref[...]
        for j in range(nb):
            y = _dot(hb, w_ref[j])
            if j < n_rope:
                for b in range(IN_BLOCK // LANES):
                    sl = slice(b * LANES, (b + 1) * LANES)
                    qkv_ref[:, j * IN_BLOCK + b * LANES:j * IN_BLOCK + (b + 1) * LANES] = (
                        _rope(y[:, sl], cos, sa, sb).astype(BF16))
            elif j < N_QKV_BLOCKS:
                qkv_ref[:, j * IN_BLOCK:(j + 1) * IN_BLOCK] = y.astype(BF16)
            else:
                u_ref[:, (j - N_QKV_BLOCKS) * IN_BLOCK:(j - N_QKV_BLOCKS + 1) * IN_BLOCK] = y

    tab = pl.BlockSpec((tm, LANES), lambda i: (i, 0))
    row = lambda w: pl.BlockSpec((tm, w), lambda i: (i, 0))
    outs, _ = carrier_call(
        core, name=name, grid=(n // tm,),
        in_specs=[row(d), _vec_spec(d), _mod_whole(d), pl.BlockSpec((nb, d, IN_BLOCK), lambda i: (0, 0, 0)),
                  tab, tab, tab],
        out_specs=[row(d), row(3 * NA_WIDTH), row(POOL_WIDTH)],
        out_shape=[jax.ShapeDtypeStruct((n, d), BF16), jax.ShapeDtypeStruct((n, 3 * NA_WIDTH), BF16),
                   jax.ShapeDtypeStruct((n, POOL_WIDTH), F32)],
        inputs=[x, g, mod, win8, *tables])
    return outs


def qkv_bwd(dq, dk, dv, du, tables, name):
    n = dq.shape[0]
    w = NA_WIDTH

    def core(dq_ref, dk_ref, dv_ref, du_ref, c_ref, sa_ref, sb_ref, o_ref):
        cos, sa, sb = c_ref[...], -sa_ref[...], -sb_ref[...]
        for b in range(N_HEAD_BLOCKS):
            sl = slice(b * LANES, (b + 1) * LANES)
            o_ref[:, b * LANES:(b + 1) * LANES] = _rope(dq_ref[:, sl], cos, sa, sb).astype(BF16)
            o_ref[:, w + b * LANES:w + (b + 1) * LANES] = _rope(dk_ref[:, sl], cos, sa, sb).astype(BF16)
        o_ref[:, 2 * w:3 * w] = dv_ref[...].astype(BF16)
        o_ref[:, 3 * w:] = du_ref[...].astype(BF16)

    tab = pl.BlockSpec((TM, LANES), lambda i: (i, 0))
    tile = pl.BlockSpec((TM, w), lambda i: (i, 0))
    outs, _ = carrier_call(
        core, name=name, grid=(n // TM,), in_specs=[tile, tile, tile, tile, tab, tab, tab],
        out_specs=[pl.BlockSpec((TM, 4 * w), lambda i: (i, 0))],
        out_shape=[jax.ShapeDtypeStruct((n, 4 * w), BF16)], inputs=[dq, dk, dv, du, *tables])
    return outs[0]


def mix_out(na, py, wout, x, g, mod, s, name):
    n, d = x.shape
    tm = _token_tile(n)
    half = na.shape[1]

    def core(na_ref, py_ref, w_ref, x_ref, g_ref, mod_ref, f_ref, xo_ref):
        i = pl.program_id(0)
        ff = _dot(na_ref[...], w_ref[:half, :]) + _dot(py_ref[...], w_ref[half:, :])
        f_ref[...] = ff
        xo_ref[...] = _post(x_ref[...], ff, g_ref[...], mod_ref, 5, 1.0, _is_ctx(i, tm, s))

    tile = pl.BlockSpec((tm, d), lambda i: (i, 0))
    htile = pl.BlockSpec((tm, half), lambda i: (i, 0))
    outs, _ = carrier_call(
        core, name=name, grid=(n // tm,),
        in_specs=[htile, htile, pl.BlockSpec((2 * half, d), lambda i: (0, 0)), tile, _vec_spec(d), _mod_whole(d)],
        out_specs=[tile, tile],
        out_shape=[jax.ShapeDtypeStruct((n, d), F32), jax.ShapeDtypeStruct((n, d), F32)],
        inputs=[na, py, wout, x, g, mod])
    return outs


def grad_wout(na, py, dfm, name):
    n, half = na.shape
    d = dfm.shape[1]
    tk = _pick(n, (1408, 1024, 768, 640, 512, 256))
    nk = n // tk

    def core(na_ref, py_ref, b_ref, o_ref, acc):
        hh, kk = pl.program_id(0), pl.program_id(1)

        @pl.when(kk == 0)
        def _():
            acc[...] = jnp.zeros_like(acc)

        @pl.when(hh == 0)
        def _():
            acc[...] += _dot(na_ref[...], b_ref[...], TN)

        @pl.when(hh == 1)
        def _():
            acc[...] += _dot(py_ref[...], b_ref[...], TN)

        @pl.when(kk == nk - 1)
        def _():
            o_ref[0] = acc[...].astype(BF16)

    htile = pl.BlockSpec((tk, half), lambda hh, kk: (kk, 0))
    outs, _ = carrier_call(
        core, name=name, grid=(2, nk), in_specs=[htile, htile, pl.BlockSpec((tk, d), lambda hh, kk: (kk, 0))],
        out_specs=[pl.BlockSpec((1, half, d), lambda hh, kk: (hh, 0, 0))],
        out_shape=[jax.ShapeDtypeStruct((2, half, d), BF16)],
        scratch_shapes=[pltpu.VMEM((half, d), F32)], inputs=[na, py, dfm])
    return outs[0]


def mix_dh(dqkvu, win8, x, dxo, g, mod, s, name, exchange=None):
    n, d = x.shape
    tm = _token_tile(n)
    nb = win8.shape[0]

    def core(dq_ref, w_ref, x_ref, dxo_ref, g_ref, mod_ref, dx_ref, red_ref):
        i = pl.program_id(0)
        dh = _dot(dq_ref[:, :IN_BLOCK], w_ref[0], NT)
        for j in range(1, nb):
            dh = dh + _dot(dq_ref[:, j * IN_BLOCK:(j + 1) * IN_BLOCK], w_ref[j], NT)
        is_ctx = _is_ctx(i, tm, s)
        dx, sums = _pre_bwd(x_ref[...], dh, dxo_ref[...], g_ref[...], mod_ref, 4, is_ctx)
        dx_ref[...] = dx
        _red_add(red_ref, i == 0, is_ctx, sums)

    tile = pl.BlockSpec((tm, d), lambda i: (i, 0))
    outs, xo = carrier_call(
        core, name=name, grid=(n // tm,),
        in_specs=[pl.BlockSpec((tm, nb * IN_BLOCK), lambda i: (i, 0)),
                  pl.BlockSpec((nb, d, IN_BLOCK), lambda i: (0, 0, 0)), tile, tile, _vec_spec(d), _mod_whole(d)],
        out_specs=[tile, _red_whole(d)],
        out_shape=[jax.ShapeDtypeStruct((n, d), F32), jax.ShapeDtypeStruct((2, 8, d), F32)],
        inputs=[dqkvu, win8, x, dxo, g, mod], exchange=exchange)
    return outs, xo


def _na_consts():
    j = np.arange(GRID_W)
    col_start = np.clip(j - NA_KW // 2, 0, GRID_W - NA_KW)
    valid = (j[None, :] >= col_start[:, None]) & (j[None, :] < col_start[:, None] + NA_KW)
    dc = np.clip(j[None, :] - j[:, None] + NA_KW - 1, 0, 2 * NA_KW - 2)
    onehot = np.zeros((LANES, GRID_W, GRID_W), np.float32)
    for d in range(2 * NA_KW - 1):
        onehot[d] = ((dc == d) & valid).astype(np.float32)
    negmask = np.where(valid, 0.0, NEG_INF).astype(np.float32)
    return onehot.reshape(LANES, GRID_W * GRID_W), np.tile(negmask, (1, NA_KH))


def bias_tables(rpb, name):
    onehot, negmask = _na_consts()
    nj = 2 * NA_KH - 1
    rows = NA_HEADS * nj
    a = jnp.pad(rpb.reshape(rows, 2 * NA_KW - 1), ((0, 0), (0, LANES - (2 * NA_KW - 1))))

    def body(a_ref, e_ref, o_ref):
        o_ref[...] = _dot(a_ref[...], e_ref[...], precision=HIGHEST)

    t = pl.pallas_call(
        body, name=name, out_shape=jax.ShapeDtypeStruct((rows, GRID_W * GRID_W), F32),
        in_specs=[pl.BlockSpec(memory_space=pltpu.VMEM)] * 2,
        out_specs=pl.BlockSpec(memory_space=pltpu.VMEM),
    )(a, jnp.asarray(onehot))
    t = t.reshape(NA_HEADS, nj, GRID_W, GRID_W)
    tb = jnp.stack([t[:, j0:j0 + NA_KH] for j0 in range(NA_KH)])
    tb = tb.transpose(0, 1, 3, 2, 4).reshape(NA_KH, NA_HEADS, GRID_W, NA_KH * GRID_W)
    return tb + jnp.asarray(negmask)[None, None]


def bias_tables_bwd(dtb, name):
    onehot, _ = _na_consts()
    nj = 2 * NA_KH - 1
    d5 = dtb.reshape(NA_KH, NA_HEADS, GRID_W, NA_KH, GRID_W).transpose(0, 3, 1, 2, 4)
    d2 = d5.reshape(NA_KH * NA_KH * NA_HEADS, GRID_W * GRID_W)

    def body(d_ref, e_ref, o_ref):
        r = _dot(d_ref[...], e_ref[...], NT, precision=HIGHEST)
        for j in range(nj):
            acc = jnp.zeros((NA_HEADS, LANES), F32)
            for j0 in range(NA_KH):
                kk = j - j0
                if 0 <= kk < NA_KH:
                    base = (j0 * NA_KH + kk) * NA_HEADS
                    acc = acc + r[base:base + NA_HEADS, :]
            o_ref[j] = acc

    out = pl.pallas_call(
        body, name=name, out_shape=jax.ShapeDtypeStruct((nj, NA_HEADS, LANES), F32),
        in_specs=[pl.BlockSpec(memory_space=pltpu.VMEM)] * 2,
        out_specs=pl.BlockSpec(memory_space=pltpu.VMEM),
        compiler_params=pltpu.CompilerParams(vmem_limit_bytes=VMEM_LIMIT),
    )(d2, jnp.asarray(onehot))
    return out[:, :, :2 * NA_KW - 1].transpose(1, 0, 2)


def _head_masks():
    lane = lax.broadcasted_iota(jnp.int32, (1, LANES), 1)
    return [(lane >= h * HEAD_DIM) & (lane < (h + 1) * HEAD_DIM) for h in range(HEADS_PER_BLOCK)]


def _row_window(r, rows):
    rs = jnp.clip(r - NA_KH // 2, 0, rows - NA_KH)
    return rs - r + NA_KH - 1, pl.multiple_of(rs * GRID_W, GRID_W)


def _stack_heads(t, masks):
    return jnp.concatenate([jnp.where(mk, t, jnp.zeros_like(t)) for mk in masks], axis=0)


def _unstack_heads(t2, masks):
    out = t2[(HEADS_PER_BLOCK - 1) * GRID_W:, :]
    for h in reversed(range(HEADS_PER_BLOCK - 1)):
        out = jnp.where(masks[h], t2[h * GRID_W:(h + 1) * GRID_W, :], out)
    return out


NA_ROWS_PER_STEP = 4
NA_STEP = NA_ROWS_PER_STEP * GRID_W
SLAB = NA_KH * GRID_W
K_COL = N_HEAD_BLOCKS
V_COL = 2 * N_HEAD_BLOCKS


def na_fwd(qkv, tb, s, name, exchange=None):
    n = qkv.shape[0]
    l = n - s
    rows = s // GRID_W
    rr = NA_ROWS_PER_STEP
    x_steps = rows // rr

    def core(q_ref, k_ref, v_ref, kc_ref, vc_ref, tb_ref, o_ref, lse_ref):
        rb = pl.program_id(1)

        @pl.when(rb >= x_steps)
        def _():
            o_ref[...] = jnp.zeros_like(o_ref)
            lse_ref[...] = jnp.zeros_like(lse_ref)

        @pl.when(rb < x_steps)
        def _():
            masks = _head_masks()
            kcb, vcb = kc_ref[...], vc_ref[...]
            wins, scores = [], []
            for t in range(rr):
                j0, off = _row_window(rb * rr + t, rows)
                q2 = _stack_heads(q_ref[t * GRID_W:(t + 1) * GRID_W, :], masks)
                bias = tb_ref[j0].reshape(HEADS_PER_BLOCK * GRID_W, SLAB)
                s_loc = _dot(q2, k_ref[pl.ds(off, SLAB), :], NT) * ATT_SCALE + bias
                s_ctx = _dot(q2, kcb, NT) * ATT_SCALE
                wins.append(off)
                scores.append((s_loc, s_ctx))
            probs = []
            for s_loc, s_ctx in scores:
                m = jnp.maximum(jnp.max(s_loc, axis=-1, keepdims=True), jnp.max(s_ctx, axis=-1, keepdims=True))
                p_loc = jnp.exp(s_loc - m)
                p_ctx = jnp.exp(s_ctx - m)
                den = jnp.sum(p_loc, axis=-1, keepdims=True) + jnp.sum(p_ctx, axis=-1, keepdims=True)
                probs.append((p_loc.astype(BF16), p_ctx.astype(BF16), den, m + jnp.log(den)))
            for t, (p_loc, p_ctx, den, lse2) in enumerate(probs):
                o2 = (_dot(p_loc, v_ref[pl.ds(wins[t], SLAB), :]) + _dot(p_ctx, vcb)) / den
                o_ref[t * GRID_W:(t + 1) * GRID_W, :] = _unstack_heads(o2, masks).astype(BF16)
                lse_ref[0, t * GRID_W:(t + 1) * GRID_W, :] = _unstack_heads(lse2, masks)

    cb = s // l
    outs, xo = carrier_call(
        core, name=name, grid=(N_HEAD_BLOCKS, n // NA_STEP),
        in_specs=[pl.BlockSpec((NA_STEP, LANES), lambda hb, rb: (jnp.minimum(rb, x_steps - 1), hb)),
                  pl.BlockSpec((s, LANES), lambda hb, rb: (0, K_COL + hb)),
                  pl.BlockSpec((s, LANES), lambda hb, rb: (0, V_COL + hb)),
                  pl.BlockSpec((l, LANES), lambda hb, rb: (cb, K_COL + hb)),
                  pl.BlockSpec((l, LANES), lambda hb, rb: (cb, V_COL + hb)),
                  pl.BlockSpec((NA_KH, HEADS_PER_BLOCK, GRID_W, SLAB), lambda hb, rb: (0, hb, 0, 0))],
        out_specs=[pl.BlockSpec((NA_STEP, LANES), lambda hb, rb: (rb, hb)),
                   pl.BlockSpec((1, NA_STEP, LANES), lambda hb, rb: (hb, rb, 0))],
        out_shape=[jax.ShapeDtypeStruct((n, NA_WIDTH), BF16), jax.ShapeDtypeStruct((N_HEAD_BLOCKS, n, LANES), F32)],
        inputs=[qkv, qkv, qkv, qkv, qkv, tb], exchange=exchange)
    return outs, xo


def na_bwd(qkv, tb, o, dmix, lse, s, name, exchange=None):
    n = qkv.shape[0]
    l = n - s
    rows = s // GRID_W
    rr = NA_ROWS_PER_STEP
    x_steps = rows // rr

    def core(q_ref, k_ref, v_ref, kc_ref, vc_ref, tb_ref, o_ref, do_ref, lse_ref, dq_ref, dk_ref, dv_ref, dtb_ref):
        rb = pl.program_id(1)

        @pl.when(rb == 0)
        def _():
            dk_ref[...] = jnp.zeros_like(dk_ref)
            dv_ref[...] = jnp.zeros_like(dv_ref)
            dtb_ref[...] = jnp.zeros_like(dtb_ref)

        @pl.when(rb >= x_steps)
        def _():
            dq_ref[...] = jnp.zeros_like(dq_ref)

        @pl.when(rb < x_steps)
        def _():
            masks = _head_masks()
            kcb, vcb = kc_ref[...], vc_ref[...]
            stage1 = []
            for t in range(rr):
                j0, off = _row_window(rb * rr + t, rows)
                sl = slice(t * GRID_W, (t + 1) * GRID_W)
                q2 = _stack_heads(q_ref[sl, :], masks)
                do_f = do_ref[sl, :]
                do2 = _stack_heads(do_f.astype(BF16), masks)
                dd = do_f * o_ref[sl, :].astype(F32)
                delta2 = jnp.concatenate(
                    [jnp.sum(jnp.where(mk, dd, 0.0), axis=-1, keepdims=True) for mk in masks], axis=0)
                lse_t = lse_ref[0, sl, :]
                lse2 = jnp.concatenate(
                    [lse_t[:, h * HEAD_DIM:h * HEAD_DIM + 1] for h in range(HEADS_PER_BLOCK)], axis=0)
                kslab = k_ref[pl.ds(off, SLAB), :]
                vslab = v_ref[pl.ds(off, SLAB), :]
                bias = tb_ref[j0].reshape(HEADS_PER_BLOCK * GRID_W, SLAB)
                s_loc = _dot(q2, kslab, NT) * ATT_SCALE + bias - lse2
                s_ctx = _dot(q2, kcb, NT) * ATT_SCALE - lse2
                dp_loc = _dot(do2, vslab, NT) - delta2
                dp_ctx = _dot(do2, vcb, NT) - delta2
                stage1.append((j0, off, q2, do2, s_loc, s_ctx, dp_loc, dp_ctx))
            stage2 = []
            for j0, off, q2, do2, s_loc, s_ctx, dp_loc, dp_ctx in stage1:
                p_loc = jnp.exp(s_loc)
                p_ctx = jnp.exp(s_ctx)
                ds_loc = p_loc * dp_loc
                dtb_ref[j0] += ds_loc.reshape(HEADS_PER_BLOCK, GRID_W, SLAB)
                stage2.append((off, q2, do2, p_loc.astype(BF16), p_ctx.astype(BF16), ds_loc.astype(BF16),
                               (p_ctx * dp_ctx).astype(BF16)))
            for t, (off, q2, do2, p_loc, p_ctx, ds_loc, ds_ctx) in enumerate(stage2):
                dq2 = (_dot(ds_loc, k_ref[pl.ds(off, SLAB), :]) + _dot(ds_ctx, kcb)) * ATT_SCALE
                dq_ref[t * GRID_W:(t + 1) * GRID_W, :] = _unstack_heads(dq2, masks)
                dk_ref[pl.ds(off, SLAB), :] += _dot(ds_loc, q2, TN) * ATT_SCALE
                dv_ref[pl.ds(off, SLAB), :] += _dot(p_loc, do2, TN)
                dk_ref[s:, :] += _dot(ds_ctx, q2, TN) * ATT_SCALE
                dv_ref[s:, :] += _dot(p_ctx, do2, TN)

    cb = s // l
    clamp = lambda hb, rb: (jnp.minimum(rb, x_steps - 1), hb)
    tile_in = pl.BlockSpec((NA_STEP, LANES), clamp)
    whole_out = pl.BlockSpec((n, LANES), lambda hb, rb: (0, hb))
    tbs = pl.BlockSpec((NA_KH, HEADS_PER_BLOCK, GRID_W, SLAB), lambda hb, rb: (0, hb, 0, 0))
    f32n = jax.ShapeDtypeStruct((n, NA_WIDTH), F32)
    outs, xo = carrier_call(
        core, name=name, grid=(N_HEAD_BLOCKS, n // NA_STEP),
        in_specs=[tile_in,
                  pl.BlockSpec((s, LANES), lambda hb, rb: (0, K_COL + hb)),
                  pl.BlockSpec((s, LANES), lambda hb, rb: (0, V_COL + hb)),
                  pl.BlockSpec((l, LANES), lambda hb, rb: (cb, K_COL + hb)),
                  pl.BlockSpec((l, LANES), lambda hb, rb: (cb, V_COL + hb)),
                  tbs, tile_in, tile_in,
                  pl.BlockSpec((1, NA_STEP, LANES), lambda hb, rb: (hb, jnp.minimum(rb, x_steps - 1), 0))],
        out_specs=[pl.BlockSpec((NA_STEP, LANES), lambda hb, rb: (rb, hb)), whole_out, whole_out, tbs],
        out_shape=[f32n, f32n, f32n, jax.ShapeDtypeStruct((NA_KH, NA_HEADS, GRID_W, SLAB), F32)],
        inputs=[qkv, qkv, qkv, qkv, qkv, tb, o, dmix, lse], exchange=exchange)
    return outs, xo


def ctx_attn_fwd(qkv, na, s, name):
    n = qkv.shape[0]
    l = n - s
    cb = s // l

    def core(q_ref, k_ref, v_ref, na_in, o_ref, lse_ref):
        masks = _head_masks()
        qt, kb, vb = q_ref[...], k_ref[...], v_ref[...]
        o_acc = jnp.zeros((l, LANES), F32)
        lse_acc = jnp.zeros((l, LANES), F32)
        for h in range(HEADS_PER_BLOCK):
            qh = jnp.where(masks[h], qt, jnp.zeros_like(qt))
            sc = _dot(qh, kb, NT) * ATT_SCALE
            m = jnp.max(sc, axis=-1, keepdims=True)
            p = jnp.exp(sc - m)
            den = jnp.sum(p, axis=-1, keepdims=True)
            o_acc = jnp.where(masks[h], _dot(p.astype(BF16), vb) / den, o_acc)
            lse_acc = jnp.where(masks[h], m + jnp.log(den), lse_acc)
        o_ref[...] = o_acc.astype(BF16)
        lse_ref[0] = lse_acc

    outs, _ = carrier_call(
        core, name=name, grid=(N_HEAD_BLOCKS,),
        in_specs=[pl.BlockSpec((l, LANES), lambda hb: (cb, hb)), pl.BlockSpec((l, LANES), lambda hb: (cb, K_COL + hb)),
                  pl.BlockSpec((l, LANES), lambda hb: (cb, V_COL + hb)), ANY],
        out_specs=[pl.BlockSpec((l, LANES), lambda hb: (cb, hb)), pl.BlockSpec((1, l, LANES), lambda hb: (hb, 0, 0))],
        out_shape=[jax.ShapeDtypeStruct(na.shape, BF16), jax.ShapeDtypeStruct((N_HEAD_BLOCKS, l, LANES), F32)],
        inputs=[qkv, qkv, qkv, na], aliases={3: 0})
    return outs


def ctx_attn_bwd(qkv, na, dmix, lse, dq, dk, dv, s, name):
    n = qkv.shape[0]
    l = n - s
    cb = s // l

    def core(q_ref, k_ref, v_ref, o_ref, do_ref, lse_ref, dq_in, dk_in, dv_in, dq_ref, dk_ref, dv_ref):
        masks = _head_masks()
        qt, kb, vb = q_ref[...], k_ref[...], v_ref[...]
        do_f = do_ref[...]
        dd = do_f * o_ref[...].astype(F32)
        do_b = do_f.astype(BF16)
        lse_t = lse_ref[0]
        dq_acc = jnp.zeros((l, LANES), F32)
        dk_acc = jnp.zeros((l, LANES), F32)
        dv_acc = jnp.zeros((l, LANES), F32)
        for h in range(HEADS_PER_BLOCK):
            qh = jnp.where(masks[h], qt, jnp.zeros_like(qt))
            doh = jnp.where(masks[h], do_b, jnp.zeros_like(do_b))
            delta = jnp.sum(jnp.where(masks[h], dd, 0.0), axis=-1, keepdims=True)
            p = jnp.exp(_dot(qh, kb, NT) * ATT_SCALE - lse_t[:, h * HEAD_DIM:h * HEAD_DIM + 1])
            ds = (p * (_dot(doh, vb, NT) - delta)).astype(BF16)
            dq_acc = jnp.where(masks[h], _dot(ds, kb) * ATT_SCALE, dq_acc)
            dk_acc = dk_acc + _dot(ds, qh, TN)
            dv_acc = dv_acc + _dot(p.astype(BF16), doh, TN)
        dq_ref[...] = dq_acc
        dk_ref[...] = dk_in[...] + dk_acc * ATT_SCALE
        dv_ref[...] = dv_in[...] + dv_acc

    blk = pl.BlockSpec((l, LANES), lambda hb: (cb, hb))
    f32n = jax.ShapeDtypeStruct((n, NA_WIDTH), F32)
    outs, _ = carrier_call(
        core, name=name, grid=(N_HEAD_BLOCKS,),
        in_specs=[blk, pl.BlockSpec((l, LANES), lambda hb: (cb, K_COL + hb)),
                  pl.BlockSpec((l, LANES), lambda hb: (cb, V_COL + hb)), blk, blk,
                  pl.BlockSpec((1, l, LANES), lambda hb: (hb, 0, 0)), ANY, blk, blk],
        out_specs=[blk, blk, blk], out_shape=[f32n, f32n, f32n],
        inputs=[qkv, qkv, qkv, na, dmix, lse, dq, dk, dv], aliases={6: 0, 7: 1, 8: 2})
    return outs


def _pool_consts(l):
    assert l == TM
    mem = np.zeros((2, POOL_GROUPS, TM, TM), np.float32)
    inv = np.zeros((2, POOL_GROUPS, TM, LANES), np.float32)
    for which, length in ((0, GRID_W), (1, l)):
        t = np.arange(length)
        for g, w in enumerate(POOL_WINDOWS):
            lo = np.clip(t - w // 2, 0, length)
            hi = np.clip(t - w // 2 + w, 0, length)
            blockm = ((t[None, :] >= lo[:, None]) & (t[None, :] < hi[:, None])).astype(np.float32)
            cnt = (hi - lo).astype(np.float32)
            for b in range(TM // length):
                mem[which, g, b * length:(b + 1) * length, b * length:(b + 1) * length] = blockm
                inv[which, g, b * length:(b + 1) * length, :] = (1.0 / cnt)[:, None]
    return mem, np.ascontiguousarray(mem.transpose(0, 1, 3, 2)), inv


def _split_dot(m01, val):
    hi = val.astype(BF16)
    lo = (val - hi.astype(F32)).astype(BF16)
    return _dot(m01, hi) + _dot(m01, lo)


def pool_fwd(u, mem, inv, wp, scale, nx_tiles, name):
    n = u.shape[0]

    def core(u_ref, m_ref, i_ref, wp_ref, s_ref, o_ref):
        for g in range(POOL_GROUPS):
            sl = slice(g * POOL_CH, (g + 1) * POOL_CH)
            ug = u_ref[:, sl]
            dg = _split_dot(m_ref[0, g], ug) * i_ref[0, g] - ug
            o_ref[:, sl] = (_dot(dg.astype(BF16), wp_ref[g]) * s_ref[:, sl]).astype(BF16)

    grp = lambda i: (i // nx_tiles, 0, 0, 0)
    outs, _ = carrier_call(
        core, name=name, grid=(n // TM,),
        in_specs=[pl.BlockSpec((TM, POOL_WIDTH), lambda i: (i, 0)),
                  pl.BlockSpec((1, POOL_GROUPS, TM, TM), grp),
                  pl.BlockSpec((1, POOL_GROUPS, TM, LANES), grp),
                  pl.BlockSpec((POOL_GROUPS, POOL_CH, POOL_CH), lambda i: (0, 0, 0)),
                  pl.BlockSpec((1, POOL_WIDTH), lambda i: (0, 0))],
        out_specs=[pl.BlockSpec((TM, POOL_WIDTH), lambda i: (i, 0))],
        out_shape=[jax.ShapeDtypeStruct((n, POOL_WIDTH), BF16)], inputs=[u, mem, inv, wp, scale])
    return outs[0]


def pool_bwd(dmix, u, mem, mem_t, inv, wp, scale, nx_tiles, name):
    n = u.shape[0]

    def core(dy_ref, u_ref, m_ref, mt_ref, i_ref, wp_ref, s_ref, du_ref, dwp_ref, dsc_ref):
        @pl.when(pl.program_id(0) == 0)
        def _():
            dwp_ref[...] = jnp.zeros_like(dwp_ref)
            dsc_ref[...] = jnp.zeros_like(dsc_ref)

        for g in range(POOL_GROUPS):
            sl = slice(g * POOL_CH, (g + 1) * POOL_CH)
            ug = u_ref[:, sl]
            dy = dy_ref[:, sl]
            dg = (_split_dot(m_ref[0, g], ug) * i_ref[0, g] - ug).astype(BF16)
            z = _dot(dg, wp_ref[g])
            dsc_ref[0:1, sl] += jnp.sum(dy * z, axis=0, keepdims=True)
            dz = (dy * s_ref[:, sl]).astype(BF16)
            dwp_ref[g] += _dot(dg, dz, TN)
            dd = _dot(dz, wp_ref[g], NT)
            du_ref[:, sl] = _split_dot(mt_ref[0, g], dd * i_ref[0, g]) - dd

    grp = lambda i: (i // nx_tiles, 0, 0, 0)
    outs, _ = carrier_call(
        core, name=name, grid=(n // TM,),
        in_specs=[pl.BlockSpec((TM, POOL_WIDTH), lambda i: (i, 1)),
                  pl.BlockSpec((TM, POOL_WIDTH), lambda i: (i, 0)),
                  pl.BlockSpec((1, POOL_GROUPS, TM, TM), grp),
                  pl.BlockSpec((1, POOL_GROUPS, TM, TM), grp),
                  pl.BlockSpec((1, POOL_GROUPS, TM, LANES), grp),
                  pl.BlockSpec((POOL_GROUPS, POOL_CH, POOL_CH), lambda i: (0, 0, 0)),
                  pl.BlockSpec((1, POOL_WIDTH), lambda i: (0, 0))],
        out_specs=[pl.BlockSpec((TM, POOL_WIDTH), lambda i: (i, 0)),
                   pl.BlockSpec((POOL_GROUPS, POOL_CH, POOL_CH), lambda i: (0, 0, 0)),
                   pl.BlockSpec((8, POOL_WIDTH), lambda i: (0, 0))],
        out_shape=[jax.ShapeDtypeStruct((n, POOL_WIDTH), F32),
                   jax.ShapeDtypeStruct((POOL_GROUPS, POOL_CH, POOL_CH), F32),
                   jax.ShapeDtypeStruct((8, POOL_WIDTH), F32)],
        inputs=[dmix, u, mem, mem_t, inv, wp, scale])
    return outs


MOD_ROWS = 16


def mod_fwd(cvecs, w, b, name):
    _, d = cvecs.shape
    cl = w.shape[2]
    tc = _pick(cl, (384, 128))

    def core(c_ref, w_ref, b_ref, o_ref):
        a = _silu(c_ref[...]).astype(BF16)
        o_ref[0] = _dot(a, w_ref[0].astype(BF16)) + b_ref[0]

    outs, _ = carrier_call(
        core, name=name, grid=(DEPTH, cl // tc),
        in_specs=[pl.BlockSpec((MOD_ROWS, d), lambda li, j: (0, 0)),
                  pl.BlockSpec((1, d, tc), lambda li, j: (li, 0, j)),
                  pl.BlockSpec((1, 1, tc), lambda li, j: (li, 0, j))],
        out_specs=[pl.BlockSpec((1, MOD_ROWS, tc), lambda li, j: (li, 0, j))],
        out_shape=[jax.ShapeDtypeStruct((DEPTH, MOD_ROWS, cl), F32)], inputs=[cvecs, w, b])
    return outs[0]


def mod_bwd(cvecs, dm, w, name):
    _, d = cvecs.shape
    cl = w.shape[2]
    tc = _pick(cl, (384, 128))

    def core(c_ref, dm_ref, w_ref, dw_ref, da_ref):
        @pl.when((pl.program_id(0) == 0) & (pl.program_id(1) == 0))
        def _():
            da_ref[...] = jnp.zeros_like(da_ref)

        a = _silu(c_ref[...]).astype(BF16)
        dmb = dm_ref[0].astype(BF16)
        dw_ref[0] = _dot(a, dmb, TN)
        da_ref[...] += _dot(dmb, w_ref[0].astype(BF16), NT)

    outs, _ = carrier_call(
        core, name=name, grid=(DEPTH, cl // tc),
        in_specs=[pl.BlockSpec((MOD_ROWS, d), lambda li, j: (0, 0)),
                  pl.BlockSpec((1, MOD_ROWS, tc), lambda li, j: (li, 0, j)),
                  pl.BlockSpec((1, d, tc), lambda li, j: (li, 0, j))],
        out_specs=[pl.BlockSpec((1, d, tc), lambda li, j: (li, 0, j)),
                   pl.BlockSpec((MOD_ROWS, d), lambda li, j: (0, 0))],
        out_shape=[jax.ShapeDtypeStruct((DEPTH, d, cl), F32), jax.ShapeDtypeStruct((MOD_ROWS, d), F32)],
        inputs=[cvecs, dm, w])
    return outs


def loss_head(y, target, name):
    n, d = y.shape
    s = target.shape[0]
    nt, nx = n // TM, s // TM

    def core(y_ref, t_ref, l_ref, dy_ref, acc_ref):
        i = pl.program_id(0)

        @pl.when(i == 0)
        def _():
            acc_ref[...] = jnp.zeros_like(acc_ref)

        @pl.when(i < nx)
        def _():
            e = y_ref[...] - t_ref[...]
            dy_ref[...] = e * (1.0 / d)
            acc_ref[...] += jnp.sum(e * e, axis=0, keepdims=True)

        @pl.when(i >= nx)
        def _():
            dy_ref[...] = jnp.zeros_like(dy_ref)

        @pl.when(i == nt - 1)
        def _():
            l_ref[...] = jnp.sum(acc_ref[...], axis=1, keepdims=True) * (0.5 / d)

    tile = pl.BlockSpec((TM, d), lambda i: (i, 0))
    outs, _ = carrier_call(
        core, name=name, grid=(nt,),
        in_specs=[tile, pl.BlockSpec((TM, d), lambda i: (jnp.minimum(i, nx - 1), 0))],
        out_specs=[pl.BlockSpec((1, 1), lambda i: (0, 0)), tile],
        out_shape=[jax.ShapeDtypeStruct((1, 1), F32), jax.ShapeDtypeStruct((n, d), F32)],
        scratch_shapes=[pltpu.VMEM((1, d), F32)], inputs=[y, target])
    return outs


def sum_devices(v, name):
    _, r, c = v.shape
    tr = _pick(r, (64, 8))

    def core(v_ref, o_ref):
        acc = v_ref[0]
        for p in range(1, N_DEV):
            acc = acc + v_ref[p]
        o_ref[...] = acc

    outs, _ = carrier_call(
        core, name=name, grid=(r // tr,), in_specs=[pl.BlockSpec((N_DEV, tr, c), lambda i: (0, i, 0))],
        out_specs=[pl.BlockSpec((tr, c), lambda i: (i, 0))], out_shape=[jax.ShapeDtypeStruct((r, c), F32)],
        inputs=[v])
    return outs[0]


def cctx_grad(parts, c_ctx, name):
    d = c_ctx.shape[1]

    def body(p_ref, c_ref, o_ref):
        acc = p_ref[0]
        for p in range(1, N_DEV):
            acc = acc + p_ref[p]
        o_ref[...] = acc[8:9, :] * _dsilu(c_ref[...])

    return pl.pallas_call(
        body, name=name, out_shape=jax.ShapeDtypeStruct((1, d), F32),
        in_specs=[pl.BlockSpec(memory_space=pltpu.VMEM)] * 2,
        out_specs=pl.BlockSpec(memory_space=pltpu.VMEM),
    )(parts, c_ctx)


def _adam_math(w, g, m, v):
    m2 = ADAM_B1 * m + (1.0 - ADAM_B1) * g
    v2 = ADAM_B2 * v + (1.0 - ADAM_B2) * (g * g)
    m_hat = m2 / (1.0 - ADAM_B1 ** ADAM_STEP)
    v_hat = v2 / (1.0 - ADAM_B2 ** ADAM_STEP)
    delta = -ADAM_LR * (m_hat / (jnp.sqrt(v_hat) + ADAM_EPS) + ADAM_WD * w)
    return delta, m2, v2


def adamw(w, g, m, v, name):
    r, c = w.shape
    tr = _pick(r, (256, 128, 64, 32, 16, 8, r))

    def core(w_ref, g_ref, m_ref, v_ref, d_ref, m2_ref, v2_ref):
        d_ref[...], m2_ref[...], v2_ref[...] = _adam_math(w_ref[...], g_ref[...], m_ref[...], v_ref[...])

    tile = pl.BlockSpec((tr, c), lambda i: (i, 0))
    out = jax.ShapeDtypeStruct((r, c), F32)
    outs, _ = carrier_call(core, name=name, grid=(r // tr,), in_specs=[tile] * 4, out_specs=[tile] * 3,
                           out_shape=[out] * 3, inputs=[w, g, m, v])
    return outs


def reduce_adamw(recv, w, m, v, name):
    r, c = w.shape
    tr = _pick(r, (256, 128, 64, 8))

    def core(recv_ref, w_ref, m_ref, v_ref, g_ref, d_ref, m2_ref, v2_ref):
        acc = recv_ref[0].astype(F32)
        for p in range(1, N_DEV):
            acc = acc + recv_ref[p].astype(F32)
        g_ref[...] = acc
        d_ref[...], m2_ref[...], v2_ref[...] = _adam_math(w_ref[...], acc, m_ref[...], v_ref[...])

    tile = pl.BlockSpec((tr, c), lambda i: (i, 0))
    out = jax.ShapeDtypeStruct((r, c), F32)
    outs, _ = carrier_call(
        core, name=name, grid=(r // tr,),
        in_specs=[pl.BlockSpec((N_DEV, tr, c), lambda i: (0, i, 0)), tile, tile, tile],
        out_specs=[tile] * 4, out_shape=[out] * 4, inputs=[recv, w, m, v])
    return outs


def kernel(x, c, ctx, c_ctx, w_mod, b_mod, norm_g, w_ffn_gate_up, w_ffn_down, w_in, w_out, na_rpb, w_pool, pool_scale, loss_target, m_c_ctx, m_w_mod, m_b_mod, m_norm_g, m_w_ffn_gate_up, m_w_ffn_down, m_w_in, m_w_out, m_na_rpb, m_w_pool, m_pool_scale, v_c_ctx, v_w_mod, v_b_mod, v_norm_g, v_w_ffn_gate_up, v_w_ffn_down, v_w_in, v_w_out, v_na_rpb, v_w_pool, v_pool_scale):
    s, d = x.shape[1], x.shape[2]
    l = ctx.shape[1]
    n = s + l
    nx = s // TM
    fq = w_ffn_gate_up.shape[-1]
    fr = w_ffn_down.shape[2]
    cl = w_mod.shape[2]
    dl = norm_g.shape[2]
    me = 4 * lax.axis_index("x") + 2 * lax.axis_index("y") + lax.axis_index("c")

    c_all = all_gather(c, "gather_c").reshape(N_DEV, d)
    cvecs = jnp.concatenate([c_all, c_ctx[None, :], jnp.zeros((MOD_ROWS - N_DEV - 1, d), F32)], axis=0)
    b_loc = lax.dynamic_slice(b_mod, (0, me * cl), (DEPTH, cl)).reshape(DEPTH, 1, cl)
    mod_loc = mod_fwd(cvecs, w_mod, b_loc, "mod_fwd")
    mod_all = all_gather(mod_loc.reshape(DEPTH * MOD_ROWS, cl), "gather_mod")
    mod_all = mod_all.reshape(N_DEV, DEPTH, MOD_ROWS, cl).transpose(1, 2, 0, 3).reshape(DEPTH, MOD_ROWS, N_DEV * cl)
    mine = lax.dynamic_slice(mod_all, (0, me, 0), (DEPTH, 1, N_DEV * cl))
    mods = jnp.concatenate([mine, mod_all[:, N_DEV:N_DEV + 1]], axis=1).reshape(DEPTH, 2, N_MOD, d)

    gu_b = w_ffn_gate_up.astype(BF16)
    dn_b = w_ffn_down.astype(BF16)
    wi_b = w_in.astype(BF16)
    wo_b = w_out.astype(BF16)
    wp_b = w_pool.astype(BF16)

    def ffn_shards(li, i):
        return [gu_b[li, i], dn_b[li, i]]

    def mix_shards(li):
        return [wi_b[li], wo_b[li]]

    def as_ffn_weights(gathered):
        return gathered[0].reshape(2, 4, d, fq), gathered[1].reshape(4, 2 * fr, d)

    def as_mix_weights(gathered):
        return gathered[0], gathered[1].reshape(N_DEV * wo_b.shape[1], d)

    tables = _rope_tables(s, n)
    mem_np, mem_t_np, inv_np = _pool_consts(l)
    mem, mem_t, inv = jnp.asarray(mem_np, BF16), jnp.asarray(mem_t_np, BF16), jnp.asarray(inv_np)
    first = exchange_only(Exchange(gathers=[norm_g.reshape(DEPTH * 6, dl)] + ffn_shards(0, 0)), "gather_first")
    g_full = first[0].reshape(N_DEV, DEPTH, 6, dl).transpose(1, 2, 0, 3).reshape(DEPTH, 6, 1, N_DEV * dl)

    weights = {("ffn", 0, 0): as_ffn_weights(first[1:])}
    saved = {}
    xcur = jnp.concatenate([x[0], ctx[0]], axis=0)
    for li in range(DEPTH):
        last = li == DEPTH - 1
        for i in range(2):
            tag = f"l{li}_ffn{i}"
            wgu, wd4 = weights[("ffn", li, i)]
            if i == 0:
                ex_up, ex_dn = Exchange(gathers=mix_shards(li)), None
            elif not last:
                ex_up, ex_dn = Exchange(gathers=[gu_b[li + 1, 0]]), Exchange(gathers=[dn_b[li + 1, 0]])
            else:
                ex_up = ex_dn = None
            (hb, gu, a4), got_up = ffn_up(xcur, g_full[li, 4 * i], mods[li], wgu, s, 6 * i, tag + "_up", ex_up)
            (ff, xnext), got_dn = ffn_down(a4, wd4, xcur, g_full[li, 4 * i + 1], mods[li], s, 6 * i + 2, tag + "_down", ex_dn)
            saved[("ffn", li, i)] = (xcur, hb, gu, a4, ff)
            xcur = xnext
            if i == 0:
                weights[("mix", li)] = as_mix_weights(got_up)
            elif not last:
                weights[("ffn", li + 1, 0)] = as_ffn_weights(got_up + got_dn)
            if i == 0:
                tag = f"l{li}_mix"
                win8, wout = weights[("mix", li)]
                hb, qkv, u = mix_in(xcur, g_full[li, 2], mods[li], win8, tables, s, tag + "_in")
                tb = bias_tables(na_rpb[li], tag + "_bias")
                (na, lse), got = na_fwd(qkv, tb, s, tag + "_na", Exchange(gathers=ffn_shards(li, 1)))
                weights[("ffn", li, 1)] = as_ffn_weights(got)
                lse_c = None
                if not last:
                    na, lse_c = ctx_attn_fwd(qkv, na, s, tag + "_ctx_attn")
                py = pool_fwd(u, mem, inv, wp_b[li], pool_scale[li][None, :], nx, tag + "_pool")
                fm, xnext = mix_out(na, py, wout, xcur, g_full[li, 3], mods[li], s, tag + "_out")
                saved[("mix", li)] = (xcur, hb, qkv, u, tb, na, lse, lse_c, py, fm)
                xcur = xnext

    loss_local, dcur = loss_head(xcur, loss_target[0], "loss")
    loss = lax.psum(loss_local[0, 0], ("x", "y", "c"))

    recv = {"gu": lax.empty((N_DEV, 2 * DEPTH, d, fq), BF16), "dn": lax.empty((N_DEV, 2 * DEPTH, fr, d), BF16),
            "wi": lax.empty((N_DEV, DEPTH, d, IN_BLOCK), BF16), "wo": lax.empty((N_DEV, DEPTH, wo_b.shape[1], d), BF16)}
    pending = []

    def take(keys):
        nonlocal pending
        jobs = [(gr, recv[key], st) for key, gr, st in pending if key in keys]
        order = [key for key, _, _ in pending if key in keys]
        pending = [p for p in pending if p[0] not in keys]
        return Exchange(a2as=jobs), order

    def put(order, bufs):
        for key, buf in zip(order, bufs):
            recv[key] = buf

    d_rpb, d_wp, d_ps, d_mod, d_g = [], [], [], [], []
    for li in reversed(range(DEPTH)):
        reds = {}
        for i in (1, 0):
            tag = f"l{li}_ffn{i}"
            xin, hb, gu, a4, ff = saved[("ffn", li, i)]
            wgu, wd4 = weights[("ffn", li, i)]
            dff, red1 = post_bwd(ff, dcur, g_full[li, 4 * i + 1], mods[li], 6 * i + 2, 0.5, s, tag + "_post_bwd")
            ex, order = take(("gu", "wi", "wo"))
            dgu, bufs = ffn_da(dff, wd4, gu, tag + "_da", ex)
            put(order, bufs)
            ex, order = take(("dn",))
            (dcur, red2), bufs = ffn_dh(dgu, wgu, xin, dcur, g_full[li, 4 * i], mods[li], s, 6 * i, tag + "_dh", ex)
            put(order, bufs)
            g_dn = grad_weight(a4, dff, tag + "_dwdown", a_lead=4).reshape(N_DEV, fr, d)
            g_gu = grad_weight(hb, dgu.reshape(N_DEV, n, fq), tag + "_dwgu", b_lead=N_DEV)
            pending += [("gu", g_gu, 2 * li + i), ("dn", g_dn, 2 * li + i)]
            reds[i] = (red1, red2)
            if i == 1:
                tag = f"l{li}_mix"
                xin, hb, qkv, u, tb, na, lse, lse_c, py, fm = saved[("mix", li)]
                win8, wout = weights[("mix", li)]
                dfm, redm1 = post_bwd(fm, dcur, g_full[li, 3], mods[li], 5, 1.0, s, tag + "_post_bwd")
                dmix = matmul_nt(dfm, wout, tag + "_dmix")
                g_wo = grad_wout(na, py, dfm, tag + "_dwout").reshape(N_DEV, wo_b.shape[1], d)
                du, gwp, gps = pool_bwd(dmix, u, mem, mem_t, inv, wp_b[li], pool_scale[li][None, :], nx, tag + "_pool_bwd")
                ex, order = take(("gu", "dn"))
                (dq, dk, dv, dtb), bufs = na_bwd(qkv, tb, na, dmix, lse, s, tag + "_na_bwd", ex)
                put(order, bufs)
                if li != DEPTH - 1:
                    dq, dk, dv = ctx_attn_bwd(qkv, na, dmix, lse_c, dq, dk, dv, s, tag + "_ctx_attn_bwd")
                grpb = bias_tables_bwd(dtb, tag + "_bias_bwd")
                dqkvu = qkv_bwd(dq, dk, dv, du, tables, tag + "_rope_bwd")
                (dcur, redm2), _ = mix_dh(dqkvu, win8, xin, dcur, g_full[li, 2], mods[li], s, tag + "_dh")
                g_wi = grad_weight(hb, dqkvu, tag + "_dwin", b_cols=IN_BLOCK)
                pending += [("wi", g_wi, li), ("wo", g_wo, li)]
                d_rpb.insert(0, grpb)
                d_wp.insert(0, gwp)
                d_ps.insert(0, gps[0])
        (ra1, ra2), (rb1, rb2) = reds[0], reds[1]
        d_mod.insert(0, jnp.stack([ra2[:, 0], ra2[:, 1], ra1[:, 0], redm2[:, 0], redm2[:, 1], redm1[:, 0],
                                   rb2[:, 0], rb2[:, 1], rb1[:, 0]], axis=1))
        d_g.insert(0, jnp.stack([t[0] + t[1] for t in (ra2[:, 2], ra1[:, 1], redm2[:, 2], redm1[:, 1], rb2[:, 2], rb1[:, 1])]))
    grad_x = dcur[:s][None]

    def pad8(t):
        t = t.reshape(-1, d) if t.size % d == 0 else jnp.pad(t.reshape(-1), (0, -t.size % d)).reshape(-1, d)
        return jnp.pad(t, ((0, -t.shape[0] % 8), (0, 0)))

    small_parts = [pad8(jnp.stack(d_mod)), pad8(jnp.stack(d_g)), pad8(jnp.stack(d_wp)), pad8(jnp.stack(d_ps)),
                   pad8(jnp.stack(d_rpb))]
    offs = np.cumsum([0] + [p.shape[0] for p in small_parts])
    small_parts.append(jnp.zeros((-offs[-1] % 64, d), F32))
    small = jnp.concatenate(small_parts, axis=0)
    ex, order = take(("gu", "dn", "wi", "wo"))
    ex.gathers = [small]
    ex.n_jobs += 1
    bufs = exchange_only(ex, "exchange_last")
    small_all = bufs[0]
    put(order, bufs[1:])
    small_sum = sum_devices(small_all, "sum_small_grads")

    n_mod_rows = DEPTH * 2 * N_MOD
    dmod_all = small_all[:, :n_mod_rows].reshape(N_DEV, DEPTH, 2, N_MOD * d)
    dmod_sum = small_sum[:n_mod_rows].reshape(DEPTH, 2, N_MOD * d)
    dm_rows = jnp.concatenate([dmod_all[:, :, 0].transpose(1, 0, 2), dmod_sum[:, 1:2],
                               jnp.zeros((DEPTH, MOD_ROWS - N_DEV - 1, N_MOD * d), F32)], axis=1)
    grad_b_mod = dmod_sum[:, 0] + dmod_sum[:, 1]
    dm_loc = lax.dynamic_slice(dm_rows, (0, 0, me * cl), (DEPTH, MOD_ROWS, cl))
    grad_w_mod, da_part = mod_bwd(cvecs, dm_loc, w_mod, "mod_bwd")
    da_all = all_gather(da_part, "gather_dcvec")
    grad_c_ctx = cctx_grad(da_all, c_ctx[None, :], "c_ctx_grad")[0]

    grad_norm_full = small_sum[offs[1]:offs[1] + DEPTH * 6].reshape(DEPTH, 6, d)
    grad_norm_g = lax.dynamic_slice(grad_norm_full, (0, 0, me * dl), (DEPTH, 6, dl))
    grad_w_pool = small_sum[offs[2]:offs[2] + w_pool.size // d].reshape(w_pool.shape)
    grad_pool_scale = small_sum[offs[3]:offs[3] + pool_scale.size // d].reshape(pool_scale.shape)
    grad_na_rpb = small_sum[offs[4]:offs[5]].reshape(-1)[:na_rpb.size].reshape(na_rpb.shape)

    def big_adam(key, w, m, v, name):
        shp = w.shape
        cols = shp[-1]
        outs = reduce_adamw(recv[key].reshape(N_DEV, -1, cols), w.reshape(-1, cols), m.reshape(-1, cols),
                            v.reshape(-1, cols), name)
        return tuple(t.reshape(shp) for t in outs)

    def small_adam(w, g, m, v, name):
        shp = w.shape
        cols = shp[-1]
        outs = adamw(w.reshape(-1, cols), g.reshape(-1, cols), m.reshape(-1, cols), v.reshape(-1, cols), name)
        return tuple(t.reshape(shp) for t in outs)

    b_gu = big_adam("gu", w_ffn_gate_up, m_w_ffn_gate_up, v_w_ffn_gate_up, "adam_gate_up")
    b_dn = big_adam("dn", w_ffn_down, m_w_ffn_down, v_w_ffn_down, "adam_down")
    b_wi = big_adam("wi", w_in, m_w_in, v_w_in, "adam_w_in")
    b_wo = big_adam("wo", w_out, m_w_out, v_w_out, "adam_w_out")
    a_cc = small_adam(c_ctx, grad_c_ctx, m_c_ctx, v_c_ctx, "adam_c_ctx")
    a_wm = small_adam(w_mod, grad_w_mod, m_w_mod, v_w_mod, "adam_w_mod")
    a_bm = small_adam(b_mod, grad_b_mod, m_b_mod, v_b_mod, "adam_b_mod")
    a_ng = small_adam(norm_g, grad_norm_g, m_norm_g, v_norm_g, "adam_norm_g")
    a_rp = small_adam(na_rpb, grad_na_rpb, m_na_rpb, v_na_rpb, "adam_na_rpb")
    a_wp = small_adam(w_pool, grad_w_pool, m_w_pool, v_w_pool, "adam_w_pool")
    a_ps = small_adam(pool_scale, grad_pool_scale, m_pool_scale, v_pool_scale, "adam_pool_scale")

    grads = (grad_c_ctx, grad_w_mod, grad_b_mod, grad_norm_g, b_gu[0], b_dn[0], b_wi[0], b_wo[0], grad_na_rpb, grad_w_pool, grad_pool_scale)
    deltas = (a_cc[0], a_wm[0], a_bm[0], a_ng[0], b_gu[1], b_dn[1], b_wi[1], b_wo[1], a_rp[0], a_wp[0], a_ps[0])
    new_m = (a_cc[1], a_wm[1], a_bm[1], a_ng[1], b_gu[2], b_dn[2], b_wi[2], b_wo[2], a_rp[1], a_wp[1], a_ps[1])
    new_v = (a_cc[2], a_wm[2], a_bm[2], a_ng[2], b_gu[3], b_dn[3], b_wi[3], b_wo[3], a_rp[2], a_wp[2], a_ps[2])
    return (loss, grad_x, *grads, *deltas, *new_m, *new_v)
```

```python
import functools
import math

import numpy as np
import jax
import jax.numpy as jnp
from jax import lax
from jax.experimental import pallas as pl
from jax.experimental.pallas import tpu as pltpu

F32 = jnp.float32
BF16 = jnp.bfloat16

N_DEV = 8
DEPTH = 2
GRID_W = 64
N_MOD = 9
NA_HEADS = 8
HEAD_DIM = 64
NA_WIDTH = NA_HEADS * HEAD_DIM
NA_KH = 8
NA_KW = 16
POOL_GROUPS = 4
POOL_CH = 128
POOL_WIDTH = POOL_GROUPS * POOL_CH
POOL_WINDOWS = (2, 4, 8, 16)
ROPE_THETA = 10000.0
ROPE_PAIRS = HEAD_DIM // 4
RMS_EPS = 1e-6
NEG_INF = -1e30
ATT_SCALE = HEAD_DIM ** -0.5

ADAM_LR = 0.001
ADAM_B1 = 0.9
ADAM_B2 = 0.999
ADAM_EPS = 1e-08
ADAM_WD = 0.01
ADAM_STEP = 10

TM = 256
LANES = 128
HEADS_PER_BLOCK = LANES // HEAD_DIM
N_HEAD_BLOCKS = NA_WIDTH // LANES
IN_BLOCK = 2 * LANES
N_QKV_BLOCKS = 3 * NA_WIDTH // IN_BLOCK
VMEM_LIMIT = 48 * 1024 * 1024
HIGHEST = lax.Precision.HIGHEST
MESH = pl.DeviceIdType.MESH
ANY = pl.BlockSpec(memory_space=pl.ANY)

NN = (((1,), (0,)), ((), ()))
NT = (((1,), (1,)), ((), ()))
TN = (((0,), (0,)), ((), ()))


def _pick(n, cands):
    for t in cands:
        if n % t == 0:
            return t
    raise ValueError(f"no tile for {n} among {cands}")


def _dot(a, b, dn=NN, precision=None):
    return lax.dot_general(a, b, dn, preferred_element_type=F32, precision=precision)


def _silu(x):
    return x * jax.nn.sigmoid(x)


def _dsilu(x):
    s = jax.nn.sigmoid(x)
    return s * (1.0 + x * (1.0 - s))


def _peer(mask):
    x, y, c = lax.axis_index("x"), lax.axis_index("y"), lax.axis_index("c")
    px = 1 - x if mask & 4 else x
    py = 1 - y if mask & 2 else y
    pc = 1 - c if mask & 1 else c
    return (px, py, pc), 4 * px + 2 * py + pc


class Exchange:
    def __init__(self, gathers=(), a2as=()):
        self.gathers = list(gathers)
        self.a2as = list(a2as)
        self.n_jobs = len(self.gathers) + len(self.a2as)

    def inputs(self):
        out = list(self.gathers)
        for v, buf, _ in self.a2as:
            out += [v, buf]
        return out

    def out_shapes(self):
        shapes = [jax.ShapeDtypeStruct((N_DEV,) + v.shape, v.dtype) for v in self.gathers]
        shapes += [jax.ShapeDtypeStruct(buf.shape, buf.dtype) for _, buf, _ in self.a2as]
        return shapes

    def aliases(self, n_in, n_out):
        ng = len(self.gathers)
        return {n_in + ng + 2 * k + 1: n_out + ng + k for k in range(len(self.a2as))}

    def scratch(self):
        per = N_DEV - 1
        return [pltpu.SemaphoreType.DMA((per * self.n_jobs,)), pltpu.SemaphoreType.DMA((per * self.n_jobs,)),
                pltpu.SemaphoreType.DMA((self.n_jobs,))]

    def _copies(self, in_refs, out_refs, sems, with_recvs):
        send_sems, recv_sems, local_sems = sems
        _, me = _peer(0)
        ng = len(self.gathers)
        local, sends, recvs = [], [], []
        for job in range(self.n_jobs):
            if job < ng:
                src_of = lambda pid, r=in_refs[job]: r
                dst_of = lambda pid, r=out_refs[job]: r.at[pid]
            else:
                k = job - ng
                stage = self.a2as[k][2]
                src_of = lambda pid, r=in_refs[ng + 2 * k]: r.at[pid]
                dst_of = lambda pid, r=out_refs[job], st=stage: r.at[pid, st]
            local.append(pltpu.make_async_copy(src_of(me), dst_of(me), local_sems.at[job]))
            for mask in range(1, N_DEV):
                peer, pid = _peer(mask)
                idx = job * (N_DEV - 1) + mask - 1
                sends.append(pltpu.make_async_remote_copy(
                    src_ref=src_of(pid), dst_ref=dst_of(me), send_sem=send_sems.at[idx],
                    recv_sem=recv_sems.at[idx], device_id=peer, device_id_type=MESH))
                if with_recvs:
                    recvs.append(pltpu.make_async_remote_copy(
                        src_ref=src_of(pid), dst_ref=dst_of(pid), send_sem=send_sems.at[idx],
                        recv_sem=recv_sems.at[idx], device_id=peer, device_id_type=MESH))
        return local, sends, recvs

    def start(self, in_refs, out_refs, sems):
        local, sends, _ = self._copies(in_refs, out_refs, sems, False)
        for cp in local + sends:
            cp.start()

    def wait(self, in_refs, out_refs, sems):
        local, sends, recvs = self._copies(in_refs, out_refs, sems, True)
        for cp in recvs:
            cp.wait_recv()
        for cp in sends:
            cp.wait_send()
        for cp in local:
            cp.wait()


def carrier_call(core, *, name, grid, in_specs, out_specs, out_shape, inputs, scratch_shapes=(), aliases=None,
                 exchange=None):
    aliases = dict(aliases or {})
    n_in, n_out, n_sc = len(in_specs), len(out_specs), len(scratch_shapes)
    sem = ("arbitrary",) * len(grid)
    params = pltpu.CompilerParams(dimension_semantics=sem, vmem_limit_bytes=VMEM_LIMIT)
    if exchange is None or exchange.n_jobs == 0:
        outs = pl.pallas_call(core, name=name, grid=grid, in_specs=list(in_specs), out_specs=tuple(out_specs),
                              out_shape=tuple(out_shape), scratch_shapes=list(scratch_shapes),
                              input_output_aliases=aliases, compiler_params=params)(*inputs)
        return list(outs), []
    x_in = exchange.inputs()
    x_out = exchange.out_shapes()
    aliases.update(exchange.aliases(n_in, n_out))

    def body(*refs):
        a = n_in + len(x_in)
        b = a + n_out + len(x_out)
        core_in, job_in = refs[:n_in], refs[n_in:a]
        core_out, job_out = refs[a:a + n_out], refs[a + n_out:b]
        core_sc, job_sc = refs[b:b + n_sc], refs[b + n_sc:]
        first = functools.reduce(lambda p, q: p & q, [pl.program_id(ax) == 0 for ax in range(len(grid))])
        last = functools.reduce(lambda p, q: p & q, [pl.program_id(ax) == g - 1 for ax, g in enumerate(grid)])

        @pl.when(first)
        def _():
            exchange.start(job_in, job_out, job_sc)

        core(*core_in, *core_out, *core_sc)

        @pl.when(last)
        def _():
            exchange.wait(job_in, job_out, job_sc)

    outs = pl.pallas_call(
        body, name=name, grid=grid, in_specs=list(in_specs) + [ANY] * len(x_in),
        out_specs=tuple(out_specs) + (ANY,) * len(x_out), out_shape=tuple(out_shape) + tuple(x_out),
        scratch_shapes=list(scratch_shapes) + exchange.scratch(), input_output_aliases=aliases,
        compiler_params=params)(*inputs, *x_in)
    return list(outs[:n_out]), list(outs[n_out:])


def exchange_only(exchange, name):
    def body(*refs):
        n_in, n_out = len(exchange.inputs()), len(exchange.out_shapes())
        job_in, job_out, sems = refs[:n_in], refs[n_in:n_in + n_out], refs[n_in + n_out:]
        exchange.start(job_in, job_out, sems)
        exchange.wait(job_in, job_out, sems)

    x_in = exchange.inputs()
    outs = pl.pallas_call(
        body, name=name, in_specs=[ANY] * len(x_in), out_specs=(ANY,) * len(exchange.out_shapes()),
        out_shape=tuple(exchange.out_shapes()), scratch_shapes=exchange.scratch(),
        input_output_aliases=exchange.aliases(0, 0))(*x_in)
    return list(outs)


def all_gather(v, name):
    return exchange_only(Exchange(gathers=[v]), name)[0]


def _rms(xf):
    return lax.rsqrt(jnp.mean(xf * xf, axis=-1, keepdims=True) + RMS_EPS)


def _is_ctx(i, tm, s):
    return (i * tm + lax.broadcasted_iota(jnp.int32, (tm, 1), 0)) >= s


def _by_tile_kind(i, tm, s, fn):
    n_latent = s // tm

    @pl.when(i < n_latent)
    def _():
        fn(None)

    @pl.when(i >= n_latent)
    def _():
        fn(_is_ctx(i, tm, s))


def _mod_rows(mod_ref, k, is_ctx):
    if is_ctx is None:
        return mod_ref[0, k:k + 1, :]
    return jnp.where(is_ctx, mod_ref[1, k:k + 1, :], mod_ref[0, k:k + 1, :])


def _norm_mod(xf, g, mod_ref, k_shift, k_scale, is_ctx):
    nrm = xf * _rms(xf) * g
    return (nrm * (1.0 + _mod_rows(mod_ref, k_scale, is_ctx)) + _mod_rows(mod_ref, k_shift, is_ctx)).astype(BF16)


def _post(xf, ff, g, mod_ref, k_gate, coef, is_ctx):
    return xf + coef * _mod_rows(mod_ref, k_gate, is_ctx) * (ff * _rms(ff) * g)


def _red_add(red_ref, first, is_ctx, rows):
    @pl.when(first)
    def _():
        red_ref[...] = jnp.zeros_like(red_ref)

    for r, val in enumerate(rows):
        tot = jnp.sum(val, axis=0, keepdims=True)
        if is_ctx is None:
            red_ref[0, r:r + 1, :] += tot
        else:
            ctx = jnp.sum(jnp.where(is_ctx, val, 0.0), axis=0, keepdims=True)
            red_ref[0, r:r + 1, :] += tot - ctx
            red_ref[1, r:r + 1, :] += ctx


def _pre_bwd(xf, dh, dxo, g, mod_ref, k_scale, is_ctx):
    r = _rms(xf)
    xhat = xf * r
    dn = dh * (1.0 + _mod_rows(mod_ref, k_scale, is_ctx))
    dxhat = dn * g
    dx = dxo + r * (dxhat - xhat * jnp.mean(dxhat * xhat, axis=-1, keepdims=True))
    return dx, [dh, dh * (xhat * g), dn * xhat]


def _vec_spec(d):
    return pl.BlockSpec((1, d), lambda *_: (0, 0))


def _mod_whole(d):
    return pl.BlockSpec((2, N_MOD, d), lambda *_: (0, 0, 0))


def _red_whole(d):
    return pl.BlockSpec((2, 8, d), lambda *_: (0, 0, 0))


def _token_tile(n):
    return _pick(n, (768, 640, 512, 384, 256))


def _ffn_tile(n):
    return _pick(n, (384, 640, 256))


def _resident(shape):
    zeros = (0,) * len(shape)
    return pl.BlockSpec(shape, lambda i: zeros, pipeline_mode=pl.Buffered(1))


def ffn_up(x, g, mod, wgu, s, k0, name, exchange=None):
    n, d = x.shape
    nk, fq = wgu.shape[1], wgu.shape[-1]
    tm = _ffn_tile(n)

    def core(x_ref, g_ref, mod_ref, w_ref, hb_ref, gu_ref, a_ref):
        i = pl.program_id(0)

        def prologue(is_ctx):
            hb_ref[...] = _norm_mod(x_ref[...], g_ref[...], mod_ref, k0, k0 + 1, is_ctx)

        _by_tile_kind(i, tm, s, prologue)
        h = hb_ref[...]
        for k in range(nk):
            gg = _dot(h, w_ref[0, k])
            uu = _dot(h, w_ref[1, k])
            gu_ref[0, k] = gg.astype(BF16)
            gu_ref[1, k] = uu.astype(BF16)
            a_ref[k] = (_silu(gg) * uu).astype(BF16)

    outs, xo = carrier_call(
        core, name=name, grid=(n // tm,),
        in_specs=[pl.BlockSpec((tm, d), lambda i: (i, 0)), _vec_spec(d), _mod_whole(d), _resident(wgu.shape)],
        out_specs=[pl.BlockSpec((tm, d), lambda i: (i, 0)),
                   pl.BlockSpec((2, nk, tm, fq), lambda i: (0, 0, i, 0)),
                   pl.BlockSpec((nk, tm, fq), lambda i: (0, i, 0))],
        out_shape=[jax.ShapeDtypeStruct((n, d), BF16), jax.ShapeDtypeStruct((2, nk, n, fq), BF16),
                   jax.ShapeDtypeStruct((nk, n, fq), BF16)],
        inputs=[x, g, mod, wgu], exchange=exchange)
    return outs, xo


def ffn_down(a4, wd4, x, g, mod, s, k_gate, name, exchange=None):
    n, d = x.shape
    nk, fq = wd4.shape[0], wd4.shape[1]
    tm = _ffn_tile(n)

    def core(a_ref, w_ref, x_ref, g_ref, mod_ref, f_ref, xo_ref):
        i = pl.program_id(0)
        ff = _dot(a_ref[0], w_ref[0])
        for k in range(1, nk):
            ff = ff + _dot(a_ref[k], w_ref[k])
        f_ref[...] = ff

        def epilogue(is_ctx):
            xo_ref[...] = _post(x_ref[...], f_ref[...], g_ref[...], mod_ref, k_gate, 0.5, is_ctx)

        _by_tile_kind(i, tm, s, epilogue)

    tile = pl.BlockSpec((tm, d), lambda i: (i, 0))
    outs, xo = carrier_call(
        core, name=name, grid=(n // tm,),
        in_specs=[pl.BlockSpec((nk, tm, fq), lambda i: (0, i, 0)), _resident(wd4.shape), tile, _vec_spec(d),
                  _mod_whole(d)],
        out_specs=[tile, tile],
        out_shape=[jax.ShapeDtypeStruct((n, d), F32), jax.ShapeDtypeStruct((n, d), F32)],
        inputs=[a4, wd4, x, g, mod], exchange=exchange)
    return outs, xo


def ffn_da(df, wd4, gu, name, exchange=None):
    n, d = df.shape
    nk, fq = wd4.shape[0], wd4.shape[1]
    tm = _ffn_tile(n)

    def core(df_ref, w_ref, gu_ref, o_ref):
        dfv = df_ref[...]
        for k in range(nk):
            da = _dot(dfv, w_ref[k], NT)
            gg = gu_ref[0, k].astype(F32)
            uu = gu_ref[1, k].astype(F32)
            o_ref[0, k] = (da * uu * _dsilu(gg)).astype(BF16)
            o_ref[1, k] = (da * _silu(gg)).astype(BF16)

    gu_spec = pl.BlockSpec((2, nk, tm, fq), lambda i: (0, 0, i, 0))
    outs, xo = carrier_call(
        core, name=name, grid=(n // tm,),
        in_specs=[pl.BlockSpec((tm, d), lambda i: (i, 0)), _resident(wd4.shape), gu_spec],
        out_specs=[gu_spec], out_shape=[jax.ShapeDtypeStruct(gu.shape, BF16)],
        inputs=[df, wd4, gu], exchange=exchange)
    return outs[0], xo


def ffn_dh(dgu, wgu, x, dxo, g, mod, s, k0, name, exchange=None):
    n, d = x.shape
    nk, fq = wgu.shape[1], wgu.shape[-1]
    tm = _ffn_tile(n)

    def core(dgu_ref, w_ref, x_ref, dxo_ref, g_ref, mod_ref, dx_ref, red_ref, dh_s):
        i = pl.program_id(0)
        dh = _dot(dgu_ref[0, 0], w_ref[0, 0], NT) + _dot(dgu_ref[1, 0], w_ref[1, 0], NT)
        for k in range(1, nk):
            dh = dh + _dot(dgu_ref[0, k], w_ref[0, k], NT) + _dot(dgu_ref[1, k], w_ref[1, k], NT)
        dh_s[...] = dh

        def epilogue(is_ctx):
            dx, sums = _pre_bwd(x_ref[...], dh_s[...], dxo_ref[...], g_ref[...], mod_ref, k0 + 1, is_ctx)
            dx_ref[...] = dx
            _red_add(red_ref, i == 0, is_ctx, sums)

        _by_tile_kind(i, tm, s, epilogue)

    tile = pl.BlockSpec((tm, d), lambda i: (i, 0))
    outs, xo = carrier_call(
        core, name=name, grid=(n // tm,),
        in_specs=[pl.BlockSpec((2, nk, tm, fq), lambda i: (0, 0, i, 0)), _resident(wgu.shape), tile, tile,
                  _vec_spec(d), _mod_whole(d)],
        out_specs=[tile, _red_whole(d)],
        out_shape=[jax.ShapeDtypeStruct((n, d), F32), jax.ShapeDtypeStruct((2, 8, d), F32)],
        scratch_shapes=[pltpu.VMEM((tm, d), F32)], inputs=[dgu, wgu, x, dxo, g, mod], exchange=exchange)
    return outs, xo


def grad_weight(a, b, name, a_lead=None, b_lead=None, b_cols=None):
    n = a.shape[-2]
    ka = a.shape[-1]
    kb = b_cols or b.shape[-1]
    nj = a_lead or b_lead or (b.shape[-1] // b_cols)
    tk = _pick(n, (1408, 1024, 768, 640, 512, 256))
    nk = n // tk

    def core(a_ref, b_ref, o_ref, acc):
        kk = pl.program_id(1)

        @pl.when(kk == 0)
        def _():
            acc[...] = jnp.zeros_like(acc)

        av = a_ref[0] if a_lead else a_ref[...]
        bv = b_ref[0] if b_lead else b_ref[...]
        acc[...] += _dot(av, bv, TN)

        @pl.when(kk == nk - 1)
        def _():
            o_ref[0] = acc[...].astype(BF16)

    a_spec = (pl.BlockSpec((1, tk, ka), lambda j, kk: (j, kk, 0)) if a_lead
              else pl.BlockSpec((tk, ka), lambda j, kk: (kk, 0)))
    if b_lead:
        b_spec = pl.BlockSpec((1, tk, kb), lambda j, kk: (j, kk, 0))
    elif b_cols:
        b_spec = pl.BlockSpec((tk, kb), lambda j, kk: (kk, j))
    else:
        b_spec = pl.BlockSpec((tk, kb), lambda j, kk: (kk, 0))
    outs, _ = carrier_call(
        core, name=name, grid=(nj, nk), in_specs=[a_spec, b_spec],
        out_specs=[pl.BlockSpec((1, ka, kb), lambda j, kk: (j, 0, 0))],
        out_shape=[jax.ShapeDtypeStruct((nj, ka, kb), BF16)],
        scratch_shapes=[pltpu.VMEM((ka, kb), F32)], inputs=[a, b])
    return outs[0]


def post_bwd(f, dxo, g, mod, k_gate, coef, s, name):
    n, d = f.shape

    def core(f_ref, dxo_ref, g_ref, mod_ref, df_ref, red_ref):
        i = pl.program_id(0)
        is_ctx = _is_ctx(i, TM, s)
        ff = f_ref[...]
        dxo_ = dxo_ref[...]
        gg = g_ref[...]
        r = _rms(ff)
        fn = ff * r
        dy = (coef * _mod_rows(mod_ref, k_gate, is_ctx)) * dxo_
        dfn = dy * gg
        df_ref[...] = (r * (dfn - fn * jnp.mean(dfn * fn, axis=-1, keepdims=True))).astype(BF16)
        _red_add(red_ref, i == 0, is_ctx, [coef * (fn * gg) * dxo_, dy * fn])

    tile = pl.BlockSpec((TM, d), lambda i: (i, 0))
    outs, _ = carrier_call(
        core, name=name, grid=(n // TM,), in_specs=[tile, tile, _vec_spec(d), _mod_whole(d)],
        out_specs=[tile, _red_whole(d)],
        out_shape=[jax.ShapeDtypeStruct((n, d), BF16), jax.ShapeDtypeStruct((2, 8, d), F32)],
        inputs=[f, dxo, g, mod])
    return outs


def matmul_nt(a, b, name):
    m, k = a.shape
    n = b.shape[0]
    tm = _token_tile(m)

    def core(a_ref, b_ref, o_ref):
        o_ref[...] = _dot(a_ref[...], b_ref[...], NT)

    outs, _ = carrier_call(
        core, name=name, grid=(m // tm,),
        in_specs=[pl.BlockSpec((tm, k), lambda i: (i, 0)), pl.BlockSpec((n, k), lambda i: (0, 0))],
        out_specs=[pl.BlockSpec((tm, n), lambda i: (i, 0))], out_shape=[jax.ShapeDtypeStruct((m, n), F32)],
        inputs=[a, b])
    return outs[0]


def _rope_tables(s, n):
    t = jnp.arange(n)
    lane = jnp.arange(LANES)
    dd = lane % HEAD_DIM
    inv = ROPE_THETA ** (-(dd % ROPE_PAIRS).astype(F32) / ROPE_PAIRS)
    pos = jnp.where(dd[None, :] < HEAD_DIM // 2, (t // GRID_W)[:, None], (t % GRID_W)[:, None]).astype(F32)
    ang = pos * inv[None, :]
    live = (t < s)[:, None]
    first = ((dd % (2 * ROPE_PAIRS)) < ROPE_PAIRS)[None, :]
    cos = jnp.where(live, jnp.cos(ang), 1.0)
    sin = jnp.where(live, jnp.sin(ang), 0.0)
    sa = jnp.where(first, -sin, 0.0)
    sb = jnp.where(first, 0.0, sin)
    return cos.astype(F32), sa.astype(F32), sb.astype(F32)


def _rope(xv, cos, sa, sb):
    return (xv * cos + pltpu.roll(xv, LANES - ROPE_PAIRS, 1) * sa + pltpu.roll(xv, ROPE_PAIRS, 1) * sb)


def mix_in(x, g, mod, win8, tables, s, name):
    n, d = x.shape
    tm = _token_tile(n)
    nb = win8.shape[0]
    n_rope = 2 * NA_WIDTH // IN_BLOCK

    def core(x_ref, g_ref, mod_ref, w_ref, c_ref, sa_ref, sb_ref, hb_ref, qkv_ref, u_ref):
        i = pl.program_id(0)
        hb = _norm_mod(x_ref[...], g_ref[...], mod_ref, 3, 4, _is_ctx(i, tm, s))
        hb_ref[...] = hb
        cos, sa, sb = c_ref[...], sa_ref[...], sb_ref[...]
        for j in range(nb):
            y = _dot(hb, w_ref[j])
            if j < n_rope:
                for b in range(IN_BLOCK // LANES):
                    sl = slice(b * LANES, (b + 1) * LANES)
                    qkv_ref[:, j * IN_BLOCK + b * LANES:j * IN_BLOCK + (b + 1) * LANES] = (
                        _rope(y[:, sl], cos, sa, sb).astype(BF16))
            elif j < N_QKV_BLOCKS:
                qkv_ref[:, j * IN_BLOCK:(j + 1) * IN_BLOCK] = y.astype(BF16)
            else:
                u_ref[:, (j - N_QKV_BLOCKS) * IN_BLOCK:(j - N_QKV_BLOCKS + 1) * IN_BLOCK] = y

    tab = pl.BlockSpec((tm, LANES), lambda i: (i, 0))
    row = lambda w: pl.BlockSpec((tm, w), lambda i: (i, 0))
    outs, _ = carrier_call(
        core, name=name, grid=(n // tm,),
        in_specs=[row(d), _vec_spec(d), _mod_whole(d), pl.BlockSpec((nb, d, IN_BLOCK), lambda i: (0, 0, 0)),
                  tab, tab, tab],
        out_specs=[row(d), row(3 * NA_WIDTH), row(POOL_WIDTH)],
        out_shape=[jax.ShapeDtypeStruct((n, d), BF16), jax.ShapeDtypeStruct((n, 3 * NA_WIDTH), BF16),
                   jax.ShapeDtypeStruct((n, POOL_WIDTH), F32)],
        inputs=[x, g, mod, win8, *tables])
    return outs


def qkv_bwd(dq, dk, dv, du, tables, name):
    n = dq.shape[0]
    w = NA_WIDTH

    def core(dq_ref, dk_ref, dv_ref, du_ref, c_ref, sa_ref, sb_ref, o_ref):
        cos, sa, sb = c_ref[...], -sa_ref[...], -sb_ref[...]
        for b in range(N_HEAD_BLOCKS):
            sl = slice(b * LANES, (b + 1) * LANES)
            o_ref[:, b * LANES:(b + 1) * LANES] = _rope(dq_ref[:, sl], cos, sa, sb).astype(BF16)
            o_ref[:, w + b * LANES:w + (b + 1) * LANES] = _rope(dk_ref[:, sl], cos, sa, sb).astype(BF16)
        o_ref[:, 2 * w:3 * w] = dv_ref[...].astype(BF16)
        o_ref[:, 3 * w:] = du_ref[...].astype(BF16)

    tab = pl.BlockSpec((TM, LANES), lambda i: (i, 0))
    tile = pl.BlockSpec((TM, w), lambda i: (i, 0))
    outs, _ = carrier_call(
        core, name=name, grid=(n // TM,), in_specs=[tile, tile, tile, tile, tab, tab, tab],
        out_specs=[pl.BlockSpec((TM, 4 * w), lambda i: (i, 0))],
        out_shape=[jax.ShapeDtypeStruct((n, 4 * w), BF16)], inputs=[dq, dk, dv, du, *tables])
    return outs[0]


def mix_out(na, py, wout, x, g, mod, s, name):
    n, d = x.shape
    tm = _token_tile(n)
    half = na.shape[1]

    def core(na_ref, py_ref, w_ref, x_ref, g_ref, mod_ref, f_ref, xo_ref):
        i = pl.program_id(0)
        ff = _dot(na_ref[...], w_ref[:half, :]) + _dot(py_ref[...], w_ref[half:, :])
        f_ref[...] = ff
        xo_ref[...] = _post(x_ref[...], ff, g_ref[...], mod_ref, 5, 1.0, _is_ctx(i, tm, s))

    tile = pl.BlockSpec((tm, d), lambda i: (i, 0))
    htile = pl.BlockSpec((tm, half), lambda i: (i, 0))
    outs, _ = carrier_call(
        core, name=name, grid=(n // tm,),
        in_specs=[htile, htile, pl.BlockSpec((2 * half, d), lambda i: (0, 0)), tile, _vec_spec(d), _mod_whole(d)],
        out_specs=[tile, tile],
        out_shape=[jax.ShapeDtypeStruct((n, d), F32), jax.ShapeDtypeStruct((n, d), F32)],
        inputs=[na, py, wout, x, g, mod])
    return outs


def grad_wout(na, py, dfm, name):
    n, half = na.shape
    d = dfm.shape[1]
    tk = _pick(n, (1408, 1024, 768, 640, 512, 256))
    nk = n // tk

    def core(na_ref, py_ref, b_ref, o_ref, acc):
        hh, kk = pl.program_id(0), pl.program_id(1)

        @pl.when(kk == 0)
        def _():
            acc[...] = jnp.zeros_like(acc)

        @pl.when(hh == 0)
        def _():
            acc[...] += _dot(na_ref[...], b_ref[...], TN)

        @pl.when(hh == 1)
        def _():
            acc[...] += _dot(py_ref[...], b_ref[...], TN)

        @pl.when(kk == nk - 1)
        def _():
            o_ref[0] = acc[...].astype(BF16)

    htile = pl.BlockSpec((tk, half), lambda hh, kk: (kk, 0))
    outs, _ = carrier_call(
        core, name=name, grid=(2, nk), in_specs=[htile, htile, pl.BlockSpec((tk, d), lambda hh, kk: (kk, 0))],
        out_specs=[pl.BlockSpec((1, half, d), lambda hh, kk: (hh, 0, 0))],
        out_shape=[jax.ShapeDtypeStruct((2, half, d), BF16)],
        scratch_shapes=[pltpu.VMEM((half, d), F32)], inputs=[na, py, dfm])
    return outs[0]


def mix_dh(dqkvu, win8, x, dxo, g, mod, s, name, exchange=None):
    n, d = x.shape
    tm = _token_tile(n)
    nb = win8.shape[0]

    def core(dq_ref, w_ref, x_ref, dxo_ref, g_ref, mod_ref, dx_ref, red_ref):
        i = pl.program_id(0)
        dh = _dot(dq_ref[:, :IN_BLOCK], w_ref[0], NT)
        for j in range(1, nb):
            dh = dh + _dot(dq_ref[:, j * IN_BLOCK:(j + 1) * IN_BLOCK], w_ref[j], NT)
        is_ctx = _is_ctx(i, tm, s)
        dx, sums = _pre_bwd(x_ref[...], dh, dxo_ref[...], g_ref[...], mod_ref, 4, is_ctx)
        dx_ref[...] = dx
        _red_add(red_ref, i == 0, is_ctx, sums)

    tile = pl.BlockSpec((tm, d), lambda i: (i, 0))
    outs, xo = carrier_call(
        core, name=name, grid=(n // tm,),
        in_specs=[pl.BlockSpec((tm, nb * IN_BLOCK), lambda i: (i, 0)),
                  pl.BlockSpec((nb, d, IN_BLOCK), lambda i: (0, 0, 0)), tile, tile, _vec_spec(d), _mod_whole(d)],
        out_specs=[tile, _red_whole(d)],
        out_shape=[jax.ShapeDtypeStruct((n, d), F32), jax.ShapeDtypeStruct((2, 8, d), F32)],
        inputs=[dqkvu, win8, x, dxo, g, mod], exchange=exchange)
    return outs, xo


def _na_consts():
    j = np.arange(GRID_W)
    col_start = np.clip(j - NA_KW // 2, 0, GRID_W - NA_KW)
    valid = (j[None, :] >= col_start[:, None]) & (j[None, :] < col_start[:, None] + NA_KW)
    dc = np.clip(j[None, :] - j[:, None] + NA_KW - 1, 0, 2 * NA_KW - 2)
    onehot = np.zeros((LANES, GRID_W, GRID_W), np.float32)
    for d in range(2 * NA_KW - 1):
        onehot[d] = ((dc == d) & valid).astype(np.float32)
    negmask = np.where(valid, 0.0, NEG_INF).astype(np.float32)
    return onehot.reshape(LANES, GRID_W * GRID_W), np.tile(negmask, (1, NA_KH))


def bias_tables(rpb, name):
    onehot, negmask = _na_consts()
    nj = 2 * NA_KH - 1
    rows = NA_HEADS * nj
    a = jnp.pad(rpb.reshape(rows, 2 * NA_KW - 1), ((0, 0), (0, LANES - (2 * NA_KW - 1))))

    def body(a_ref, e_ref, o_ref):
        o_ref[...] = _dot(a_ref[...], e_ref[...], precision=HIGHEST)

    t = pl.pallas_call(
        body, name=name, out_shape=jax.ShapeDtypeStruct((rows, GRID_W * GRID_W), F32),
        in_specs=[pl.BlockSpec(memory_space=pltpu.VMEM)] * 2,
        out_specs=pl.BlockSpec(memory_space=pltpu.VMEM),
    )(a, jnp.asarray(onehot))
    t = t.reshape(NA_HEADS, nj, GRID_W, GRID_W)
    tb = jnp.stack([t[:, j0:j0 + NA_KH] for j0 in range(NA_KH)])
    tb = tb.transpose(0, 1, 3, 2, 4).reshape(NA_KH, NA_HEADS, GRID_W, NA_KH * GRID_W)
    return tb + jnp.asarray(negmask)[None, None]


def bias_tables_bwd(dtb, name):
    onehot, _ = _na_consts()
    nj = 2 * NA_KH - 1
    d5 = dtb.reshape(NA_KH, NA_HEADS, GRID_W, NA_KH, GRID_W).transpose(0, 3, 1, 2, 4)
    d2 = d5.reshape(NA_KH * NA_KH * NA_HEADS, GRID_W * GRID_W)

    def body(d_ref, e_ref, o_ref):
        r = _dot(d_ref[...], e_ref[...], NT, precision=HIGHEST)
        for j in range(nj):
            acc = jnp.zeros((NA_HEADS, LANES), F32)
            for j0 in range(NA_KH):
                kk = j - j0
                if 0 <= kk < NA_KH:
                    base = (j0 * NA_KH + kk) * NA_HEADS
                    acc = acc + r[base:base + NA_HEADS, :]
            o_ref[j] = acc

    out = pl.pallas_call(
        body, name=name, out_shape=jax.ShapeDtypeStruct((nj, NA_HEADS, LANES), F32),
        in_specs=[pl.BlockSpec(memory_space=pltpu.VMEM)] * 2,
        out_specs=pl.BlockSpec(memory_space=pltpu.VMEM),
        compiler_params=pltpu.CompilerParams(vmem_limit_bytes=VMEM_LIMIT),
    )(d2, jnp.asarray(onehot))
    return out[:, :, :2 * NA_KW - 1].transpose(1, 0, 2)


def _head_masks():
    lane = lax.broadcasted_iota(jnp.int32, (1, LANES), 1)
    return [(lane >= h * HEAD_DIM) & (lane < (h + 1) * HEAD_DIM) for h in range(HEADS_PER_BLOCK)]


def _row_window(r, rows):
    rs = jnp.clip(r - NA_KH // 2, 0, rows - NA_KH)
    return rs - r + NA_KH - 1, pl.multiple_of(rs * GRID_W, GRID_W)


def _stack_heads(t, masks):
    return jnp.concatenate([jnp.where(mk, t, jnp.zeros_like(t)) for mk in masks], axis=0)


def _unstack_heads(t2, masks):
    out = t2[(HEADS_PER_BLOCK - 1) * GRID_W:, :]
    for h in reversed(range(HEADS_PER_BLOCK - 1)):
        out = jnp.where(masks[h], t2[h * GRID_W:(h + 1) * GRID_W, :], out)
    return out


NA_ROWS_PER_STEP = 4
NA_STEP = NA_ROWS_PER_STEP * GRID_W
SLAB = NA_KH * GRID_W
K_COL = N_HEAD_BLOCKS
V_COL = 2 * N_HEAD_BLOCKS


def na_fwd(qkv, tb, s, name, exchange=None):
    n = qkv.shape[0]
    l = n - s
    rows = s // GRID_W
    rr = NA_ROWS_PER_STEP
    x_steps = rows // rr

    def core(q_ref, k_ref, v_ref, kc_ref, vc_ref, tb_ref, o_ref, lse_ref):
        rb = pl.program_id(1)

        @pl.when(rb >= x_steps)
        def _():
            o_ref[...] = jnp.zeros_like(o_ref)
            lse_ref[...] = jnp.zeros_like(lse_ref)

        @pl.when(rb < x_steps)
        def _():
            masks = _head_masks()
            kcb, vcb = kc_ref[...], vc_ref[...]
            wins, scores = [], []
            for t in range(rr):
                j0, off = _row_window(rb * rr + t, rows)
                q2 = _stack_heads(q_ref[t * GRID_W:(t + 1) * GRID_W, :], masks)
                bias = tb_ref[j0].reshape(HEADS_PER_BLOCK * GRID_W, SLAB)
                s_loc = _dot(q2, k_ref[pl.ds(off, SLAB), :], NT) * ATT_SCALE + bias
                s_ctx = _dot(q2, kcb, NT) * ATT_SCALE
                wins.append(off)
                scores.append((s_loc, s_ctx))
            probs = []
            for s_loc, s_ctx in scores:
                m = jnp.maximum(jnp.max(s_loc, axis=-1, keepdims=True), jnp.max(s_ctx, axis=-1, keepdims=True))
                p_loc = jnp.exp(s_loc - m)
                p_ctx = jnp.exp(s_ctx - m)
                den = jnp.sum(p_loc, axis=-1, keepdims=True) + jnp.sum(p_ctx, axis=-1, keepdims=True)
                probs.append((p_loc.astype(BF16), p_ctx.astype(BF16), den, m + jnp.log(den)))
            for t, (p_loc, p_ctx, den, lse2) in enumerate(probs):
                o2 = (_dot(p_loc, v_ref[pl.ds(wins[t], SLAB), :]) + _dot(p_ctx, vcb)) / den
                o_ref[t * GRID_W:(t + 1) * GRID_W, :] = _unstack_heads(o2, masks).astype(BF16)
                lse_ref[0, t * GRID_W:(t + 1) * GRID_W, :] = _unstack_heads(lse2, masks)

    cb = s // l
    outs, xo = carrier_call(
        core, name=name, grid=(N_HEAD_BLOCKS, n // NA_STEP),
        in_specs=[pl.BlockSpec((NA_STEP, LANES), lambda hb, rb: (jnp.minimum(rb, x_steps - 1), hb)),
                  pl.BlockSpec((s, LANES), lambda hb, rb: (0, K_COL + hb)),
                  pl.BlockSpec((s, LANES), lambda hb, rb: (0, V_COL + hb)),
                  pl.BlockSpec((l, LANES), lambda hb, rb: (cb, K_COL + hb)),
                  pl.BlockSpec((l, LANES), lambda hb, rb: (cb, V_COL + hb)),
                  pl.BlockSpec((NA_KH, HEADS_PER_BLOCK, GRID_W, SLAB), lambda hb, rb: (0, hb, 0, 0))],
        out_specs=[pl.BlockSpec((NA_STEP, LANES), lambda hb, rb: (rb, hb)),
                   pl.BlockSpec((1, NA_STEP, LANES), lambda hb, rb: (hb, rb, 0))],
        out_shape=[jax.ShapeDtypeStruct((n, NA_WIDTH), BF16), jax.ShapeDtypeStruct((N_HEAD_BLOCKS, n, LANES), F32)],
        inputs=[qkv, qkv, qkv, qkv, qkv, tb], exchange=exchange)
    return outs, xo


def na_bwd(qkv, tb, o, dmix, lse, s, name, exchange=None):
    n = qkv.shape[0]
    l = n - s
    rows = s // GRID_W
    rr = NA_ROWS_PER_STEP
    x_steps = rows // rr

    def core(q_ref, k_ref, v_ref, kc_ref, vc_ref, tb_ref, o_ref, do_ref, lse_ref, dq_ref, dk_ref, dv_ref, dtb_ref):
        rb = pl.program_id(1)

        @pl.when(rb == 0)
        def _():
            dk_ref[...] = jnp.zeros_like(dk_ref)
            dv_ref[...] = jnp.zeros_like(dv_ref)
            dtb_ref[...] = jnp.zeros_like(dtb_ref)

        @pl.when(rb >= x_steps)
        def _():
            dq_ref[...] = jnp.zeros_like(dq_ref)

        @pl.when(rb < x_steps)
        def _():
            masks = _head_masks()
            kcb, vcb = kc_ref[...], vc_ref[...]
            stage1 = []
            for t in range(rr):
                j0, off = _row_window(rb * rr + t, rows)
                sl = slice(t * GRID_W, (t + 1) * GRID_W)
                q2 = _stack_heads(q_ref[sl, :], masks)
                do_f = do_ref[sl, :]
                do2 = _stack_heads(do_f.astype(BF16), masks)
                dd = do_f * o_ref[sl, :].astype(F32)
                delta2 = jnp.concatenate(
                    [jnp.sum(jnp.where(mk, dd, 0.0), axis=-1, keepdims=True) for mk in masks], axis=0)
                lse_t = lse_ref[0, sl, :]
                lse2 = jnp.concatenate(
                    [lse_t[:, h * HEAD_DIM:h * HEAD_DIM + 1] for h in range(HEADS_PER_BLOCK)], axis=0)
                kslab = k_ref[pl.ds(off, SLAB), :]
                vslab = v_ref[pl.ds(off, SLAB), :]
                bias = tb_ref[j0].reshape(HEADS_PER_BLOCK * GRID_W, SLAB)
                s_loc = _dot(q2, kslab, NT) * ATT_SCALE + bias - lse2
                s_ctx = _dot(q2, kcb, NT) * ATT_SCALE - lse2
                dp_loc = _dot(do2, vslab, NT) - delta2
                dp_ctx = _dot(do2, vcb, NT) - delta2
                stage1.append((j0, off, q2, do2, s_loc, s_ctx, dp_loc, dp_ctx))
            stage2 = []
            for j0, off, q2, do2, s_loc, s_ctx, dp_loc, dp_ctx in stage1:
                p_loc = jnp.exp(s_loc)
                p_ctx = jnp.exp(s_ctx)
                ds_loc = p_loc * dp_loc
                dtb_ref[j0] += ds_loc.reshape(HEADS_PER_BLOCK, GRID_W, SLAB)
                stage2.append((off, q2, do2, p_loc.astype(BF16), p_ctx.astype(BF16), ds_loc.astype(BF16),
                               (p_ctx * dp_ctx).astype(BF16)))
            for t, (off, q2, do2, p_loc, p_ctx, ds_loc, ds_ctx) in enumerate(stage2):
                dq2 = (_dot(ds_loc, k_ref[pl.ds(off, SLAB), :]) + _dot(ds_ctx, kcb)) * ATT_SCALE
                dq_ref[t * GRID_W:(t + 1) * GRID_W, :] = _unstack_heads(dq2, masks)
                dk_ref[pl.ds(off, SLAB), :] += _dot(ds_loc, q2, TN) * ATT_SCALE
                dv_ref[pl.ds(off, SLAB), :] += _dot(p_loc, do2, TN)
                dk_ref[s:, :] += _dot(ds_ctx, q2, TN) * ATT_SCALE
                dv_ref[s:, :] += _dot(p_ctx, do2, TN)

    cb = s // l
    clamp = lambda hb, rb: (jnp.minimum(rb, x_steps - 1), hb)
    tile_in = pl.BlockSpec((NA_STEP, LANES), clamp)
    whole_out = pl.BlockSpec((n, LANES), lambda hb, rb: (0, hb))
    tbs = pl.BlockSpec((NA_KH, HEADS_PER_BLOCK, GRID_W, SLAB), lambda hb, rb: (0, hb, 0, 0))
    f32n = jax.ShapeDtypeStruct((n, NA_WIDTH), F32)
    outs, xo = carrier_call(
        core, name=name, grid=(N_HEAD_BLOCKS, n // NA_STEP),
        in_specs=[tile_in,
                  pl.BlockSpec((s, LANES), lambda hb, rb: (0, K_COL + hb)),
                  pl.BlockSpec((s, LANES), lambda hb, rb: (0, V_COL + hb)),
                  pl.BlockSpec((l, LANES), lambda hb, rb: (cb, K_COL + hb)),
                  pl.BlockSpec((l, LANES), lambda hb, rb: (cb, V_COL + hb)),
                  tbs, tile_in, tile_in,
                  pl.BlockSpec((1, NA_STEP, LANES), lambda hb, rb: (hb, jnp.minimum(rb, x_steps - 1), 0))],
        out_specs=[pl.BlockSpec((NA_STEP, LANES), lambda hb, rb: (rb, hb)), whole_out, whole_out, tbs],
        out_shape=[f32n, f32n, f32n, jax.ShapeDtypeStruct((NA_KH, NA_HEADS, GRID_W, SLAB), F32)],
        inputs=[qkv, qkv, qkv, qkv, qkv, tb, o, dmix, lse], exchange=exchange)
    return outs, xo


def ctx_attn_fwd(qkv, na, s, name):
    n = qkv.shape[0]
    l = n - s
    cb = s // l

    def core(q_ref, k_ref, v_ref, na_in, o_ref, lse_ref):
        masks = _head_masks()
        qt, kb, vb = q_ref[...], k_ref[...], v_ref[...]
        o_acc = jnp.zeros((l, LANES), F32)
        lse_acc = jnp.zeros((l, LANES), F32)
        for h in range(HEADS_PER_BLOCK):
            qh = jnp.where(masks[h], qt, jnp.zeros_like(qt))
            sc = _dot(qh, kb, NT) * ATT_SCALE
            m = jnp.max(sc, axis=-1, keepdims=True)
            p = jnp.exp(sc - m)
            den = jnp.sum(p, axis=-1, keepdims=True)
            o_acc = jnp.where(masks[h], _dot(p.astype(BF16), vb) / den, o_acc)
            lse_acc = jnp.where(masks[h], m + jnp.log(den), lse_acc)
        o_ref[...] = o_acc.astype(BF16)
        lse_ref[0] = lse_acc

    outs, _ = carrier_call(
        core, name=name, grid=(N_HEAD_BLOCKS,),
        in_specs=[pl.BlockSpec((l, LANES), lambda hb: (cb, hb)), pl.BlockSpec((l, LANES), lambda hb: (cb, K_COL + hb)),
                  pl.BlockSpec((l, LANES), lambda hb: (cb, V_COL + hb)), ANY],
        out_specs=[pl.BlockSpec((l, LANES), lambda hb: (cb, hb)), pl.BlockSpec((1, l, LANES), lambda hb: (hb, 0, 0))],
        out_shape=[jax.ShapeDtypeStruct(na.shape, BF16), jax.ShapeDtypeStruct((N_HEAD_BLOCKS, l, LANES), F32)],
        inputs=[qkv, qkv, qkv, na], aliases={3: 0})
    return outs


def ctx_attn_bwd(qkv, na, dmix, lse, dq, dk, dv, s, name):
    n = qkv.shape[0]
    l = n - s
    cb = s // l

    def core(q_ref, k_ref, v_ref, o_ref, do_ref, lse_ref, dq_in, dk_in, dv_in, dq_ref, dk_ref, dv_ref):
        masks = _head_masks()
        qt, kb, vb = q_ref[...], k_ref[...], v_ref[...]
        do_f = do_ref[...]
        dd = do_f * o_ref[...].astype(F32)
        do_b = do_f.astype(BF16)
        lse_t = lse_ref[0]
        dq_acc = jnp.zeros((l, LANES), F32)
        dk_acc = jnp.zeros((l, LANES), F32)
        dv_acc = jnp.zeros((l, LANES), F32)
        for h in range(HEADS_PER_BLOCK):
            qh = jnp.where(masks[h], qt, jnp.zeros_like(qt))
            doh = jnp.where(masks[h], do_b, jnp.zeros_like(do_b))
            delta = jnp.sum(jnp.where(masks[h], dd, 0.0), axis=-1, keepdims=True)
            p = jnp.exp(_dot(qh, kb, NT) * ATT_SCALE - lse_t[:, h * HEAD_DIM:h * HEAD_DIM + 1])
            ds = (p * (_dot(doh, vb, NT) - delta)).astype(BF16)
            dq_acc = jnp.where(masks[h], _dot(ds, kb) * ATT_SCALE, dq_acc)
            dk_acc = dk_acc + _dot(ds, qh, TN)
            dv_acc = dv_acc + _dot(p.astype(BF16), doh, TN)
        dq_ref[...] = dq_acc
        dk_ref[...] = dk_in[...] + dk_acc * ATT_SCALE
        dv_ref[...] = dv_in[...] + dv_acc

    blk = pl.BlockSpec((l, LANES), lambda hb: (cb, hb))
    f32n = jax.ShapeDtypeStruct((n, NA_WIDTH), F32)
    outs, _ = carrier_call(
        core, name=name, grid=(N_HEAD_BLOCKS,),
        in_specs=[blk, pl.BlockSpec((l, LANES), lambda hb: (cb, K_COL + hb)),
                  pl.BlockSpec((l, LANES), lambda hb: (cb, V_COL + hb)), blk, blk,
                  pl.BlockSpec((1, l, LANES), lambda hb: (hb, 0, 0)), ANY, blk, blk],
        out_specs=[blk, blk, blk], out_shape=[f32n, f32n, f32n],
        inputs=[qkv, qkv, qkv, na, dmix, lse, dq, dk, dv], aliases={6: 0, 7: 1, 8: 2})
    return outs


def _pool_consts(l):
    assert l == TM
    mem = np.zeros((2, POOL_GROUPS, TM, TM), np.float32)
    inv = np.zeros((2, POOL_GROUPS, TM, LANES), np.float32)
    for which, length in ((0, GRID_W), (1, l)):
        t = np.arange(length)
        for g, w in enumerate(POOL_WINDOWS):
            lo = np.clip(t - w // 2, 0, length)
            hi = np.clip(t - w // 2 + w, 0, length)
            blockm = ((t[None, :] >= lo[:, None]) & (t[None, :] < hi[:, None])).astype(np.float32)
            cnt = (hi - lo).astype(np.float32)
            for b in range(TM // length):
                mem[which, g, b * length:(b + 1) * length, b * length:(b + 1) * length] = blockm
                inv[which, g, b * length:(b + 1) * length, :] = (1.0 / cnt)[:, None]
    return mem, np.ascontiguousarray(mem.transpose(0, 1, 3, 2)), inv


def _split_dot(m01, val):
    hi = val.astype(BF16)
    lo = (val - hi.astype(F32)).astype(BF16)
    return _dot(m01, hi) + _dot(m01, lo)


def pool_fwd(u, mem, inv, wp, scale, nx_tiles, name):
    n = u.shape[0]

    def core(u_ref, m_ref, i_ref, wp_ref, s_ref, o_ref):
        for g in range(POOL_GROUPS):
            sl = slice(g * POOL_CH, (g + 1) * POOL_CH)
            ug = u_ref[:, sl]
            dg = _split_dot(m_ref[0, g], ug) * i_ref[0, g] - ug
            o_ref[:, sl] = (_dot(dg.astype(BF16), wp_ref[g]) * s_ref[:, sl]).astype(BF16)

    grp = lambda i: (i // nx_tiles, 0, 0, 0)
    outs, _ = carrier_call(
        core, name=name, grid=(n // TM,),
        in_specs=[pl.BlockSpec((TM, POOL_WIDTH), lambda i: (i, 0)),
                  pl.BlockSpec((1, POOL_GROUPS, TM, TM), grp),
                  pl.BlockSpec((1, POOL_GROUPS, TM, LANES), grp),
                  pl.BlockSpec((POOL_GROUPS, POOL_CH, POOL_CH), lambda i: (0, 0, 0)),
                  pl.BlockSpec((1, POOL_WIDTH), lambda i: (0, 0))],
        out_specs=[pl.BlockSpec((TM, POOL_WIDTH), lambda i: (i, 0))],
        out_shape=[jax.ShapeDtypeStruct((n, POOL_WIDTH), BF16)], inputs=[u, mem, inv, wp, scale])
    return outs[0]


def pool_bwd(dmix, u, mem, mem_t, inv, wp, scale, nx_tiles, name):
    n = u.shape[0]

    def core(dy_ref, u_ref, m_ref, mt_ref, i_ref, wp_ref, s_ref, du_ref, dwp_ref, dsc_ref):
        @pl.when(pl.program_id(0) == 0)
        def _():
            dwp_ref[...] = jnp.zeros_like(dwp_ref)
            dsc_ref[...] = jnp.zeros_like(dsc_ref)

        for g in range(POOL_GROUPS):
            sl = slice(g * POOL_CH, (g + 1) * POOL_CH)
            ug = u_ref[:, sl]
            dy = dy_ref[:, sl]
            dg = (_split_dot(m_ref[0, g], ug) * i_ref[0, g] - ug).astype(BF16)
            z = _dot(dg, wp_ref[g])
            dsc_ref[0:1, sl] += jnp.sum(dy * z, axis=0, keepdims=True)
            dz = (dy * s_ref[:, sl]).astype(BF16)
            dwp_ref[g] += _dot(dg, dz, TN)
            dd = _dot(dz, wp_ref[g], NT)
            du_ref[:, sl] = _split_dot(mt_ref[0, g], dd * i_ref[0, g]) - dd

    grp = lambda i: (i // nx_tiles, 0, 0, 0)
    outs, _ = carrier_call(
        core, name=name, grid=(n // TM,),
        in_specs=[pl.BlockSpec((TM, POOL_WIDTH), lambda i: (i, 1)),
                  pl.BlockSpec((TM, POOL_WIDTH), lambda i: (i, 0)),
                  pl.BlockSpec((1, POOL_GROUPS, TM, TM), grp),
                  pl.BlockSpec((1, POOL_GROUPS, TM, TM), grp),
                  pl.BlockSpec((1, POOL_GROUPS, TM, LANES), grp),
                  pl.BlockSpec((POOL_GROUPS, POOL_CH, POOL_CH), lambda i: (0, 0, 0)),
                  pl.BlockSpec((1, POOL_WIDTH), lambda i: (0, 0))],
        out_specs=[pl.BlockSpec((TM, POOL_WIDTH), lambda i: (i, 0)),
                   pl.BlockSpec((POOL_GROUPS, POOL_CH, POOL_CH), lambda i: (0, 0, 0)),
                   pl.BlockSpec((8, POOL_WIDTH), lambda i: (0, 0))],
        out_shape=[jax.ShapeDtypeStruct((n, POOL_WIDTH), F32),
                   jax.ShapeDtypeStruct((POOL_GROUPS, POOL_CH, POOL_CH), F32),
                   jax.ShapeDtypeStruct((8, POOL_WIDTH), F32)],
        inputs=[dmix, u, mem, mem_t, inv, wp, scale])
    return outs


MOD_ROWS = 16


def mod_fwd(cvecs, w, b, name):
    _, d = cvecs.shape
    cl = w.shape[2]
    tc = _pick(cl, (384, 128))

    def core(c_ref, w_ref, b_ref, o_ref):
        a = _silu(c_ref[...]).astype(BF16)
        o_ref[0] = _dot(a, w_ref[0].astype(BF16)) + b_ref[0]

    outs, _ = carrier_call(
        core, name=name, grid=(DEPTH, cl // tc),
        in_specs=[pl.BlockSpec((MOD_ROWS, d), lambda li, j: (0, 0)),
                  pl.BlockSpec((1, d, tc), lambda li, j: (li, 0, j)),
                  pl.BlockSpec((1, 1, tc), lambda li, j: (li, 0, j))],
        out_specs=[pl.BlockSpec((1, MOD_ROWS, tc), lambda li, j: (li, 0, j))],
        out_shape=[jax.ShapeDtypeStruct((DEPTH, MOD_ROWS, cl), F32)], inputs=[cvecs, w, b])
    return outs[0]


def mod_bwd(cvecs, dm, w, name):
    _, d = cvecs.shape
    cl = w.shape[2]
    tc = _pick(cl, (384, 128))

    def core(c_ref, dm_ref, w_ref, dw_ref, da_ref):
        @pl.when((pl.program_id(0) == 0) & (pl.program_id(1) == 0))
        def _():
            da_ref[...] = jnp.zeros_like(da_ref)

        a = _silu(c_ref[...]).astype(BF16)
        dmb = dm_ref[0].astype(BF16)
        dw_ref[0] = _dot(a, dmb, TN)
        da_ref[...] += _dot(dmb, w_ref[0].astype(BF16), NT)

    outs, _ = carrier_call(
        core, name=name, grid=(DEPTH, cl // tc),
        in_specs=[pl.BlockSpec((MOD_ROWS, d), lambda li, j: (0, 0)),
                  pl.BlockSpec((1, MOD_ROWS, tc), lambda li, j: (li, 0, j)),
                  pl.BlockSpec((1, d, tc), lambda li, j: (li, 0, j))],
        out_specs=[pl.BlockSpec((1, d, tc), lambda li, j: (li, 0, j)),
                   pl.BlockSpec((MOD_ROWS, d), lambda li, j: (0, 0))],
        out_shape=[jax.ShapeDtypeStruct((DEPTH, d, cl), F32), jax.ShapeDtypeStruct((MOD_ROWS, d), F32)],
        inputs=[cvecs, dm, w])
    return outs


def loss_head(y, target, name):
    n, d = y.shape
    s = target.shape[0]
    nt, nx = n // TM, s // TM

    def core(y_ref, t_ref, l_ref, dy_ref, acc_ref):
        i = pl.program_id(0)

        @pl.when(i == 0)
        def _():
            acc_ref[...] = jnp.zeros_like(acc_ref)

        @pl.when(i < nx)
        def _():
            e = y_ref[...] - t_ref[...]
            dy_ref[...] = e * (1.0 / d)
            acc_ref[...] += jnp.sum(e * e, axis=0, keepdims=True)

        @pl.when(i >= nx)
        def _():
            dy_ref[...] = jnp.zeros_like(dy_ref)

        @pl.when(i == nt - 1)
        def _():
            l_ref[...] = jnp.sum(acc_ref[...], axis=1, keepdims=True) * (0.5 / d)

    tile = pl.BlockSpec((TM, d), lambda i: (i, 0))
    outs, _ = carrier_call(
        core, name=name, grid=(nt,),
        in_specs=[tile, pl.BlockSpec((TM, d), lambda i: (jnp.minimum(i, nx - 1), 0))],
        out_specs=[pl.BlockSpec((1, 1), lambda i: (0, 0)), tile],
        out_shape=[jax.ShapeDtypeStruct((1, 1), F32), jax.ShapeDtypeStruct((n, d), F32)],
        scratch_shapes=[pltpu.VMEM((1, d), F32)], inputs=[y, target])
    return outs


def sum_devices(v, name):
    _, r, c = v.shape
    tr = _pick(r, (64, 8))

    def core(v_ref, o_ref):
        acc = v_ref[0]
        for p in range(1, N_DEV):
            acc = acc + v_ref[p]
        o_ref[...] = acc

    outs, _ = carrier_call(
        core, name=name, grid=(r // tr,), in_specs=[pl.BlockSpec((N_DEV, tr, c), lambda i: (0, i, 0))],
        out_specs=[pl.BlockSpec((tr, c), lambda i: (i, 0))], out_shape=[jax.ShapeDtypeStruct((r, c), F32)],
        inputs=[v])
    return outs[0]


def cctx_grad(parts, c_ctx, name):
    d = c_ctx.shape[1]

    def body(p_ref, c_ref, o_ref):
        acc = p_ref[0]
        for p in range(1, N_DEV):
            acc = acc + p_ref[p]
        o_ref[...] = acc[8:9, :] * _dsilu(c_ref[...])

    return pl.pallas_call(
        body, name=name, out_shape=jax.ShapeDtypeStruct((1, d), F32),
        in_specs=[pl.BlockSpec(memory_space=pltpu.VMEM)] * 2,
        out_specs=pl.BlockSpec(memory_space=pltpu.VMEM),
    )(parts, c_ctx)


def _adam_math(w, g, m, v):
    m2 = ADAM_B1 * m + (1.0 - ADAM_B1) * g
    v2 = ADAM_B2 * v + (1.0 - ADAM_B2) * (g * g)
    m_hat = m2 / (1.0 - ADAM_B1 ** ADAM_STEP)
    v_hat = v2 / (1.0 - ADAM_B2 ** ADAM_STEP)
    delta = -ADAM_LR * (m_hat / (jnp.sqrt(v_hat) + ADAM_EPS) + ADAM_WD * w)
    return delta, m2, v2


def adamw(w, g, m, v, name):
    r, c = w.shape
    tr = _pick(r, (256, 128, 64, 32, 16, 8, r))

    def core(w_ref, g_ref, m_ref, v_ref, d_ref, m2_ref, v2_ref):
        d_ref[...], m2_ref[...], v2_ref[...] = _adam_math(w_ref[...], g_ref[...], m_ref[...], v_ref[...])

    tile = pl.BlockSpec((tr, c), lambda i: (i, 0))
    out = jax.ShapeDtypeStruct((r, c), F32)
    outs, _ = carrier_call(core, name=name, grid=(r // tr,), in_specs=[tile] * 4, out_specs=[tile] * 3,
                           out_shape=[out] * 3, inputs=[w, g, m, v])
    return outs


def reduce_adamw(recv, w, m, v, name):
    r, c = w.shape
    tr = _pick(r, (256, 128, 64, 8))

    def core(recv_ref, w_ref, m_ref, v_ref, g_ref, d_ref, m2_ref, v2_ref):
        acc = recv_ref[0].astype(F32)
        for p in range(1, N_DEV):
            acc = acc + recv_ref[p].astype(F32)
        g_ref[...] = acc
        d_ref[...], m2_ref[...], v2_ref[...] = _adam_math(w_ref[...], acc, m_ref[...], v_ref[...])

    tile = pl.BlockSpec((tr, c), lambda i: (i, 0))
    out = jax.ShapeDtypeStruct((r, c), F32)
    outs, _ = carrier_call(
        core, name=name, grid=(r // tr,),
        in_specs=[pl.BlockSpec((N_DEV, tr, c), lambda i: (0, i, 0)), tile, tile, tile],
        out_specs=[tile] * 4, out_shape=[out] * 4, inputs=[recv, w, m, v])
    return outs


def kernel(x, c, ctx, c_ctx, w_mod, b_mod, norm_g, w_ffn_gate_up, w_ffn_down, w_in, w_out, na_rpb, w_pool, pool_scale, loss_target, m_c_ctx, m_w_mod, m_b_mod, m_norm_g, m_w_ffn_gate_up, m_w_ffn_down, m_w_in, m_w_out, m_na_rpb, m_w_pool, m_pool_scale, v_c_ctx, v_w_mod, v_b_mod, v_norm_g, v_w_ffn_gate_up, v_w_ffn_down, v_w_in, v_w_out, v_na_rpb, v_w_pool, v_pool_scale):
    s, d = x.shape[1], x.shape[2]
    l = ctx.shape[1]
    n = s + l
    nx = s // TM
    fq = w_ffn_gate_up.shape[-1]
    fr = w_ffn_down.shape[2]
    cl = w_mod.shape[2]
    dl = norm_g.shape[2]
    me = 4 * lax.axis_index("x") + 2 * lax.axis_index("y") + lax.axis_index("c")

    c_all = all_gather(c, "gather_c").reshape(N_DEV, d)
    cvecs = jnp.concatenate([c_all, c_ctx[None, :], jnp.zeros((MOD_ROWS - N_DEV - 1, d), F32)], axis=0)
    b_loc = lax.dynamic_slice(b_mod, (0, me * cl), (DEPTH, cl)).reshape(DEPTH, 1, cl)
    mod_loc = mod_fwd(cvecs, w_mod, b_loc, "mod_fwd")
    mod_all = all_gather(mod_loc.reshape(DEPTH * MOD_ROWS, cl), "gather_mod")
    mod_all = mod_all.reshape(N_DEV, DEPTH, MOD_ROWS, cl).transpose(1, 2, 0, 3).reshape(DEPTH, MOD_ROWS, N_DEV * cl)
    mine = lax.dynamic_slice(mod_all, (0, me, 0), (DEPTH, 1, N_DEV * cl))
    mods = jnp.concatenate([mine, mod_all[:, N_DEV:N_DEV + 1]], axis=1).reshape(DEPTH, 2, N_MOD, d)

    gu_b = w_ffn_gate_up.astype(BF16)
    dn_b = w_ffn_down.astype(BF16)
    wi_b = w_in.astype(BF16)
    wo_b = w_out.astype(BF16)
    wp_b = w_pool.astype(BF16)

    def ffn_shards(li, i):
        return [gu_b[li, i], dn_b[li, i]]

    def mix_shards(li):
        return [wi_b[li], wo_b[li]]

    def as_ffn_weights(gathered):
        return gathered[0].reshape(2, 4, d, fq), gathered[1].reshape(4, 2 * fr, d)

    def as_mix_weights(gathered):
        return gathered[0], gathered[1].reshape(N_DEV * wo_b.shape[1], d)

    tables = _rope_tables(s, n)
    mem_np, mem_t_np, inv_np = _pool_consts(l)
    mem, mem_t, inv = jnp.asarray(mem_np, BF16), jnp.asarray(mem_t_np, BF16), jnp.asarray(inv_np)
    first = exchange_only(Exchange(gathers=[norm_g.reshape(DEPTH * 6, dl), gu_b[0, 0]]), "gather_first")
    g_full = first[0].reshape(N_DEV, DEPTH, 6, dl).transpose(1, 2, 0, 3).reshape(DEPTH, 6, 1, N_DEV * dl)

    weights = {("ffn", 0, 0): (first[1].reshape(2, 4, d, fq), None)}
    saved = {}
    xcur = jnp.concatenate([x[0], ctx[0]], axis=0)
    for li in range(DEPTH):
        last = li == DEPTH - 1
        for i in range(2):
            tag = f"l{li}_ffn{i}"
            wgu, wd4 = weights[("ffn", li, i)]
            if i == 0:
                ex_up, ex_dn = Exchange(gathers=mix_shards(li) + ([dn_b[0, 0]] if wd4 is None else [])), None
            elif not last:
                ex_up, ex_dn = Exchange(gathers=[gu_b[li + 1, 0]]), Exchange(gathers=[dn_b[li + 1, 0]])
            else:
                ex_up = ex_dn = None
            (hb, gu, a4), got_up = ffn_up(xcur, g_full[li, 4 * i], mods[li], wgu, s, 6 * i, tag + "_up", ex_up)
            if wd4 is None:
                wd4 = got_up.pop().reshape(4, 2 * fr, d)
                weights[("ffn", li, i)] = (wgu, wd4)
            (ff, xnext), got_dn = ffn_down(a4, wd4, xcur, g_full[li, 4 * i + 1], mods[li], s, 6 * i + 2, tag + "_down", ex_dn)
            saved[("ffn", li, i)] = (xcur, hb, gu, a4, ff)
            xcur = xnext
            if i == 0:
                weights[("mix", li)] = as_mix_weights(got_up)
            elif not last:
                weights[("ffn", li + 1, 0)] = as_ffn_weights(got_up + got_dn)
            if i == 0:
                tag = f"l{li}_mix"
                win8, wout = weights[("mix", li)]
                hb, qkv, u = mix_in(xcur, g_full[li, 2], mods[li], win8, tables, s, tag + "_in")
                tb = bias_tables(na_rpb[li], tag + "_bias")
                (na, lse), got = na_fwd(qkv, tb, s, tag + "_na", Exchange(gathers=ffn_shards(li, 1)))
                weights[("ffn", li, 1)] = as_ffn_weights(got)
                lse_c = None
                if not last:
                    na, lse_c = ctx_attn_fwd(qkv, na, s, tag + "_ctx_attn")
                py = pool_fwd(u, mem, inv, wp_b[li], pool_scale[li][None, :], nx, tag + "_pool")
                fm, xnext = mix_out(na, py, wout, xcur, g_full[li, 3], mods[li], s, tag + "_out")
                saved[("mix", li)] = (xcur, hb, qkv, u, tb, na, lse, lse_c, py, fm)
                xcur = xnext

    loss_local, dcur = loss_head(xcur, loss_target[0], "loss")
    loss = lax.psum(loss_local[0, 0], ("x", "y", "c"))

    recv = {"gu": lax.empty((N_DEV, 2 * DEPTH, d, fq), BF16), "dn": lax.empty((N_DEV, 2 * DEPTH, fr, d), BF16),
            "wi": lax.empty((N_DEV, DEPTH, d, IN_BLOCK), BF16), "wo": lax.empty((N_DEV, DEPTH, wo_b.shape[1], d), BF16)}
    pending = []

    def take(keys):
        nonlocal pending
        jobs = [(gr, recv[key], st) for key, gr, st in pending if key in keys]
        order = [key for key, _, _ in pending if key in keys]
        pending = [p for p in pending if p[0] not in keys]
        return Exchange(a2as=jobs), order

    def put(order, bufs):
        for key, buf in zip(order, bufs):
            recv[key] = buf

    d_rpb, d_wp, d_ps, d_mod, d_g = [], [], [], [], []
    for li in reversed(range(DEPTH)):
        reds = {}
        for i in (1, 0):
            tag = f"l{li}_ffn{i}"
            xin, hb, gu, a4, ff = saved[("ffn", li, i)]
            wgu, wd4 = weights[("ffn", li, i)]
            dff, red1 = post_bwd(ff, dcur, g_full[li, 4 * i + 1], mods[li], 6 * i + 2, 0.5, s, tag + "_post_bwd")
            g_dn = grad_weight(a4, dff, tag + "_dwdown", a_lead=4).reshape(N_DEV, fr, d)
            pending += [("dn", g_dn, 2 * li + i)]
            ex, order = take(("dn", "wi", "wo"))
            dgu, bufs = ffn_da(dff, wd4, gu, tag + "_da", ex)
            put(order, bufs)
            g_gu = grad_weight(hb, dgu.reshape(N_DEV, n, fq), tag + "_dwgu", b_lead=N_DEV)
            pending += [("gu", g_gu, 2 * li + i)]
            ex, order = take(("gu",))
            (dcur, red2), bufs = ffn_dh(dgu, wgu, xin, dcur, g_full[li, 4 * i], mods[li], s, 6 * i, tag + "_dh", ex)
            put(order, bufs)
            reds[i] = (red1, red2)
            if i == 1:
                tag = f"l{li}_mix"
                xin, hb, qkv, u, tb, na, lse, lse_c, py, fm = saved[("mix", li)]
                win8, wout = weights[("mix", li)]
                dfm, redm1 = post_bwd(fm, dcur, g_full[li, 3], mods[li], 5, 1.0, s, tag + "_post_bwd")
                dmix = matmul_nt(dfm, wout, tag + "_dmix")
                g_wo = grad_wout(na, py, dfm, tag + "_dwout").reshape(N_DEV, wo_b.shape[1], d)
                du, gwp, gps = pool_bwd(dmix, u, mem, mem_t, inv, wp_b[li], pool_scale[li][None, :], nx, tag + "_pool_bwd")
                ex, order = take(("gu", "dn"))
                (dq, dk, dv, dtb), bufs = na_bwd(qkv, tb, na, dmix, lse, s, tag + "_na_bwd", ex)
                put(order, bufs)
                if li != DEPTH - 1:
                    dq, dk, dv = ctx_attn_bwd(qkv, na, dmix, lse_c, dq, dk, dv, s, tag + "_ctx_attn_bwd")
                grpb = bias_tables_bwd(dtb, tag + "_bias_bwd")
                dqkvu = qkv_bwd(dq, dk, dv, du, tables, tag + "_rope_bwd")
                (dcur, redm2), _ = mix_dh(dqkvu, win8, xin, dcur, g_full[li, 2], mods[li], s, tag + "_dh")
                g_wi = grad_weight(hb, dqkvu, tag + "_dwin", b_cols=IN_BLOCK)
                pending += [("wi", g_wi, li), ("wo", g_wo, li)]
                d_rpb.insert(0, grpb)
                d_wp.insert(0, gwp)
                d_ps.insert(0, gps[0])
        (ra1, ra2), (rb1, rb2) = reds[0], reds[1]
        d_mod.insert(0, jnp.stack([ra2[:, 0], ra2[:, 1], ra1[:, 0], redm2[:, 0], redm2[:, 1], redm1[:, 0],
                                   rb2[:, 0], rb2[:, 1], rb1[:, 0]], axis=1))
        d_g.insert(0, jnp.stack([t[0] + t[1] for t in (ra2[:, 2], ra1[:, 1], redm2[:, 2], redm1[:, 1], rb2[:, 2], rb1[:, 1])]))
    grad_x = dcur[:s][None]

    def pad8(t):
        t = t.reshape(-1, d) if t.size % d == 0 else jnp.pad(t.reshape(-1), (0, -t.size % d)).reshape(-1, d)
        return jnp.pad(t, ((0, -t.shape[0] % 8), (0, 0)))

    small_parts = [pad8(jnp.stack(d_mod)), pad8(jnp.stack(d_g)), pad8(jnp.stack(d_wp)), pad8(jnp.stack(d_ps)),
                   pad8(jnp.stack(d_rpb))]
    offs = np.cumsum([0] + [p.shape[0] for p in small_parts])
    small_parts.append(jnp.zeros((-offs[-1] % 64, d), F32))
    small = jnp.concatenate(small_parts, axis=0)
    ex, order = take(("gu", "dn", "wi", "wo"))
    ex.gathers = [small]
    ex.n_jobs += 1
    bufs = exchange_only(ex, "exchange_last")
    small_all = bufs[0]
    put(order, bufs[1:])
    small_sum = sum_devices(small_all, "sum_small_grads")

    n_mod_rows = DEPTH * 2 * N_MOD
    dmod_all = small_all[:, :n_mod_rows].reshape(N_DEV, DEPTH, 2, N_MOD * d)
    dmod_sum = small_sum[:n_mod_rows].reshape(DEPTH, 2, N_MOD * d)
    dm_rows = jnp.concatenate([dmod_all[:, :, 0].transpose(1, 0, 2), dmod_sum[:, 1:2],
                               jnp.zeros((DEPTH, MOD_ROWS - N_DEV - 1, N_MOD * d), F32)], axis=1)
    grad_b_mod = dmod_sum[:, 0] + dmod_sum[:, 1]
    dm_loc = lax.dynamic_slice(dm_rows, (0, 0, me * cl), (DEPTH, MOD_ROWS, cl))
    grad_w_mod, da_part = mod_bwd(cvecs, dm_loc, w_mod, "mod_bwd")
    da_all = all_gather(da_part, "gather_dcvec")
    grad_c_ctx = cctx_grad(da_all, c_ctx[None, :], "c_ctx_grad")[0]

    grad_norm_full = small_sum[offs[1]:offs[1] + DEPTH * 6].reshape(DEPTH, 6, d)
    grad_norm_g = lax.dynamic_slice(grad_norm_full, (0, 0, me * dl), (DEPTH, 6, dl))
    grad_w_pool = small_sum[offs[2]:offs[2] + w_pool.size // d].reshape(w_pool.shape)
    grad_pool_scale = small_sum[offs[3]:offs[3] + pool_scale.size // d].reshape(pool_scale.shape)
    grad_na_rpb = small_sum[offs[4]:offs[5]].reshape(-1)[:na_rpb.size].reshape(na_rpb.shape)

    def big_adam(key, w, m, v, name):
        shp = w.shape
        cols = shp[-1]
        outs = reduce_adamw(recv[key].reshape(N_DEV, -1, cols), w.reshape(-1, cols), m.reshape(-1, cols),
                            v.reshape(-1, cols), name)
        return tuple(t.reshape(shp) for t in outs)

    def small_adam(w, g, m, v, name):
        shp = w.shape
        cols = shp[-1]
        outs = adamw(w.reshape(-1, cols), g.reshape(-1, cols), m.reshape(-1, cols), v.reshape(-1, cols), name)
        return tuple(t.reshape(shp) for t in outs)

    b_gu = big_adam("gu", w_ffn_gate_up, m_w_ffn_gate_up, v_w_ffn_gate_up, "adam_gate_up")
    b_dn = big_adam("dn", w_ffn_down, m_w_ffn_down, v_w_ffn_down, "adam_down")
    b_wi = big_adam("wi", w_in, m_w_in, v_w_in, "adam_w_in")
    b_wo = big_adam("wo", w_out, m_w_out, v_w_out, "adam_w_out")
    a_cc = small_adam(c_ctx, grad_c_ctx, m_c_ctx, v_c_ctx, "adam_c_ctx")
    a_wm = small_adam(w_mod, grad_w_mod, m_w_mod, v_w_mod, "adam_w_mod")
    a_bm = small_adam(b_mod, grad_b_mod, m_b_mod, v_b_mod, "adam_b_mod")
    a_ng = small_adam(norm_g, grad_norm_g, m_norm_g, v_norm_g, "adam_norm_g")
    a_rp = small_adam(na_rpb, grad_na_rpb, m_na_rpb, v_na_rpb, "adam_na_rpb")
    a_wp = small_adam(w_pool, grad_w_pool, m_w_pool, v_w_pool, "adam_w_pool")
    a_ps = small_adam(pool_scale, grad_pool_scale, m_pool_scale, v_pool_scale, "adam_pool_scale")

    grads = (grad_c_ctx, grad_w_mod, grad_b_mod, grad_norm_g, b_gu[0], b_dn[0], b_wi[0], b_wo[0], grad_na_rpb, grad_w_pool, grad_pool_scale)
    deltas = (a_cc[0], a_wm[0], a_bm[0], a_ng[0], b_gu[1], b_dn[1], b_wi[1], b_wo[1], a_rp[0], a_wp[0], a_ps[0])
    new_m = (a_cc[1], a_wm[1], a_bm[1], a_ng[1], b_gu[2], b_dn[2], b_wi[2], b_wo[2], a_rp[1], a_wp[1], a_ps[1])
    new_v = (a_cc[2], a_wm[2], a_bm[2], a_ng[2], b_gu[3], b_dn[3], b_wi[3], b_wo[3], a_rp[2], a_wp[2], a_ps[2])
    return (loss, grad_x, *grads, *deltas, *new_m, *new_v)
```

```python
import functools
import math

import numpy as np
import jax
import jax.numpy as jnp
from jax import lax
from jax.experimental import pallas as pl
from jax.experimental.pallas import tpu as pltpu

F32 = jnp.float32
BF16 = jnp.bfloat16

N_DEV = 8
DEPTH = 2
GRID_W = 64
N_MOD = 9
NA_HEADS = 8
HEAD_DIM = 64
NA_WIDTH = NA_HEADS * HEAD_DIM
NA_KH = 8
NA_KW = 16
POOL_GROUPS = 4
POOL_CH = 128
POOL_WIDTH = POOL_GROUPS * POOL_CH
POOL_WINDOWS = (2, 4, 8, 16)
ROPE_THETA = 10000.0
ROPE_PAIRS = HEAD_DIM // 4
RMS_EPS = 1e-6
NEG_INF = -1e30
ATT_SCALE = HEAD_DIM ** -0.5

ADAM_LR = 0.001
ADAM_B1 = 0.9
ADAM_B2 = 0.999
ADAM_EPS = 1e-08
ADAM_WD = 0.01
ADAM_STEP = 10

TM = 256
LANES = 128
HEADS_PER_BLOCK = LANES // HEAD_DIM
N_HEAD_BLOCKS = NA_WIDTH // LANES
IN_BLOCK = 2 * LANES
N_QKV_BLOCKS = 3 * NA_WIDTH // IN_BLOCK
VMEM_LIMIT = 48 * 1024 * 1024
HIGHEST = lax.Precision.HIGHEST
MESH = pl.DeviceIdType.MESH
ANY = pl.BlockSpec(memory_space=pl.ANY)

NN = (((1,), (0,)), ((), ()))
NT = (((1,), (1,)), ((), ()))
TN = (((0,), (0,)), ((), ()))


def _pick(n, cands):
    for t in cands:
        if n % t == 0:
            return t
    raise ValueError(f"no tile for {n} among {cands}")


def _dot(a, b, dn=NN, precision=None):
    return lax.dot_general(a, b, dn, preferred_element_type=F32, precision=precision)


def _silu(x):
    return x * jax.nn.sigmoid(x)


def _dsilu(x):
    s = jax.nn.sigmoid(x)
    return s * (1.0 + x * (1.0 - s))


def _peer(mask):
    x, y, c = lax.axis_index("x"), lax.axis_index("y"), lax.axis_index("c")
    px = 1 - x if mask & 4 else x
    py = 1 - y if mask & 2 else y
    pc = 1 - c if mask & 1 else c
    return (px, py, pc), 4 * px + 2 * py + pc


class Exchange:
    def __init__(self, gathers=(), a2as=()):
        self.gathers = list(gathers)
        self.a2as = list(a2as)
        self.n_jobs = len(self.gathers) + len(self.a2as)

    def inputs(self):
        out = list(self.gathers)
        for v, buf, _ in self.a2as:
            out += [v, buf]
        return out

    def out_shapes(self):
        shapes = [jax.ShapeDtypeStruct((N_DEV,) + v.shape, v.dtype) for v in self.gathers]
        shapes += [jax.ShapeDtypeStruct(buf.shape, buf.dtype) for _, buf, _ in self.a2as]
        return shapes

    def aliases(self, n_in, n_out):
        ng = len(self.gathers)
        return {n_in + ng + 2 * k + 1: n_out + ng + k for k in range(len(self.a2as))}

    def scratch(self):
        per = N_DEV - 1
        return [pltpu.SemaphoreType.DMA((per * self.n_jobs,)), pltpu.SemaphoreType.DMA((per * self.n_jobs,)),
                pltpu.SemaphoreType.DMA((self.n_jobs,))]

    def _copies(self, in_refs, out_refs, sems, with_recvs):
        send_sems, recv_sems, local_sems = sems
        _, me = _peer(0)
        ng = len(self.gathers)
        local, sends, recvs = [], [], []
        for job in range(self.n_jobs):
            if job < ng:
                src_of = lambda pid, r=in_refs[job]: r
                dst_of = lambda pid, r=out_refs[job]: r.at[pid]
            else:
                k = job - ng
                stage = self.a2as[k][2]
                src_of = lambda pid, r=in_refs[ng + 2 * k]: r.at[pid]
                dst_of = lambda pid, r=out_refs[job], st=stage: r.at[pid, st]
            local.append(pltpu.make_async_copy(src_of(me), dst_of(me), local_sems.at[job]))
            for mask in range(1, N_DEV):
                peer, pid = _peer(mask)
                idx = job * (N_DEV - 1) + mask - 1
                sends.append(pltpu.make_async_remote_copy(
                    src_ref=src_of(pid), dst_ref=dst_of(me), send_sem=send_sems.at[idx],
                    recv_sem=recv_sems.at[idx], device_id=peer, device_id_type=MESH))
                if with_recvs:
                    recvs.append(pltpu.make_async_remote_copy(
                        src_ref=src_of(pid), dst_ref=dst_of(pid), send_sem=send_sems.at[idx],
                        recv_sem=recv_sems.at[idx], device_id=peer, device_id_type=MESH))
        return local, sends, recvs

    def start(self, in_refs, out_refs, sems):
        local, sends, _ = self._copies(in_refs, out_refs, sems, False)
        for cp in local + sends:
            cp.start()

    def wait(self, in_refs, out_refs, sems):
        local, sends, recvs = self._copies(in_refs, out_refs, sems, True)
        for cp in recvs:
            cp.wait_recv()
        for cp in sends:
            cp.wait_send()
        for cp in local:
            cp.wait()


def carrier_call(core, *, name, grid, in_specs, out_specs, out_shape, inputs, scratch_shapes=(), aliases=None,
                 exchange=None):
    aliases = dict(aliases or {})
    n_in, n_out, n_sc = len(in_specs), len(out_specs), len(scratch_shapes)
    sem = ("arbitrary",) * len(grid)
    params = pltpu.CompilerParams(dimension_semantics=sem, vmem_limit_bytes=VMEM_LIMIT)
    if exchange is None or exchange.n_jobs == 0:
        outs = pl.pallas_call(core, name=name, grid=grid, in_specs=list(in_specs), out_specs=tuple(out_specs),
                              out_shape=tuple(out_shape), scratch_shapes=list(scratch_shapes),
                              input_output_aliases=aliases, compiler_params=params)(*inputs)
        return list(outs), []
    x_in = exchange.inputs()
    x_out = exchange.out_shapes()
    aliases.update(exchange.aliases(n_in, n_out))

    def body(*refs):
        a = n_in + len(x_in)
        b = a + n_out + len(x_out)
        core_in, job_in = refs[:n_in], refs[n_in:a]
        core_out, job_out = refs[a:a + n_out], refs[a + n_out:b]
        core_sc, job_sc = refs[b:b + n_sc], refs[b + n_sc:]
        first = functools.reduce(lambda p, q: p & q, [pl.program_id(ax) == 0 for ax in range(len(grid))])
        last = functools.reduce(lambda p, q: p & q, [pl.program_id(ax) == g - 1 for ax, g in enumerate(grid)])

        @pl.when(first)
        def _():
            exchange.start(job_in, job_out, job_sc)

        core(*core_in, *core_out, *core_sc)

        @pl.when(last)
        def _():
            exchange.wait(job_in, job_out, job_sc)

    outs = pl.pallas_call(
        body, name=name, grid=grid, in_specs=list(in_specs) + [ANY] * len(x_in),
        out_specs=tuple(out_specs) + (ANY,) * len(x_out), out_shape=tuple(out_shape) + tuple(x_out),
        scratch_shapes=list(scratch_shapes) + exchange.scratch(), input_output_aliases=aliases,
        compiler_params=params)(*inputs, *x_in)
    return list(outs[:n_out]), list(outs[n_out:])


def exchange_only(exchange, name):
    def body(*refs):
        n_in, n_out = len(exchange.inputs()), len(exchange.out_shapes())
        job_in, job_out, sems = refs[:n_in], refs[n_in:n_in + n_out], refs[n_in + n_out:]
        exchange.start(job_in, job_out, sems)
        exchange.wait(job_in, job_out, sems)

    x_in = exchange.inputs()
    outs = pl.pallas_call(
        body, name=name, in_specs=[ANY] * len(x_in), out_specs=(ANY,) * len(exchange.out_shapes()),
        out_shape=tuple(exchange.out_shapes()), scratch_shapes=exchange.scratch(),
        input_output_aliases=exchange.aliases(0, 0))(*x_in)
    return list(outs)


def all_gather(v, name):
    return exchange_only(Exchange(gathers=[v]), name)[0]


def gather_two_level(vs, name):
    nv = len(vs)
    per = N_DEV - 1

    def body(*refs):
        v_refs, o_refs = refs[:nv], refs[nv:2 * nv]
        send_sems, recv_sems, local_sems = refs[2 * nv:]
        x, y, c = lax.axis_index("x"), lax.axis_index("y"), lax.axis_index("c")
        me, sibling = (x, y, c), (x, y, 1 - c)
        chips = [(1 - x, y), (x, 1 - y), (1 - x, 1 - y)]

        def copy(a, k, block, to, src=None):
            dst = o_refs[a].at[4 * block[0] + 2 * block[1] + block[2]]
            return pltpu.make_async_remote_copy(
                src_ref=dst if src is None else src, dst_ref=dst, send_sem=send_sems.at[a * per + k],
                recv_sem=recv_sems.at[a * per + k], device_id=to, device_id_type=MESH)

        mine = [pltpu.make_async_copy(v_refs[a], o_refs[a].at[4 * x + 2 * y + c], local_sems.at[a]) for a in range(nv)]
        first = []
        for a in range(nv):
            first.append(copy(a, 0, me, sibling, src=v_refs[a]))
            first += [copy(a, 1 + j, me, (*chip, c), src=v_refs[a]) for j, chip in enumerate(chips)]
        for cp in mine + first:
            cp.start()
        passed = []
        for a in range(nv):
            for j, chip in enumerate(chips):
                copy(a, 1 + j, (*chip, c), me).wait_recv()
                passed.append(copy(a, 4 + j, (*chip, c), sibling))
                passed[-1].start()
        for a in range(nv):
            copy(a, 0, sibling, me).wait_recv()
            for j, chip in enumerate(chips):
                copy(a, 4 + j, (*chip, 1 - c), me).wait_recv()
        for cp in first + passed:
            cp.wait_send()
        for cp in mine:
            cp.wait()

    outs = pl.pallas_call(
        body, name=name, in_specs=[ANY] * nv, out_specs=(ANY,) * nv,
        out_shape=tuple(jax.ShapeDtypeStruct((N_DEV,) + v.shape, v.dtype) for v in vs),
        scratch_shapes=[pltpu.SemaphoreType.DMA((per * nv,)), pltpu.SemaphoreType.DMA((per * nv,)),
                        pltpu.SemaphoreType.DMA((nv,))])(*vs)
    return list(outs)


def _rms(xf):
    return lax.rsqrt(jnp.mean(xf * xf, axis=-1, keepdims=True) + RMS_EPS)


def _is_ctx(i, tm, s):
    return (i * tm + lax.broadcasted_iota(jnp.int32, (tm, 1), 0)) >= s


def _by_tile_kind(i, tm, s, fn):
    n_latent = s // tm

    @pl.when(i < n_latent)
    def _():
        fn(None)

    @pl.when(i >= n_latent)
    def _():
        fn(_is_ctx(i, tm, s))


def _mod_rows(mod_ref, k, is_ctx):
    if is_ctx is None:
        return mod_ref[0, k:k + 1, :]
    return jnp.where(is_ctx, mod_ref[1, k:k + 1, :], mod_ref[0, k:k + 1, :])


def _norm_mod(xf, g, mod_ref, k_shift, k_scale, is_ctx):
    nrm = xf * _rms(xf) * g
    return (nrm * (1.0 + _mod_rows(mod_ref, k_scale, is_ctx)) + _mod_rows(mod_ref, k_shift, is_ctx)).astype(BF16)


def _post(xf, ff, g, mod_ref, k_gate, coef, is_ctx):
    return xf + coef * _mod_rows(mod_ref, k_gate, is_ctx) * (ff * _rms(ff) * g)


def _red_add(red_ref, first, is_ctx, rows):
    @pl.when(first)
    def _():
        red_ref[...] = jnp.zeros_like(red_ref)

    for r, val in enumerate(rows):
        tot = jnp.sum(val, axis=0, keepdims=True)
        if is_ctx is None:
            red_ref[0, r:r + 1, :] += tot
        else:
            ctx = jnp.sum(jnp.where(is_ctx, val, 0.0), axis=0, keepdims=True)
            red_ref[0, r:r + 1, :] += tot - ctx
            red_ref[1, r:r + 1, :] += ctx


def _pre_bwd(xf, dh, dxo, g, mod_ref, k_scale, is_ctx):
    r = _rms(xf)
    xhat = xf * r
    dn = dh * (1.0 + _mod_rows(mod_ref, k_scale, is_ctx))
    dxhat = dn * g
    dx = dxo + r * (dxhat - xhat * jnp.mean(dxhat * xhat, axis=-1, keepdims=True))
    return dx, [dh, dh * (xhat * g), dn * xhat]


def _vec_spec(d):
    return pl.BlockSpec((1, d), lambda *_: (0, 0))


def _mod_whole(d):
    return pl.BlockSpec((2, N_MOD, d), lambda *_: (0, 0, 0))


def _red_whole(d):
    return pl.BlockSpec((2, 8, d), lambda *_: (0, 0, 0))


def _token_tile(n):
    return _pick(n, (768, 640, 512, 384, 256))


def _ffn_tile(n):
    return _pick(n, (384, 640, 256))


def _resident(shape):
    zeros = (0,) * len(shape)
    return pl.BlockSpec(shape, lambda i: zeros, pipeline_mode=pl.Buffered(1))


def ffn_up(x, g, mod, wgu, s, k0, name, exchange=None):
    n, d = x.shape
    nk, fq = wgu.shape[1], wgu.shape[-1]
    tm = _ffn_tile(n)

    def core(x_ref, g_ref, mod_ref, w_ref, hb_ref, gu_ref, a_ref):
        i = pl.program_id(0)

        def prologue(is_ctx):
            hb_ref[...] = _norm_mod(x_ref[...], g_ref[...], mod_ref, k0, k0 + 1, is_ctx)

        _by_tile_kind(i, tm, s, prologue)
        h = hb_ref[...]
        for k in range(nk):
            gg = _dot(h, w_ref[0, k])
            uu = _dot(h, w_ref[1, k])
            gu_ref[0, k] = gg.astype(BF16)
            gu_ref[1, k] = uu.astype(BF16)
            a_ref[k] = (_silu(gg) * uu).astype(BF16)

    outs, xo = carrier_call(
        core, name=name, grid=(n // tm,),
        in_specs=[pl.BlockSpec((tm, d), lambda i: (i, 0)), _vec_spec(d), _mod_whole(d), _resident(wgu.shape)],
        out_specs=[pl.BlockSpec((tm, d), lambda i: (i, 0)),
                   pl.BlockSpec((2, nk, tm, fq), lambda i: (0, 0, i, 0)),
                   pl.BlockSpec((nk, tm, fq), lambda i: (0, i, 0))],
        out_shape=[jax.ShapeDtypeStruct((n, d), BF16), jax.ShapeDtypeStruct((2, nk, n, fq), BF16),
                   jax.ShapeDtypeStruct((nk, n, fq), BF16)],
        inputs=[x, g, mod, wgu], exchange=exchange)
    return outs, xo


def ffn_down(a4, wd4, x, g, mod, s, k_gate, name, exchange=None):
    n, d = x.shape
    nk, fq = wd4.shape[0], wd4.shape[1]
    tm = _ffn_tile(n)

    def core(a_ref, w_ref, x_ref, g_ref, mod_ref, f_ref, xo_ref):
        i = pl.program_id(0)
        ff = _dot(a_ref[0], w_ref[0])
        for k in range(1, nk):
            ff = ff + _dot(a_ref[k], w_ref[k])
        f_ref[...] = ff

        def epilogue(is_ctx):
            xo_ref[...] = _post(x_ref[...], f_ref[...], g_ref[...], mod_ref, k_gate, 0.5, is_ctx)

        _by_tile_kind(i, tm, s, epilogue)

    tile = pl.BlockSpec((tm, d), lambda i: (i, 0))
    outs, xo = carrier_call(
        core, name=name, grid=(n // tm,),
        in_specs=[pl.BlockSpec((nk, tm, fq), lambda i: (0, i, 0)), _resident(wd4.shape), tile, _vec_spec(d),
                  _mod_whole(d)],
        out_specs=[tile, tile],
        out_shape=[jax.ShapeDtypeStruct((n, d), F32), jax.ShapeDtypeStruct((n, d), F32)],
        inputs=[a4, wd4, x, g, mod], exchange=exchange)
    return outs, xo


def ffn_da(df, wd4, gu, name, exchange=None):
    n, d = df.shape
    nk, fq = wd4.shape[0], wd4.shape[1]
    tm = _ffn_tile(n)

    def core(df_ref, w_ref, gu_ref, o_ref):
        dfv = df_ref[...]
        for k in range(nk):
            da = _dot(dfv, w_ref[k], NT).astype(BF16)
            gg = gu_ref[0, k]
            uu = gu_ref[1, k]
            sg = jax.nn.sigmoid(gg.astype(F32)).astype(BF16)
            o_ref[0, k] = da * (uu * (sg * (1 + gg * (1 - sg))))
            o_ref[1, k] = da * (gg * sg)

    gu_spec = pl.BlockSpec((2, nk, tm, fq), lambda i: (0, 0, i, 0))
    outs, xo = carrier_call(
        core, name=name, grid=(n // tm,),
        in_specs=[pl.BlockSpec((tm, d), lambda i: (i, 0)), _resident(wd4.shape), gu_spec],
        out_specs=[gu_spec], out_shape=[jax.ShapeDtypeStruct(gu.shape, BF16)],
        inputs=[df, wd4, gu], exchange=exchange)
    return outs[0], xo


def ffn_dh(dgu, wgu, x, dxo, g, mod, s, k0, name, exchange=None):
    n, d = x.shape
    nk, fq = wgu.shape[1], wgu.shape[-1]
    tm = _ffn_tile(n)

    def core(dgu_ref, w_ref, x_ref, dxo_ref, g_ref, mod_ref, dx_ref, red_ref, dh_s):
        i = pl.program_id(0)
        dh = _dot(dgu_ref[0, 0], w_ref[0, 0], NT) + _dot(dgu_ref[1, 0], w_ref[1, 0], NT)
        for k in range(1, nk):
            dh = dh + _dot(dgu_ref[0, k], w_ref[0, k], NT) + _dot(dgu_ref[1, k], w_ref[1, k], NT)
        dh_s[...] = dh

        def epilogue(is_ctx):
            dx, sums = _pre_bwd(x_ref[...], dh_s[...], dxo_ref[...], g_ref[...], mod_ref, k0 + 1, is_ctx)
            dx_ref[...] = dx
            _red_add(red_ref, i == 0, is_ctx, sums)

        _by_tile_kind(i, tm, s, epilogue)

    tile = pl.BlockSpec((tm, d), lambda i: (i, 0))
    outs, xo = carrier_call(
        core, name=name, grid=(n // tm,),
        in_specs=[pl.BlockSpec((2, nk, tm, fq), lambda i: (0, 0, i, 0)), _resident(wgu.shape), tile, tile,
                  _vec_spec(d), _mod_whole(d)],
        out_specs=[tile, _red_whole(d)],
        out_shape=[jax.ShapeDtypeStruct((n, d), F32), jax.ShapeDtypeStruct((2, 8, d), F32)],
        scratch_shapes=[pltpu.VMEM((tm, d), F32)], inputs=[dgu, wgu, x, dxo, g, mod], exchange=exchange)
    return outs, xo


def grad_weight(a, b, name, a_lead=None, b_lead=None, b_cols=None):
    n = a.shape[-2]
    ka = a.shape[-1]
    kb = b_cols or b.shape[-1]
    nj = a_lead or b_lead or (b.shape[-1] // b_cols)
    tk = _pick(n, (1408, 1024, 768, 640, 512, 256))
    nk = n // tk

    def core(a_ref, b_ref, o_ref, acc):
        kk = pl.program_id(1)

        @pl.when(kk == 0)
        def _():
            acc[...] = jnp.zeros_like(acc)

        av = a_ref[0] if a_lead else a_ref[...]
        bv = b_ref[0] if b_lead else b_ref[...]
        acc[...] += _dot(av, bv, TN)

        @pl.when(kk == nk - 1)
        def _():
            o_ref[0] = acc[...].astype(BF16)

    a_spec = (pl.BlockSpec((1, tk, ka), lambda j, kk: (j, kk, 0)) if a_lead
              else pl.BlockSpec((tk, ka), lambda j, kk: (kk, 0)))
    if b_lead:
        b_spec = pl.BlockSpec((1, tk, kb), lambda j, kk: (j, kk, 0))
    elif b_cols:
        b_spec = pl.BlockSpec((tk, kb), lambda j, kk: (kk, j))
    else:
        b_spec = pl.BlockSpec((tk, kb), lambda j, kk: (kk, 0))
    outs, _ = carrier_call(
        core, name=name, grid=(nj, nk), in_specs=[a_spec, b_spec],
        out_specs=[pl.BlockSpec((1, ka, kb), lambda j, kk: (j, 0, 0))],
        out_shape=[jax.ShapeDtypeStruct((nj, ka, kb), BF16)],
        scratch_shapes=[pltpu.VMEM((ka, kb), F32)], inputs=[a, b])
    return outs[0]


def post_bwd(f, dxo, g, mod, k_gate, coef, s, name):
    n, d = f.shape

    def core(f_ref, dxo_ref, g_ref, mod_ref, df_ref, red_ref):
        i = pl.program_id(0)

        def body(is_ctx):
            ff = f_ref[...]
            dxo_ = dxo_ref[...]
            gg = g_ref[...]
            r = _rms(ff)
            fn = ff * r
            dy = (coef * _mod_rows(mod_ref, k_gate, is_ctx)) * dxo_
            dfn = dy * gg
            df_ref[...] = (r * (dfn - fn * jnp.mean(dfn * fn, axis=-1, keepdims=True))).astype(BF16)
            _red_add(red_ref, i == 0, is_ctx, [coef * (fn * gg) * dxo_, dy * fn])

        _by_tile_kind(i, TM, s, body)

    tile = pl.BlockSpec((TM, d), lambda i: (i, 0))
    outs, _ = carrier_call(
        core, name=name, grid=(n // TM,), in_specs=[tile, tile, _vec_spec(d), _mod_whole(d)],
        out_specs=[tile, _red_whole(d)],
        out_shape=[jax.ShapeDtypeStruct((n, d), BF16), jax.ShapeDtypeStruct((2, 8, d), F32)],
        inputs=[f, dxo, g, mod])
    return outs


def matmul_nt(a, b, name):
    m, k = a.shape
    n = b.shape[0]
    tm = _token_tile(m)

    def core(a_ref, b_ref, o_ref):
        o_ref[...] = _dot(a_ref[...], b_ref[...], NT)

    outs, _ = carrier_call(
        core, name=name, grid=(m // tm,),
        in_specs=[pl.BlockSpec((tm, k), lambda i: (i, 0)), pl.BlockSpec((n, k), lambda i: (0, 0))],
        out_specs=[pl.BlockSpec((tm, n), lambda i: (i, 0))], out_shape=[jax.ShapeDtypeStruct((m, n), F32)],
        inputs=[a, b])
    return outs[0]


def _rope_tables(s, n):
    t = jnp.arange(n)
    lane = jnp.arange(LANES)
    dd = lane % HEAD_DIM
    inv = ROPE_THETA ** (-(dd % ROPE_PAIRS).astype(F32) / ROPE_PAIRS)
    pos = jnp.where(dd[None, :] < HEAD_DIM // 2, (t // GRID_W)[:, None], (t % GRID_W)[:, None]).astype(F32)
    ang = pos * inv[None, :]
    live = (t < s)[:, None]
    first = ((dd % (2 * ROPE_PAIRS)) < ROPE_PAIRS)[None, :]
    cos = jnp.where(live, jnp.cos(ang), 1.0)
    sin = jnp.where(live, jnp.sin(ang), 0.0)
    sa = jnp.where(first, -sin, 0.0)
    sb = jnp.where(first, 0.0, sin)
    return cos.astype(F32), sa.astype(F32), sb.astype(F32)


def _rope(xv, cos, sa, sb):
    return (xv * cos + pltpu.roll(xv, LANES - ROPE_PAIRS, 1) * sa + pltpu.roll(xv, ROPE_PAIRS, 1) * sb)


def mix_in(x, g, mod, win8, tables, s, name):
    n, d = x.shape
    tm = _token_tile(n)
    nb = win8.shape[0]
    n_rope = 2 * NA_WIDTH // IN_BLOCK

    def core(x_ref, g_ref, mod_ref, w_ref, c_ref, sa_ref, sb_ref, hb_ref, qkv_ref, u_ref):
        i = pl.program_id(0)
        hb = _norm_mod(x_ref[...], g_ref[...], mod_ref, 3, 4, _is_ctx(i, tm, s))
        hb_ref[...] = hb
        cos, sa, sb = c_ref[...], sa_ref[...], sb_ref[...]
        for j in range(nb):
            y = _dot(hb, w_ref[j])
            if j < n_rope:
                for b in range(IN_BLOCK // LANES):
                    sl = slice(b * LANES, (b + 1) * LANES)
                    qkv_ref[:, j * IN_BLOCK + b * LANES:j * IN_BLOCK + (b + 1) * LANES] = (
                        _rope(y[:, sl], cos, sa, sb).astype(BF16))
            elif j < N_QKV_BLOCKS:
                qkv_ref[:, j * IN_BLOCK:(j + 1) * IN_BLOCK] = y.astype(BF16)
            else:
                u_ref[:, (j - N_QKV_BLOCKS) * IN_BLOCK:(j - N_QKV_BLOCKS + 1) * IN_BLOCK] = y

    tab = pl.BlockSpec((tm, LANES), lambda i: (i, 0))
    row = lambda w: pl.BlockSpec((tm, w), lambda i: (i, 0))
    outs, _ = carrier_call(
        core, name=name, grid=(n // tm,),
        in_specs=[row(d), _vec_spec(d), _mod_whole(d), pl.BlockSpec((nb, d, IN_BLOCK), lambda i: (0, 0, 0)),
                  tab, tab, tab],
        out_specs=[row(d), row(3 * NA_WIDTH), row(POOL_WIDTH)],
        out_shape=[jax.ShapeDtypeStruct((n, d), BF16), jax.ShapeDtypeStruct((n, 3 * NA_WIDTH), BF16),
                   jax.ShapeDtypeStruct((n, POOL_WIDTH), F32)],
        inputs=[x, g, mod, win8, *tables])
    return outs


def qkv_bwd(dq, dk, dv, du, tables, name):
    n = dq.shape[0]
    w = NA_WIDTH

    def core(dq_ref, dk_ref, dv_ref, du_ref, c_ref, sa_ref, sb_ref, o_ref):
        cos, sa, sb = c_ref[...], -sa_ref[...], -sb_ref[...]
        for b in range(N_HEAD_BLOCKS):
            sl = slice(b * LANES, (b + 1) * LANES)
            o_ref[:, b * LANES:(b + 1) * LANES] = _rope(dq_ref[:, sl], cos, sa, sb).astype(BF16)
            o_ref[:, w + b * LANES:w + (b + 1) * LANES] = _rope(dk_ref[:, sl], cos, sa, sb).astype(BF16)
        o_ref[:, 2 * w:3 * w] = dv_ref[...].astype(BF16)
        o_ref[:, 3 * w:] = du_ref[...].astype(BF16)

    tab = pl.BlockSpec((TM, LANES), lambda i: (i, 0))
    tile = pl.BlockSpec((TM, w), lambda i: (i, 0))
    outs, _ = carrier_call(
        core, name=name, grid=(n // TM,), in_specs=[tile, tile, tile, tile, tab, tab, tab],
        out_specs=[pl.BlockSpec((TM, 4 * w), lambda i: (i, 0))],
        out_shape=[jax.ShapeDtypeStruct((n, 4 * w), BF16)], inputs=[dq, dk, dv, du, *tables])
    return outs[0]


def mix_out(na, py, wout, x, g, mod, s, name):
    n, d = x.shape
    tm = _token_tile(n)
    half = na.shape[1]

    def core(na_ref, py_ref, w_ref, x_ref, g_ref, mod_ref, f_ref, xo_ref):
        i = pl.program_id(0)
        ff = _dot(na_ref[...], w_ref[:half, :]) + _dot(py_ref[...], w_ref[half:, :])
        f_ref[...] = ff
        xo_ref[...] = _post(x_ref[...], ff, g_ref[...], mod_ref, 5, 1.0, _is_ctx(i, tm, s))

    tile = pl.BlockSpec((tm, d), lambda i: (i, 0))
    htile = pl.BlockSpec((tm, half), lambda i: (i, 0))
    outs, _ = carrier_call(
        core, name=name, grid=(n // tm,),
        in_specs=[htile, htile, pl.BlockSpec((2 * half, d), lambda i: (0, 0)), tile, _vec_spec(d), _mod_whole(d)],
        out_specs=[tile, tile],
        out_shape=[jax.ShapeDtypeStruct((n, d), F32), jax.ShapeDtypeStruct((n, d), F32)],
        inputs=[na, py, wout, x, g, mod])
    return outs


def grad_wout(na, py, dfm, name):
    n, half = na.shape
    d = dfm.shape[1]
    tk = _pick(n, (1408, 1024, 768, 640, 512, 256))
    nk = n // tk

    def core(na_ref, py_ref, b_ref, o_ref, acc):
        hh, kk = pl.program_id(0), pl.program_id(1)

        @pl.when(kk == 0)
        def _():
            acc[...] = jnp.zeros_like(acc)

        @pl.when(hh == 0)
        def _():
            acc[...] += _dot(na_ref[...], b_ref[...], TN)

        @pl.when(hh == 1)
        def _():
            acc[...] += _dot(py_ref[...], b_ref[...], TN)

        @pl.when(kk == nk - 1)
        def _():
            o_ref[0] = acc[...].astype(BF16)

    htile = pl.BlockSpec((tk, half), lambda hh, kk: (kk, 0))
    outs, _ = carrier_call(
        core, name=name, grid=(2, nk), in_specs=[htile, htile, pl.BlockSpec((tk, d), lambda hh, kk: (kk, 0))],
        out_specs=[pl.BlockSpec((1, half, d), lambda hh, kk: (hh, 0, 0))],
        out_shape=[jax.ShapeDtypeStruct((2, half, d), BF16)],
        scratch_shapes=[pltpu.VMEM((half, d), F32)], inputs=[na, py, dfm])
    return outs[0]


def mix_dh(dqkvu, win8, x, dxo, g, mod, s, name, exchange=None):
    n, d = x.shape
    tm = _token_tile(n)
    nb = win8.shape[0]

    def core(dq_ref, w_ref, x_ref, dxo_ref, g_ref, mod_ref, dx_ref, red_ref):
        i = pl.program_id(0)
        dh = _dot(dq_ref[:, :IN_BLOCK], w_ref[0], NT)
        for j in range(1, nb):
            dh = dh + _dot(dq_ref[:, j * IN_BLOCK:(j + 1) * IN_BLOCK], w_ref[j], NT)
        is_ctx = _is_ctx(i, tm, s)
        dx, sums = _pre_bwd(x_ref[...], dh, dxo_ref[...], g_ref[...], mod_ref, 4, is_ctx)
        dx_ref[...] = dx
        _red_add(red_ref, i == 0, is_ctx, sums)

    tile = pl.BlockSpec((tm, d), lambda i: (i, 0))
    outs, xo = carrier_call(
        core, name=name, grid=(n // tm,),
        in_specs=[pl.BlockSpec((tm, nb * IN_BLOCK), lambda i: (i, 0)),
                  pl.BlockSpec((nb, d, IN_BLOCK), lambda i: (0, 0, 0)), tile, tile, _vec_spec(d), _mod_whole(d)],
        out_specs=[tile, _red_whole(d)],
        out_shape=[jax.ShapeDtypeStruct((n, d), F32), jax.ShapeDtypeStruct((2, 8, d), F32)],
        inputs=[dqkvu, win8, x, dxo, g, mod], exchange=exchange)
    return outs, xo


def _na_consts():
    j = np.arange(GRID_W)
    col_start = np.clip(j - NA_KW // 2, 0, GRID_W - NA_KW)
    valid = (j[None, :] >= col_start[:, None]) & (j[None, :] < col_start[:, None] + NA_KW)
    dc = np.clip(j[None, :] - j[:, None] + NA_KW - 1, 0, 2 * NA_KW - 2)
    onehot = np.zeros((LANES, GRID_W, GRID_W), np.float32)
    for d in range(2 * NA_KW - 1):
        onehot[d] = ((dc == d) & valid).astype(np.float32)
    negmask = np.where(valid, 0.0, NEG_INF).astype(np.float32)
    return onehot.reshape(LANES, GRID_W * GRID_W), np.tile(negmask, (1, NA_KH))


def bias_tables(rpb, name):
    onehot, negmask = _na_consts()
    nj = 2 * NA_KH - 1
    rows = NA_HEADS * nj
    a = jnp.pad(rpb.reshape(rows, 2 * NA_KW - 1), ((0, 0), (0, LANES - (2 * NA_KW - 1))))

    def body(a_ref, e_ref, o_ref):
        o_ref[...] = _dot(a_ref[...], e_ref[...], precision=HIGHEST)

    t = pl.pallas_call(
        body, name=name, out_shape=jax.ShapeDtypeStruct((rows, GRID_W * GRID_W), F32),
        in_specs=[pl.BlockSpec(memory_space=pltpu.VMEM)] * 2,
        out_specs=pl.BlockSpec(memory_space=pltpu.VMEM),
    )(a, jnp.asarray(onehot))
    t = t.reshape(NA_HEADS, nj, GRID_W, GRID_W)
    tb = jnp.stack([t[:, j0:j0 + NA_KH] for j0 in range(NA_KH)])
    tb = tb.transpose(0, 1, 3, 2, 4).reshape(NA_KH, NA_HEADS, GRID_W, NA_KH * GRID_W)
    return tb + jnp.asarray(negmask)[None, None]


def bias_tables_bwd(dtb, name):
    onehot, _ = _na_consts()
    nj = 2 * NA_KH - 1
    d5 = dtb.reshape(NA_KH, NA_HEADS, GRID_W, NA_KH, GRID_W).transpose(0, 3, 1, 2, 4)
    d2 = d5.reshape(NA_KH * NA_KH * NA_HEADS, GRID_W * GRID_W)

    def body(d_ref, e_ref, o_ref):
        r = _dot(d_ref[...], e_ref[...], NT, precision=HIGHEST)
        for j in range(nj):
            acc = jnp.zeros((NA_HEADS, LANES), F32)
            for j0 in range(NA_KH):
                kk = j - j0
                if 0 <= kk < NA_KH:
                    base = (j0 * NA_KH + kk) * NA_HEADS
                    acc = acc + r[base:base + NA_HEADS, :]
            o_ref[j] = acc

    out = pl.pallas_call(
        body, name=name, out_shape=jax.ShapeDtypeStruct((nj, NA_HEADS, LANES), F32),
        in_specs=[pl.BlockSpec(memory_space=pltpu.VMEM)] * 2,
        out_specs=pl.BlockSpec(memory_space=pltpu.VMEM),
        compiler_params=pltpu.CompilerParams(vmem_limit_bytes=VMEM_LIMIT),
    )(d2, jnp.asarray(onehot))
    return out[:, :, :2 * NA_KW - 1].transpose(1, 0, 2)


def _head_masks():
    lane = lax.broadcasted_iota(jnp.int32, (1, LANES), 1)
    return [(lane >= h * HEAD_DIM) & (lane < (h + 1) * HEAD_DIM) for h in range(HEADS_PER_BLOCK)]


def _row_window(r, rows):
    rs = jnp.clip(r - NA_KH // 2, 0, rows - NA_KH)
    return rs - r + NA_KH - 1, pl.multiple_of(rs * GRID_W, GRID_W)


def _stack_heads(t, masks):
    return jnp.concatenate([jnp.where(mk, t, jnp.zeros_like(t)) for mk in masks], axis=0)


def _unstack_heads(t2, masks):
    out = t2[(HEADS_PER_BLOCK - 1) * GRID_W:, :]
    for h in reversed(range(HEADS_PER_BLOCK - 1)):
        out = jnp.where(masks[h], t2[h * GRID_W:(h + 1) * GRID_W, :], out)
    return out


NA_ROWS_PER_STEP = 4
NA_STEP = NA_ROWS_PER_STEP * GRID_W
SLAB = NA_KH * GRID_W
K_COL = N_HEAD_BLOCKS
V_COL = 2 * N_HEAD_BLOCKS


def na_fwd(qkv, tb, s, name, exchange=None):
    n = qkv.shape[0]
    l = n - s
    rows = s // GRID_W
    rr = NA_ROWS_PER_STEP
    x_steps = rows // rr

    def core(q_ref, k_ref, v_ref, kc_ref, vc_ref, tb_ref, o_ref, lse_ref):
        rb = pl.program_id(1)

        @pl.when(rb >= x_steps)
        def _():
            o_ref[...] = jnp.zeros_like(o_ref)
            lse_ref[...] = jnp.zeros_like(lse_ref)

        @pl.when(rb < x_steps)
        def _():
            masks = _head_masks()
            kcb, vcb = kc_ref[...], vc_ref[...]
            wins, scores = [], []
            for t in range(rr):
                j0, off = _row_window(rb * rr + t, rows)
                q2 = _stack_heads(q_ref[t * GRID_W:(t + 1) * GRID_W, :] * ATT_SCALE, masks)
                bias = tb_ref[j0].reshape(HEADS_PER_BLOCK * GRID_W, SLAB)
                s_loc = _dot(q2, k_ref[pl.ds(off, SLAB), :], NT) + bias
                s_ctx = _dot(q2, kcb, NT)
                wins.append(off)
                scores.append((s_loc, s_ctx))
            probs = []
            for s_loc, s_ctx in scores:
                m = jnp.maximum(jnp.max(s_loc, axis=-1, keepdims=True), jnp.max(s_ctx, axis=-1, keepdims=True))
                p_loc = jnp.exp(s_loc - m)
                p_ctx = jnp.exp(s_ctx - m)
                den = jnp.sum(p_loc, axis=-1, keepdims=True) + jnp.sum(p_ctx, axis=-1, keepdims=True)
                probs.append((p_loc.astype(BF16), p_ctx.astype(BF16), den, m + jnp.log(den)))
            for t, (p_loc, p_ctx, den, lse2) in enumerate(probs):
                o2 = (_dot(p_loc, v_ref[pl.ds(wins[t], SLAB), :]) + _dot(p_ctx, vcb)) / den
                o_ref[t * GRID_W:(t + 1) * GRID_W, :] = _unstack_heads(o2, masks).astype(BF16)
                lse_ref[0, t * GRID_W:(t + 1) * GRID_W, :] = _unstack_heads(lse2, masks)

    cb = s // l
    outs, xo = carrier_call(
        core, name=name, grid=(N_HEAD_BLOCKS, n // NA_STEP),
        in_specs=[pl.BlockSpec((NA_STEP, LANES), lambda hb, rb: (jnp.minimum(rb, x_steps - 1), hb)),
                  pl.BlockSpec((s, LANES), lambda hb, rb: (0, K_COL + hb)),
                  pl.BlockSpec((s, LANES), lambda hb, rb: (0, V_COL + hb)),
                  pl.BlockSpec((l, LANES), lambda hb, rb: (cb, K_COL + hb)),
                  pl.BlockSpec((l, LANES), lambda hb, rb: (cb, V_COL + hb)),
                  pl.BlockSpec((NA_KH, HEADS_PER_BLOCK, GRID_W, SLAB), lambda hb, rb: (0, hb, 0, 0))],
        out_specs=[pl.BlockSpec((NA_STEP, LANES), lambda hb, rb: (rb, hb)),
                   pl.BlockSpec((1, NA_STEP, LANES), lambda hb, rb: (hb, rb, 0))],
        out_shape=[jax.ShapeDtypeStruct((n, NA_WIDTH), BF16), jax.ShapeDtypeStruct((N_HEAD_BLOCKS, n, LANES), F32)],
        inputs=[qkv, qkv, qkv, qkv, qkv, tb], exchange=exchange)
    return outs, xo


def na_bwd(qkv, tb, o, dmix, lse, s, name, exchange=None):
    n = qkv.shape[0]
    l = n - s
    rows = s // GRID_W
    rr = NA_ROWS_PER_STEP
    x_steps = rows // rr

    def core(q_ref, k_ref, v_ref, kc_ref, vc_ref, tb_ref, o_ref, do_ref, lse_ref, dq_ref, dk_ref, dv_ref, dtb_ref):
        rb = pl.program_id(1)

        @pl.when(rb == 0)
        def _():
            dk_ref[...] = jnp.zeros_like(dk_ref)
            dv_ref[...] = jnp.zeros_like(dv_ref)
            dtb_ref[...] = jnp.zeros_like(dtb_ref)

        @pl.when(rb >= x_steps)
        def _():
            dq_ref[...] = jnp.zeros_like(dq_ref)

        @pl.when(rb < x_steps)
        def _():
            masks = _head_masks()
            kcb, vcb = kc_ref[...], vc_ref[...]
            stage1 = []
            for t in range(rr):
                j0, off = _row_window(rb * rr + t, rows)
                sl = slice(t * GRID_W, (t + 1) * GRID_W)
                q2 = _stack_heads(q_ref[sl, :] * ATT_SCALE, masks)
                do_f = do_ref[sl, :]
                do2 = _stack_heads(do_f.astype(BF16), masks)
                dd = do_f * o_ref[sl, :].astype(F32)
                delta2 = jnp.concatenate(
                    [jnp.sum(jnp.where(mk, dd, 0.0), axis=-1, keepdims=True) for mk in masks], axis=0)
                lse_t = lse_ref[0, sl, :]
                lse2 = jnp.concatenate(
                    [lse_t[:, h * HEAD_DIM:h * HEAD_DIM + 1] for h in range(HEADS_PER_BLOCK)], axis=0)
                kslab = k_ref[pl.ds(off, SLAB), :]
                vslab = v_ref[pl.ds(off, SLAB), :]
                bias = tb_ref[j0].reshape(HEADS_PER_BLOCK * GRID_W, SLAB)
                s_loc = _dot(q2, kslab, NT) + bias - lse2
                s_ctx = _dot(q2, kcb, NT) - lse2
                dp_loc = _dot(do2, vslab, NT) - delta2
                dp_ctx = _dot(do2, vcb, NT) - delta2
                stage1.append((j0, off, q2, do2, s_loc, s_ctx, dp_loc, dp_ctx))
            stage2 = []
            for j0, off, q2, do2, s_loc, s_ctx, dp_loc, dp_ctx in stage1:
                p_loc = jnp.exp(s_loc)
                p_ctx = jnp.exp(s_ctx)
                ds_loc = p_loc * dp_loc
                dtb_ref[j0] += ds_loc.reshape(HEADS_PER_BLOCK, GRID_W, SLAB)
                stage2.append((off, q2, do2, p_loc.astype(BF16), p_ctx.astype(BF16), ds_loc.astype(BF16),
                               (p_ctx * dp_ctx).astype(BF16)))
            for t, (off, q2, do2, p_loc, p_ctx, ds_loc, ds_ctx) in enumerate(stage2):
                dq2 = (_dot(ds_loc, k_ref[pl.ds(off, SLAB), :]) + _dot(ds_ctx, kcb)) * ATT_SCALE
                dq_ref[t * GRID_W:(t + 1) * GRID_W, :] = _unstack_heads(dq2, masks)
                dk_ref[pl.ds(off, SLAB), :] += _dot(ds_loc, q2, TN)
                dv_ref[pl.ds(off, SLAB), :] += _dot(p_loc, do2, TN)
                dk_ref[s:, :] += _dot(ds_ctx, q2, TN)
                dv_ref[s:, :] += _dot(p_ctx, do2, TN)

    cb = s // l
    clamp = lambda hb, rb: (jnp.minimum(rb, x_steps - 1), hb)
    tile_in = pl.BlockSpec((NA_STEP, LANES), clamp)
    whole_out = pl.BlockSpec((n, LANES), lambda hb, rb: (0, hb))
    tbs = pl.BlockSpec((NA_KH, HEADS_PER_BLOCK, GRID_W, SLAB), lambda hb, rb: (0, hb, 0, 0))
    f32n = jax.ShapeDtypeStruct((n, NA_WIDTH), F32)
    outs, xo = carrier_call(
        core, name=name, grid=(N_HEAD_BLOCKS, n // NA_STEP),
        in_specs=[tile_in,
                  pl.BlockSpec((s, LANES), lambda hb, rb: (0, K_COL + hb)),
                  pl.BlockSpec((s, LANES), lambda hb, rb: (0, V_COL + hb)),
                  pl.BlockSpec((l, LANES), lambda hb, rb: (cb, K_COL + hb)),
                  pl.BlockSpec((l, LANES), lambda hb, rb: (cb, V_COL + hb)),
                  tbs, tile_in, tile_in,
                  pl.BlockSpec((1, NA_STEP, LANES), lambda hb, rb: (hb, jnp.minimum(rb, x_steps - 1), 0))],
        out_specs=[pl.BlockSpec((NA_STEP, LANES), lambda hb, rb: (rb, hb)), whole_out, whole_out, tbs],
        out_shape=[f32n, f32n, f32n, jax.ShapeDtypeStruct((NA_KH, NA_HEADS, GRID_W, SLAB), F32)],
        inputs=[qkv, qkv, qkv, qkv, qkv, tb, o, dmix, lse], exchange=exchange)
    return outs, xo


def ctx_attn_fwd(qkv, na, s, name):
    n = qkv.shape[0]
    l = n - s
    cb = s // l

    def core(q_ref, k_ref, v_ref, na_in, o_ref, lse_ref):
        masks = _head_masks()
        qt, kb, vb = q_ref[...], k_ref[...], v_ref[...]
        o_acc = jnp.zeros((l, LANES), F32)
        lse_acc = jnp.zeros((l, LANES), F32)
        for h in range(HEADS_PER_BLOCK):
            qh = jnp.where(masks[h], qt, jnp.zeros_like(qt))
            sc = _dot(qh, kb, NT) * ATT_SCALE
            m = jnp.max(sc, axis=-1, keepdims=True)
            p = jnp.exp(sc - m)
            den = jnp.sum(p, axis=-1, keepdims=True)
            o_acc = jnp.where(masks[h], _dot(p.astype(BF16), vb) / den, o_acc)
            lse_acc = jnp.where(masks[h], m + jnp.log(den), lse_acc)
        o_ref[...] = o_acc.astype(BF16)
        lse_ref[0] = lse_acc

    outs, _ = carrier_call(
        core, name=name, grid=(N_HEAD_BLOCKS,),
        in_specs=[pl.BlockSpec((l, LANES), lambda hb: (cb, hb)), pl.BlockSpec((l, LANES), lambda hb: (cb, K_COL + hb)),
                  pl.BlockSpec((l, LANES), lambda hb: (cb, V_COL + hb)), ANY],
        out_specs=[pl.BlockSpec((l, LANES), lambda hb: (cb, hb)), pl.BlockSpec((1, l, LANES), lambda hb: (hb, 0, 0))],
        out_shape=[jax.ShapeDtypeStruct(na.shape, BF16), jax.ShapeDtypeStruct((N_HEAD_BLOCKS, l, LANES), F32)],
        inputs=[qkv, qkv, qkv, na], aliases={3: 0})
    return outs


def ctx_attn_bwd(qkv, na, dmix, lse, dq, dk, dv, s, name):
    n = qkv.shape[0]
    l = n - s
    cb = s // l

    def core(q_ref, k_ref, v_ref, o_ref, do_ref, lse_ref, dq_in, dk_in, dv_in, dq_ref, dk_ref, dv_ref):
        masks = _head_masks()
        qt, kb, vb = q_ref[...], k_ref[...], v_ref[...]
        do_f = do_ref[...]
        dd = do_f * o_ref[...].astype(F32)
        do_b = do_f.astype(BF16)
        lse_t = lse_ref[0]
        dq_acc = jnp.zeros((l, LANES), F32)
        dk_acc = jnp.zeros((l, LANES), F32)
        dv_acc = jnp.zeros((l, LANES), F32)
        for h in range(HEADS_PER_BLOCK):
            qh = jnp.where(masks[h], qt, jnp.zeros_like(qt))
            doh = jnp.where(masks[h], do_b, jnp.zeros_like(do_b))
            delta = jnp.sum(jnp.where(masks[h], dd, 0.0), axis=-1, keepdims=True)
            p = jnp.exp(_dot(qh, kb, NT) * ATT_SCALE - lse_t[:, h * HEAD_DIM:h * HEAD_DIM + 1])
            ds = (p * (_dot(doh, vb, NT) - delta)).astype(BF16)
            dq_acc = jnp.where(masks[h], _dot(ds, kb) * ATT_SCALE, dq_acc)
            dk_acc = dk_acc + _dot(ds, qh, TN)
            dv_acc = dv_acc + _dot(p.astype(BF16), doh, TN)
        dq_ref[...] = dq_acc
        dk_ref[...] = dk_in[...] + dk_acc * ATT_SCALE
        dv_ref[...] = dv_in[...] + dv_acc

    blk = pl.BlockSpec((l, LANES), lambda hb: (cb, hb))
    f32n = jax.ShapeDtypeStruct((n, NA_WIDTH), F32)
    outs, _ = carrier_call(
        core, name=name, grid=(N_HEAD_BLOCKS,),
        in_specs=[blk, pl.BlockSpec((l, LANES), lambda hb: (cb, K_COL + hb)),
                  pl.BlockSpec((l, LANES), lambda hb: (cb, V_COL + hb)), blk, blk,
                  pl.BlockSpec((1, l, LANES), lambda hb: (hb, 0, 0)), ANY, blk, blk],
        out_specs=[blk, blk, blk], out_shape=[f32n, f32n, f32n],
        inputs=[qkv, qkv, qkv, na, dmix, lse, dq, dk, dv], aliases={6: 0, 7: 1, 8: 2})
    return outs


def _pool_consts(l):
    assert l == TM
    mem = np.zeros((2, POOL_GROUPS, TM, TM), np.float32)
    inv = np.zeros((2, POOL_GROUPS, TM, LANES), np.float32)
    for which, length in ((0, GRID_W), (1, l)):
        t = np.arange(length)
        for g, w in enumerate(POOL_WINDOWS):
            lo = np.clip(t - w // 2, 0, length)
            hi = np.clip(t - w // 2 + w, 0, length)
            blockm = ((t[None, :] >= lo[:, None]) & (t[None, :] < hi[:, None])).astype(np.float32)
            cnt = (hi - lo).astype(np.float32)
            for b in range(TM // length):
                mem[which, g, b * length:(b + 1) * length, b * length:(b + 1) * length] = blockm
                inv[which, g, b * length:(b + 1) * length, :] = (1.0 / cnt)[:, None]
    return mem, np.ascontiguousarray(mem.transpose(0, 1, 3, 2)), inv


def _split_dot(m01, val):
    hi = val.astype(BF16)
    lo = (val - hi.astype(F32)).astype(BF16)
    return _dot(m01, hi) + _dot(m01, lo)


def pool_fwd(u, mem, inv, wp, scale, nx_tiles, name):
    n = u.shape[0]

    def core(u_ref, m_ref, i_ref, wp_ref, s_ref, o_ref):
        for g in range(POOL_GROUPS):
            sl = slice(g * POOL_CH, (g + 1) * POOL_CH)
            ug = u_ref[:, sl]
            dg = _split_dot(m_ref[0, g], ug) * i_ref[0, g] - ug
            o_ref[:, sl] = (_dot(dg.astype(BF16), wp_ref[g]) * s_ref[:, sl]).astype(BF16)

    grp = lambda i: (i // nx_tiles, 0, 0, 0)
    outs, _ = carrier_call(
        core, name=name, grid=(n // TM,),
        in_specs=[pl.BlockSpec((TM, POOL_WIDTH), lambda i: (i, 0)),
                  pl.BlockSpec((1, POOL_GROUPS, TM, TM), grp),
                  pl.BlockSpec((1, POOL_GROUPS, TM, LANES), grp),
                  pl.BlockSpec((POOL_GROUPS, POOL_CH, POOL_CH), lambda i: (0, 0, 0)),
                  pl.BlockSpec((1, POOL_WIDTH), lambda i: (0, 0))],
        out_specs=[pl.BlockSpec((TM, POOL_WIDTH), lambda i: (i, 0))],
        out_shape=[jax.ShapeDtypeStruct((n, POOL_WIDTH), BF16)], inputs=[u, mem, inv, wp, scale])
    return outs[0]


def pool_bwd(dmix, u, mem, mem_t, inv, wp, scale, nx_tiles, name):
    n = u.shape[0]

    def core(dy_ref, u_ref, m_ref, mt_ref, i_ref, wp_ref, s_ref, du_ref, dwp_ref, dsc_ref):
        @pl.when(pl.program_id(0) == 0)
        def _():
            dwp_ref[...] = jnp.zeros_like(dwp_ref)
            dsc_ref[...] = jnp.zeros_like(dsc_ref)

        for g in range(POOL_GROUPS):
            sl = slice(g * POOL_CH, (g + 1) * POOL_CH)
            ug = u_ref[:, sl]
            dy = dy_ref[:, sl]
            dg = (_split_dot(m_ref[0, g], ug) * i_ref[0, g] - ug).astype(BF16)
            z = _dot(dg, wp_ref[g])
            dsc_ref[0:1, sl] += jnp.sum(dy * z, axis=0, keepdims=True)
            dz = (dy * s_ref[:, sl]).astype(BF16)
            dwp_ref[g] += _dot(dg, dz, TN)
            dd = _dot(dz, wp_ref[g], NT)
            du_ref[:, sl] = _split_dot(mt_ref[0, g], dd * i_ref[0, g]) - dd

    grp = lambda i: (i // nx_tiles, 0, 0, 0)
    outs, _ = carrier_call(
        core, name=name, grid=(n // TM,),
        in_specs=[pl.BlockSpec((TM, POOL_WIDTH), lambda i: (i, 1)),
                  pl.BlockSpec((TM, POOL_WIDTH), lambda i: (i, 0)),
                  pl.BlockSpec((1, POOL_GROUPS, TM, TM), grp),
                  pl.BlockSpec((1, POOL_GROUPS, TM, TM), grp),
                  pl.BlockSpec((1, POOL_GROUPS, TM, LANES), grp),
                  pl.BlockSpec((POOL_GROUPS, POOL_CH, POOL_CH), lambda i: (0, 0, 0)),
                  pl.BlockSpec((1, POOL_WIDTH), lambda i: (0, 0))],
        out_specs=[pl.BlockSpec((TM, POOL_WIDTH), lambda i: (i, 0)),
                   pl.BlockSpec((POOL_GROUPS, POOL_CH, POOL_CH), lambda i: (0, 0, 0)),
                   pl.BlockSpec((8, POOL_WIDTH), lambda i: (0, 0))],
        out_shape=[jax.ShapeDtypeStruct((n, POOL_WIDTH), F32),
                   jax.ShapeDtypeStruct((POOL_GROUPS, POOL_CH, POOL_CH), F32),
                   jax.ShapeDtypeStruct((8, POOL_WIDTH), F32)],
        inputs=[dmix, u, mem, mem_t, inv, wp, scale])
    return outs


MOD_ROWS = 16


def mod_fwd(cvecs, w, b, name):
    _, d = cvecs.shape
    cl = w.shape[2]
    tc = _pick(cl, (384, 128))

    def core(c_ref, w_ref, b_ref, o_ref):
        a = _silu(c_ref[...]).astype(BF16)
        o_ref[0] = _dot(a, w_ref[0].astype(BF16)) + b_ref[0]

    outs, _ = carrier_call(
        core, name=name, grid=(DEPTH, cl // tc),
        in_specs=[pl.BlockSpec((MOD_ROWS, d), lambda li, j: (0, 0)),
                  pl.BlockSpec((1, d, tc), lambda li, j: (li, 0, j)),
                  pl.BlockSpec((1, 1, tc), lambda li, j: (li, 0, j))],
        out_specs=[pl.BlockSpec((1, MOD_ROWS, tc), lambda li, j: (li, 0, j))],
        out_shape=[jax.ShapeDtypeStruct((DEPTH, MOD_ROWS, cl), F32)], inputs=[cvecs, w, b])
    return outs[0]


def mod_bwd(cvecs, dm, w, name):
    _, d = cvecs.shape
    cl = w.shape[2]
    tc = _pick(cl, (384, 128))

    def core(c_ref, dm_ref, w_ref, dw_ref, da_ref):
        @pl.when((pl.program_id(0) == 0) & (pl.program_id(1) == 0))
        def _():
            da_ref[...] = jnp.zeros_like(da_ref)

        a = _silu(c_ref[...]).astype(BF16)
        dmb = dm_ref[0].astype(BF16)
        dw_ref[0] = _dot(a, dmb, TN)
        da_ref[...] += _dot(dmb, w_ref[0].astype(BF16), NT)

    outs, _ = carrier_call(
        core, name=name, grid=(DEPTH, cl // tc),
        in_specs=[pl.BlockSpec((MOD_ROWS, d), lambda li, j: (0, 0)),
                  pl.BlockSpec((1, MOD_ROWS, tc), lambda li, j: (li, 0, j)),
                  pl.BlockSpec((1, d, tc), lambda li, j: (li, 0, j))],
        out_specs=[pl.BlockSpec((1, d, tc), lambda li, j: (li, 0, j)),
                   pl.BlockSpec((MOD_ROWS, d), lambda li, j: (0, 0))],
        out_shape=[jax.ShapeDtypeStruct((DEPTH, d, cl), F32), jax.ShapeDtypeStruct((MOD_ROWS, d), F32)],
        inputs=[cvecs, dm, w])
    return outs


def loss_head(y, target, name):
    n, d = y.shape
    s = target.shape[0]
    nt, nx = n // TM, s // TM

    def core(y_ref, t_ref, l_ref, dy_ref, acc_ref):
        i = pl.program_id(0)

        @pl.when(i == 0)
        def _():
            acc_ref[...] = jnp.zeros_like(acc_ref)

        @pl.when(i < nx)
        def _():
            e = y_ref[...] - t_ref[...]
            dy_ref[...] = e * (1.0 / d)
            acc_ref[...] += jnp.sum(e * e, axis=0, keepdims=True)

        @pl.when(i >= nx)
        def _():
            dy_ref[...] = jnp.zeros_like(dy_ref)

        @pl.when(i == nt - 1)
        def _():
            l_ref[...] = jnp.sum(acc_ref[...], axis=1, keepdims=True) * (0.5 / d)

    tile = pl.BlockSpec((TM, d), lambda i: (i, 0))
    outs, _ = carrier_call(
        core, name=name, grid=(nt,),
        in_specs=[tile, pl.BlockSpec((TM, d), lambda i: (jnp.minimum(i, nx - 1), 0))],
        out_specs=[pl.BlockSpec((1, 1), lambda i: (0, 0)), tile],
        out_shape=[jax.ShapeDtypeStruct((1, 1), F32), jax.ShapeDtypeStruct((n, d), F32)],
        scratch_shapes=[pltpu.VMEM((1, d), F32)], inputs=[y, target])
    return outs


def sum_devices(v, name):
    _, r, c = v.shape
    tr = _pick(r, (64, 8))

    def core(v_ref, o_ref):
        acc = v_ref[0]
        for p in range(1, N_DEV):
            acc = acc + v_ref[p]
        o_ref[...] = acc

    outs, _ = carrier_call(
        core, name=name, grid=(r // tr,), in_specs=[pl.BlockSpec((N_DEV, tr, c), lambda i: (0, i, 0))],
        out_specs=[pl.BlockSpec((tr, c), lambda i: (i, 0))], out_shape=[jax.ShapeDtypeStruct((r, c), F32)],
        inputs=[v])
    return outs[0]


def cctx_grad(parts, c_ctx, name):
    d = c_ctx.shape[1]

    def body(p_ref, c_ref, o_ref):
        acc = p_ref[0]
        for p in range(1, N_DEV):
            acc = acc + p_ref[p]
        o_ref[...] = acc[8:9, :] * _dsilu(c_ref[...])

    return pl.pallas_call(
        body, name=name, out_shape=jax.ShapeDtypeStruct((1, d), F32),
        in_specs=[pl.BlockSpec(memory_space=pltpu.VMEM)] * 2,
        out_specs=pl.BlockSpec(memory_space=pltpu.VMEM),
    )(parts, c_ctx)


def _adam_math(w, g, m, v):
    m2 = ADAM_B1 * m + (1.0 - ADAM_B1) * g
    v2 = ADAM_B2 * v + (1.0 - ADAM_B2) * (g * g)
    m_hat = m2 / (1.0 - ADAM_B1 ** ADAM_STEP)
    v_hat = v2 / (1.0 - ADAM_B2 ** ADAM_STEP)
    delta = -ADAM_LR * (m_hat / (jnp.sqrt(v_hat) + ADAM_EPS) + ADAM_WD * w)
    return delta, m2, v2


def adamw(w, g, m, v, name):
    r, c = w.shape
    tr = _pick(r, (256, 128, 64, 32, 16, 8, r))

    def core(w_ref, g_ref, m_ref, v_ref, d_ref, m2_ref, v2_ref):
        d_ref[...], m2_ref[...], v2_ref[...] = _adam_math(w_ref[...], g_ref[...], m_ref[...], v_ref[...])

    tile = pl.BlockSpec((tr, c), lambda i: (i, 0))
    out = jax.ShapeDtypeStruct((r, c), F32)
    outs, _ = carrier_call(core, name=name, grid=(r // tr,), in_specs=[tile] * 4, out_specs=[tile] * 3,
                           out_shape=[out] * 3, inputs=[w, g, m, v])
    return outs


def reduce_adamw(recv, w, m, v, name):
    r, c = w.shape
    tr = _pick(r, (256, 128, 64, 8))

    def core(recv_ref, w_ref, m_ref, v_ref, g_ref, d_ref, m2_ref, v2_ref):
        acc = recv_ref[0].astype(F32)
        for p in range(1, N_DEV):
            acc = acc + recv_ref[p].astype(F32)
        g_ref[...] = acc
        d_ref[...], m2_ref[...], v2_ref[...] = _adam_math(w_ref[...], acc, m_ref[...], v_ref[...])

    tile = pl.BlockSpec((tr, c), lambda i: (i, 0))
    out = jax.ShapeDtypeStruct((r, c), F32)
    outs, _ = carrier_call(
        core, name=name, grid=(r // tr,),
        in_specs=[pl.BlockSpec((N_DEV, tr, c), lambda i: (0, i, 0)), tile, tile, tile],
        out_specs=[tile] * 4, out_shape=[out] * 4, inputs=[recv, w, m, v])
    return outs


def kernel(x, c, ctx, c_ctx, w_mod, b_mod, norm_g, w_ffn_gate_up, w_ffn_down, w_in, w_out, na_rpb, w_pool, pool_scale, loss_target, m_c_ctx, m_w_mod, m_b_mod, m_norm_g, m_w_ffn_gate_up, m_w_ffn_down, m_w_in, m_w_out, m_na_rpb, m_w_pool, m_pool_scale, v_c_ctx, v_w_mod, v_b_mod, v_norm_g, v_w_ffn_gate_up, v_w_ffn_down, v_w_in, v_w_out, v_na_rpb, v_w_pool, v_pool_scale):
    s, d = x.shape[1], x.shape[2]
    l = ctx.shape[1]
    n = s + l
    nx = s // TM
    fq = w_ffn_gate_up.shape[-1]
    fr = w_ffn_down.shape[2]
    cl = w_mod.shape[2]
    dl = norm_g.shape[2]
    me = 4 * lax.axis_index("x") + 2 * lax.axis_index("y") + lax.axis_index("c")

    c_all = all_gather(c, "gather_c").reshape(N_DEV, d)
    cvecs = jnp.concatenate([c_all, c_ctx[None, :], jnp.zeros((MOD_ROWS - N_DEV - 1, d), F32)], axis=0)
    b_loc = lax.dynamic_slice(b_mod, (0, me * cl), (DEPTH, cl)).reshape(DEPTH, 1, cl)
    mod_loc = mod_fwd(cvecs, w_mod, b_loc, "mod_fwd")
    mod_all = all_gather(mod_loc.reshape(DEPTH * MOD_ROWS, cl), "gather_mod")
    mod_all = mod_all.reshape(N_DEV, DEPTH, MOD_ROWS, cl).transpose(1, 2, 0, 3).reshape(DEPTH, MOD_ROWS, N_DEV * cl)
    mine = lax.dynamic_slice(mod_all, (0, me, 0), (DEPTH, 1, N_DEV * cl))
    mods = jnp.concatenate([mine, mod_all[:, N_DEV:N_DEV + 1]], axis=1).reshape(DEPTH, 2, N_MOD, d)

    gu_b = w_ffn_gate_up.astype(BF16)
    dn_b = w_ffn_down.astype(BF16)
    wi_b = w_in.astype(BF16)
    wo_b = w_out.astype(BF16)
    wp_b = w_pool.astype(BF16)

    def ffn_shards(li, i):
        return [gu_b[li, i], dn_b[li, i]]

    def mix_shards(li):
        return [wi_b[li], wo_b[li]]

    def as_ffn_weights(gathered):
        return gathered[0].reshape(2, 4, d, fq), gathered[1].reshape(4, 2 * fr, d)

    def as_mix_weights(gathered):
        return gathered[0], gathered[1].reshape(N_DEV * wo_b.shape[1], d)

    tables = _rope_tables(s, n)
    mem_np, mem_t_np, inv_np = _pool_consts(l)
    mem, mem_t, inv = jnp.asarray(mem_np, BF16), jnp.asarray(mem_t_np, BF16), jnp.asarray(inv_np)
    first = gather_two_level([norm_g.reshape(DEPTH * 6, dl), gu_b[0, 0]], "gather_first")
    g_full = first[0].reshape(N_DEV, DEPTH, 6, dl).transpose(1, 2, 0, 3).reshape(DEPTH, 6, 1, N_DEV * dl)

    weights = {("ffn", 0, 0): (first[1].reshape(2, 4, d, fq), None)}
    saved = {}
    xcur = jnp.concatenate([x[0], ctx[0]], axis=0)
    for li in range(DEPTH):
        last = li == DEPTH - 1
        for i in range(2):
            tag = f"l{li}_ffn{i}"
            wgu, wd4 = weights[("ffn", li, i)]
            if i == 0:
                ex_up, ex_dn = Exchange(gathers=mix_shards(li) + ([dn_b[0, 0]] if wd4 is None else [])), None
            elif not last:
                ex_up, ex_dn = Exchange(gathers=[gu_b[li + 1, 0]]), Exchange(gathers=[dn_b[li + 1, 0]])
            else:
                ex_up = ex_dn = None
            (hb, gu, a4), got_up = ffn_up(xcur, g_full[li, 4 * i], mods[li], wgu, s, 6 * i, tag + "_up", ex_up)
            if wd4 is None:
                wd4 = got_up.pop().reshape(4, 2 * fr, d)
                weights[("ffn", li, i)] = (wgu, wd4)
            (ff, xnext), got_dn = ffn_down(a4, wd4, xcur, g_full[li, 4 * i + 1], mods[li], s, 6 * i + 2, tag + "_down", ex_dn)
            saved[("ffn", li, i)] = (xcur, hb, gu, a4, ff)
            xcur = xnext
            if i == 0:
                weights[("mix", li)] = as_mix_weights(got_up)
            elif not last:
                weights[("ffn", li + 1, 0)] = as_ffn_weights(got_up + got_dn)
            if i == 0:
                tag = f"l{li}_mix"
                win8, wout = weights[("mix", li)]
                hb, qkv, u = mix_in(xcur, g_full[li, 2], mods[li], win8, tables, s, tag + "_in")
                tb = bias_tables(na_rpb[li], tag + "_bias")
                (na, lse), got = na_fwd(qkv, tb, s, tag + "_na", Exchange(gathers=ffn_shards(li, 1)))
                weights[("ffn", li, 1)] = as_ffn_weights(got)
                lse_c = None
                if not last:
                    na, lse_c = ctx_attn_fwd(qkv, na, s, tag + "_ctx_attn")
                py = pool_fwd(u, mem, inv, wp_b[li], pool_scale[li][None, :], nx, tag + "_pool")
                fm, xnext = mix_out(na, py, wout, xcur, g_full[li, 3], mods[li], s, tag + "_out")
                saved[("mix", li)] = (xcur, hb, qkv, u, tb, na, lse, lse_c, py, fm)
                xcur = xnext

    loss_local, dcur = loss_head(xcur, loss_target[0], "loss")
    loss = lax.psum(loss_local[0, 0], ("x", "y", "c"))

    recv = {"gu": lax.empty((N_DEV, 2 * DEPTH, d, fq), BF16), "dn": lax.empty((N_DEV, 2 * DEPTH, fr, d), BF16),
            "wi": lax.empty((N_DEV, DEPTH, d, IN_BLOCK), BF16), "wo": lax.empty((N_DEV, DEPTH, wo_b.shape[1], d), BF16)}
    pending = []

    def take(keys, gathers=()):
        nonlocal pending
        jobs = [(gr, recv[key], st) for key, gr, st in pending if key in keys]
        order = [key for key, _, _ in pending if key in keys]
        pending = [p for p in pending if p[0] not in keys]
        return Exchange(gathers=gathers, a2as=jobs), order

    def pad8(t):
        t = t.reshape(-1, d) if t.size % d == 0 else jnp.pad(t.reshape(-1), (0, -t.size % d)).reshape(-1, d)
        return jnp.pad(t, ((0, -t.shape[0] % 8), (0, 0)))

    def packed(parts):
        parts = [pad8(p) for p in parts]
        offs = np.cumsum([0] + [p.shape[0] for p in parts])
        return jnp.concatenate(parts + [jnp.zeros((-offs[-1] % 64, d), F32)], axis=0), offs

    def put(order, bufs):
        for key, buf in zip(order, bufs):
            recv[key] = buf

    d_rpb, d_wp, d_ps, d_mod, d_g = [], [], [], [], []
    for li in reversed(range(DEPTH)):
        reds = {}
        for i in (1, 0):
            tag = f"l{li}_ffn{i}"
            xin, hb, gu, a4, ff = saved[("ffn", li, i)]
            wgu, wd4 = weights[("ffn", li, i)]
            dff, red1 = post_bwd(ff, dcur, g_full[li, 4 * i + 1], mods[li], 6 * i + 2, 0.5, s, tag + "_post_bwd")
            g_dn = grad_weight(a4, dff, tag + "_dwdown", a_lead=4).reshape(N_DEV, fr, d)
            pending += [("dn", g_dn, 2 * li + i)]
            early = []
            if (li, i) == (0, 0):
                early_small, early_offs = packed([jnp.stack(d_wp), jnp.stack(d_ps), jnp.stack(d_rpb)])
                early = [early_small]
            ex, order = take(("dn", "wi", "wo"), early)
            dgu, bufs = ffn_da(dff, wd4, gu, tag + "_da", ex)
            put(order, bufs[len(early):])
            if early:
                early_sum = sum_devices(bufs[0], "sum_early_small_grads")
            g_gu = grad_weight(hb, dgu.reshape(N_DEV, n, fq), tag + "_dwgu", b_lead=N_DEV)
            pending += [("gu", g_gu, 2 * li + i)]
            ex, order = take(("gu",))
            (dcur, red2), bufs = ffn_dh(dgu, wgu, xin, dcur, g_full[li, 4 * i], mods[li], s, 6 * i, tag + "_dh", ex)
            put(order, bufs)
            reds[i] = (red1, red2)
            if i == 1:
                tag = f"l{li}_mix"
                xin, hb, qkv, u, tb, na, lse, lse_c, py, fm = saved[("mix", li)]
                win8, wout = weights[("mix", li)]
                dfm, redm1 = post_bwd(fm, dcur, g_full[li, 3], mods[li], 5, 1.0, s, tag + "_post_bwd")
                dmix = matmul_nt(dfm, wout, tag + "_dmix")
                g_wo = grad_wout(na, py, dfm, tag + "_dwout").reshape(N_DEV, wo_b.shape[1], d)
                du, gwp, gps = pool_bwd(dmix, u, mem, mem_t, inv, wp_b[li], pool_scale[li][None, :], nx, tag + "_pool_bwd")
                ex, order = take(("gu", "dn"))
                (dq, dk, dv, dtb), bufs = na_bwd(qkv, tb, na, dmix, lse, s, tag + "_na_bwd", ex)
                put(order, bufs)
                if li != DEPTH - 1:
                    dq, dk, dv = ctx_attn_bwd(qkv, na, dmix, lse_c, dq, dk, dv, s, tag + "_ctx_attn_bwd")
                grpb = bias_tables_bwd(dtb, tag + "_bias_bwd")
                dqkvu = qkv_bwd(dq, dk, dv, du, tables, tag + "_rope_bwd")
                (dcur, redm2), _ = mix_dh(dqkvu, win8, xin, dcur, g_full[li, 2], mods[li], s, tag + "_dh")
                g_wi = grad_weight(hb, dqkvu, tag + "_dwin", b_cols=IN_BLOCK)
                pending += [("wi", g_wi, li), ("wo", g_wo, li)]
                d_rpb.insert(0, grpb)
                d_wp.insert(0, gwp)
                d_ps.insert(0, gps[0])
        (ra1, ra2), (rb1, rb2) = reds[0], reds[1]
        d_mod.insert(0, jnp.stack([ra2[:, 0], ra2[:, 1], ra1[:, 0], redm2[:, 0], redm2[:, 1], redm1[:, 0],
                                   rb2[:, 0], rb2[:, 1], rb1[:, 0]], axis=1))
        d_g.insert(0, jnp.stack([t[0] + t[1] for t in (ra2[:, 2], ra1[:, 1], redm2[:, 2], redm1[:, 1], rb2[:, 2], rb1[:, 1])]))
    grad_x = dcur[:s][None]

    small, offs = packed([jnp.stack(d_mod), jnp.stack(d_g)])
    ex, order = take(("gu", "dn", "wi", "wo"), [small])
    bufs = exchange_only(ex, "exchange_last")
    small_all = bufs[0]
    put(order, bufs[1:])
    small_sum = sum_devices(small_all, "sum_small_grads")

    n_mod_rows = DEPTH * 2 * N_MOD
    dmod_all = small_all[:, :n_mod_rows].reshape(N_DEV, DEPTH, 2, N_MOD * d)
    dmod_sum = small_sum[:n_mod_rows].reshape(DEPTH, 2, N_MOD * d)
    dm_rows = jnp.concatenate([dmod_all[:, :, 0].transpose(1, 0, 2), dmod_sum[:, 1:2],
                               jnp.zeros((DEPTH, MOD_ROWS - N_DEV - 1, N_MOD * d), F32)], axis=1)
    grad_b_mod = dmod_sum[:, 0] + dmod_sum[:, 1]
    dm_loc = lax.dynamic_slice(dm_rows, (0, 0, me * cl), (DEPTH, MOD_ROWS, cl))
    grad_w_mod, da_part = mod_bwd(cvecs, dm_loc, w_mod, "mod_bwd")
    da_all = all_gather(da_part, "gather_dcvec")
    grad_c_ctx = cctx_grad(da_all, c_ctx[None, :], "c_ctx_grad")[0]

    grad_norm_full = small_sum[offs[1]:offs[1] + DEPTH * 6].reshape(DEPTH, 6, d)
    grad_norm_g = lax.dynamic_slice(grad_norm_full, (0, 0, me * dl), (DEPTH, 6, dl))
    grad_w_pool = early_sum[early_offs[0]:early_offs[0] + w_pool.size // d].reshape(w_pool.shape)
    grad_pool_scale = early_sum[early_offs[1]:early_offs[1] + pool_scale.size // d].reshape(pool_scale.shape)
    grad_na_rpb = early_sum[early_offs[2]:early_offs[3]].reshape(-1)[:na_rpb.size].reshape(na_rpb.shape)

    def big_adam(key, w, m, v, name):
        shp = w.shape
        cols = shp[-1]
        outs = reduce_adamw(recv[key].reshape(N_DEV, -1, cols), w.reshape(-1, cols), m.reshape(-1, cols),
                            v.reshape(-1, cols), name)
        return tuple(t.reshape(shp) for t in outs)

    def small_adam(w, g, m, v, name):
        shp = w.shape
        cols = shp[-1]
        outs = adamw(w.reshape(-1, cols), g.reshape(-1, cols), m.reshape(-1, cols), v.reshape(-1, cols), name)
        return tuple(t.reshape(shp) for t in outs)

    b_gu = big_adam("gu", w_ffn_gate_up, m_w_ffn_gate_up, v_w_ffn_gate_up, "adam_gate_up")
    b_dn = big_adam("dn", w_ffn_down, m_w_ffn_down, v_w_ffn_down, "adam_down")
    b_wi = big_adam("wi", w_in, m_w_in, v_w_in, "adam_w_in")
    b_wo = big_adam("wo", w_out, m_w_out, v_w_out, "adam_w_out")
    a_cc = small_adam(c_ctx, grad_c_ctx, m_c_ctx, v_c_ctx, "adam_c_ctx")
    a_wm = small_adam(w_mod, grad_w_mod, m_w_mod, v_w_mod, "adam_w_mod")
    a_bm = small_adam(b_mod, grad_b_mod, m_b_mod, v_b_mod, "adam_b_mod")
    a_ng = small_adam(norm_g, grad_norm_g, m_norm_g, v_norm_g, "adam_norm_g")
    a_rp = small_adam(na_rpb, grad_na_rpb, m_na_rpb, v_na_rpb, "adam_na_rpb")
    a_wp = small_adam(w_pool, grad_w_pool, m_w_pool, v_w_pool, "adam_w_pool")
    a_ps = small_adam(pool_scale, grad_pool_scale, m_pool_scale, v_pool_scale, "adam_pool_scale")

    grads = (grad_c_ctx, grad_w_mod, grad_b_mod, grad_norm_g, b_gu[0], b_dn[0], b_wi[0], b_wo[0], grad_na_rpb, grad_w_pool, grad_pool_scale)
    deltas = (a_cc[0], a_wm[0], a_bm[0], a_ng[0], b_gu[1], b_dn[1], b_wi[1], b_wo[1], a_rp[0], a_wp[0], a_ps[0])
    new_m = (a_cc[1], a_wm[1], a_bm[1], a_ng[1], b_gu[2], b_dn[2], b_wi[2], b_wo[2], a_rp[1], a_wp[1], a_ps[1])
    new_v = (a_cc[2], a_wm[2], a_bm[2], a_ng[2], b_gu[3], b_dn[3], b_wi[3], b_wo[3], a_rp[2], a_wp[2], a_ps[2])
    return (loss, grad_x, *grads, *deltas, *new_m, *new_v)
```

```python
import functools
import math

import numpy as np
import jax
import jax.numpy as jnp
from jax import lax
from jax.experimental import pallas as pl
from jax.experimental.pallas import tpu as pltpu

F32 = jnp.float32
BF16 = jnp.bfloat16

N_DEV = 8
DEPTH = 2
GRID_W = 64
N_MOD = 9
NA_HEADS = 8
HEAD_DIM = 64
NA_WIDTH = NA_HEADS * HEAD_DIM
NA_KH = 8
NA_KW = 16
POOL_GROUPS = 4
POOL_CH = 128
POOL_WIDTH = POOL_GROUPS * POOL_CH
POOL_WINDOWS = (2, 4, 8, 16)
ROPE_THETA = 10000.0
ROPE_PAIRS = HEAD_DIM // 4
RMS_EPS = 1e-6
NEG_INF = -1e30
ATT_SCALE = HEAD_DIM ** -0.5

ADAM_LR = 0.001
ADAM_B1 = 0.9
ADAM_B2 = 0.999
ADAM_EPS = 1e-08
ADAM_WD = 0.01
ADAM_STEP = 10

TM = 256
LANES = 128
HEADS_PER_BLOCK = LANES // HEAD_DIM
N_HEAD_BLOCKS = NA_WIDTH // LANES
IN_BLOCK = 2 * LANES
N_QKV_BLOCKS = 3 * NA_WIDTH // IN_BLOCK
VMEM_LIMIT = 56 * 1024 * 1024
HIGHEST = lax.Precision.HIGHEST
MESH = pl.DeviceIdType.MESH
ANY = pl.BlockSpec(memory_space=pl.ANY)

NN = (((1,), (0,)), ((), ()))
NT = (((1,), (1,)), ((), ()))
TN = (((0,), (0,)), ((), ()))


def _pick(n, cands):
    for t in cands:
        if n % t == 0:
            return t
    raise ValueError(f"no tile for {n} among {cands}")


def _dot(a, b, dn=NN, precision=None):
    return lax.dot_general(a, b, dn, preferred_element_type=F32, precision=precision)


def _silu(x):
    return x * jax.nn.sigmoid(x)


def _dsilu(x):
    s = jax.nn.sigmoid(x)
    return s * (1.0 + x * (1.0 - s))


def _peer(mask):
    x, y, c = lax.axis_index("x"), lax.axis_index("y"), lax.axis_index("c")
    px = 1 - x if mask & 4 else x
    py = 1 - y if mask & 2 else y
    pc = 1 - c if mask & 1 else c
    return (px, py, pc), 4 * px + 2 * py + pc


class Exchange:
    def __init__(self, gathers=(), a2as=()):
        self.gathers = list(gathers)
        self.a2as = list(a2as)
        self.n_jobs = len(self.gathers) + len(self.a2as)

    def inputs(self):
        out = list(self.gathers)
        for v, buf, _ in self.a2as:
            out += [v, buf]
        return out

    def out_shapes(self):
        shapes = [jax.ShapeDtypeStruct((N_DEV,) + v.shape, v.dtype) for v in self.gathers]
        shapes += [jax.ShapeDtypeStruct(buf.shape, buf.dtype) for _, buf, _ in self.a2as]
        return shapes

    def aliases(self, n_in, n_out):
        ng = len(self.gathers)
        return {n_in + ng + 2 * k + 1: n_out + ng + k for k in range(len(self.a2as))}

    def scratch(self):
        per = N_DEV - 1
        return [pltpu.SemaphoreType.DMA((per * self.n_jobs,)), pltpu.SemaphoreType.DMA((per * self.n_jobs,)),
                pltpu.SemaphoreType.DMA((self.n_jobs,))]

    def _copies(self, in_refs, out_refs, sems, with_recvs):
        send_sems, recv_sems, local_sems = sems
        _, me = _peer(0)
        ng = len(self.gathers)
        local, sends, recvs = [], [], []
        for job in range(self.n_jobs):
            if job < ng:
                src_of = lambda pid, r=in_refs[job]: r
                dst_of = lambda pid, r=out_refs[job]: r.at[pid]
            else:
                k = job - ng
                stage = self.a2as[k][2]
                src_of = lambda pid, r=in_refs[ng + 2 * k]: r.at[pid]
                dst_of = lambda pid, r=out_refs[job], st=stage: r.at[pid, st]
            local.append(pltpu.make_async_copy(src_of(me), dst_of(me), local_sems.at[job]))
            for mask in range(1, N_DEV):
                peer, pid = _peer(mask)
                idx = job * (N_DEV - 1) + mask - 1
                sends.append(pltpu.make_async_remote_copy(
                    src_ref=src_of(pid), dst_ref=dst_of(me), send_sem=send_sems.at[idx],
                    recv_sem=recv_sems.at[idx], device_id=peer, device_id_type=MESH))
                if with_recvs:
                    recvs.append(pltpu.make_async_remote_copy(
                        src_ref=src_of(pid), dst_ref=dst_of(pid), send_sem=send_sems.at[idx],
                        recv_sem=recv_sems.at[idx], device_id=peer, device_id_type=MESH))
        return local, sends, recvs

    def start(self, in_refs, out_refs, sems):
        local, sends, _ = self._copies(in_refs, out_refs, sems, False)
        for cp in local + sends:
            cp.start()

    def wait(self, in_refs, out_refs, sems):
        local, sends, recvs = self._copies(in_refs, out_refs, sems, True)
        for cp in recvs:
            cp.wait_recv()
        for cp in sends:
            cp.wait_send()
        for cp in local:
            cp.wait()


def carrier_call(core, *, name, grid, in_specs, out_specs, out_shape, inputs, scratch_shapes=(), aliases=None,
                 exchange=None):
    aliases = dict(aliases or {})
    n_in, n_out, n_sc = len(in_specs), len(out_specs), len(scratch_shapes)
    sem = ("arbitrary",) * len(grid)
    params = pltpu.CompilerParams(dimension_semantics=sem, vmem_limit_bytes=VMEM_LIMIT)
    if exchange is None or exchange.n_jobs == 0:
        outs = pl.pallas_call(core, name=name, grid=grid, in_specs=list(in_specs), out_specs=tuple(out_specs),
                              out_shape=tuple(out_shape), scratch_shapes=list(scratch_shapes),
                              input_output_aliases=aliases, compiler_params=params)(*inputs)
        return list(outs), []
    x_in = exchange.inputs()
    x_out = exchange.out_shapes()
    aliases.update(exchange.aliases(n_in, n_out))

    def body(*refs):
        a = n_in + len(x_in)
        b = a + n_out + len(x_out)
        core_in, job_in = refs[:n_in], refs[n_in:a]
        core_out, job_out = refs[a:a + n_out], refs[a + n_out:b]
        core_sc, job_sc = refs[b:b + n_sc], refs[b + n_sc:]
        first = functools.reduce(lambda p, q: p & q, [pl.program_id(ax) == 0 for ax in range(len(grid))])
        last = functools.reduce(lambda p, q: p & q, [pl.program_id(ax) == g - 1 for ax, g in enumerate(grid)])

        @pl.when(first)
        def _():
            exchange.start(job_in, job_out, job_sc)

        core(*core_in, *core_out, *core_sc)

        @pl.when(last)
        def _():
            exchange.wait(job_in, job_out, job_sc)

    outs = pl.pallas_call(
        body, name=name, grid=grid, in_specs=list(in_specs) + [ANY] * len(x_in),
        out_specs=tuple(out_specs) + (ANY,) * len(x_out), out_shape=tuple(out_shape) + tuple(x_out),
        scratch_shapes=list(scratch_shapes) + exchange.scratch(), input_output_aliases=aliases,
        compiler_params=params)(*inputs, *x_in)
    return list(outs[:n_out]), list(outs[n_out:])


def exchange_only(exchange, name):
    def body(*refs):
        n_in, n_out = len(exchange.inputs()), len(exchange.out_shapes())
        job_in, job_out, sems = refs[:n_in], refs[n_in:n_in + n_out], refs[n_in + n_out:]
        exchange.start(job_in, job_out, sems)
        exchange.wait(job_in, job_out, sems)

    x_in = exchange.inputs()
    outs = pl.pallas_call(
        body, name=name, in_specs=[ANY] * len(x_in), out_specs=(ANY,) * len(exchange.out_shapes()),
        out_shape=tuple(exchange.out_shapes()), scratch_shapes=exchange.scratch(),
        input_output_aliases=exchange.aliases(0, 0))(*x_in)
    return list(outs)


def all_gather(v, name):
    return exchange_only(Exchange(gathers=[v]), name)[0]


def gather_two_level(vs, name):
    nv = len(vs)
    per = N_DEV - 1

    def body(*refs):
        v_refs, o_refs = refs[:nv], refs[nv:2 * nv]
        send_sems, recv_sems, local_sems = refs[2 * nv:]
        x, y, c = lax.axis_index("x"), lax.axis_index("y"), lax.axis_index("c")
        me, sibling = (x, y, c), (x, y, 1 - c)
        chips = [(1 - x, y), (x, 1 - y), (1 - x, 1 - y)]

        def copy(a, k, block, to, src=None):
            dst = o_refs[a].at[4 * block[0] + 2 * block[1] + block[2]]
            return pltpu.make_async_remote_copy(
                src_ref=dst if src is None else src, dst_ref=dst, send_sem=send_sems.at[a * per + k],
                recv_sem=recv_sems.at[a * per + k], device_id=to, device_id_type=MESH)

        mine = [pltpu.make_async_copy(v_refs[a], o_refs[a].at[4 * x + 2 * y + c], local_sems.at[a]) for a in range(nv)]
        first = []
        for a in range(nv):
            first.append(copy(a, 0, me, sibling, src=v_refs[a]))
            first += [copy(a, 1 + j, me, (*chip, c), src=v_refs[a]) for j, chip in enumerate(chips)]
        for cp in mine + first:
            cp.start()
        passed = []
        for a in range(nv):
            for j, chip in enumerate(chips):
                copy(a, 1 + j, (*chip, c), me).wait_recv()
                passed.append(copy(a, 4 + j, (*chip, c), sibling))
                passed[-1].start()
        for a in range(nv):
            copy(a, 0, sibling, me).wait_recv()
            for j, chip in enumerate(chips):
                copy(a, 4 + j, (*chip, 1 - c), me).wait_recv()
        for cp in first + passed:
            cp.wait_send()
        for cp in mine:
            cp.wait()

    outs = pl.pallas_call(
        body, name=name, in_specs=[ANY] * nv, out_specs=(ANY,) * nv,
        out_shape=tuple(jax.ShapeDtypeStruct((N_DEV,) + v.shape, v.dtype) for v in vs),
        scratch_shapes=[pltpu.SemaphoreType.DMA((per * nv,)), pltpu.SemaphoreType.DMA((per * nv,)),
                        pltpu.SemaphoreType.DMA((nv,))])(*vs)
    return list(outs)


def _rms(xf):
    return lax.rsqrt(jnp.mean(xf * xf, axis=-1, keepdims=True) + RMS_EPS)


def _is_ctx(i, tm, s):
    return (i * tm + lax.broadcasted_iota(jnp.int32, (tm, 1), 0)) >= s


def _by_tile_kind(i, tm, s, fn):
    n_latent = s // tm

    @pl.when(i < n_latent)
    def _():
        fn(None)

    @pl.when(i >= n_latent)
    def _():
        fn(_is_ctx(i, tm, s))


def _mod_rows(mod_ref, k, is_ctx):
    if is_ctx is None:
        return mod_ref[0, k:k + 1, :]
    return jnp.where(is_ctx, mod_ref[1, k:k + 1, :], mod_ref[0, k:k + 1, :])


def _norm_mod(xf, g, mod_ref, k_shift, k_scale, is_ctx):
    nrm = xf * _rms(xf) * g
    return (nrm * (1.0 + _mod_rows(mod_ref, k_scale, is_ctx)) + _mod_rows(mod_ref, k_shift, is_ctx)).astype(BF16)


def _post(xf, ff, g, mod_ref, k_gate, coef, is_ctx):
    return xf + coef * _mod_rows(mod_ref, k_gate, is_ctx) * (ff * _rms(ff) * g)


def _red_add(red_ref, first, is_ctx, rows):
    @pl.when(first)
    def _():
        red_ref[...] = jnp.zeros_like(red_ref)

    for r, val in enumerate(rows):
        tot = jnp.sum(val, axis=0, keepdims=True)
        if is_ctx is None:
            red_ref[0, r:r + 1, :] += tot
        else:
            ctx = jnp.sum(jnp.where(is_ctx, val, 0.0), axis=0, keepdims=True)
            red_ref[0, r:r + 1, :] += tot - ctx
            red_ref[1, r:r + 1, :] += ctx


def _pre_bwd(xf, dh, dxo, g, mod_ref, k_scale, is_ctx):
    r = _rms(xf)
    xhat = xf * r
    dn = dh * (1.0 + _mod_rows(mod_ref, k_scale, is_ctx))
    dxhat = dn * g
    dx = dxo + r * (dxhat - xhat * jnp.mean(dxhat * xhat, axis=-1, keepdims=True))
    return dx, [dh, dh * (xhat * g), dn * xhat]


def _vec_spec(d):
    return pl.BlockSpec((1, d), lambda *_: (0, 0))


def _mod_whole(d):
    return pl.BlockSpec((2, N_MOD, d), lambda *_: (0, 0, 0))


def _red_whole(d):
    return pl.BlockSpec((2, 8, d), lambda *_: (0, 0, 0))


def _token_tile(n):
    return _pick(n, (768, 640, 512, 384, 256))


def _ffn_tile(n):
    return _pick(n, (528, 384, 640, 256))


def _resident(shape):
    zeros = (0,) * len(shape)
    return pl.BlockSpec(shape, lambda i: zeros, pipeline_mode=pl.Buffered(1))


def ffn_up(x, g, mod, wgu, s, k0, name, exchange=None):
    n, d = x.shape
    nk, fq = wgu.shape[1], wgu.shape[-1]
    tm = _ffn_tile(n)

    def core(x_ref, g_ref, mod_ref, w_ref, hb_ref, gu_ref, a_ref):
        i = pl.program_id(0)

        def prologue(is_ctx):
            hb_ref[...] = _norm_mod(x_ref[...], g_ref[...], mod_ref, k0, k0 + 1, is_ctx)

        _by_tile_kind(i, tm, s, prologue)
        h = hb_ref[...]
        for k in range(nk):
            gg = _dot(h, w_ref[0, k])
            uu = _dot(h, w_ref[1, k])
            gu_ref[0, k] = gg.astype(BF16)
            gu_ref[1, k] = uu.astype(BF16)
            a_ref[k] = (_silu(gg) * uu).astype(BF16)

    outs, xo = carrier_call(
        core, name=name, grid=(n // tm,),
        in_specs=[pl.BlockSpec((tm, d), lambda i: (i, 0)), _vec_spec(d), _mod_whole(d), _resident(wgu.shape)],
        out_specs=[pl.BlockSpec((tm, d), lambda i: (i, 0)),
                   pl.BlockSpec((2, nk, tm, fq), lambda i: (0, 0, i, 0)),
                   pl.BlockSpec((nk, tm, fq), lambda i: (0, i, 0))],
        out_shape=[jax.ShapeDtypeStruct((n, d), BF16), jax.ShapeDtypeStruct((2, nk, n, fq), BF16),
                   jax.ShapeDtypeStruct((nk, n, fq), BF16)],
        inputs=[x, g, mod, wgu], exchange=exchange)
    return outs, xo


def ffn_down(a4, wd4, x, g, mod, s, k_gate, name, exchange=None):
    n, d = x.shape
    nk, fq = wd4.shape[0], wd4.shape[1]
    tm = _ffn_tile(n)

    def core(a_ref, w_ref, x_ref, g_ref, mod_ref, f_ref, xo_ref):
        i = pl.program_id(0)
        ff = _dot(a_ref[0], w_ref[0])
        for k in range(1, nk):
            ff = ff + _dot(a_ref[k], w_ref[k])
        f_ref[...] = ff

        def epilogue(is_ctx):
            xo_ref[...] = _post(x_ref[...], f_ref[...], g_ref[...], mod_ref, k_gate, 0.5, is_ctx)

        _by_tile_kind(i, tm, s, epilogue)

    tile = pl.BlockSpec((tm, d), lambda i: (i, 0))
    outs, xo = carrier_call(
        core, name=name, grid=(n // tm,),
        in_specs=[pl.BlockSpec((nk, tm, fq), lambda i: (0, i, 0)), _resident(wd4.shape), tile, _vec_spec(d),
                  _mod_whole(d)],
        out_specs=[tile, tile],
        out_shape=[jax.ShapeDtypeStruct((n, d), F32), jax.ShapeDtypeStruct((n, d), F32)],
        inputs=[a4, wd4, x, g, mod], exchange=exchange)
    return outs, xo


def ffn_da(df, wd4, gu, name, exchange=None):
    n, d = df.shape
    nk, fq = wd4.shape[0], wd4.shape[1]
    tm = _ffn_tile(n)

    def core(df_ref, w_ref, gu_ref, o_ref):
        dfv = df_ref[...]
        for k in range(nk):
            da = _dot(dfv, w_ref[k], NT).astype(BF16)
            gg = gu_ref[0, k]
            uu = gu_ref[1, k]
            sg = jax.nn.sigmoid(gg.astype(F32)).astype(BF16)
            o_ref[0, k] = da * (uu * (sg * (1 + gg * (1 - sg))))
            o_ref[1, k] = da * (gg * sg)

    gu_spec = pl.BlockSpec((2, nk, tm, fq), lambda i: (0, 0, i, 0))
    outs, xo = carrier_call(
        core, name=name, grid=(n // tm,),
        in_specs=[pl.BlockSpec((tm, d), lambda i: (i, 0)), _resident(wd4.shape), gu_spec],
        out_specs=[gu_spec], out_shape=[jax.ShapeDtypeStruct(gu.shape, BF16)],
        inputs=[df, wd4, gu], exchange=exchange)
    return outs[0], xo


def ffn_dh(dgu, wgu, x, dxo, g, mod, s, k0, name, exchange=None):
    n, d = x.shape
    nk, fq = wgu.shape[1], wgu.shape[-1]
    tm = _ffn_tile(n)

    def core(dgu_ref, w_ref, x_ref, dxo_ref, g_ref, mod_ref, dx_ref, red_ref, dh_s):
        i = pl.program_id(0)
        dh = _dot(dgu_ref[0, 0], w_ref[0, 0], NT) + _dot(dgu_ref[1, 0], w_ref[1, 0], NT)
        for k in range(1, nk):
            dh = dh + _dot(dgu_ref[0, k], w_ref[0, k], NT) + _dot(dgu_ref[1, k], w_ref[1, k], NT)
        dh_s[...] = dh

        def epilogue(is_ctx):
            dx, sums = _pre_bwd(x_ref[...], dh_s[...], dxo_ref[...], g_ref[...], mod_ref, k0 + 1, is_ctx)
            dx_ref[...] = dx
            _red_add(red_ref, i == 0, is_ctx, sums)

        _by_tile_kind(i, tm, s, epilogue)

    tile = pl.BlockSpec((tm, d), lambda i: (i, 0))
    outs, xo = carrier_call(
        core, name=name, grid=(n // tm,),
        in_specs=[pl.BlockSpec((2, nk, tm, fq), lambda i: (0, 0, i, 0)), _resident(wgu.shape), tile, tile,
                  _vec_spec(d), _mod_whole(d)],
        out_specs=[tile, _red_whole(d)],
        out_shape=[jax.ShapeDtypeStruct((n, d), F32), jax.ShapeDtypeStruct((2, 8, d), F32)],
        scratch_shapes=[pltpu.VMEM((tm, d), F32)], inputs=[dgu, wgu, x, dxo, g, mod], exchange=exchange)
    return outs, xo


def grad_weight(a, b, name, a_lead=None, b_lead=None, b_cols=None, exchange=None):
    n = a.shape[-2]
    ka = a.shape[-1]
    kb = b_cols or b.shape[-1]
    nj = a_lead or b_lead or (b.shape[-1] // b_cols)
    tk = _pick(n, (1408, 1024, 768, 640, 512, 256))
    nk = n // tk

    def core(a_ref, b_ref, o_ref, acc):
        kk = pl.program_id(1)

        @pl.when(kk == 0)
        def _():
            acc[...] = jnp.zeros_like(acc)

        av = a_ref[0] if a_lead else a_ref[...]
        bv = b_ref[0] if b_lead else b_ref[...]
        acc[...] += _dot(av, bv, TN)

        @pl.when(kk == nk - 1)
        def _():
            o_ref[0] = acc[...].astype(BF16)

    a_spec = (pl.BlockSpec((1, tk, ka), lambda j, kk: (j, kk, 0)) if a_lead
              else pl.BlockSpec((tk, ka), lambda j, kk: (kk, 0)))
    if b_lead:
        b_spec = pl.BlockSpec((1, tk, kb), lambda j, kk: (j, kk, 0))
    elif b_cols:
        b_spec = pl.BlockSpec((tk, kb), lambda j, kk: (kk, j))
    else:
        b_spec = pl.BlockSpec((tk, kb), lambda j, kk: (kk, 0))
    outs, xo = carrier_call(
        core, name=name, grid=(nj, nk), in_specs=[a_spec, b_spec],
        out_specs=[pl.BlockSpec((1, ka, kb), lambda j, kk: (j, 0, 0))],
        out_shape=[jax.ShapeDtypeStruct((nj, ka, kb), BF16)],
        scratch_shapes=[pltpu.VMEM((ka, kb), F32)], inputs=[a, b], exchange=exchange)
    return outs[0], xo


def post_bwd(f, dxo, g, mod, k_gate, coef, s, name):
    n, d = f.shape

    def core(f_ref, dxo_ref, g_ref, mod_ref, df_ref, red_ref):
        i = pl.program_id(0)

        def body(is_ctx):
            ff = f_ref[...]
            dxo_ = dxo_ref[...]
            gg = g_ref[...]
            r = _rms(ff)
            fn = ff * r
            dy = (coef * _mod_rows(mod_ref, k_gate, is_ctx)) * dxo_
            dfn = dy * gg
            df_ref[...] = (r * (dfn - fn * jnp.mean(dfn * fn, axis=-1, keepdims=True))).astype(BF16)
            _red_add(red_ref, i == 0, is_ctx, [coef * (fn * gg) * dxo_, dy * fn])

        _by_tile_kind(i, TM, s, body)

    tile = pl.BlockSpec((TM, d), lambda i: (i, 0))
    outs, _ = carrier_call(
        core, name=name, grid=(n // TM,), in_specs=[tile, tile, _vec_spec(d), _mod_whole(d)],
        out_specs=[tile, _red_whole(d)],
        out_shape=[jax.ShapeDtypeStruct((n, d), BF16), jax.ShapeDtypeStruct((2, 8, d), F32)],
        inputs=[f, dxo, g, mod])
    return outs


def matmul_nt(a, b, name):
    m, k = a.shape
    n = b.shape[0]
    tm = _token_tile(m)

    def core(a_ref, b_ref, o_ref):
        o_ref[...] = _dot(a_ref[...], b_ref[...], NT)

    outs, _ = carrier_call(
        core, name=name, grid=(m // tm,),
        in_specs=[pl.BlockSpec((tm, k), lambda i: (i, 0)), pl.BlockSpec((n, k), lambda i: (0, 0))],
        out_specs=[pl.BlockSpec((tm, n), lambda i: (i, 0))], out_shape=[jax.ShapeDtypeStruct((m, n), F32)],
        inputs=[a, b])
    return outs[0]


def _rope_tables(s, n):
    t = jnp.arange(n)
    lane = jnp.arange(LANES)
    dd = lane % HEAD_DIM
    inv = ROPE_THETA ** (-(dd % ROPE_PAIRS).astype(F32) / ROPE_PAIRS)
    pos = jnp.where(dd[None, :] < HEAD_DIM // 2, (t // GRID_W)[:, None], (t % GRID_W)[:, None]).astype(F32)
    ang = pos * inv[None, :]
    live = (t < s)[:, None]
    first = ((dd % (2 * ROPE_PAIRS)) < ROPE_PAIRS)[None, :]
    cos = jnp.where(live, jnp.cos(ang), 1.0)
    sin = jnp.where(live, jnp.sin(ang), 0.0)
    sa = jnp.where(first, -sin, 0.0)
    sb = jnp.where(first, 0.0, sin)
    return cos.astype(F32), sa.astype(F32), sb.astype(F32)


def _rope(xv, cos, sa, sb):
    return (xv * cos + pltpu.roll(xv, LANES - ROPE_PAIRS, 1) * sa + pltpu.roll(xv, ROPE_PAIRS, 1) * sb)


def mix_in(x, g, mod, win8, tables, s, name):
    n, d = x.shape
    tm = _token_tile(n)
    nb = win8.shape[0]
    n_rope = 2 * NA_WIDTH // IN_BLOCK

    def core(x_ref, g_ref, mod_ref, w_ref, c_ref, sa_ref, sb_ref, hb_ref, qkv_ref, u_ref):
        i = pl.program_id(0)
        hb = _norm_mod(x_ref[...], g_ref[...], mod_ref, 3, 4, _is_ctx(i, tm, s))
        hb_ref[...] = hb
        cos, sa, sb = c_ref[...], sa_ref[...], sb_ref[...]
        for j in range(nb):
            y = _dot(hb, w_ref[j])
            if j < n_rope:
                for b in range(IN_BLOCK // LANES):
                    sl = slice(b * LANES, (b + 1) * LANES)
                    qkv_ref[:, j * IN_BLOCK + b * LANES:j * IN_BLOCK + (b + 1) * LANES] = (
                        _rope(y[:, sl], cos, sa, sb).astype(BF16))
            elif j < N_QKV_BLOCKS:
                qkv_ref[:, j * IN_BLOCK:(j + 1) * IN_BLOCK] = y.astype(BF16)
            else:
                u_ref[:, (j - N_QKV_BLOCKS) * IN_BLOCK:(j - N_QKV_BLOCKS + 1) * IN_BLOCK] = y

    tab = pl.BlockSpec((tm, LANES), lambda i: (i, 0))
    row = lambda w: pl.BlockSpec((tm, w), lambda i: (i, 0))
    outs, _ = carrier_call(
        core, name=name, grid=(n // tm,),
        in_specs=[row(d), _vec_spec(d), _mod_whole(d), pl.BlockSpec((nb, d, IN_BLOCK), lambda i: (0, 0, 0)),
                  tab, tab, tab],
        out_specs=[row(d), row(3 * NA_WIDTH), row(POOL_WIDTH)],
        out_shape=[jax.ShapeDtypeStruct((n, d), BF16), jax.ShapeDtypeStruct((n, 3 * NA_WIDTH), BF16),
                   jax.ShapeDtypeStruct((n, POOL_WIDTH), F32)],
        inputs=[x, g, mod, win8, *tables])
    return outs


def qkv_bwd(dq, dk, dv, du, tables, name):
    n = dq.shape[0]
    w = NA_WIDTH

    def core(dq_ref, dk_ref, dv_ref, du_ref, c_ref, sa_ref, sb_ref, o_ref):
        cos, sa, sb = c_ref[...], -sa_ref[...], -sb_ref[...]
        for b in range(N_HEAD_BLOCKS):
            sl = slice(b * LANES, (b + 1) * LANES)
            o_ref[:, b * LANES:(b + 1) * LANES] = _rope(dq_ref[:, sl], cos, sa, sb).astype(BF16)
            o_ref[:, w + b * LANES:w + (b + 1) * LANES] = _rope(dk_ref[:, sl], cos, sa, sb).astype(BF16)
        o_ref[:, 2 * w:3 * w] = dv_ref[...].astype(BF16)
        o_ref[:, 3 * w:] = du_ref[...].astype(BF16)

    tab = pl.BlockSpec((TM, LANES), lambda i: (i, 0))
    tile = pl.BlockSpec((TM, w), lambda i: (i, 0))
    outs, _ = carrier_call(
        core, name=name, grid=(n // TM,), in_specs=[tile, tile, tile, tile, tab, tab, tab],
        out_specs=[pl.BlockSpec((TM, 4 * w), lambda i: (i, 0))],
        out_shape=[jax.ShapeDtypeStruct((n, 4 * w), BF16)], inputs=[dq, dk, dv, du, *tables])
    return outs[0]


def mix_out(na, py, wout, x, g, mod, s, name):
    n, d = x.shape
    tm = _token_tile(n)
    half = na.shape[1]

    def core(na_ref, py_ref, w_ref, x_ref, g_ref, mod_ref, f_ref, xo_ref):
        i = pl.program_id(0)
        ff = _dot(na_ref[...], w_ref[:half, :]) + _dot(py_ref[...], w_ref[half:, :])
        f_ref[...] = ff
        xo_ref[...] = _post(x_ref[...], ff, g_ref[...], mod_ref, 5, 1.0, _is_ctx(i, tm, s))

    tile = pl.BlockSpec((tm, d), lambda i: (i, 0))
    htile = pl.BlockSpec((tm, half), lambda i: (i, 0))
    outs, _ = carrier_call(
        core, name=name, grid=(n // tm,),
        in_specs=[htile, htile, pl.BlockSpec((2 * half, d), lambda i: (0, 0)), tile, _vec_spec(d), _mod_whole(d)],
        out_specs=[tile, tile],
        out_shape=[jax.ShapeDtypeStruct((n, d), F32), jax.ShapeDtypeStruct((n, d), F32)],
        inputs=[na, py, wout, x, g, mod])
    return outs


def grad_wout(na, py, dfm, name):
    n, half = na.shape
    d = dfm.shape[1]
    tk = _pick(n, (1408, 1024, 768, 640, 512, 256))
    nk = n // tk

    def core(na_ref, py_ref, b_ref, o_ref, acc):
        hh, kk = pl.program_id(0), pl.program_id(1)

        @pl.when(kk == 0)
        def _():
            acc[...] = jnp.zeros_like(acc)

        @pl.when(hh == 0)
        def _():
            acc[...] += _dot(na_ref[...], b_ref[...], TN)

        @pl.when(hh == 1)
        def _():
            acc[...] += _dot(py_ref[...], b_ref[...], TN)

        @pl.when(kk == nk - 1)
        def _():
            o_ref[0] = acc[...].astype(BF16)

    htile = pl.BlockSpec((tk, half), lambda hh, kk: (kk, 0))
    outs, _ = carrier_call(
        core, name=name, grid=(2, nk), in_specs=[htile, htile, pl.BlockSpec((tk, d), lambda hh, kk: (kk, 0))],
        out_specs=[pl.BlockSpec((1, half, d), lambda hh, kk: (hh, 0, 0))],
        out_shape=[jax.ShapeDtypeStruct((2, half, d), BF16)],
        scratch_shapes=[pltpu.VMEM((half, d), F32)], inputs=[na, py, dfm])
    return outs[0]


def mix_dh(dqkvu, win8, x, dxo, g, mod, s, name, exchange=None):
    n, d = x.shape
    tm = _token_tile(n)
    nb = win8.shape[0]

    def core(dq_ref, w_ref, x_ref, dxo_ref, g_ref, mod_ref, dx_ref, red_ref):
        i = pl.program_id(0)
        dh = _dot(dq_ref[:, :IN_BLOCK], w_ref[0], NT)
        for j in range(1, nb):
            dh = dh + _dot(dq_ref[:, j * IN_BLOCK:(j + 1) * IN_BLOCK], w_ref[j], NT)
        is_ctx = _is_ctx(i, tm, s)
        dx, sums = _pre_bwd(x_ref[...], dh, dxo_ref[...], g_ref[...], mod_ref, 4, is_ctx)
        dx_ref[...] = dx
        _red_add(red_ref, i == 0, is_ctx, sums)

    tile = pl.BlockSpec((tm, d), lambda i: (i, 0))
    outs, xo = carrier_call(
        core, name=name, grid=(n // tm,),
        in_specs=[pl.BlockSpec((tm, nb * IN_BLOCK), lambda i: (i, 0)),
                  pl.BlockSpec((nb, d, IN_BLOCK), lambda i: (0, 0, 0)), tile, tile, _vec_spec(d), _mod_whole(d)],
        out_specs=[tile, _red_whole(d)],
        out_shape=[jax.ShapeDtypeStruct((n, d), F32), jax.ShapeDtypeStruct((2, 8, d), F32)],
        inputs=[dqkvu, win8, x, dxo, g, mod], exchange=exchange)
    return outs, xo


def _na_consts():
    j = np.arange(GRID_W)
    col_start = np.clip(j - NA_KW // 2, 0, GRID_W - NA_KW)
    valid = (j[None, :] >= col_start[:, None]) & (j[None, :] < col_start[:, None] + NA_KW)
    dc = np.clip(j[None, :] - j[:, None] + NA_KW - 1, 0, 2 * NA_KW - 2)
    onehot = np.zeros((LANES, GRID_W, GRID_W), np.float32)
    for d in range(2 * NA_KW - 1):
        onehot[d] = ((dc == d) & valid).astype(np.float32)
    negmask = np.where(valid, 0.0, NEG_INF).astype(np.float32)
    return onehot.reshape(LANES, GRID_W * GRID_W), np.tile(negmask, (1, NA_KH))


def bias_tables(rpb, name):
    onehot, negmask = _na_consts()
    nj = 2 * NA_KH - 1
    rows = NA_HEADS * nj
    a = jnp.pad(rpb.reshape(rows, 2 * NA_KW - 1), ((0, 0), (0, LANES - (2 * NA_KW - 1))))

    def body(a_ref, e_ref, o_ref):
        o_ref[...] = _dot(a_ref[...], e_ref[...], precision=HIGHEST)

    t = pl.pallas_call(
        body, name=name, out_shape=jax.ShapeDtypeStruct((rows, GRID_W * GRID_W), F32),
        in_specs=[pl.BlockSpec(memory_space=pltpu.VMEM)] * 2,
        out_specs=pl.BlockSpec(memory_space=pltpu.VMEM),
    )(a, jnp.asarray(onehot))
    t = t.reshape(NA_HEADS, nj, GRID_W, GRID_W)
    tb = jnp.stack([t[:, j0:j0 + NA_KH] for j0 in range(NA_KH)])
    tb = tb.transpose(0, 1, 3, 2, 4).reshape(NA_KH, NA_HEADS, GRID_W, NA_KH * GRID_W)
    return tb + jnp.asarray(negmask)[None, None]


def bias_tables_bwd(dtb, name):
    onehot, _ = _na_consts()
    nj = 2 * NA_KH - 1
    d5 = dtb.reshape(NA_KH, NA_HEADS, GRID_W, NA_KH, GRID_W).transpose(0, 3, 1, 2, 4)
    d2 = d5.reshape(NA_KH * NA_KH * NA_HEADS, GRID_W * GRID_W)

    def body(d_ref, e_ref, o_ref):
        r = _dot(d_ref[...], e_ref[...], NT, precision=HIGHEST)
        for j in range(nj):
            acc = jnp.zeros((NA_HEADS, LANES), F32)
            for j0 in range(NA_KH):
                kk = j - j0
                if 0 <= kk < NA_KH:
                    base = (j0 * NA_KH + kk) * NA_HEADS
                    acc = acc + r[base:base + NA_HEADS, :]
            o_ref[j] = acc

    out = pl.pallas_call(
        body, name=name, out_shape=jax.ShapeDtypeStruct((nj, NA_HEADS, LANES), F32),
        in_specs=[pl.BlockSpec(memory_space=pltpu.VMEM)] * 2,
        out_specs=pl.BlockSpec(memory_space=pltpu.VMEM),
        compiler_params=pltpu.CompilerParams(vmem_limit_bytes=VMEM_LIMIT),
    )(d2, jnp.asarray(onehot))
    return out[:, :, :2 * NA_KW - 1].transpose(1, 0, 2)


def _head_masks():
    lane = lax.broadcasted_iota(jnp.int32, (1, LANES), 1)
    return [(lane >= h * HEAD_DIM) & (lane < (h + 1) * HEAD_DIM) for h in range(HEADS_PER_BLOCK)]


def _row_window(r, rows):
    rs = jnp.clip(r - NA_KH // 2, 0, rows - NA_KH)
    return rs - r + NA_KH - 1, pl.multiple_of(rs * GRID_W, GRID_W)


def _stack_heads(t, masks):
    return jnp.concatenate([jnp.where(mk, t, jnp.zeros_like(t)) for mk in masks], axis=0)


def _unstack_heads(t2, masks):
    out = t2[(HEADS_PER_BLOCK - 1) * GRID_W:, :]
    for h in reversed(range(HEADS_PER_BLOCK - 1)):
        out = jnp.where(masks[h], t2[h * GRID_W:(h + 1) * GRID_W, :], out)
    return out


NA_ROWS_PER_STEP = 4
NA_STEP = NA_ROWS_PER_STEP * GRID_W
SLAB = NA_KH * GRID_W
K_COL = N_HEAD_BLOCKS
V_COL = 2 * N_HEAD_BLOCKS


def na_fwd(qkv, tb, s, name, exchange=None):
    n = qkv.shape[0]
    l = n - s
    rows = s // GRID_W
    rr = NA_ROWS_PER_STEP
    x_steps = rows // rr

    def core(q_ref, k_ref, v_ref, kc_ref, vc_ref, tb_ref, o_ref, lse_ref):
        rb = pl.program_id(1)

        @pl.when(rb >= x_steps)
        def _():
            o_ref[...] = jnp.zeros_like(o_ref)
            lse_ref[...] = jnp.zeros_like(lse_ref)

        @pl.when(rb < x_steps)
        def _():
            masks = _head_masks()
            kcb, vcb = kc_ref[...], vc_ref[...]
            wins, scores = [], []
            for t in range(rr):
                j0, off = _row_window(rb * rr + t, rows)
                q2 = _stack_heads(q_ref[t * GRID_W:(t + 1) * GRID_W, :] * ATT_SCALE, masks)
                bias = tb_ref[j0].reshape(HEADS_PER_BLOCK * GRID_W, SLAB)
                s_loc = _dot(q2, k_ref[pl.ds(off, SLAB), :], NT) + bias
                s_ctx = _dot(q2, kcb, NT)
                wins.append(off)
                scores.append((s_loc, s_ctx))
            probs = []
            for s_loc, s_ctx in scores:
                m = jnp.maximum(jnp.max(s_loc, axis=-1, keepdims=True), jnp.max(s_ctx, axis=-1, keepdims=True))
                p_loc = jnp.exp(s_loc - m)
                p_ctx = jnp.exp(s_ctx - m)
                den = jnp.sum(p_loc, axis=-1, keepdims=True) + jnp.sum(p_ctx, axis=-1, keepdims=True)
                probs.append((p_loc.astype(BF16), p_ctx.astype(BF16), den, m + jnp.log(den)))
            for t, (p_loc, p_ctx, den, lse2) in enumerate(probs):
                o2 = (_dot(p_loc, v_ref[pl.ds(wins[t], SLAB), :]) + _dot(p_ctx, vcb)) / den
                o_ref[t * GRID_W:(t + 1) * GRID_W, :] = _unstack_heads(o2, masks).astype(BF16)
                lse_ref[0, t * GRID_W:(t + 1) * GRID_W, :] = _unstack_heads(lse2, masks)

    cb = s // l
    outs, xo = carrier_call(
        core, name=name, grid=(N_HEAD_BLOCKS, n // NA_STEP),
        in_specs=[pl.BlockSpec((NA_STEP, LANES), lambda hb, rb: (jnp.minimum(rb, x_steps - 1), hb)),
                  pl.BlockSpec((s, LANES), lambda hb, rb: (0, K_COL + hb)),
                  pl.BlockSpec((s, LANES), lambda hb, rb: (0, V_COL + hb)),
                  pl.BlockSpec((l, LANES), lambda hb, rb: (cb, K_COL + hb)),
                  pl.BlockSpec((l, LANES), lambda hb, rb: (cb, V_COL + hb)),
                  pl.BlockSpec((NA_KH, HEADS_PER_BLOCK, GRID_W, SLAB), lambda hb, rb: (0, hb, 0, 0))],
        out_specs=[pl.BlockSpec((NA_STEP, LANES), lambda hb, rb: (rb, hb)),
                   pl.BlockSpec((1, NA_STEP, LANES), lambda hb, rb: (hb, rb, 0))],
        out_shape=[jax.ShapeDtypeStruct((n, NA_WIDTH), BF16), jax.ShapeDtypeStruct((N_HEAD_BLOCKS, n, LANES), F32)],
        inputs=[qkv, qkv, qkv, qkv, qkv, tb], exchange=exchange)
    return outs, xo


def na_bwd(qkv, tb, o, dmix, lse, s, name, exchange=None):
    n = qkv.shape[0]
    l = n - s
    rows = s // GRID_W
    rr = NA_ROWS_PER_STEP
    x_steps = rows // rr

    def core(q_ref, k_ref, v_ref, kc_ref, vc_ref, tb_ref, o_ref, do_ref, lse_ref, dq_ref, dk_ref, dv_ref, dtb_ref):
        rb = pl.program_id(1)

        @pl.when(rb == 0)
        def _():
            dk_ref[...] = jnp.zeros_like(dk_ref)
            dv_ref[...] = jnp.zeros_like(dv_ref)
            dtb_ref[...] = jnp.zeros_like(dtb_ref)

        @pl.when(rb >= x_steps)
        def _():
            dq_ref[...] = jnp.zeros_like(dq_ref)

        @pl.when(rb < x_steps)
        def _():
            masks = _head_masks()
            kcb, vcb = kc_ref[...], vc_ref[...]
            stage1 = []
            for t in range(rr):
                j0, off = _row_window(rb * rr + t, rows)
                sl = slice(t * GRID_W, (t + 1) * GRID_W)
                q2 = _stack_heads(q_ref[sl, :] * ATT_SCALE, masks)
                do_f = do_ref[sl, :]
                do2 = _stack_heads(do_f.astype(BF16), masks)
                dd = do_f * o_ref[sl, :].astype(F32)
                delta2 = jnp.concatenate(
                    [jnp.sum(jnp.where(mk, dd, 0.0), axis=-1, keepdims=True) for mk in masks], axis=0)
                lse_t = lse_ref[0, sl, :]
                lse2 = jnp.concatenate(
                    [lse_t[:, h * HEAD_DIM:h * HEAD_DIM + 1] for h in range(HEADS_PER_BLOCK)], axis=0)
                kslab = k_ref[pl.ds(off, SLAB), :]
                vslab = v_ref[pl.ds(off, SLAB), :]
                bias = tb_ref[j0].reshape(HEADS_PER_BLOCK * GRID_W, SLAB)
                s_loc = _dot(q2, kslab, NT) + bias - lse2
                s_ctx = _dot(q2, kcb, NT) - lse2
                dp_loc = _dot(do2, vslab, NT) - delta2
                dp_ctx = _dot(do2, vcb, NT) - delta2
                stage1.append((j0, off, q2, do2, s_loc, s_ctx, dp_loc, dp_ctx))
            stage2 = []
            for j0, off, q2, do2, s_loc, s_ctx, dp_loc, dp_ctx in stage1:
                p_loc = jnp.exp(s_loc)
                p_ctx = jnp.exp(s_ctx)
                ds_loc = p_loc * dp_loc
                dtb_ref[j0] += ds_loc.reshape(HEADS_PER_BLOCK, GRID_W, SLAB)
                stage2.append((off, q2, do2, p_loc.astype(BF16), p_ctx.astype(BF16), ds_loc.astype(BF16),
                               (p_ctx * dp_ctx).astype(BF16)))
            for t, (off, q2, do2, p_loc, p_ctx, ds_loc, ds_ctx) in enumerate(stage2):
                dq2 = (_dot(ds_loc, k_ref[pl.ds(off, SLAB), :]) + _dot(ds_ctx, kcb)) * ATT_SCALE
                dq_ref[t * GRID_W:(t + 1) * GRID_W, :] = _unstack_heads(dq2, masks)
                dk_ref[pl.ds(off, SLAB), :] += _dot(ds_loc, q2, TN)
                dv_ref[pl.ds(off, SLAB), :] += _dot(p_loc, do2, TN)
                dk_ref[s:, :] += _dot(ds_ctx, q2, TN)
                dv_ref[s:, :] += _dot(p_ctx, do2, TN)

    cb = s // l
    clamp = lambda hb, rb: (jnp.minimum(rb, x_steps - 1), hb)
    tile_in = pl.BlockSpec((NA_STEP, LANES), clamp)
    whole_out = pl.BlockSpec((n, LANES), lambda hb, rb: (0, hb))
    tbs = pl.BlockSpec((NA_KH, HEADS_PER_BLOCK, GRID_W, SLAB), lambda hb, rb: (0, hb, 0, 0))
    f32n = jax.ShapeDtypeStruct((n, NA_WIDTH), F32)
    outs, xo = carrier_call(
        core, name=name, grid=(N_HEAD_BLOCKS, n // NA_STEP),
        in_specs=[tile_in,
                  pl.BlockSpec((s, LANES), lambda hb, rb: (0, K_COL + hb)),
                  pl.BlockSpec((s, LANES), lambda hb, rb: (0, V_COL + hb)),
                  pl.BlockSpec((l, LANES), lambda hb, rb: (cb, K_COL + hb)),
                  pl.BlockSpec((l, LANES), lambda hb, rb: (cb, V_COL + hb)),
                  tbs, tile_in, tile_in,
                  pl.BlockSpec((1, NA_STEP, LANES), lambda hb, rb: (hb, jnp.minimum(rb, x_steps - 1), 0))],
        out_specs=[pl.BlockSpec((NA_STEP, LANES), lambda hb, rb: (rb, hb)), whole_out, whole_out, tbs],
        out_shape=[f32n, f32n, f32n, jax.ShapeDtypeStruct((NA_KH, NA_HEADS, GRID_W, SLAB), F32)],
        inputs=[qkv, qkv, qkv, qkv, qkv, tb, o, dmix, lse], exchange=exchange)
    return outs, xo


def ctx_attn_fwd(qkv, na, s, name):
    n = qkv.shape[0]
    l = n - s
    cb = s // l

    def core(q_ref, k_ref, v_ref, na_in, o_ref, lse_ref):
        masks = _head_masks()
        qt, kb, vb = q_ref[...], k_ref[...], v_ref[...]
        o_acc = jnp.zeros((l, LANES), F32)
        lse_acc = jnp.zeros((l, LANES), F32)
        for h in range(HEADS_PER_BLOCK):
            qh = jnp.where(masks[h], qt, jnp.zeros_like(qt))
            sc = _dot(qh, kb, NT) * ATT_SCALE
            m = jnp.max(sc, axis=-1, keepdims=True)
            p = jnp.exp(sc - m)
            den = jnp.sum(p, axis=-1, keepdims=True)
            o_acc = jnp.where(masks[h], _dot(p.astype(BF16), vb) / den, o_acc)
            lse_acc = jnp.where(masks[h], m + jnp.log(den), lse_acc)
        o_ref[...] = o_acc.astype(BF16)
        lse_ref[0] = lse_acc

    outs, _ = carrier_call(
        core, name=name, grid=(N_HEAD_BLOCKS,),
        in_specs=[pl.BlockSpec((l, LANES), lambda hb: (cb, hb)), pl.BlockSpec((l, LANES), lambda hb: (cb, K_COL + hb)),
                  pl.BlockSpec((l, LANES), lambda hb: (cb, V_COL + hb)), ANY],
        out_specs=[pl.BlockSpec((l, LANES), lambda hb: (cb, hb)), pl.BlockSpec((1, l, LANES), lambda hb: (hb, 0, 0))],
        out_shape=[jax.ShapeDtypeStruct(na.shape, BF16), jax.ShapeDtypeStruct((N_HEAD_BLOCKS, l, LANES), F32)],
        inputs=[qkv, qkv, qkv, na], aliases={3: 0})
    return outs


def ctx_attn_bwd(qkv, na, dmix, lse, dq, dk, dv, s, name):
    n = qkv.shape[0]
    l = n - s
    cb = s // l

    def core(q_ref, k_ref, v_ref, o_ref, do_ref, lse_ref, dq_in, dk_in, dv_in, dq_ref, dk_ref, dv_ref):
        masks = _head_masks()
        qt, kb, vb = q_ref[...], k_ref[...], v_ref[...]
        do_f = do_ref[...]
        dd = do_f * o_ref[...].astype(F32)
        do_b = do_f.astype(BF16)
        lse_t = lse_ref[0]
        dq_acc = jnp.zeros((l, LANES), F32)
        dk_acc = jnp.zeros((l, LANES), F32)
        dv_acc = jnp.zeros((l, LANES), F32)
        for h in range(HEADS_PER_BLOCK):
            qh = jnp.where(masks[h], qt, jnp.zeros_like(qt))
            doh = jnp.where(masks[h], do_b, jnp.zeros_like(do_b))
            delta = jnp.sum(jnp.where(masks[h], dd, 0.0), axis=-1, keepdims=True)
            p = jnp.exp(_dot(qh, kb, NT) * ATT_SCALE - lse_t[:, h * HEAD_DIM:h * HEAD_DIM + 1])
            ds = (p * (_dot(doh, vb, NT) - delta)).astype(BF16)
            dq_acc = jnp.where(masks[h], _dot(ds, kb) * ATT_SCALE, dq_acc)
            dk_acc = dk_acc + _dot(ds, qh, TN)
            dv_acc = dv_acc + _dot(p.astype(BF16), doh, TN)
        dq_ref[...] = dq_acc
        dk_ref[...] = dk_in[...] + dk_acc * ATT_SCALE
        dv_ref[...] = dv_in[...] + dv_acc

    blk = pl.BlockSpec((l, LANES), lambda hb: (cb, hb))
    f32n = jax.ShapeDtypeStruct((n, NA_WIDTH), F32)
    outs, _ = carrier_call(
        core, name=name, grid=(N_HEAD_BLOCKS,),
        in_specs=[blk, pl.BlockSpec((l, LANES), lambda hb: (cb, K_COL + hb)),
                  pl.BlockSpec((l, LANES), lambda hb: (cb, V_COL + hb)), blk, blk,
                  pl.BlockSpec((1, l, LANES), lambda hb: (hb, 0, 0)), ANY, blk, blk],
        out_specs=[blk, blk, blk], out_shape=[f32n, f32n, f32n],
        inputs=[qkv, qkv, qkv, na, dmix, lse, dq, dk, dv], aliases={6: 0, 7: 1, 8: 2})
    return outs


def _pool_consts(l):
    assert l == TM
    mem = np.zeros((2, POOL_GROUPS, TM, TM), np.float32)
    inv = np.zeros((2, POOL_GROUPS, TM, LANES), np.float32)
    for which, length in ((0, GRID_W), (1, l)):
        t = np.arange(length)
        for g, w in enumerate(POOL_WINDOWS):
            lo = np.clip(t - w // 2, 0, length)
            hi = np.clip(t - w // 2 + w, 0, length)
            blockm = ((t[None, :] >= lo[:, None]) & (t[None, :] < hi[:, None])).astype(np.float32)
            cnt = (hi - lo).astype(np.float32)
            for b in range(TM // length):
                mem[which, g, b * length:(b + 1) * length, b * length:(b + 1) * length] = blockm
                inv[which, g, b * length:(b + 1) * length, :] = (1.0 / cnt)[:, None]
    return mem, np.ascontiguousarray(mem.transpose(0, 1, 3, 2)), inv


def _split_dot(m01, val):
    hi = val.astype(BF16)
    lo = (val - hi.astype(F32)).astype(BF16)
    return _dot(m01, hi) + _dot(m01, lo)


def pool_fwd(u, mem, inv, wp, scale, nx_tiles, name):
    n = u.shape[0]

    def core(u_ref, m_ref, i_ref, wp_ref, s_ref, o_ref):
        for g in range(POOL_GROUPS):
            sl = slice(g * POOL_CH, (g + 1) * POOL_CH)
            ug = u_ref[:, sl]
            dg = _split_dot(m_ref[0, g], ug) * i_ref[0, g] - ug
            o_ref[:, sl] = (_dot(dg.astype(BF16), wp_ref[g]) * s_ref[:, sl]).astype(BF16)

    grp = lambda i: (i // nx_tiles, 0, 0, 0)
    outs, _ = carrier_call(
        core, name=name, grid=(n // TM,),
        in_specs=[pl.BlockSpec((TM, POOL_WIDTH), lambda i: (i, 0)),
                  pl.BlockSpec((1, POOL_GROUPS, TM, TM), grp),
                  pl.BlockSpec((1, POOL_GROUPS, TM, LANES), grp),
                  pl.BlockSpec((POOL_GROUPS, POOL_CH, POOL_CH), lambda i: (0, 0, 0)),
                  pl.BlockSpec((1, POOL_WIDTH), lambda i: (0, 0))],
        out_specs=[pl.BlockSpec((TM, POOL_WIDTH), lambda i: (i, 0))],
        out_shape=[jax.ShapeDtypeStruct((n, POOL_WIDTH), BF16)], inputs=[u, mem, inv, wp, scale])
    return outs[0]


def pool_bwd(dmix, u, mem, mem_t, inv, wp, scale, nx_tiles, name):
    n = u.shape[0]

    def core(dy_ref, u_ref, m_ref, mt_ref, i_ref, wp_ref, s_ref, du_ref, dwp_ref, dsc_ref):
        @pl.when(pl.program_id(0) == 0)
        def _():
            dwp_ref[...] = jnp.zeros_like(dwp_ref)
            dsc_ref[...] = jnp.zeros_like(dsc_ref)

        for g in range(POOL_GROUPS):
            sl = slice(g * POOL_CH, (g + 1) * POOL_CH)
            ug = u_ref[:, sl]
            dy = dy_ref[:, sl]
            dg = (_split_dot(m_ref[0, g], ug) * i_ref[0, g] - ug).astype(BF16)
            z = _dot(dg, wp_ref[g])
            dsc_ref[0:1, sl] += jnp.sum(dy * z, axis=0, keepdims=True)
            dz = (dy * s_ref[:, sl]).astype(BF16)
            dwp_ref[g] += _dot(dg, dz, TN)
            dd = _dot(dz, wp_ref[g], NT)
            du_ref[:, sl] = _split_dot(mt_ref[0, g], dd * i_ref[0, g]) - dd

    grp = lambda i: (i // nx_tiles, 0, 0, 0)
    outs, _ = carrier_call(
        core, name=name, grid=(n // TM,),
        in_specs=[pl.BlockSpec((TM, POOL_WIDTH), lambda i: (i, 1)),
                  pl.BlockSpec((TM, POOL_WIDTH), lambda i: (i, 0)),
                  pl.BlockSpec((1, POOL_GROUPS, TM, TM), grp),
                  pl.BlockSpec((1, POOL_GROUPS, TM, TM), grp),
                  pl.BlockSpec((1, POOL_GROUPS, TM, LANES), grp),
                  pl.BlockSpec((POOL_GROUPS, POOL_CH, POOL_CH), lambda i: (0, 0, 0)),
                  pl.BlockSpec((1, POOL_WIDTH), lambda i: (0, 0))],
        out_specs=[pl.BlockSpec((TM, POOL_WIDTH), lambda i: (i, 0)),
                   pl.BlockSpec((POOL_GROUPS, POOL_CH, POOL_CH), lambda i: (0, 0, 0)),
                   pl.BlockSpec((8, POOL_WIDTH), lambda i: (0, 0))],
        out_shape=[jax.ShapeDtypeStruct((n, POOL_WIDTH), F32),
                   jax.ShapeDtypeStruct((POOL_GROUPS, POOL_CH, POOL_CH), F32),
                   jax.ShapeDtypeStruct((8, POOL_WIDTH), F32)],
        inputs=[dmix, u, mem, mem_t, inv, wp, scale])
    return outs


MOD_ROWS = 16


def mod_fwd(cvecs, w, b, name):
    _, d = cvecs.shape
    cl = w.shape[2]
    tc = _pick(cl, (384, 128))

    def core(c_ref, w_ref, b_ref, o_ref):
        a = _silu(c_ref[...]).astype(BF16)
        o_ref[0] = _dot(a, w_ref[0].astype(BF16)) + b_ref[0]

    outs, _ = carrier_call(
        core, name=name, grid=(DEPTH, cl // tc),
        in_specs=[pl.BlockSpec((MOD_ROWS, d), lambda li, j: (0, 0)),
                  pl.BlockSpec((1, d, tc), lambda li, j: (li, 0, j)),
                  pl.BlockSpec((1, 1, tc), lambda li, j: (li, 0, j))],
        out_specs=[pl.BlockSpec((1, MOD_ROWS, tc), lambda li, j: (li, 0, j))],
        out_shape=[jax.ShapeDtypeStruct((DEPTH, MOD_ROWS, cl), F32)], inputs=[cvecs, w, b])
    return outs[0]


def mod_bwd(cvecs, dm, w, name):
    _, d = cvecs.shape
    cl = w.shape[2]
    tc = _pick(cl, (384, 128))

    def core(c_ref, dm_ref, w_ref, dw_ref, da_ref):
        @pl.when((pl.program_id(0) == 0) & (pl.program_id(1) == 0))
        def _():
            da_ref[...] = jnp.zeros_like(da_ref)

        a = _silu(c_ref[...]).astype(BF16)
        dmb = dm_ref[0].astype(BF16)
        dw_ref[0] = _dot(a, dmb, TN)
        da_ref[...] += _dot(dmb, w_ref[0].astype(BF16), NT)

    outs, _ = carrier_call(
        core, name=name, grid=(DEPTH, cl // tc),
        in_specs=[pl.BlockSpec((MOD_ROWS, d), lambda li, j: (0, 0)),
                  pl.BlockSpec((1, MOD_ROWS, tc), lambda li, j: (li, 0, j)),
                  pl.BlockSpec((1, d, tc), lambda li, j: (li, 0, j))],
        out_specs=[pl.BlockSpec((1, d, tc), lambda li, j: (li, 0, j)),
                   pl.BlockSpec((MOD_ROWS, d), lambda li, j: (0, 0))],
        out_shape=[jax.ShapeDtypeStruct((DEPTH, d, cl), F32), jax.ShapeDtypeStruct((MOD_ROWS, d), F32)],
        inputs=[cvecs, dm, w])
    return outs


def loss_head(y, target, name):
    n, d = y.shape
    s = target.shape[0]
    nt, nx = n // TM, s // TM

    def core(y_ref, t_ref, l_ref, dy_ref, acc_ref):
        i = pl.program_id(0)

        @pl.when(i == 0)
        def _():
            acc_ref[...] = jnp.zeros_like(acc_ref)

        @pl.when(i < nx)
        def _():
            e = y_ref[...] - t_ref[...]
            dy_ref[...] = e * (1.0 / d)
            acc_ref[...] += jnp.sum(e * e, axis=0, keepdims=True)

        @pl.when(i >= nx)
        def _():
            dy_ref[...] = jnp.zeros_like(dy_ref)

        @pl.when(i == nt - 1)
        def _():
            l_ref[...] = jnp.sum(acc_ref[...], axis=1, keepdims=True) * (0.5 / d)

    tile = pl.BlockSpec((TM, d), lambda i: (i, 0))
    outs, _ = carrier_call(
        core, name=name, grid=(nt,),
        in_specs=[tile, pl.BlockSpec((TM, d), lambda i: (jnp.minimum(i, nx - 1), 0))],
        out_specs=[pl.BlockSpec((1, 1), lambda i: (0, 0)), tile],
        out_shape=[jax.ShapeDtypeStruct((1, 1), F32), jax.ShapeDtypeStruct((n, d), F32)],
        scratch_shapes=[pltpu.VMEM((1, d), F32)], inputs=[y, target])
    return outs


def sum_devices(v, name):
    _, r, c = v.shape
    tr = _pick(r, (64, 8))

    def core(v_ref, o_ref):
        acc = v_ref[0]
        for p in range(1, N_DEV):
            acc = acc + v_ref[p]
        o_ref[...] = acc

    outs, _ = carrier_call(
        core, name=name, grid=(r // tr,), in_specs=[pl.BlockSpec((N_DEV, tr, c), lambda i: (0, i, 0))],
        out_specs=[pl.BlockSpec((tr, c), lambda i: (i, 0))], out_shape=[jax.ShapeDtypeStruct((r, c), F32)],
        inputs=[v])
    return outs[0]


def cctx_grad(parts, c_ctx, name):
    d = c_ctx.shape[1]

    def body(p_ref, c_ref, o_ref):
        acc = p_ref[0]
        for p in range(1, N_DEV):
            acc = acc + p_ref[p]
        o_ref[...] = acc[8:9, :] * _dsilu(c_ref[...])

    return pl.pallas_call(
        body, name=name, out_shape=jax.ShapeDtypeStruct((1, d), F32),
        in_specs=[pl.BlockSpec(memory_space=pltpu.VMEM)] * 2,
        out_specs=pl.BlockSpec(memory_space=pltpu.VMEM),
    )(parts, c_ctx)


def _adam_math(w, g, m, v):
    m2 = ADAM_B1 * m + (1.0 - ADAM_B1) * g
    v2 = ADAM_B2 * v + (1.0 - ADAM_B2) * (g * g)
    m_hat = m2 / (1.0 - ADAM_B1 ** ADAM_STEP)
    v_hat = v2 / (1.0 - ADAM_B2 ** ADAM_STEP)
    delta = -ADAM_LR * (m_hat / (jnp.sqrt(v_hat) + ADAM_EPS) + ADAM_WD * w)
    return delta, m2, v2


def adamw(w, g, m, v, name):
    r, c = w.shape
    tr = _pick(r, (256, 128, 64, 32, 16, 8, r))

    def core(w_ref, g_ref, m_ref, v_ref, d_ref, m2_ref, v2_ref):
        d_ref[...], m2_ref[...], v2_ref[...] = _adam_math(w_ref[...], g_ref[...], m_ref[...], v_ref[...])

    tile = pl.BlockSpec((tr, c), lambda i: (i, 0))
    out = jax.ShapeDtypeStruct((r, c), F32)
    outs, _ = carrier_call(core, name=name, grid=(r // tr,), in_specs=[tile] * 4, out_specs=[tile] * 3,
                           out_shape=[out] * 3, inputs=[w, g, m, v])
    return outs


def reduce_adamw(recv, w, m, v, name):
    r, c = w.shape
    tr = _pick(r, (256, 128, 64, 8))

    def core(recv_ref, w_ref, m_ref, v_ref, g_ref, d_ref, m2_ref, v2_ref):
        acc = recv_ref[0].astype(F32)
        for p in range(1, N_DEV):
            acc = acc + recv_ref[p].astype(F32)
        g_ref[...] = acc
        d_ref[...], m2_ref[...], v2_ref[...] = _adam_math(w_ref[...], acc, m_ref[...], v_ref[...])

    tile = pl.BlockSpec((tr, c), lambda i: (i, 0))
    out = jax.ShapeDtypeStruct((r, c), F32)
    outs, _ = carrier_call(
        core, name=name, grid=(r // tr,),
        in_specs=[pl.BlockSpec((N_DEV, tr, c), lambda i: (0, i, 0)), tile, tile, tile],
        out_specs=[tile] * 4, out_shape=[out] * 4, inputs=[recv, w, m, v])
    return outs


def kernel(x, c, ctx, c_ctx, w_mod, b_mod, norm_g, w_ffn_gate_up, w_ffn_down, w_in, w_out, na_rpb, w_pool, pool_scale, loss_target, m_c_ctx, m_w_mod, m_b_mod, m_norm_g, m_w_ffn_gate_up, m_w_ffn_down, m_w_in, m_w_out, m_na_rpb, m_w_pool, m_pool_scale, v_c_ctx, v_w_mod, v_b_mod, v_norm_g, v_w_ffn_gate_up, v_w_ffn_down, v_w_in, v_w_out, v_na_rpb, v_w_pool, v_pool_scale):
    s, d = x.shape[1], x.shape[2]
    l = ctx.shape[1]
    n = s + l
    nx = s // TM
    fq = w_ffn_gate_up.shape[-1]
    fr = w_ffn_down.shape[2]
    cl = w_mod.shape[2]
    dl = norm_g.shape[2]
    me = 4 * lax.axis_index("x") + 2 * lax.axis_index("y") + lax.axis_index("c")

    c_all = all_gather(c, "gather_c").reshape(N_DEV, d)
    cvecs = jnp.concatenate([c_all, c_ctx[None, :], jnp.zeros((MOD_ROWS - N_DEV - 1, d), F32)], axis=0)
    b_loc = lax.dynamic_slice(b_mod, (0, me * cl), (DEPTH, cl)).reshape(DEPTH, 1, cl)
    mod_loc = mod_fwd(cvecs, w_mod, b_loc, "mod_fwd")
    mod_all = all_gather(mod_loc.reshape(DEPTH * MOD_ROWS, cl), "gather_mod")
    mod_all = mod_all.reshape(N_DEV, DEPTH, MOD_ROWS, cl).transpose(1, 2, 0, 3).reshape(DEPTH, MOD_ROWS, N_DEV * cl)
    mine = lax.dynamic_slice(mod_all, (0, me, 0), (DEPTH, 1, N_DEV * cl))
    mods = jnp.concatenate([mine, mod_all[:, N_DEV:N_DEV + 1]], axis=1).reshape(DEPTH, 2, N_MOD, d)

    gu_b = w_ffn_gate_up.astype(BF16)
    dn_b = w_ffn_down.astype(BF16)
    wi_b = w_in.astype(BF16)
    wo_b = w_out.astype(BF16)
    wp_b = w_pool.astype(BF16)

    def ffn_shards(li, i):
        return [gu_b[li, i], dn_b[li, i]]

    def mix_shards(li):
        return [wi_b[li], wo_b[li]]

    def as_ffn_weights(gathered):
        return gathered[0].reshape(2, 4, d, fq), gathered[1].reshape(4, 2 * fr, d)

    def as_mix_weights(gathered):
        return gathered[0], gathered[1].reshape(N_DEV * wo_b.shape[1], d)

    tables = _rope_tables(s, n)
    mem_np, mem_t_np, inv_np = _pool_consts(l)
    mem, mem_t, inv = jnp.asarray(mem_np, BF16), jnp.asarray(mem_t_np, BF16), jnp.asarray(inv_np)
    first = gather_two_level([norm_g.reshape(DEPTH * 6, dl), gu_b[0, 0]], "gather_first")
    g_full = first[0].reshape(N_DEV, DEPTH, 6, dl).transpose(1, 2, 0, 3).reshape(DEPTH, 6, 1, N_DEV * dl)

    weights = {("ffn", 0, 0): (first[1].reshape(2, 4, d, fq), None)}
    saved = {}
    xcur = jnp.concatenate([x[0], ctx[0]], axis=0)
    for li in range(DEPTH):
        last = li == DEPTH - 1
        for i in range(2):
            tag = f"l{li}_ffn{i}"
            wgu, wd4 = weights[("ffn", li, i)]
            if i == 0:
                ex_up, ex_dn = Exchange(gathers=mix_shards(li) + ([dn_b[0, 0]] if wd4 is None else [])), None
            elif not last:
                ex_up, ex_dn = Exchange(gathers=[gu_b[li + 1, 0]]), Exchange(gathers=[dn_b[li + 1, 0]])
            else:
                ex_up = ex_dn = None
            (hb, gu, a4), got_up = ffn_up(xcur, g_full[li, 4 * i], mods[li], wgu, s, 6 * i, tag + "_up", ex_up)
            if wd4 is None:
                wd4 = got_up.pop().reshape(4, 2 * fr, d)
                weights[("ffn", li, i)] = (wgu, wd4)
            (ff, xnext), got_dn = ffn_down(a4, wd4, xcur, g_full[li, 4 * i + 1], mods[li], s, 6 * i + 2, tag + "_down", ex_dn)
            saved[("ffn", li, i)] = (xcur, hb, gu, a4, ff)
            xcur = xnext
            if i == 0:
                weights[("mix", li)] = as_mix_weights(got_up)
            elif not last:
                weights[("ffn", li + 1, 0)] = as_ffn_weights(got_up + got_dn)
            if i == 0:
                tag = f"l{li}_mix"
                win8, wout = weights[("mix", li)]
                hb, qkv, u = mix_in(xcur, g_full[li, 2], mods[li], win8, tables, s, tag + "_in")
                tb = bias_tables(na_rpb[li], tag + "_bias")
                (na, lse), got = na_fwd(qkv, tb, s, tag + "_na", Exchange(gathers=ffn_shards(li, 1)))
                weights[("ffn", li, 1)] = as_ffn_weights(got)
                lse_c = None
                if not last:
                    na, lse_c = ctx_attn_fwd(qkv, na, s, tag + "_ctx_attn")
                py = pool_fwd(u, mem, inv, wp_b[li], pool_scale[li][None, :], nx, tag + "_pool")
                fm, xnext = mix_out(na, py, wout, xcur, g_full[li, 3], mods[li], s, tag + "_out")
                saved[("mix", li)] = (xcur, hb, qkv, u, tb, na, lse, lse_c, py, fm)
                xcur = xnext

    loss_local, dcur = loss_head(xcur, loss_target[0], "loss")
    loss = lax.psum(loss_local[0, 0], ("x", "y", "c"))

    recv = {"gu": lax.empty((N_DEV, 2 * DEPTH, d, fq), BF16), "dn": lax.empty((N_DEV, 2 * DEPTH, fr, d), BF16),
            "wi": lax.empty((N_DEV, DEPTH, d, IN_BLOCK), BF16), "wo": lax.empty((N_DEV, DEPTH, wo_b.shape[1], d), BF16)}
    pending = []

    def take(keys, gathers=()):
        nonlocal pending
        jobs = [(gr, recv[key], st) for key, gr, st in pending if key in keys]
        order = [key for key, _, _ in pending if key in keys]
        pending = [p for p in pending if p[0] not in keys]
        return Exchange(gathers=gathers, a2as=jobs), order

    def pad8(t):
        t = t.reshape(-1, d) if t.size % d == 0 else jnp.pad(t.reshape(-1), (0, -t.size % d)).reshape(-1, d)
        return jnp.pad(t, ((0, -t.shape[0] % 8), (0, 0)))

    def packed(parts):
        parts = [pad8(p) for p in parts]
        offs = np.cumsum([0] + [p.shape[0] for p in parts])
        return jnp.concatenate(parts + [jnp.zeros((-offs[-1] % 64, d), F32)], axis=0), offs

    def put(order, bufs):
        for key, buf in zip(order, bufs):
            recv[key] = buf

    d_rpb, d_wp, d_ps, d_mod, d_g = [], [], [], [], []
    for li in reversed(range(DEPTH)):
        reds = {}
        for i in (1, 0):
            tag = f"l{li}_ffn{i}"
            xin, hb, gu, a4, ff = saved[("ffn", li, i)]
            wgu, wd4 = weights[("ffn", li, i)]
            dff, red1 = post_bwd(ff, dcur, g_full[li, 4 * i + 1], mods[li], 6 * i + 2, 0.5, s, tag + "_post_bwd")
            early = []
            if (li, i) == (0, 0):
                early_small, early_offs = packed([jnp.stack(d_wp), jnp.stack(d_ps), jnp.stack(d_rpb)])
                early = [early_small]
            ex, _ = take((), early)
            g_dn, bufs = grad_weight(a4, dff, tag + "_dwdown", a_lead=4, exchange=ex)
            if early:
                early_sum = sum_devices(bufs[0], "sum_early_small_grads")
            pending += [("dn", g_dn.reshape(N_DEV, fr, d), 2 * li + i)]
            ex, order = take(("dn",))
            dgu, bufs = ffn_da(dff, wd4, gu, tag + "_da", ex)
            put(order, bufs)
            ex, order = take(("wi", "wo"))
            g_gu, bufs = grad_weight(hb, dgu.reshape(N_DEV, n, fq), tag + "_dwgu", b_lead=N_DEV, exchange=ex)
            put(order, bufs)
            pending += [("gu", g_gu, 2 * li + i)]
            ex, order = take(("gu",))
            (dcur, red2), bufs = ffn_dh(dgu, wgu, xin, dcur, g_full[li, 4 * i], mods[li], s, 6 * i, tag + "_dh", ex)
            put(order, bufs)
            reds[i] = (red1, red2)
            if i == 1:
                tag = f"l{li}_mix"
                xin, hb, qkv, u, tb, na, lse, lse_c, py, fm = saved[("mix", li)]
                win8, wout = weights[("mix", li)]
                dfm, redm1 = post_bwd(fm, dcur, g_full[li, 3], mods[li], 5, 1.0, s, tag + "_post_bwd")
                dmix = matmul_nt(dfm, wout, tag + "_dmix")
                g_wo = grad_wout(na, py, dfm, tag + "_dwout").reshape(N_DEV, wo_b.shape[1], d)
                du, gwp, gps = pool_bwd(dmix, u, mem, mem_t, inv, wp_b[li], pool_scale[li][None, :], nx, tag + "_pool_bwd")
                ex, order = take(("gu", "dn"))
                (dq, dk, dv, dtb), bufs = na_bwd(qkv, tb, na, dmix, lse, s, tag + "_na_bwd", ex)
                put(order, bufs)
                if li != DEPTH - 1:
                    dq, dk, dv = ctx_attn_bwd(qkv, na, dmix, lse_c, dq, dk, dv, s, tag + "_ctx_attn_bwd")
                grpb = bias_tables_bwd(dtb, tag + "_bias_bwd")
                dqkvu = qkv_bwd(dq, dk, dv, du, tables, tag + "_rope_bwd")
                (dcur, redm2), _ = mix_dh(dqkvu, win8, xin, dcur, g_full[li, 2], mods[li], s, tag + "_dh")
                g_wi, _ = grad_weight(hb, dqkvu, tag + "_dwin", b_cols=IN_BLOCK)
                pending += [("wi", g_wi, li), ("wo", g_wo, li)]
                d_rpb.insert(0, grpb)
                d_wp.insert(0, gwp)
                d_ps.insert(0, gps[0])
        (ra1, ra2), (rb1, rb2) = reds[0], reds[1]
        d_mod.insert(0, jnp.stack([ra2[:, 0], ra2[:, 1], ra1[:, 0], redm2[:, 0], redm2[:, 1], redm1[:, 0],
                                   rb2[:, 0], rb2[:, 1], rb1[:, 0]], axis=1))
        d_g.insert(0, jnp.stack([t[0] + t[1] for t in (ra2[:, 2], ra1[:, 1], redm2[:, 2], redm1[:, 1], rb2[:, 2], rb1[:, 1])]))
    grad_x = dcur[:s][None]

    small, offs = packed([jnp.stack(d_mod), jnp.stack(d_g)])
    ex, order = take(("gu", "dn", "wi", "wo"), [small])
    bufs = exchange_only(ex, "exchange_last")
    small_all = bufs[0]
    put(order, bufs[1:])
    small_sum = sum_devices(small_all, "sum_small_grads")

    n_mod_rows = DEPTH * 2 * N_MOD
    dmod_all = small_all[:, :n_mod_rows].reshape(N_DEV, DEPTH, 2, N_MOD * d)
    dmod_sum = small_sum[:n_mod_rows].reshape(DEPTH, 2, N_MOD * d)
    dm_rows = jnp.concatenate([dmod_all[:, :, 0].transpose(1, 0, 2), dmod_sum[:, 1:2],
                               jnp.zeros((DEPTH, MOD_ROWS - N_DEV - 1, N_MOD * d), F32)], axis=1)
    grad_b_mod = dmod_sum[:, 0] + dmod_sum[:, 1]
    dm_loc = lax.dynamic_slice(dm_rows, (0, 0, me * cl), (DEPTH, MOD_ROWS, cl))
    grad_w_mod, da_part = mod_bwd(cvecs, dm_loc, w_mod, "mod_bwd")
    da_all = all_gather(da_part, "gather_dcvec")
    grad_c_ctx = cctx_grad(da_all, c_ctx[None, :], "c_ctx_grad")[0]

    grad_norm_full = small_sum[offs[1]:offs[1] + DEPTH * 6].reshape(DEPTH, 6, d)
    grad_norm_g = lax.dynamic_slice(grad_norm_full, (0, 0, me * dl), (DEPTH, 6, dl))
    grad_w_pool = early_sum[early_offs[0]:early_offs[0] + w_pool.size // d].reshape(w_pool.shape)
    grad_pool_scale = early_sum[early_offs[1]:early_offs[1] + pool_scale.size // d].reshape(pool_scale.shape)
    grad_na_rpb = early_sum[early_offs[2]:early_offs[3]].reshape(-1)[:na_rpb.size].reshape(na_rpb.shape)

    def big_adam(key, w, m, v, name):
        shp = w.shape
        cols = shp[-1]
        outs = reduce_adamw(recv[key].reshape(N_DEV, -1, cols), w.reshape(-1, cols), m.reshape(-1, cols),
                            v.reshape(-1, cols), name)
        return tuple(t.reshape(shp) for t in outs)

    def small_adam(w, g, m, v, name):
        shp = w.shape
        cols = shp[-1]
        outs = adamw(w.reshape(-1, cols), g.reshape(-1, cols), m.reshape(-1, cols), v.reshape(-1, cols), name)
        return tuple(t.reshape(shp) for t in outs)

    b_gu = big_adam("gu", w_ffn_gate_up, m_w_ffn_gate_up, v_w_ffn_gate_up, "adam_gate_up")
    b_dn = big_adam("dn", w_ffn_down, m_w_ffn_down, v_w_ffn_down, "adam_down")
    b_wi = big_adam("wi", w_in, m_w_in, v_w_in, "adam_w_in")
    b_wo = big_adam("wo", w_out, m_w_out, v_w_out, "adam_w_out")
    a_cc = small_adam(c_ctx, grad_c_ctx, m_c_ctx, v_c_ctx, "adam_c_ctx")
    a_wm = small_adam(w_mod, grad_w_mod, m_w_mod, v_w_mod, "adam_w_mod")
    a_bm = small_adam(b_mod, grad_b_mod, m_b_mod, v_b_mod, "adam_b_mod")
    a_ng = small_adam(norm_g, grad_norm_g, m_norm_g, v_norm_g, "adam_norm_g")
    a_rp = small_adam(na_rpb, grad_na_rpb, m_na_rpb, v_na_rpb, "adam_na_rpb")
    a_wp = small_adam(w_pool, grad_w_pool, m_w_pool, v_w_pool, "adam_w_pool")
    a_ps = small_adam(pool_scale, grad_pool_scale, m_pool_scale, v_pool_scale, "adam_pool_scale")

    grads = (grad_c_ctx, grad_w_mod, grad_b_mod, grad_norm_g, b_gu[0], b_dn[0], b_wi[0], b_wo[0], grad_na_rpb, grad_w_pool, grad_pool_scale)
    deltas = (a_cc[0], a_wm[0], a_bm[0], a_ng[0], b_gu[1], b_dn[1], b_wi[1], b_wo[1], a_rp[0], a_wp[0], a_ps[0])
    new_m = (a_cc[1], a_wm[1], a_bm[1], a_ng[1], b_gu[2], b_dn[2], b_wi[2], b_wo[2], a_rp[1], a_wp[1], a_ps[1])
    new_v = (a_cc[2], a_wm[2], a_bm[2], a_ng[2], b_gu[3], b_dn[3], b_wi[3], b_wo[3], a_rp[2], a_wp[2], a_ps[2])
    return (loss, grad_x, *grads, *deltas, *new_m, *new_v)
```

```python
import functools
import math

import numpy as np
import jax
import jax.numpy as jnp
from jax import lax
from jax.experimental import pallas as pl
from jax.experimental.pallas import tpu as pltpu

F32 = jnp.float32
BF16 = jnp.bfloat16

N_DEV = 8
DEPTH = 2
GRID_W = 64
N_MOD = 9
NA_HEADS = 8
HEAD_DIM = 64
NA_WIDTH = NA_HEADS * HEAD_DIM
NA_KH = 8
NA_KW = 16
POOL_GROUPS = 4
POOL_CH = 128
POOL_WIDTH = POOL_GROUPS * POOL_CH
POOL_WINDOWS = (2, 4, 8, 16)
ROPE_THETA = 10000.0
ROPE_PAIRS = HEAD_DIM // 4
RMS_EPS = 1e-6
NEG_INF = -1e30
ATT_SCALE = HEAD_DIM ** -0.5

ADAM_LR = 0.001
ADAM_B1 = 0.9
ADAM_B2 = 0.999
ADAM_EPS = 1e-08
ADAM_WD = 0.01
ADAM_STEP = 10

TM = 256
LANES = 128
HEADS_PER_BLOCK = LANES // HEAD_DIM
N_HEAD_BLOCKS = NA_WIDTH // LANES
IN_BLOCK = 2 * LANES
N_QKV_BLOCKS = 3 * NA_WIDTH // IN_BLOCK
VMEM_LIMIT = 56 * 1024 * 1024
HIGHEST = lax.Precision.HIGHEST
MESH = pl.DeviceIdType.MESH
ANY = pl.BlockSpec(memory_space=pl.ANY)

NN = (((1,), (0,)), ((), ()))
NT = (((1,), (1,)), ((), ()))
TN = (((0,), (0,)), ((), ()))


def _pick(n, cands):
    for t in cands:
        if n % t == 0:
            return t
    raise ValueError(f"no tile for {n} among {cands}")


def _dot(a, b, dn=NN, precision=None):
    return lax.dot_general(a, b, dn, preferred_element_type=F32, precision=precision)


def _silu(x):
    return x * jax.nn.sigmoid(x)


def _dsilu(x):
    s = jax.nn.sigmoid(x)
    return s * (1.0 + x * (1.0 - s))


def _peer(mask):
    x, y, c = lax.axis_index("x"), lax.axis_index("y"), lax.axis_index("c")
    px = 1 - x if mask & 4 else x
    py = 1 - y if mask & 2 else y
    pc = 1 - c if mask & 1 else c
    return (px, py, pc), 4 * px + 2 * py + pc


class Exchange:
    def __init__(self, gathers=(), a2as=()):
        self.gathers = list(gathers)
        self.a2as = list(a2as)
        self.n_jobs = len(self.gathers) + len(self.a2as)

    def inputs(self):
        out = list(self.gathers)
        for v, buf, _ in self.a2as:
            out += [v, buf]
        return out

    def out_shapes(self):
        shapes = [jax.ShapeDtypeStruct((N_DEV,) + v.shape, v.dtype) for v in self.gathers]
        shapes += [jax.ShapeDtypeStruct(buf.shape, buf.dtype) for _, buf, _ in self.a2as]
        return shapes

    def aliases(self, n_in, n_out):
        ng = len(self.gathers)
        return {n_in + ng + 2 * k + 1: n_out + ng + k for k in range(len(self.a2as))}

    def scratch(self):
        per = N_DEV - 1
        return [pltpu.SemaphoreType.DMA((per * self.n_jobs,)), pltpu.SemaphoreType.DMA((per * self.n_jobs,)),
                pltpu.SemaphoreType.DMA((self.n_jobs,))]

    def _copies(self, in_refs, out_refs, sems, with_recvs):
        send_sems, recv_sems, local_sems = sems
        _, me = _peer(0)
        ng = len(self.gathers)
        local, sends, recvs = [], [], []
        for job in range(self.n_jobs):
            if job < ng:
                src_of = lambda pid, r=in_refs[job]: r
                dst_of = lambda pid, r=out_refs[job]: r.at[pid]
            else:
                k = job - ng
                stage = self.a2as[k][2]
                src_of = lambda pid, r=in_refs[ng + 2 * k]: r.at[pid]
                dst_of = lambda pid, r=out_refs[job], st=stage: r.at[pid, st]
            local.append(pltpu.make_async_copy(src_of(me), dst_of(me), local_sems.at[job]))
            for mask in range(1, N_DEV):
                peer, pid = _peer(mask)
                idx = job * (N_DEV - 1) + mask - 1
                sends.append(pltpu.make_async_remote_copy(
                    src_ref=src_of(pid), dst_ref=dst_of(me), send_sem=send_sems.at[idx],
                    recv_sem=recv_sems.at[idx], device_id=peer, device_id_type=MESH))
                if with_recvs:
                    recvs.append(pltpu.make_async_remote_copy(
                        src_ref=src_of(pid), dst_ref=dst_of(pid), send_sem=send_sems.at[idx],
                        recv_sem=recv_sems.at[idx], device_id=peer, device_id_type=MESH))
        return local, sends, recvs

    def start(self, in_refs, out_refs, sems):
        local, sends, _ = self._copies(in_refs, out_refs, sems, False)
        for cp in local + sends:
            cp.start()

    def wait(self, in_refs, out_refs, sems):
        local, sends, recvs = self._copies(in_refs, out_refs, sems, True)
        for cp in recvs:
            cp.wait_recv()
        for cp in sends:
            cp.wait_send()
        for cp in local:
            cp.wait()


def carrier_call(core, *, name, grid, in_specs, out_specs, out_shape, inputs, scratch_shapes=(), aliases=None,
                 exchange=None):
    aliases = dict(aliases or {})
    n_in, n_out, n_sc = len(in_specs), len(out_specs), len(scratch_shapes)
    sem = ("arbitrary",) * len(grid)
    params = pltpu.CompilerParams(dimension_semantics=sem, vmem_limit_bytes=VMEM_LIMIT)
    if exchange is None or exchange.n_jobs == 0:
        outs = pl.pallas_call(core, name=name, grid=grid, in_specs=list(in_specs), out_specs=tuple(out_specs),
                              out_shape=tuple(out_shape), scratch_shapes=list(scratch_shapes),
                              input_output_aliases=aliases, compiler_params=params)(*inputs)
        return list(outs), []
    x_in = exchange.inputs()
    x_out = exchange.out_shapes()
    aliases.update(exchange.aliases(n_in, n_out))

    def body(*refs):
        a = n_in + len(x_in)
        b = a + n_out + len(x_out)
        core_in, job_in = refs[:n_in], refs[n_in:a]
        core_out, job_out = refs[a:a + n_out], refs[a + n_out:b]
        core_sc, job_sc = refs[b:b + n_sc], refs[b + n_sc:]
        first = functools.reduce(lambda p, q: p & q, [pl.program_id(ax) == 0 for ax in range(len(grid))])
        last = functools.reduce(lambda p, q: p & q, [pl.program_id(ax) == g - 1 for ax, g in enumerate(grid)])

        @pl.when(first)
        def _():
            exchange.start(job_in, job_out, job_sc)

        core(*core_in, *core_out, *core_sc)

        @pl.when(last)
        def _():
            exchange.wait(job_in, job_out, job_sc)

    outs = pl.pallas_call(
        body, name=name, grid=grid, in_specs=list(in_specs) + [ANY] * len(x_in),
        out_specs=tuple(out_specs) + (ANY,) * len(x_out), out_shape=tuple(out_shape) + tuple(x_out),
        scratch_shapes=list(scratch_shapes) + exchange.scratch(), input_output_aliases=aliases,
        compiler_params=params)(*inputs, *x_in)
    return list(outs[:n_out]), list(outs[n_out:])


def exchange_only(exchange, name):
    def body(*refs):
        n_in, n_out = len(exchange.inputs()), len(exchange.out_shapes())
        job_in, job_out, sems = refs[:n_in], refs[n_in:n_in + n_out], refs[n_in + n_out:]
        exchange.start(job_in, job_out, sems)
        exchange.wait(job_in, job_out, sems)

    x_in = exchange.inputs()
    outs = pl.pallas_call(
        body, name=name, in_specs=[ANY] * len(x_in), out_specs=(ANY,) * len(exchange.out_shapes()),
        out_shape=tuple(exchange.out_shapes()), scratch_shapes=exchange.scratch(),
        input_output_aliases=exchange.aliases(0, 0))(*x_in)
    return list(outs)


def all_gather(v, name):
    return exchange_only(Exchange(gathers=[v]), name)[0]


def gather_two_level(vs, name):
    nv = len(vs)
    per = N_DEV - 1

    def body(*refs):
        v_refs, o_refs = refs[:nv], refs[nv:2 * nv]
        send_sems, recv_sems, local_sems = refs[2 * nv:]
        x, y, c = lax.axis_index("x"), lax.axis_index("y"), lax.axis_index("c")
        me, sibling = (x, y, c), (x, y, 1 - c)
        chips = [(1 - x, y), (x, 1 - y), (1 - x, 1 - y)]

        def copy(a, k, block, to, src=None):
            dst = o_refs[a].at[4 * block[0] + 2 * block[1] + block[2]]
            return pltpu.make_async_remote_copy(
                src_ref=dst if src is None else src, dst_ref=dst, send_sem=send_sems.at[a * per + k],
                recv_sem=recv_sems.at[a * per + k], device_id=to, device_id_type=MESH)

        mine = [pltpu.make_async_copy(v_refs[a], o_refs[a].at[4 * x + 2 * y + c], local_sems.at[a]) for a in range(nv)]
        first = []
        for a in range(nv):
            first.append(copy(a, 0, me, sibling, src=v_refs[a]))
            first += [copy(a, 1 + j, me, (*chip, c), src=v_refs[a]) for j, chip in enumerate(chips)]
        for cp in mine + first:
            cp.start()
        passed = []
        for a in range(nv):
            for j, chip in enumerate(chips):
                copy(a, 1 + j, (*chip, c), me).wait_recv()
                passed.append(copy(a, 4 + j, (*chip, c), sibling))
                passed[-1].start()
        for a in range(nv):
            copy(a, 0, sibling, me).wait_recv()
            for j, chip in enumerate(chips):
                copy(a, 4 + j, (*chip, 1 - c), me).wait_recv()
        for cp in first + passed:
            cp.wait_send()
        for cp in mine:
            cp.wait()

    outs = pl.pallas_call(
        body, name=name, in_specs=[ANY] * nv, out_specs=(ANY,) * nv,
        out_shape=tuple(jax.ShapeDtypeStruct((N_DEV,) + v.shape, v.dtype) for v in vs),
        scratch_shapes=[pltpu.SemaphoreType.DMA((per * nv,)), pltpu.SemaphoreType.DMA((per * nv,)),
                        pltpu.SemaphoreType.DMA((nv,))])(*vs)
    return list(outs)


def _rms(xf):
    return lax.rsqrt(jnp.mean(xf * xf, axis=-1, keepdims=True) + RMS_EPS)


def _is_ctx(i, tm, s):
    return (i * tm + lax.broadcasted_iota(jnp.int32, (tm, 1), 0)) >= s


def _by_tile_kind(i, tm, s, fn):
    n_latent = s // tm

    @pl.when(i < n_latent)
    def _():
        fn(None)

    @pl.when(i >= n_latent)
    def _():
        fn(_is_ctx(i, tm, s))


def _mod_rows(mod_ref, k, is_ctx):
    if is_ctx is None:
        return mod_ref[0, k:k + 1, :]
    return jnp.where(is_ctx, mod_ref[1, k:k + 1, :], mod_ref[0, k:k + 1, :])


def _norm_mod(xf, g, mod_ref, k_shift, k_scale, is_ctx):
    nrm = xf * _rms(xf) * g
    return (nrm * (1.0 + _mod_rows(mod_ref, k_scale, is_ctx)) + _mod_rows(mod_ref, k_shift, is_ctx)).astype(BF16)


def _post(xf, ff, g, mod_ref, k_gate, coef, is_ctx):
    return xf + coef * _mod_rows(mod_ref, k_gate, is_ctx) * (ff * _rms(ff) * g)


def _red_add(red_ref, first, is_ctx, rows):
    @pl.when(first)
    def _():
        red_ref[...] = jnp.zeros_like(red_ref)

    for r, val in enumerate(rows):
        tot = jnp.sum(val, axis=0, keepdims=True)
        if is_ctx is None:
            red_ref[0, r:r + 1, :] += tot
        else:
            ctx = jnp.sum(jnp.where(is_ctx, val, 0.0), axis=0, keepdims=True)
            red_ref[0, r:r + 1, :] += tot - ctx
            red_ref[1, r:r + 1, :] += ctx


def _pre_bwd(xf, dh, dxo, g, mod_ref, k_scale, is_ctx):
    r = _rms(xf)
    xhat = xf * r
    dn = dh * (1.0 + _mod_rows(mod_ref, k_scale, is_ctx))
    dxhat = dn * g
    dx = dxo + r * (dxhat - xhat * jnp.mean(dxhat * xhat, axis=-1, keepdims=True))
    return dx, [dh, dh * (xhat * g), dn * xhat]


def _vec_spec(d):
    return pl.BlockSpec((1, d), lambda *_: (0, 0))


def _mod_whole(d):
    return pl.BlockSpec((2, N_MOD, d), lambda *_: (0, 0, 0))


def _red_whole(d):
    return pl.BlockSpec((2, 8, d), lambda *_: (0, 0, 0))


def _token_tile(n):
    return _pick(n, (768, 640, 512, 384, 256))


def _ffn_tile(n):
    return _pick(n, (528, 384, 640, 256))


def _resident(shape):
    zeros = (0,) * len(shape)
    return pl.BlockSpec(shape, lambda i: zeros, pipeline_mode=pl.Buffered(1))


def ffn_up(x, g, mod, wgu, s, k0, name, exchange=None):
    n, d = x.shape
    nk, fq = wgu.shape[1], wgu.shape[-1]
    tm = _ffn_tile(n)

    def core(x_ref, g_ref, mod_ref, w_ref, hb_ref, gu_ref, a_ref):
        i = pl.program_id(0)

        def prologue(is_ctx):
            hb_ref[...] = _norm_mod(x_ref[...], g_ref[...], mod_ref, k0, k0 + 1, is_ctx)

        _by_tile_kind(i, tm, s, prologue)
        h = hb_ref[...]
        for k in range(nk):
            gg = _dot(h, w_ref[0, k])
            uu = _dot(h, w_ref[1, k])
            gu_ref[0, k] = gg.astype(BF16)
            gu_ref[1, k] = uu.astype(BF16)
            a_ref[k] = (_silu(gg) * uu).astype(BF16)

    outs, xo = carrier_call(
        core, name=name, grid=(n // tm,),
        in_specs=[pl.BlockSpec((tm, d), lambda i: (i, 0)), _vec_spec(d), _mod_whole(d), _resident(wgu.shape)],
        out_specs=[pl.BlockSpec((tm, d), lambda i: (i, 0)),
                   pl.BlockSpec((2, nk, tm, fq), lambda i: (0, 0, i, 0)),
                   pl.BlockSpec((nk, tm, fq), lambda i: (0, i, 0))],
        out_shape=[jax.ShapeDtypeStruct((n, d), BF16), jax.ShapeDtypeStruct((2, nk, n, fq), BF16),
                   jax.ShapeDtypeStruct((nk, n, fq), BF16)],
        inputs=[x, g, mod, wgu], exchange=exchange)
    return outs, xo


def ffn_down(a4, wd4, x, g, mod, s, k_gate, name, exchange=None):
    n, d = x.shape
    nk, fq = wd4.shape[0], wd4.shape[1]
    tm = _ffn_tile(n)

    def core(a_ref, w_ref, x_ref, g_ref, mod_ref, f_ref, xo_ref):
        i = pl.program_id(0)
        ff = _dot(a_ref[0], w_ref[0])
        for k in range(1, nk):
            ff = ff + _dot(a_ref[k], w_ref[k])
        f_ref[...] = ff

        def epilogue(is_ctx):
            xo_ref[...] = _post(x_ref[...], f_ref[...], g_ref[...], mod_ref, k_gate, 0.5, is_ctx)

        _by_tile_kind(i, tm, s, epilogue)

    tile = pl.BlockSpec((tm, d), lambda i: (i, 0))
    outs, xo = carrier_call(
        core, name=name, grid=(n // tm,),
        in_specs=[pl.BlockSpec((nk, tm, fq), lambda i: (0, i, 0)), _resident(wd4.shape), tile, _vec_spec(d),
                  _mod_whole(d)],
        out_specs=[tile, tile],
        out_shape=[jax.ShapeDtypeStruct((n, d), F32), jax.ShapeDtypeStruct((n, d), F32)],
        inputs=[a4, wd4, x, g, mod], exchange=exchange)
    return outs, xo


def ffn_da(df, wd4, gu, name, exchange=None):
    n, d = df.shape
    nk, fq = wd4.shape[0], wd4.shape[1]
    tm = _ffn_tile(n)

    def core(df_ref, w_ref, gu_ref, o_ref):
        dfv = df_ref[...]
        for k in range(nk):
            da = _dot(dfv, w_ref[k], NT).astype(BF16)
            gg = gu_ref[0, k]
            uu = gu_ref[1, k]
            sg = jax.nn.sigmoid(gg.astype(F32)).astype(BF16)
            o_ref[0, k] = da * (uu * (sg * (1 + gg * (1 - sg))))
            o_ref[1, k] = da * (gg * sg)

    gu_spec = pl.BlockSpec((2, nk, tm, fq), lambda i: (0, 0, i, 0))
    outs, xo = carrier_call(
        core, name=name, grid=(n // tm,),
        in_specs=[pl.BlockSpec((tm, d), lambda i: (i, 0)), _resident(wd4.shape), gu_spec],
        out_specs=[gu_spec], out_shape=[jax.ShapeDtypeStruct(gu.shape, BF16)],
        inputs=[df, wd4, gu], exchange=exchange)
    return outs[0], xo


def ffn_dh(dgu, wgu, x, dxo, g, mod, s, k0, name, exchange=None):
    n, d = x.shape
    nk, fq = wgu.shape[1], wgu.shape[-1]
    tm = _ffn_tile(n)

    def core(dgu_ref, w_ref, x_ref, dxo_ref, g_ref, mod_ref, dx_ref, red_ref, dh_s):
        i = pl.program_id(0)
        dh = _dot(dgu_ref[0, 0], w_ref[0, 0], NT) + _dot(dgu_ref[1, 0], w_ref[1, 0], NT)
        for k in range(1, nk):
            dh = dh + _dot(dgu_ref[0, k], w_ref[0, k], NT) + _dot(dgu_ref[1, k], w_ref[1, k], NT)
        dh_s[...] = dh

        def epilogue(is_ctx):
            dx, sums = _pre_bwd(x_ref[...], dh_s[...], dxo_ref[...], g_ref[...], mod_ref, k0 + 1, is_ctx)
            dx_ref[...] = dx
            _red_add(red_ref, i == 0, is_ctx, sums)

        _by_tile_kind(i, tm, s, epilogue)

    tile = pl.BlockSpec((tm, d), lambda i: (i, 0))
    outs, xo = carrier_call(
        core, name=name, grid=(n // tm,),
        in_specs=[pl.BlockSpec((2, nk, tm, fq), lambda i: (0, 0, i, 0)), _resident(wgu.shape), tile, tile,
                  _vec_spec(d), _mod_whole(d)],
        out_specs=[tile, _red_whole(d)],
        out_shape=[jax.ShapeDtypeStruct((n, d), F32), jax.ShapeDtypeStruct((2, 8, d), F32)],
        scratch_shapes=[pltpu.VMEM((tm, d), F32)], inputs=[dgu, wgu, x, dxo, g, mod], exchange=exchange)
    return outs, xo


def grad_weight(a, b, name, a_lead=None, b_lead=None, b_cols=None, per_step=1, exchange=None):
    n = a.shape[-2]
    ka = a.shape[-1]
    kb = b_cols or b.shape[-1]
    nj = a_lead or b_lead or (b.shape[-1] // b_cols)
    tk = _pick(n, (1408, 1024, 768, 640, 512, 256))
    nk = n // tk
    assert nj % per_step == 0 and (per_step == 1 or b_cols)

    def core(a_ref, b_ref, o_ref, acc):
        kk = pl.program_id(1)

        @pl.when(kk == 0)
        def _():
            acc[...] = jnp.zeros_like(acc)

        av = a_ref[0] if a_lead else a_ref[...]
        bv = b_ref[0] if b_lead else b_ref[...]
        acc[...] += _dot(av, bv, TN)

        @pl.when(kk == nk - 1)
        def _():
            for t in range(per_step):
                o_ref[t] = acc[:, t * kb:(t + 1) * kb].astype(BF16)

    a_spec = (pl.BlockSpec((1, tk, ka), lambda j, kk: (j, kk, 0)) if a_lead
              else pl.BlockSpec((tk, ka), lambda j, kk: (kk, 0)))
    if b_lead:
        b_spec = pl.BlockSpec((1, tk, kb), lambda j, kk: (j, kk, 0))
    elif b_cols:
        b_spec = pl.BlockSpec((tk, per_step * kb), lambda j, kk: (kk, j))
    else:
        b_spec = pl.BlockSpec((tk, kb), lambda j, kk: (kk, 0))
    outs, xo = carrier_call(
        core, name=name, grid=(nj // per_step, nk), in_specs=[a_spec, b_spec],
        out_specs=[pl.BlockSpec((per_step, ka, kb), lambda j, kk: (j, 0, 0))],
        out_shape=[jax.ShapeDtypeStruct((nj, ka, kb), BF16)],
        scratch_shapes=[pltpu.VMEM((ka, per_step * kb), F32)], inputs=[a, b], exchange=exchange)
    return outs[0], xo


def post_bwd(f, dxo, g, mod, k_gate, coef, s, name):
    n, d = f.shape

    def core(f_ref, dxo_ref, g_ref, mod_ref, df_ref, red_ref):
        i = pl.program_id(0)

        def body(is_ctx):
            ff = f_ref[...]
            dxo_ = dxo_ref[...]
            gg = g_ref[...]
            r = _rms(ff)
            fn = ff * r
            dy = (coef * _mod_rows(mod_ref, k_gate, is_ctx)) * dxo_
            dfn = dy * gg
            df_ref[...] = (r * (dfn - fn * jnp.mean(dfn * fn, axis=-1, keepdims=True))).astype(BF16)
            _red_add(red_ref, i == 0, is_ctx, [coef * (fn * gg) * dxo_, dy * fn])

        _by_tile_kind(i, TM, s, body)

    tile = pl.BlockSpec((TM, d), lambda i: (i, 0))
    outs, _ = carrier_call(
        core, name=name, grid=(n // TM,), in_specs=[tile, tile, _vec_spec(d), _mod_whole(d)],
        out_specs=[tile, _red_whole(d)],
        out_shape=[jax.ShapeDtypeStruct((n, d), BF16), jax.ShapeDtypeStruct((2, 8, d), F32)],
        inputs=[f, dxo, g, mod])
    return outs


def matmul_nt(a, b, name):
    m, k = a.shape
    n = b.shape[0]
    tm = _token_tile(m)

    def core(a_ref, b_ref, o_ref):
        o_ref[...] = _dot(a_ref[...], b_ref[...], NT)

    outs, _ = carrier_call(
        core, name=name, grid=(m // tm,),
        in_specs=[pl.BlockSpec((tm, k), lambda i: (i, 0)), pl.BlockSpec((n, k), lambda i: (0, 0))],
        out_specs=[pl.BlockSpec((tm, n), lambda i: (i, 0))], out_shape=[jax.ShapeDtypeStruct((m, n), F32)],
        inputs=[a, b])
    return outs[0]


def _rope_tables(s, n):
    t = jnp.arange(n)
    lane = jnp.arange(LANES)
    dd = lane % HEAD_DIM
    inv = ROPE_THETA ** (-(dd % ROPE_PAIRS).astype(F32) / ROPE_PAIRS)
    pos = jnp.where(dd[None, :] < HEAD_DIM // 2, (t // GRID_W)[:, None], (t % GRID_W)[:, None]).astype(F32)
    ang = pos * inv[None, :]
    live = (t < s)[:, None]
    first = ((dd % (2 * ROPE_PAIRS)) < ROPE_PAIRS)[None, :]
    cos = jnp.where(live, jnp.cos(ang), 1.0)
    sin = jnp.where(live, jnp.sin(ang), 0.0)
    sa = jnp.where(first, -sin, 0.0)
    sb = jnp.where(first, 0.0, sin)
    return cos.astype(F32), sa.astype(F32), sb.astype(F32)


def _rope(xv, cos, sa, sb):
    return (xv * cos + pltpu.roll(xv, LANES - ROPE_PAIRS, 1) * sa + pltpu.roll(xv, ROPE_PAIRS, 1) * sb)


def mix_in(x, g, mod, win8, tables, s, name, exchange=None):
    n, d = x.shape
    tm = _token_tile(n)
    nb = win8.shape[0]
    n_rope = 2 * NA_WIDTH // IN_BLOCK

    def core(x_ref, g_ref, mod_ref, w_ref, c_ref, sa_ref, sb_ref, hb_ref, qkv_ref, u_ref):
        i = pl.program_id(0)
        hb = _norm_mod(x_ref[...], g_ref[...], mod_ref, 3, 4, _is_ctx(i, tm, s))
        hb_ref[...] = hb
        cos, sa, sb = c_ref[...], sa_ref[...], sb_ref[...]
        for j in range(nb):
            y = _dot(hb, w_ref[j])
            if j < n_rope:
                for b in range(IN_BLOCK // LANES):
                    sl = slice(b * LANES, (b + 1) * LANES)
                    qkv_ref[:, j * IN_BLOCK + b * LANES:j * IN_BLOCK + (b + 1) * LANES] = (
                        _rope(y[:, sl], cos, sa, sb).astype(BF16))
            elif j < N_QKV_BLOCKS:
                qkv_ref[:, j * IN_BLOCK:(j + 1) * IN_BLOCK] = y.astype(BF16)
            else:
                u_ref[:, (j - N_QKV_BLOCKS) * IN_BLOCK:(j - N_QKV_BLOCKS + 1) * IN_BLOCK] = y

    tab = pl.BlockSpec((tm, LANES), lambda i: (i, 0))
    row = lambda w: pl.BlockSpec((tm, w), lambda i: (i, 0))
    outs, xo = carrier_call(
        core, name=name, grid=(n // tm,),
        in_specs=[row(d), _vec_spec(d), _mod_whole(d), pl.BlockSpec((nb, d, IN_BLOCK), lambda i: (0, 0, 0)),
                  tab, tab, tab],
        out_specs=[row(d), row(3 * NA_WIDTH), row(POOL_WIDTH)],
        out_shape=[jax.ShapeDtypeStruct((n, d), BF16), jax.ShapeDtypeStruct((n, 3 * NA_WIDTH), BF16),
                   jax.ShapeDtypeStruct((n, POOL_WIDTH), F32)],
        inputs=[x, g, mod, win8, *tables], exchange=exchange)
    return outs, xo


def qkv_bwd(dq, dk, dv, du, tables, name):
    n = dq.shape[0]
    w = NA_WIDTH

    def core(dq_ref, dk_ref, dv_ref, du_ref, c_ref, sa_ref, sb_ref, o_ref):
        cos, sa, sb = c_ref[...], -sa_ref[...], -sb_ref[...]
        for b in range(N_HEAD_BLOCKS):
            sl = slice(b * LANES, (b + 1) * LANES)
            o_ref[:, b * LANES:(b + 1) * LANES] = _rope(dq_ref[:, sl], cos, sa, sb).astype(BF16)
            o_ref[:, w + b * LANES:w + (b + 1) * LANES] = _rope(dk_ref[:, sl], cos, sa, sb).astype(BF16)
        o_ref[:, 2 * w:3 * w] = dv_ref[...].astype(BF16)
        o_ref[:, 3 * w:] = du_ref[...].astype(BF16)

    tab = pl.BlockSpec((TM, LANES), lambda i: (i, 0))
    tile = pl.BlockSpec((TM, w), lambda i: (i, 0))
    outs, _ = carrier_call(
        core, name=name, grid=(n // TM,), in_specs=[tile, tile, tile, tile, tab, tab, tab],
        out_specs=[pl.BlockSpec((TM, 4 * w), lambda i: (i, 0))],
        out_shape=[jax.ShapeDtypeStruct((n, 4 * w), BF16)], inputs=[dq, dk, dv, du, *tables])
    return outs[0]


def mix_out(na, py, wout, x, g, mod, s, name):
    n, d = x.shape
    tm = _token_tile(n)
    half = na.shape[1]

    def core(na_ref, py_ref, w_ref, x_ref, g_ref, mod_ref, f_ref, xo_ref):
        i = pl.program_id(0)
        ff = _dot(na_ref[...], w_ref[:half, :]) + _dot(py_ref[...], w_ref[half:, :])
        f_ref[...] = ff
        xo_ref[...] = _post(x_ref[...], ff, g_ref[...], mod_ref, 5, 1.0, _is_ctx(i, tm, s))

    tile = pl.BlockSpec((tm, d), lambda i: (i, 0))
    htile = pl.BlockSpec((tm, half), lambda i: (i, 0))
    outs, _ = carrier_call(
        core, name=name, grid=(n // tm,),
        in_specs=[htile, htile, pl.BlockSpec((2 * half, d), lambda i: (0, 0)), tile, _vec_spec(d), _mod_whole(d)],
        out_specs=[tile, tile],
        out_shape=[jax.ShapeDtypeStruct((n, d), F32), jax.ShapeDtypeStruct((n, d), F32)],
        inputs=[na, py, wout, x, g, mod])
    return outs


def grad_wout(na, py, dfm, name):
    n, half = na.shape
    d = dfm.shape[1]
    tk = _pick(n, (1408, 1024, 768, 640, 512, 256))
    nk = n // tk

    def core(na_ref, py_ref, b_ref, o_ref, acc):
        hh, kk = pl.program_id(0), pl.program_id(1)

        @pl.when(kk == 0)
        def _():
            acc[...] = jnp.zeros_like(acc)

        @pl.when(hh == 0)
        def _():
            acc[...] += _dot(na_ref[...], b_ref[...], TN)

        @pl.when(hh == 1)
        def _():
            acc[...] += _dot(py_ref[...], b_ref[...], TN)

        @pl.when(kk == nk - 1)
        def _():
            o_ref[0] = acc[...].astype(BF16)

    htile = pl.BlockSpec((tk, half), lambda hh, kk: (kk, 0))
    outs, _ = carrier_call(
        core, name=name, grid=(2, nk), in_specs=[htile, htile, pl.BlockSpec((tk, d), lambda hh, kk: (kk, 0))],
        out_specs=[pl.BlockSpec((1, half, d), lambda hh, kk: (hh, 0, 0))],
        out_shape=[jax.ShapeDtypeStruct((2, half, d), BF16)],
        scratch_shapes=[pltpu.VMEM((half, d), F32)], inputs=[na, py, dfm])
    return outs[0]


def mix_dh(dqkvu, win8, x, dxo, g, mod, s, name, exchange=None):
    n, d = x.shape
    tm = _token_tile(n)
    nb = win8.shape[0]

    def core(dq_ref, w_ref, x_ref, dxo_ref, g_ref, mod_ref, dx_ref, red_ref):
        i = pl.program_id(0)
        dh = _dot(dq_ref[:, :IN_BLOCK], w_ref[0], NT)
        for j in range(1, nb):
            dh = dh + _dot(dq_ref[:, j * IN_BLOCK:(j + 1) * IN_BLOCK], w_ref[j], NT)
        is_ctx = _is_ctx(i, tm, s)
        dx, sums = _pre_bwd(x_ref[...], dh, dxo_ref[...], g_ref[...], mod_ref, 4, is_ctx)
        dx_ref[...] = dx
        _red_add(red_ref, i == 0, is_ctx, sums)

    tile = pl.BlockSpec((tm, d), lambda i: (i, 0))
    outs, xo = carrier_call(
        core, name=name, grid=(n // tm,),
        in_specs=[pl.BlockSpec((tm, nb * IN_BLOCK), lambda i: (i, 0)),
                  pl.BlockSpec((nb, d, IN_BLOCK), lambda i: (0, 0, 0)), tile, tile, _vec_spec(d), _mod_whole(d)],
        out_specs=[tile, _red_whole(d)],
        out_shape=[jax.ShapeDtypeStruct((n, d), F32), jax.ShapeDtypeStruct((2, 8, d), F32)],
        inputs=[dqkvu, win8, x, dxo, g, mod], exchange=exchange)
    return outs, xo


def _na_consts():
    j = np.arange(GRID_W)
    col_start = np.clip(j - NA_KW // 2, 0, GRID_W - NA_KW)
    valid = (j[None, :] >= col_start[:, None]) & (j[None, :] < col_start[:, None] + NA_KW)
    dc = np.clip(j[None, :] - j[:, None] + NA_KW - 1, 0, 2 * NA_KW - 2)
    onehot = np.zeros((LANES, GRID_W, GRID_W), np.float32)
    for d in range(2 * NA_KW - 1):
        onehot[d] = ((dc == d) & valid).astype(np.float32)
    negmask = np.where(valid, 0.0, NEG_INF).astype(np.float32)
    return onehot.reshape(LANES, GRID_W * GRID_W), np.tile(negmask, (1, NA_KH))


def bias_tables(rpb, name):
    onehot, negmask = _na_consts()
    nj = 2 * NA_KH - 1
    rows = NA_HEADS * nj
    a = jnp.pad(rpb.reshape(rows, 2 * NA_KW - 1), ((0, 0), (0, LANES - (2 * NA_KW - 1))))

    def body(a_ref, e_ref, o_ref):
        o_ref[...] = _dot(a_ref[...], e_ref[...], precision=HIGHEST)

    t = pl.pallas_call(
        body, name=name, out_shape=jax.ShapeDtypeStruct((rows, GRID_W * GRID_W), F32),
        in_specs=[pl.BlockSpec(memory_space=pltpu.VMEM)] * 2,
        out_specs=pl.BlockSpec(memory_space=pltpu.VMEM),
    )(a, jnp.asarray(onehot))
    t = t.reshape(NA_HEADS, nj, GRID_W, GRID_W)
    tb = jnp.stack([t[:, j0:j0 + NA_KH] for j0 in range(NA_KH)])
    tb = tb.transpose(0, 1, 3, 2, 4).reshape(NA_KH, NA_HEADS, GRID_W, NA_KH * GRID_W)
    return tb + jnp.asarray(negmask)[None, None]


def bias_tables_bwd(dtb, name):
    onehot, _ = _na_consts()
    nj = 2 * NA_KH - 1
    d5 = dtb.reshape(NA_KH, NA_HEADS, GRID_W, NA_KH, GRID_W).transpose(0, 3, 1, 2, 4)
    d2 = d5.reshape(NA_KH * NA_KH * NA_HEADS, GRID_W * GRID_W)

    def body(d_ref, e_ref, o_ref):
        r = _dot(d_ref[...], e_ref[...], NT, precision=HIGHEST)
        for j in range(nj):
            acc = jnp.zeros((NA_HEADS, LANES), F32)
            for j0 in range(NA_KH):
                kk = j - j0
                if 0 <= kk < NA_KH:
                    base = (j0 * NA_KH + kk) * NA_HEADS
                    acc = acc + r[base:base + NA_HEADS, :]
            o_ref[j] = acc

    out = pl.pallas_call(
        body, name=name, out_shape=jax.ShapeDtypeStruct((nj, NA_HEADS, LANES), F32),
        in_specs=[pl.BlockSpec(memory_space=pltpu.VMEM)] * 2,
        out_specs=pl.BlockSpec(memory_space=pltpu.VMEM),
        compiler_params=pltpu.CompilerParams(vmem_limit_bytes=VMEM_LIMIT),
    )(d2, jnp.asarray(onehot))
    return out[:, :, :2 * NA_KW - 1].transpose(1, 0, 2)


def _head_masks():
    lane = lax.broadcasted_iota(jnp.int32, (1, LANES), 1)
    return [(lane >= h * HEAD_DIM) & (lane < (h + 1) * HEAD_DIM) for h in range(HEADS_PER_BLOCK)]


def _row_window(r, rows):
    rs = jnp.clip(r - NA_KH // 2, 0, rows - NA_KH)
    return rs - r + NA_KH - 1, pl.multiple_of(rs * GRID_W, GRID_W)


def _stack_heads(t, masks):
    return jnp.concatenate([jnp.where(mk, t, jnp.zeros_like(t)) for mk in masks], axis=0)


def _unstack_heads(t2, masks):
    out = t2[(HEADS_PER_BLOCK - 1) * GRID_W:, :]
    for h in reversed(range(HEADS_PER_BLOCK - 1)):
        out = jnp.where(masks[h], t2[h * GRID_W:(h + 1) * GRID_W, :], out)
    return out


NA_ROWS_PER_STEP = 4
NA_STEP = NA_ROWS_PER_STEP * GRID_W
SLAB = NA_KH * GRID_W
K_COL = N_HEAD_BLOCKS
V_COL = 2 * N_HEAD_BLOCKS


def na_fwd(qkv, tb, s, name, exchange=None):
    n = qkv.shape[0]
    l = n - s
    rows = s // GRID_W
    rr = NA_ROWS_PER_STEP
    x_steps = rows // rr

    def core(q_ref, k_ref, v_ref, kc_ref, vc_ref, tb_ref, o_ref, lse_ref):
        rb = pl.program_id(1)

        @pl.when(rb >= x_steps)
        def _():
            o_ref[...] = jnp.zeros_like(o_ref)
            lse_ref[...] = jnp.zeros_like(lse_ref)

        @pl.when(rb < x_steps)
        def _():
            masks = _head_masks()
            kcb, vcb = kc_ref[...], vc_ref[...]
            wins, scores = [], []
            for t in range(rr):
                j0, off = _row_window(rb * rr + t, rows)
                q2 = _stack_heads(q_ref[t * GRID_W:(t + 1) * GRID_W, :] * ATT_SCALE, masks)
                bias = tb_ref[j0].reshape(HEADS_PER_BLOCK * GRID_W, SLAB)
                s_loc = _dot(q2, k_ref[pl.ds(off, SLAB), :], NT) + bias
                s_ctx = _dot(q2, kcb, NT)
                wins.append(off)
                scores.append((s_loc, s_ctx))
            probs = []
            for s_loc, s_ctx in scores:
                m = jnp.maximum(jnp.max(s_loc, axis=-1, keepdims=True), jnp.max(s_ctx, axis=-1, keepdims=True))
                p_loc = jnp.exp(s_loc - m)
                p_ctx = jnp.exp(s_ctx - m)
                den = jnp.sum(p_loc, axis=-1, keepdims=True) + jnp.sum(p_ctx, axis=-1, keepdims=True)
                probs.append((p_loc.astype(BF16), p_ctx.astype(BF16), den, m + jnp.log(den)))
            for t, (p_loc, p_ctx, den, lse2) in enumerate(probs):
                o2 = (_dot(p_loc, v_ref[pl.ds(wins[t], SLAB), :]) + _dot(p_ctx, vcb)) / den
                o_ref[t * GRID_W:(t + 1) * GRID_W, :] = _unstack_heads(o2, masks).astype(BF16)
                lse_ref[0, t * GRID_W:(t + 1) * GRID_W, :] = _unstack_heads(lse2, masks)

    cb = s // l
    outs, xo = carrier_call(
        core, name=name, grid=(N_HEAD_BLOCKS, n // NA_STEP),
        in_specs=[pl.BlockSpec((NA_STEP, LANES), lambda hb, rb: (jnp.minimum(rb, x_steps - 1), hb)),
                  pl.BlockSpec((s, LANES), lambda hb, rb: (0, K_COL + hb)),
                  pl.BlockSpec((s, LANES), lambda hb, rb: (0, V_COL + hb)),
                  pl.BlockSpec((l, LANES), lambda hb, rb: (cb, K_COL + hb)),
                  pl.BlockSpec((l, LANES), lambda hb, rb: (cb, V_COL + hb)),
                  pl.BlockSpec((NA_KH, HEADS_PER_BLOCK, GRID_W, SLAB), lambda hb, rb: (0, hb, 0, 0))],
        out_specs=[pl.BlockSpec((NA_STEP, LANES), lambda hb, rb: (rb, hb)),
                   pl.BlockSpec((1, NA_STEP, LANES), lambda hb, rb: (hb, rb, 0))],
        out_shape=[jax.ShapeDtypeStruct((n, NA_WIDTH), BF16), jax.ShapeDtypeStruct((N_HEAD_BLOCKS, n, LANES), F32)],
        inputs=[qkv, qkv, qkv, qkv, qkv, tb], exchange=exchange)
    return outs, xo


def na_bwd(qkv, tb, o, dmix, lse, s, name, exchange=None):
    n = qkv.shape[0]
    l = n - s
    rows = s // GRID_W
    rr = NA_ROWS_PER_STEP
    x_steps = rows // rr

    def core(q_ref, k_ref, v_ref, kc_ref, vc_ref, tb_ref, o_ref, do_ref, lse_ref, dq_ref, dk_ref, dv_ref, dtb_ref):
        rb = pl.program_id(1)

        @pl.when(rb == 0)
        def _():
            dk_ref[...] = jnp.zeros_like(dk_ref)
            dv_ref[...] = jnp.zeros_like(dv_ref)
            dtb_ref[...] = jnp.zeros_like(dtb_ref)

        @pl.when(rb >= x_steps)
        def _():
            dq_ref[...] = jnp.zeros_like(dq_ref)

        @pl.when(rb < x_steps)
        def _():
            masks = _head_masks()
            kcb, vcb = kc_ref[...], vc_ref[...]
            stage1 = []
            for t in range(rr):
                j0, off = _row_window(rb * rr + t, rows)
                sl = slice(t * GRID_W, (t + 1) * GRID_W)
                q2 = _stack_heads(q_ref[sl, :] * ATT_SCALE, masks)
                do_f = do_ref[sl, :]
                do2 = _stack_heads(do_f.astype(BF16), masks)
                dd = do_f * o_ref[sl, :].astype(F32)
                delta2 = jnp.concatenate(
                    [jnp.sum(jnp.where(mk, dd, 0.0), axis=-1, keepdims=True) for mk in masks], axis=0)
                lse_t = lse_ref[0, sl, :]
                lse2 = jnp.concatenate(
                    [lse_t[:, h * HEAD_DIM:h * HEAD_DIM + 1] for h in range(HEADS_PER_BLOCK)], axis=0)
                kslab = k_ref[pl.ds(off, SLAB), :]
                vslab = v_ref[pl.ds(off, SLAB), :]
                bias = tb_ref[j0].reshape(HEADS_PER_BLOCK * GRID_W, SLAB)
                s_loc = _dot(q2, kslab, NT) + bias - lse2
                s_ctx = _dot(q2, kcb, NT) - lse2
                dp_loc = _dot(do2, vslab, NT) - delta2
                dp_ctx = _dot(do2, vcb, NT) - delta2
                stage1.append((j0, off, q2, do2, s_loc, s_ctx, dp_loc, dp_ctx))
            stage2 = []
            for j0, off, q2, do2, s_loc, s_ctx, dp_loc, dp_ctx in stage1:
                p_loc = jnp.exp(s_loc)
                p_ctx = jnp.exp(s_ctx)
                ds_loc = p_loc * dp_loc
                dtb_ref[j0] += ds_loc.reshape(HEADS_PER_BLOCK, GRID_W, SLAB)
                stage2.append((off, q2, do2, p_loc.astype(BF16), p_ctx.astype(BF16), ds_loc.astype(BF16),
                               (p_ctx * dp_ctx).astype(BF16)))
            for t, (off, q2, do2, p_loc, p_ctx, ds_loc, ds_ctx) in enumerate(stage2):
                dq2 = (_dot(ds_loc, k_ref[pl.ds(off, SLAB), :]) + _dot(ds_ctx, kcb)) * ATT_SCALE
                dq_ref[t * GRID_W:(t + 1) * GRID_W, :] = _unstack_heads(dq2, masks)
                dk_ref[pl.ds(off, SLAB), :] += _dot(ds_loc, q2, TN)
                dv_ref[pl.ds(off, SLAB), :] += _dot(p_loc, do2, TN)
                dk_ref[s:, :] += _dot(ds_ctx, q2, TN)
                dv_ref[s:, :] += _dot(p_ctx, do2, TN)

    cb = s // l
    clamp = lambda hb, rb: (jnp.minimum(rb, x_steps - 1), hb)
    tile_in = pl.BlockSpec((NA_STEP, LANES), clamp)
    whole_out = pl.BlockSpec((n, LANES), lambda hb, rb: (0, hb))
    tbs = pl.BlockSpec((NA_KH, HEADS_PER_BLOCK, GRID_W, SLAB), lambda hb, rb: (0, hb, 0, 0))
    f32n = jax.ShapeDtypeStruct((n, NA_WIDTH), F32)
    outs, xo = carrier_call(
        core, name=name, grid=(N_HEAD_BLOCKS, n // NA_STEP),
        in_specs=[tile_in,
                  pl.BlockSpec((s, LANES), lambda hb, rb: (0, K_COL + hb)),
                  pl.BlockSpec((s, LANES), lambda hb, rb: (0, V_COL + hb)),
                  pl.BlockSpec((l, LANES), lambda hb, rb: (cb, K_COL + hb)),
                  pl.BlockSpec((l, LANES), lambda hb, rb: (cb, V_COL + hb)),
                  tbs, tile_in, tile_in,
                  pl.BlockSpec((1, NA_STEP, LANES), lambda hb, rb: (hb, jnp.minimum(rb, x_steps - 1), 0))],
        out_specs=[pl.BlockSpec((NA_STEP, LANES), lambda hb, rb: (rb, hb)), whole_out, whole_out, tbs],
        out_shape=[f32n, f32n, f32n, jax.ShapeDtypeStruct((NA_KH, NA_HEADS, GRID_W, SLAB), F32)],
        inputs=[qkv, qkv, qkv, qkv, qkv, tb, o, dmix, lse], exchange=exchange)
    return outs, xo


def ctx_attn_fwd(qkv, na, s, name):
    n = qkv.shape[0]
    l = n - s
    cb = s // l

    def core(q_ref, k_ref, v_ref, na_in, o_ref, lse_ref):
        masks = _head_masks()
        qt, kb, vb = q_ref[...], k_ref[...], v_ref[...]
        o_acc = jnp.zeros((l, LANES), F32)
        lse_acc = jnp.zeros((l, LANES), F32)
        for h in range(HEADS_PER_BLOCK):
            qh = jnp.where(masks[h], qt, jnp.zeros_like(qt))
            sc = _dot(qh, kb, NT) * ATT_SCALE
            m = jnp.max(sc, axis=-1, keepdims=True)
            p = jnp.exp(sc - m)
            den = jnp.sum(p, axis=-1, keepdims=True)
            o_acc = jnp.where(masks[h], _dot(p.astype(BF16), vb) / den, o_acc)
            lse_acc = jnp.where(masks[h], m + jnp.log(den), lse_acc)
        o_ref[...] = o_acc.astype(BF16)
        lse_ref[0] = lse_acc

    outs, _ = carrier_call(
        core, name=name, grid=(N_HEAD_BLOCKS,),
        in_specs=[pl.BlockSpec((l, LANES), lambda hb: (cb, hb)), pl.BlockSpec((l, LANES), lambda hb: (cb, K_COL + hb)),
                  pl.BlockSpec((l, LANES), lambda hb: (cb, V_COL + hb)), ANY],
        out_specs=[pl.BlockSpec((l, LANES), lambda hb: (cb, hb)), pl.BlockSpec((1, l, LANES), lambda hb: (hb, 0, 0))],
        out_shape=[jax.ShapeDtypeStruct(na.shape, BF16), jax.ShapeDtypeStruct((N_HEAD_BLOCKS, l, LANES), F32)],
        inputs=[qkv, qkv, qkv, na], aliases={3: 0})
    return outs


def ctx_attn_bwd(qkv, na, dmix, lse, dq, dk, dv, s, name):
    n = qkv.shape[0]
    l = n - s
    cb = s // l

    def core(q_ref, k_ref, v_ref, o_ref, do_ref, lse_ref, dq_in, dk_in, dv_in, dq_ref, dk_ref, dv_ref):
        masks = _head_masks()
        qt, kb, vb = q_ref[...], k_ref[...], v_ref[...]
        do_f = do_ref[...]
        dd = do_f * o_ref[...].astype(F32)
        do_b = do_f.astype(BF16)
        lse_t = lse_ref[0]
        dq_acc = jnp.zeros((l, LANES), F32)
        dk_acc = jnp.zeros((l, LANES), F32)
        dv_acc = jnp.zeros((l, LANES), F32)
        for h in range(HEADS_PER_BLOCK):
            qh = jnp.where(masks[h], qt, jnp.zeros_like(qt))
            doh = jnp.where(masks[h], do_b, jnp.zeros_like(do_b))
            delta = jnp.sum(jnp.where(masks[h], dd, 0.0), axis=-1, keepdims=True)
            p = jnp.exp(_dot(qh, kb, NT) * ATT_SCALE - lse_t[:, h * HEAD_DIM:h * HEAD_DIM + 1])
            ds = (p * (_dot(doh, vb, NT) - delta)).astype(BF16)
            dq_acc = jnp.where(masks[h], _dot(ds, kb) * ATT_SCALE, dq_acc)
            dk_acc = dk_acc + _dot(ds, qh, TN)
            dv_acc = dv_acc + _dot(p.astype(BF16), doh, TN)
        dq_ref[...] = dq_acc
        dk_ref[...] = dk_in[...] + dk_acc * ATT_SCALE
        dv_ref[...] = dv_in[...] + dv_acc

    blk = pl.BlockSpec((l, LANES), lambda hb: (cb, hb))
    f32n = jax.ShapeDtypeStruct((n, NA_WIDTH), F32)
    outs, _ = carrier_call(
        core, name=name, grid=(N_HEAD_BLOCKS,),
        in_specs=[blk, pl.BlockSpec((l, LANES), lambda hb: (cb, K_COL + hb)),
                  pl.BlockSpec((l, LANES), lambda hb: (cb, V_COL + hb)), blk, blk,
                  pl.BlockSpec((1, l, LANES), lambda hb: (hb, 0, 0)), ANY, blk, blk],
        out_specs=[blk, blk, blk], out_shape=[f32n, f32n, f32n],
        inputs=[qkv, qkv, qkv, na, dmix, lse, dq, dk, dv], aliases={6: 0, 7: 1, 8: 2})
    return outs


def _pool_consts(l):
    assert l == TM
    mem = np.zeros((2, POOL_GROUPS, TM, TM), np.float32)
    inv = np.zeros((2, POOL_GROUPS, TM, LANES), np.float32)
    for which, length in ((0, GRID_W), (1, l)):
        t = np.arange(length)
        for g, w in enumerate(POOL_WINDOWS):
            lo = np.clip(t - w // 2, 0, length)
            hi = np.clip(t - w // 2 + w, 0, length)
            blockm = ((t[None, :] >= lo[:, None]) & (t[None, :] < hi[:, None])).astype(np.float32)
            cnt = (hi - lo).astype(np.float32)
            for b in range(TM // length):
                mem[which, g, b * length:(b + 1) * length, b * length:(b + 1) * length] = blockm
                inv[which, g, b * length:(b + 1) * length, :] = (1.0 / cnt)[:, None]
    return mem, np.ascontiguousarray(mem.transpose(0, 1, 3, 2)), inv


def _split_dot(m01, val):
    hi = val.astype(BF16)
    lo = (val - hi.astype(F32)).astype(BF16)
    return _dot(m01, hi) + _dot(m01, lo)


def pool_fwd(u, mem, inv, wp, scale, nx_tiles, name):
    n = u.shape[0]

    def core(u_ref, m_ref, i_ref, wp_ref, s_ref, o_ref):
        for g in range(POOL_GROUPS):
            sl = slice(g * POOL_CH, (g + 1) * POOL_CH)
            ug = u_ref[:, sl]
            dg = _split_dot(m_ref[0, g], ug) * i_ref[0, g] - ug
            o_ref[:, sl] = (_dot(dg.astype(BF16), wp_ref[g]) * s_ref[:, sl]).astype(BF16)

    grp = lambda i: (i // nx_tiles, 0, 0, 0)
    outs, _ = carrier_call(
        core, name=name, grid=(n // TM,),
        in_specs=[pl.BlockSpec((TM, POOL_WIDTH), lambda i: (i, 0)),
                  pl.BlockSpec((1, POOL_GROUPS, TM, TM), grp),
                  pl.BlockSpec((1, POOL_GROUPS, TM, LANES), grp),
                  pl.BlockSpec((POOL_GROUPS, POOL_CH, POOL_CH), lambda i: (0, 0, 0)),
                  pl.BlockSpec((1, POOL_WIDTH), lambda i: (0, 0))],
        out_specs=[pl.BlockSpec((TM, POOL_WIDTH), lambda i: (i, 0))],
        out_shape=[jax.ShapeDtypeStruct((n, POOL_WIDTH), BF16)], inputs=[u, mem, inv, wp, scale])
    return outs[0]


def pool_bwd(dmix, u, mem, mem_t, inv, wp, scale, nx_tiles, name):
    n = u.shape[0]

    def core(dy_ref, u_ref, m_ref, mt_ref, i_ref, wp_ref, s_ref, du_ref, dwp_ref, dsc_ref):
        @pl.when(pl.program_id(0) == 0)
        def _():
            dwp_ref[...] = jnp.zeros_like(dwp_ref)
            dsc_ref[...] = jnp.zeros_like(dsc_ref)

        for g in range(POOL_GROUPS):
            sl = slice(g * POOL_CH, (g + 1) * POOL_CH)
            ug = u_ref[:, sl]
            dy = dy_ref[:, sl]
            dg = (_split_dot(m_ref[0, g], ug) * i_ref[0, g] - ug).astype(BF16)
            z = _dot(dg, wp_ref[g])
            dsc_ref[0:1, sl] += jnp.sum(dy * z, axis=0, keepdims=True)
            dz = (dy * s_ref[:, sl]).astype(BF16)
            dwp_ref[g] += _dot(dg, dz, TN)
            dd = _dot(dz, wp_ref[g], NT)
            du_ref[:, sl] = _split_dot(mt_ref[0, g], dd * i_ref[0, g]) - dd

    grp = lambda i: (i // nx_tiles, 0, 0, 0)
    outs, _ = carrier_call(
        core, name=name, grid=(n // TM,),
        in_specs=[pl.BlockSpec((TM, POOL_WIDTH), lambda i: (i, 1)),
                  pl.BlockSpec((TM, POOL_WIDTH), lambda i: (i, 0)),
                  pl.BlockSpec((1, POOL_GROUPS, TM, TM), grp),
                  pl.BlockSpec((1, POOL_GROUPS, TM, TM), grp),
                  pl.BlockSpec((1, POOL_GROUPS, TM, LANES), grp),
                  pl.BlockSpec((POOL_GROUPS, POOL_CH, POOL_CH), lambda i: (0, 0, 0)),
                  pl.BlockSpec((1, POOL_WIDTH), lambda i: (0, 0))],
        out_specs=[pl.BlockSpec((TM, POOL_WIDTH), lambda i: (i, 0)),
                   pl.BlockSpec((POOL_GROUPS, POOL_CH, POOL_CH), lambda i: (0, 0, 0)),
                   pl.BlockSpec((8, POOL_WIDTH), lambda i: (0, 0))],
        out_shape=[jax.ShapeDtypeStruct((n, POOL_WIDTH), F32),
                   jax.ShapeDtypeStruct((POOL_GROUPS, POOL_CH, POOL_CH), F32),
                   jax.ShapeDtypeStruct((8, POOL_WIDTH), F32)],
        inputs=[dmix, u, mem, mem_t, inv, wp, scale])
    return outs


MOD_ROWS = 16


def mod_fwd(cvecs, w, b, name):
    _, d = cvecs.shape
    cl = w.shape[2]
    tc = _pick(cl, (384, 128))

    def core(c_ref, w_ref, b_ref, o_ref):
        a = _silu(c_ref[...]).astype(BF16)
        o_ref[0] = _dot(a, w_ref[0].astype(BF16)) + b_ref[0]

    outs, _ = carrier_call(
        core, name=name, grid=(DEPTH, cl // tc),
        in_specs=[pl.BlockSpec((MOD_ROWS, d), lambda li, j: (0, 0)),
                  pl.BlockSpec((1, d, tc), lambda li, j: (li, 0, j)),
                  pl.BlockSpec((1, 1, tc), lambda li, j: (li, 0, j))],
        out_specs=[pl.BlockSpec((1, MOD_ROWS, tc), lambda li, j: (li, 0, j))],
        out_shape=[jax.ShapeDtypeStruct((DEPTH, MOD_ROWS, cl), F32)], inputs=[cvecs, w, b])
    return outs[0]


def mod_bwd(cvecs, dm, w, name):
    _, d = cvecs.shape
    cl = w.shape[2]
    tc = _pick(cl, (384, 128))

    def core(c_ref, dm_ref, w_ref, dw_ref, da_ref):
        @pl.when((pl.program_id(0) == 0) & (pl.program_id(1) == 0))
        def _():
            da_ref[...] = jnp.zeros_like(da_ref)

        a = _silu(c_ref[...]).astype(BF16)
        dmb = dm_ref[0].astype(BF16)
        dw_ref[0] = _dot(a, dmb, TN)
        da_ref[...] += _dot(dmb, w_ref[0].astype(BF16), NT)

    outs, _ = carrier_call(
        core, name=name, grid=(DEPTH, cl // tc),
        in_specs=[pl.BlockSpec((MOD_ROWS, d), lambda li, j: (0, 0)),
                  pl.BlockSpec((1, MOD_ROWS, tc), lambda li, j: (li, 0, j)),
                  pl.BlockSpec((1, d, tc), lambda li, j: (li, 0, j))],
        out_specs=[pl.BlockSpec((1, d, tc), lambda li, j: (li, 0, j)),
                   pl.BlockSpec((MOD_ROWS, d), lambda li, j: (0, 0))],
        out_shape=[jax.ShapeDtypeStruct((DEPTH, d, cl), F32), jax.ShapeDtypeStruct((MOD_ROWS, d), F32)],
        inputs=[cvecs, dm, w])
    return outs


def loss_head(y, target, name):
    n, d = y.shape
    s = target.shape[0]
    nt, nx = n // TM, s // TM

    def core(y_ref, t_ref, l_ref, dy_ref, acc_ref):
        i = pl.program_id(0)

        @pl.when(i == 0)
        def _():
            acc_ref[...] = jnp.zeros_like(acc_ref)

        @pl.when(i < nx)
        def _():
            e = y_ref[...] - t_ref[...]
            dy_ref[...] = e * (1.0 / d)
            acc_ref[...] += jnp.sum(e * e, axis=0, keepdims=True)

        @pl.when(i >= nx)
        def _():
            dy_ref[...] = jnp.zeros_like(dy_ref)

        @pl.when(i == nt - 1)
        def _():
            l_ref[...] = jnp.sum(acc_ref[...], axis=1, keepdims=True) * (0.5 / d)

    tile = pl.BlockSpec((TM, d), lambda i: (i, 0))
    outs, _ = carrier_call(
        core, name=name, grid=(nt,),
        in_specs=[tile, pl.BlockSpec((TM, d), lambda i: (jnp.minimum(i, nx - 1), 0))],
        out_specs=[pl.BlockSpec((1, 1), lambda i: (0, 0)), tile],
        out_shape=[jax.ShapeDtypeStruct((1, 1), F32), jax.ShapeDtypeStruct((n, d), F32)],
        scratch_shapes=[pltpu.VMEM((1, d), F32)], inputs=[y, target])
    return outs


def sum_devices(v, name):
    _, r, c = v.shape
    tr = _pick(r, (64, 8))

    def core(v_ref, o_ref):
        acc = v_ref[0]
        for p in range(1, N_DEV):
            acc = acc + v_ref[p]
        o_ref[...] = acc

    outs, _ = carrier_call(
        core, name=name, grid=(r // tr,), in_specs=[pl.BlockSpec((N_DEV, tr, c), lambda i: (0, i, 0))],
        out_specs=[pl.BlockSpec((tr, c), lambda i: (i, 0))], out_shape=[jax.ShapeDtypeStruct((r, c), F32)],
        inputs=[v])
    return outs[0]


def cctx_grad(parts, c_ctx, name):
    d = c_ctx.shape[1]

    def body(p_ref, c_ref, o_ref):
        acc = p_ref[0]
        for p in range(1, N_DEV):
            acc = acc + p_ref[p]
        o_ref[...] = acc[8:9, :] * _dsilu(c_ref[...])

    return pl.pallas_call(
        body, name=name, out_shape=jax.ShapeDtypeStruct((1, d), F32),
        in_specs=[pl.BlockSpec(memory_space=pltpu.VMEM)] * 2,
        out_specs=pl.BlockSpec(memory_space=pltpu.VMEM),
    )(parts, c_ctx)


def _adam_math(w, g, m, v):
    m2 = ADAM_B1 * m + (1.0 - ADAM_B1) * g
    v2 = ADAM_B2 * v + (1.0 - ADAM_B2) * (g * g)
    m_hat = m2 / (1.0 - ADAM_B1 ** ADAM_STEP)
    v_hat = v2 / (1.0 - ADAM_B2 ** ADAM_STEP)
    delta = -ADAM_LR * (m_hat / (jnp.sqrt(v_hat) + ADAM_EPS) + ADAM_WD * w)
    return delta, m2, v2


def adamw(w, g, m, v, name):
    r, c = w.shape
    tr = _pick(r, (256, 128, 64, 32, 16, 8, r))

    def core(w_ref, g_ref, m_ref, v_ref, d_ref, m2_ref, v2_ref):
        d_ref[...], m2_ref[...], v2_ref[...] = _adam_math(w_ref[...], g_ref[...], m_ref[...], v_ref[...])

    tile = pl.BlockSpec((tr, c), lambda i: (i, 0))
    out = jax.ShapeDtypeStruct((r, c), F32)
    outs, _ = carrier_call(core, name=name, grid=(r // tr,), in_specs=[tile] * 4, out_specs=[tile] * 3,
                           out_shape=[out] * 3, inputs=[w, g, m, v])
    return outs


def reduce_adamw(recv, w, m, v, name):
    r, c = w.shape
    tr = _pick(r, (256, 128, 64, 8))

    def core(recv_ref, w_ref, m_ref, v_ref, g_ref, d_ref, m2_ref, v2_ref):
        acc = recv_ref[0].astype(F32)
        for p in range(1, N_DEV):
            acc = acc + recv_ref[p].astype(F32)
        g_ref[...] = acc
        d_ref[...], m2_ref[...], v2_ref[...] = _adam_math(w_ref[...], acc, m_ref[...], v_ref[...])

    tile = pl.BlockSpec((tr, c), lambda i: (i, 0))
    out = jax.ShapeDtypeStruct((r, c), F32)
    outs, _ = carrier_call(
        core, name=name, grid=(r // tr,),
        in_specs=[pl.BlockSpec((N_DEV, tr, c), lambda i: (0, i, 0)), tile, tile, tile],
        out_specs=[tile] * 4, out_shape=[out] * 4, inputs=[recv, w, m, v])
    return outs


def kernel(x, c, ctx, c_ctx, w_mod, b_mod, norm_g, w_ffn_gate_up, w_ffn_down, w_in, w_out, na_rpb, w_pool, pool_scale, loss_target, m_c_ctx, m_w_mod, m_b_mod, m_norm_g, m_w_ffn_gate_up, m_w_ffn_down, m_w_in, m_w_out, m_na_rpb, m_w_pool, m_pool_scale, v_c_ctx, v_w_mod, v_b_mod, v_norm_g, v_w_ffn_gate_up, v_w_ffn_down, v_w_in, v_w_out, v_na_rpb, v_w_pool, v_pool_scale):
    s, d = x.shape[1], x.shape[2]
    l = ctx.shape[1]
    n = s + l
    nx = s // TM
    fq = w_ffn_gate_up.shape[-1]
    fr = w_ffn_down.shape[2]
    cl = w_mod.shape[2]
    dl = norm_g.shape[2]
    me = 4 * lax.axis_index("x") + 2 * lax.axis_index("y") + lax.axis_index("c")

    c_all = all_gather(c, "gather_c").reshape(N_DEV, d)
    cvecs = jnp.concatenate([c_all, c_ctx[None, :], jnp.zeros((MOD_ROWS - N_DEV - 1, d), F32)], axis=0)
    b_loc = lax.dynamic_slice(b_mod, (0, me * cl), (DEPTH, cl)).reshape(DEPTH, 1, cl)
    mod_loc = mod_fwd(cvecs, w_mod, b_loc, "mod_fwd")
    mod_all = all_gather(mod_loc.reshape(DEPTH * MOD_ROWS, cl), "gather_mod")
    mod_all = mod_all.reshape(N_DEV, DEPTH, MOD_ROWS, cl).transpose(1, 2, 0, 3).reshape(DEPTH, MOD_ROWS, N_DEV * cl)
    mine = lax.dynamic_slice(mod_all, (0, me, 0), (DEPTH, 1, N_DEV * cl))
    mods = jnp.concatenate([mine, mod_all[:, N_DEV:N_DEV + 1]], axis=1).reshape(DEPTH, 2, N_MOD, d)

    gu_b = w_ffn_gate_up.astype(BF16)
    dn_b = w_ffn_down.astype(BF16)
    wi_b = w_in.astype(BF16)
    wo_b = w_out.astype(BF16)
    wp_b = w_pool.astype(BF16)

    def ffn_shards(li, i):
        return [gu_b[li, i], dn_b[li, i]]

    def mix_shards(li):
        return [wi_b[li], wo_b[li]]

    def as_ffn_weights(gathered):
        return gathered[0].reshape(2, 4, d, fq), gathered[1].reshape(4, 2 * fr, d)

    def as_mix_weights(gathered):
        return gathered[0], gathered[1].reshape(N_DEV * wo_b.shape[1], d)

    tables = _rope_tables(s, n)
    mem_np, mem_t_np, inv_np = _pool_consts(l)
    mem, mem_t, inv = jnp.asarray(mem_np, BF16), jnp.asarray(mem_t_np, BF16), jnp.asarray(inv_np)
    first = gather_two_level([norm_g.reshape(DEPTH * 6, dl), gu_b[0, 0]], "gather_first")
    g_full = first[0].reshape(N_DEV, DEPTH, 6, dl).transpose(1, 2, 0, 3).reshape(DEPTH, 6, 1, N_DEV * dl)

    weights = {("ffn", 0, 0): (first[1].reshape(2, 4, d, fq), None)}
    saved = {}
    xcur = jnp.concatenate([x[0], ctx[0]], axis=0)
    for li in range(DEPTH):
        last = li == DEPTH - 1
        for i in range(2):
            tag = f"l{li}_ffn{i}"
            wgu, wd4 = weights[("ffn", li, i)]
            if i == 0:
                ex_up, ex_dn = Exchange(gathers=mix_shards(li) + ([dn_b[0, 0]] if wd4 is None else [])), None
            elif not last:
                ex_up, ex_dn = Exchange(gathers=[gu_b[li + 1, 0]]), Exchange(gathers=[dn_b[li + 1, 0]])
            else:
                ex_up = ex_dn = None
            (hb, gu, a4), got_up = ffn_up(xcur, g_full[li, 4 * i], mods[li], wgu, s, 6 * i, tag + "_up", ex_up)
            if wd4 is None:
                wd4 = got_up.pop().reshape(4, 2 * fr, d)
                weights[("ffn", li, i)] = (wgu, wd4)
            (ff, xnext), got_dn = ffn_down(a4, wd4, xcur, g_full[li, 4 * i + 1], mods[li], s, 6 * i + 2, tag + "_down", ex_dn)
            saved[("ffn", li, i)] = (xcur, hb, gu, a4, ff)
            xcur = xnext
            if i == 0:
                weights[("mix", li)] = as_mix_weights(got_up)
            elif not last:
                weights[("ffn", li + 1, 0)] = as_ffn_weights(got_up + got_dn)
            if i == 0:
                tag = f"l{li}_mix"
                win8, wout = weights[("mix", li)]
                (hb, qkv, u), got_dn = mix_in(xcur, g_full[li, 2], mods[li], win8, tables, s, tag + "_in",
                                              Exchange(gathers=[dn_b[li, 1]]))
                tb = bias_tables(na_rpb[li], tag + "_bias")
                (na, lse), got = na_fwd(qkv, tb, s, tag + "_na", Exchange(gathers=[gu_b[li, 1]]))
                weights[("ffn", li, 1)] = as_ffn_weights(got + got_dn)
                lse_c = None
                if not last:
                    na, lse_c = ctx_attn_fwd(qkv, na, s, tag + "_ctx_attn")
                py = pool_fwd(u, mem, inv, wp_b[li], pool_scale[li][None, :], nx, tag + "_pool")
                fm, xnext = mix_out(na, py, wout, xcur, g_full[li, 3], mods[li], s, tag + "_out")
                saved[("mix", li)] = (xcur, hb, qkv, u, tb, na, lse, lse_c, py, fm)
                xcur = xnext

    loss_local, dcur = loss_head(xcur, loss_target[0], "loss")
    loss = lax.psum(loss_local[0, 0], ("x", "y", "c"))

    recv = {"gu": lax.empty((N_DEV, 2 * DEPTH, d, fq), BF16), "dn": lax.empty((N_DEV, 2 * DEPTH, fr, d), BF16),
            "wi": lax.empty((N_DEV, DEPTH, d, IN_BLOCK), BF16), "wo": lax.empty((N_DEV, DEPTH, wo_b.shape[1], d), BF16)}
    pending = []

    def take(keys, gathers=()):
        nonlocal pending
        jobs = [(gr, recv[key], st) for key, gr, st in pending if key in keys]
        order = [key for key, _, _ in pending if key in keys]
        pending = [p for p in pending if p[0] not in keys]
        return Exchange(gathers=gathers, a2as=jobs), order

    def pad8(t):
        t = t.reshape(-1, d) if t.size % d == 0 else jnp.pad(t.reshape(-1), (0, -t.size % d)).reshape(-1, d)
        return jnp.pad(t, ((0, -t.shape[0] % 8), (0, 0)))

    def packed(parts):
        parts = [pad8(p) for p in parts]
        offs = np.cumsum([0] + [p.shape[0] for p in parts])
        return jnp.concatenate(parts + [jnp.zeros((-offs[-1] % 64, d), F32)], axis=0), offs

    def put(order, bufs):
        for key, buf in zip(order, bufs):
            recv[key] = buf

    d_rpb, d_wp, d_ps, d_mod, d_g = [], [], [], [], []
    for li in reversed(range(DEPTH)):
        reds = {}
        for i in (1, 0):
            tag = f"l{li}_ffn{i}"
            xin, hb, gu, a4, ff = saved[("ffn", li, i)]
            wgu, wd4 = weights[("ffn", li, i)]
            dff, red1 = post_bwd(ff, dcur, g_full[li, 4 * i + 1], mods[li], 6 * i + 2, 0.5, s, tag + "_post_bwd")
            early = []
            if (li, i) == (0, 0):
                early_small, early_offs = packed([jnp.stack(d_wp), jnp.stack(d_ps), jnp.stack(d_rpb)])
                early = [early_small]
            ex, _ = take((), early)
            g_dn, bufs = grad_weight(a4, dff, tag + "_dwdown", a_lead=4, exchange=ex)
            if early:
                early_sum = sum_devices(bufs[0], "sum_early_small_grads")
            pending += [("dn", g_dn.reshape(N_DEV, fr, d), 2 * li + i)]
            ex, order = take(("dn",))
            dgu, bufs = ffn_da(dff, wd4, gu, tag + "_da", ex)
            put(order, bufs)
            ex, order = take(("wi", "wo"))
            g_gu, bufs = grad_weight(hb, dgu.reshape(N_DEV, n, fq), tag + "_dwgu", b_lead=N_DEV, exchange=ex)
            put(order, bufs)
            pending += [("gu", g_gu, 2 * li + i)]
            ex, order = take(("gu",))
            (dcur, red2), bufs = ffn_dh(dgu, wgu, xin, dcur, g_full[li, 4 * i], mods[li], s, 6 * i, tag + "_dh", ex)
            put(order, bufs)
            reds[i] = (red1, red2)
            if i == 1:
                tag = f"l{li}_mix"
                xin, hb, qkv, u, tb, na, lse, lse_c, py, fm = saved[("mix", li)]
                win8, wout = weights[("mix", li)]
                dfm, redm1 = post_bwd(fm, dcur, g_full[li, 3], mods[li], 5, 1.0, s, tag + "_post_bwd")
                dmix = matmul_nt(dfm, wout, tag + "_dmix")
                g_wo = grad_wout(na, py, dfm, tag + "_dwout").reshape(N_DEV, wo_b.shape[1], d)
                du, gwp, gps = pool_bwd(dmix, u, mem, mem_t, inv, wp_b[li], pool_scale[li][None, :], nx, tag + "_pool_bwd")
                ex, order = take(("gu", "dn"))
                (dq, dk, dv, dtb), bufs = na_bwd(qkv, tb, na, dmix, lse, s, tag + "_na_bwd", ex)
                put(order, bufs)
                if li != DEPTH - 1:
                    dq, dk, dv = ctx_attn_bwd(qkv, na, dmix, lse_c, dq, dk, dv, s, tag + "_ctx_attn_bwd")
                grpb = bias_tables_bwd(dtb, tag + "_bias_bwd")
                dqkvu = qkv_bwd(dq, dk, dv, du, tables, tag + "_rope_bwd")
                (dcur, redm2), _ = mix_dh(dqkvu, win8, xin, dcur, g_full[li, 2], mods[li], s, tag + "_dh")
                g_wi, _ = grad_weight(hb, dqkvu, tag + "_dwin", b_cols=IN_BLOCK, per_step=4)
                pending += [("wi", g_wi, li), ("wo", g_wo, li)]
                d_rpb.insert(0, grpb)
                d_wp.insert(0, gwp)
                d_ps.insert(0, gps[0])
        (ra1, ra2), (rb1, rb2) = reds[0], reds[1]
        d_mod.insert(0, jnp.stack([ra2[:, 0], ra2[:, 1], ra1[:, 0], redm2[:, 0], redm2[:, 1], redm1[:, 0],
                                   rb2[:, 0], rb2[:, 1], rb1[:, 0]], axis=1))
        d_g.insert(0, jnp.stack([t[0] + t[1] for t in (ra2[:, 2], ra1[:, 1], redm2[:, 2], redm1[:, 1], rb2[:, 2], rb1[:, 1])]))
    grad_x = dcur[:s][None]

    small, offs = packed([jnp.stack(d_mod), jnp.stack(d_g)])
    ex, order = take(("gu", "dn", "wi", "wo"), [small])
    bufs = exchange_only(ex, "exchange_last")
    small_all = bufs[0]
    put(order, bufs[1:])
    small_sum = sum_devices(small_all, "sum_small_grads")

    n_mod_rows = DEPTH * 2 * N_MOD
    dmod_all = small_all[:, :n_mod_rows].reshape(N_DEV, DEPTH, 2, N_MOD * d)
    dmod_sum = small_sum[:n_mod_rows].reshape(DEPTH, 2, N_MOD * d)
    dm_rows = jnp.concatenate([dmod_all[:, :, 0].transpose(1, 0, 2), dmod_sum[:, 1:2],
                               jnp.zeros((DEPTH, MOD_ROWS - N_DEV - 1, N_MOD * d), F32)], axis=1)
    grad_b_mod = dmod_sum[:, 0] + dmod_sum[:, 1]
    dm_loc = lax.dynamic_slice(dm_rows, (0, 0, me * cl), (DEPTH, MOD_ROWS, cl))
    grad_w_mod, da_part = mod_bwd(cvecs, dm_loc, w_mod, "mod_bwd")
    da_all = all_gather(da_part, "gather_dcvec")
    grad_c_ctx = cctx_grad(da_all, c_ctx[None, :], "c_ctx_grad")[0]

    grad_norm_full = small_sum[offs[1]:offs[1] + DEPTH * 6].reshape(DEPTH, 6, d)
    grad_norm_g = lax.dynamic_slice(grad_norm_full, (0, 0, me * dl), (DEPTH, 6, dl))
    grad_w_pool = early_sum[early_offs[0]:early_offs[0] + w_pool.size // d].reshape(w_pool.shape)
    grad_pool_scale = early_sum[early_offs[1]:early_offs[1] + pool_scale.size // d].reshape(pool_scale.shape)
    grad_na_rpb = early_sum[early_offs[2]:early_offs[3]].reshape(-1)[:na_rpb.size].reshape(na_rpb.shape)

    def big_adam(key, w, m, v, name):
        shp = w.shape
        cols = shp[-1]
        outs = reduce_adamw(recv[key].reshape(N_DEV, -1, cols), w.reshape(-1, cols), m.reshape(-1, cols),
                            v.reshape(-1, cols), name)
        return tuple(t.reshape(shp) for t in outs)

    def small_adam(w, g, m, v, name):
        shp = w.shape
        cols = shp[-1]
        outs = adamw(w.reshape(-1, cols), g.reshape(-1, cols), m.reshape(-1, cols), v.reshape(-1, cols), name)
        return tuple(t.reshape(shp) for t in outs)

    b_gu = big_adam("gu", w_ffn_gate_up, m_w_ffn_gate_up, v_w_ffn_gate_up, "adam_gate_up")
    b_dn = big_adam("dn", w_ffn_down, m_w_ffn_down, v_w_ffn_down, "adam_down")
    b_wi = big_adam("wi", w_in, m_w_in, v_w_in, "adam_w_in")
    b_wo = big_adam("wo", w_out, m_w_out, v_w_out, "adam_w_out")
    a_cc = small_adam(c_ctx, grad_c_ctx, m_c_ctx, v_c_ctx, "adam_c_ctx")
    a_wm = small_adam(w_mod, grad_w_mod, m_w_mod, v_w_mod, "adam_w_mod")
    a_bm = small_adam(b_mod, grad_b_mod, m_b_mod, v_b_mod, "adam_b_mod")
    a_ng = small_adam(norm_g, grad_norm_g, m_norm_g, v_norm_g, "adam_norm_g")
    a_rp = small_adam(na_rpb, grad_na_rpb, m_na_rpb, v_na_rpb, "adam_na_rpb")
    a_wp = small_adam(w_pool, grad_w_pool, m_w_pool, v_w_pool, "adam_w_pool")
    a_ps = small_adam(pool_scale, grad_pool_scale, m_pool_scale, v_pool_scale, "adam_pool_scale")

    grads = (grad_c_ctx, grad_w_mod, grad_b_mod, grad_norm_g, b_gu[0], b_dn[0], b_wi[0], b_wo[0], grad_na_rpb, grad_w_pool, grad_pool_scale)
    deltas = (a_cc[0], a_wm[0], a_bm[0], a_ng[0], b_gu[1], b_dn[1], b_wi[1], b_wo[1], a_rp[0], a_wp[0], a_ps[0])
    new_m = (a_cc[1], a_wm[1], a_bm[1], a_ng[1], b_gu[2], b_dn[2], b_wi[2], b_wo[2], a_rp[1], a_wp[1], a_ps[1])
    new_v = (a_cc[2], a_wm[2], a_bm[2], a_ng[2], b_gu[3], b_dn[3], b_wi[3], b_wo[3], a_rp[2], a_wp[2], a_ps[2])
    return (loss, grad_x, *grads, *deltas, *new_m, *new_v)
```

```python
import functools
import math

import numpy as np
import jax
import jax.numpy as jnp
from jax import lax
from jax.experimental import pallas as pl
from jax.experimental.pallas import tpu as pltpu

F32 = jnp.float32
BF16 = jnp.bfloat16

N_DEV = 8
DEPTH = 2
GRID_W = 64
N_MOD = 9
NA_HEADS = 8
HEAD_DIM = 64
NA_WIDTH = NA_HEADS * HEAD_DIM
NA_KH = 8
NA_KW = 16
POOL_GROUPS = 4
POOL_CH = 128
POOL_WIDTH = POOL_GROUPS * POOL_CH
POOL_WINDOWS = (2, 4, 8, 16)
ROPE_THETA = 10000.0
ROPE_PAIRS = HEAD_DIM // 4
RMS_EPS = 1e-6
NEG_INF = -1e30
ATT_SCALE = HEAD_DIM ** -0.5

ADAM_LR = 0.001
ADAM_B1 = 0.9
ADAM_B2 = 0.999
ADAM_EPS = 1e-08
ADAM_WD = 0.01
ADAM_STEP = 10

TM = 256
LANES = 128
HEADS_PER_BLOCK = LANES // HEAD_DIM
N_HEAD_BLOCKS = NA_WIDTH // LANES
IN_BLOCK = 2 * LANES
N_QKV_BLOCKS = 3 * NA_WIDTH // IN_BLOCK
VMEM_LIMIT = 56 * 1024 * 1024
HIGHEST = lax.Precision.HIGHEST
MESH = pl.DeviceIdType.MESH
ANY = pl.BlockSpec(memory_space=pl.ANY)

NN = (((1,), (0,)), ((), ()))
NT = (((1,), (1,)), ((), ()))
TN = (((0,), (0,)), ((), ()))


def _pick(n, cands):
    for t in cands:
        if n % t == 0:
            return t
    raise ValueError(f"no tile for {n} among {cands}")


def _dot(a, b, dn=NN, precision=None):
    return lax.dot_general(a, b, dn, preferred_element_type=F32, precision=precision)


def _silu(x):
    return x * jax.nn.sigmoid(x)


def _dsilu(x):
    s = jax.nn.sigmoid(x)
    return s * (1.0 + x * (1.0 - s))


def _peer(mask):
    x, y, c = lax.axis_index("x"), lax.axis_index("y"), lax.axis_index("c")
    px = 1 - x if mask & 4 else x
    py = 1 - y if mask & 2 else y
    pc = 1 - c if mask & 1 else c
    return (px, py, pc), 4 * px + 2 * py + pc


class Exchange:
    def __init__(self, gathers=(), a2as=()):
        self.gathers = list(gathers)
        self.a2as = list(a2as)
        self.n_jobs = len(self.gathers) + len(self.a2as)

    def inputs(self):
        out = list(self.gathers)
        for v, buf, _ in self.a2as:
            out += [v, buf]
        return out

    def out_shapes(self):
        shapes = [jax.ShapeDtypeStruct((N_DEV,) + v.shape, v.dtype) for v in self.gathers]
        shapes += [jax.ShapeDtypeStruct(buf.shape, buf.dtype) for _, buf, _ in self.a2as]
        return shapes

    def aliases(self, n_in, n_out):
        ng = len(self.gathers)
        return {n_in + ng + 2 * k + 1: n_out + ng + k for k in range(len(self.a2as))}

    def scratch(self):
        per = N_DEV - 1
        return [pltpu.SemaphoreType.DMA((per * self.n_jobs,)), pltpu.SemaphoreType.DMA((per * self.n_jobs,)),
                pltpu.SemaphoreType.DMA((self.n_jobs,))]

    def _copies(self, in_refs, out_refs, sems, with_recvs):
        send_sems, recv_sems, local_sems = sems
        _, me = _peer(0)
        ng = len(self.gathers)
        local, sends, recvs = [], [], []
        for job in range(self.n_jobs):
            if job < ng:
                src_of = lambda pid, r=in_refs[job]: r
                dst_of = lambda pid, r=out_refs[job]: r.at[pid]
            else:
                k = job - ng
                stage = self.a2as[k][2]
                src_of = lambda pid, r=in_refs[ng + 2 * k]: r.at[pid]
                dst_of = lambda pid, r=out_refs[job], st=stage: r.at[pid, st]
            local.append(pltpu.make_async_copy(src_of(me), dst_of(me), local_sems.at[job]))
            for mask in range(1, N_DEV):
                peer, pid = _peer(mask)
                idx = job * (N_DEV - 1) + mask - 1
                sends.append(pltpu.make_async_remote_copy(
                    src_ref=src_of(pid), dst_ref=dst_of(me), send_sem=send_sems.at[idx],
                    recv_sem=recv_sems.at[idx], device_id=peer, device_id_type=MESH))
                if with_recvs:
                    recvs.append(pltpu.make_async_remote_copy(
                        src_ref=src_of(pid), dst_ref=dst_of(pid), send_sem=send_sems.at[idx],
                        recv_sem=recv_sems.at[idx], device_id=peer, device_id_type=MESH))
        return local, sends, recvs

    def start(self, in_refs, out_refs, sems):
        local, sends, _ = self._copies(in_refs, out_refs, sems, False)
        for cp in local + sends:
            cp.start()

    def wait(self, in_refs, out_refs, sems):
        local, sends, recvs = self._copies(in_refs, out_refs, sems, True)
        for cp in recvs:
            cp.wait_recv()
        for cp in sends:
            cp.wait_send()
        for cp in local:
            cp.wait()


def carrier_call(core, *, name, grid, in_specs, out_specs, out_shape, inputs, scratch_shapes=(), aliases=None,
                 exchange=None):
    aliases = dict(aliases or {})
    n_in, n_out, n_sc = len(in_specs), len(out_specs), len(scratch_shapes)
    sem = ("arbitrary",) * len(grid)
    params = pltpu.CompilerParams(dimension_semantics=sem, vmem_limit_bytes=VMEM_LIMIT)
    if exchange is None or exchange.n_jobs == 0:
        outs = pl.pallas_call(core, name=name, grid=grid, in_specs=list(in_specs), out_specs=tuple(out_specs),
                              out_shape=tuple(out_shape), scratch_shapes=list(scratch_shapes),
                              input_output_aliases=aliases, compiler_params=params)(*inputs)
        return list(outs), []
    x_in = exchange.inputs()
    x_out = exchange.out_shapes()
    aliases.update(exchange.aliases(n_in, n_out))

    def body(*refs):
        a = n_in + len(x_in)
        b = a + n_out + len(x_out)
        core_in, job_in = refs[:n_in], refs[n_in:a]
        core_out, job_out = refs[a:a + n_out], refs[a + n_out:b]
        core_sc, job_sc = refs[b:b + n_sc], refs[b + n_sc:]
        first = functools.reduce(lambda p, q: p & q, [pl.program_id(ax) == 0 for ax in range(len(grid))])
        last = functools.reduce(lambda p, q: p & q, [pl.program_id(ax) == g - 1 for ax, g in enumerate(grid)])

        @pl.when(first)
        def _():
            exchange.start(job_in, job_out, job_sc)

        core(*core_in, *core_out, *core_sc)

        @pl.when(last)
        def _():
            exchange.wait(job_in, job_out, job_sc)

    outs = pl.pallas_call(
        body, name=name, grid=grid, in_specs=list(in_specs) + [ANY] * len(x_in),
        out_specs=tuple(out_specs) + (ANY,) * len(x_out), out_shape=tuple(out_shape) + tuple(x_out),
        scratch_shapes=list(scratch_shapes) + exchange.scratch(), input_output_aliases=aliases,
        compiler_params=params)(*inputs, *x_in)
    return list(outs[:n_out]), list(outs[n_out:])


def exchange_only(exchange, name):
    def body(*refs):
        n_in, n_out = len(exchange.inputs()), len(exchange.out_shapes())
        job_in, job_out, sems = refs[:n_in], refs[n_in:n_in + n_out], refs[n_in + n_out:]
        exchange.start(job_in, job_out, sems)
        exchange.wait(job_in, job_out, sems)

    x_in = exchange.inputs()
    outs = pl.pallas_call(
        body, name=name, in_specs=[ANY] * len(x_in), out_specs=(ANY,) * len(exchange.out_shapes()),
        out_shape=tuple(exchange.out_shapes()), scratch_shapes=exchange.scratch(),
        input_output_aliases=exchange.aliases(0, 0))(*x_in)
    return list(outs)


def all_gather(v, name):
    return exchange_only(Exchange(gathers=[v]), name)[0]


def gather_two_level(vs, name):
    nv = len(vs)
    per = N_DEV - 1

    def body(*refs):
        v_refs, o_refs = refs[:nv], refs[nv:2 * nv]
        send_sems, recv_sems, local_sems = refs[2 * nv:]
        x, y, c = lax.axis_index("x"), lax.axis_index("y"), lax.axis_index("c")
        me, sibling = (x, y, c), (x, y, 1 - c)
        chips = [(1 - x, y), (x, 1 - y), (1 - x, 1 - y)]

        def copy(a, k, block, to, src=None):
            dst = o_refs[a].at[4 * block[0] + 2 * block[1] + block[2]]
            return pltpu.make_async_remote_copy(
                src_ref=dst if src is None else src, dst_ref=dst, send_sem=send_sems.at[a * per + k],
                recv_sem=recv_sems.at[a * per + k], device_id=to, device_id_type=MESH)

        mine = [pltpu.make_async_copy(v_refs[a], o_refs[a].at[4 * x + 2 * y + c], local_sems.at[a]) for a in range(nv)]
        first = []
        for a in range(nv):
            first.append(copy(a, 0, me, sibling, src=v_refs[a]))
            first += [copy(a, 1 + j, me, (*chip, c), src=v_refs[a]) for j, chip in enumerate(chips)]
        for cp in mine + first:
            cp.start()
        passed = []
        for a in range(nv):
            for j, chip in enumerate(chips):
                copy(a, 1 + j, (*chip, c), me).wait_recv()
                passed.append(copy(a, 4 + j, (*chip, c), sibling))
                passed[-1].start()
        for a in range(nv):
            copy(a, 0, sibling, me).wait_recv()
            for j, chip in enumerate(chips):
                copy(a, 4 + j, (*chip, 1 - c), me).wait_recv()
        for cp in first + passed:
            cp.wait_send()
        for cp in mine:
            cp.wait()

    outs = pl.pallas_call(
        body, name=name, in_specs=[ANY] * nv, out_specs=(ANY,) * nv,
        out_shape=tuple(jax.ShapeDtypeStruct((N_DEV,) + v.shape, v.dtype) for v in vs),
        scratch_shapes=[pltpu.SemaphoreType.DMA((per * nv,)), pltpu.SemaphoreType.DMA((per * nv,)),
                        pltpu.SemaphoreType.DMA((nv,))])(*vs)
    return list(outs)


def _rms(xf):
    return lax.rsqrt(jnp.mean(xf * xf, axis=-1, keepdims=True) + RMS_EPS)


def _is_ctx(i, tm, s):
    return (i * tm + lax.broadcasted_iota(jnp.int32, (tm, 1), 0)) >= s


def _by_tile_kind(i, tm, s, fn):
    n_latent = s // tm

    @pl.when(i < n_latent)
    def _():
        fn(None)

    @pl.when(i >= n_latent)
    def _():
        fn(_is_ctx(i, tm, s))


def _mod_rows(mod_ref, k, is_ctx):
    if is_ctx is None:
        return mod_ref[0, k:k + 1, :]
    return jnp.where(is_ctx, mod_ref[1, k:k + 1, :], mod_ref[0, k:k + 1, :])


def _norm_mod(xf, g, mod_ref, k_shift, k_scale, is_ctx):
    nrm = xf * _rms(xf) * g
    return (nrm * (1.0 + _mod_rows(mod_ref, k_scale, is_ctx)) + _mod_rows(mod_ref, k_shift, is_ctx)).astype(BF16)


def _post(xf, ff, g, mod_ref, k_gate, coef, is_ctx):
    return xf + coef * _mod_rows(mod_ref, k_gate, is_ctx) * (ff * _rms(ff) * g)


def _red_add(red_ref, first, is_ctx, rows):
    @pl.when(first)
    def _():
        red_ref[...] = jnp.zeros_like(red_ref)

    for r, val in enumerate(rows):
        tot = jnp.sum(val, axis=0, keepdims=True)
        if is_ctx is None:
            red_ref[0, r:r + 1, :] += tot
        else:
            ctx = jnp.sum(jnp.where(is_ctx, val, 0.0), axis=0, keepdims=True)
            red_ref[0, r:r + 1, :] += tot - ctx
            red_ref[1, r:r + 1, :] += ctx


def _pre_bwd(xf, dh, dxo, g, mod_ref, k_scale, is_ctx):
    r = _rms(xf)
    xhat = xf * r
    dn = dh * (1.0 + _mod_rows(mod_ref, k_scale, is_ctx))
    dxhat = dn * g
    dx = dxo + r * (dxhat - xhat * jnp.mean(dxhat * xhat, axis=-1, keepdims=True))
    return dx, [dh, dh * (xhat * g), dn * xhat]


def _vec_spec(d):
    return pl.BlockSpec((1, d), lambda *_: (0, 0))


def _mod_whole(d):
    return pl.BlockSpec((2, N_MOD, d), lambda *_: (0, 0, 0))


def _red_whole(d):
    return pl.BlockSpec((2, 8, d), lambda *_: (0, 0, 0))


def _token_tile(n):
    return _pick(n, (768, 640, 512, 384, 256))


def _ffn_tile(n):
    return _pick(n, (528, 384, 640, 256))


def _resident(shape):
    zeros = (0,) * len(shape)
    return pl.BlockSpec(shape, lambda i: zeros, pipeline_mode=pl.Buffered(1))


def ffn_up(x, g, mod, wgu, s, k0, name, exchange=None):
    n, d = x.shape
    nk, fq = wgu.shape[1], wgu.shape[-1]
    tm = _ffn_tile(n)

    def core(x_ref, g_ref, mod_ref, w_ref, hb_ref, gu_ref, a_ref):
        i = pl.program_id(0)

        def prologue(is_ctx):
            hb_ref[...] = _norm_mod(x_ref[...], g_ref[...], mod_ref, k0, k0 + 1, is_ctx)

        _by_tile_kind(i, tm, s, prologue)
        h = hb_ref[...]
        for k in range(nk):
            gg = _dot(h, w_ref[0, k])
            uu = _dot(h, w_ref[1, k])
            gu_ref[0, k] = gg.astype(BF16)
            gu_ref[1, k] = uu.astype(BF16)
            a_ref[k] = (_silu(gg) * uu).astype(BF16)

    outs, xo = carrier_call(
        core, name=name, grid=(n // tm,),
        in_specs=[pl.BlockSpec((tm, d), lambda i: (i, 0)), _vec_spec(d), _mod_whole(d), _resident(wgu.shape)],
        out_specs=[pl.BlockSpec((tm, d), lambda i: (i, 0)),
                   pl.BlockSpec((2, nk, tm, fq), lambda i: (0, 0, i, 0)),
                   pl.BlockSpec((nk, tm, fq), lambda i: (0, i, 0))],
        out_shape=[jax.ShapeDtypeStruct((n, d), BF16), jax.ShapeDtypeStruct((2, nk, n, fq), BF16),
                   jax.ShapeDtypeStruct((nk, n, fq), BF16)],
        inputs=[x, g, mod, wgu], exchange=exchange)
    return outs, xo


def ffn_down(a4, wd4, x, g, mod, s, k_gate, name, exchange=None):
    n, d = x.shape
    nk, fq = wd4.shape[0], wd4.shape[1]
    tm = _ffn_tile(n)

    def core(a_ref, w_ref, x_ref, g_ref, mod_ref, f_ref, xo_ref):
        i = pl.program_id(0)
        ff = _dot(a_ref[0], w_ref[0])
        for k in range(1, nk):
            ff = ff + _dot(a_ref[k], w_ref[k])
        f_ref[...] = ff

        def epilogue(is_ctx):
            xo_ref[...] = _post(x_ref[...], f_ref[...], g_ref[...], mod_ref, k_gate, 0.5, is_ctx)

        _by_tile_kind(i, tm, s, epilogue)

    tile = pl.BlockSpec((tm, d), lambda i: (i, 0))
    outs, xo = carrier_call(
        core, name=name, grid=(n // tm,),
        in_specs=[pl.BlockSpec((nk, tm, fq), lambda i: (0, i, 0)), _resident(wd4.shape), tile, _vec_spec(d),
                  _mod_whole(d)],
        out_specs=[tile, tile],
        out_shape=[jax.ShapeDtypeStruct((n, d), F32), jax.ShapeDtypeStruct((n, d), F32)],
        inputs=[a4, wd4, x, g, mod], exchange=exchange)
    return outs, xo


def ffn_da(df, wd4, gu, name, exchange=None):
    n, d = df.shape
    nk, fq = wd4.shape[0], wd4.shape[1]
    tm = _ffn_tile(n)

    def core(df_ref, w_ref, gu_ref, o_ref):
        dfv = df_ref[...]
        for k in range(nk):
            da = _dot(dfv, w_ref[k], NT).astype(BF16)
            gg = gu_ref[0, k]
            uu = gu_ref[1, k]
            sg = jax.nn.sigmoid(gg.astype(F32)).astype(BF16)
            o_ref[0, k] = da * (uu * (sg * (1 + gg * (1 - sg))))
            o_ref[1, k] = da * (gg * sg)

    gu_spec = pl.BlockSpec((2, nk, tm, fq), lambda i: (0, 0, i, 0))
    outs, xo = carrier_call(
        core, name=name, grid=(n // tm,),
        in_specs=[pl.BlockSpec((tm, d), lambda i: (i, 0)), _resident(wd4.shape), gu_spec],
        out_specs=[gu_spec], out_shape=[jax.ShapeDtypeStruct(gu.shape, BF16)],
        inputs=[df, wd4, gu], exchange=exchange)
    return outs[0], xo


def ffn_dh(dgu, wgu, x, dxo, g, mod, s, k0, name, exchange=None):
    n, d = x.shape
    nk, fq = wgu.shape[1], wgu.shape[-1]
    tm = _ffn_tile(n)

    def core(dgu_ref, w_ref, x_ref, dxo_ref, g_ref, mod_ref, dx_ref, red_ref, dh_s):
        i = pl.program_id(0)
        dh = _dot(dgu_ref[0, 0], w_ref[0, 0], NT) + _dot(dgu_ref[1, 0], w_ref[1, 0], NT)
        for k in range(1, nk):
            dh = dh + _dot(dgu_ref[0, k], w_ref[0, k], NT) + _dot(dgu_ref[1, k], w_ref[1, k], NT)
        dh_s[...] = dh

        def epilogue(is_ctx):
            dx, sums = _pre_bwd(x_ref[...], dh_s[...], dxo_ref[...], g_ref[...], mod_ref, k0 + 1, is_ctx)
            dx_ref[...] = dx
            _red_add(red_ref, i == 0, is_ctx, sums)

        _by_tile_kind(i, tm, s, epilogue)

    tile = pl.BlockSpec((tm, d), lambda i: (i, 0))
    outs, xo = carrier_call(
        core, name=name, grid=(n // tm,),
        in_specs=[pl.BlockSpec((2, nk, tm, fq), lambda i: (0, 0, i, 0)), _resident(wgu.shape), tile, tile,
                  _vec_spec(d), _mod_whole(d)],
        out_specs=[tile, _red_whole(d)],
        out_shape=[jax.ShapeDtypeStruct((n, d), F32), jax.ShapeDtypeStruct((2, 8, d), F32)],
        scratch_shapes=[pltpu.VMEM((tm, d), F32)], inputs=[dgu, wgu, x, dxo, g, mod], exchange=exchange)
    return outs, xo


def grad_weight(a, b, name, a_lead=None, b_lead=None, b_cols=None, per_step=1, exchange=None):
    n = a.shape[-2]
    ka = a.shape[-1]
    kb = b_cols or b.shape[-1]
    nj = a_lead or b_lead or (b.shape[-1] // b_cols)
    tk = _pick(n, (1408, 1024, 768, 640, 512, 256))
    nk = n // tk
    ps = per_step
    assert nj % ps == 0

    def core(a_ref, b_ref, o_ref, acc):
        kk = pl.program_id(1)

        @pl.when(kk == 0)
        def _():
            acc[...] = jnp.zeros_like(acc)

        if b_cols:
            acc[...] += _dot(a_ref[...], b_ref[...], TN)
        else:
            for t in range(ps):
                acc[t] += _dot(a_ref[t] if a_lead else a_ref[...], b_ref[t] if b_lead else b_ref[...], TN)

        @pl.when(kk == nk - 1)
        def _():
            for t in range(ps):
                o_ref[t] = (acc[:, t * kb:(t + 1) * kb] if b_cols else acc[t]).astype(BF16)

    a_spec = (pl.BlockSpec((ps, tk, ka), lambda j, kk: (j, kk, 0)) if a_lead
              else pl.BlockSpec((tk, ka), lambda j, kk: (kk, 0)))
    if b_lead:
        b_spec = pl.BlockSpec((ps, tk, kb), lambda j, kk: (j, kk, 0))
    elif b_cols:
        b_spec = pl.BlockSpec((tk, ps * kb), lambda j, kk: (kk, j))
    else:
        b_spec = pl.BlockSpec((tk, kb), lambda j, kk: (kk, 0))
    outs, xo = carrier_call(
        core, name=name, grid=(nj // ps, nk), in_specs=[a_spec, b_spec],
        out_specs=[pl.BlockSpec((ps, ka, kb), lambda j, kk: (j, 0, 0))],
        out_shape=[jax.ShapeDtypeStruct((nj, ka, kb), BF16)],
        scratch_shapes=[pltpu.VMEM((ka, ps * kb) if b_cols else (ps, ka, kb), F32)], inputs=[a, b], exchange=exchange)
    return outs[0], xo


def post_bwd(f, dxo, g, mod, k_gate, coef, s, name):
    n, d = f.shape

    def core(f_ref, dxo_ref, g_ref, mod_ref, df_ref, red_ref):
        i = pl.program_id(0)

        def body(is_ctx):
            ff = f_ref[...]
            dxo_ = dxo_ref[...]
            gg = g_ref[...]
            r = _rms(ff)
            fn = ff * r
            dy = (coef * _mod_rows(mod_ref, k_gate, is_ctx)) * dxo_
            dfn = dy * gg
            df_ref[...] = (r * (dfn - fn * jnp.mean(dfn * fn, axis=-1, keepdims=True))).astype(BF16)
            _red_add(red_ref, i == 0, is_ctx, [coef * (fn * gg) * dxo_, dy * fn])

        _by_tile_kind(i, TM, s, body)

    tile = pl.BlockSpec((TM, d), lambda i: (i, 0))
    outs, _ = carrier_call(
        core, name=name, grid=(n // TM,), in_specs=[tile, tile, _vec_spec(d), _mod_whole(d)],
        out_specs=[tile, _red_whole(d)],
        out_shape=[jax.ShapeDtypeStruct((n, d), BF16), jax.ShapeDtypeStruct((2, 8, d), F32)],
        inputs=[f, dxo, g, mod])
    return outs


def matmul_nt(a, b, name):
    m, k = a.shape
    n = b.shape[0]
    tm = _token_tile(m)

    def core(a_ref, b_ref, o_ref):
        o_ref[...] = _dot(a_ref[...], b_ref[...], NT)

    outs, _ = carrier_call(
        core, name=name, grid=(m // tm,),
        in_specs=[pl.BlockSpec((tm, k), lambda i: (i, 0)), pl.BlockSpec((n, k), lambda i: (0, 0))],
        out_specs=[pl.BlockSpec((tm, n), lambda i: (i, 0))], out_shape=[jax.ShapeDtypeStruct((m, n), F32)],
        inputs=[a, b])
    return outs[0]


def _rope_tables(s, n):
    t = jnp.arange(n)
    lane = jnp.arange(LANES)
    dd = lane % HEAD_DIM
    inv = ROPE_THETA ** (-(dd % ROPE_PAIRS).astype(F32) / ROPE_PAIRS)
    pos = jnp.where(dd[None, :] < HEAD_DIM // 2, (t // GRID_W)[:, None], (t % GRID_W)[:, None]).astype(F32)
    ang = pos * inv[None, :]
    live = (t < s)[:, None]
    first = ((dd % (2 * ROPE_PAIRS)) < ROPE_PAIRS)[None, :]
    cos = jnp.where(live, jnp.cos(ang), 1.0)
    sin = jnp.where(live, jnp.sin(ang), 0.0)
    sa = jnp.where(first, -sin, 0.0)
    sb = jnp.where(first, 0.0, sin)
    return cos.astype(F32), sa.astype(F32), sb.astype(F32)


def _rope(xv, cos, sa, sb):
    return (xv * cos + pltpu.roll(xv, LANES - ROPE_PAIRS, 1) * sa + pltpu.roll(xv, ROPE_PAIRS, 1) * sb)


def mix_in(x, g, mod, win8, tables, s, name, exchange=None):
    n, d = x.shape
    tm = _token_tile(n)
    nb = win8.shape[0]
    n_rope = 2 * NA_WIDTH // IN_BLOCK

    def core(x_ref, g_ref, mod_ref, w_ref, c_ref, sa_ref, sb_ref, hb_ref, qkv_ref, u_ref):
        i = pl.program_id(0)
        hb = _norm_mod(x_ref[...], g_ref[...], mod_ref, 3, 4, _is_ctx(i, tm, s))
        hb_ref[...] = hb
        cos, sa, sb = c_ref[...], sa_ref[...], sb_ref[...]
        for j in range(nb):
            y = _dot(hb, w_ref[j])
            if j < n_rope:
                for b in range(IN_BLOCK // LANES):
                    sl = slice(b * LANES, (b + 1) * LANES)
                    qkv_ref[:, j * IN_BLOCK + b * LANES:j * IN_BLOCK + (b + 1) * LANES] = (
                        _rope(y[:, sl], cos, sa, sb).astype(BF16))
            elif j < N_QKV_BLOCKS:
                qkv_ref[:, j * IN_BLOCK:(j + 1) * IN_BLOCK] = y.astype(BF16)
            else:
                u_ref[:, (j - N_QKV_BLOCKS) * IN_BLOCK:(j - N_QKV_BLOCKS + 1) * IN_BLOCK] = y

    tab = pl.BlockSpec((tm, LANES), lambda i: (i, 0))
    row = lambda w: pl.BlockSpec((tm, w), lambda i: (i, 0))
    outs, xo = carrier_call(
        core, name=name, grid=(n // tm,),
        in_specs=[row(d), _vec_spec(d), _mod_whole(d), pl.BlockSpec((nb, d, IN_BLOCK), lambda i: (0, 0, 0)),
                  tab, tab, tab],
        out_specs=[row(d), row(3 * NA_WIDTH), row(POOL_WIDTH)],
        out_shape=[jax.ShapeDtypeStruct((n, d), BF16), jax.ShapeDtypeStruct((n, 3 * NA_WIDTH), BF16),
                   jax.ShapeDtypeStruct((n, POOL_WIDTH), F32)],
        inputs=[x, g, mod, win8, *tables], exchange=exchange)
    return outs, xo


def qkv_bwd(dq, dk, dv, du, tables, name):
    n = dq.shape[0]
    w = NA_WIDTH

    def core(dq_ref, dk_ref, dv_ref, du_ref, c_ref, sa_ref, sb_ref, o_ref):
        cos, sa, sb = c_ref[...], -sa_ref[...], -sb_ref[...]
        for b in range(N_HEAD_BLOCKS):
            sl = slice(b * LANES, (b + 1) * LANES)
            o_ref[:, b * LANES:(b + 1) * LANES] = _rope(dq_ref[:, sl], cos, sa, sb).astype(BF16)
            o_ref[:, w + b * LANES:w + (b + 1) * LANES] = _rope(dk_ref[:, sl], cos, sa, sb).astype(BF16)
        o_ref[:, 2 * w:3 * w] = dv_ref[...].astype(BF16)
        o_ref[:, 3 * w:] = du_ref[...].astype(BF16)

    tab = pl.BlockSpec((TM, LANES), lambda i: (i, 0))
    tile = pl.BlockSpec((TM, w), lambda i: (i, 0))
    outs, _ = carrier_call(
        core, name=name, grid=(n // TM,), in_specs=[tile, tile, tile, tile, tab, tab, tab],
        out_specs=[pl.BlockSpec((TM, 4 * w), lambda i: (i, 0))],
        out_shape=[jax.ShapeDtypeStruct((n, 4 * w), BF16)], inputs=[dq, dk, dv, du, *tables])
    return outs[0]


def mix_out(na, py, wout, x, g, mod, s, name):
    n, d = x.shape
    tm = _token_tile(n)
    half = na.shape[1]

    def core(na_ref, py_ref, w_ref, x_ref, g_ref, mod_ref, f_ref, xo_ref):
        i = pl.program_id(0)
        ff = _dot(na_ref[...], w_ref[:half, :]) + _dot(py_ref[...], w_ref[half:, :])
        f_ref[...] = ff
        xo_ref[...] = _post(x_ref[...], ff, g_ref[...], mod_ref, 5, 1.0, _is_ctx(i, tm, s))

    tile = pl.BlockSpec((tm, d), lambda i: (i, 0))
    htile = pl.BlockSpec((tm, half), lambda i: (i, 0))
    outs, _ = carrier_call(
        core, name=name, grid=(n // tm,),
        in_specs=[htile, htile, pl.BlockSpec((2 * half, d), lambda i: (0, 0)), tile, _vec_spec(d), _mod_whole(d)],
        out_specs=[tile, tile],
        out_shape=[jax.ShapeDtypeStruct((n, d), F32), jax.ShapeDtypeStruct((n, d), F32)],
        inputs=[na, py, wout, x, g, mod])
    return outs


def grad_wout(na, py, dfm, name):
    n, half = na.shape
    d = dfm.shape[1]
    tk = _pick(n, (1408, 1024, 768, 640, 512, 256))
    nk = n // tk

    def core(na_ref, py_ref, b_ref, o_ref, acc):
        hh, kk = pl.program_id(0), pl.program_id(1)

        @pl.when(kk == 0)
        def _():
            acc[...] = jnp.zeros_like(acc)

        @pl.when(hh == 0)
        def _():
            acc[...] += _dot(na_ref[...], b_ref[...], TN)

        @pl.when(hh == 1)
        def _():
            acc[...] += _dot(py_ref[...], b_ref[...], TN)

        @pl.when(kk == nk - 1)
        def _():
            o_ref[0] = acc[...].astype(BF16)

    htile = pl.BlockSpec((tk, half), lambda hh, kk: (kk, 0))
    outs, _ = carrier_call(
        core, name=name, grid=(2, nk), in_specs=[htile, htile, pl.BlockSpec((tk, d), lambda hh, kk: (kk, 0))],
        out_specs=[pl.BlockSpec((1, half, d), lambda hh, kk: (hh, 0, 0))],
        out_shape=[jax.ShapeDtypeStruct((2, half, d), BF16)],
        scratch_shapes=[pltpu.VMEM((half, d), F32)], inputs=[na, py, dfm])
    return outs[0]


def mix_dh(dqkvu, win8, x, dxo, g, mod, s, name, exchange=None):
    n, d = x.shape
    tm = _token_tile(n)
    nb = win8.shape[0]

    def core(dq_ref, w_ref, x_ref, dxo_ref, g_ref, mod_ref, dx_ref, red_ref):
        i = pl.program_id(0)
        dh = _dot(dq_ref[:, :IN_BLOCK], w_ref[0], NT)
        for j in range(1, nb):
            dh = dh + _dot(dq_ref[:, j * IN_BLOCK:(j + 1) * IN_BLOCK], w_ref[j], NT)
        is_ctx = _is_ctx(i, tm, s)
        dx, sums = _pre_bwd(x_ref[...], dh, dxo_ref[...], g_ref[...], mod_ref, 4, is_ctx)
        dx_ref[...] = dx
        _red_add(red_ref, i == 0, is_ctx, sums)

    tile = pl.BlockSpec((tm, d), lambda i: (i, 0))
    outs, xo = carrier_call(
        core, name=name, grid=(n // tm,),
        in_specs=[pl.BlockSpec((tm, nb * IN_BLOCK), lambda i: (i, 0)),
                  pl.BlockSpec((nb, d, IN_BLOCK), lambda i: (0, 0, 0)), tile, tile, _vec_spec(d), _mod_whole(d)],
        out_specs=[tile, _red_whole(d)],
        out_shape=[jax.ShapeDtypeStruct((n, d), F32), jax.ShapeDtypeStruct((2, 8, d), F32)],
        inputs=[dqkvu, win8, x, dxo, g, mod], exchange=exchange)
    return outs, xo


def _na_consts():
    j = np.arange(GRID_W)
    col_start = np.clip(j - NA_KW // 2, 0, GRID_W - NA_KW)
    valid = (j[None, :] >= col_start[:, None]) & (j[None, :] < col_start[:, None] + NA_KW)
    dc = np.clip(j[None, :] - j[:, None] + NA_KW - 1, 0, 2 * NA_KW - 2)
    onehot = np.zeros((LANES, GRID_W, GRID_W), np.float32)
    for d in range(2 * NA_KW - 1):
        onehot[d] = ((dc == d) & valid).astype(np.float32)
    negmask = np.where(valid, 0.0, NEG_INF).astype(np.float32)
    return onehot.reshape(LANES, GRID_W * GRID_W), np.tile(negmask, (1, NA_KH))


def bias_tables(rpb, name):
    onehot, negmask = _na_consts()
    nj = 2 * NA_KH - 1
    rows = NA_HEADS * nj
    a = jnp.pad(rpb.reshape(rows, 2 * NA_KW - 1), ((0, 0), (0, LANES - (2 * NA_KW - 1))))

    def body(a_ref, e_ref, o_ref):
        o_ref[...] = _dot(a_ref[...], e_ref[...], precision=HIGHEST)

    t = pl.pallas_call(
        body, name=name, out_shape=jax.ShapeDtypeStruct((rows, GRID_W * GRID_W), F32),
        in_specs=[pl.BlockSpec(memory_space=pltpu.VMEM)] * 2,
        out_specs=pl.BlockSpec(memory_space=pltpu.VMEM),
    )(a, jnp.asarray(onehot))
    t = t.reshape(NA_HEADS, nj, GRID_W, GRID_W)
    tb = jnp.stack([t[:, j0:j0 + NA_KH] for j0 in range(NA_KH)])
    tb = tb.transpose(0, 1, 3, 2, 4).reshape(NA_KH, NA_HEADS, GRID_W, NA_KH * GRID_W)
    return tb + jnp.asarray(negmask)[None, None]


def bias_tables_bwd(dtb, name):
    onehot, _ = _na_consts()
    nj = 2 * NA_KH - 1
    d5 = dtb.reshape(NA_KH, NA_HEADS, GRID_W, NA_KH, GRID_W).transpose(0, 3, 1, 2, 4)
    d2 = d5.reshape(NA_KH * NA_KH * NA_HEADS, GRID_W * GRID_W)

    def body(d_ref, e_ref, o_ref):
        r = _dot(d_ref[...], e_ref[...], NT, precision=HIGHEST)
        for j in range(nj):
            acc = jnp.zeros((NA_HEADS, LANES), F32)
            for j0 in range(NA_KH):
                kk = j - j0
                if 0 <= kk < NA_KH:
                    base = (j0 * NA_KH + kk) * NA_HEADS
                    acc = acc + r[base:base + NA_HEADS, :]
            o_ref[j] = acc

    out = pl.pallas_call(
        body, name=name, out_shape=jax.ShapeDtypeStruct((nj, NA_HEADS, LANES), F32),
        in_specs=[pl.BlockSpec(memory_space=pltpu.VMEM)] * 2,
        out_specs=pl.BlockSpec(memory_space=pltpu.VMEM),
        compiler_params=pltpu.CompilerParams(vmem_limit_bytes=VMEM_LIMIT),
    )(d2, jnp.asarray(onehot))
    return out[:, :, :2 * NA_KW - 1].transpose(1, 0, 2)


def _head_masks():
    lane = lax.broadcasted_iota(jnp.int32, (1, LANES), 1)
    return [(lane >= h * HEAD_DIM) & (lane < (h + 1) * HEAD_DIM) for h in range(HEADS_PER_BLOCK)]


def _row_window(r, rows):
    rs = jnp.clip(r - NA_KH // 2, 0, rows - NA_KH)
    return rs - r + NA_KH - 1, pl.multiple_of(rs * GRID_W, GRID_W)


def _stack_heads(t, masks):
    return jnp.concatenate([jnp.where(mk, t, jnp.zeros_like(t)) for mk in masks], axis=0)


def _unstack_heads(t2, masks):
    out = t2[(HEADS_PER_BLOCK - 1) * GRID_W:, :]
    for h in reversed(range(HEADS_PER_BLOCK - 1)):
        out = jnp.where(masks[h], t2[h * GRID_W:(h + 1) * GRID_W, :], out)
    return out


NA_ROWS_PER_STEP = 4
NA_STEP = NA_ROWS_PER_STEP * GRID_W
SLAB = NA_KH * GRID_W
K_COL = N_HEAD_BLOCKS
V_COL = 2 * N_HEAD_BLOCKS


def na_fwd(qkv, tb, s, name, exchange=None):
    n = qkv.shape[0]
    l = n - s
    rows = s // GRID_W
    rr = NA_ROWS_PER_STEP
    x_steps = rows // rr

    def core(q_ref, k_ref, v_ref, kc_ref, vc_ref, tb_ref, o_ref, lse_ref):
        rb = pl.program_id(1)

        @pl.when(rb >= x_steps)
        def _():
            o_ref[...] = jnp.zeros_like(o_ref)
            lse_ref[...] = jnp.zeros_like(lse_ref)

        @pl.when(rb < x_steps)
        def _():
            masks = _head_masks()
            kcb, vcb = kc_ref[...], vc_ref[...]
            wins, scores = [], []
            for t in range(rr):
                j0, off = _row_window(rb * rr + t, rows)
                q2 = _stack_heads(q_ref[t * GRID_W:(t + 1) * GRID_W, :] * ATT_SCALE, masks)
                bias = tb_ref[j0].reshape(HEADS_PER_BLOCK * GRID_W, SLAB)
                s_loc = _dot(q2, k_ref[pl.ds(off, SLAB), :], NT) + bias
                s_ctx = _dot(q2, kcb, NT)
                wins.append(off)
                scores.append((s_loc, s_ctx))
            probs = []
            for s_loc, s_ctx in scores:
                m = jnp.maximum(jnp.max(s_loc, axis=-1, keepdims=True), jnp.max(s_ctx, axis=-1, keepdims=True))
                p_loc = jnp.exp(s_loc - m)
                p_ctx = jnp.exp(s_ctx - m)
                den = jnp.sum(p_loc, axis=-1, keepdims=True) + jnp.sum(p_ctx, axis=-1, keepdims=True)
                probs.append((p_loc.astype(BF16), p_ctx.astype(BF16), den, m + jnp.log(den)))
            for t, (p_loc, p_ctx, den, lse2) in enumerate(probs):
                o2 = (_dot(p_loc, v_ref[pl.ds(wins[t], SLAB), :]) + _dot(p_ctx, vcb)) / den
                o_ref[t * GRID_W:(t + 1) * GRID_W, :] = _unstack_heads(o2, masks).astype(BF16)
                lse_ref[0, t * GRID_W:(t + 1) * GRID_W, :] = _unstack_heads(lse2, masks)

    cb = s // l
    outs, xo = carrier_call(
        core, name=name, grid=(N_HEAD_BLOCKS, n // NA_STEP),
        in_specs=[pl.BlockSpec((NA_STEP, LANES), lambda hb, rb: (jnp.minimum(rb, x_steps - 1), hb)),
                  pl.BlockSpec((s, LANES), lambda hb, rb: (0, K_COL + hb)),
                  pl.BlockSpec((s, LANES), lambda hb, rb: (0, V_COL + hb)),
                  pl.BlockSpec((l, LANES), lambda hb, rb: (cb, K_COL + hb)),
                  pl.BlockSpec((l, LANES), lambda hb, rb: (cb, V_COL + hb)),
                  pl.BlockSpec((NA_KH, HEADS_PER_BLOCK, GRID_W, SLAB), lambda hb, rb: (0, hb, 0, 0))],
        out_specs=[pl.BlockSpec((NA_STEP, LANES), lambda hb, rb: (rb, hb)),
                   pl.BlockSpec((1, NA_STEP, LANES), lambda hb, rb: (hb, rb, 0))],
        out_shape=[jax.ShapeDtypeStruct((n, NA_WIDTH), BF16), jax.ShapeDtypeStruct((N_HEAD_BLOCKS, n, LANES), F32)],
        inputs=[qkv, qkv, qkv, qkv, qkv, tb], exchange=exchange)
    return outs, xo


def na_bwd(qkv, tb, o, dmix, lse, s, name, exchange=None):
    n = qkv.shape[0]
    l = n - s
    rows = s // GRID_W
    rr = NA_ROWS_PER_STEP
    x_steps = rows // rr

    def core(q_ref, k_ref, v_ref, kc_ref, vc_ref, tb_ref, o_ref, do_ref, lse_ref, dq_ref, dk_ref, dv_ref, dtb_ref):
        rb = pl.program_id(1)

        @pl.when(rb == 0)
        def _():
            dk_ref[...] = jnp.zeros_like(dk_ref)
            dv_ref[...] = jnp.zeros_like(dv_ref)
            dtb_ref[...] = jnp.zeros_like(dtb_ref)

        @pl.when(rb >= x_steps)
        def _():
            dq_ref[...] = jnp.zeros_like(dq_ref)

        @pl.when(rb < x_steps)
        def _():
            masks = _head_masks()
            kcb, vcb = kc_ref[...], vc_ref[...]
            stage1 = []
            for t in range(rr):
                j0, off = _row_window(rb * rr + t, rows)
                sl = slice(t * GRID_W, (t + 1) * GRID_W)
                q2 = _stack_heads(q_ref[sl, :] * ATT_SCALE, masks)
                do_f = do_ref[sl, :]
                do2 = _stack_heads(do_f.astype(BF16), masks)
                dd = do_f * o_ref[sl, :].astype(F32)
                delta2 = jnp.concatenate(
                    [jnp.sum(jnp.where(mk, dd, 0.0), axis=-1, keepdims=True) for mk in masks], axis=0)
                lse_t = lse_ref[0, sl, :]
                lse2 = jnp.concatenate(
                    [lse_t[:, h * HEAD_DIM:h * HEAD_DIM + 1] for h in range(HEADS_PER_BLOCK)], axis=0)
                kslab = k_ref[pl.ds(off, SLAB), :]
                vslab = v_ref[pl.ds(off, SLAB), :]
                bias = tb_ref[j0].reshape(HEADS_PER_BLOCK * GRID_W, SLAB)
                s_loc = _dot(q2, kslab, NT) + bias - lse2
                s_ctx = _dot(q2, kcb, NT) - lse2
                dp_loc = _dot(do2, vslab, NT) - delta2
                dp_ctx = _dot(do2, vcb, NT) - delta2
                stage1.append((j0, off, q2, do2, s_loc, s_ctx, dp_loc, dp_ctx))
            stage2 = []
            for j0, off, q2, do2, s_loc, s_ctx, dp_loc, dp_ctx in stage1:
                p_loc = jnp.exp(s_loc)
                p_ctx = jnp.exp(s_ctx)
                ds_loc = p_loc * dp_loc
                dtb_ref[j0] += ds_loc.reshape(HEADS_PER_BLOCK, GRID_W, SLAB)
                stage2.append((off, q2, do2, p_loc.astype(BF16), p_ctx.astype(BF16), ds_loc.astype(BF16),
                               (p_ctx * dp_ctx).astype(BF16)))
            for t, (off, q2, do2, p_loc, p_ctx, ds_loc, ds_ctx) in enumerate(stage2):
                dq2 = (_dot(ds_loc, k_ref[pl.ds(off, SLAB), :]) + _dot(ds_ctx, kcb)) * ATT_SCALE
                dq_ref[t * GRID_W:(t + 1) * GRID_W, :] = _unstack_heads(dq2, masks)
                dk_ref[pl.ds(off, SLAB), :] += _dot(ds_loc, q2, TN)
                dv_ref[pl.ds(off, SLAB), :] += _dot(p_loc, do2, TN)
                dk_ref[s:, :] += _dot(ds_ctx, q2, TN)
                dv_ref[s:, :] += _dot(p_ctx, do2, TN)

    cb = s // l
    clamp = lambda hb, rb: (jnp.minimum(rb, x_steps - 1), hb)
    tile_in = pl.BlockSpec((NA_STEP, LANES), clamp)
    whole_out = pl.BlockSpec((n, LANES), lambda hb, rb: (0, hb))
    tbs = pl.BlockSpec((NA_KH, HEADS_PER_BLOCK, GRID_W, SLAB), lambda hb, rb: (0, hb, 0, 0))
    f32n = jax.ShapeDtypeStruct((n, NA_WIDTH), F32)
    outs, xo = carrier_call(
        core, name=name, grid=(N_HEAD_BLOCKS, n // NA_STEP),
        in_specs=[tile_in,
                  pl.BlockSpec((s, LANES), lambda hb, rb: (0, K_COL + hb)),
                  pl.BlockSpec((s, LANES), lambda hb, rb: (0, V_COL + hb)),
                  pl.BlockSpec((l, LANES), lambda hb, rb: (cb, K_COL + hb)),
                  pl.BlockSpec((l, LANES), lambda hb, rb: (cb, V_COL + hb)),
                  tbs, tile_in, tile_in,
                  pl.BlockSpec((1, NA_STEP, LANES), lambda hb, rb: (hb, jnp.minimum(rb, x_steps - 1), 0))],
        out_specs=[pl.BlockSpec((NA_STEP, LANES), lambda hb, rb: (rb, hb)), whole_out, whole_out, tbs],
        out_shape=[f32n, f32n, f32n, jax.ShapeDtypeStruct((NA_KH, NA_HEADS, GRID_W, SLAB), F32)],
        inputs=[qkv, qkv, qkv, qkv, qkv, tb, o, dmix, lse], exchange=exchange)
    return outs, xo


def ctx_attn_fwd(qkv, na, s, name):
    n = qkv.shape[0]
    l = n - s
    cb = s // l

    def core(q_ref, k_ref, v_ref, na_in, o_ref, lse_ref):
        masks = _head_masks()
        qt, kb, vb = q_ref[...], k_ref[...], v_ref[...]
        o_acc = jnp.zeros((l, LANES), F32)
        lse_acc = jnp.zeros((l, LANES), F32)
        for h in range(HEADS_PER_BLOCK):
            qh = jnp.where(masks[h], qt, jnp.zeros_like(qt))
            sc = _dot(qh, kb, NT) * ATT_SCALE
            m = jnp.max(sc, axis=-1, keepdims=True)
            p = jnp.exp(sc - m)
            den = jnp.sum(p, axis=-1, keepdims=True)
            o_acc = jnp.where(masks[h], _dot(p.astype(BF16), vb) / den, o_acc)
            lse_acc = jnp.where(masks[h], m + jnp.log(den), lse_acc)
        o_ref[...] = o_acc.astype(BF16)
        lse_ref[0] = lse_acc

    outs, _ = carrier_call(
        core, name=name, grid=(N_HEAD_BLOCKS,),
        in_specs=[pl.BlockSpec((l, LANES), lambda hb: (cb, hb)), pl.BlockSpec((l, LANES), lambda hb: (cb, K_COL + hb)),
                  pl.BlockSpec((l, LANES), lambda hb: (cb, V_COL + hb)), ANY],
        out_specs=[pl.BlockSpec((l, LANES), lambda hb: (cb, hb)), pl.BlockSpec((1, l, LANES), lambda hb: (hb, 0, 0))],
        out_shape=[jax.ShapeDtypeStruct(na.shape, BF16), jax.ShapeDtypeStruct((N_HEAD_BLOCKS, l, LANES), F32)],
        inputs=[qkv, qkv, qkv, na], aliases={3: 0})
    return outs


def ctx_attn_bwd(qkv, na, dmix, lse, dq, dk, dv, s, name):
    n = qkv.shape[0]
    l = n - s
    cb = s // l

    def core(q_ref, k_ref, v_ref, o_ref, do_ref, lse_ref, dq_in, dk_in, dv_in, dq_ref, dk_ref, dv_ref):
        masks = _head_masks()
        qt, kb, vb = q_ref[...], k_ref[...], v_ref[...]
        do_f = do_ref[...]
        dd = do_f * o_ref[...].astype(F32)
        do_b = do_f.astype(BF16)
        lse_t = lse_ref[0]
        dq_acc = jnp.zeros((l, LANES), F32)
        dk_acc = jnp.zeros((l, LANES), F32)
        dv_acc = jnp.zeros((l, LANES), F32)
        for h in range(HEADS_PER_BLOCK):
            qh = jnp.where(masks[h], qt, jnp.zeros_like(qt))
            doh = jnp.where(masks[h], do_b, jnp.zeros_like(do_b))
            delta = jnp.sum(jnp.where(masks[h], dd, 0.0), axis=-1, keepdims=True)
            p = jnp.exp(_dot(qh, kb, NT) * ATT_SCALE - lse_t[:, h * HEAD_DIM:h * HEAD_DIM + 1])
            ds = (p * (_dot(doh, vb, NT) - delta)).astype(BF16)
            dq_acc = jnp.where(masks[h], _dot(ds, kb) * ATT_SCALE, dq_acc)
            dk_acc = dk_acc + _dot(ds, qh, TN)
            dv_acc = dv_acc + _dot(p.astype(BF16), doh, TN)
        dq_ref[...] = dq_acc
        dk_ref[...] = dk_in[...] + dk_acc * ATT_SCALE
        dv_ref[...] = dv_in[...] + dv_acc

    blk = pl.BlockSpec((l, LANES), lambda hb: (cb, hb))
    f32n = jax.ShapeDtypeStruct((n, NA_WIDTH), F32)
    outs, _ = carrier_call(
        core, name=name, grid=(N_HEAD_BLOCKS,),
        in_specs=[blk, pl.BlockSpec((l, LANES), lambda hb: (cb, K_COL + hb)),
                  pl.BlockSpec((l, LANES), lambda hb: (cb, V_COL + hb)), blk, blk,
                  pl.BlockSpec((1, l, LANES), lambda hb: (hb, 0, 0)), ANY, blk, blk],
        out_specs=[blk, blk, blk], out_shape=[f32n, f32n, f32n],
        inputs=[qkv, qkv, qkv, na, dmix, lse, dq, dk, dv], aliases={6: 0, 7: 1, 8: 2})
    return outs


def _pool_consts(l):
    assert l == TM
    mem = np.zeros((2, POOL_GROUPS, TM, TM), np.float32)
    inv = np.zeros((2, POOL_GROUPS, TM, LANES), np.float32)
    for which, length in ((0, GRID_W), (1, l)):
        t = np.arange(length)
        for g, w in enumerate(POOL_WINDOWS):
            lo = np.clip(t - w // 2, 0, length)
            hi = np.clip(t - w // 2 + w, 0, length)
            blockm = ((t[None, :] >= lo[:, None]) & (t[None, :] < hi[:, None])).astype(np.float32)
            cnt = (hi - lo).astype(np.float32)
            for b in range(TM // length):
                mem[which, g, b * length:(b + 1) * length, b * length:(b + 1) * length] = blockm
                inv[which, g, b * length:(b + 1) * length, :] = (1.0 / cnt)[:, None]
    return mem, np.ascontiguousarray(mem.transpose(0, 1, 3, 2)), inv


def _split_dot(m01, val):
    hi = val.astype(BF16)
    lo = (val - hi.astype(F32)).astype(BF16)
    return _dot(m01, hi) + _dot(m01, lo)


def pool_fwd(u, mem, inv, wp, scale, nx_tiles, name):
    n = u.shape[0]

    def core(u_ref, m_ref, i_ref, wp_ref, s_ref, o_ref):
        for g in range(POOL_GROUPS):
            sl = slice(g * POOL_CH, (g + 1) * POOL_CH)
            ug = u_ref[:, sl]
            dg = _split_dot(m_ref[0, g], ug) * i_ref[0, g] - ug
            o_ref[:, sl] = (_dot(dg.astype(BF16), wp_ref[g]) * s_ref[:, sl]).astype(BF16)

    grp = lambda i: (i // nx_tiles, 0, 0, 0)
    outs, _ = carrier_call(
        core, name=name, grid=(n // TM,),
        in_specs=[pl.BlockSpec((TM, POOL_WIDTH), lambda i: (i, 0)),
                  pl.BlockSpec((1, POOL_GROUPS, TM, TM), grp),
                  pl.BlockSpec((1, POOL_GROUPS, TM, LANES), grp),
                  pl.BlockSpec((POOL_GROUPS, POOL_CH, POOL_CH), lambda i: (0, 0, 0)),
                  pl.BlockSpec((1, POOL_WIDTH), lambda i: (0, 0))],
        out_specs=[pl.BlockSpec((TM, POOL_WIDTH), lambda i: (i, 0))],
        out_shape=[jax.ShapeDtypeStruct((n, POOL_WIDTH), BF16)], inputs=[u, mem, inv, wp, scale])
    return outs[0]


def pool_bwd(dmix, u, mem, mem_t, inv, wp, scale, nx_tiles, name):
    n = u.shape[0]

    def core(dy_ref, u_ref, m_ref, mt_ref, i_ref, wp_ref, s_ref, du_ref, dwp_ref, dsc_ref):
        @pl.when(pl.program_id(0) == 0)
        def _():
            dwp_ref[...] = jnp.zeros_like(dwp_ref)
            dsc_ref[...] = jnp.zeros_like(dsc_ref)

        for g in range(POOL_GROUPS):
            sl = slice(g * POOL_CH, (g + 1) * POOL_CH)
            ug = u_ref[:, sl]
            dy = dy_ref[:, sl]
            dg = (_split_dot(m_ref[0, g], ug) * i_ref[0, g] - ug).astype(BF16)
            z = _dot(dg, wp_ref[g])
            dsc_ref[0:1, sl] += jnp.sum(dy * z, axis=0, keepdims=True)
            dz = (dy * s_ref[:, sl]).astype(BF16)
            dwp_ref[g] += _dot(dg, dz, TN)
            dd = _dot(dz, wp_ref[g], NT)
            du_ref[:, sl] = _split_dot(mt_ref[0, g], dd * i_ref[0, g]) - dd

    grp = lambda i: (i // nx_tiles, 0, 0, 0)
    outs, _ = carrier_call(
        core, name=name, grid=(n // TM,),
        in_specs=[pl.BlockSpec((TM, POOL_WIDTH), lambda i: (i, 1)),
                  pl.BlockSpec((TM, POOL_WIDTH), lambda i: (i, 0)),
                  pl.BlockSpec((1, POOL_GROUPS, TM, TM), grp),
                  pl.BlockSpec((1, POOL_GROUPS, TM, TM), grp),
                  pl.BlockSpec((1, POOL_GROUPS, TM, LANES), grp),
                  pl.BlockSpec((POOL_GROUPS, POOL_CH, POOL_CH), lambda i: (0, 0, 0)),
                  pl.BlockSpec((1, POOL_WIDTH), lambda i: (0, 0))],
        out_specs=[pl.BlockSpec((TM, POOL_WIDTH), lambda i: (i, 0)),
                   pl.BlockSpec((POOL_GROUPS, POOL_CH, POOL_CH), lambda i: (0, 0, 0)),
                   pl.BlockSpec((8, POOL_WIDTH), lambda i: (0, 0))],
        out_shape=[jax.ShapeDtypeStruct((n, POOL_WIDTH), F32),
                   jax.ShapeDtypeStruct((POOL_GROUPS, POOL_CH, POOL_CH), F32),
                   jax.ShapeDtypeStruct((8, POOL_WIDTH), F32)],
        inputs=[dmix, u, mem, mem_t, inv, wp, scale])
    return outs


MOD_ROWS = 16


def mod_fwd(cvecs, w, b, name):
    _, d = cvecs.shape
    cl = w.shape[2]
    tc = _pick(cl, (384, 128))

    def core(c_ref, w_ref, b_ref, o_ref):
        a = _silu(c_ref[...]).astype(BF16)
        o_ref[0] = _dot(a, w_ref[0].astype(BF16)) + b_ref[0]

    outs, _ = carrier_call(
        core, name=name, grid=(DEPTH, cl // tc),
        in_specs=[pl.BlockSpec((MOD_ROWS, d), lambda li, j: (0, 0)),
                  pl.BlockSpec((1, d, tc), lambda li, j: (li, 0, j)),
                  pl.BlockSpec((1, 1, tc), lambda li, j: (li, 0, j))],
        out_specs=[pl.BlockSpec((1, MOD_ROWS, tc), lambda li, j: (li, 0, j))],
        out_shape=[jax.ShapeDtypeStruct((DEPTH, MOD_ROWS, cl), F32)], inputs=[cvecs, w, b])
    return outs[0]


def mod_bwd(cvecs, dm, w, name):
    _, d = cvecs.shape
    cl = w.shape[2]
    tc = _pick(cl, (384, 128))

    def core(c_ref, dm_ref, w_ref, dw_ref, da_ref):
        @pl.when((pl.program_id(0) == 0) & (pl.program_id(1) == 0))
        def _():
            da_ref[...] = jnp.zeros_like(da_ref)

        a = _silu(c_ref[...]).astype(BF16)
        dmb = dm_ref[0].astype(BF16)
        dw_ref[0] = _dot(a, dmb, TN)
        da_ref[...] += _dot(dmb, w_ref[0].astype(BF16), NT)

    outs, _ = carrier_call(
        core, name=name, grid=(DEPTH, cl // tc),
        in_specs=[pl.BlockSpec((MOD_ROWS, d), lambda li, j: (0, 0)),
                  pl.BlockSpec((1, MOD_ROWS, tc), lambda li, j: (li, 0, j)),
                  pl.BlockSpec((1, d, tc), lambda li, j: (li, 0, j))],
        out_specs=[pl.BlockSpec((1, d, tc), lambda li, j: (li, 0, j)),
                   pl.BlockSpec((MOD_ROWS, d), lambda li, j: (0, 0))],
        out_shape=[jax.ShapeDtypeStruct((DEPTH, d, cl), F32), jax.ShapeDtypeStruct((MOD_ROWS, d), F32)],
        inputs=[cvecs, dm, w])
    return outs


def loss_head(y, target, name):
    n, d = y.shape
    s = target.shape[0]
    nt, nx = n // TM, s // TM

    def core(y_ref, t_ref, l_ref, dy_ref, acc_ref):
        i = pl.program_id(0)

        @pl.when(i == 0)
        def _():
            acc_ref[...] = jnp.zeros_like(acc_ref)

        @pl.when(i < nx)
        def _():
            e = y_ref[...] - t_ref[...]
            dy_ref[...] = e * (1.0 / d)
            acc_ref[...] += jnp.sum(e * e, axis=0, keepdims=True)

        @pl.when(i >= nx)
        def _():
            dy_ref[...] = jnp.zeros_like(dy_ref)

        @pl.when(i == nt - 1)
        def _():
            l_ref[...] = jnp.sum(acc_ref[...], axis=1, keepdims=True) * (0.5 / d)

    tile = pl.BlockSpec((TM, d), lambda i: (i, 0))
    outs, _ = carrier_call(
        core, name=name, grid=(nt,),
        in_specs=[tile, pl.BlockSpec((TM, d), lambda i: (jnp.minimum(i, nx - 1), 0))],
        out_specs=[pl.BlockSpec((1, 1), lambda i: (0, 0)), tile],
        out_shape=[jax.ShapeDtypeStruct((1, 1), F32), jax.ShapeDtypeStruct((n, d), F32)],
        scratch_shapes=[pltpu.VMEM((1, d), F32)], inputs=[y, target])
    return outs


def sum_devices(v, name):
    _, r, c = v.shape
    tr = _pick(r, (64, 8))

    def core(v_ref, o_ref):
        acc = v_ref[0]
        for p in range(1, N_DEV):
            acc = acc + v_ref[p]
        o_ref[...] = acc

    outs, _ = carrier_call(
        core, name=name, grid=(r // tr,), in_specs=[pl.BlockSpec((N_DEV, tr, c), lambda i: (0, i, 0))],
        out_specs=[pl.BlockSpec((tr, c), lambda i: (i, 0))], out_shape=[jax.ShapeDtypeStruct((r, c), F32)],
        inputs=[v])
    return outs[0]


def cctx_grad(parts, c_ctx, name):
    d = c_ctx.shape[1]

    def body(p_ref, c_ref, o_ref):
        acc = p_ref[0]
        for p in range(1, N_DEV):
            acc = acc + p_ref[p]
        o_ref[...] = acc[8:9, :] * _dsilu(c_ref[...])

    return pl.pallas_call(
        body, name=name, out_shape=jax.ShapeDtypeStruct((1, d), F32),
        in_specs=[pl.BlockSpec(memory_space=pltpu.VMEM)] * 2,
        out_specs=pl.BlockSpec(memory_space=pltpu.VMEM),
    )(parts, c_ctx)


def _adam_math(w, g, m, v):
    m2 = ADAM_B1 * m + (1.0 - ADAM_B1) * g
    v2 = ADAM_B2 * v + (1.0 - ADAM_B2) * (g * g)
    m_hat = m2 / (1.0 - ADAM_B1 ** ADAM_STEP)
    v_hat = v2 / (1.0 - ADAM_B2 ** ADAM_STEP)
    delta = -ADAM_LR * (m_hat / (jnp.sqrt(v_hat) + ADAM_EPS) + ADAM_WD * w)
    return delta, m2, v2


def adamw(w, g, m, v, name):
    r, c = w.shape
    tr = _pick(r, (256, 128, 64, 32, 16, 8, r))

    def core(w_ref, g_ref, m_ref, v_ref, d_ref, m2_ref, v2_ref):
        d_ref[...], m2_ref[...], v2_ref[...] = _adam_math(w_ref[...], g_ref[...], m_ref[...], v_ref[...])

    tile = pl.BlockSpec((tr, c), lambda i: (i, 0))
    out = jax.ShapeDtypeStruct((r, c), F32)
    outs, _ = carrier_call(core, name=name, grid=(r // tr,), in_specs=[tile] * 4, out_specs=[tile] * 3,
                           out_shape=[out] * 3, inputs=[w, g, m, v])
    return outs


def reduce_adamw(recv, w, m, v, name):
    r, c = w.shape
    tr = _pick(r, (256, 128, 64, 8))

    def core(recv_ref, w_ref, m_ref, v_ref, g_ref, d_ref, m2_ref, v2_ref):
        acc = recv_ref[0].astype(F32)
        for p in range(1, N_DEV):
            acc = acc + recv_ref[p].astype(F32)
        g_ref[...] = acc
        d_ref[...], m2_ref[...], v2_ref[...] = _adam_math(w_ref[...], acc, m_ref[...], v_ref[...])

    tile = pl.BlockSpec((tr, c), lambda i: (i, 0))
    out = jax.ShapeDtypeStruct((r, c), F32)
    outs, _ = carrier_call(
        core, name=name, grid=(r // tr,),
        in_specs=[pl.BlockSpec((N_DEV, tr, c), lambda i: (0, i, 0)), tile, tile, tile],
        out_specs=[tile] * 4, out_shape=[out] * 4, inputs=[recv, w, m, v])
    return outs


def kernel(x, c, ctx, c_ctx, w_mod, b_mod, norm_g, w_ffn_gate_up, w_ffn_down, w_in, w_out, na_rpb, w_pool, pool_scale, loss_target, m_c_ctx, m_w_mod, m_b_mod, m_norm_g, m_w_ffn_gate_up, m_w_ffn_down, m_w_in, m_w_out, m_na_rpb, m_w_pool, m_pool_scale, v_c_ctx, v_w_mod, v_b_mod, v_norm_g, v_w_ffn_gate_up, v_w_ffn_down, v_w_in, v_w_out, v_na_rpb, v_w_pool, v_pool_scale):
    s, d = x.shape[1], x.shape[2]
    l = ctx.shape[1]
    n = s + l
    nx = s // TM
    fq = w_ffn_gate_up.shape[-1]
    fr = w_ffn_down.shape[2]
    cl = w_mod.shape[2]
    dl = norm_g.shape[2]
    me = 4 * lax.axis_index("x") + 2 * lax.axis_index("y") + lax.axis_index("c")

    c_all = all_gather(c, "gather_c").reshape(N_DEV, d)
    cvecs = jnp.concatenate([c_all, c_ctx[None, :], jnp.zeros((MOD_ROWS - N_DEV - 1, d), F32)], axis=0)
    b_loc = lax.dynamic_slice(b_mod, (0, me * cl), (DEPTH, cl)).reshape(DEPTH, 1, cl)
    mod_loc = mod_fwd(cvecs, w_mod, b_loc, "mod_fwd")
    mod_all = all_gather(mod_loc.reshape(DEPTH * MOD_ROWS, cl), "gather_mod")
    mod_all = mod_all.reshape(N_DEV, DEPTH, MOD_ROWS, cl).transpose(1, 2, 0, 3).reshape(DEPTH, MOD_ROWS, N_DEV * cl)
    mine = lax.dynamic_slice(mod_all, (0, me, 0), (DEPTH, 1, N_DEV * cl))
    mods = jnp.concatenate([mine, mod_all[:, N_DEV:N_DEV + 1]], axis=1).reshape(DEPTH, 2, N_MOD, d)

    gu_b = w_ffn_gate_up.astype(BF16)
    dn_b = w_ffn_down.astype(BF16)
    wi_b = w_in.astype(BF16)
    wo_b = w_out.astype(BF16)
    wp_b = w_pool.astype(BF16)

    def ffn_shards(li, i):
        return [gu_b[li, i], dn_b[li, i]]

    def mix_shards(li):
        return [wi_b[li], wo_b[li]]

    def as_ffn_weights(gathered):
        return gathered[0].reshape(2, 4, d, fq), gathered[1].reshape(4, 2 * fr, d)

    def as_mix_weights(gathered):
        return gathered[0], gathered[1].reshape(N_DEV * wo_b.shape[1], d)

    tables = _rope_tables(s, n)
    mem_np, mem_t_np, inv_np = _pool_consts(l)
    mem, mem_t, inv = jnp.asarray(mem_np, BF16), jnp.asarray(mem_t_np, BF16), jnp.asarray(inv_np)
    first = gather_two_level([norm_g.reshape(DEPTH * 6, dl), gu_b[0, 0]], "gather_first")
    g_full = first[0].reshape(N_DEV, DEPTH, 6, dl).transpose(1, 2, 0, 3).reshape(DEPTH, 6, 1, N_DEV * dl)

    weights = {("ffn", 0, 0): (first[1].reshape(2, 4, d, fq), None)}
    saved = {}
    xcur = jnp.concatenate([x[0], ctx[0]], axis=0)
    for li in range(DEPTH):
        last = li == DEPTH - 1
        for i in range(2):
            tag = f"l{li}_ffn{i}"
            wgu, wd4 = weights[("ffn", li, i)]
            if i == 0:
                ex_up = Exchange(gathers=mix_shards(li) + ([dn_b[0, 0]] if wd4 is None else []))
                ex_dn = Exchange(gathers=[dn_b[li, 1]])
            elif not last:
                ex_up, ex_dn = Exchange(gathers=[gu_b[li + 1, 0]]), Exchange(gathers=[dn_b[li + 1, 0]])
            else:
                ex_up = ex_dn = None
            (hb, gu, a4), got_up = ffn_up(xcur, g_full[li, 4 * i], mods[li], wgu, s, 6 * i, tag + "_up", ex_up)
            if wd4 is None:
                wd4 = got_up.pop().reshape(4, 2 * fr, d)
                weights[("ffn", li, i)] = (wgu, wd4)
            (ff, xnext), got_dn = ffn_down(a4, wd4, xcur, g_full[li, 4 * i + 1], mods[li], s, 6 * i + 2, tag + "_down", ex_dn)
            saved[("ffn", li, i)] = (xcur, hb, gu, a4, ff)
            xcur = xnext
            if i == 0:
                weights[("mix", li)] = as_mix_weights(got_up)
                next_dn = got_dn
            elif not last:
                weights[("ffn", li + 1, 0)] = as_ffn_weights(got_up + got_dn)
            if i == 0:
                tag = f"l{li}_mix"
                win8, wout = weights[("mix", li)]
                (hb, qkv, u), _ = mix_in(xcur, g_full[li, 2], mods[li], win8, tables, s, tag + "_in")
                tb = bias_tables(na_rpb[li], tag + "_bias")
                (na, lse), got = na_fwd(qkv, tb, s, tag + "_na", Exchange(gathers=[gu_b[li, 1]]))
                weights[("ffn", li, 1)] = as_ffn_weights(got + next_dn)
                lse_c = None
                if not last:
                    na, lse_c = ctx_attn_fwd(qkv, na, s, tag + "_ctx_attn")
                py = pool_fwd(u, mem, inv, wp_b[li], pool_scale[li][None, :], nx, tag + "_pool")
                fm, xnext = mix_out(na, py, wout, xcur, g_full[li, 3], mods[li], s, tag + "_out")
                saved[("mix", li)] = (xcur, hb, qkv, u, tb, na, lse, lse_c, py, fm)
                xcur = xnext

    loss_local, dcur = loss_head(xcur, loss_target[0], "loss")
    loss = lax.psum(loss_local[0, 0], ("x", "y", "c"))

    recv = {"gu": lax.empty((N_DEV, 2 * DEPTH, d, fq), BF16), "dn": lax.empty((N_DEV, 2 * DEPTH, fr, d), BF16),
            "wi": lax.empty((N_DEV, DEPTH, d, IN_BLOCK), BF16), "wo": lax.empty((N_DEV, DEPTH, wo_b.shape[1], d), BF16)}
    pending = []

    def take(keys, gathers=()):
        nonlocal pending
        jobs = [(gr, recv[key], st) for key, gr, st in pending if key in keys]
        order = [key for key, _, _ in pending if key in keys]
        pending = [p for p in pending if p[0] not in keys]
        return Exchange(gathers=gathers, a2as=jobs), order

    def pad8(t):
        t = t.reshape(-1, d) if t.size % d == 0 else jnp.pad(t.reshape(-1), (0, -t.size % d)).reshape(-1, d)
        return jnp.pad(t, ((0, -t.shape[0] % 8), (0, 0)))

    def packed(parts):
        parts = [pad8(p) for p in parts]
        offs = np.cumsum([0] + [p.shape[0] for p in parts])
        return jnp.concatenate(parts + [jnp.zeros((-offs[-1] % 64, d), F32)], axis=0), offs

    def put(order, bufs):
        for key, buf in zip(order, bufs):
            recv[key] = buf

    d_rpb, d_wp, d_ps, d_mod, d_g = [], [], [], [], []
    for li in reversed(range(DEPTH)):
        reds = {}
        for i in (1, 0):
            tag = f"l{li}_ffn{i}"
            xin, hb, gu, a4, ff = saved[("ffn", li, i)]
            wgu, wd4 = weights[("ffn", li, i)]
            dff, red1 = post_bwd(ff, dcur, g_full[li, 4 * i + 1], mods[li], 6 * i + 2, 0.5, s, tag + "_post_bwd")
            early = []
            if (li, i) == (0, 0):
                early_small, early_offs = packed([jnp.stack(d_wp), jnp.stack(d_ps), jnp.stack(d_rpb)])
                early = [early_small]
            ex, _ = take((), early)
            g_dn, bufs = grad_weight(a4, dff, tag + "_dwdown", a_lead=4, per_step=2, exchange=ex)
            if early:
                early_sum = sum_devices(bufs[0], "sum_early_small_grads")
            pending += [("dn", g_dn.reshape(N_DEV, fr, d), 2 * li + i)]
            ex, order = take(("dn",))
            dgu, bufs = ffn_da(dff, wd4, gu, tag + "_da", ex)
            put(order, bufs)
            ex, order = take(("wi", "wo"))
            g_gu, bufs = grad_weight(hb, dgu.reshape(N_DEV, n, fq), tag + "_dwgu", b_lead=N_DEV, per_step=2, exchange=ex)
            put(order, bufs)
            pending += [("gu", g_gu, 2 * li + i)]
            ex, order = take(("gu",))
            (dcur, red2), bufs = ffn_dh(dgu, wgu, xin, dcur, g_full[li, 4 * i], mods[li], s, 6 * i, tag + "_dh", ex)
            put(order, bufs)
            reds[i] = (red1, red2)
            if i == 1:
                tag = f"l{li}_mix"
                xin, hb, qkv, u, tb, na, lse, lse_c, py, fm = saved[("mix", li)]
                win8, wout = weights[("mix", li)]
                dfm, redm1 = post_bwd(fm, dcur, g_full[li, 3], mods[li], 5, 1.0, s, tag + "_post_bwd")
                dmix = matmul_nt(dfm, wout, tag + "_dmix")
                g_wo = grad_wout(na, py, dfm, tag + "_dwout").reshape(N_DEV, wo_b.shape[1], d)
                du, gwp, gps = pool_bwd(dmix, u, mem, mem_t, inv, wp_b[li], pool_scale[li][None, :], nx, tag + "_pool_bwd")
                ex, order = take(("gu", "dn"))
                (dq, dk, dv, dtb), bufs = na_bwd(qkv, tb, na, dmix, lse, s, tag + "_na_bwd", ex)
                put(order, bufs)
                if li != DEPTH - 1:
                    dq, dk, dv = ctx_attn_bwd(qkv, na, dmix, lse_c, dq, dk, dv, s, tag + "_ctx_attn_bwd")
                grpb = bias_tables_bwd(dtb, tag + "_bias_bwd")
                dqkvu = qkv_bwd(dq, dk, dv, du, tables, tag + "_rope_bwd")
                (dcur, redm2), _ = mix_dh(dqkvu, win8, xin, dcur, g_full[li, 2], mods[li], s, tag + "_dh")
                g_wi, _ = grad_weight(hb, dqkvu, tag + "_dwin", b_cols=IN_BLOCK, per_step=4)
                pending += [("wi", g_wi, li), ("wo", g_wo, li)]
                d_rpb.insert(0, grpb)
                d_wp.insert(0, gwp)
                d_ps.insert(0, gps[0])
        (ra1, ra2), (rb1, rb2) = reds[0], reds[1]
        d_mod.insert(0, jnp.stack([ra2[:, 0], ra2[:, 1], ra1[:, 0], redm2[:, 0], redm2[:, 1], redm1[:, 0],
                                   rb2[:, 0], rb2[:, 1], rb1[:, 0]], axis=1))
        d_g.insert(0, jnp.stack([t[0] + t[1] for t in (ra2[:, 2], ra1[:, 1], redm2[:, 2], redm1[:, 1], rb2[:, 2], rb1[:, 1])]))
    grad_x = dcur[:s][None]

    small, offs = packed([jnp.stack(d_mod), jnp.stack(d_g)])
    ex, order = take(("gu", "dn", "wi", "wo"), [small])
    bufs = exchange_only(ex, "exchange_last")
    small_all = bufs[0]
    put(order, bufs[1:])
    small_sum = sum_devices(small_all, "sum_small_grads")

    n_mod_rows = DEPTH * 2 * N_MOD
    dmod_all = small_all[:, :n_mod_rows].reshape(N_DEV, DEPTH, 2, N_MOD * d)
    dmod_sum = small_sum[:n_mod_rows].reshape(DEPTH, 2, N_MOD * d)
    dm_rows = jnp.concatenate([dmod_all[:, :, 0].transpose(1, 0, 2), dmod_sum[:, 1:2],
                               jnp.zeros((DEPTH, MOD_ROWS - N_DEV - 1, N_MOD * d), F32)], axis=1)
    grad_b_mod = dmod_sum[:, 0] + dmod_sum[:, 1]
    dm_loc = lax.dynamic_slice(dm_rows, (0, 0, me * cl), (DEPTH, MOD_ROWS, cl))
    grad_w_mod, da_part = mod_bwd(cvecs, dm_loc, w_mod, "mod_bwd")
    da_all = all_gather(da_part, "gather_dcvec")
    grad_c_ctx = cctx_grad(da_all, c_ctx[None, :], "c_ctx_grad")[0]

    grad_norm_full = small_sum[offs[1]:offs[1] + DEPTH * 6].reshape(DEPTH, 6, d)
    grad_norm_g = lax.dynamic_slice(grad_norm_full, (0, 0, me * dl), (DEPTH, 6, dl))
    grad_w_pool = early_sum[early_offs[0]:early_offs[0] + w_pool.size // d].reshape(w_pool.shape)
    grad_pool_scale = early_sum[early_offs[1]:early_offs[1] + pool_scale.size // d].reshape(pool_scale.shape)
    grad_na_rpb = early_sum[early_offs[2]:early_offs[3]].reshape(-1)[:na_rpb.size].reshape(na_rpb.shape)

    def big_adam(key, w, m, v, name):
        shp = w.shape
        cols = shp[-1]
        outs = reduce_adamw(recv[key].reshape(N_DEV, -1, cols), w.reshape(-1, cols), m.reshape(-1, cols),
                            v.reshape(-1, cols), name)
        return tuple(t.reshape(shp) for t in outs)

    def small_adam(w, g, m, v, name):
        shp = w.shape
        cols = shp[-1]
        outs = adamw(w.reshape(-1, cols), g.reshape(-1, cols), m.reshape(-1, cols), v.reshape(-1, cols), name)
        return tuple(t.reshape(shp) for t in outs)

    b_gu = big_adam("gu", w_ffn_gate_up, m_w_ffn_gate_up, v_w_ffn_gate_up, "adam_gate_up")
    b_dn = big_adam("dn", w_ffn_down, m_w_ffn_down, v_w_ffn_down, "adam_down")
    b_wi = big_adam("wi", w_in, m_w_in, v_w_in, "adam_w_in")
    b_wo = big_adam("wo", w_out, m_w_out, v_w_out, "adam_w_out")
    a_cc = small_adam(c_ctx, grad_c_ctx, m_c_ctx, v_c_ctx, "adam_c_ctx")
    a_wm = small_adam(w_mod, grad_w_mod, m_w_mod, v_w_mod, "adam_w_mod")
    a_bm = small_adam(b_mod, grad_b_mod, m_b_mod, v_b_mod, "adam_b_mod")
    a_ng = small_adam(norm_g, grad_norm_g, m_norm_g, v_norm_g, "adam_norm_g")
    a_rp = small_adam(na_rpb, grad_na_rpb, m_na_rpb, v_na_rpb, "adam_na_rpb")
    a_wp = small_adam(w_pool, grad_w_pool, m_w_pool, v_w_pool, "adam_w_pool")
    a_ps = small_adam(pool_scale, grad_pool_scale, m_pool_scale, v_pool_scale, "adam_pool_scale")

    grads = (grad_c_ctx, grad_w_mod, grad_b_mod, grad_norm_g, b_gu[0], b_dn[0], b_wi[0], b_wo[0], grad_na_rpb, grad_w_pool, grad_pool_scale)
    deltas = (a_cc[0], a_wm[0], a_bm[0], a_ng[0], b_gu[1], b_dn[1], b_wi[1], b_wo[1], a_rp[0], a_wp[0], a_ps[0])
    new_m = (a_cc[1], a_wm[1], a_bm[1], a_ng[1], b_gu[2], b_dn[2], b_wi[2], b_wo[2], a_rp[1], a_wp[1], a_ps[1])
    new_v = (a_cc[2], a_wm[2], a_bm[2], a_ng[2], b_gu[3], b_dn[3], b_wi[3], b_wo[3], a_rp[2], a_wp[2], a_ps[2])
    return (loss, grad_x, *grads, *deltas, *new_m, *new_v)
```

```python
import functools
import math

import numpy as np
import jax
import jax.numpy as jnp
from jax import lax
from jax.experimental import pallas as pl
from jax.experimental.pallas import tpu as pltpu

F32 = jnp.float32
BF16 = jnp.bfloat16

N_DEV = 8
DEPTH = 2
GRID_W = 64
N_MOD = 9
NA_HEADS = 8
HEAD_DIM = 64
NA_WIDTH = NA_HEADS * HEAD_DIM
NA_KH = 8
NA_KW = 16
POOL_GROUPS = 4
POOL_CH = 128
POOL_WIDTH = POOL_GROUPS * POOL_CH
POOL_WINDOWS = (2, 4, 8, 16)
ROPE_THETA = 10000.0
ROPE_PAIRS = HEAD_DIM // 4
RMS_EPS = 1e-6
NEG_INF = -1e30
ATT_SCALE = HEAD_DIM ** -0.5

ADAM_LR = 0.001
ADAM_B1 = 0.9
ADAM_B2 = 0.999
ADAM_EPS = 1e-08
ADAM_WD = 0.01
ADAM_STEP = 10

TM = 256
LANES = 128
HEADS_PER_BLOCK = LANES // HEAD_DIM
N_HEAD_BLOCKS = NA_WIDTH // LANES
IN_BLOCK = 2 * LANES
N_QKV_BLOCKS = 3 * NA_WIDTH // IN_BLOCK
VMEM_LIMIT = 56 * 1024 * 1024
HIGHEST = lax.Precision.HIGHEST
MESH = pl.DeviceIdType.MESH
ANY = pl.BlockSpec(memory_space=pl.ANY)

NN = (((1,), (0,)), ((), ()))
NT = (((1,), (1,)), ((), ()))
TN = (((0,), (0,)), ((), ()))


def _pick(n, cands):
    for t in cands:
        if n % t == 0:
            return t
    raise ValueError(f"no tile for {n} among {cands}")


def _dot(a, b, dn=NN, precision=None):
    return lax.dot_general(a, b, dn, preferred_element_type=F32, precision=precision)


def _silu(x):
    return x * jax.nn.sigmoid(x)


def _dsilu(x):
    s = jax.nn.sigmoid(x)
    return s * (1.0 + x * (1.0 - s))


def _peer(mask):
    x, y, c = lax.axis_index("x"), lax.axis_index("y"), lax.axis_index("c")
    px = 1 - x if mask & 4 else x
    py = 1 - y if mask & 2 else y
    pc = 1 - c if mask & 1 else c
    return (px, py, pc), 4 * px + 2 * py + pc


class Exchange:
    def __init__(self, gathers=(), a2as=()):
        self.gathers = list(gathers)
        self.a2as = list(a2as)
        self.n_jobs = len(self.gathers) + len(self.a2as)

    def inputs(self):
        out = list(self.gathers)
        for v, buf, _ in self.a2as:
            out += [v, buf]
        return out

    def out_shapes(self):
        shapes = [jax.ShapeDtypeStruct((N_DEV,) + v.shape, v.dtype) for v in self.gathers]
        shapes += [jax.ShapeDtypeStruct(buf.shape, buf.dtype) for _, buf, _ in self.a2as]
        return shapes

    def aliases(self, n_in, n_out):
        ng = len(self.gathers)
        return {n_in + ng + 2 * k + 1: n_out + ng + k for k in range(len(self.a2as))}

    def scratch(self):
        per = N_DEV - 1
        return [pltpu.SemaphoreType.DMA((per * self.n_jobs,)), pltpu.SemaphoreType.DMA((per * self.n_jobs,)),
                pltpu.SemaphoreType.DMA((self.n_jobs,))]

    def _copies(self, in_refs, out_refs, sems, with_recvs):
        send_sems, recv_sems, local_sems = sems
        _, me = _peer(0)
        ng = len(self.gathers)
        local, sends, recvs = [], [], []
        for job in range(self.n_jobs):
            if job < ng:
                src_of = lambda pid, r=in_refs[job]: r
                dst_of = lambda pid, r=out_refs[job]: r.at[pid]
            else:
                k = job - ng
                stage = self.a2as[k][2]
                src_of = lambda pid, r=in_refs[ng + 2 * k]: r.at[pid]
                dst_of = lambda pid, r=out_refs[job], st=stage: r.at[pid, st]
            local.append(pltpu.make_async_copy(src_of(me), dst_of(me), local_sems.at[job]))
            for mask in range(1, N_DEV):
                peer, pid = _peer(mask)
                idx = job * (N_DEV - 1) + mask - 1
                sends.append(pltpu.make_async_remote_copy(
                    src_ref=src_of(pid), dst_ref=dst_of(me), send_sem=send_sems.at[idx],
                    recv_sem=recv_sems.at[idx], device_id=peer, device_id_type=MESH))
                if with_recvs:
                    recvs.append(pltpu.make_async_remote_copy(
                        src_ref=src_of(pid), dst_ref=dst_of(pid), send_sem=send_sems.at[idx],
                        recv_sem=recv_sems.at[idx], device_id=peer, device_id_type=MESH))
        return local, sends, recvs

    def start(self, in_refs, out_refs, sems):
        local, sends, _ = self._copies(in_refs, out_refs, sems, False)
        for cp in local + sends:
            cp.start()

    def wait(self, in_refs, out_refs, sems):
        local, sends, recvs = self._copies(in_refs, out_refs, sems, True)
        for cp in recvs:
            cp.wait_recv()
        for cp in sends:
            cp.wait_send()
        for cp in local:
            cp.wait()


def carrier_call(core, *, name, grid, in_specs, out_specs, out_shape, inputs, scratch_shapes=(), aliases=None,
                 exchange=None):
    aliases = dict(aliases or {})
    n_in, n_out, n_sc = len(in_specs), len(out_specs), len(scratch_shapes)
    sem = ("arbitrary",) * len(grid)
    params = pltpu.CompilerParams(dimension_semantics=sem, vmem_limit_bytes=VMEM_LIMIT)
    if exchange is None or exchange.n_jobs == 0:
        outs = pl.pallas_call(core, name=name, grid=grid, in_specs=list(in_specs), out_specs=tuple(out_specs),
                              out_shape=tuple(out_shape), scratch_shapes=list(scratch_shapes),
                              input_output_aliases=aliases, compiler_params=params)(*inputs)
        return list(outs), []
    x_in = exchange.inputs()
    x_out = exchange.out_shapes()
    aliases.update(exchange.aliases(n_in, n_out))

    def body(*refs):
        a = n_in + len(x_in)
        b = a + n_out + len(x_out)
        core_in, job_in = refs[:n_in], refs[n_in:a]
        core_out, job_out = refs[a:a + n_out], refs[a + n_out:b]
        core_sc, job_sc = refs[b:b + n_sc], refs[b + n_sc:]
        first = functools.reduce(lambda p, q: p & q, [pl.program_id(ax) == 0 for ax in range(len(grid))])
        last = functools.reduce(lambda p, q: p & q, [pl.program_id(ax) == g - 1 for ax, g in enumerate(grid)])

        @pl.when(first)
        def _():
            exchange.start(job_in, job_out, job_sc)

        core(*core_in, *core_out, *core_sc)

        @pl.when(last)
        def _():
            exchange.wait(job_in, job_out, job_sc)

    outs = pl.pallas_call(
        body, name=name, grid=grid, in_specs=list(in_specs) + [ANY] * len(x_in),
        out_specs=tuple(out_specs) + (ANY,) * len(x_out), out_shape=tuple(out_shape) + tuple(x_out),
        scratch_shapes=list(scratch_shapes) + exchange.scratch(), input_output_aliases=aliases,
        compiler_params=params)(*inputs, *x_in)
    return list(outs[:n_out]), list(outs[n_out:])


def exchange_only(exchange, name):
    def body(*refs):
        n_in, n_out = len(exchange.inputs()), len(exchange.out_shapes())
        job_in, job_out, sems = refs[:n_in], refs[n_in:n_in + n_out], refs[n_in + n_out:]
        exchange.start(job_in, job_out, sems)
        exchange.wait(job_in, job_out, sems)

    x_in = exchange.inputs()
    outs = pl.pallas_call(
        body, name=name, in_specs=[ANY] * len(x_in), out_specs=(ANY,) * len(exchange.out_shapes()),
        out_shape=tuple(exchange.out_shapes()), scratch_shapes=exchange.scratch(),
        input_output_aliases=exchange.aliases(0, 0))(*x_in)
    return list(outs)


def all_gather(v, name):
    return exchange_only(Exchange(gathers=[v]), name)[0]


def gather_two_level(vs, name):
    nv = len(vs)
    per = N_DEV - 1

    def body(*refs):
        v_refs, o_refs = refs[:nv], refs[nv:2 * nv]
        send_sems, recv_sems, local_sems = refs[2 * nv:]
        x, y, c = lax.axis_index("x"), lax.axis_index("y"), lax.axis_index("c")
        me, sibling = (x, y, c), (x, y, 1 - c)
        chips = [(1 - x, y), (x, 1 - y), (1 - x, 1 - y)]

        def copy(a, k, block, to, src=None):
            dst = o_refs[a].at[4 * block[0] + 2 * block[1] + block[2]]
            return pltpu.make_async_remote_copy(
                src_ref=dst if src is None else src, dst_ref=dst, send_sem=send_sems.at[a * per + k],
                recv_sem=recv_sems.at[a * per + k], device_id=to, device_id_type=MESH)

        mine = [pltpu.make_async_copy(v_refs[a], o_refs[a].at[4 * x + 2 * y + c], local_sems.at[a]) for a in range(nv)]
        first = []
        for a in range(nv):
            first.append(copy(a, 0, me, sibling, src=v_refs[a]))
            first += [copy(a, 1 + j, me, (*chip, c), src=v_refs[a]) for j, chip in enumerate(chips)]
        for cp in mine + first:
            cp.start()
        passed = []
        for a in range(nv):
            for j, chip in enumerate(chips):
                copy(a, 1 + j, (*chip, c), me).wait_recv()
                passed.append(copy(a, 4 + j, (*chip, c), sibling))
                passed[-1].start()
        for a in range(nv):
            copy(a, 0, sibling, me).wait_recv()
            for j, chip in enumerate(chips):
                copy(a, 4 + j, (*chip, 1 - c), me).wait_recv()
        for cp in first + passed:
            cp.wait_send()
        for cp in mine:
            cp.wait()

    outs = pl.pallas_call(
        body, name=name, in_specs=[ANY] * nv, out_specs=(ANY,) * nv,
        out_shape=tuple(jax.ShapeDtypeStruct((N_DEV,) + v.shape, v.dtype) for v in vs),
        scratch_shapes=[pltpu.SemaphoreType.DMA((per * nv,)), pltpu.SemaphoreType.DMA((per * nv,)),
                        pltpu.SemaphoreType.DMA((nv,))])(*vs)
    return list(outs)


def _rms(xf):
    return lax.rsqrt(jnp.mean(xf * xf, axis=-1, keepdims=True) + RMS_EPS)


def _is_ctx(i, tm, s):
    return (i * tm + lax.broadcasted_iota(jnp.int32, (tm, 1), 0)) >= s


def _by_tile_kind(i, tm, s, fn):
    n_latent = s // tm

    @pl.when(i < n_latent)
    def _():
        fn(None)

    @pl.when(i >= n_latent)
    def _():
        fn(_is_ctx(i, tm, s))


def _mod_rows(mod_ref, k, is_ctx):
    if is_ctx is None:
        return mod_ref[0, k:k + 1, :]
    return jnp.where(is_ctx, mod_ref[1, k:k + 1, :], mod_ref[0, k:k + 1, :])


def _norm_mod(xf, g, mod_ref, k_shift, k_scale, is_ctx):
    nrm = xf * _rms(xf) * g
    return (nrm * (1.0 + _mod_rows(mod_ref, k_scale, is_ctx)) + _mod_rows(mod_ref, k_shift, is_ctx)).astype(BF16)


def _post(xf, ff, g, mod_ref, k_gate, coef, is_ctx):
    return xf + coef * _mod_rows(mod_ref, k_gate, is_ctx) * (ff * _rms(ff) * g)


def _red_add(red_ref, first, is_ctx, rows):
    @pl.when(first)
    def _():
        red_ref[...] = jnp.zeros_like(red_ref)

    for r, val in enumerate(rows):
        tot = jnp.sum(val, axis=0, keepdims=True)
        if is_ctx is None:
            red_ref[0, r:r + 1, :] += tot
        else:
            ctx = jnp.sum(jnp.where(is_ctx, val, 0.0), axis=0, keepdims=True)
            red_ref[0, r:r + 1, :] += tot - ctx
            red_ref[1, r:r + 1, :] += ctx


def _pre_bwd(xf, dh, dxo, g, mod_ref, k_scale, is_ctx):
    r = _rms(xf)
    xhat = xf * r
    dn = dh * (1.0 + _mod_rows(mod_ref, k_scale, is_ctx))
    dxhat = dn * g
    dx = dxo + r * (dxhat - xhat * jnp.mean(dxhat * xhat, axis=-1, keepdims=True))
    return dx, [dh, dh * (xhat * g), dn * xhat]


def _vec_spec(d):
    return pl.BlockSpec((1, d), lambda *_: (0, 0))


def _mod_whole(d):
    return pl.BlockSpec((2, N_MOD, d), lambda *_: (0, 0, 0))


def _red_whole(d):
    return pl.BlockSpec((2, 8, d), lambda *_: (0, 0, 0))


def _token_tile(n):
    return _pick(n, (768, 640, 512, 384, 256))


def _ffn_tile(n):
    return _pick(n, (528, 384, 640, 256))


def _resident(shape):
    zeros = (0,) * len(shape)
    return pl.BlockSpec(shape, lambda i: zeros, pipeline_mode=pl.Buffered(1))


def ffn_up(x, g, mod, wgu, s, k0, name, exchange=None):
    n, d = x.shape
    nk, fq = wgu.shape[1], wgu.shape[-1]
    tm = _ffn_tile(n)

    def core(x_ref, g_ref, mod_ref, w_ref, hb_ref, gu_ref, a_ref):
        i = pl.program_id(0)

        def prologue(is_ctx):
            hb_ref[...] = _norm_mod(x_ref[...], g_ref[...], mod_ref, k0, k0 + 1, is_ctx)

        _by_tile_kind(i, tm, s, prologue)
        h = hb_ref[...]
        for k in range(nk):
            gg = _dot(h, w_ref[0, k])
            uu = _dot(h, w_ref[1, k])
            gu_ref[0, k] = gg.astype(BF16)
            gu_ref[1, k] = uu.astype(BF16)
            a_ref[k] = (_silu(gg) * uu).astype(BF16)

    outs, xo = carrier_call(
        core, name=name, grid=(n // tm,),
        in_specs=[pl.BlockSpec((tm, d), lambda i: (i, 0)), _vec_spec(d), _mod_whole(d), _resident(wgu.shape)],
        out_specs=[pl.BlockSpec((tm, d), lambda i: (i, 0)),
                   pl.BlockSpec((2, nk, tm, fq), lambda i: (0, 0, i, 0)),
                   pl.BlockSpec((nk, tm, fq), lambda i: (0, i, 0))],
        out_shape=[jax.ShapeDtypeStruct((n, d), BF16), jax.ShapeDtypeStruct((2, nk, n, fq), BF16),
                   jax.ShapeDtypeStruct((nk, n, fq), BF16)],
        inputs=[x, g, mod, wgu], exchange=exchange)
    return outs, xo


def ffn_down(a4, wd4, x, g, mod, s, k_gate, name, exchange=None):
    n, d = x.shape
    nk, fq = wd4.shape[0], wd4.shape[1]
    tm = _ffn_tile(n)

    def core(a_ref, w_ref, x_ref, g_ref, mod_ref, f_ref, xo_ref):
        i = pl.program_id(0)
        ff = _dot(a_ref[0], w_ref[0])
        for k in range(1, nk):
            ff = ff + _dot(a_ref[k], w_ref[k])
        f_ref[...] = ff

        def epilogue(is_ctx):
            xo_ref[...] = _post(x_ref[...], f_ref[...], g_ref[...], mod_ref, k_gate, 0.5, is_ctx)

        _by_tile_kind(i, tm, s, epilogue)

    tile = pl.BlockSpec((tm, d), lambda i: (i, 0))
    outs, xo = carrier_call(
        core, name=name, grid=(n // tm,),
        in_specs=[pl.BlockSpec((nk, tm, fq), lambda i: (0, i, 0)), _resident(wd4.shape), tile, _vec_spec(d),
                  _mod_whole(d)],
        out_specs=[tile, tile],
        out_shape=[jax.ShapeDtypeStruct((n, d), F32), jax.ShapeDtypeStruct((n, d), F32)],
        inputs=[a4, wd4, x, g, mod], exchange=exchange)
    return outs, xo


def ffn_da(df, wd4, gu, name, exchange=None):
    n, d = df.shape
    nk, fq = wd4.shape[0], wd4.shape[1]
    tm = _ffn_tile(n)

    def core(df_ref, w_ref, gu_ref, o_ref):
        dfv = df_ref[...]
        for k in range(nk):
            da = _dot(dfv, w_ref[k], NT).astype(BF16)
            gg = gu_ref[0, k]
            uu = gu_ref[1, k]
            sg = jax.nn.sigmoid(gg.astype(F32)).astype(BF16)
            o_ref[0, k] = da * (uu * (sg * (1 + gg * (1 - sg))))
            o_ref[1, k] = da * (gg * sg)

    gu_spec = pl.BlockSpec((2, nk, tm, fq), lambda i: (0, 0, i, 0))
    outs, xo = carrier_call(
        core, name=name, grid=(n // tm,),
        in_specs=[pl.BlockSpec((tm, d), lambda i: (i, 0)), _resident(wd4.shape), gu_spec],
        out_specs=[gu_spec], out_shape=[jax.ShapeDtypeStruct(gu.shape, BF16)],
        inputs=[df, wd4, gu], exchange=exchange)
    return outs[0], xo


def ffn_dh(dgu, wgu, x, dxo, g, mod, s, k0, name, exchange=None):
    n, d = x.shape
    nk, fq = wgu.shape[1], wgu.shape[-1]
    tm = _ffn_tile(n)

    def core(dgu_ref, w_ref, x_ref, dxo_ref, g_ref, mod_ref, dx_ref, red_ref, dh_s):
        i = pl.program_id(0)
        dh = _dot(dgu_ref[0, 0], w_ref[0, 0], NT) + _dot(dgu_ref[1, 0], w_ref[1, 0], NT)
        for k in range(1, nk):
            dh = dh + _dot(dgu_ref[0, k], w_ref[0, k], NT) + _dot(dgu_ref[1, k], w_ref[1, k], NT)
        dh_s[...] = dh

        def epilogue(is_ctx):
            dx, sums = _pre_bwd(x_ref[...], dh_s[...], dxo_ref[...], g_ref[...], mod_ref, k0 + 1, is_ctx)
            dx_ref[...] = dx
            _red_add(red_ref, i == 0, is_ctx, sums)

        _by_tile_kind(i, tm, s, epilogue)

    tile = pl.BlockSpec((tm, d), lambda i: (i, 0))
    outs, xo = carrier_call(
        core, name=name, grid=(n // tm,),
        in_specs=[pl.BlockSpec((2, nk, tm, fq), lambda i: (0, 0, i, 0)), _resident(wgu.shape), tile, tile,
                  _vec_spec(d), _mod_whole(d)],
        out_specs=[tile, _red_whole(d)],
        out_shape=[jax.ShapeDtypeStruct((n, d), F32), jax.ShapeDtypeStruct((2, 8, d), F32)],
        scratch_shapes=[pltpu.VMEM((tm, d), F32)], inputs=[dgu, wgu, x, dxo, g, mod], exchange=exchange)
    return outs, xo


def grad_weight(a, b, name, a_lead=None, b_lead=None, b_cols=None, per_step=1, exchange=None):
    n = a.shape[-2]
    ka = a.shape[-1]
    kb = b_cols or b.shape[-1]
    nj = a_lead or b_lead or (b.shape[-1] // b_cols)
    tk = _pick(n, (1408, 1024, 768, 640, 512, 256))
    nk = n // tk
    ps = per_step
    assert nj % ps == 0

    def core(a_ref, b_ref, o_ref, acc):
        kk = pl.program_id(1)

        @pl.when(kk == 0)
        def _():
            acc[...] = jnp.zeros_like(acc)

        if b_cols:
            acc[...] += _dot(a_ref[...], b_ref[...], TN)
        else:
            for t in range(ps):
                acc[t] += _dot(a_ref[t] if a_lead else a_ref[...], b_ref[t] if b_lead else b_ref[...], TN)

        @pl.when(kk == nk - 1)
        def _():
            for t in range(ps):
                o_ref[t] = (acc[:, t * kb:(t + 1) * kb] if b_cols else acc[t]).astype(BF16)

    a_spec = (pl.BlockSpec((ps, tk, ka), lambda j, kk: (j, kk, 0)) if a_lead
              else pl.BlockSpec((tk, ka), lambda j, kk: (kk, 0)))
    if b_lead:
        b_spec = pl.BlockSpec((ps, tk, kb), lambda j, kk: (j, kk, 0))
    elif b_cols:
        b_spec = pl.BlockSpec((tk, ps * kb), lambda j, kk: (kk, j))
    else:
        b_spec = pl.BlockSpec((tk, kb), lambda j, kk: (kk, 0))
    outs, xo = carrier_call(
        core, name=name, grid=(nj // ps, nk), in_specs=[a_spec, b_spec],
        out_specs=[pl.BlockSpec((ps, ka, kb), lambda j, kk: (j, 0, 0))],
        out_shape=[jax.ShapeDtypeStruct((nj, ka, kb), BF16)],
        scratch_shapes=[pltpu.VMEM((ka, ps * kb) if b_cols else (ps, ka, kb), F32)], inputs=[a, b], exchange=exchange)
    return outs[0], xo


def post_bwd(f, dxo, g, mod, k_gate, coef, s, name):
    n, d = f.shape

    def core(f_ref, dxo_ref, g_ref, mod_ref, df_ref, red_ref):
        i = pl.program_id(0)

        def body(is_ctx):
            ff = f_ref[...]
            dxo_ = dxo_ref[...]
            gg = g_ref[...]
            r = _rms(ff)
            fn = ff * r
            dy = (coef * _mod_rows(mod_ref, k_gate, is_ctx)) * dxo_
            dfn = dy * gg
            df_ref[...] = (r * (dfn - fn * jnp.mean(dfn * fn, axis=-1, keepdims=True))).astype(BF16)
            _red_add(red_ref, i == 0, is_ctx, [coef * (fn * gg) * dxo_, dy * fn])

        _by_tile_kind(i, TM, s, body)

    tile = pl.BlockSpec((TM, d), lambda i: (i, 0))
    outs, _ = carrier_call(
        core, name=name, grid=(n // TM,), in_specs=[tile, tile, _vec_spec(d), _mod_whole(d)],
        out_specs=[tile, _red_whole(d)],
        out_shape=[jax.ShapeDtypeStruct((n, d), BF16), jax.ShapeDtypeStruct((2, 8, d), F32)],
        inputs=[f, dxo, g, mod])
    return outs


def matmul_nt(a, b, name):
    m, k = a.shape
    n = b.shape[0]
    tm = _token_tile(m)

    def core(a_ref, b_ref, o_ref):
        o_ref[...] = _dot(a_ref[...], b_ref[...], NT)

    outs, _ = carrier_call(
        core, name=name, grid=(m // tm,),
        in_specs=[pl.BlockSpec((tm, k), lambda i: (i, 0)), pl.BlockSpec((n, k), lambda i: (0, 0))],
        out_specs=[pl.BlockSpec((tm, n), lambda i: (i, 0))], out_shape=[jax.ShapeDtypeStruct((m, n), F32)],
        inputs=[a, b])
    return outs[0]


def _rope_tables(s, n):
    t = jnp.arange(n)
    lane = jnp.arange(LANES)
    dd = lane % HEAD_DIM
    inv = ROPE_THETA ** (-(dd % ROPE_PAIRS).astype(F32) / ROPE_PAIRS)
    pos = jnp.where(dd[None, :] < HEAD_DIM // 2, (t // GRID_W)[:, None], (t % GRID_W)[:, None]).astype(F32)
    ang = pos * inv[None, :]
    live = (t < s)[:, None]
    first = ((dd % (2 * ROPE_PAIRS)) < ROPE_PAIRS)[None, :]
    cos = jnp.where(live, jnp.cos(ang), 1.0)
    sin = jnp.where(live, jnp.sin(ang), 0.0)
    sa = jnp.where(first, -sin, 0.0)
    sb = jnp.where(first, 0.0, sin)
    return cos.astype(F32), sa.astype(F32), sb.astype(F32)


def _rope(xv, cos, sa, sb):
    return (xv * cos + pltpu.roll(xv, LANES - ROPE_PAIRS, 1) * sa + pltpu.roll(xv, ROPE_PAIRS, 1) * sb)


def mix_in(x, g, mod, win8, tables, s, name, exchange=None):
    n, d = x.shape
    tm = _token_tile(n)
    nb = win8.shape[0]
    n_rope = 2 * NA_WIDTH // IN_BLOCK

    def core(x_ref, g_ref, mod_ref, w_ref, c_ref, sa_ref, sb_ref, hb_ref, qkv_ref, u_ref):
        i = pl.program_id(0)
        hb = _norm_mod(x_ref[...], g_ref[...], mod_ref, 3, 4, _is_ctx(i, tm, s))
        hb_ref[...] = hb
        cos, sa, sb = c_ref[...], sa_ref[...], sb_ref[...]
        for j in range(nb):
            y = _dot(hb, w_ref[j])
            if j < n_rope:
                for b in range(IN_BLOCK // LANES):
                    sl = slice(b * LANES, (b + 1) * LANES)
                    qkv_ref[:, j * IN_BLOCK + b * LANES:j * IN_BLOCK + (b + 1) * LANES] = (
                        _rope(y[:, sl], cos, sa, sb).astype(BF16))
            elif j < N_QKV_BLOCKS:
                qkv_ref[:, j * IN_BLOCK:(j + 1) * IN_BLOCK] = y.astype(BF16)
            else:
                u_ref[:, (j - N_QKV_BLOCKS) * IN_BLOCK:(j - N_QKV_BLOCKS + 1) * IN_BLOCK] = y

    tab = pl.BlockSpec((tm, LANES), lambda i: (i, 0))
    row = lambda w: pl.BlockSpec((tm, w), lambda i: (i, 0))
    outs, xo = carrier_call(
        core, name=name, grid=(n // tm,),
        in_specs=[row(d), _vec_spec(d), _mod_whole(d), pl.BlockSpec((nb, d, IN_BLOCK), lambda i: (0, 0, 0)),
                  tab, tab, tab],
        out_specs=[row(d), row(3 * NA_WIDTH), row(POOL_WIDTH)],
        out_shape=[jax.ShapeDtypeStruct((n, d), BF16), jax.ShapeDtypeStruct((n, 3 * NA_WIDTH), BF16),
                   jax.ShapeDtypeStruct((n, POOL_WIDTH), F32)],
        inputs=[x, g, mod, win8, *tables], exchange=exchange)
    return outs, xo


def qkv_bwd(dq, dk, dv, du, tables, name):
    n = dq.shape[0]
    w = NA_WIDTH

    def core(dq_ref, dk_ref, dv_ref, du_ref, c_ref, sa_ref, sb_ref, o_ref):
        cos, sa, sb = c_ref[...], -sa_ref[...], -sb_ref[...]
        for b in range(N_HEAD_BLOCKS):
            sl = slice(b * LANES, (b + 1) * LANES)
            o_ref[:, b * LANES:(b + 1) * LANES] = _rope(dq_ref[:, sl], cos, sa, sb).astype(BF16)
            o_ref[:, w + b * LANES:w + (b + 1) * LANES] = _rope(dk_ref[:, sl], cos, sa, sb).astype(BF16)
        o_ref[:, 2 * w:3 * w] = dv_ref[...].astype(BF16)
        o_ref[:, 3 * w:] = du_ref[...].astype(BF16)

    tab = pl.BlockSpec((TM, LANES), lambda i: (i, 0))
    tile = pl.BlockSpec((TM, w), lambda i: (i, 0))
    outs, _ = carrier_call(
        core, name=name, grid=(n // TM,), in_specs=[tile, tile, tile, tile, tab, tab, tab],
        out_specs=[pl.BlockSpec((TM, 4 * w), lambda i: (i, 0))],
        out_shape=[jax.ShapeDtypeStruct((n, 4 * w), BF16)], inputs=[dq, dk, dv, du, *tables])
    return outs[0]


def mix_out(na, py, wout, x, g, mod, s, name):
    n, d = x.shape
    tm = _token_tile(n)
    half = na.shape[1]

    def core(na_ref, py_ref, w_ref, x_ref, g_ref, mod_ref, f_ref, xo_ref):
        i = pl.program_id(0)
        ff = _dot(na_ref[...], w_ref[:half, :]) + _dot(py_ref[...], w_ref[half:, :])
        f_ref[...] = ff
        xo_ref[...] = _post(x_ref[...], ff, g_ref[...], mod_ref, 5, 1.0, _is_ctx(i, tm, s))

    tile = pl.BlockSpec((tm, d), lambda i: (i, 0))
    htile = pl.BlockSpec((tm, half), lambda i: (i, 0))
    outs, _ = carrier_call(
        core, name=name, grid=(n // tm,),
        in_specs=[htile, htile, pl.BlockSpec((2 * half, d), lambda i: (0, 0)), tile, _vec_spec(d), _mod_whole(d)],
        out_specs=[tile, tile],
        out_shape=[jax.ShapeDtypeStruct((n, d), F32), jax.ShapeDtypeStruct((n, d), F32)],
        inputs=[na, py, wout, x, g, mod])
    return outs


def grad_wout(na, py, dfm, name):
    n, half = na.shape
    d = dfm.shape[1]
    tk = _pick(n, (1408, 1024, 768, 640, 512, 256))
    nk = n // tk

    def core(na_ref, py_ref, b_ref, o_ref, acc):
        hh, kk = pl.program_id(0), pl.program_id(1)

        @pl.when(kk == 0)
        def _():
            acc[...] = jnp.zeros_like(acc)

        @pl.when(hh == 0)
        def _():
            acc[...] += _dot(na_ref[...], b_ref[...], TN)

        @pl.when(hh == 1)
        def _():
            acc[...] += _dot(py_ref[...], b_ref[...], TN)

        @pl.when(kk == nk - 1)
        def _():
            o_ref[0] = acc[...].astype(BF16)

    htile = pl.BlockSpec((tk, half), lambda hh, kk: (kk, 0))
    outs, _ = carrier_call(
        core, name=name, grid=(2, nk), in_specs=[htile, htile, pl.BlockSpec((tk, d), lambda hh, kk: (kk, 0))],
        out_specs=[pl.BlockSpec((1, half, d), lambda hh, kk: (hh, 0, 0))],
        out_shape=[jax.ShapeDtypeStruct((2, half, d), BF16)],
        scratch_shapes=[pltpu.VMEM((half, d), F32)], inputs=[na, py, dfm])
    return outs[0]


def mix_dh(dqkvu, win8, x, dxo, g, mod, s, name, exchange=None):
    n, d = x.shape
    tm = _token_tile(n)
    nb = win8.shape[0]

    def core(dq_ref, w_ref, x_ref, dxo_ref, g_ref, mod_ref, dx_ref, red_ref):
        i = pl.program_id(0)
        dh = _dot(dq_ref[:, :IN_BLOCK], w_ref[0], NT)
        for j in range(1, nb):
            dh = dh + _dot(dq_ref[:, j * IN_BLOCK:(j + 1) * IN_BLOCK], w_ref[j], NT)
        is_ctx = _is_ctx(i, tm, s)
        dx, sums = _pre_bwd(x_ref[...], dh, dxo_ref[...], g_ref[...], mod_ref, 4, is_ctx)
        dx_ref[...] = dx
        _red_add(red_ref, i == 0, is_ctx, sums)

    tile = pl.BlockSpec((tm, d), lambda i: (i, 0))
    outs, xo = carrier_call(
        core, name=name, grid=(n // tm,),
        in_specs=[pl.BlockSpec((tm, nb * IN_BLOCK), lambda i: (i, 0)),
                  pl.BlockSpec((nb, d, IN_BLOCK), lambda i: (0, 0, 0)), tile, tile, _vec_spec(d), _mod_whole(d)],
        out_specs=[tile, _red_whole(d)],
        out_shape=[jax.ShapeDtypeStruct((n, d), F32), jax.ShapeDtypeStruct((2, 8, d), F32)],
        inputs=[dqkvu, win8, x, dxo, g, mod], exchange=exchange)
    return outs, xo


def _na_consts():
    j = np.arange(GRID_W)
    col_start = np.clip(j - NA_KW // 2, 0, GRID_W - NA_KW)
    valid = (j[None, :] >= col_start[:, None]) & (j[None, :] < col_start[:, None] + NA_KW)
    dc = np.clip(j[None, :] - j[:, None] + NA_KW - 1, 0, 2 * NA_KW - 2)
    onehot = np.zeros((LANES, GRID_W, GRID_W), np.float32)
    for d in range(2 * NA_KW - 1):
        onehot[d] = ((dc == d) & valid).astype(np.float32)
    negmask = np.where(valid, 0.0, NEG_INF).astype(np.float32)
    return onehot.reshape(LANES, GRID_W * GRID_W), np.tile(negmask, (1, NA_KH))


def bias_tables(rpb, name):
    onehot, negmask = _na_consts()
    nj = 2 * NA_KH - 1
    rows = NA_HEADS * nj
    a = jnp.pad(rpb.reshape(rows, 2 * NA_KW - 1), ((0, 0), (0, LANES - (2 * NA_KW - 1))))

    def body(a_ref, e_ref, o_ref):
        o_ref[...] = _dot(a_ref[...], e_ref[...], precision=HIGHEST)

    t = pl.pallas_call(
        body, name=name, out_shape=jax.ShapeDtypeStruct((rows, GRID_W * GRID_W), F32),
        in_specs=[pl.BlockSpec(memory_space=pltpu.VMEM)] * 2,
        out_specs=pl.BlockSpec(memory_space=pltpu.VMEM),
    )(a, jnp.asarray(onehot))
    t = t.reshape(NA_HEADS, nj, GRID_W, GRID_W)
    tb = jnp.stack([t[:, j0:j0 + NA_KH] for j0 in range(NA_KH)])
    tb = tb.transpose(0, 1, 3, 2, 4).reshape(NA_KH, NA_HEADS, GRID_W, NA_KH * GRID_W)
    return tb + jnp.asarray(negmask)[None, None]


def bias_tables_bwd(dtb, name):
    onehot, _ = _na_consts()
    nj = 2 * NA_KH - 1
    d5 = dtb.reshape(NA_KH, NA_HEADS, GRID_W, NA_KH, GRID_W).transpose(0, 3, 1, 2, 4)
    d2 = d5.reshape(NA_KH * NA_KH * NA_HEADS, GRID_W * GRID_W)

    def body(d_ref, e_ref, o_ref):
        r = _dot(d_ref[...], e_ref[...], NT, precision=HIGHEST)
        for j in range(nj):
            acc = jnp.zeros((NA_HEADS, LANES), F32)
            for j0 in range(NA_KH):
                kk = j - j0
                if 0 <= kk < NA_KH:
                    base = (j0 * NA_KH + kk) * NA_HEADS
                    acc = acc + r[base:base + NA_HEADS, :]
            o_ref[j] = acc

    out = pl.pallas_call(
        body, name=name, out_shape=jax.ShapeDtypeStruct((nj, NA_HEADS, LANES), F32),
        in_specs=[pl.BlockSpec(memory_space=pltpu.VMEM)] * 2,
        out_specs=pl.BlockSpec(memory_space=pltpu.VMEM),
        compiler_params=pltpu.CompilerParams(vmem_limit_bytes=VMEM_LIMIT),
    )(d2, jnp.asarray(onehot))
    return out[:, :, :2 * NA_KW - 1].transpose(1, 0, 2)


def _head_masks():
    lane = lax.broadcasted_iota(jnp.int32, (1, LANES), 1)
    return [(lane >= h * HEAD_DIM) & (lane < (h + 1) * HEAD_DIM) for h in range(HEADS_PER_BLOCK)]


def _row_window(r, rows):
    rs = jnp.clip(r - NA_KH // 2, 0, rows - NA_KH)
    return rs - r + NA_KH - 1, pl.multiple_of(rs * GRID_W, GRID_W)


def _stack_heads(t, masks):
    return jnp.concatenate([jnp.where(mk, t, jnp.zeros_like(t)) for mk in masks], axis=0)


def _unstack_heads(t2, masks):
    out = t2[(HEADS_PER_BLOCK - 1) * GRID_W:, :]
    for h in reversed(range(HEADS_PER_BLOCK - 1)):
        out = jnp.where(masks[h], t2[h * GRID_W:(h + 1) * GRID_W, :], out)
    return out


NA_ROWS_PER_STEP = 4
NA_STEP = NA_ROWS_PER_STEP * GRID_W
SLAB = NA_KH * GRID_W
K_COL = N_HEAD_BLOCKS
V_COL = 2 * N_HEAD_BLOCKS


def na_fwd(qkv, tb, s, name, exchange=None):
    n = qkv.shape[0]
    l = n - s
    rows = s // GRID_W
    rr = NA_ROWS_PER_STEP
    x_steps = rows // rr

    def core(q_ref, k_ref, v_ref, kc_ref, vc_ref, tb_ref, o_ref, lse_ref):
        rb = pl.program_id(1)

        @pl.when(rb >= x_steps)
        def _():
            o_ref[...] = jnp.zeros_like(o_ref)
            lse_ref[...] = jnp.zeros_like(lse_ref)

        @pl.when(rb < x_steps)
        def _():
            masks = _head_masks()
            kcb, vcb = kc_ref[...], vc_ref[...]
            hq = HEADS_PER_BLOCK * GRID_W
            wins, q2s = [], []
            for t in range(rr):
                j0, off = _row_window(rb * rr + t, rows)
                wins.append((j0, off))
                q2s.append(_stack_heads(q_ref[t * GRID_W:(t + 1) * GRID_W, :] * ATT_SCALE, masks))
            s_ctx_all = _dot(jnp.concatenate(q2s, axis=0), kcb, NT)
            scores = []
            for t, (j0, off) in enumerate(wins):
                s_loc = _dot(q2s[t], k_ref[pl.ds(off, SLAB), :], NT) + tb_ref[j0].reshape(hq, SLAB)
                scores.append((s_loc, s_ctx_all[t * hq:(t + 1) * hq, :]))
            probs = []
            for s_loc, s_ctx in scores:
                m = jnp.maximum(jnp.max(s_loc, axis=-1, keepdims=True), jnp.max(s_ctx, axis=-1, keepdims=True))
                p_loc = jnp.exp(s_loc - m)
                p_ctx = jnp.exp(s_ctx - m)
                den = jnp.sum(p_loc, axis=-1, keepdims=True) + jnp.sum(p_ctx, axis=-1, keepdims=True)
                probs.append((p_loc.astype(BF16), p_ctx.astype(BF16), den, m + jnp.log(den)))
            o_ctx_all = _dot(jnp.concatenate([p[1] for p in probs], axis=0), vcb)
            for t, (p_loc, p_ctx, den, lse2) in enumerate(probs):
                o2 = (_dot(p_loc, v_ref[pl.ds(wins[t][1], SLAB), :]) + o_ctx_all[t * hq:(t + 1) * hq, :]) / den
                o_ref[t * GRID_W:(t + 1) * GRID_W, :] = _unstack_heads(o2, masks).astype(BF16)
                lse_ref[0, t * GRID_W:(t + 1) * GRID_W, :] = _unstack_heads(lse2, masks)

    cb = s // l
    outs, xo = carrier_call(
        core, name=name, grid=(N_HEAD_BLOCKS, n // NA_STEP),
        in_specs=[pl.BlockSpec((NA_STEP, LANES), lambda hb, rb: (jnp.minimum(rb, x_steps - 1), hb)),
                  pl.BlockSpec((s, LANES), lambda hb, rb: (0, K_COL + hb)),
                  pl.BlockSpec((s, LANES), lambda hb, rb: (0, V_COL + hb)),
                  pl.BlockSpec((l, LANES), lambda hb, rb: (cb, K_COL + hb)),
                  pl.BlockSpec((l, LANES), lambda hb, rb: (cb, V_COL + hb)),
                  pl.BlockSpec((NA_KH, HEADS_PER_BLOCK, GRID_W, SLAB), lambda hb, rb: (0, hb, 0, 0))],
        out_specs=[pl.BlockSpec((NA_STEP, LANES), lambda hb, rb: (rb, hb)),
                   pl.BlockSpec((1, NA_STEP, LANES), lambda hb, rb: (hb, rb, 0))],
        out_shape=[jax.ShapeDtypeStruct((n, NA_WIDTH), BF16), jax.ShapeDtypeStruct((N_HEAD_BLOCKS, n, LANES), F32)],
        inputs=[qkv, qkv, qkv, qkv, qkv, tb], exchange=exchange)
    return outs, xo


def na_bwd(qkv, tb, o, dmix, lse, s, name, exchange=None):
    n = qkv.shape[0]
    l = n - s
    rows = s // GRID_W
    rr = NA_ROWS_PER_STEP
    x_steps = rows // rr

    def core(q_ref, k_ref, v_ref, kc_ref, vc_ref, tb_ref, o_ref, do_ref, lse_ref, dq_ref, dk_ref, dv_ref, dtb_ref):
        rb = pl.program_id(1)

        @pl.when(rb == 0)
        def _():
            dk_ref[...] = jnp.zeros_like(dk_ref)
            dv_ref[...] = jnp.zeros_like(dv_ref)
            dtb_ref[...] = jnp.zeros_like(dtb_ref)

        @pl.when(rb >= x_steps)
        def _():
            dq_ref[...] = jnp.zeros_like(dq_ref)

        @pl.when(rb < x_steps)
        def _():
            masks = _head_masks()
            kcb, vcb = kc_ref[...], vc_ref[...]
            hq = HEADS_PER_BLOCK * GRID_W
            rows1 = []
            for t in range(rr):
                j0, off = _row_window(rb * rr + t, rows)
                sl = slice(t * GRID_W, (t + 1) * GRID_W)
                q2 = _stack_heads(q_ref[sl, :] * ATT_SCALE, masks)
                do_f = do_ref[sl, :]
                do2 = _stack_heads(do_f.astype(BF16), masks)
                dd = do_f * o_ref[sl, :].astype(F32)
                delta2 = jnp.concatenate(
                    [jnp.sum(jnp.where(mk, dd, 0.0), axis=-1, keepdims=True) for mk in masks], axis=0)
                lse_t = lse_ref[0, sl, :]
                lse2 = jnp.concatenate(
                    [lse_t[:, h * HEAD_DIM:h * HEAD_DIM + 1] for h in range(HEADS_PER_BLOCK)], axis=0)
                rows1.append((j0, off, q2, do2, lse2, delta2))
            q2_all = jnp.concatenate([r[2] for r in rows1], axis=0)
            do2_all = jnp.concatenate([r[3] for r in rows1], axis=0)
            lse2_all = jnp.concatenate([r[4] for r in rows1], axis=0)
            delta2_all = jnp.concatenate([r[5] for r in rows1], axis=0)
            s_ctx_all = _dot(q2_all, kcb, NT) - lse2_all
            dp_ctx_all = _dot(do2_all, vcb, NT) - delta2_all
            stage1 = []
            for j0, off, q2, do2, lse2, delta2 in rows1:
                kslab = k_ref[pl.ds(off, SLAB), :]
                vslab = v_ref[pl.ds(off, SLAB), :]
                s_loc = _dot(q2, kslab, NT) + tb_ref[j0].reshape(hq, SLAB) - lse2
                dp_loc = _dot(do2, vslab, NT) - delta2
                stage1.append((j0, off, q2, do2, s_loc, dp_loc))
            p_ctx_all = jnp.exp(s_ctx_all)
            ds_ctx_all = (p_ctx_all * dp_ctx_all).astype(BF16)
            p_ctx_all = p_ctx_all.astype(BF16)
            stage2 = []
            for j0, off, q2, do2, s_loc, dp_loc in stage1:
                p_loc = jnp.exp(s_loc)
                ds_loc = p_loc * dp_loc
                dtb_ref[j0] += ds_loc.reshape(HEADS_PER_BLOCK, GRID_W, SLAB)
                stage2.append((off, q2, do2, p_loc.astype(BF16), ds_loc.astype(BF16)))
            dq_ctx_all = _dot(ds_ctx_all, kcb)
            for t, (off, q2, do2, p_loc, ds_loc) in enumerate(stage2):
                dq2 = (_dot(ds_loc, k_ref[pl.ds(off, SLAB), :]) + dq_ctx_all[t * hq:(t + 1) * hq, :]) * ATT_SCALE
                dq_ref[t * GRID_W:(t + 1) * GRID_W, :] = _unstack_heads(dq2, masks)
                dk_ref[pl.ds(off, SLAB), :] += _dot(ds_loc, q2, TN)
                dv_ref[pl.ds(off, SLAB), :] += _dot(p_loc, do2, TN)
            dk_ref[s:, :] += _dot(ds_ctx_all, q2_all, TN)
            dv_ref[s:, :] += _dot(p_ctx_all, do2_all, TN)

    cb = s // l
    clamp = lambda hb, rb: (jnp.minimum(rb, x_steps - 1), hb)
    tile_in = pl.BlockSpec((NA_STEP, LANES), clamp)
    whole_out = pl.BlockSpec((n, LANES), lambda hb, rb: (0, hb))
    tbs = pl.BlockSpec((NA_KH, HEADS_PER_BLOCK, GRID_W, SLAB), lambda hb, rb: (0, hb, 0, 0))
    f32n = jax.ShapeDtypeStruct((n, NA_WIDTH), F32)
    outs, xo = carrier_call(
        core, name=name, grid=(N_HEAD_BLOCKS, n // NA_STEP),
        in_specs=[tile_in,
                  pl.BlockSpec((s, LANES), lambda hb, rb: (0, K_COL + hb)),
                  pl.BlockSpec((s, LANES), lambda hb, rb: (0, V_COL + hb)),
                  pl.BlockSpec((l, LANES), lambda hb, rb: (cb, K_COL + hb)),
                  pl.BlockSpec((l, LANES), lambda hb, rb: (cb, V_COL + hb)),
                  tbs, tile_in, tile_in,
                  pl.BlockSpec((1, NA_STEP, LANES), lambda hb, rb: (hb, jnp.minimum(rb, x_steps - 1), 0))],
        out_specs=[pl.BlockSpec((NA_STEP, LANES), lambda hb, rb: (rb, hb)), whole_out, whole_out, tbs],
        out_shape=[f32n, f32n, f32n, jax.ShapeDtypeStruct((NA_KH, NA_HEADS, GRID_W, SLAB), F32)],
        inputs=[qkv, qkv, qkv, qkv, qkv, tb, o, dmix, lse], exchange=exchange)
    return outs, xo


def ctx_attn_fwd(qkv, na, s, name):
    n = qkv.shape[0]
    l = n - s
    cb = s // l

    def core(q_ref, k_ref, v_ref, na_in, o_ref, lse_ref):
        masks = _head_masks()
        qt, kb, vb = q_ref[...], k_ref[...], v_ref[...]
        o_acc = jnp.zeros((l, LANES), F32)
        lse_acc = jnp.zeros((l, LANES), F32)
        for h in range(HEADS_PER_BLOCK):
            qh = jnp.where(masks[h], qt, jnp.zeros_like(qt))
            sc = _dot(qh, kb, NT) * ATT_SCALE
            m = jnp.max(sc, axis=-1, keepdims=True)
            p = jnp.exp(sc - m)
            den = jnp.sum(p, axis=-1, keepdims=True)
            o_acc = jnp.where(masks[h], _dot(p.astype(BF16), vb) / den, o_acc)
            lse_acc = jnp.where(masks[h], m + jnp.log(den), lse_acc)
        o_ref[...] = o_acc.astype(BF16)
        lse_ref[0] = lse_acc

    outs, _ = carrier_call(
        core, name=name, grid=(N_HEAD_BLOCKS,),
        in_specs=[pl.BlockSpec((l, LANES), lambda hb: (cb, hb)), pl.BlockSpec((l, LANES), lambda hb: (cb, K_COL + hb)),
                  pl.BlockSpec((l, LANES), lambda hb: (cb, V_COL + hb)), ANY],
        out_specs=[pl.BlockSpec((l, LANES), lambda hb: (cb, hb)), pl.BlockSpec((1, l, LANES), lambda hb: (hb, 0, 0))],
        out_shape=[jax.ShapeDtypeStruct(na.shape, BF16), jax.ShapeDtypeStruct((N_HEAD_BLOCKS, l, LANES), F32)],
        inputs=[qkv, qkv, qkv, na], aliases={3: 0})
    return outs


def ctx_attn_bwd(qkv, na, dmix, lse, dq, dk, dv, s, name):
    n = qkv.shape[0]
    l = n - s
    cb = s // l

    def core(q_ref, k_ref, v_ref, o_ref, do_ref, lse_ref, dq_in, dk_in, dv_in, dq_ref, dk_ref, dv_ref):
        masks = _head_masks()
        qt, kb, vb = q_ref[...], k_ref[...], v_ref[...]
        do_f = do_ref[...]
        dd = do_f * o_ref[...].astype(F32)
        do_b = do_f.astype(BF16)
        lse_t = lse_ref[0]
        dq_acc = jnp.zeros((l, LANES), F32)
        dk_acc = jnp.zeros((l, LANES), F32)
        dv_acc = jnp.zeros((l, LANES), F32)
        for h in range(HEADS_PER_BLOCK):
            qh = jnp.where(masks[h], qt, jnp.zeros_like(qt))
            doh = jnp.where(masks[h], do_b, jnp.zeros_like(do_b))
            delta = jnp.sum(jnp.where(masks[h], dd, 0.0), axis=-1, keepdims=True)
            p = jnp.exp(_dot(qh, kb, NT) * ATT_SCALE - lse_t[:, h * HEAD_DIM:h * HEAD_DIM + 1])
            ds = (p * (_dot(doh, vb, NT) - delta)).astype(BF16)
            dq_acc = jnp.where(masks[h], _dot(ds, kb) * ATT_SCALE, dq_acc)
            dk_acc = dk_acc + _dot(ds, qh, TN)
            dv_acc = dv_acc + _dot(p.astype(BF16), doh, TN)
        dq_ref[...] = dq_acc
        dk_ref[...] = dk_in[...] + dk_acc * ATT_SCALE
        dv_ref[...] = dv_in[...] + dv_acc

    blk = pl.BlockSpec((l, LANES), lambda hb: (cb, hb))
    f32n = jax.ShapeDtypeStruct((n, NA_WIDTH), F32)
    outs, _ = carrier_call(
        core, name=name, grid=(N_HEAD_BLOCKS,),
        in_specs=[blk, pl.BlockSpec((l, LANES), lambda hb: (cb, K_COL + hb)),
                  pl.BlockSpec((l, LANES), lambda hb: (cb, V_COL + hb)), blk, blk,
                  pl.BlockSpec((1, l, LANES), lambda hb: (hb, 0, 0)), ANY, blk, blk],
        out_specs=[blk, blk, blk], out_shape=[f32n, f32n, f32n],
        inputs=[qkv, qkv, qkv, na, dmix, lse, dq, dk, dv], aliases={6: 0, 7: 1, 8: 2})
    return outs


def _pool_consts(l):
    assert l == TM
    mem = np.zeros((2, POOL_GROUPS, TM, TM), np.float32)
    inv = np.zeros((2, POOL_GROUPS, TM, LANES), np.float32)
    for which, length in ((0, GRID_W), (1, l)):
        t = np.arange(length)
        for g, w in enumerate(POOL_WINDOWS):
            lo = np.clip(t - w // 2, 0, length)
            hi = np.clip(t - w // 2 + w, 0, length)
            blockm = ((t[None, :] >= lo[:, None]) & (t[None, :] < hi[:, None])).astype(np.float32)
            cnt = (hi - lo).astype(np.float32)
            for b in range(TM // length):
                mem[which, g, b * length:(b + 1) * length, b * length:(b + 1) * length] = blockm
                inv[which, g, b * length:(b + 1) * length, :] = (1.0 / cnt)[:, None]
    return mem, np.ascontiguousarray(mem.transpose(0, 1, 3, 2)), inv


def _split_dot(m01, val):
    hi = val.astype(BF16)
    lo = (val - hi.astype(F32)).astype(BF16)
    return _dot(m01, hi) + _dot(m01, lo)


def pool_fwd(u, mem, inv, wp, scale, nx_tiles, name):
    n = u.shape[0]

    def core(u_ref, m_ref, i_ref, wp_ref, s_ref, o_ref):
        for g in range(POOL_GROUPS):
            sl = slice(g * POOL_CH, (g + 1) * POOL_CH)
            ug = u_ref[:, sl]
            dg = _split_dot(m_ref[0, g], ug) * i_ref[0, g] - ug
            o_ref[:, sl] = (_dot(dg.astype(BF16), wp_ref[g]) * s_ref[:, sl]).astype(BF16)

    grp = lambda i: (i // nx_tiles, 0, 0, 0)
    outs, _ = carrier_call(
        core, name=name, grid=(n // TM,),
        in_specs=[pl.BlockSpec((TM, POOL_WIDTH), lambda i: (i, 0)),
                  pl.BlockSpec((1, POOL_GROUPS, TM, TM), grp),
                  pl.BlockSpec((1, POOL_GROUPS, TM, LANES), grp),
                  pl.BlockSpec((POOL_GROUPS, POOL_CH, POOL_CH), lambda i: (0, 0, 0)),
                  pl.BlockSpec((1, POOL_WIDTH), lambda i: (0, 0))],
        out_specs=[pl.BlockSpec((TM, POOL_WIDTH), lambda i: (i, 0))],
        out_shape=[jax.ShapeDtypeStruct((n, POOL_WIDTH), BF16)], inputs=[u, mem, inv, wp, scale])
    return outs[0]


def pool_bwd(dmix, u, mem, mem_t, inv, wp, scale, nx_tiles, name):
    n = u.shape[0]

    def core(dy_ref, u_ref, m_ref, mt_ref, i_ref, wp_ref, s_ref, du_ref, dwp_ref, dsc_ref):
        @pl.when(pl.program_id(0) == 0)
        def _():
            dwp_ref[...] = jnp.zeros_like(dwp_ref)
            dsc_ref[...] = jnp.zeros_like(dsc_ref)

        for g in range(POOL_GROUPS):
            sl = slice(g * POOL_CH, (g + 1) * POOL_CH)
            ug = u_ref[:, sl]
            dy = dy_ref[:, sl]
            dg = (_split_dot(m_ref[0, g], ug) * i_ref[0, g] - ug).astype(BF16)
            z = _dot(dg, wp_ref[g])
            dsc_ref[0:1, sl] += jnp.sum(dy * z, axis=0, keepdims=True)
            dz = (dy * s_ref[:, sl]).astype(BF16)
            dwp_ref[g] += _dot(dg, dz, TN)
            dd = _dot(dz, wp_ref[g], NT)
            du_ref[:, sl] = _split_dot(mt_ref[0, g], dd * i_ref[0, g]) - dd

    grp = lambda i: (i // nx_tiles, 0, 0, 0)
    outs, _ = carrier_call(
        core, name=name, grid=(n // TM,),
        in_specs=[pl.BlockSpec((TM, POOL_WIDTH), lambda i: (i, 1)),
                  pl.BlockSpec((TM, POOL_WIDTH), lambda i: (i, 0)),
                  pl.BlockSpec((1, POOL_GROUPS, TM, TM), grp),
                  pl.BlockSpec((1, POOL_GROUPS, TM, TM), grp),
                  pl.BlockSpec((1, POOL_GROUPS, TM, LANES), grp),
                  pl.BlockSpec((POOL_GROUPS, POOL_CH, POOL_CH), lambda i: (0, 0, 0)),
                  pl.BlockSpec((1, POOL_WIDTH), lambda i: (0, 0))],
        out_specs=[pl.BlockSpec((TM, POOL_WIDTH), lambda i: (i, 0)),
                   pl.BlockSpec((POOL_GROUPS, POOL_CH, POOL_CH), lambda i: (0, 0, 0)),
                   pl.BlockSpec((8, POOL_WIDTH), lambda i: (0, 0))],
        out_shape=[jax.ShapeDtypeStruct((n, POOL_WIDTH), F32),
                   jax.ShapeDtypeStruct((POOL_GROUPS, POOL_CH, POOL_CH), F32),
                   jax.ShapeDtypeStruct((8, POOL_WIDTH), F32)],
        inputs=[dmix, u, mem, mem_t, inv, wp, scale])
    return outs


MOD_ROWS = 16


def mod_fwd(cvecs, w, b, name):
    _, d = cvecs.shape
    cl = w.shape[2]
    tc = _pick(cl, (384, 128))

    def core(c_ref, w_ref, b_ref, o_ref):
        a = _silu(c_ref[...]).astype(BF16)
        o_ref[0] = _dot(a, w_ref[0].astype(BF16)) + b_ref[0]

    outs, _ = carrier_call(
        core, name=name, grid=(DEPTH, cl // tc),
        in_specs=[pl.BlockSpec((MOD_ROWS, d), lambda li, j: (0, 0)),
                  pl.BlockSpec((1, d, tc), lambda li, j: (li, 0, j)),
                  pl.BlockSpec((1, 1, tc), lambda li, j: (li, 0, j))],
        out_specs=[pl.BlockSpec((1, MOD_ROWS, tc), lambda li, j: (li, 0, j))],
        out_shape=[jax.ShapeDtypeStruct((DEPTH, MOD_ROWS, cl), F32)], inputs=[cvecs, w, b])
    return outs[0]


def mod_bwd(cvecs, dm, w, name):
    _, d = cvecs.shape
    cl = w.shape[2]
    tc = _pick(cl, (384, 128))

    def core(c_ref, dm_ref, w_ref, dw_ref, da_ref):
        @pl.when((pl.program_id(0) == 0) & (pl.program_id(1) == 0))
        def _():
            da_ref[...] = jnp.zeros_like(da_ref)

        a = _silu(c_ref[...]).astype(BF16)
        dmb = dm_ref[0].astype(BF16)
        dw_ref[0] = _dot(a, dmb, TN)
        da_ref[...] += _dot(dmb, w_ref[0].astype(BF16), NT)

    outs, _ = carrier_call(
        core, name=name, grid=(DEPTH, cl // tc),
        in_specs=[pl.BlockSpec((MOD_ROWS, d), lambda li, j: (0, 0)),
                  pl.BlockSpec((1, MOD_ROWS, tc), lambda li, j: (li, 0, j)),
                  pl.BlockSpec((1, d, tc), lambda li, j: (li, 0, j))],
        out_specs=[pl.BlockSpec((1, d, tc), lambda li, j: (li, 0, j)),
                   pl.BlockSpec((MOD_ROWS, d), lambda li, j: (0, 0))],
        out_shape=[jax.ShapeDtypeStruct((DEPTH, d, cl), F32), jax.ShapeDtypeStruct((MOD_ROWS, d), F32)],
        inputs=[cvecs, dm, w])
    return outs


def loss_head(y, target, name):
    n, d = y.shape
    s = target.shape[0]
    nt, nx = n // TM, s // TM

    def core(y_ref, t_ref, l_ref, dy_ref, acc_ref):
        i = pl.program_id(0)

        @pl.when(i == 0)
        def _():
            acc_ref[...] = jnp.zeros_like(acc_ref)

        @pl.when(i < nx)
        def _():
            e = y_ref[...] - t_ref[...]
            dy_ref[...] = e * (1.0 / d)
            acc_ref[...] += jnp.sum(e * e, axis=0, keepdims=True)

        @pl.when(i >= nx)
        def _():
            dy_ref[...] = jnp.zeros_like(dy_ref)

        @pl.when(i == nt - 1)
        def _():
            l_ref[...] = jnp.sum(acc_ref[...], axis=1, keepdims=True) * (0.5 / d)

    tile = pl.BlockSpec((TM, d), lambda i: (i, 0))
    outs, _ = carrier_call(
        core, name=name, grid=(nt,),
        in_specs=[tile, pl.BlockSpec((TM, d), lambda i: (jnp.minimum(i, nx - 1), 0))],
        out_specs=[pl.BlockSpec((1, 1), lambda i: (0, 0)), tile],
        out_shape=[jax.ShapeDtypeStruct((1, 1), F32), jax.ShapeDtypeStruct((n, d), F32)],
        scratch_shapes=[pltpu.VMEM((1, d), F32)], inputs=[y, target])
    return outs


def sum_devices(v, name):
    _, r, c = v.shape
    tr = _pick(r, (64, 8))

    def core(v_ref, o_ref):
        acc = v_ref[0]
        for p in range(1, N_DEV):
            acc = acc + v_ref[p]
        o_ref[...] = acc

    outs, _ = carrier_call(
        core, name=name, grid=(r // tr,), in_specs=[pl.BlockSpec((N_DEV, tr, c), lambda i: (0, i, 0))],
        out_specs=[pl.BlockSpec((tr, c), lambda i: (i, 0))], out_shape=[jax.ShapeDtypeStruct((r, c), F32)],
        inputs=[v])
    return outs[0]


def cctx_grad(parts, c_ctx, name):
    d = c_ctx.shape[1]

    def body(p_ref, c_ref, o_ref):
        acc = p_ref[0]
        for p in range(1, N_DEV):
            acc = acc + p_ref[p]
        o_ref[...] = acc[8:9, :] * _dsilu(c_ref[...])

    return pl.pallas_call(
        body, name=name, out_shape=jax.ShapeDtypeStruct((1, d), F32),
        in_specs=[pl.BlockSpec(memory_space=pltpu.VMEM)] * 2,
        out_specs=pl.BlockSpec(memory_space=pltpu.VMEM),
    )(parts, c_ctx)


def _adam_math(w, g, m, v):
    m2 = ADAM_B1 * m + (1.0 - ADAM_B1) * g
    v2 = ADAM_B2 * v + (1.0 - ADAM_B2) * (g * g)
    m_hat = m2 / (1.0 - ADAM_B1 ** ADAM_STEP)
    v_hat = v2 / (1.0 - ADAM_B2 ** ADAM_STEP)
    delta = -ADAM_LR * (m_hat / (jnp.sqrt(v_hat) + ADAM_EPS) + ADAM_WD * w)
    return delta, m2, v2


def adamw(w, g, m, v, name):
    r, c = w.shape
    tr = _pick(r, (256, 128, 64, 32, 16, 8, r))

    def core(w_ref, g_ref, m_ref, v_ref, d_ref, m2_ref, v2_ref):
        d_ref[...], m2_ref[...], v2_ref[...] = _adam_math(w_ref[...], g_ref[...], m_ref[...], v_ref[...])

    tile = pl.BlockSpec((tr, c), lambda i: (i, 0))
    out = jax.ShapeDtypeStruct((r, c), F32)
    outs, _ = carrier_call(core, name=name, grid=(r // tr,), in_specs=[tile] * 4, out_specs=[tile] * 3,
                           out_shape=[out] * 3, inputs=[w, g, m, v])
    return outs


def reduce_adamw(recv, w, m, v, name):
    r, c = w.shape
    tr = _pick(r, (256, 128, 64, 8))

    def core(recv_ref, w_ref, m_ref, v_ref, g_ref, d_ref, m2_ref, v2_ref):
        acc = recv_ref[0].astype(F32)
        for p in range(1, N_DEV):
            acc = acc + recv_ref[p].astype(F32)
        g_ref[...] = acc
        d_ref[...], m2_ref[...], v2_ref[...] = _adam_math(w_ref[...], acc, m_ref[...], v_ref[...])

    tile = pl.BlockSpec((tr, c), lambda i: (i, 0))
    out = jax.ShapeDtypeStruct((r, c), F32)
    outs, _ = carrier_call(
        core, name=name, grid=(r // tr,),
        in_specs=[pl.BlockSpec((N_DEV, tr, c), lambda i: (0, i, 0)), tile, tile, tile],
        out_specs=[tile] * 4, out_shape=[out] * 4, inputs=[recv, w, m, v])
    return outs


def kernel(x, c, ctx, c_ctx, w_mod, b_mod, norm_g, w_ffn_gate_up, w_ffn_down, w_in, w_out, na_rpb, w_pool, pool_scale, loss_target, m_c_ctx, m_w_mod, m_b_mod, m_norm_g, m_w_ffn_gate_up, m_w_ffn_down, m_w_in, m_w_out, m_na_rpb, m_w_pool, m_pool_scale, v_c_ctx, v_w_mod, v_b_mod, v_norm_g, v_w_ffn_gate_up, v_w_ffn_down, v_w_in, v_w_out, v_na_rpb, v_w_pool, v_pool_scale):
    s, d = x.shape[1], x.shape[2]
    l = ctx.shape[1]
    n = s + l
    nx = s // TM
    fq = w_ffn_gate_up.shape[-1]
    fr = w_ffn_down.shape[2]
    cl = w_mod.shape[2]
    dl = norm_g.shape[2]
    me = 4 * lax.axis_index("x") + 2 * lax.axis_index("y") + lax.axis_index("c")

    c_all = all_gather(c, "gather_c").reshape(N_DEV, d)
    cvecs = jnp.concatenate([c_all, c_ctx[None, :], jnp.zeros((MOD_ROWS - N_DEV - 1, d), F32)], axis=0)
    b_loc = lax.dynamic_slice(b_mod, (0, me * cl), (DEPTH, cl)).reshape(DEPTH, 1, cl)
    mod_loc = mod_fwd(cvecs, w_mod, b_loc, "mod_fwd")
    mod_all = all_gather(mod_loc.reshape(DEPTH * MOD_ROWS, cl), "gather_mod")
    mod_all = mod_all.reshape(N_DEV, DEPTH, MOD_ROWS, cl).transpose(1, 2, 0, 3).reshape(DEPTH, MOD_ROWS, N_DEV * cl)
    mine = lax.dynamic_slice(mod_all, (0, me, 0), (DEPTH, 1, N_DEV * cl))
    mods = jnp.concatenate([mine, mod_all[:, N_DEV:N_DEV + 1]], axis=1).reshape(DEPTH, 2, N_MOD, d)

    gu_b = w_ffn_gate_up.astype(BF16)
    dn_b = w_ffn_down.astype(BF16)
    wi_b = w_in.astype(BF16)
    wo_b = w_out.astype(BF16)
    wp_b = w_pool.astype(BF16)

    def ffn_shards(li, i):
        return [gu_b[li, i], dn_b[li, i]]

    def mix_shards(li):
        return [wi_b[li], wo_b[li]]

    def as_ffn_weights(gathered):
        return gathered[0].reshape(2, 4, d, fq), gathered[1].reshape(4, 2 * fr, d)

    def as_mix_weights(gathered):
        return gathered[0], gathered[1].reshape(N_DEV * wo_b.shape[1], d)

    tables = _rope_tables(s, n)
    mem_np, mem_t_np, inv_np = _pool_consts(l)
    mem, mem_t, inv = jnp.asarray(mem_np, BF16), jnp.asarray(mem_t_np, BF16), jnp.asarray(inv_np)
    first = gather_two_level([norm_g.reshape(DEPTH * 6, dl), gu_b[0, 0]], "gather_first")
    g_full = first[0].reshape(N_DEV, DEPTH, 6, dl).transpose(1, 2, 0, 3).reshape(DEPTH, 6, 1, N_DEV * dl)

    weights = {("ffn", 0, 0): (first[1].reshape(2, 4, d, fq), None)}
    saved = {}
    xcur = jnp.concatenate([x[0], ctx[0]], axis=0)
    for li in range(DEPTH):
        last = li == DEPTH - 1
        for i in range(2):
            tag = f"l{li}_ffn{i}"
            wgu, wd4 = weights[("ffn", li, i)]
            if i == 0:
                ex_up = Exchange(gathers=mix_shards(li) + ([dn_b[0, 0]] if wd4 is None else []))
                ex_dn = Exchange(gathers=[dn_b[li, 1]])
            elif not last:
                ex_up, ex_dn = Exchange(gathers=[gu_b[li + 1, 0]]), Exchange(gathers=[dn_b[li + 1, 0]])
            else:
                ex_up = ex_dn = None
            (hb, gu, a4), got_up = ffn_up(xcur, g_full[li, 4 * i], mods[li], wgu, s, 6 * i, tag + "_up", ex_up)
            if wd4 is None:
                wd4 = got_up.pop().reshape(4, 2 * fr, d)
                weights[("ffn", li, i)] = (wgu, wd4)
            (ff, xnext), got_dn = ffn_down(a4, wd4, xcur, g_full[li, 4 * i + 1], mods[li], s, 6 * i + 2, tag + "_down", ex_dn)
            saved[("ffn", li, i)] = (xcur, hb, gu, a4, ff)
            xcur = xnext
            if i == 0:
                weights[("mix", li)] = as_mix_weights(got_up)
                next_dn = got_dn
            elif not last:
                weights[("ffn", li + 1, 0)] = as_ffn_weights(got_up + got_dn)
            if i == 0:
                tag = f"l{li}_mix"
                win8, wout = weights[("mix", li)]
                (hb, qkv, u), _ = mix_in(xcur, g_full[li, 2], mods[li], win8, tables, s, tag + "_in")
                tb = bias_tables(na_rpb[li], tag + "_bias")
                (na, lse), got = na_fwd(qkv, tb, s, tag + "_na", Exchange(gathers=[gu_b[li, 1]]))
                weights[("ffn", li, 1)] = as_ffn_weights(got + next_dn)
                lse_c = None
                if not last:
                    na, lse_c = ctx_attn_fwd(qkv, na, s, tag + "_ctx_attn")
                py = pool_fwd(u, mem, inv, wp_b[li], pool_scale[li][None, :], nx, tag + "_pool")
                fm, xnext = mix_out(na, py, wout, xcur, g_full[li, 3], mods[li], s, tag + "_out")
                saved[("mix", li)] = (xcur, hb, qkv, u, tb, na, lse, lse_c, py, fm)
                xcur = xnext

    loss_local, dcur = loss_head(xcur, loss_target[0], "loss")
    loss = lax.psum(loss_local[0, 0], ("x", "y", "c"))

    recv = {"gu": lax.empty((N_DEV, 2 * DEPTH, d, fq), BF16), "dn": lax.empty((N_DEV, 2 * DEPTH, fr, d), BF16),
            "wi": lax.empty((N_DEV, DEPTH, d, IN_BLOCK), BF16), "wo": lax.empty((N_DEV, DEPTH, wo_b.shape[1], d), BF16)}
    pending = []

    def take(keys, gathers=()):
        nonlocal pending
        jobs = [(gr, recv[key], st) for key, gr, st in pending if key in keys]
        order = [key for key, _, _ in pending if key in keys]
        pending = [p for p in pending if p[0] not in keys]
        return Exchange(gathers=gathers, a2as=jobs), order

    def pad8(t):
        t = t.reshape(-1, d) if t.size % d == 0 else jnp.pad(t.reshape(-1), (0, -t.size % d)).reshape(-1, d)
        return jnp.pad(t, ((0, -t.shape[0] % 8), (0, 0)))

    def packed(parts):
        parts = [pad8(p) for p in parts]
        offs = np.cumsum([0] + [p.shape[0] for p in parts])
        return jnp.concatenate(parts + [jnp.zeros((-offs[-1] % 64, d), F32)], axis=0), offs

    def put(order, bufs):
        for key, buf in zip(order, bufs):
            recv[key] = buf

    d_rpb, d_wp, d_ps, d_mod, d_g = [], [], [], [], []
    for li in reversed(range(DEPTH)):
        reds = {}
        for i in (1, 0):
            tag = f"l{li}_ffn{i}"
            xin, hb, gu, a4, ff = saved[("ffn", li, i)]
            wgu, wd4 = weights[("ffn", li, i)]
            dff, red1 = post_bwd(ff, dcur, g_full[li, 4 * i + 1], mods[li], 6 * i + 2, 0.5, s, tag + "_post_bwd")
            early = []
            if (li, i) == (0, 0):
                early_small, early_offs = packed([jnp.stack(d_wp), jnp.stack(d_ps), jnp.stack(d_rpb)])
                early = [early_small]
            ex, _ = take((), early)
            g_dn, bufs = grad_weight(a4, dff, tag + "_dwdown", a_lead=4, per_step=4, exchange=ex)
            if early:
                early_sum = sum_devices(bufs[0], "sum_early_small_grads")
            pending += [("dn", g_dn.reshape(N_DEV, fr, d), 2 * li + i)]
            ex, order = take(("dn",))
            dgu, bufs = ffn_da(dff, wd4, gu, tag + "_da", ex)
            put(order, bufs)
            ex, order = take(("wi", "wo"))
            g_gu, bufs = grad_weight(hb, dgu.reshape(N_DEV, n, fq), tag + "_dwgu", b_lead=N_DEV, per_step=4, exchange=ex)
            put(order, bufs)
            pending += [("gu", g_gu, 2 * li + i)]
            ex, order = take(("gu",))
            (dcur, red2), bufs = ffn_dh(dgu, wgu, xin, dcur, g_full[li, 4 * i], mods[li], s, 6 * i, tag + "_dh", ex)
            put(order, bufs)
            reds[i] = (red1, red2)
            if i == 1:
                tag = f"l{li}_mix"
                xin, hb, qkv, u, tb, na, lse, lse_c, py, fm = saved[("mix", li)]
                win8, wout = weights[("mix", li)]
                dfm, redm1 = post_bwd(fm, dcur, g_full[li, 3], mods[li], 5, 1.0, s, tag + "_post_bwd")
                dmix = matmul_nt(dfm, wout, tag + "_dmix")
                g_wo = grad_wout(na, py, dfm, tag + "_dwout").reshape(N_DEV, wo_b.shape[1], d)
                du, gwp, gps = pool_bwd(dmix, u, mem, mem_t, inv, wp_b[li], pool_scale[li][None, :], nx, tag + "_pool_bwd")
                ex, order = take(("gu", "dn"))
                (dq, dk, dv, dtb), bufs = na_bwd(qkv, tb, na, dmix, lse, s, tag + "_na_bwd", ex)
                put(order, bufs)
                if li != DEPTH - 1:
                    dq, dk, dv = ctx_attn_bwd(qkv, na, dmix, lse_c, dq, dk, dv, s, tag + "_ctx_attn_bwd")
                grpb = bias_tables_bwd(dtb, tag + "_bias_bwd")
                dqkvu = qkv_bwd(dq, dk, dv, du, tables, tag + "_rope_bwd")
                (dcur, redm2), _ = mix_dh(dqkvu, win8, xin, dcur, g_full[li, 2], mods[li], s, tag + "_dh")
                g_wi, _ = grad_weight(hb, dqkvu, tag + "_dwin", b_cols=IN_BLOCK, per_step=4)
                pending += [("wi", g_wi, li), ("wo", g_wo, li)]
                d_rpb.insert(0, grpb)
                d_wp.insert(0, gwp)
                d_ps.insert(0, gps[0])
        (ra1, ra2), (rb1, rb2) = reds[0], reds[1]
        d_mod.insert(0, jnp.stack([ra2[:, 0], ra2[:, 1], ra1[:, 0], redm2[:, 0], redm2[:, 1], redm1[:, 0],
                                   rb2[:, 0], rb2[:, 1], rb1[:, 0]], axis=1))
        d_g.insert(0, jnp.stack([t[0] + t[1] for t in (ra2[:, 2], ra1[:, 1], redm2[:, 2], redm1[:, 1], rb2[:, 2], rb1[:, 1])]))
    grad_x = dcur[:s][None]

    small, offs = packed([jnp.stack(d_mod), jnp.stack(d_g)])
    ex, order = take(("gu", "dn", "wi", "wo"), [small])
    bufs = exchange_only(ex, "exchange_last")
    small_all = bufs[0]
    put(order, bufs[1:])
    small_sum = sum_devices(small_all, "sum_small_grads")

    n_mod_rows = DEPTH * 2 * N_MOD
    dmod_all = small_all[:, :n_mod_rows].reshape(N_DEV, DEPTH, 2, N_MOD * d)
    dmod_sum = small_sum[:n_mod_rows].reshape(DEPTH, 2, N_MOD * d)
    dm_rows = jnp.concatenate([dmod_all[:, :, 0].transpose(1, 0, 2), dmod_sum[:, 1:2],
                               jnp.zeros((DEPTH, MOD_ROWS - N_DEV - 1, N_MOD * d), F32)], axis=1)
    grad_b_mod = dmod_sum[:, 0] + dmod_sum[:, 1]
    dm_loc = lax.dynamic_slice(dm_rows, (0, 0, me * cl), (DEPTH, MOD_ROWS, cl))
    grad_w_mod, da_part = mod_bwd(cvecs, dm_loc, w_mod, "mod_bwd")
    da_all = all_gather(da_part, "gather_dcvec")
    grad_c_ctx = cctx_grad(da_all, c_ctx[None, :], "c_ctx_grad")[0]

    grad_norm_full = small_sum[offs[1]:offs[1] + DEPTH * 6].reshape(DEPTH, 6, d)
    grad_norm_g = lax.dynamic_slice(grad_norm_full, (0, 0, me * dl), (DEPTH, 6, dl))
    grad_w_pool = early_sum[early_offs[0]:early_offs[0] + w_pool.size // d].reshape(w_pool.shape)
    grad_pool_scale = early_sum[early_offs[1]:early_offs[1] + pool_scale.size // d].reshape(pool_scale.shape)
    grad_na_rpb = early_sum[early_offs[2]:early_offs[3]].reshape(-1)[:na_rpb.size].reshape(na_rpb.shape)

    def big_adam(key, w, m, v, name):
        shp = w.shape
        cols = shp[-1]
        outs = reduce_adamw(recv[key].reshape(N_DEV, -1, cols), w.reshape(-1, cols), m.reshape(-1, cols),
                            v.reshape(-1, cols), name)
        return tuple(t.reshape(shp) for t in outs)

    def small_adam(w, g, m, v, name):
        shp = w.shape
        cols = shp[-1]
        outs = adamw(w.reshape(-1, cols), g.reshape(-1, cols), m.reshape(-1, cols), v.reshape(-1, cols), name)
        return tuple(t.reshape(shp) for t in outs)

    b_gu = big_adam("gu", w_ffn_gate_up, m_w_ffn_gate_up, v_w_ffn_gate_up, "adam_gate_up")
    b_dn = big_adam("dn", w_ffn_down, m_w_ffn_down, v_w_ffn_down, "adam_down")
    b_wi = big_adam("wi", w_in, m_w_in, v_w_in, "adam_w_in")
    b_wo = big_adam("wo", w_out, m_w_out, v_w_out, "adam_w_out")
    a_cc = small_adam(c_ctx, grad_c_ctx, m_c_ctx, v_c_ctx, "adam_c_ctx")
    a_wm = small_adam(w_mod, grad_w_mod, m_w_mod, v_w_mod, "adam_w_mod")
    a_bm = small_adam(b_mod, grad_b_mod, m_b_mod, v_b_mod, "adam_b_mod")
    a_ng = small_adam(norm_g, grad_norm_g, m_norm_g, v_norm_g, "adam_norm_g")
    a_rp = small_adam(na_rpb, grad_na_rpb, m_na_rpb, v_na_rpb, "adam_na_rpb")
    a_wp = small_adam(w_pool, grad_w_pool, m_w_pool, v_w_pool, "adam_w_pool")
    a_ps = small_adam(pool_scale, grad_pool_scale, m_pool_scale, v_pool_scale, "adam_pool_scale")

    grads = (grad_c_ctx, grad_w_mod, grad_b_mod, grad_norm_g, b_gu[0], b_dn[0], b_wi[0], b_wo[0], grad_na_rpb, grad_w_pool, grad_pool_scale)
    deltas = (a_cc[0], a_wm[0], a_bm[0], a_ng[0], b_gu[1], b_dn[1], b_wi[1], b_wo[1], a_rp[0], a_wp[0], a_ps[0])
    new_m = (a_cc[1], a_wm[1], a_bm[1], a_ng[1], b_gu[2], b_dn[2], b_wi[2], b_wo[2], a_rp[1], a_wp[1], a_ps[1])
    new_v = (a_cc[2], a_wm[2], a_bm[2], a_ng[2], b_gu[3], b_dn[3], b_wi[3], b_wo[3], a_rp[2], a_wp[2], a_ps[2])
    return (loss, grad_x, *grads, *deltas, *new_m, *new_v)
```

```python
import functools
import math

import numpy as np
import jax
import jax.numpy as jnp
from jax import lax
from jax.experimental import pallas as pl
from jax.experimental.pallas import tpu as pltpu

F32 = jnp.float32
BF16 = jnp.bfloat16

N_DEV = 8
DEPTH = 2
GRID_W = 64
N_MOD = 9
NA_HEADS = 8
HEAD_DIM = 64
NA_WIDTH = NA_HEADS * HEAD_DIM
NA_KH = 8
NA_KW = 16
POOL_GROUPS = 4
POOL_CH = 128
POOL_WIDTH = POOL_GROUPS * POOL_CH
POOL_WINDOWS = (2, 4, 8, 16)
ROPE_THETA = 10000.0
ROPE_PAIRS = HEAD_DIM // 4
RMS_EPS = 1e-6
NEG_INF = -1e30
ATT_SCALE = HEAD_DIM ** -0.5

ADAM_LR = 0.001
ADAM_B1 = 0.9
ADAM_B2 = 0.999
ADAM_EPS = 1e-08
ADAM_WD = 0.01
ADAM_STEP = 10

TM = 256
LANES = 128
HEADS_PER_BLOCK = LANES // HEAD_DIM
N_HEAD_BLOCKS = NA_WIDTH // LANES
IN_BLOCK = 2 * LANES
N_QKV_BLOCKS = 3 * NA_WIDTH // IN_BLOCK
VMEM_LIMIT = 56 * 1024 * 1024
HIGHEST = lax.Precision.HIGHEST
MESH = pl.DeviceIdType.MESH
ANY = pl.BlockSpec(memory_space=pl.ANY)

NN = (((1,), (0,)), ((), ()))
NT = (((1,), (1,)), ((), ()))
TN = (((0,), (0,)), ((), ()))


def _pick(n, cands):
    for t in cands:
        if n % t == 0:
            return t
    raise ValueError(f"no tile for {n} among {cands}")


def _dot(a, b, dn=NN, precision=None):
    return lax.dot_general(a, b, dn, preferred_element_type=F32, precision=precision)


def _silu(x):
    return x * jax.nn.sigmoid(x)


def _dsilu(x):
    s = jax.nn.sigmoid(x)
    return s * (1.0 + x * (1.0 - s))


def _peer(mask):
    x, y, c = lax.axis_index("x"), lax.axis_index("y"), lax.axis_index("c")
    px = 1 - x if mask & 4 else x
    py = 1 - y if mask & 2 else y
    pc = 1 - c if mask & 1 else c
    return (px, py, pc), 4 * px + 2 * py + pc


class Exchange:
    def __init__(self, gathers=(), a2as=()):
        self.gathers = list(gathers)
        self.a2as = list(a2as)
        self.n_jobs = len(self.gathers) + len(self.a2as)

    def inputs(self):
        out = list(self.gathers)
        for v, buf, _ in self.a2as:
            out += [v, buf]
        return out

    def out_shapes(self):
        shapes = [jax.ShapeDtypeStruct((N_DEV,) + v.shape, v.dtype) for v in self.gathers]
        shapes += [jax.ShapeDtypeStruct(buf.shape, buf.dtype) for _, buf, _ in self.a2as]
        return shapes

    def aliases(self, n_in, n_out):
        ng = len(self.gathers)
        return {n_in + ng + 2 * k + 1: n_out + ng + k for k in range(len(self.a2as))}

    def scratch(self):
        per = N_DEV - 1
        return [pltpu.SemaphoreType.DMA((per * self.n_jobs,)), pltpu.SemaphoreType.DMA((per * self.n_jobs,)),
                pltpu.SemaphoreType.DMA((self.n_jobs,))]

    def _copies(self, in_refs, out_refs, sems, with_recvs):
        send_sems, recv_sems, local_sems = sems
        _, me = _peer(0)
        ng = len(self.gathers)
        local, sends, recvs = [], [], []
        for job in range(self.n_jobs):
            if job < ng:
                src_of = lambda pid, r=in_refs[job]: r
                dst_of = lambda pid, r=out_refs[job]: r.at[pid]
            else:
                k = job - ng
                stage = self.a2as[k][2]
                src_of = lambda pid, r=in_refs[ng + 2 * k]: r.at[pid]
                dst_of = lambda pid, r=out_refs[job], st=stage: r.at[pid, st]
            local.append(pltpu.make_async_copy(src_of(me), dst_of(me), local_sems.at[job]))
            for mask in range(1, N_DEV):
                peer, pid = _peer(mask)
                idx = job * (N_DEV - 1) + mask - 1
                sends.append(pltpu.make_async_remote_copy(
                    src_ref=src_of(pid), dst_ref=dst_of(me), send_sem=send_sems.at[idx],
                    recv_sem=recv_sems.at[idx], device_id=peer, device_id_type=MESH))
                if with_recvs:
                    recvs.append(pltpu.make_async_remote_copy(
                        src_ref=src_of(pid), dst_ref=dst_of(pid), send_sem=send_sems.at[idx],
                        recv_sem=recv_sems.at[idx], device_id=peer, device_id_type=MESH))
        return local, sends, recvs

    def start(self, in_refs, out_refs, sems):
        local, sends, _ = self._copies(in_refs, out_refs, sems, False)
        for cp in local + sends:
            cp.start()

    def wait(self, in_refs, out_refs, sems):
        local, sends, recvs = self._copies(in_refs, out_refs, sems, True)
        for cp in recvs:
            cp.wait_recv()
        for cp in sends:
            cp.wait_send()
        for cp in local:
            cp.wait()


def carrier_call(core, *, name, grid, in_specs, out_specs, out_shape, inputs, scratch_shapes=(), aliases=None,
                 exchange=None):
    aliases = dict(aliases or {})
    n_in, n_out, n_sc = len(in_specs), len(out_specs), len(scratch_shapes)
    sem = ("arbitrary",) * len(grid)
    params = pltpu.CompilerParams(dimension_semantics=sem, vmem_limit_bytes=VMEM_LIMIT)
    if exchange is None or exchange.n_jobs == 0:
        outs = pl.pallas_call(core, name=name, grid=grid, in_specs=list(in_specs), out_specs=tuple(out_specs),
                              out_shape=tuple(out_shape), scratch_shapes=list(scratch_shapes),
                              input_output_aliases=aliases, compiler_params=params)(*inputs)
        return list(outs), []
    x_in = exchange.inputs()
    x_out = exchange.out_shapes()
    aliases.update(exchange.aliases(n_in, n_out))

    def body(*refs):
        a = n_in + len(x_in)
        b = a + n_out + len(x_out)
        core_in, job_in = refs[:n_in], refs[n_in:a]
        core_out, job_out = refs[a:a + n_out], refs[a + n_out:b]
        core_sc, job_sc = refs[b:b + n_sc], refs[b + n_sc:]
        first = functools.reduce(lambda p, q: p & q, [pl.program_id(ax) == 0 for ax in range(len(grid))])
        last = functools.reduce(lambda p, q: p & q, [pl.program_id(ax) == g - 1 for ax, g in enumerate(grid)])

        @pl.when(first)
        def _():
            exchange.start(job_in, job_out, job_sc)

        core(*core_in, *core_out, *core_sc)

        @pl.when(last)
        def _():
            exchange.wait(job_in, job_out, job_sc)

    outs = pl.pallas_call(
        body, name=name, grid=grid, in_specs=list(in_specs) + [ANY] * len(x_in),
        out_specs=tuple(out_specs) + (ANY,) * len(x_out), out_shape=tuple(out_shape) + tuple(x_out),
        scratch_shapes=list(scratch_shapes) + exchange.scratch(), input_output_aliases=aliases,
        compiler_params=params)(*inputs, *x_in)
    return list(outs[:n_out]), list(outs[n_out:])


def exchange_only(exchange, name):
    def body(*refs):
        n_in, n_out = len(exchange.inputs()), len(exchange.out_shapes())
        job_in, job_out, sems = refs[:n_in], refs[n_in:n_in + n_out], refs[n_in + n_out:]
        exchange.start(job_in, job_out, sems)
        exchange.wait(job_in, job_out, sems)

    x_in = exchange.inputs()
    outs = pl.pallas_call(
        body, name=name, in_specs=[ANY] * len(x_in), out_specs=(ANY,) * len(exchange.out_shapes()),
        out_shape=tuple(exchange.out_shapes()), scratch_shapes=exchange.scratch(),
        input_output_aliases=exchange.aliases(0, 0))(*x_in)
    return list(outs)


def all_gather(v, name):
    return exchange_only(Exchange(gathers=[v]), name)[0]


def gather_two_level(vs, name):
    nv = len(vs)
    per = N_DEV - 1

    def body(*refs):
        v_refs, o_refs = refs[:nv], refs[nv:2 * nv]
        send_sems, recv_sems, local_sems = refs[2 * nv:]
        x, y, c = lax.axis_index("x"), lax.axis_index("y"), lax.axis_index("c")
        me, sibling = (x, y, c), (x, y, 1 - c)
        chips = [(1 - x, y), (x, 1 - y), (1 - x, 1 - y)]

        def copy(a, k, block, to, src=None):
            dst = o_refs[a].at[4 * block[0] + 2 * block[1] + block[2]]
            return pltpu.make_async_remote_copy(
                src_ref=dst if src is None else src, dst_ref=dst, send_sem=send_sems.at[a * per + k],
                recv_sem=recv_sems.at[a * per + k], device_id=to, device_id_type=MESH)

        mine = [pltpu.make_async_copy(v_refs[a], o_refs[a].at[4 * x + 2 * y + c], local_sems.at[a]) for a in range(nv)]
        first = []
        for a in range(nv):
            first.append(copy(a, 0, me, sibling, src=v_refs[a]))
            first += [copy(a, 1 + j, me, (*chip, c), src=v_refs[a]) for j, chip in enumerate(chips)]
        for cp in mine + first:
            cp.start()
        passed = []
        for a in range(nv):
            for j, chip in enumerate(chips):
                copy(a, 1 + j, (*chip, c), me).wait_recv()
                passed.append(copy(a, 4 + j, (*chip, c), sibling))
                passed[-1].start()
        for a in range(nv):
            copy(a, 0, sibling, me).wait_recv()
            for j, chip in enumerate(chips):
                copy(a, 4 + j, (*chip, 1 - c), me).wait_recv()
        for cp in first + passed:
            cp.wait_send()
        for cp in mine:
            cp.wait()

    outs = pl.pallas_call(
        body, name=name, in_specs=[ANY] * nv, out_specs=(ANY,) * nv,
        out_shape=tuple(jax.ShapeDtypeStruct((N_DEV,) + v.shape, v.dtype) for v in vs),
        scratch_shapes=[pltpu.SemaphoreType.DMA((per * nv,)), pltpu.SemaphoreType.DMA((per * nv,)),
                        pltpu.SemaphoreType.DMA((nv,))])(*vs)
    return list(outs)


def _rms(xf):
    return lax.rsqrt(jnp.mean(xf * xf, axis=-1, keepdims=True) + RMS_EPS)


def _is_ctx(i, tm, s):
    return (i * tm + lax.broadcasted_iota(jnp.int32, (tm, 1), 0)) >= s


def _by_tile_kind(i, tm, s, fn):
    n_latent = s // tm

    @pl.when(i < n_latent)
    def _():
        fn(None)

    @pl.when(i >= n_latent)
    def _():
        fn(_is_ctx(i, tm, s))


def _mod_rows(mod_ref, k, is_ctx):
    if is_ctx is None:
        return mod_ref[0, k:k + 1, :]
    return jnp.where(is_ctx, mod_ref[1, k:k + 1, :], mod_ref[0, k:k + 1, :])


def _norm_mod(xf, g, mod_ref, k_shift, k_scale, is_ctx):
    nrm = xf * _rms(xf) * g
    return (nrm * (1.0 + _mod_rows(mod_ref, k_scale, is_ctx)) + _mod_rows(mod_ref, k_shift, is_ctx)).astype(BF16)


def _post(xf, ff, g, mod_ref, k_gate, coef, is_ctx):
    return xf + coef * _mod_rows(mod_ref, k_gate, is_ctx) * (ff * _rms(ff) * g)


def _red_add(red_ref, first, is_ctx, rows):
    @pl.when(first)
    def _():
        red_ref[...] = jnp.zeros_like(red_ref)

    for r, val in enumerate(rows):
        tot = jnp.sum(val, axis=0, keepdims=True)
        if is_ctx is None:
            red_ref[0, r:r + 1, :] += tot
        else:
            ctx = jnp.sum(jnp.where(is_ctx, val, 0.0), axis=0, keepdims=True)
            red_ref[0, r:r + 1, :] += tot - ctx
            red_ref[1, r:r + 1, :] += ctx


def _pre_bwd(xf, dh, dxo, g, mod_ref, k_scale, is_ctx):
    r = _rms(xf)
    xhat = xf * r
    dn = dh * (1.0 + _mod_rows(mod_ref, k_scale, is_ctx))
    dxhat = dn * g
    dx = dxo + r * (dxhat - xhat * jnp.mean(dxhat * xhat, axis=-1, keepdims=True))
    return dx, [dh, dh * xhat]


def _finish_pre(red_ref, g, mod_ref, k_scale):
    sums = red_ref[:, 1:2, :]
    red_ref[:, 2:3, :] = (1.0 + mod_ref[:, k_scale:k_scale + 1, :]) * sums
    red_ref[:, 1:2, :] = g * sums


def _vec_spec(d):
    return pl.BlockSpec((1, d), lambda *_: (0, 0))


def _mod_whole(d):
    return pl.BlockSpec((2, N_MOD, d), lambda *_: (0, 0, 0))


def _red_whole(d):
    return pl.BlockSpec((2, 8, d), lambda *_: (0, 0, 0))


def _token_tile(n):
    return _pick(n, (768, 640, 512, 384, 256))


def _ffn_tile(n):
    return _pick(n, (528, 384, 640, 256))


def _resident(shape):
    zeros = (0,) * len(shape)
    return pl.BlockSpec(shape, lambda i: zeros, pipeline_mode=pl.Buffered(1))


def ffn_up(x, g, mod, wgu, s, k0, name, exchange=None):
    n, d = x.shape
    nk, fq = wgu.shape[1], wgu.shape[-1]
    tm = _ffn_tile(n)

    def core(x_ref, g_ref, mod_ref, w_ref, hb_ref, gu_ref, a_ref):
        i = pl.program_id(0)

        def prologue(is_ctx):
            hb_ref[...] = _norm_mod(x_ref[...], g_ref[...], mod_ref, k0, k0 + 1, is_ctx)

        _by_tile_kind(i, tm, s, prologue)
        h = hb_ref[...]
        for k in range(nk):
            gg = _dot(h, w_ref[0, k])
            uu = _dot(h, w_ref[1, k])
            gu_ref[0, k] = gg.astype(BF16)
            gu_ref[1, k] = uu.astype(BF16)
            a_ref[k] = (_silu(gg) * uu).astype(BF16)

    outs, xo = carrier_call(
        core, name=name, grid=(n // tm,),
        in_specs=[pl.BlockSpec((tm, d), lambda i: (i, 0)), _vec_spec(d), _mod_whole(d), _resident(wgu.shape)],
        out_specs=[pl.BlockSpec((tm, d), lambda i: (i, 0)),
                   pl.BlockSpec((2, nk, tm, fq), lambda i: (0, 0, i, 0)),
                   pl.BlockSpec((nk, tm, fq), lambda i: (0, i, 0))],
        out_shape=[jax.ShapeDtypeStruct((n, d), BF16), jax.ShapeDtypeStruct((2, nk, n, fq), BF16),
                   jax.ShapeDtypeStruct((nk, n, fq), BF16)],
        inputs=[x, g, mod, wgu], exchange=exchange)
    return outs, xo


def ffn_down(a4, wd4, x, g, mod, s, k_gate, name, exchange=None):
    n, d = x.shape
    nk, fq = wd4.shape[0], wd4.shape[1]
    tm = _ffn_tile(n)

    def core(a_ref, w_ref, x_ref, g_ref, mod_ref, f_ref, xo_ref):
        i = pl.program_id(0)
        ff = _dot(a_ref[0], w_ref[0])
        for k in range(1, nk):
            ff = ff + _dot(a_ref[k], w_ref[k])
        f_ref[...] = ff

        def epilogue(is_ctx):
            xo_ref[...] = _post(x_ref[...], f_ref[...], g_ref[...], mod_ref, k_gate, 0.5, is_ctx)

        _by_tile_kind(i, tm, s, epilogue)

    tile = pl.BlockSpec((tm, d), lambda i: (i, 0))
    outs, xo = carrier_call(
        core, name=name, grid=(n // tm,),
        in_specs=[pl.BlockSpec((nk, tm, fq), lambda i: (0, i, 0)), _resident(wd4.shape), tile, _vec_spec(d),
                  _mod_whole(d)],
        out_specs=[tile, tile],
        out_shape=[jax.ShapeDtypeStruct((n, d), F32), jax.ShapeDtypeStruct((n, d), F32)],
        inputs=[a4, wd4, x, g, mod], exchange=exchange)
    return outs, xo


def ffn_da(df, wd4, gu, name, exchange=None):
    n, d = df.shape
    nk, fq = wd4.shape[0], wd4.shape[1]
    tm = _ffn_tile(n)

    def core(df_ref, w_ref, gu_ref, o_ref):
        dfv = df_ref[...]
        for k in range(nk):
            da = _dot(dfv, w_ref[k], NT).astype(BF16)
            gg = gu_ref[0, k]
            uu = gu_ref[1, k]
            sg = jax.nn.sigmoid(gg.astype(F32)).astype(BF16)
            o_ref[0, k] = da * (uu * (sg * (1 + gg * (1 - sg))))
            o_ref[1, k] = da * (gg * sg)

    gu_spec = pl.BlockSpec((2, nk, tm, fq), lambda i: (0, 0, i, 0))
    outs, xo = carrier_call(
        core, name=name, grid=(n // tm,),
        in_specs=[pl.BlockSpec((tm, d), lambda i: (i, 0)), _resident(wd4.shape), gu_spec],
        out_specs=[gu_spec], out_shape=[jax.ShapeDtypeStruct(gu.shape, BF16)],
        inputs=[df, wd4, gu], exchange=exchange)
    return outs[0], xo


def ffn_dh(dgu, wgu, x, dxo, g, mod, s, k0, name, exchange=None):
    n, d = x.shape
    nk, fq = wgu.shape[1], wgu.shape[-1]
    tm = _ffn_tile(n)

    def core(dgu_ref, w_ref, x_ref, dxo_ref, g_ref, mod_ref, dx_ref, red_ref, dh_s):
        i = pl.program_id(0)
        dh = _dot(dgu_ref[0, 0], w_ref[0, 0], NT) + _dot(dgu_ref[1, 0], w_ref[1, 0], NT)
        for k in range(1, nk):
            dh = dh + _dot(dgu_ref[0, k], w_ref[0, k], NT) + _dot(dgu_ref[1, k], w_ref[1, k], NT)
        dh_s[...] = dh

        def epilogue(is_ctx):
            dx, sums = _pre_bwd(x_ref[...], dh_s[...], dxo_ref[...], g_ref[...], mod_ref, k0 + 1, is_ctx)
            dx_ref[...] = dx
            _red_add(red_ref, i == 0, is_ctx, sums)

        _by_tile_kind(i, tm, s, epilogue)

        @pl.when(i == n // tm - 1)
        def _():
            _finish_pre(red_ref, g_ref[...], mod_ref, k0 + 1)

    tile = pl.BlockSpec((tm, d), lambda i: (i, 0))
    outs, xo = carrier_call(
        core, name=name, grid=(n // tm,),
        in_specs=[pl.BlockSpec((2, nk, tm, fq), lambda i: (0, 0, i, 0)), _resident(wgu.shape), tile, tile,
                  _vec_spec(d), _mod_whole(d)],
        out_specs=[tile, _red_whole(d)],
        out_shape=[jax.ShapeDtypeStruct((n, d), F32), jax.ShapeDtypeStruct((2, 8, d), F32)],
        scratch_shapes=[pltpu.VMEM((tm, d), F32)], inputs=[dgu, wgu, x, dxo, g, mod], exchange=exchange)
    return outs, xo


def grad_weight(a, b, name, a_lead=None, b_lead=None, b_cols=None, per_step=1, exchange=None):
    n = a.shape[-2]
    ka = a.shape[-1]
    kb = b_cols or b.shape[-1]
    nj = a_lead or b_lead or (b.shape[-1] // b_cols)
    tk = _pick(n, (1408, 1024, 768, 640, 512, 256))
    nk = n // tk
    ps = per_step
    assert nj % ps == 0

    def core(a_ref, b_ref, o_ref, acc):
        kk = pl.program_id(1)

        @pl.when(kk == 0)
        def _():
            acc[...] = jnp.zeros_like(acc)

        if b_cols:
            acc[...] += _dot(a_ref[...], b_ref[...], TN)
        else:
            for t in range(ps):
                acc[t] += _dot(a_ref[t] if a_lead else a_ref[...], b_ref[t] if b_lead else b_ref[...], TN)

        @pl.when(kk == nk - 1)
        def _():
            for t in range(ps):
                o_ref[t] = (acc[:, t * kb:(t + 1) * kb] if b_cols else acc[t]).astype(BF16)

    a_spec = (pl.BlockSpec((ps, tk, ka), lambda j, kk: (j, kk, 0)) if a_lead
              else pl.BlockSpec((tk, ka), lambda j, kk: (kk, 0)))
    if b_lead:
        b_spec = pl.BlockSpec((ps, tk, kb), lambda j, kk: (j, kk, 0))
    elif b_cols:
        b_spec = pl.BlockSpec((tk, ps * kb), lambda j, kk: (kk, j))
    else:
        b_spec = pl.BlockSpec((tk, kb), lambda j, kk: (kk, 0))
    outs, xo = carrier_call(
        core, name=name, grid=(nj // ps, nk), in_specs=[a_spec, b_spec],
        out_specs=[pl.BlockSpec((ps, ka, kb), lambda j, kk: (j, 0, 0))],
        out_shape=[jax.ShapeDtypeStruct((nj, ka, kb), BF16)],
        scratch_shapes=[pltpu.VMEM((ka, ps * kb) if b_cols else (ps, ka, kb), F32)], inputs=[a, b], exchange=exchange)
    return outs[0], xo


def post_bwd(f, dxo, g, mod, k_gate, coef, s, name):
    n, d = f.shape

    def core(f_ref, dxo_ref, g_ref, mod_ref, df_ref, red_ref):
        i = pl.program_id(0)

        def body(is_ctx):
            ff = f_ref[...]
            dxo_ = dxo_ref[...]
            r = _rms(ff)
            fn = ff * r
            cg = (coef * _mod_rows(mod_ref, k_gate, is_ctx)) * g_ref[...]
            w = dxo_ * fn
            dfn = dxo_ * cg
            df_ref[...] = (r * (dfn - fn * jnp.mean(w * cg, axis=-1, keepdims=True))).astype(BF16)
            _red_add(red_ref, i == 0, is_ctx, [w])

        _by_tile_kind(i, TM, s, body)

        @pl.when(i == n // TM - 1)
        def _():
            sums = red_ref[:, 0:1, :]
            red_ref[:, 1:2, :] = (coef * mod_ref[:, k_gate:k_gate + 1, :]) * sums
            red_ref[:, 0:1, :] = (coef * g_ref[...]) * sums

    tile = pl.BlockSpec((TM, d), lambda i: (i, 0))
    outs, _ = carrier_call(
        core, name=name, grid=(n // TM,), in_specs=[tile, tile, _vec_spec(d), _mod_whole(d)],
        out_specs=[tile, _red_whole(d)],
        out_shape=[jax.ShapeDtypeStruct((n, d), BF16), jax.ShapeDtypeStruct((2, 8, d), F32)],
        inputs=[f, dxo, g, mod])
    return outs


def matmul_nt(a, b, name):
    m, k = a.shape
    n = b.shape[0]
    tm = _token_tile(m)

    def core(a_ref, b_ref, o_ref):
        o_ref[...] = _dot(a_ref[...], b_ref[...], NT)

    outs, _ = carrier_call(
        core, name=name, grid=(m // tm,),
        in_specs=[pl.BlockSpec((tm, k), lambda i: (i, 0)), pl.BlockSpec((n, k), lambda i: (0, 0))],
        out_specs=[pl.BlockSpec((tm, n), lambda i: (i, 0))], out_shape=[jax.ShapeDtypeStruct((m, n), F32)],
        inputs=[a, b])
    return outs[0]


def _rope_tables(s, n):
    t = jnp.arange(n)
    lane = jnp.arange(LANES)
    dd = lane % HEAD_DIM
    inv = ROPE_THETA ** (-(dd % ROPE_PAIRS).astype(F32) / ROPE_PAIRS)
    pos = jnp.where(dd[None, :] < HEAD_DIM // 2, (t // GRID_W)[:, None], (t % GRID_W)[:, None]).astype(F32)
    ang = pos * inv[None, :]
    live = (t < s)[:, None]
    first = ((dd % (2 * ROPE_PAIRS)) < ROPE_PAIRS)[None, :]
    cos = jnp.where(live, jnp.cos(ang), 1.0)
    sin = jnp.where(live, jnp.sin(ang), 0.0)
    sa = jnp.where(first, -sin, 0.0)
    sb = jnp.where(first, 0.0, sin)
    return cos.astype(F32), sa.astype(F32), sb.astype(F32)


def _rope(xv, cos, sa, sb):
    return (xv * cos + pltpu.roll(xv, LANES - ROPE_PAIRS, 1) * sa + pltpu.roll(xv, ROPE_PAIRS, 1) * sb)


def mix_in(x, g, mod, win8, tables, s, name, exchange=None):
    n, d = x.shape
    tm = _token_tile(n)
    nb = win8.shape[0]
    n_rope = 2 * NA_WIDTH // IN_BLOCK

    def core(x_ref, g_ref, mod_ref, w_ref, c_ref, sa_ref, sb_ref, hb_ref, qkv_ref, u_ref):
        i = pl.program_id(0)
        hb = _norm_mod(x_ref[...], g_ref[...], mod_ref, 3, 4, _is_ctx(i, tm, s))
        hb_ref[...] = hb
        cos, sa, sb = c_ref[...], sa_ref[...], sb_ref[...]
        for j in range(nb):
            y = _dot(hb, w_ref[j])
            if j < n_rope:
                for b in range(IN_BLOCK // LANES):
                    sl = slice(b * LANES, (b + 1) * LANES)
                    qkv_ref[:, j * IN_BLOCK + b * LANES:j * IN_BLOCK + (b + 1) * LANES] = (
                        _rope(y[:, sl], cos, sa, sb).astype(BF16))
            elif j < N_QKV_BLOCKS:
                qkv_ref[:, j * IN_BLOCK:(j + 1) * IN_BLOCK] = y.astype(BF16)
            else:
                u_ref[:, (j - N_QKV_BLOCKS) * IN_BLOCK:(j - N_QKV_BLOCKS + 1) * IN_BLOCK] = y

    tab = pl.BlockSpec((tm, LANES), lambda i: (i, 0))
    row = lambda w: pl.BlockSpec((tm, w), lambda i: (i, 0))
    outs, xo = carrier_call(
        core, name=name, grid=(n // tm,),
        in_specs=[row(d), _vec_spec(d), _mod_whole(d), pl.BlockSpec((nb, d, IN_BLOCK), lambda i: (0, 0, 0)),
                  tab, tab, tab],
        out_specs=[row(d), row(3 * NA_WIDTH), row(POOL_WIDTH)],
        out_shape=[jax.ShapeDtypeStruct((n, d), BF16), jax.ShapeDtypeStruct((n, 3 * NA_WIDTH), BF16),
                   jax.ShapeDtypeStruct((n, POOL_WIDTH), F32)],
        inputs=[x, g, mod, win8, *tables], exchange=exchange)
    return outs, xo


def qkv_bwd(dq, dk, dv, du, tables, name):
    n = dq.shape[0]
    w = NA_WIDTH

    def core(dq_ref, dk_ref, dv_ref, du_ref, c_ref, sa_ref, sb_ref, o_ref):
        cos, sa, sb = c_ref[...], -sa_ref[...], -sb_ref[...]
        for b in range(N_HEAD_BLOCKS):
            sl = slice(b * LANES, (b + 1) * LANES)
            o_ref[:, b * LANES:(b + 1) * LANES] = _rope(dq_ref[:, sl], cos, sa, sb).astype(BF16)
            o_ref[:, w + b * LANES:w + (b + 1) * LANES] = _rope(dk_ref[:, sl], cos, sa, sb).astype(BF16)
        o_ref[:, 2 * w:3 * w] = dv_ref[...].astype(BF16)
        o_ref[:, 3 * w:] = du_ref[...].astype(BF16)

    tab = pl.BlockSpec((TM, LANES), lambda i: (i, 0))
    tile = pl.BlockSpec((TM, w), lambda i: (i, 0))
    outs, _ = carrier_call(
        core, name=name, grid=(n // TM,), in_specs=[tile, tile, tile, tile, tab, tab, tab],
        out_specs=[pl.BlockSpec((TM, 4 * w), lambda i: (i, 0))],
        out_shape=[jax.ShapeDtypeStruct((n, 4 * w), BF16)], inputs=[dq, dk, dv, du, *tables])
    return outs[0]


def mix_out(na, py, wout, x, g, mod, s, name):
    n, d = x.shape
    tm = _token_tile(n)
    half = na.shape[1]

    def core(na_ref, py_ref, w_ref, x_ref, g_ref, mod_ref, f_ref, xo_ref):
        i = pl.program_id(0)
        ff = _dot(na_ref[...], w_ref[:half, :]) + _dot(py_ref[...], w_ref[half:, :])
        f_ref[...] = ff
        xo_ref[...] = _post(x_ref[...], ff, g_ref[...], mod_ref, 5, 1.0, _is_ctx(i, tm, s))

    tile = pl.BlockSpec((tm, d), lambda i: (i, 0))
    htile = pl.BlockSpec((tm, half), lambda i: (i, 0))
    outs, _ = carrier_call(
        core, name=name, grid=(n // tm,),
        in_specs=[htile, htile, pl.BlockSpec((2 * half, d), lambda i: (0, 0)), tile, _vec_spec(d), _mod_whole(d)],
        out_specs=[tile, tile],
        out_shape=[jax.ShapeDtypeStruct((n, d), F32), jax.ShapeDtypeStruct((n, d), F32)],
        inputs=[na, py, wout, x, g, mod])
    return outs


def grad_wout(na, py, dfm, name):
    n, half = na.shape
    d = dfm.shape[1]
    tk = _pick(n, (1408, 1024, 768, 640, 512, 256))
    nk = n // tk

    def core(na_ref, py_ref, b_ref, o_ref, acc):
        hh, kk = pl.program_id(0), pl.program_id(1)

        @pl.when(kk == 0)
        def _():
            acc[...] = jnp.zeros_like(acc)

        @pl.when(hh == 0)
        def _():
            acc[...] += _dot(na_ref[...], b_ref[...], TN)

        @pl.when(hh == 1)
        def _():
            acc[...] += _dot(py_ref[...], b_ref[...], TN)

        @pl.when(kk == nk - 1)
        def _():
            o_ref[0] = acc[...].astype(BF16)

    htile = pl.BlockSpec((tk, half), lambda hh, kk: (kk, 0))
    outs, _ = carrier_call(
        core, name=name, grid=(2, nk), in_specs=[htile, htile, pl.BlockSpec((tk, d), lambda hh, kk: (kk, 0))],
        out_specs=[pl.BlockSpec((1, half, d), lambda hh, kk: (hh, 0, 0))],
        out_shape=[jax.ShapeDtypeStruct((2, half, d), BF16)],
        scratch_shapes=[pltpu.VMEM((half, d), F32)], inputs=[na, py, dfm])
    return outs[0]


def mix_dh(dqkvu, win8, x, dxo, g, mod, s, name, exchange=None):
    n, d = x.shape
    tm = _token_tile(n)
    nb = win8.shape[0]

    def core(dq_ref, w_ref, x_ref, dxo_ref, g_ref, mod_ref, dx_ref, red_ref):
        i = pl.program_id(0)
        dh = _dot(dq_ref[:, :IN_BLOCK], w_ref[0], NT)
        for j in range(1, nb):
            dh = dh + _dot(dq_ref[:, j * IN_BLOCK:(j + 1) * IN_BLOCK], w_ref[j], NT)
        is_ctx = _is_ctx(i, tm, s)
        dx, sums = _pre_bwd(x_ref[...], dh, dxo_ref[...], g_ref[...], mod_ref, 4, is_ctx)
        dx_ref[...] = dx
        _red_add(red_ref, i == 0, is_ctx, sums)

        @pl.when(i == n // tm - 1)
        def _():
            _finish_pre(red_ref, g_ref[...], mod_ref, 4)

    tile = pl.BlockSpec((tm, d), lambda i: (i, 0))
    outs, xo = carrier_call(
        core, name=name, grid=(n // tm,),
        in_specs=[pl.BlockSpec((tm, nb * IN_BLOCK), lambda i: (i, 0)),
                  pl.BlockSpec((nb, d, IN_BLOCK), lambda i: (0, 0, 0)), tile, tile, _vec_spec(d), _mod_whole(d)],
        out_specs=[tile, _red_whole(d)],
        out_shape=[jax.ShapeDtypeStruct((n, d), F32), jax.ShapeDtypeStruct((2, 8, d), F32)],
        inputs=[dqkvu, win8, x, dxo, g, mod], exchange=exchange)
    return outs, xo


def _na_consts():
    j = np.arange(GRID_W)
    col_start = np.clip(j - NA_KW // 2, 0, GRID_W - NA_KW)
    valid = (j[None, :] >= col_start[:, None]) & (j[None, :] < col_start[:, None] + NA_KW)
    dc = np.clip(j[None, :] - j[:, None] + NA_KW - 1, 0, 2 * NA_KW - 2)
    onehot = np.zeros((LANES, GRID_W, GRID_W), np.float32)
    for d in range(2 * NA_KW - 1):
        onehot[d] = ((dc == d) & valid).astype(np.float32)
    negmask = np.where(valid, 0.0, NEG_INF).astype(np.float32)
    return onehot.reshape(LANES, GRID_W * GRID_W), np.tile(negmask, (1, NA_KH))


def bias_tables(rpb, name):
    onehot, negmask = _na_consts()
    nj = 2 * NA_KH - 1
    rows = NA_HEADS * nj
    a = jnp.pad(rpb.reshape(rows, 2 * NA_KW - 1), ((0, 0), (0, LANES - (2 * NA_KW - 1))))

    def body(a_ref, e_ref, o_ref):
        o_ref[...] = _dot(a_ref[...], e_ref[...], precision=HIGHEST)

    t = pl.pallas_call(
        body, name=name, out_shape=jax.ShapeDtypeStruct((rows, GRID_W * GRID_W), F32),
        in_specs=[pl.BlockSpec(memory_space=pltpu.VMEM)] * 2,
        out_specs=pl.BlockSpec(memory_space=pltpu.VMEM),
    )(a, jnp.asarray(onehot))
    t = t.reshape(NA_HEADS, nj, GRID_W, GRID_W)
    tb = jnp.stack([t[:, j0:j0 + NA_KH] for j0 in range(NA_KH)])
    tb = tb.transpose(0, 1, 3, 2, 4).reshape(NA_KH, NA_HEADS, GRID_W, NA_KH * GRID_W)
    return tb + jnp.asarray(negmask)[None, None]


def bias_tables_bwd(dtb, name):
    onehot, _ = _na_consts()
    nj = 2 * NA_KH - 1
    d5 = dtb.reshape(NA_KH, NA_HEADS, GRID_W, NA_KH, GRID_W).transpose(0, 3, 1, 2, 4)
    d2 = d5.reshape(NA_KH * NA_KH * NA_HEADS, GRID_W * GRID_W)

    def body(d_ref, e_ref, o_ref):
        r = _dot(d_ref[...], e_ref[...], NT, precision=HIGHEST)
        for j in range(nj):
            acc = jnp.zeros((NA_HEADS, LANES), F32)
            for j0 in range(NA_KH):
                kk = j - j0
                if 0 <= kk < NA_KH:
                    base = (j0 * NA_KH + kk) * NA_HEADS
                    acc = acc + r[base:base + NA_HEADS, :]
            o_ref[j] = acc

    out = pl.pallas_call(
        body, name=name, out_shape=jax.ShapeDtypeStruct((nj, NA_HEADS, LANES), F32),
        in_specs=[pl.BlockSpec(memory_space=pltpu.VMEM)] * 2,
        out_specs=pl.BlockSpec(memory_space=pltpu.VMEM),
        compiler_params=pltpu.CompilerParams(vmem_limit_bytes=VMEM_LIMIT),
    )(d2, jnp.asarray(onehot))
    return out[:, :, :2 * NA_KW - 1].transpose(1, 0, 2)


def _head_masks():
    lane = lax.broadcasted_iota(jnp.int32, (1, LANES), 1)
    return [(lane >= h * HEAD_DIM) & (lane < (h + 1) * HEAD_DIM) for h in range(HEADS_PER_BLOCK)]


def _row_window(r, rows):
    rs = jnp.clip(r - NA_KH // 2, 0, rows - NA_KH)
    return rs - r + NA_KH - 1, pl.multiple_of(rs * GRID_W, GRID_W)


def _stack_heads(t, masks):
    return jnp.concatenate([jnp.where(mk, t, jnp.zeros_like(t)) for mk in masks], axis=0)


def _unstack_heads(t2, masks):
    out = t2[(HEADS_PER_BLOCK - 1) * GRID_W:, :]
    for h in reversed(range(HEADS_PER_BLOCK - 1)):
        out = jnp.where(masks[h], t2[h * GRID_W:(h + 1) * GRID_W, :], out)
    return out


NA_ROWS_PER_STEP = 4
NA_STEP = NA_ROWS_PER_STEP * GRID_W
SLAB = NA_KH * GRID_W
K_COL = N_HEAD_BLOCKS
V_COL = 2 * N_HEAD_BLOCKS


def na_fwd(qkv, tb, s, name, exchange=None):
    n = qkv.shape[0]
    l = n - s
    rows = s // GRID_W
    rr = NA_ROWS_PER_STEP
    x_steps = rows // rr

    def core(q_ref, k_ref, v_ref, kc_ref, vc_ref, tb_ref, o_ref, lse_ref):
        rb = pl.program_id(1)

        @pl.when(rb >= x_steps)
        def _():
            o_ref[...] = jnp.zeros_like(o_ref)
            lse_ref[...] = jnp.zeros_like(lse_ref)

        @pl.when(rb < x_steps)
        def _():
            masks = _head_masks()
            kcb, vcb = kc_ref[...], vc_ref[...]
            hq = HEADS_PER_BLOCK * GRID_W
            wins, q2s = [], []
            for t in range(rr):
                j0, off = _row_window(rb * rr + t, rows)
                wins.append((j0, off))
                q2s.append(_stack_heads(q_ref[t * GRID_W:(t + 1) * GRID_W, :] * ATT_SCALE, masks))
            s_ctx_all = _dot(jnp.concatenate(q2s, axis=0), kcb, NT)
            scores = []
            for t, (j0, off) in enumerate(wins):
                s_loc = _dot(q2s[t], k_ref[pl.ds(off, SLAB), :], NT) + tb_ref[j0].reshape(hq, SLAB)
                scores.append((s_loc, s_ctx_all[t * hq:(t + 1) * hq, :]))
            probs = []
            for s_loc, s_ctx in scores:
                m = jnp.maximum(jnp.max(s_loc, axis=-1, keepdims=True), jnp.max(s_ctx, axis=-1, keepdims=True))
                p_loc = jnp.exp(s_loc - m)
                p_ctx = jnp.exp(s_ctx - m)
                den = jnp.sum(p_loc, axis=-1, keepdims=True) + jnp.sum(p_ctx, axis=-1, keepdims=True)
                probs.append((p_loc.astype(BF16), p_ctx.astype(BF16), den, m + jnp.log(den)))
            o_ctx_all = _dot(jnp.concatenate([p[1] for p in probs], axis=0), vcb)
            for t, (p_loc, p_ctx, den, lse2) in enumerate(probs):
                o2 = (_dot(p_loc, v_ref[pl.ds(wins[t][1], SLAB), :]) + o_ctx_all[t * hq:(t + 1) * hq, :]) / den
                o_ref[t * GRID_W:(t + 1) * GRID_W, :] = _unstack_heads(o2, masks).astype(BF16)
                lse_ref[0, t * GRID_W:(t + 1) * GRID_W, :] = _unstack_heads(lse2, masks)

    cb = s // l
    outs, xo = carrier_call(
        core, name=name, grid=(N_HEAD_BLOCKS, n // NA_STEP),
        in_specs=[pl.BlockSpec((NA_STEP, LANES), lambda hb, rb: (jnp.minimum(rb, x_steps - 1), hb)),
                  pl.BlockSpec((s, LANES), lambda hb, rb: (0, K_COL + hb)),
                  pl.BlockSpec((s, LANES), lambda hb, rb: (0, V_COL + hb)),
                  pl.BlockSpec((l, LANES), lambda hb, rb: (cb, K_COL + hb)),
                  pl.BlockSpec((l, LANES), lambda hb, rb: (cb, V_COL + hb)),
                  pl.BlockSpec((NA_KH, HEADS_PER_BLOCK, GRID_W, SLAB), lambda hb, rb: (0, hb, 0, 0))],
        out_specs=[pl.BlockSpec((NA_STEP, LANES), lambda hb, rb: (rb, hb)),
                   pl.BlockSpec((1, NA_STEP, LANES), lambda hb, rb: (hb, rb, 0))],
        out_shape=[jax.ShapeDtypeStruct((n, NA_WIDTH), BF16), jax.ShapeDtypeStruct((N_HEAD_BLOCKS, n, LANES), F32)],
        inputs=[qkv, qkv, qkv, qkv, qkv, tb], exchange=exchange)
    return outs, xo


def na_bwd(qkv, tb, o, dmix, lse, s, name, exchange=None):
    n = qkv.shape[0]
    l = n - s
    rows = s // GRID_W
    rr = NA_ROWS_PER_STEP
    x_steps = rows // rr

    def core(q_ref, k_ref, v_ref, kc_ref, vc_ref, tb_ref, o_ref, do_ref, lse_ref, dq_ref, dk_ref, dv_ref, dtb_ref):
        rb = pl.program_id(1)

        @pl.when(rb == 0)
        def _():
            dk_ref[...] = jnp.zeros_like(dk_ref)
            dv_ref[...] = jnp.zeros_like(dv_ref)
            dtb_ref[...] = jnp.zeros_like(dtb_ref)

        @pl.when(rb >= x_steps)
        def _():
            dq_ref[...] = jnp.zeros_like(dq_ref)

        @pl.when(rb < x_steps)
        def _():
            masks = _head_masks()
            kcb, vcb = kc_ref[...], vc_ref[...]
            hq = HEADS_PER_BLOCK * GRID_W
            rows1 = []
            for t in range(rr):
                j0, off = _row_window(rb * rr + t, rows)
                sl = slice(t * GRID_W, (t + 1) * GRID_W)
                q2 = _stack_heads(q_ref[sl, :] * ATT_SCALE, masks)
                do_f = do_ref[sl, :]
                do2 = _stack_heads(do_f.astype(BF16), masks)
                dd = do_f * o_ref[sl, :].astype(F32)
                delta2 = jnp.concatenate(
                    [jnp.sum(jnp.where(mk, dd, 0.0), axis=-1, keepdims=True) for mk in masks], axis=0)
                lse_t = lse_ref[0, sl, :]
                lse2 = jnp.concatenate(
                    [lse_t[:, h * HEAD_DIM:h * HEAD_DIM + 1] for h in range(HEADS_PER_BLOCK)], axis=0)
                rows1.append((j0, off, q2, do2, lse2, delta2))
            q2_all = jnp.concatenate([r[2] for r in rows1], axis=0)
            do2_all = jnp.concatenate([r[3] for r in rows1], axis=0)
            lse2_all = jnp.concatenate([r[4] for r in rows1], axis=0)
            delta2_all = jnp.concatenate([r[5] for r in rows1], axis=0)
            s_ctx_all = _dot(q2_all, kcb, NT) - lse2_all
            dp_ctx_all = _dot(do2_all, vcb, NT) - delta2_all
            stage1 = []
            for j0, off, q2, do2, lse2, delta2 in rows1:
                kslab = k_ref[pl.ds(off, SLAB), :]
                vslab = v_ref[pl.ds(off, SLAB), :]
                s_loc = _dot(q2, kslab, NT) + tb_ref[j0].reshape(hq, SLAB) - lse2
                dp_loc = _dot(do2, vslab, NT) - delta2
                stage1.append((j0, off, q2, do2, s_loc, dp_loc))
            p_ctx_all = jnp.exp(s_ctx_all)
            ds_ctx_all = (p_ctx_all * dp_ctx_all).astype(BF16)
            p_ctx_all = p_ctx_all.astype(BF16)
            stage2 = []
            for j0, off, q2, do2, s_loc, dp_loc in stage1:
                p_loc = jnp.exp(s_loc)
                ds_loc = p_loc * dp_loc
                dtb_ref[j0] += ds_loc.reshape(HEADS_PER_BLOCK, GRID_W, SLAB)
                stage2.append((off, q2, do2, p_loc.astype(BF16), ds_loc.astype(BF16)))
            dq_ctx_all = _dot(ds_ctx_all, kcb)
            for t, (off, q2, do2, p_loc, ds_loc) in enumerate(stage2):
                dq2 = (_dot(ds_loc, k_ref[pl.ds(off, SLAB), :]) + dq_ctx_all[t * hq:(t + 1) * hq, :]) * ATT_SCALE
                dq_ref[t * GRID_W:(t + 1) * GRID_W, :] = _unstack_heads(dq2, masks)
                dk_ref[pl.ds(off, SLAB), :] += _dot(ds_loc, q2, TN)
                dv_ref[pl.ds(off, SLAB), :] += _dot(p_loc, do2, TN)
            dk_ref[s:, :] += _dot(ds_ctx_all, q2_all, TN)
            dv_ref[s:, :] += _dot(p_ctx_all, do2_all, TN)

    cb = s // l
    clamp = lambda hb, rb: (jnp.minimum(rb, x_steps - 1), hb)
    tile_in = pl.BlockSpec((NA_STEP, LANES), clamp)
    whole_out = pl.BlockSpec((n, LANES), lambda hb, rb: (0, hb))
    tbs = pl.BlockSpec((NA_KH, HEADS_PER_BLOCK, GRID_W, SLAB), lambda hb, rb: (0, hb, 0, 0))
    f32n = jax.ShapeDtypeStruct((n, NA_WIDTH), F32)
    outs, xo = carrier_call(
        core, name=name, grid=(N_HEAD_BLOCKS, n // NA_STEP),
        in_specs=[tile_in,
                  pl.BlockSpec((s, LANES), lambda hb, rb: (0, K_COL + hb)),
                  pl.BlockSpec((s, LANES), lambda hb, rb: (0, V_COL + hb)),
                  pl.BlockSpec((l, LANES), lambda hb, rb: (cb, K_COL + hb)),
                  pl.BlockSpec((l, LANES), lambda hb, rb: (cb, V_COL + hb)),
                  tbs, tile_in, tile_in,
                  pl.BlockSpec((1, NA_STEP, LANES), lambda hb, rb: (hb, jnp.minimum(rb, x_steps - 1), 0))],
        out_specs=[pl.BlockSpec((NA_STEP, LANES), lambda hb, rb: (rb, hb)), whole_out, whole_out, tbs],
        out_shape=[f32n, f32n, f32n, jax.ShapeDtypeStruct((NA_KH, NA_HEADS, GRID_W, SLAB), F32)],
        inputs=[qkv, qkv, qkv, qkv, qkv, tb, o, dmix, lse], exchange=exchange)
    return outs, xo


def ctx_attn_fwd(qkv, na, s, name):
    n = qkv.shape[0]
    l = n - s
    cb = s // l

    def core(q_ref, k_ref, v_ref, na_in, o_ref, lse_ref):
        masks = _head_masks()
        qt, kb, vb = q_ref[...], k_ref[...], v_ref[...]
        o_acc = jnp.zeros((l, LANES), F32)
        lse_acc = jnp.zeros((l, LANES), F32)
        for h in range(HEADS_PER_BLOCK):
            qh = jnp.where(masks[h], qt, jnp.zeros_like(qt))
            sc = _dot(qh, kb, NT) * ATT_SCALE
            m = jnp.max(sc, axis=-1, keepdims=True)
            p = jnp.exp(sc - m)
            den = jnp.sum(p, axis=-1, keepdims=True)
            o_acc = jnp.where(masks[h], _dot(p.astype(BF16), vb) / den, o_acc)
            lse_acc = jnp.where(masks[h], m + jnp.log(den), lse_acc)
        o_ref[...] = o_acc.astype(BF16)
        lse_ref[0] = lse_acc

    outs, _ = carrier_call(
        core, name=name, grid=(N_HEAD_BLOCKS,),
        in_specs=[pl.BlockSpec((l, LANES), lambda hb: (cb, hb)), pl.BlockSpec((l, LANES), lambda hb: (cb, K_COL + hb)),
                  pl.BlockSpec((l, LANES), lambda hb: (cb, V_COL + hb)), ANY],
        out_specs=[pl.BlockSpec((l, LANES), lambda hb: (cb, hb)), pl.BlockSpec((1, l, LANES), lambda hb: (hb, 0, 0))],
        out_shape=[jax.ShapeDtypeStruct(na.shape, BF16), jax.ShapeDtypeStruct((N_HEAD_BLOCKS, l, LANES), F32)],
        inputs=[qkv, qkv, qkv, na], aliases={3: 0})
    return outs


def ctx_attn_bwd(qkv, na, dmix, lse, dq, dk, dv, s, name):
    n = qkv.shape[0]
    l = n - s
    cb = s // l

    def core(q_ref, k_ref, v_ref, o_ref, do_ref, lse_ref, dq_in, dk_in, dv_in, dq_ref, dk_ref, dv_ref):
        masks = _head_masks()
        qt, kb, vb = q_ref[...], k_ref[...], v_ref[...]
        do_f = do_ref[...]
        dd = do_f * o_ref[...].astype(F32)
        do_b = do_f.astype(BF16)
        lse_t = lse_ref[0]
        dq_acc = jnp.zeros((l, LANES), F32)
        dk_acc = jnp.zeros((l, LANES), F32)
        dv_acc = jnp.zeros((l, LANES), F32)
        for h in range(HEADS_PER_BLOCK):
            qh = jnp.where(masks[h], qt, jnp.zeros_like(qt))
            doh = jnp.where(masks[h], do_b, jnp.zeros_like(do_b))
            delta = jnp.sum(jnp.where(masks[h], dd, 0.0), axis=-1, keepdims=True)
            p = jnp.exp(_dot(qh, kb, NT) * ATT_SCALE - lse_t[:, h * HEAD_DIM:h * HEAD_DIM + 1])
            ds = (p * (_dot(doh, vb, NT) - delta)).astype(BF16)
            dq_acc = jnp.where(masks[h], _dot(ds, kb) * ATT_SCALE, dq_acc)
            dk_acc = dk_acc + _dot(ds, qh, TN)
            dv_acc = dv_acc + _dot(p.astype(BF16), doh, TN)
        dq_ref[...] = dq_acc
        dk_ref[...] = dk_in[...] + dk_acc * ATT_SCALE
        dv_ref[...] = dv_in[...] + dv_acc

    blk = pl.BlockSpec((l, LANES), lambda hb: (cb, hb))
    f32n = jax.ShapeDtypeStruct((n, NA_WIDTH), F32)
    outs, _ = carrier_call(
        core, name=name, grid=(N_HEAD_BLOCKS,),
        in_specs=[blk, pl.BlockSpec((l, LANES), lambda hb: (cb, K_COL + hb)),
                  pl.BlockSpec((l, LANES), lambda hb: (cb, V_COL + hb)), blk, blk,
                  pl.BlockSpec((1, l, LANES), lambda hb: (hb, 0, 0)), ANY, blk, blk],
        out_specs=[blk, blk, blk], out_shape=[f32n, f32n, f32n],
        inputs=[qkv, qkv, qkv, na, dmix, lse, dq, dk, dv], aliases={6: 0, 7: 1, 8: 2})
    return outs


def _pool_consts(l):
    assert l == TM
    mem = np.zeros((2, POOL_GROUPS, TM, TM), np.float32)
    inv = np.zeros((2, POOL_GROUPS, TM, LANES), np.float32)
    for which, length in ((0, GRID_W), (1, l)):
        t = np.arange(length)
        for g, w in enumerate(POOL_WINDOWS):
            lo = np.clip(t - w // 2, 0, length)
            hi = np.clip(t - w // 2 + w, 0, length)
            blockm = ((t[None, :] >= lo[:, None]) & (t[None, :] < hi[:, None])).astype(np.float32)
            cnt = (hi - lo).astype(np.float32)
            for b in range(TM // length):
                mem[which, g, b * length:(b + 1) * length, b * length:(b + 1) * length] = blockm
                inv[which, g, b * length:(b + 1) * length, :] = (1.0 / cnt)[:, None]
    return mem, np.ascontiguousarray(mem.transpose(0, 1, 3, 2)), inv


def _split_dot(m01, val):
    hi = val.astype(BF16)
    lo = (val - hi.astype(F32)).astype(BF16)
    return _dot(m01, hi) + _dot(m01, lo)


def pool_fwd(u, mem, inv, wp, scale, nx_tiles, name):
    n = u.shape[0]

    def core(u_ref, m_ref, i_ref, wp_ref, s_ref, o_ref):
        for g in range(POOL_GROUPS):
            sl = slice(g * POOL_CH, (g + 1) * POOL_CH)
            ug = u_ref[:, sl]
            dg = _split_dot(m_ref[0, g], ug) * i_ref[0, g] - ug
            o_ref[:, sl] = (_dot(dg.astype(BF16), wp_ref[g]) * s_ref[:, sl]).astype(BF16)

    grp = lambda i: (i // nx_tiles, 0, 0, 0)
    outs, _ = carrier_call(
        core, name=name, grid=(n // TM,),
        in_specs=[pl.BlockSpec((TM, POOL_WIDTH), lambda i: (i, 0)),
                  pl.BlockSpec((1, POOL_GROUPS, TM, TM), grp),
                  pl.BlockSpec((1, POOL_GROUPS, TM, LANES), grp),
                  pl.BlockSpec((POOL_GROUPS, POOL_CH, POOL_CH), lambda i: (0, 0, 0)),
                  pl.BlockSpec((1, POOL_WIDTH), lambda i: (0, 0))],
        out_specs=[pl.BlockSpec((TM, POOL_WIDTH), lambda i: (i, 0))],
        out_shape=[jax.ShapeDtypeStruct((n, POOL_WIDTH), BF16)], inputs=[u, mem, inv, wp, scale])
    return outs[0]


def pool_bwd(dmix, u, mem, mem_t, inv, wp, scale, nx_tiles, name):
    n = u.shape[0]

    def core(dy_ref, u_ref, m_ref, mt_ref, i_ref, wp_ref, s_ref, du_ref, dwp_ref, dsc_ref):
        @pl.when(pl.program_id(0) == 0)
        def _():
            dwp_ref[...] = jnp.zeros_like(dwp_ref)
            dsc_ref[...] = jnp.zeros_like(dsc_ref)

        for g in range(POOL_GROUPS):
            sl = slice(g * POOL_CH, (g + 1) * POOL_CH)
            ug = u_ref[:, sl]
            dy = dy_ref[:, sl]
            dg = (_split_dot(m_ref[0, g], ug) * i_ref[0, g] - ug).astype(BF16)
            z = _dot(dg, wp_ref[g])
            dsc_ref[0:1, sl] += jnp.sum(dy * z, axis=0, keepdims=True)
            dz = (dy * s_ref[:, sl]).astype(BF16)
            dwp_ref[g] += _dot(dg, dz, TN)
            dd = _dot(dz, wp_ref[g], NT)
            du_ref[:, sl] = _split_dot(mt_ref[0, g], dd * i_ref[0, g]) - dd

    grp = lambda i: (i // nx_tiles, 0, 0, 0)
    outs, _ = carrier_call(
        core, name=name, grid=(n // TM,),
        in_specs=[pl.BlockSpec((TM, POOL_WIDTH), lambda i: (i, 1)),
                  pl.BlockSpec((TM, POOL_WIDTH), lambda i: (i, 0)),
                  pl.BlockSpec((1, POOL_GROUPS, TM, TM), grp),
                  pl.BlockSpec((1, POOL_GROUPS, TM, TM), grp),
                  pl.BlockSpec((1, POOL_GROUPS, TM, LANES), grp),
                  pl.BlockSpec((POOL_GROUPS, POOL_CH, POOL_CH), lambda i: (0, 0, 0)),
                  pl.BlockSpec((1, POOL_WIDTH), lambda i: (0, 0))],
        out_specs=[pl.BlockSpec((TM, POOL_WIDTH), lambda i: (i, 0)),
                   pl.BlockSpec((POOL_GROUPS, POOL_CH, POOL_CH), lambda i: (0, 0, 0)),
                   pl.BlockSpec((8, POOL_WIDTH), lambda i: (0, 0))],
        out_shape=[jax.ShapeDtypeStruct((n, POOL_WIDTH), F32),
                   jax.ShapeDtypeStruct((POOL_GROUPS, POOL_CH, POOL_CH), F32),
                   jax.ShapeDtypeStruct((8, POOL_WIDTH), F32)],
        inputs=[dmix, u, mem, mem_t, inv, wp, scale])
    return outs


MOD_ROWS = 16


def mod_fwd(cvecs, w, b, name):
    _, d = cvecs.shape
    cl = w.shape[2]
    tc = _pick(cl, (384, 128))

    def core(c_ref, w_ref, b_ref, o_ref):
        a = _silu(c_ref[...]).astype(BF16)
        o_ref[0] = _dot(a, w_ref[0].astype(BF16)) + b_ref[0]

    outs, _ = carrier_call(
        core, name=name, grid=(DEPTH, cl // tc),
        in_specs=[pl.BlockSpec((MOD_ROWS, d), lambda li, j: (0, 0)),
                  pl.BlockSpec((1, d, tc), lambda li, j: (li, 0, j)),
                  pl.BlockSpec((1, 1, tc), lambda li, j: (li, 0, j))],
        out_specs=[pl.BlockSpec((1, MOD_ROWS, tc), lambda li, j: (li, 0, j))],
        out_shape=[jax.ShapeDtypeStruct((DEPTH, MOD_ROWS, cl), F32)], inputs=[cvecs, w, b])
    return outs[0]


def mod_bwd(cvecs, dm, w, name):
    _, d = cvecs.shape
    cl = w.shape[2]
    tc = _pick(cl, (384, 128))

    def core(c_ref, dm_ref, w_ref, dw_ref, da_ref):
        @pl.when((pl.program_id(0) == 0) & (pl.program_id(1) == 0))
        def _():
            da_ref[...] = jnp.zeros_like(da_ref)

        a = _silu(c_ref[...]).astype(BF16)
        dmb = dm_ref[0].astype(BF16)
        dw_ref[0] = _dot(a, dmb, TN)
        da_ref[...] += _dot(dmb, w_ref[0].astype(BF16), NT)

    outs, _ = carrier_call(
        core, name=name, grid=(DEPTH, cl // tc),
        in_specs=[pl.BlockSpec((MOD_ROWS, d), lambda li, j: (0, 0)),
                  pl.BlockSpec((1, MOD_ROWS, tc), lambda li, j: (li, 0, j)),
                  pl.BlockSpec((1, d, tc), lambda li, j: (li, 0, j))],
        out_specs=[pl.BlockSpec((1, d, tc), lambda li, j: (li, 0, j)),
                   pl.BlockSpec((MOD_ROWS, d), lambda li, j: (0, 0))],
        out_shape=[jax.ShapeDtypeStruct((DEPTH, d, cl), F32), jax.ShapeDtypeStruct((MOD_ROWS, d), F32)],
        inputs=[cvecs, dm, w])
    return outs


def loss_head(y, target, name):
    n, d = y.shape
    s = target.shape[0]
    nt, nx = n // TM, s // TM

    def core(y_ref, t_ref, l_ref, dy_ref, acc_ref):
        i = pl.program_id(0)

        @pl.when(i == 0)
        def _():
            acc_ref[...] = jnp.zeros_like(acc_ref)

        @pl.when(i < nx)
        def _():
            e = y_ref[...] - t_ref[...]
            dy_ref[...] = e * (1.0 / d)
            acc_ref[...] += jnp.sum(e * e, axis=0, keepdims=True)

        @pl.when(i >= nx)
        def _():
            dy_ref[...] = jnp.zeros_like(dy_ref)

        @pl.when(i == nt - 1)
        def _():
            l_ref[...] = jnp.sum(acc_ref[...], axis=1, keepdims=True) * (0.5 / d)

    tile = pl.BlockSpec((TM, d), lambda i: (i, 0))
    outs, _ = carrier_call(
        core, name=name, grid=(nt,),
        in_specs=[tile, pl.BlockSpec((TM, d), lambda i: (jnp.minimum(i, nx - 1), 0))],
        out_specs=[pl.BlockSpec((1, 1), lambda i: (0, 0)), tile],
        out_shape=[jax.ShapeDtypeStruct((1, 1), F32), jax.ShapeDtypeStruct((n, d), F32)],
        scratch_shapes=[pltpu.VMEM((1, d), F32)], inputs=[y, target])
    return outs


def sum_devices(v, name):
    _, r, c = v.shape
    tr = _pick(r, (64, 8))

    def core(v_ref, o_ref):
        acc = v_ref[0]
        for p in range(1, N_DEV):
            acc = acc + v_ref[p]
        o_ref[...] = acc

    outs, _ = carrier_call(
        core, name=name, grid=(r // tr,), in_specs=[pl.BlockSpec((N_DEV, tr, c), lambda i: (0, i, 0))],
        out_specs=[pl.BlockSpec((tr, c), lambda i: (i, 0))], out_shape=[jax.ShapeDtypeStruct((r, c), F32)],
        inputs=[v])
    return outs[0]


def cctx_grad(parts, c_ctx, name):
    d = c_ctx.shape[1]

    def body(p_ref, c_ref, o_ref):
        acc = p_ref[0]
        for p in range(1, N_DEV):
            acc = acc + p_ref[p]
        o_ref[...] = acc[8:9, :] * _dsilu(c_ref[...])

    return pl.pallas_call(
        body, name=name, out_shape=jax.ShapeDtypeStruct((1, d), F32),
        in_specs=[pl.BlockSpec(memory_space=pltpu.VMEM)] * 2,
        out_specs=pl.BlockSpec(memory_space=pltpu.VMEM),
    )(parts, c_ctx)


def _adam_math(w, g, m, v):
    m2 = ADAM_B1 * m + (1.0 - ADAM_B1) * g
    v2 = ADAM_B2 * v + (1.0 - ADAM_B2) * (g * g)
    m_hat = m2 / (1.0 - ADAM_B1 ** ADAM_STEP)
    v_hat = v2 / (1.0 - ADAM_B2 ** ADAM_STEP)
    delta = -ADAM_LR * (m_hat / (jnp.sqrt(v_hat) + ADAM_EPS) + ADAM_WD * w)
    return delta, m2, v2


def adamw(w, g, m, v, name):
    r, c = w.shape
    tr = _pick(r, (256, 128, 64, 32, 16, 8, r))

    def core(w_ref, g_ref, m_ref, v_ref, d_ref, m2_ref, v2_ref):
        d_ref[...], m2_ref[...], v2_ref[...] = _adam_math(w_ref[...], g_ref[...], m_ref[...], v_ref[...])

    tile = pl.BlockSpec((tr, c), lambda i: (i, 0))
    out = jax.ShapeDtypeStruct((r, c), F32)
    outs, _ = carrier_call(core, name=name, grid=(r // tr,), in_specs=[tile] * 4, out_specs=[tile] * 3,
                           out_shape=[out] * 3, inputs=[w, g, m, v])
    return outs


def reduce_adamw(recv, w, m, v, name):
    r, c = w.shape
    tr = _pick(r, (256, 128, 64, 8))

    def core(recv_ref, w_ref, m_ref, v_ref, g_ref, d_ref, m2_ref, v2_ref):
        acc = recv_ref[0].astype(F32)
        for p in range(1, N_DEV):
            acc = acc + recv_ref[p].astype(F32)
        g_ref[...] = acc
        d_ref[...], m2_ref[...], v2_ref[...] = _adam_math(w_ref[...], acc, m_ref[...], v_ref[...])

    tile = pl.BlockSpec((tr, c), lambda i: (i, 0))
    out = jax.ShapeDtypeStruct((r, c), F32)
    outs, _ = carrier_call(
        core, name=name, grid=(r // tr,),
        in_specs=[pl.BlockSpec((N_DEV, tr, c), lambda i: (0, i, 0)), tile, tile, tile],
        out_specs=[tile] * 4, out_shape=[out] * 4, inputs=[recv, w, m, v])
    return outs


def kernel(x, c, ctx, c_ctx, w_mod, b_mod, norm_g, w_ffn_gate_up, w_ffn_down, w_in, w_out, na_rpb, w_pool, pool_scale, loss_target, m_c_ctx, m_w_mod, m_b_mod, m_norm_g, m_w_ffn_gate_up, m_w_ffn_down, m_w_in, m_w_out, m_na_rpb, m_w_pool, m_pool_scale, v_c_ctx, v_w_mod, v_b_mod, v_norm_g, v_w_ffn_gate_up, v_w_ffn_down, v_w_in, v_w_out, v_na_rpb, v_w_pool, v_pool_scale):
    s, d = x.shape[1], x.shape[2]
    l = ctx.shape[1]
    n = s + l
    nx = s // TM
    fq = w_ffn_gate_up.shape[-1]
    fr = w_ffn_down.shape[2]
    cl = w_mod.shape[2]
    dl = norm_g.shape[2]
    me = 4 * lax.axis_index("x") + 2 * lax.axis_index("y") + lax.axis_index("c")

    c_all = all_gather(c, "gather_c").reshape(N_DEV, d)
    cvecs = jnp.concatenate([c_all, c_ctx[None, :], jnp.zeros((MOD_ROWS - N_DEV - 1, d), F32)], axis=0)
    b_loc = lax.dynamic_slice(b_mod, (0, me * cl), (DEPTH, cl)).reshape(DEPTH, 1, cl)
    mod_loc = mod_fwd(cvecs, w_mod, b_loc, "mod_fwd")
    mod_all = all_gather(mod_loc.reshape(DEPTH * MOD_ROWS, cl), "gather_mod")
    mod_all = mod_all.reshape(N_DEV, DEPTH, MOD_ROWS, cl).transpose(1, 2, 0, 3).reshape(DEPTH, MOD_ROWS, N_DEV * cl)
    mine = lax.dynamic_slice(mod_all, (0, me, 0), (DEPTH, 1, N_DEV * cl))
    mods = jnp.concatenate([mine, mod_all[:, N_DEV:N_DEV + 1]], axis=1).reshape(DEPTH, 2, N_MOD, d)

    gu_b = w_ffn_gate_up.astype(BF16)
    dn_b = w_ffn_down.astype(BF16)
    wi_b = w_in.astype(BF16)
    wo_b = w_out.astype(BF16)
    wp_b = w_pool.astype(BF16)

    def ffn_shards(li, i):
        return [gu_b[li, i], dn_b[li, i]]

    def mix_shards(li):
        return [wi_b[li], wo_b[li]]

    def as_ffn_weights(gathered):
        return gathered[0].reshape(2, 4, d, fq), gathered[1].reshape(4, 2 * fr, d)

    def as_mix_weights(gathered):
        return gathered[0], gathered[1].reshape(N_DEV * wo_b.shape[1], d)

    tables = _rope_tables(s, n)
    mem_np, mem_t_np, inv_np = _pool_consts(l)
    mem, mem_t, inv = jnp.asarray(mem_np, BF16), jnp.asarray(mem_t_np, BF16), jnp.asarray(inv_np)
    first = gather_two_level([norm_g.reshape(DEPTH * 6, dl), gu_b[0, 0]], "gather_first")
    g_full = first[0].reshape(N_DEV, DEPTH, 6, dl).transpose(1, 2, 0, 3).reshape(DEPTH, 6, 1, N_DEV * dl)

    weights = {("ffn", 0, 0): (first[1].reshape(2, 4, d, fq), None)}
    saved = {}
    xcur = jnp.concatenate([x[0], ctx[0]], axis=0)
    for li in range(DEPTH):
        last = li == DEPTH - 1
        for i in range(2):
            tag = f"l{li}_ffn{i}"
            wgu, wd4 = weights[("ffn", li, i)]
            if i == 0:
                ex_up = Exchange(gathers=mix_shards(li) + ([dn_b[0, 0]] if wd4 is None else []))
                ex_dn = Exchange(gathers=[dn_b[li, 1]])
            elif not last:
                ex_up, ex_dn = Exchange(gathers=[gu_b[li + 1, 0]]), Exchange(gathers=[dn_b[li + 1, 0]])
            else:
                ex_up = ex_dn = None
            (hb, gu, a4), got_up = ffn_up(xcur, g_full[li, 4 * i], mods[li], wgu, s, 6 * i, tag + "_up", ex_up)
            if wd4 is None:
                wd4 = got_up.pop().reshape(4, 2 * fr, d)
                weights[("ffn", li, i)] = (wgu, wd4)
            (ff, xnext), got_dn = ffn_down(a4, wd4, xcur, g_full[li, 4 * i + 1], mods[li], s, 6 * i + 2, tag + "_down", ex_dn)
            saved[("ffn", li, i)] = (xcur, hb, gu, a4, ff)
            xcur = xnext
            if i == 0:
                weights[("mix", li)] = as_mix_weights(got_up)
                next_dn = got_dn
            elif not last:
                weights[("ffn", li + 1, 0)] = as_ffn_weights(got_up + got_dn)
            if i == 0:
                tag = f"l{li}_mix"
                win8, wout = weights[("mix", li)]
                (hb, qkv, u), _ = mix_in(xcur, g_full[li, 2], mods[li], win8, tables, s, tag + "_in")
                tb = bias_tables(na_rpb[li], tag + "_bias")
                (na, lse), got = na_fwd(qkv, tb, s, tag + "_na", Exchange(gathers=[gu_b[li, 1]]))
                weights[("ffn", li, 1)] = as_ffn_weights(got + next_dn)
                lse_c = None
                if not last:
                    na, lse_c = ctx_attn_fwd(qkv, na, s, tag + "_ctx_attn")
                py = pool_fwd(u, mem, inv, wp_b[li], pool_scale[li][None, :], nx, tag + "_pool")
                fm, xnext = mix_out(na, py, wout, xcur, g_full[li, 3], mods[li], s, tag + "_out")
                saved[("mix", li)] = (xcur, hb, qkv, u, tb, na, lse, lse_c, py, fm)
                xcur = xnext

    loss_local, dcur = loss_head(xcur, loss_target[0], "loss")
    loss = lax.psum(loss_local[0, 0], ("x", "y", "c"))

    recv = {"gu": lax.empty((N_DEV, 2 * DEPTH, d, fq), BF16), "dn": lax.empty((N_DEV, 2 * DEPTH, fr, d), BF16),
            "wi": lax.empty((N_DEV, DEPTH, d, IN_BLOCK), BF16), "wo": lax.empty((N_DEV, DEPTH, wo_b.shape[1], d), BF16)}
    pending = []

    def take(keys, gathers=()):
        nonlocal pending
        jobs = [(gr, recv[key], st) for key, gr, st in pending if key in keys]
        order = [key for key, _, _ in pending if key in keys]
        pending = [p for p in pending if p[0] not in keys]
        return Exchange(gathers=gathers, a2as=jobs), order

    def pad8(t):
        t = t.reshape(-1, d) if t.size % d == 0 else jnp.pad(t.reshape(-1), (0, -t.size % d)).reshape(-1, d)
        return jnp.pad(t, ((0, -t.shape[0] % 8), (0, 0)))

    def packed(parts):
        parts = [pad8(p) for p in parts]
        offs = np.cumsum([0] + [p.shape[0] for p in parts])
        return jnp.concatenate(parts + [jnp.zeros((-offs[-1] % 64, d), F32)], axis=0), offs

    def put(order, bufs):
        for key, buf in zip(order, bufs):
            recv[key] = buf

    d_rpb, d_wp, d_ps, d_mod, d_g = [], [], [], [], []
    for li in reversed(range(DEPTH)):
        reds = {}
        for i in (1, 0):
            tag = f"l{li}_ffn{i}"
            xin, hb, gu, a4, ff = saved[("ffn", li, i)]
            wgu, wd4 = weights[("ffn", li, i)]
            dff, red1 = post_bwd(ff, dcur, g_full[li, 4 * i + 1], mods[li], 6 * i + 2, 0.5, s, tag + "_post_bwd")
            early = []
            if (li, i) == (0, 0):
                early_small, early_offs = packed([jnp.stack(d_wp), jnp.stack(d_ps), jnp.stack(d_rpb)])
                early = [early_small]
            ex, _ = take((), early)
            g_dn, bufs = grad_weight(a4, dff, tag + "_dwdown", a_lead=4, per_step=4, exchange=ex)
            if early:
                early_sum = sum_devices(bufs[0], "sum_early_small_grads")
            pending += [("dn", g_dn.reshape(N_DEV, fr, d), 2 * li + i)]
            ex, order = take(("dn",))
            dgu, bufs = ffn_da(dff, wd4, gu, tag + "_da", ex)
            put(order, bufs)
            ex, order = take(("wi", "wo"))
            g_gu, bufs = grad_weight(hb, dgu.reshape(N_DEV, n, fq), tag + "_dwgu", b_lead=N_DEV, per_step=4, exchange=ex)
            put(order, bufs)
            pending += [("gu", g_gu, 2 * li + i)]
            ex, order = take(("gu",))
            (dcur, red2), bufs = ffn_dh(dgu, wgu, xin, dcur, g_full[li, 4 * i], mods[li], s, 6 * i, tag + "_dh", ex)
            put(order, bufs)
            reds[i] = (red1, red2)
            if i == 1:
                tag = f"l{li}_mix"
                xin, hb, qkv, u, tb, na, lse, lse_c, py, fm = saved[("mix", li)]
                win8, wout = weights[("mix", li)]
                dfm, redm1 = post_bwd(fm, dcur, g_full[li, 3], mods[li], 5, 1.0, s, tag + "_post_bwd")
                dmix = matmul_nt(dfm, wout, tag + "_dmix")
                g_wo = grad_wout(na, py, dfm, tag + "_dwout").reshape(N_DEV, wo_b.shape[1], d)
                du, gwp, gps = pool_bwd(dmix, u, mem, mem_t, inv, wp_b[li], pool_scale[li][None, :], nx, tag + "_pool_bwd")
                ex, order = take(("gu", "dn"))
                (dq, dk, dv, dtb), bufs = na_bwd(qkv, tb, na, dmix, lse, s, tag + "_na_bwd", ex)
                put(order, bufs)
                if li != DEPTH - 1:
                    dq, dk, dv = ctx_attn_bwd(qkv, na, dmix, lse_c, dq, dk, dv, s, tag + "_ctx_attn_bwd")
                grpb = bias_tables_bwd(dtb, tag + "_bias_bwd")
                dqkvu = qkv_bwd(dq, dk, dv, du, tables, tag + "_rope_bwd")
                (dcur, redm2), _ = mix_dh(dqkvu, win8, xin, dcur, g_full[li, 2], mods[li], s, tag + "_dh")
                g_wi, _ = grad_weight(hb, dqkvu, tag + "_dwin", b_cols=IN_BLOCK, per_step=4)
                pending += [("wi", g_wi, li), ("wo", g_wo, li)]
                d_rpb.insert(0, grpb)
                d_wp.insert(0, gwp)
                d_ps.insert(0, gps[0])
        (ra1, ra2), (rb1, rb2) = reds[0], reds[1]
        d_mod.insert(0, jnp.stack([ra2[:, 0], ra2[:, 1], ra1[:, 0], redm2[:, 0], redm2[:, 1], redm1[:, 0],
                                   rb2[:, 0], rb2[:, 1], rb1[:, 0]], axis=1))
        d_g.insert(0, jnp.stack([t[0] + t[1] for t in (ra2[:, 2], ra1[:, 1], redm2[:, 2], redm1[:, 1], rb2[:, 2], rb1[:, 1])]))
    grad_x = dcur[:s][None]

    small, offs = packed([jnp.stack(d_mod), jnp.stack(d_g)])
    ex, order = take(("gu", "dn", "wi", "wo"), [small])
    bufs = exchange_only(ex, "exchange_last")
    small_all = bufs[0]
    put(order, bufs[1:])
    small_sum = sum_devices(small_all, "sum_small_grads")

    n_mod_rows = DEPTH * 2 * N_MOD
    dmod_all = small_all[:, :n_mod_rows].reshape(N_DEV, DEPTH, 2, N_MOD * d)
    dmod_sum = small_sum[:n_mod_rows].reshape(DEPTH, 2, N_MOD * d)
    dm_rows = jnp.concatenate([dmod_all[:, :, 0].transpose(1, 0, 2), dmod_sum[:, 1:2],
                               jnp.zeros((DEPTH, MOD_ROWS - N_DEV - 1, N_MOD * d), F32)], axis=1)
    grad_b_mod = dmod_sum[:, 0] + dmod_sum[:, 1]
    dm_loc = lax.dynamic_slice(dm_rows, (0, 0, me * cl), (DEPTH, MOD_ROWS, cl))
    grad_w_mod, da_part = mod_bwd(cvecs, dm_loc, w_mod, "mod_bwd")
    da_all = all_gather(da_part, "gather_dcvec")
    grad_c_ctx = cctx_grad(da_all, c_ctx[None, :], "c_ctx_grad")[0]

    grad_norm_full = small_sum[offs[1]:offs[1] + DEPTH * 6].reshape(DEPTH, 6, d)
    grad_norm_g = lax.dynamic_slice(grad_norm_full, (0, 0, me * dl), (DEPTH, 6, dl))
    grad_w_pool = early_sum[early_offs[0]:early_offs[0] + w_pool.size // d].reshape(w_pool.shape)
    grad_pool_scale = early_sum[early_offs[1]:early_offs[1] + pool_scale.size // d].reshape(pool_scale.shape)
    grad_na_rpb = early_sum[early_offs[2]:early_offs[3]].reshape(-1)[:na_rpb.size].reshape(na_rpb.shape)

    def big_adam(key, w, m, v, name):
        shp = w.shape
        cols = shp[-1]
        outs = reduce_adamw(recv[key].reshape(N_DEV, -1, cols), w.reshape(-1, cols), m.reshape(-1, cols),
                            v.reshape(-1, cols), name)
        return tuple(t.reshape(shp) for t in outs)

    def small_adam(w, g, m, v, name):
        shp = w.shape
        cols = shp[-1]
        outs = adamw(w.reshape(-1, cols), g.reshape(-1, cols), m.reshape(-1, cols), v.reshape(-1, cols), name)
        return tuple(t.reshape(shp) for t in outs)

    b_gu = big_adam("gu", w_ffn_gate_up, m_w_ffn_gate_up, v_w_ffn_gate_up, "adam_gate_up")
    b_dn = big_adam("dn", w_ffn_down, m_w_ffn_down, v_w_ffn_down, "adam_down")
    b_wi = big_adam("wi", w_in, m_w_in, v_w_in, "adam_w_in")
    b_wo = big_adam("wo", w_out, m_w_out, v_w_out, "adam_w_out")
    a_cc = small_adam(c_ctx, grad_c_ctx, m_c_ctx, v_c_ctx, "adam_c_ctx")
    a_wm = small_adam(w_mod, grad_w_mod, m_w_mod, v_w_mod, "adam_w_mod")
    a_bm = small_adam(b_mod, grad_b_mod, m_b_mod, v_b_mod, "adam_b_mod")
    a_ng = small_adam(norm_g, grad_norm_g, m_norm_g, v_norm_g, "adam_norm_g")
    a_rp = small_adam(na_rpb, grad_na_rpb, m_na_rpb, v_na_rpb, "adam_na_rpb")
    a_wp = small_adam(w_pool, grad_w_pool, m_w_pool, v_w_pool, "adam_w_pool")
    a_ps = small_adam(pool_scale, grad_pool_scale, m_pool_scale, v_pool_scale, "adam_pool_scale")

    grads = (grad_c_ctx, grad_w_mod, grad_b_mod, grad_norm_g, b_gu[0], b_dn[0], b_wi[0], b_wo[0], grad_na_rpb, grad_w_pool, grad_pool_scale)
    deltas = (a_cc[0], a_wm[0], a_bm[0], a_ng[0], b_gu[1], b_dn[1], b_wi[1], b_wo[1], a_rp[0], a_wp[0], a_ps[0])
    new_m = (a_cc[1], a_wm[1], a_bm[1], a_ng[1], b_gu[2], b_dn[2], b_wi[2], b_wo[2], a_rp[1], a_wp[1], a_ps[1])
    new_v = (a_cc[2], a_wm[2], a_bm[2], a_ng[2], b_gu[3], b_dn[3], b_wi[3], b_wo[3], a_rp[2], a_wp[2], a_ps[2])
    return (loss, grad_x, *grads, *deltas, *new_m, *new_v)
```

```python
import functools
import math

import numpy as np
import jax
import jax.numpy as jnp
from jax import lax
from jax.experimental import pallas as pl
from jax.experimental.pallas import tpu as pltpu

F32 = jnp.float32
BF16 = jnp.bfloat16

N_DEV = 8
DEPTH = 2
GRID_W = 64
N_MOD = 9
NA_HEADS = 8
HEAD_DIM = 64
NA_WIDTH = NA_HEADS * HEAD_DIM
NA_KH = 8
NA_KW = 16
POOL_GROUPS = 4
POOL_CH = 128
POOL_WIDTH = POOL_GROUPS * POOL_CH
POOL_WINDOWS = (2, 4, 8, 16)
ROPE_THETA = 10000.0
ROPE_PAIRS = HEAD_DIM // 4
RMS_EPS = 1e-6
NEG_INF = -1e30
ATT_SCALE = HEAD_DIM ** -0.5

ADAM_LR = 0.001
ADAM_B1 = 0.9
ADAM_B2 = 0.999
ADAM_EPS = 1e-08
ADAM_WD = 0.01
ADAM_STEP = 10

TM = 256
LANES = 128
HEADS_PER_BLOCK = LANES // HEAD_DIM
N_HEAD_BLOCKS = NA_WIDTH // LANES
IN_BLOCK = 2 * LANES
N_QKV_BLOCKS = 3 * NA_WIDTH // IN_BLOCK
VMEM_LIMIT = 56 * 1024 * 1024
HIGHEST = lax.Precision.HIGHEST
MESH = pl.DeviceIdType.MESH
ANY = pl.BlockSpec(memory_space=pl.ANY)

NN = (((1,), (0,)), ((), ()))
NT = (((1,), (1,)), ((), ()))
TN = (((0,), (0,)), ((), ()))


def _pick(n, cands):
    for t in cands:
        if n % t == 0:
            return t
    raise ValueError(f"no tile for {n} among {cands}")


def _dot(a, b, dn=NN, precision=None):
    return lax.dot_general(a, b, dn, preferred_element_type=F32, precision=precision)


def _silu(x):
    return x * jax.nn.sigmoid(x)


def _dsilu(x):
    s = jax.nn.sigmoid(x)
    return s * (1.0 + x * (1.0 - s))


def _peer(mask):
    x, y, c = lax.axis_index("x"), lax.axis_index("y"), lax.axis_index("c")
    px = 1 - x if mask & 4 else x
    py = 1 - y if mask & 2 else y
    pc = 1 - c if mask & 1 else c
    return (px, py, pc), 4 * px + 2 * py + pc


class Exchange:
    def __init__(self, gathers=(), a2as=()):
        self.gathers = list(gathers)
        self.a2as = list(a2as)
        self.n_jobs = len(self.gathers) + len(self.a2as)

    def inputs(self):
        out = list(self.gathers)
        for v, buf, _ in self.a2as:
            out += [v, buf]
        return out

    def out_shapes(self):
        shapes = [jax.ShapeDtypeStruct((N_DEV,) + v.shape, v.dtype) for v in self.gathers]
        shapes += [jax.ShapeDtypeStruct(buf.shape, buf.dtype) for _, buf, _ in self.a2as]
        return shapes

    def aliases(self, n_in, n_out):
        ng = len(self.gathers)
        return {n_in + ng + 2 * k + 1: n_out + ng + k for k in range(len(self.a2as))}

    def scratch(self):
        per = N_DEV - 1
        return [pltpu.SemaphoreType.DMA((per * self.n_jobs,)), pltpu.SemaphoreType.DMA((per * self.n_jobs,)),
                pltpu.SemaphoreType.DMA((self.n_jobs,))]

    def _copies(self, in_refs, out_refs, sems, with_recvs):
        send_sems, recv_sems, local_sems = sems
        _, me = _peer(0)
        ng = len(self.gathers)
        local, sends, recvs = [], [], []
        for job in range(self.n_jobs):
            if job < ng:
                src_of = lambda pid, r=in_refs[job]: r
                dst_of = lambda pid, r=out_refs[job]: r.at[pid]
            else:
                k = job - ng
                stage = self.a2as[k][2]
                src_of = lambda pid, r=in_refs[ng + 2 * k]: r.at[pid]
                dst_of = lambda pid, r=out_refs[job], st=stage: r.at[pid, st]
            local.append(pltpu.make_async_copy(src_of(me), dst_of(me), local_sems.at[job]))
            for mask in range(1, N_DEV):
                peer, pid = _peer(mask)
                idx = job * (N_DEV - 1) + mask - 1
                sends.append(pltpu.make_async_remote_copy(
                    src_ref=src_of(pid), dst_ref=dst_of(me), send_sem=send_sems.at[idx],
                    recv_sem=recv_sems.at[idx], device_id=peer, device_id_type=MESH))
                if with_recvs:
                    recvs.append(pltpu.make_async_remote_copy(
                        src_ref=src_of(pid), dst_ref=dst_of(pid), send_sem=send_sems.at[idx],
                        recv_sem=recv_sems.at[idx], device_id=peer, device_id_type=MESH))
        return local, sends, recvs

    def start(self, in_refs, out_refs, sems):
        local, sends, _ = self._copies(in_refs, out_refs, sems, False)
        for cp in local + sends:
            cp.start()

    def wait(self, in_refs, out_refs, sems):
        local, sends, recvs = self._copies(in_refs, out_refs, sems, True)
        for cp in recvs:
            cp.wait_recv()
        for cp in sends:
            cp.wait_send()
        for cp in local:
            cp.wait()


def carrier_call(core, *, name, grid, in_specs, out_specs, out_shape, inputs, scratch_shapes=(), aliases=None,
                 exchange=None):
    aliases = dict(aliases or {})
    n_in, n_out, n_sc = len(in_specs), len(out_specs), len(scratch_shapes)
    sem = ("arbitrary",) * len(grid)
    params = pltpu.CompilerParams(dimension_semantics=sem, vmem_limit_bytes=VMEM_LIMIT)
    if exchange is None or exchange.n_jobs == 0:
        outs = pl.pallas_call(core, name=name, grid=grid, in_specs=list(in_specs), out_specs=tuple(out_specs),
                              out_shape=tuple(out_shape), scratch_shapes=list(scratch_shapes),
                              input_output_aliases=aliases, compiler_params=params)(*inputs)
        return list(outs), []
    x_in = exchange.inputs()
    x_out = exchange.out_shapes()
    aliases.update(exchange.aliases(n_in, n_out))

    def body(*refs):
        a = n_in + len(x_in)
        b = a + n_out + len(x_out)
        core_in, job_in = refs[:n_in], refs[n_in:a]
        core_out, job_out = refs[a:a + n_out], refs[a + n_out:b]
        core_sc, job_sc = refs[b:b + n_sc], refs[b + n_sc:]
        first = functools.reduce(lambda p, q: p & q, [pl.program_id(ax) == 0 for ax in range(len(grid))])
        last = functools.reduce(lambda p, q: p & q, [pl.program_id(ax) == g - 1 for ax, g in enumerate(grid)])

        @pl.when(first)
        def _():
            exchange.start(job_in, job_out, job_sc)

        core(*core_in, *core_out, *core_sc)

        @pl.when(last)
        def _():
            exchange.wait(job_in, job_out, job_sc)

    outs = pl.pallas_call(
        body, name=name, grid=grid, in_specs=list(in_specs) + [ANY] * len(x_in),
        out_specs=tuple(out_specs) + (ANY,) * len(x_out), out_shape=tuple(out_shape) + tuple(x_out),
        scratch_shapes=list(scratch_shapes) + exchange.scratch(), input_output_aliases=aliases,
        compiler_params=params)(*inputs, *x_in)
    return list(outs[:n_out]), list(outs[n_out:])


def exchange_only(exchange, name):
    def body(*refs):
        n_in, n_out = len(exchange.inputs()), len(exchange.out_shapes())
        job_in, job_out, sems = refs[:n_in], refs[n_in:n_in + n_out], refs[n_in + n_out:]
        exchange.start(job_in, job_out, sems)
        exchange.wait(job_in, job_out, sems)

    x_in = exchange.inputs()
    outs = pl.pallas_call(
        body, name=name, in_specs=[ANY] * len(x_in), out_specs=(ANY,) * len(exchange.out_shapes()),
        out_shape=tuple(exchange.out_shapes()), scratch_shapes=exchange.scratch(),
        input_output_aliases=exchange.aliases(0, 0))(*x_in)
    return list(outs)


def all_gather(v, name):
    return exchange_only(Exchange(gathers=[v]), name)[0]


def gather_two_level(vs, name):
    nv = len(vs)
    per = N_DEV - 1

    def body(*refs):
        v_refs, o_refs = refs[:nv], refs[nv:2 * nv]
        send_sems, recv_sems, local_sems = refs[2 * nv:]
        x, y, c = lax.axis_index("x"), lax.axis_index("y"), lax.axis_index("c")
        me, sibling = (x, y, c), (x, y, 1 - c)
        chips = [(1 - x, y), (x, 1 - y), (1 - x, 1 - y)]

        def copy(a, k, block, to, src=None):
            dst = o_refs[a].at[4 * block[0] + 2 * block[1] + block[2]]
            return pltpu.make_async_remote_copy(
                src_ref=dst if src is None else src, dst_ref=dst, send_sem=send_sems.at[a * per + k],
                recv_sem=recv_sems.at[a * per + k], device_id=to, device_id_type=MESH)

        mine = [pltpu.make_async_copy(v_refs[a], o_refs[a].at[4 * x + 2 * y + c], local_sems.at[a]) for a in range(nv)]
        first = []
        for a in range(nv):
            first.append(copy(a, 0, me, sibling, src=v_refs[a]))
            first += [copy(a, 1 + j, me, (*chip, c), src=v_refs[a]) for j, chip in enumerate(chips)]
        for cp in mine + first:
            cp.start()
        passed = []
        for a in range(nv):
            for j, chip in enumerate(chips):
                copy(a, 1 + j, (*chip, c), me).wait_recv()
                passed.append(copy(a, 4 + j, (*chip, c), sibling))
                passed[-1].start()
        for a in range(nv):
            copy(a, 0, sibling, me).wait_recv()
            for j, chip in enumerate(chips):
                copy(a, 4 + j, (*chip, 1 - c), me).wait_recv()
        for cp in first + passed:
            cp.wait_send()
        for cp in mine:
            cp.wait()

    outs = pl.pallas_call(
        body, name=name, in_specs=[ANY] * nv, out_specs=(ANY,) * nv,
        out_shape=tuple(jax.ShapeDtypeStruct((N_DEV,) + v.shape, v.dtype) for v in vs),
        scratch_shapes=[pltpu.SemaphoreType.DMA((per * nv,)), pltpu.SemaphoreType.DMA((per * nv,)),
                        pltpu.SemaphoreType.DMA((nv,))])(*vs)
    return list(outs)


def _rms(xf):
    return lax.rsqrt(jnp.mean(xf * xf, axis=-1, keepdims=True) + RMS_EPS)


def _is_ctx(i, tm, s):
    return (i * tm + lax.broadcasted_iota(jnp.int32, (tm, 1), 0)) >= s


def _by_tile_kind(i, tm, s, fn):
    n_latent = s // tm

    @pl.when(i < n_latent)
    def _():
        fn(None)

    @pl.when(i >= n_latent)
    def _():
        fn(_is_ctx(i, tm, s))


def _mod_rows(mod_ref, k, is_ctx):
    if is_ctx is None:
        return mod_ref[0, k:k + 1, :]
    return jnp.where(is_ctx, mod_ref[1, k:k + 1, :], mod_ref[0, k:k + 1, :])


def _norm_mod(xf, g, mod_ref, k_shift, k_scale, is_ctx):
    nrm = xf * _rms(xf) * g
    return (nrm * (1.0 + _mod_rows(mod_ref, k_scale, is_ctx)) + _mod_rows(mod_ref, k_shift, is_ctx)).astype(BF16)


def _post(xf, ff, g, mod_ref, k_gate, coef, is_ctx):
    return xf + coef * _mod_rows(mod_ref, k_gate, is_ctx) * (ff * _rms(ff) * g)


def _red_add(red_ref, first, is_ctx, rows):
    @pl.when(first)
    def _():
        red_ref[...] = jnp.zeros_like(red_ref)

    for r, val in enumerate(rows):
        tot = jnp.sum(val, axis=0, keepdims=True)
        if is_ctx is None:
            red_ref[0, r:r + 1, :] += tot
        else:
            ctx = jnp.sum(jnp.where(is_ctx, val, 0.0), axis=0, keepdims=True)
            red_ref[0, r:r + 1, :] += tot - ctx
            red_ref[1, r:r + 1, :] += ctx


def _pre_bwd(xf, dh, dxo, g, mod_ref, k_scale, is_ctx):
    r = _rms(xf)
    xhat = xf * r
    dn = dh * (1.0 + _mod_rows(mod_ref, k_scale, is_ctx))
    dxhat = dn * g
    dx = dxo + r * (dxhat - xhat * jnp.mean(dxhat * xhat, axis=-1, keepdims=True))
    return dx, [dh, dh * xhat]


def _finish_pre(red_ref, g, mod_ref, k_scale):
    sums = red_ref[:, 1:2, :]
    red_ref[:, 2:3, :] = (1.0 + mod_ref[:, k_scale:k_scale + 1, :]) * sums
    red_ref[:, 1:2, :] = g * sums


def _vec_spec(d):
    return pl.BlockSpec((1, d), lambda *_: (0, 0))


def _mod_whole(d):
    return pl.BlockSpec((2, N_MOD, d), lambda *_: (0, 0, 0))


def _red_whole(d):
    return pl.BlockSpec((2, 8, d), lambda *_: (0, 0, 0))


def _token_tile(n):
    return _pick(n, (768, 640, 512, 384, 256))


def _ffn_tile(n):
    return _pick(n, (528, 384, 640, 256))


def _resident(shape):
    zeros = (0,) * len(shape)
    return pl.BlockSpec(shape, lambda i: zeros, pipeline_mode=pl.Buffered(1))


def ffn_up(x, g, mod, wgu, s, k0, name, exchange=None):
    n, d = x.shape
    nk, fq = wgu.shape[1], wgu.shape[-1]
    tm = _ffn_tile(n)

    def core(x_ref, g_ref, mod_ref, w_ref, hb_ref, gu_ref, a_ref):
        i = pl.program_id(0)

        def prologue(is_ctx):
            hb_ref[...] = _norm_mod(x_ref[...], g_ref[...], mod_ref, k0, k0 + 1, is_ctx)

        _by_tile_kind(i, tm, s, prologue)
        h = hb_ref[...]
        for k in range(nk):
            gg = _dot(h, w_ref[0, k])
            uu = _dot(h, w_ref[1, k])
            gu_ref[0, k] = gg.astype(BF16)
            gu_ref[1, k] = uu.astype(BF16)
            a_ref[k] = (_silu(gg) * uu).astype(BF16)

    outs, xo = carrier_call(
        core, name=name, grid=(n // tm,),
        in_specs=[pl.BlockSpec((tm, d), lambda i: (i, 0)), _vec_spec(d), _mod_whole(d), _resident(wgu.shape)],
        out_specs=[pl.BlockSpec((tm, d), lambda i: (i, 0)),
                   pl.BlockSpec((2, nk, tm, fq), lambda i: (0, 0, i, 0)),
                   pl.BlockSpec((nk, tm, fq), lambda i: (0, i, 0))],
        out_shape=[jax.ShapeDtypeStruct((n, d), BF16), jax.ShapeDtypeStruct((2, nk, n, fq), BF16),
                   jax.ShapeDtypeStruct((nk, n, fq), BF16)],
        inputs=[x, g, mod, wgu], exchange=exchange)
    return outs, xo


def ffn_down(a4, wd4, x, g, mod, s, k_gate, name, exchange=None):
    n, d = x.shape
    nk, fq = wd4.shape[0], wd4.shape[1]
    tm = _ffn_tile(n)

    def core(a_ref, w_ref, x_ref, g_ref, mod_ref, f_ref, xo_ref):
        i = pl.program_id(0)
        ff = _dot(a_ref[0], w_ref[0])
        for k in range(1, nk):
            ff = ff + _dot(a_ref[k], w_ref[k])
        f_ref[...] = ff

        def epilogue(is_ctx):
            xo_ref[...] = _post(x_ref[...], f_ref[...], g_ref[...], mod_ref, k_gate, 0.5, is_ctx)

        _by_tile_kind(i, tm, s, epilogue)

    tile = pl.BlockSpec((tm, d), lambda i: (i, 0))
    outs, xo = carrier_call(
        core, name=name, grid=(n // tm,),
        in_specs=[pl.BlockSpec((nk, tm, fq), lambda i: (0, i, 0)), _resident(wd4.shape), tile, _vec_spec(d),
                  _mod_whole(d)],
        out_specs=[tile, tile],
        out_shape=[jax.ShapeDtypeStruct((n, d), F32), jax.ShapeDtypeStruct((n, d), F32)],
        inputs=[a4, wd4, x, g, mod], exchange=exchange)
    return outs, xo


def ffn_da(df, wd4, gu, name, exchange=None):
    n, d = df.shape
    nk, fq = wd4.shape[0], wd4.shape[1]
    tm = _ffn_tile(n)

    def core(df_ref, w_ref, gu_ref, o_ref):
        dfv = df_ref[...]
        for k in range(nk):
            da = _dot(dfv, w_ref[k], NT).astype(BF16)
            gg = gu_ref[0, k]
            uu = gu_ref[1, k]
            sg = jax.nn.sigmoid(gg.astype(F32)).astype(BF16)
            o_ref[0, k] = da * (uu * (sg * (1 + gg * (1 - sg))))
            o_ref[1, k] = da * (gg * sg)

    gu_spec = pl.BlockSpec((2, nk, tm, fq), lambda i: (0, 0, i, 0))
    outs, xo = carrier_call(
        core, name=name, grid=(n // tm,),
        in_specs=[pl.BlockSpec((tm, d), lambda i: (i, 0)), _resident(wd4.shape), gu_spec],
        out_specs=[gu_spec], out_shape=[jax.ShapeDtypeStruct(gu.shape, BF16)],
        inputs=[df, wd4, gu], exchange=exchange)
    return outs[0], xo


def ffn_dh(dgu, wgu, x, dxo, g, mod, s, k0, name, exchange=None):
    n, d = x.shape
    nk, fq = wgu.shape[1], wgu.shape[-1]
    tm = _ffn_tile(n)

    def core(dgu_ref, w_ref, x_ref, dxo_ref, g_ref, mod_ref, dx_ref, red_ref, dh_s):
        i = pl.program_id(0)
        dh = _dot(dgu_ref[0, 0], w_ref[0, 0], NT) + _dot(dgu_ref[1, 0], w_ref[1, 0], NT)
        for k in range(1, nk):
            dh = dh + _dot(dgu_ref[0, k], w_ref[0, k], NT) + _dot(dgu_ref[1, k], w_ref[1, k], NT)
        dh_s[...] = dh

        def epilogue(is_ctx):
            dx, sums = _pre_bwd(x_ref[...], dh_s[...], dxo_ref[...], g_ref[...], mod_ref, k0 + 1, is_ctx)
            dx_ref[...] = dx
            _red_add(red_ref, i == 0, is_ctx, sums)

        _by_tile_kind(i, tm, s, epilogue)

        @pl.when(i == n // tm - 1)
        def _():
            _finish_pre(red_ref, g_ref[...], mod_ref, k0 + 1)

    tile = pl.BlockSpec((tm, d), lambda i: (i, 0))
    outs, xo = carrier_call(
        core, name=name, grid=(n // tm,),
        in_specs=[pl.BlockSpec((2, nk, tm, fq), lambda i: (0, 0, i, 0)), _resident(wgu.shape), tile, tile,
                  _vec_spec(d), _mod_whole(d)],
        out_specs=[tile, _red_whole(d)],
        out_shape=[jax.ShapeDtypeStruct((n, d), F32), jax.ShapeDtypeStruct((2, 8, d), F32)],
        scratch_shapes=[pltpu.VMEM((tm, d), F32)], inputs=[dgu, wgu, x, dxo, g, mod], exchange=exchange)
    return outs, xo


def grad_weight(a, b, name, a_lead=None, b_lead=None, b_cols=None, per_step=1, exchange=None):
    n = a.shape[-2]
    ka = a.shape[-1]
    kb = b_cols or b.shape[-1]
    nj = a_lead or b_lead or (b.shape[-1] // b_cols)
    tk = _pick(n, (1408, 1024, 768, 640, 512, 256))
    nk = n // tk
    ps = per_step
    assert nj % ps == 0

    def core(a_ref, b_ref, o_ref, acc):
        kk = pl.program_id(1)

        @pl.when(kk == 0)
        def _():
            acc[...] = jnp.zeros_like(acc)

        if b_cols:
            acc[...] += _dot(a_ref[...], b_ref[...], TN)
        else:
            for t in range(ps):
                acc[t] += _dot(a_ref[t] if a_lead else a_ref[...], b_ref[t] if b_lead else b_ref[...], TN)

        @pl.when(kk == nk - 1)
        def _():
            for t in range(ps):
                o_ref[t] = (acc[:, t * kb:(t + 1) * kb] if b_cols else acc[t]).astype(BF16)

    a_spec = (pl.BlockSpec((ps, tk, ka), lambda j, kk: (j, kk, 0)) if a_lead
              else pl.BlockSpec((tk, ka), lambda j, kk: (kk, 0)))
    if b_lead:
        b_spec = pl.BlockSpec((ps, tk, kb), lambda j, kk: (j, kk, 0))
    elif b_cols:
        b_spec = pl.BlockSpec((tk, ps * kb), lambda j, kk: (kk, j))
    else:
        b_spec = pl.BlockSpec((tk, kb), lambda j, kk: (kk, 0))
    outs, xo = carrier_call(
        core, name=name, grid=(nj // ps, nk), in_specs=[a_spec, b_spec],
        out_specs=[pl.BlockSpec((ps, ka, kb), lambda j, kk: (j, 0, 0))],
        out_shape=[jax.ShapeDtypeStruct((nj, ka, kb), BF16)],
        scratch_shapes=[pltpu.VMEM((ka, ps * kb) if b_cols else (ps, ka, kb), F32)], inputs=[a, b], exchange=exchange)
    return outs[0], xo


def post_bwd(f, dxo, g, mod, k_gate, coef, s, name):
    n, d = f.shape
    tm = _token_tile(n)

    def core(f_ref, dxo_ref, g_ref, mod_ref, df_ref, red_ref):
        i = pl.program_id(0)

        def body(is_ctx):
            ff = f_ref[...]
            dxo_ = dxo_ref[...]
            r = _rms(ff)
            fn = ff * r
            cg = (coef * _mod_rows(mod_ref, k_gate, is_ctx)) * g_ref[...]
            w = dxo_ * fn
            dfn = dxo_ * cg
            df_ref[...] = (r * (dfn - fn * jnp.mean(w * cg, axis=-1, keepdims=True))).astype(BF16)
            _red_add(red_ref, i == 0, is_ctx, [w])

        _by_tile_kind(i, tm, s, body)

        @pl.when(i == n // tm - 1)
        def _():
            sums = red_ref[:, 0:1, :]
            red_ref[:, 1:2, :] = (coef * mod_ref[:, k_gate:k_gate + 1, :]) * sums
            red_ref[:, 0:1, :] = (coef * g_ref[...]) * sums

    tile = pl.BlockSpec((tm, d), lambda i: (i, 0))
    outs, _ = carrier_call(
        core, name=name, grid=(n // tm,), in_specs=[tile, tile, _vec_spec(d), _mod_whole(d)],
        out_specs=[tile, _red_whole(d)],
        out_shape=[jax.ShapeDtypeStruct((n, d), BF16), jax.ShapeDtypeStruct((2, 8, d), F32)],
        inputs=[f, dxo, g, mod])
    return outs


def matmul_nt(a, b, name):
    m, k = a.shape
    n = b.shape[0]
    tm = _token_tile(m)

    def core(a_ref, b_ref, o_ref):
        o_ref[...] = _dot(a_ref[...], b_ref[...], NT)

    outs, _ = carrier_call(
        core, name=name, grid=(m // tm,),
        in_specs=[pl.BlockSpec((tm, k), lambda i: (i, 0)), pl.BlockSpec((n, k), lambda i: (0, 0))],
        out_specs=[pl.BlockSpec((tm, n), lambda i: (i, 0))], out_shape=[jax.ShapeDtypeStruct((m, n), F32)],
        inputs=[a, b])
    return outs[0]


def _rope_tables(s, n):
    t = jnp.arange(n)
    lane = jnp.arange(LANES)
    dd = lane % HEAD_DIM
    inv = ROPE_THETA ** (-(dd % ROPE_PAIRS).astype(F32) / ROPE_PAIRS)
    pos = jnp.where(dd[None, :] < HEAD_DIM // 2, (t // GRID_W)[:, None], (t % GRID_W)[:, None]).astype(F32)
    ang = pos * inv[None, :]
    live = (t < s)[:, None]
    first = ((dd % (2 * ROPE_PAIRS)) < ROPE_PAIRS)[None, :]
    cos = jnp.where(live, jnp.cos(ang), 1.0)
    sin = jnp.where(live, jnp.sin(ang), 0.0)
    sa = jnp.where(first, -sin, 0.0)
    sb = jnp.where(first, 0.0, sin)
    return cos.astype(F32), sa.astype(F32), sb.astype(F32)


def _rope(xv, cos, sa, sb):
    return (xv * cos + pltpu.roll(xv, LANES - ROPE_PAIRS, 1) * sa + pltpu.roll(xv, ROPE_PAIRS, 1) * sb)


def mix_in(x, g, mod, win8, tables, s, name, exchange=None):
    n, d = x.shape
    tm = _token_tile(n)
    nb = win8.shape[0]
    n_rope = 2 * NA_WIDTH // IN_BLOCK

    def core(x_ref, g_ref, mod_ref, w_ref, c_ref, sa_ref, sb_ref, hb_ref, qkv_ref, u_ref):
        i = pl.program_id(0)
        hb = _norm_mod(x_ref[...], g_ref[...], mod_ref, 3, 4, _is_ctx(i, tm, s))
        hb_ref[...] = hb
        cos, sa, sb = c_ref[...], sa_ref[...], sb_ref[...]
        for j in range(nb):
            y = _dot(hb, w_ref[j])
            if j < n_rope:
                for b in range(IN_BLOCK // LANES):
                    sl = slice(b * LANES, (b + 1) * LANES)
                    qkv_ref[:, j * IN_BLOCK + b * LANES:j * IN_BLOCK + (b + 1) * LANES] = (
                        _rope(y[:, sl], cos, sa, sb).astype(BF16))
            elif j < N_QKV_BLOCKS:
                qkv_ref[:, j * IN_BLOCK:(j + 1) * IN_BLOCK] = y.astype(BF16)
            else:
                u_ref[:, (j - N_QKV_BLOCKS) * IN_BLOCK:(j - N_QKV_BLOCKS + 1) * IN_BLOCK] = y

    tab = pl.BlockSpec((tm, LANES), lambda i: (i, 0))
    row = lambda w: pl.BlockSpec((tm, w), lambda i: (i, 0))
    outs, xo = carrier_call(
        core, name=name, grid=(n // tm,),
        in_specs=[row(d), _vec_spec(d), _mod_whole(d), pl.BlockSpec((nb, d, IN_BLOCK), lambda i: (0, 0, 0)),
                  tab, tab, tab],
        out_specs=[row(d), row(3 * NA_WIDTH), row(POOL_WIDTH)],
        out_shape=[jax.ShapeDtypeStruct((n, d), BF16), jax.ShapeDtypeStruct((n, 3 * NA_WIDTH), BF16),
                   jax.ShapeDtypeStruct((n, POOL_WIDTH), F32)],
        inputs=[x, g, mod, win8, *tables], exchange=exchange)
    return outs, xo


def qkv_bwd(dq, dk, dv, du, tables, name):
    n = dq.shape[0]
    w = NA_WIDTH

    def core(dq_ref, dk_ref, dv_ref, du_ref, c_ref, sa_ref, sb_ref, o_ref):
        cos, sa, sb = c_ref[...], -sa_ref[...], -sb_ref[...]
        for b in range(N_HEAD_BLOCKS):
            sl = slice(b * LANES, (b + 1) * LANES)
            o_ref[:, b * LANES:(b + 1) * LANES] = _rope(dq_ref[:, sl], cos, sa, sb).astype(BF16)
            o_ref[:, w + b * LANES:w + (b + 1) * LANES] = _rope(dk_ref[:, sl], cos, sa, sb).astype(BF16)
        o_ref[:, 2 * w:3 * w] = dv_ref[...].astype(BF16)
        o_ref[:, 3 * w:] = du_ref[...].astype(BF16)

    tm = _token_tile(n)
    tab = pl.BlockSpec((tm, LANES), lambda i: (i, 0))
    tile = pl.BlockSpec((tm, w), lambda i: (i, 0))
    outs, _ = carrier_call(
        core, name=name, grid=(n // tm,), in_specs=[tile, tile, tile, tile, tab, tab, tab],
        out_specs=[pl.BlockSpec((tm, 4 * w), lambda i: (i, 0))],
        out_shape=[jax.ShapeDtypeStruct((n, 4 * w), BF16)], inputs=[dq, dk, dv, du, *tables])
    return outs[0]


def mix_out(na, py, wout, x, g, mod, s, name):
    n, d = x.shape
    tm = _token_tile(n)
    half = na.shape[1]

    def core(na_ref, py_ref, w_ref, x_ref, g_ref, mod_ref, f_ref, xo_ref):
        i = pl.program_id(0)
        ff = _dot(na_ref[...], w_ref[:half, :]) + _dot(py_ref[...], w_ref[half:, :])
        f_ref[...] = ff
        xo_ref[...] = _post(x_ref[...], ff, g_ref[...], mod_ref, 5, 1.0, _is_ctx(i, tm, s))

    tile = pl.BlockSpec((tm, d), lambda i: (i, 0))
    htile = pl.BlockSpec((tm, half), lambda i: (i, 0))
    outs, _ = carrier_call(
        core, name=name, grid=(n // tm,),
        in_specs=[htile, htile, pl.BlockSpec((2 * half, d), lambda i: (0, 0)), tile, _vec_spec(d), _mod_whole(d)],
        out_specs=[tile, tile],
        out_shape=[jax.ShapeDtypeStruct((n, d), F32), jax.ShapeDtypeStruct((n, d), F32)],
        inputs=[na, py, wout, x, g, mod])
    return outs


def grad_wout(na, py, dfm, name):
    n, half = na.shape
    d = dfm.shape[1]
    tk = _pick(n, (1408, 1024, 768, 640, 512, 256))
    nk = n // tk

    def core(na_ref, py_ref, b_ref, o_ref, acc):
        hh, kk = pl.program_id(0), pl.program_id(1)

        @pl.when(kk == 0)
        def _():
            acc[...] = jnp.zeros_like(acc)

        @pl.when(hh == 0)
        def _():
            acc[...] += _dot(na_ref[...], b_ref[...], TN)

        @pl.when(hh == 1)
        def _():
            acc[...] += _dot(py_ref[...], b_ref[...], TN)

        @pl.when(kk == nk - 1)
        def _():
            o_ref[0] = acc[...].astype(BF16)

    htile = pl.BlockSpec((tk, half), lambda hh, kk: (kk, 0))
    outs, _ = carrier_call(
        core, name=name, grid=(2, nk), in_specs=[htile, htile, pl.BlockSpec((tk, d), lambda hh, kk: (kk, 0))],
        out_specs=[pl.BlockSpec((1, half, d), lambda hh, kk: (hh, 0, 0))],
        out_shape=[jax.ShapeDtypeStruct((2, half, d), BF16)],
        scratch_shapes=[pltpu.VMEM((half, d), F32)], inputs=[na, py, dfm])
    return outs[0]


def mix_dh(dqkvu, win8, x, dxo, g, mod, s, name, exchange=None):
    n, d = x.shape
    tm = _token_tile(n)
    nb = win8.shape[0]

    def core(dq_ref, w_ref, x_ref, dxo_ref, g_ref, mod_ref, dx_ref, red_ref):
        i = pl.program_id(0)
        dh = _dot(dq_ref[:, :IN_BLOCK], w_ref[0], NT)
        for j in range(1, nb):
            dh = dh + _dot(dq_ref[:, j * IN_BLOCK:(j + 1) * IN_BLOCK], w_ref[j], NT)
        is_ctx = _is_ctx(i, tm, s)
        dx, sums = _pre_bwd(x_ref[...], dh, dxo_ref[...], g_ref[...], mod_ref, 4, is_ctx)
        dx_ref[...] = dx
        _red_add(red_ref, i == 0, is_ctx, sums)

        @pl.when(i == n // tm - 1)
        def _():
            _finish_pre(red_ref, g_ref[...], mod_ref, 4)

    tile = pl.BlockSpec((tm, d), lambda i: (i, 0))
    outs, xo = carrier_call(
        core, name=name, grid=(n // tm,),
        in_specs=[pl.BlockSpec((tm, nb * IN_BLOCK), lambda i: (i, 0)),
                  pl.BlockSpec((nb, d, IN_BLOCK), lambda i: (0, 0, 0)), tile, tile, _vec_spec(d), _mod_whole(d)],
        out_specs=[tile, _red_whole(d)],
        out_shape=[jax.ShapeDtypeStruct((n, d), F32), jax.ShapeDtypeStruct((2, 8, d), F32)],
        inputs=[dqkvu, win8, x, dxo, g, mod], exchange=exchange)
    return outs, xo


def _na_consts():
    j = np.arange(GRID_W)
    col_start = np.clip(j - NA_KW // 2, 0, GRID_W - NA_KW)
    valid = (j[None, :] >= col_start[:, None]) & (j[None, :] < col_start[:, None] + NA_KW)
    dc = np.clip(j[None, :] - j[:, None] + NA_KW - 1, 0, 2 * NA_KW - 2)
    onehot = np.zeros((LANES, GRID_W, GRID_W), np.float32)
    for d in range(2 * NA_KW - 1):
        onehot[d] = ((dc == d) & valid).astype(np.float32)
    negmask = np.where(valid, 0.0, NEG_INF).astype(np.float32)
    return onehot.reshape(LANES, GRID_W * GRID_W), np.tile(negmask, (1, NA_KH))


def bias_tables(rpb, name):
    onehot, negmask = _na_consts()
    nj = 2 * NA_KH - 1
    rows = NA_HEADS * nj
    a = jnp.pad(rpb.reshape(rows, 2 * NA_KW - 1), ((0, 0), (0, LANES - (2 * NA_KW - 1))))

    def body(a_ref, e_ref, o_ref):
        o_ref[...] = _dot(a_ref[...], e_ref[...], precision=HIGHEST)

    t = pl.pallas_call(
        body, name=name, out_shape=jax.ShapeDtypeStruct((rows, GRID_W * GRID_W), F32),
        in_specs=[pl.BlockSpec(memory_space=pltpu.VMEM)] * 2,
        out_specs=pl.BlockSpec(memory_space=pltpu.VMEM),
    )(a, jnp.asarray(onehot))
    t = t.reshape(NA_HEADS, nj, GRID_W, GRID_W)
    tb = jnp.stack([t[:, j0:j0 + NA_KH] for j0 in range(NA_KH)])
    tb = tb.transpose(0, 1, 3, 2, 4).reshape(NA_KH, NA_HEADS, GRID_W, NA_KH * GRID_W)
    return tb + jnp.asarray(negmask)[None, None]


def bias_tables_bwd(dtb, name):
    onehot, _ = _na_consts()
    nj = 2 * NA_KH - 1
    d5 = dtb.reshape(NA_KH, NA_HEADS, GRID_W, NA_KH, GRID_W).transpose(0, 3, 1, 2, 4)
    d2 = d5.reshape(NA_KH * NA_KH * NA_HEADS, GRID_W * GRID_W)

    def body(d_ref, e_ref, o_ref):
        r = _dot(d_ref[...], e_ref[...], NT, precision=HIGHEST)
        for j in range(nj):
            acc = jnp.zeros((NA_HEADS, LANES), F32)
            for j0 in range(NA_KH):
                kk = j - j0
                if 0 <= kk < NA_KH:
                    base = (j0 * NA_KH + kk) * NA_HEADS
                    acc = acc + r[base:base + NA_HEADS, :]
            o_ref[j] = acc

    out = pl.pallas_call(
        body, name=name, out_shape=jax.ShapeDtypeStruct((nj, NA_HEADS, LANES), F32),
        in_specs=[pl.BlockSpec(memory_space=pltpu.VMEM)] * 2,
        out_specs=pl.BlockSpec(memory_space=pltpu.VMEM),
        compiler_params=pltpu.CompilerParams(vmem_limit_bytes=VMEM_LIMIT),
    )(d2, jnp.asarray(onehot))
    return out[:, :, :2 * NA_KW - 1].transpose(1, 0, 2)


def _head_masks():
    lane = lax.broadcasted_iota(jnp.int32, (1, LANES), 1)
    return [(lane >= h * HEAD_DIM) & (lane < (h + 1) * HEAD_DIM) for h in range(HEADS_PER_BLOCK)]


def _row_window(r, rows):
    rs = jnp.clip(r - NA_KH // 2, 0, rows - NA_KH)
    return rs - r + NA_KH - 1, pl.multiple_of(rs * GRID_W, GRID_W)


def _stack_heads(t, masks):
    return jnp.concatenate([jnp.where(mk, t, jnp.zeros_like(t)) for mk in masks], axis=0)


def _unstack_heads(t2, masks):
    out = t2[(HEADS_PER_BLOCK - 1) * GRID_W:, :]
    for h in reversed(range(HEADS_PER_BLOCK - 1)):
        out = jnp.where(masks[h], t2[h * GRID_W:(h + 1) * GRID_W, :], out)
    return out


NA_ROWS_PER_STEP = 4
NA_STEP = NA_ROWS_PER_STEP * GRID_W
SLAB = NA_KH * GRID_W
K_COL = N_HEAD_BLOCKS
V_COL = 2 * N_HEAD_BLOCKS


def na_fwd(qkv, tb, s, name, exchange=None):
    n = qkv.shape[0]
    l = n - s
    rows = s // GRID_W
    rr = NA_ROWS_PER_STEP
    x_steps = rows // rr

    def core(q_ref, k_ref, v_ref, kc_ref, vc_ref, tb_ref, o_ref, lse_ref):
        rb = pl.program_id(1)

        @pl.when(rb >= x_steps)
        def _():
            o_ref[...] = jnp.zeros_like(o_ref)
            lse_ref[...] = jnp.zeros_like(lse_ref)

        @pl.when(rb < x_steps)
        def _():
            masks = _head_masks()
            kcb, vcb = kc_ref[...], vc_ref[...]
            hq = HEADS_PER_BLOCK * GRID_W
            wins, q2s = [], []
            for t in range(rr):
                j0, off = _row_window(rb * rr + t, rows)
                wins.append((j0, off))
                q2s.append(_stack_heads(q_ref[t * GRID_W:(t + 1) * GRID_W, :] * ATT_SCALE, masks))
            s_ctx_all = _dot(jnp.concatenate(q2s, axis=0), kcb, NT)
            scores = []
            for t, (j0, off) in enumerate(wins):
                s_loc = _dot(q2s[t], k_ref[pl.ds(off, SLAB), :], NT) + tb_ref[j0].reshape(hq, SLAB)
                scores.append((s_loc, s_ctx_all[t * hq:(t + 1) * hq, :]))
            probs = []
            for s_loc, s_ctx in scores:
                m = jnp.maximum(jnp.max(s_loc, axis=-1, keepdims=True), jnp.max(s_ctx, axis=-1, keepdims=True))
                p_loc = jnp.exp(s_loc - m)
                p_ctx = jnp.exp(s_ctx - m)
                den = jnp.sum(p_loc, axis=-1, keepdims=True) + jnp.sum(p_ctx, axis=-1, keepdims=True)
                probs.append((p_loc.astype(BF16), p_ctx.astype(BF16), den, m + jnp.log(den)))
            o_ctx_all = _dot(jnp.concatenate([p[1] for p in probs], axis=0), vcb)
            for t, (p_loc, p_ctx, den, lse2) in enumerate(probs):
                o2 = (_dot(p_loc, v_ref[pl.ds(wins[t][1], SLAB), :]) + o_ctx_all[t * hq:(t + 1) * hq, :]) / den
                o_ref[t * GRID_W:(t + 1) * GRID_W, :] = _unstack_heads(o2, masks).astype(BF16)
                lse_ref[0, t * GRID_W:(t + 1) * GRID_W, :] = _unstack_heads(lse2, masks)

    cb = s // l
    outs, xo = carrier_call(
        core, name=name, grid=(N_HEAD_BLOCKS, n // NA_STEP),
        in_specs=[pl.BlockSpec((NA_STEP, LANES), lambda hb, rb: (jnp.minimum(rb, x_steps - 1), hb)),
                  pl.BlockSpec((s, LANES), lambda hb, rb: (0, K_COL + hb)),
                  pl.BlockSpec((s, LANES), lambda hb, rb: (0, V_COL + hb)),
                  pl.BlockSpec((l, LANES), lambda hb, rb: (cb, K_COL + hb)),
                  pl.BlockSpec((l, LANES), lambda hb, rb: (cb, V_COL + hb)),
                  pl.BlockSpec((NA_KH, HEADS_PER_BLOCK, GRID_W, SLAB), lambda hb, rb: (0, hb, 0, 0))],
        out_specs=[pl.BlockSpec((NA_STEP, LANES), lambda hb, rb: (rb, hb)),
                   pl.BlockSpec((1, NA_STEP, LANES), lambda hb, rb: (hb, rb, 0))],
        out_shape=[jax.ShapeDtypeStruct((n, NA_WIDTH), BF16), jax.ShapeDtypeStruct((N_HEAD_BLOCKS, n, LANES), F32)],
        inputs=[qkv, qkv, qkv, qkv, qkv, tb], exchange=exchange)
    return outs, xo


def na_bwd(qkv, tb, o, dmix, lse, s, name, exchange=None):
    n = qkv.shape[0]
    l = n - s
    rows = s // GRID_W
    rr = NA_ROWS_PER_STEP
    x_steps = rows // rr

    def core(q_ref, k_ref, v_ref, kc_ref, vc_ref, tb_ref, o_ref, do_ref, lse_ref, dq_ref, dk_ref, dv_ref, dtb_ref):
        rb = pl.program_id(1)

        @pl.when(rb == 0)
        def _():
            dk_ref[...] = jnp.zeros_like(dk_ref)
            dv_ref[...] = jnp.zeros_like(dv_ref)
            dtb_ref[...] = jnp.zeros_like(dtb_ref)

        @pl.when(rb >= x_steps)
        def _():
            dq_ref[...] = jnp.zeros_like(dq_ref)

        @pl.when(rb < x_steps)
        def _():
            masks = _head_masks()
            kcb, vcb = kc_ref[...], vc_ref[...]
            hq = HEADS_PER_BLOCK * GRID_W
            rows1 = []
            for t in range(rr):
                j0, off = _row_window(rb * rr + t, rows)
                sl = slice(t * GRID_W, (t + 1) * GRID_W)
                q2 = _stack_heads(q_ref[sl, :] * ATT_SCALE, masks)
                do_f = do_ref[sl, :]
                do2 = _stack_heads(do_f.astype(BF16), masks)
                dd = do_f * o_ref[sl, :].astype(F32)
                delta2 = jnp.concatenate(
                    [jnp.sum(jnp.where(mk, dd, 0.0), axis=-1, keepdims=True) for mk in masks], axis=0)
                lse_t = lse_ref[0, sl, :]
                lse2 = jnp.concatenate(
                    [lse_t[:, h * HEAD_DIM:h * HEAD_DIM + 1] for h in range(HEADS_PER_BLOCK)], axis=0)
                rows1.append((j0, off, q2, do2, lse2, delta2))
            q2_all = jnp.concatenate([r[2] for r in rows1], axis=0)
            do2_all = jnp.concatenate([r[3] for r in rows1], axis=0)
            lse2_all = jnp.concatenate([r[4] for r in rows1], axis=0)
            delta2_all = jnp.concatenate([r[5] for r in rows1], axis=0)
            s_ctx_all = _dot(q2_all, kcb, NT) - lse2_all
            dp_ctx_all = _dot(do2_all, vcb, NT) - delta2_all
            stage1 = []
            for j0, off, q2, do2, lse2, delta2 in rows1:
                kslab = k_ref[pl.ds(off, SLAB), :]
                vslab = v_ref[pl.ds(off, SLAB), :]
                s_loc = _dot(q2, kslab, NT) + tb_ref[j0].reshape(hq, SLAB) - lse2
                dp_loc = _dot(do2, vslab, NT) - delta2
                stage1.append((j0, off, q2, do2, s_loc, dp_loc))
            p_ctx_all = jnp.exp(s_ctx_all)
            ds_ctx_all = (p_ctx_all * dp_ctx_all).astype(BF16)
            p_ctx_all = p_ctx_all.astype(BF16)
            stage2 = []
            for j0, off, q2, do2, s_loc, dp_loc in stage1:
                p_loc = jnp.exp(s_loc)
                ds_loc = p_loc * dp_loc
                dtb_ref[j0] += ds_loc.reshape(HEADS_PER_BLOCK, GRID_W, SLAB)
                stage2.append((off, q2, do2, p_loc.astype(BF16), ds_loc.astype(BF16)))
            dq_ctx_all = _dot(ds_ctx_all, kcb)
            for t, (off, q2, do2, p_loc, ds_loc) in enumerate(stage2):
                dq2 = (_dot(ds_loc, k_ref[pl.ds(off, SLAB), :]) + dq_ctx_all[t * hq:(t + 1) * hq, :]) * ATT_SCALE
                dq_ref[t * GRID_W:(t + 1) * GRID_W, :] = _unstack_heads(dq2, masks)
                dk_ref[pl.ds(off, SLAB), :] += _dot(ds_loc, q2, TN)
                dv_ref[pl.ds(off, SLAB), :] += _dot(p_loc, do2, TN)
            dk_ref[s:, :] += _dot(ds_ctx_all, q2_all, TN)
            dv_ref[s:, :] += _dot(p_ctx_all, do2_all, TN)

    cb = s // l
    clamp = lambda hb, rb: (jnp.minimum(rb, x_steps - 1), hb)
    tile_in = pl.BlockSpec((NA_STEP, LANES), clamp)
    whole_out = pl.BlockSpec((n, LANES), lambda hb, rb: (0, hb))
    tbs = pl.BlockSpec((NA_KH, HEADS_PER_BLOCK, GRID_W, SLAB), lambda hb, rb: (0, hb, 0, 0))
    f32n = jax.ShapeDtypeStruct((n, NA_WIDTH), F32)
    outs, xo = carrier_call(
        core, name=name, grid=(N_HEAD_BLOCKS, n // NA_STEP),
        in_specs=[tile_in,
                  pl.BlockSpec((s, LANES), lambda hb, rb: (0, K_COL + hb)),
                  pl.BlockSpec((s, LANES), lambda hb, rb: (0, V_COL + hb)),
                  pl.BlockSpec((l, LANES), lambda hb, rb: (cb, K_COL + hb)),
                  pl.BlockSpec((l, LANES), lambda hb, rb: (cb, V_COL + hb)),
                  tbs, tile_in, tile_in,
                  pl.BlockSpec((1, NA_STEP, LANES), lambda hb, rb: (hb, jnp.minimum(rb, x_steps - 1), 0))],
        out_specs=[pl.BlockSpec((NA_STEP, LANES), lambda hb, rb: (rb, hb)), whole_out, whole_out, tbs],
        out_shape=[f32n, f32n, f32n, jax.ShapeDtypeStruct((NA_KH, NA_HEADS, GRID_W, SLAB), F32)],
        inputs=[qkv, qkv, qkv, qkv, qkv, tb, o, dmix, lse], exchange=exchange)
    return outs, xo


def ctx_attn_fwd(qkv, na, s, name):
    n = qkv.shape[0]
    l = n - s
    cb = s // l

    def core(q_ref, k_ref, v_ref, na_in, o_ref, lse_ref):
        masks = _head_masks()
        qt, kb, vb = q_ref[...], k_ref[...], v_ref[...]
        o_acc = jnp.zeros((l, LANES), F32)
        lse_acc = jnp.zeros((l, LANES), F32)
        for h in range(HEADS_PER_BLOCK):
            qh = jnp.where(masks[h], qt, jnp.zeros_like(qt))
            sc = _dot(qh, kb, NT) * ATT_SCALE
            m = jnp.max(sc, axis=-1, keepdims=True)
            p = jnp.exp(sc - m)
            den = jnp.sum(p, axis=-1, keepdims=True)
            o_acc = jnp.where(masks[h], _dot(p.astype(BF16), vb) / den, o_acc)
            lse_acc = jnp.where(masks[h], m + jnp.log(den), lse_acc)
        o_ref[...] = o_acc.astype(BF16)
        lse_ref[0] = lse_acc

    outs, _ = carrier_call(
        core, name=name, grid=(N_HEAD_BLOCKS,),
        in_specs=[pl.BlockSpec((l, LANES), lambda hb: (cb, hb)), pl.BlockSpec((l, LANES), lambda hb: (cb, K_COL + hb)),
                  pl.BlockSpec((l, LANES), lambda hb: (cb, V_COL + hb)), ANY],
        out_specs=[pl.BlockSpec((l, LANES), lambda hb: (cb, hb)), pl.BlockSpec((1, l, LANES), lambda hb: (hb, 0, 0))],
        out_shape=[jax.ShapeDtypeStruct(na.shape, BF16), jax.ShapeDtypeStruct((N_HEAD_BLOCKS, l, LANES), F32)],
        inputs=[qkv, qkv, qkv, na], aliases={3: 0})
    return outs


def ctx_attn_bwd(qkv, na, dmix, lse, dq, dk, dv, s, name):
    n = qkv.shape[0]
    l = n - s
    cb = s // l

    def core(q_ref, k_ref, v_ref, o_ref, do_ref, lse_ref, dq_in, dk_in, dv_in, dq_ref, dk_ref, dv_ref):
        masks = _head_masks()
        qt, kb, vb = q_ref[...], k_ref[...], v_ref[...]
        do_f = do_ref[...]
        dd = do_f * o_ref[...].astype(F32)
        do_b = do_f.astype(BF16)
        lse_t = lse_ref[0]
        dq_acc = jnp.zeros((l, LANES), F32)
        dk_acc = jnp.zeros((l, LANES), F32)
        dv_acc = jnp.zeros((l, LANES), F32)
        for h in range(HEADS_PER_BLOCK):
            qh = jnp.where(masks[h], qt, jnp.zeros_like(qt))
            doh = jnp.where(masks[h], do_b, jnp.zeros_like(do_b))
            delta = jnp.sum(jnp.where(masks[h], dd, 0.0), axis=-1, keepdims=True)
            p = jnp.exp(_dot(qh, kb, NT) * ATT_SCALE - lse_t[:, h * HEAD_DIM:h * HEAD_DIM + 1])
            ds = (p * (_dot(doh, vb, NT) - delta)).astype(BF16)
            dq_acc = jnp.where(masks[h], _dot(ds, kb) * ATT_SCALE, dq_acc)
            dk_acc = dk_acc + _dot(ds, qh, TN)
            dv_acc = dv_acc + _dot(p.astype(BF16), doh, TN)
        dq_ref[...] = dq_acc
        dk_ref[...] = dk_in[...] + dk_acc * ATT_SCALE
        dv_ref[...] = dv_in[...] + dv_acc

    blk = pl.BlockSpec((l, LANES), lambda hb: (cb, hb))
    f32n = jax.ShapeDtypeStruct((n, NA_WIDTH), F32)
    outs, _ = carrier_call(
        core, name=name, grid=(N_HEAD_BLOCKS,),
        in_specs=[blk, pl.BlockSpec((l, LANES), lambda hb: (cb, K_COL + hb)),
                  pl.BlockSpec((l, LANES), lambda hb: (cb, V_COL + hb)), blk, blk,
                  pl.BlockSpec((1, l, LANES), lambda hb: (hb, 0, 0)), ANY, blk, blk],
        out_specs=[blk, blk, blk], out_shape=[f32n, f32n, f32n],
        inputs=[qkv, qkv, qkv, na, dmix, lse, dq, dk, dv], aliases={6: 0, 7: 1, 8: 2})
    return outs


def _pool_consts(l):
    assert l == TM
    mem = np.zeros((2, POOL_GROUPS, TM, TM), np.float32)
    inv = np.zeros((2, POOL_GROUPS, TM, LANES), np.float32)
    for which, length in ((0, GRID_W), (1, l)):
        t = np.arange(length)
        for g, w in enumerate(POOL_WINDOWS):
            lo = np.clip(t - w // 2, 0, length)
            hi = np.clip(t - w // 2 + w, 0, length)
            blockm = ((t[None, :] >= lo[:, None]) & (t[None, :] < hi[:, None])).astype(np.float32)
            cnt = (hi - lo).astype(np.float32)
            for b in range(TM // length):
                mem[which, g, b * length:(b + 1) * length, b * length:(b + 1) * length] = blockm
                inv[which, g, b * length:(b + 1) * length, :] = (1.0 / cnt)[:, None]
    return mem, np.ascontiguousarray(mem.transpose(0, 1, 3, 2)), inv


def _split_dot(m01, val):
    hi = val.astype(BF16)
    lo = (val - hi.astype(F32)).astype(BF16)
    return _dot(m01, hi) + _dot(m01, lo)


def pool_fwd(u, mem, inv, wp, scale, nx_tiles, name):
    n = u.shape[0]

    def core(u_ref, m_ref, i_ref, wp_ref, s_ref, o_ref):
        for g in range(POOL_GROUPS):
            sl = slice(g * POOL_CH, (g + 1) * POOL_CH)
            ug = u_ref[:, sl]
            dg = _split_dot(m_ref[0, g], ug) * i_ref[0, g] - ug
            o_ref[:, sl] = (_dot(dg.astype(BF16), wp_ref[g]) * s_ref[:, sl]).astype(BF16)

    grp = lambda i: (i // nx_tiles, 0, 0, 0)
    outs, _ = carrier_call(
        core, name=name, grid=(n // TM,),
        in_specs=[pl.BlockSpec((TM, POOL_WIDTH), lambda i: (i, 0)),
                  pl.BlockSpec((1, POOL_GROUPS, TM, TM), grp),
                  pl.BlockSpec((1, POOL_GROUPS, TM, LANES), grp),
                  pl.BlockSpec((POOL_GROUPS, POOL_CH, POOL_CH), lambda i: (0, 0, 0)),
                  pl.BlockSpec((1, POOL_WIDTH), lambda i: (0, 0))],
        out_specs=[pl.BlockSpec((TM, POOL_WIDTH), lambda i: (i, 0))],
        out_shape=[jax.ShapeDtypeStruct((n, POOL_WIDTH), BF16)], inputs=[u, mem, inv, wp, scale])
    return outs[0]


def pool_bwd(dmix, u, mem, mem_t, inv, wp, scale, nx_tiles, name):
    n = u.shape[0]

    def core(dy_ref, u_ref, m_ref, mt_ref, i_ref, wp_ref, s_ref, du_ref, dwp_ref, dsc_ref):
        @pl.when(pl.program_id(0) == 0)
        def _():
            dwp_ref[...] = jnp.zeros_like(dwp_ref)
            dsc_ref[...] = jnp.zeros_like(dsc_ref)

        for g in range(POOL_GROUPS):
            sl = slice(g * POOL_CH, (g + 1) * POOL_CH)
            ug = u_ref[:, sl]
            dy = dy_ref[:, sl]
            dg = (_split_dot(m_ref[0, g], ug) * i_ref[0, g] - ug).astype(BF16)
            z = _dot(dg, wp_ref[g])
            dsc_ref[0:1, sl] += jnp.sum(dy * z, axis=0, keepdims=True)
            dz = (dy * s_ref[:, sl]).astype(BF16)
            dwp_ref[g] += _dot(dg, dz, TN)
            dd = _dot(dz, wp_ref[g], NT)
            du_ref[:, sl] = _split_dot(mt_ref[0, g], dd * i_ref[0, g]) - dd

    grp = lambda i: (i // nx_tiles, 0, 0, 0)
    outs, _ = carrier_call(
        core, name=name, grid=(n // TM,),
        in_specs=[pl.BlockSpec((TM, POOL_WIDTH), lambda i: (i, 1)),
                  pl.BlockSpec((TM, POOL_WIDTH), lambda i: (i, 0)),
                  pl.BlockSpec((1, POOL_GROUPS, TM, TM), grp),
                  pl.BlockSpec((1, POOL_GROUPS, TM, TM), grp),
                  pl.BlockSpec((1, POOL_GROUPS, TM, LANES), grp),
                  pl.BlockSpec((POOL_GROUPS, POOL_CH, POOL_CH), lambda i: (0, 0, 0)),
                  pl.BlockSpec((1, POOL_WIDTH), lambda i: (0, 0))],
        out_specs=[pl.BlockSpec((TM, POOL_WIDTH), lambda i: (i, 0)),
                   pl.BlockSpec((POOL_GROUPS, POOL_CH, POOL_CH), lambda i: (0, 0, 0)),
                   pl.BlockSpec((8, POOL_WIDTH), lambda i: (0, 0))],
        out_shape=[jax.ShapeDtypeStruct((n, POOL_WIDTH), F32),
                   jax.ShapeDtypeStruct((POOL_GROUPS, POOL_CH, POOL_CH), F32),
                   jax.ShapeDtypeStruct((8, POOL_WIDTH), F32)],
        inputs=[dmix, u, mem, mem_t, inv, wp, scale])
    return outs


MOD_ROWS = 16


def mod_fwd(cvecs, w, b, name):
    _, d = cvecs.shape
    cl = w.shape[2]
    tc = _pick(cl, (384, 128))

    def core(c_ref, w_ref, b_ref, o_ref):
        a = _silu(c_ref[...]).astype(BF16)
        o_ref[0] = _dot(a, w_ref[0].astype(BF16)) + b_ref[0]

    outs, _ = carrier_call(
        core, name=name, grid=(DEPTH, cl // tc),
        in_specs=[pl.BlockSpec((MOD_ROWS, d), lambda li, j: (0, 0)),
                  pl.BlockSpec((1, d, tc), lambda li, j: (li, 0, j)),
                  pl.BlockSpec((1, 1, tc), lambda li, j: (li, 0, j))],
        out_specs=[pl.BlockSpec((1, MOD_ROWS, tc), lambda li, j: (li, 0, j))],
        out_shape=[jax.ShapeDtypeStruct((DEPTH, MOD_ROWS, cl), F32)], inputs=[cvecs, w, b])
    return outs[0]


def mod_bwd(cvecs, dm, w, name):
    _, d = cvecs.shape
    cl = w.shape[2]
    tc = _pick(cl, (384, 128))

    def core(c_ref, dm_ref, w_ref, dw_ref, da_ref):
        @pl.when((pl.program_id(0) == 0) & (pl.program_id(1) == 0))
        def _():
            da_ref[...] = jnp.zeros_like(da_ref)

        a = _silu(c_ref[...]).astype(BF16)
        dmb = dm_ref[0].astype(BF16)
        dw_ref[0] = _dot(a, dmb, TN)
        da_ref[...] += _dot(dmb, w_ref[0].astype(BF16), NT)

    outs, _ = carrier_call(
        core, name=name, grid=(DEPTH, cl // tc),
        in_specs=[pl.BlockSpec((MOD_ROWS, d), lambda li, j: (0, 0)),
                  pl.BlockSpec((1, MOD_ROWS, tc), lambda li, j: (li, 0, j)),
                  pl.BlockSpec((1, d, tc), lambda li, j: (li, 0, j))],
        out_specs=[pl.BlockSpec((1, d, tc), lambda li, j: (li, 0, j)),
                   pl.BlockSpec((MOD_ROWS, d), lambda li, j: (0, 0))],
        out_shape=[jax.ShapeDtypeStruct((DEPTH, d, cl), F32), jax.ShapeDtypeStruct((MOD_ROWS, d), F32)],
        inputs=[cvecs, dm, w])
    return outs


def loss_head(y, target, name):
    n, d = y.shape
    s = target.shape[0]
    nt, nx = n // TM, s // TM

    def core(y_ref, t_ref, l_ref, dy_ref, acc_ref):
        i = pl.program_id(0)

        @pl.when(i == 0)
        def _():
            acc_ref[...] = jnp.zeros_like(acc_ref)

        @pl.when(i < nx)
        def _():
            e = y_ref[...] - t_ref[...]
            dy_ref[...] = e * (1.0 / d)
            acc_ref[...] += jnp.sum(e * e, axis=0, keepdims=True)

        @pl.when(i >= nx)
        def _():
            dy_ref[...] = jnp.zeros_like(dy_ref)

        @pl.when(i == nt - 1)
        def _():
            l_ref[...] = jnp.sum(acc_ref[...], axis=1, keepdims=True) * (0.5 / d)

    tile = pl.BlockSpec((TM, d), lambda i: (i, 0))
    outs, _ = carrier_call(
        core, name=name, grid=(nt,),
        in_specs=[tile, pl.BlockSpec((TM, d), lambda i: (jnp.minimum(i, nx - 1), 0))],
        out_specs=[pl.BlockSpec((1, 1), lambda i: (0, 0)), tile],
        out_shape=[jax.ShapeDtypeStruct((1, 1), F32), jax.ShapeDtypeStruct((n, d), F32)],
        scratch_shapes=[pltpu.VMEM((1, d), F32)], inputs=[y, target])
    return outs


def sum_devices(v, name):
    _, r, c = v.shape
    tr = _pick(r, (64, 8))

    def core(v_ref, o_ref):
        acc = v_ref[0]
        for p in range(1, N_DEV):
            acc = acc + v_ref[p]
        o_ref[...] = acc

    outs, _ = carrier_call(
        core, name=name, grid=(r // tr,), in_specs=[pl.BlockSpec((N_DEV, tr, c), lambda i: (0, i, 0))],
        out_specs=[pl.BlockSpec((tr, c), lambda i: (i, 0))], out_shape=[jax.ShapeDtypeStruct((r, c), F32)],
        inputs=[v])
    return outs[0]


def cctx_grad(parts, c_ctx, name):
    d = c_ctx.shape[1]

    def body(p_ref, c_ref, o_ref):
        acc = p_ref[0]
        for p in range(1, N_DEV):
            acc = acc + p_ref[p]
        o_ref[...] = acc[8:9, :] * _dsilu(c_ref[...])

    return pl.pallas_call(
        body, name=name, out_shape=jax.ShapeDtypeStruct((1, d), F32),
        in_specs=[pl.BlockSpec(memory_space=pltpu.VMEM)] * 2,
        out_specs=pl.BlockSpec(memory_space=pltpu.VMEM),
    )(parts, c_ctx)


def _adam_math(w, g, m, v):
    m2 = ADAM_B1 * m + (1.0 - ADAM_B1) * g
    v2 = ADAM_B2 * v + (1.0 - ADAM_B2) * (g * g)
    m_hat = m2 / (1.0 - ADAM_B1 ** ADAM_STEP)
    v_hat = v2 / (1.0 - ADAM_B2 ** ADAM_STEP)
    delta = -ADAM_LR * (m_hat / (jnp.sqrt(v_hat) + ADAM_EPS) + ADAM_WD * w)
    return delta, m2, v2


def adamw(w, g, m, v, name):
    r, c = w.shape
    tr = _pick(r, (256, 128, 64, 32, 16, 8, r))

    def core(w_ref, g_ref, m_ref, v_ref, d_ref, m2_ref, v2_ref):
        d_ref[...], m2_ref[...], v2_ref[...] = _adam_math(w_ref[...], g_ref[...], m_ref[...], v_ref[...])

    tile = pl.BlockSpec((tr, c), lambda i: (i, 0))
    out = jax.ShapeDtypeStruct((r, c), F32)
    outs, _ = carrier_call(core, name=name, grid=(r // tr,), in_specs=[tile] * 4, out_specs=[tile] * 3,
                           out_shape=[out] * 3, inputs=[w, g, m, v])
    return outs


def reduce_adamw(recv, w, m, v, name):
    r, c = w.shape
    tr = _pick(r, (256, 128, 64, 8))

    def core(recv_ref, w_ref, m_ref, v_ref, g_ref, d_ref, m2_ref, v2_ref):
        acc = recv_ref[0].astype(F32)
        for p in range(1, N_DEV):
            acc = acc + recv_ref[p].astype(F32)
        g_ref[...] = acc
        d_ref[...], m2_ref[...], v2_ref[...] = _adam_math(w_ref[...], acc, m_ref[...], v_ref[...])

    tile = pl.BlockSpec((tr, c), lambda i: (i, 0))
    out = jax.ShapeDtypeStruct((r, c), F32)
    outs, _ = carrier_call(
        core, name=name, grid=(r // tr,),
        in_specs=[pl.BlockSpec((N_DEV, tr, c), lambda i: (0, i, 0)), tile, tile, tile],
        out_specs=[tile] * 4, out_shape=[out] * 4, inputs=[recv, w, m, v])
    return outs


def kernel(x, c, ctx, c_ctx, w_mod, b_mod, norm_g, w_ffn_gate_up, w_ffn_down, w_in, w_out, na_rpb, w_pool, pool_scale, loss_target, m_c_ctx, m_w_mod, m_b_mod, m_norm_g, m_w_ffn_gate_up, m_w_ffn_down, m_w_in, m_w_out, m_na_rpb, m_w_pool, m_pool_scale, v_c_ctx, v_w_mod, v_b_mod, v_norm_g, v_w_ffn_gate_up, v_w_ffn_down, v_w_in, v_w_out, v_na_rpb, v_w_pool, v_pool_scale):
    s, d = x.shape[1], x.shape[2]
    l = ctx.shape[1]
    n = s + l
    nx = s // TM
    fq = w_ffn_gate_up.shape[-1]
    fr = w_ffn_down.shape[2]
    cl = w_mod.shape[2]
    dl = norm_g.shape[2]
    me = 4 * lax.axis_index("x") + 2 * lax.axis_index("y") + lax.axis_index("c")

    c_all = all_gather(c, "gather_c").reshape(N_DEV, d)
    cvecs = jnp.concatenate([c_all, c_ctx[None, :], jnp.zeros((MOD_ROWS - N_DEV - 1, d), F32)], axis=0)
    b_loc = lax.dynamic_slice(b_mod, (0, me * cl), (DEPTH, cl)).reshape(DEPTH, 1, cl)
    mod_loc = mod_fwd(cvecs, w_mod, b_loc, "mod_fwd")
    mod_all = all_gather(mod_loc.reshape(DEPTH * MOD_ROWS, cl), "gather_mod")
    mod_all = mod_all.reshape(N_DEV, DEPTH, MOD_ROWS, cl).transpose(1, 2, 0, 3).reshape(DEPTH, MOD_ROWS, N_DEV * cl)
    mine = lax.dynamic_slice(mod_all, (0, me, 0), (DEPTH, 1, N_DEV * cl))
    mods = jnp.concatenate([mine, mod_all[:, N_DEV:N_DEV + 1]], axis=1).reshape(DEPTH, 2, N_MOD, d)

    gu_b = w_ffn_gate_up.astype(BF16)
    dn_b = w_ffn_down.astype(BF16)
    wi_b = w_in.astype(BF16)
    wo_b = w_out.astype(BF16)
    wp_b = w_pool.astype(BF16)

    def ffn_shards(li, i):
        return [gu_b[li, i], dn_b[li, i]]

    def mix_shards(li):
        return [wi_b[li], wo_b[li]]

    def as_ffn_weights(gathered):
        return gathered[0].reshape(2, 4, d, fq), gathered[1].reshape(4, 2 * fr, d)

    def as_mix_weights(gathered):
        return gathered[0], gathered[1].reshape(N_DEV * wo_b.shape[1], d)

    tables = _rope_tables(s, n)
    mem_np, mem_t_np, inv_np = _pool_consts(l)
    mem, mem_t, inv = jnp.asarray(mem_np, BF16), jnp.asarray(mem_t_np, BF16), jnp.asarray(inv_np)
    first = gather_two_level([norm_g.reshape(DEPTH * 6, dl), gu_b[0, 0]], "gather_first")
    g_full = first[0].reshape(N_DEV, DEPTH, 6, dl).transpose(1, 2, 0, 3).reshape(DEPTH, 6, 1, N_DEV * dl)

    weights = {("ffn", 0, 0): (first[1].reshape(2, 4, d, fq), None)}
    saved = {}
    xcur = jnp.concatenate([x[0], ctx[0]], axis=0)
    for li in range(DEPTH):
        last = li == DEPTH - 1
        for i in range(2):
            tag = f"l{li}_ffn{i}"
            wgu, wd4 = weights[("ffn", li, i)]
            if i == 0:
                ex_up = Exchange(gathers=mix_shards(li) + ([dn_b[0, 0]] if wd4 is None else []))
                ex_dn = Exchange(gathers=[dn_b[li, 1]])
            elif not last:
                ex_up, ex_dn = Exchange(gathers=[gu_b[li + 1, 0]]), Exchange(gathers=[dn_b[li + 1, 0]])
            else:
                ex_up = ex_dn = None
            (hb, gu, a4), got_up = ffn_up(xcur, g_full[li, 4 * i], mods[li], wgu, s, 6 * i, tag + "_up", ex_up)
            if wd4 is None:
                wd4 = got_up.pop().reshape(4, 2 * fr, d)
                weights[("ffn", li, i)] = (wgu, wd4)
            (ff, xnext), got_dn = ffn_down(a4, wd4, xcur, g_full[li, 4 * i + 1], mods[li], s, 6 * i + 2, tag + "_down", ex_dn)
            saved[("ffn", li, i)] = (xcur, hb, gu, a4, ff)
            xcur = xnext
            if i == 0:
                weights[("mix", li)] = as_mix_weights(got_up)
                next_dn = got_dn
            elif not last:
                weights[("ffn", li + 1, 0)] = as_ffn_weights(got_up + got_dn)
            if i == 0:
                tag = f"l{li}_mix"
                win8, wout = weights[("mix", li)]
                (hb, qkv, u), _ = mix_in(xcur, g_full[li, 2], mods[li], win8, tables, s, tag + "_in")
                tb = bias_tables(na_rpb[li], tag + "_bias")
                (na, lse), got = na_fwd(qkv, tb, s, tag + "_na", Exchange(gathers=[gu_b[li, 1]]))
                weights[("ffn", li, 1)] = as_ffn_weights(got + next_dn)
                lse_c = None
                if not last:
                    na, lse_c = ctx_attn_fwd(qkv, na, s, tag + "_ctx_attn")
                py = pool_fwd(u, mem, inv, wp_b[li], pool_scale[li][None, :], nx, tag + "_pool")
                fm, xnext = mix_out(na, py, wout, xcur, g_full[li, 3], mods[li], s, tag + "_out")
                saved[("mix", li)] = (xcur, hb, qkv, u, tb, na, lse, lse_c, py, fm)
                xcur = xnext

    loss_local, dcur = loss_head(xcur, loss_target[0], "loss")
    loss = lax.psum(loss_local[0, 0], ("x", "y", "c"))

    recv = {"gu": lax.empty((N_DEV, 2 * DEPTH, d, fq), BF16), "dn": lax.empty((N_DEV, 2 * DEPTH, fr, d), BF16),
            "wi": lax.empty((N_DEV, DEPTH, d, IN_BLOCK), BF16), "wo": lax.empty((N_DEV, DEPTH, wo_b.shape[1], d), BF16)}
    pending = []

    def take(keys, gathers=()):
        nonlocal pending
        jobs = [(gr, recv[key], st) for key, gr, st in pending if key in keys]
        order = [key for key, _, _ in pending if key in keys]
        pending = [p for p in pending if p[0] not in keys]
        return Exchange(gathers=gathers, a2as=jobs), order

    def pad8(t):
        t = t.reshape(-1, d) if t.size % d == 0 else jnp.pad(t.reshape(-1), (0, -t.size % d)).reshape(-1, d)
        return jnp.pad(t, ((0, -t.shape[0] % 8), (0, 0)))

    def packed(parts):
        parts = [pad8(p) for p in parts]
        offs = np.cumsum([0] + [p.shape[0] for p in parts])
        return jnp.concatenate(parts + [jnp.zeros((-offs[-1] % 64, d), F32)], axis=0), offs

    def put(order, bufs):
        for key, buf in zip(order, bufs):
            recv[key] = buf

    d_rpb, d_wp, d_ps, d_mod, d_g = [], [], [], [], []
    for li in reversed(range(DEPTH)):
        reds = {}
        for i in (1, 0):
            tag = f"l{li}_ffn{i}"
            xin, hb, gu, a4, ff = saved[("ffn", li, i)]
            wgu, wd4 = weights[("ffn", li, i)]
            dff, red1 = post_bwd(ff, dcur, g_full[li, 4 * i + 1], mods[li], 6 * i + 2, 0.5, s, tag + "_post_bwd")
            early = []
            if (li, i) == (0, 0):
                early_small, early_offs = packed([jnp.stack(d_wp), jnp.stack(d_ps), jnp.stack(d_rpb)])
                early = [early_small]
            ex, _ = take((), early)
            g_dn, bufs = grad_weight(a4, dff, tag + "_dwdown", a_lead=4, per_step=4, exchange=ex)
            if early:
                early_sum = sum_devices(bufs[0], "sum_early_small_grads")
            pending += [("dn", g_dn.reshape(N_DEV, fr, d), 2 * li + i)]
            ex, order = take(("dn",))
            dgu, bufs = ffn_da(dff, wd4, gu, tag + "_da", ex)
            put(order, bufs)
            ex, order = take(("wi", "wo"))
            g_gu, bufs = grad_weight(hb, dgu.reshape(N_DEV, n, fq), tag + "_dwgu", b_lead=N_DEV, per_step=4, exchange=ex)
            put(order, bufs)
            pending += [("gu", g_gu, 2 * li + i)]
            ex, order = take(("gu",))
            (dcur, red2), bufs = ffn_dh(dgu, wgu, xin, dcur, g_full[li, 4 * i], mods[li], s, 6 * i, tag + "_dh", ex)
            put(order, bufs)
            reds[i] = (red1, red2)
            if i == 1:
                tag = f"l{li}_mix"
                xin, hb, qkv, u, tb, na, lse, lse_c, py, fm = saved[("mix", li)]
                win8, wout = weights[("mix", li)]
                dfm, redm1 = post_bwd(fm, dcur, g_full[li, 3], mods[li], 5, 1.0, s, tag + "_post_bwd")
                dmix = matmul_nt(dfm, wout, tag + "_dmix")
                g_wo = grad_wout(na, py, dfm, tag + "_dwout").reshape(N_DEV, wo_b.shape[1], d)
                du, gwp, gps = pool_bwd(dmix, u, mem, mem_t, inv, wp_b[li], pool_scale[li][None, :], nx, tag + "_pool_bwd")
                ex, order = take(("gu", "dn"))
                (dq, dk, dv, dtb), bufs = na_bwd(qkv, tb, na, dmix, lse, s, tag + "_na_bwd", ex)
                put(order, bufs)
                if li != DEPTH - 1:
                    dq, dk, dv = ctx_attn_bwd(qkv, na, dmix, lse_c, dq, dk, dv, s, tag + "_ctx_attn_bwd")
                grpb = bias_tables_bwd(dtb, tag + "_bias_bwd")
                dqkvu = qkv_bwd(dq, dk, dv, du, tables, tag + "_rope_bwd")
                (dcur, redm2), _ = mix_dh(dqkvu, win8, xin, dcur, g_full[li, 2], mods[li], s, tag + "_dh")
                g_wi, _ = grad_weight(hb, dqkvu, tag + "_dwin", b_cols=IN_BLOCK, per_step=4)
                pending += [("wi", g_wi, li), ("wo", g_wo, li)]
                d_rpb.insert(0, grpb)
                d_wp.insert(0, gwp)
                d_ps.insert(0, gps[0])
        (ra1, ra2), (rb1, rb2) = reds[0], reds[1]
        d_mod.insert(0, jnp.stack([ra2[:, 0], ra2[:, 1], ra1[:, 0], redm2[:, 0], redm2[:, 1], redm1[:, 0],
                                   rb2[:, 0], rb2[:, 1], rb1[:, 0]], axis=1))
        d_g.insert(0, jnp.stack([t[0] + t[1] for t in (ra2[:, 2], ra1[:, 1], redm2[:, 2], redm1[:, 1], rb2[:, 2], rb1[:, 1])]))
    grad_x = dcur[:s][None]

    small, offs = packed([jnp.stack(d_mod), jnp.stack(d_g)])
    ex, order = take(("gu", "dn", "wi", "wo"), [small])
    bufs = exchange_only(ex, "exchange_last")
    small_all = bufs[0]
    put(order, bufs[1:])
    small_sum = sum_devices(small_all, "sum_small_grads")

    n_mod_rows = DEPTH * 2 * N_MOD
    dmod_all = small_all[:, :n_mod_rows].reshape(N_DEV, DEPTH, 2, N_MOD * d)
    dmod_sum = small_sum[:n_mod_rows].reshape(DEPTH, 2, N_MOD * d)
    dm_rows = jnp.concatenate([dmod_all[:, :, 0].transpose(1, 0, 2), dmod_sum[:, 1:2],
                               jnp.zeros((DEPTH, MOD_ROWS - N_DEV - 1, N_MOD * d), F32)], axis=1)
    grad_b_mod = dmod_sum[:, 0] + dmod_sum[:, 1]
    dm_loc = lax.dynamic_slice(dm_rows, (0, 0, me * cl), (DEPTH, MOD_ROWS, cl))
    grad_w_mod, da_part = mod_bwd(cvecs, dm_loc, w_mod, "mod_bwd")
    da_all = all_gather(da_part, "gather_dcvec")
    grad_c_ctx = cctx_grad(da_all, c_ctx[None, :], "c_ctx_grad")[0]

    grad_norm_full = small_sum[offs[1]:offs[1] + DEPTH * 6].reshape(DEPTH, 6, d)
    grad_norm_g = lax.dynamic_slice(grad_norm_full, (0, 0, me * dl), (DEPTH, 6, dl))
    grad_w_pool = early_sum[early_offs[0]:early_offs[0] + w_pool.size // d].reshape(w_pool.shape)
    grad_pool_scale = early_sum[early_offs[1]:early_offs[1] + pool_scale.size // d].reshape(pool_scale.shape)
    grad_na_rpb = early_sum[early_offs[2]:early_offs[3]].reshape(-1)[:na_rpb.size].reshape(na_rpb.shape)

    def big_adam(key, w, m, v, name):
        shp = w.shape
        cols = shp[-1]
        outs = reduce_adamw(recv[key].reshape(N_DEV, -1, cols), w.reshape(-1, cols), m.reshape(-1, cols),
                            v.reshape(-1, cols), name)
        return tuple(t.reshape(shp) for t in outs)

    def small_adam(w, g, m, v, name):
        shp = w.shape
        cols = shp[-1]
        outs = adamw(w.reshape(-1, cols), g.reshape(-1, cols), m.reshape(-1, cols), v.reshape(-1, cols), name)
        return tuple(t.reshape(shp) for t in outs)

    b_gu = big_adam("gu", w_ffn_gate_up, m_w_ffn_gate_up, v_w_ffn_gate_up, "adam_gate_up")
    b_dn = big_adam("dn", w_ffn_down, m_w_ffn_down, v_w_ffn_down, "adam_down")
    b_wi = big_adam("wi", w_in, m_w_in, v_w_in, "adam_w_in")
    b_wo = big_adam("wo", w_out, m_w_out, v_w_out, "adam_w_out")
    a_cc = small_adam(c_ctx, grad_c_ctx, m_c_ctx, v_c_ctx, "adam_c_ctx")
    a_wm = small_adam(w_mod, grad_w_mod, m_w_mod, v_w_mod, "adam_w_mod")
    a_bm = small_adam(b_mod, grad_b_mod, m_b_mod, v_b_mod, "adam_b_mod")
    a_ng = small_adam(norm_g, grad_norm_g, m_norm_g, v_norm_g, "adam_norm_g")
    a_rp = small_adam(na_rpb, grad_na_rpb, m_na_rpb, v_na_rpb, "adam_na_rpb")
    a_wp = small_adam(w_pool, grad_w_pool, m_w_pool, v_w_pool, "adam_w_pool")
    a_ps = small_adam(pool_scale, grad_pool_scale, m_pool_scale, v_pool_scale, "adam_pool_scale")

    grads = (grad_c_ctx, grad_w_mod, grad_b_mod, grad_norm_g, b_gu[0], b_dn[0], b_wi[0], b_wo[0], grad_na_rpb, grad_w_pool, grad_pool_scale)
    deltas = (a_cc[0], a_wm[0], a_bm[0], a_ng[0], b_gu[1], b_dn[1], b_wi[1], b_wo[1], a_rp[0], a_wp[0], a_ps[0])
    new_m = (a_cc[1], a_wm[1], a_bm[1], a_ng[1], b_gu[2], b_dn[2], b_wi[2], b_wo[2], a_rp[1], a_wp[1], a_ps[1])
    new_v = (a_cc[2], a_wm[2], a_bm[2], a_ng[2], b_gu[3], b_dn[3], b_wi[3], b_wo[3], a_rp[2], a_wp[2], a_ps[2])
    return (loss, grad_x, *grads, *deltas, *new_m, *new_v)
```

```python
import functools
import math

import numpy as np
import jax
import jax.numpy as jnp
from jax import lax
from jax.experimental import pallas as pl
from jax.experimental.pallas import tpu as pltpu

F32 = jnp.float32
BF16 = jnp.bfloat16

N_DEV = 8
DEPTH = 2
GRID_W = 64
N_MOD = 9
NA_HEADS = 8
HEAD_DIM = 64
NA_WIDTH = NA_HEADS * HEAD_DIM
NA_KH = 8
NA_KW = 16
POOL_GROUPS = 4
POOL_CH = 128
POOL_WIDTH = POOL_GROUPS * POOL_CH
POOL_WINDOWS = (2, 4, 8, 16)
ROPE_THETA = 10000.0
ROPE_PAIRS = HEAD_DIM // 4
RMS_EPS = 1e-6
NEG_INF = -1e30
ATT_SCALE = HEAD_DIM ** -0.5

ADAM_LR = 0.001
ADAM_B1 = 0.9
ADAM_B2 = 0.999
ADAM_EPS = 1e-08
ADAM_WD = 0.01
ADAM_STEP = 10

TM = 256
LANES = 128
HEADS_PER_BLOCK = LANES // HEAD_DIM
N_HEAD_BLOCKS = NA_WIDTH // LANES
IN_BLOCK = 2 * LANES
N_QKV_BLOCKS = 3 * NA_WIDTH // IN_BLOCK
VMEM_LIMIT = 56 * 1024 * 1024
HIGHEST = lax.Precision.HIGHEST
MESH = pl.DeviceIdType.MESH
ANY = pl.BlockSpec(memory_space=pl.ANY)

NN = (((1,), (0,)), ((), ()))
NT = (((1,), (1,)), ((), ()))
TN = (((0,), (0,)), ((), ()))


def _pick(n, cands):
    for t in cands:
        if n % t == 0:
            return t
    raise ValueError(f"no tile for {n} among {cands}")


def _dot(a, b, dn=NN, precision=None):
    return lax.dot_general(a, b, dn, preferred_element_type=F32, precision=precision)


def _silu(x):
    return x * jax.nn.sigmoid(x)


def _dsilu(x):
    s = jax.nn.sigmoid(x)
    return s * (1.0 + x * (1.0 - s))


def _peer(mask):
    x, y, c = lax.axis_index("x"), lax.axis_index("y"), lax.axis_index("c")
    px = 1 - x if mask & 4 else x
    py = 1 - y if mask & 2 else y
    pc = 1 - c if mask & 1 else c
    return (px, py, pc), 4 * px + 2 * py + pc


class Exchange:
    def __init__(self, gathers=(), a2as=()):
        self.gathers = list(gathers)
        self.a2as = list(a2as)
        self.n_jobs = len(self.gathers) + len(self.a2as)

    def inputs(self):
        out = list(self.gathers)
        for v, buf, _ in self.a2as:
            out += [v, buf]
        return out

    def out_shapes(self):
        shapes = [jax.ShapeDtypeStruct((N_DEV,) + v.shape, v.dtype) for v in self.gathers]
        shapes += [jax.ShapeDtypeStruct(buf.shape, buf.dtype) for _, buf, _ in self.a2as]
        return shapes

    def aliases(self, n_in, n_out):
        ng = len(self.gathers)
        return {n_in + ng + 2 * k + 1: n_out + ng + k for k in range(len(self.a2as))}

    def scratch(self):
        per = N_DEV - 1
        return [pltpu.SemaphoreType.DMA((per * self.n_jobs,)), pltpu.SemaphoreType.DMA((per * self.n_jobs,)),
                pltpu.SemaphoreType.DMA((self.n_jobs,))]

    def _copies(self, in_refs, out_refs, sems, with_recvs):
        send_sems, recv_sems, local_sems = sems
        _, me = _peer(0)
        ng = len(self.gathers)
        local, sends, recvs = [], [], []
        for job in range(self.n_jobs):
            if job < ng:
                src_of = lambda pid, r=in_refs[job]: r
                dst_of = lambda pid, r=out_refs[job]: r.at[pid]
            else:
                k = job - ng
                stage = self.a2as[k][2]
                src_of = lambda pid, r=in_refs[ng + 2 * k]: r.at[pid]
                dst_of = lambda pid, r=out_refs[job], st=stage: r.at[pid, st]
            local.append(pltpu.make_async_copy(src_of(me), dst_of(me), local_sems.at[job]))
            for mask in range(1, N_DEV):
                peer, pid = _peer(mask)
                idx = job * (N_DEV - 1) + mask - 1
                sends.append(pltpu.make_async_remote_copy(
                    src_ref=src_of(pid), dst_ref=dst_of(me), send_sem=send_sems.at[idx],
                    recv_sem=recv_sems.at[idx], device_id=peer, device_id_type=MESH))
                if with_recvs:
                    recvs.append(pltpu.make_async_remote_copy(
                        src_ref=src_of(pid), dst_ref=dst_of(pid), send_sem=send_sems.at[idx],
                        recv_sem=recv_sems.at[idx], device_id=peer, device_id_type=MESH))
        return local, sends, recvs

    def start(self, in_refs, out_refs, sems):
        local, sends, _ = self._copies(in_refs, out_refs, sems, False)
        for cp in local + sends:
            cp.start()

    def wait(self, in_refs, out_refs, sems):
        local, sends, recvs = self._copies(in_refs, out_refs, sems, True)
        for cp in recvs:
            cp.wait_recv()
        for cp in sends:
            cp.wait_send()
        for cp in local:
            cp.wait()


def carrier_call(core, *, name, grid, in_specs, out_specs, out_shape, inputs, scratch_shapes=(), aliases=None,
                 exchange=None):
    aliases = dict(aliases or {})
    n_in, n_out, n_sc = len(in_specs), len(out_specs), len(scratch_shapes)
    sem = ("arbitrary",) * len(grid)
    params = pltpu.CompilerParams(dimension_semantics=sem, vmem_limit_bytes=VMEM_LIMIT)
    if exchange is None or exchange.n_jobs == 0:
        outs = pl.pallas_call(core, name=name, grid=grid, in_specs=list(in_specs), out_specs=tuple(out_specs),
                              out_shape=tuple(out_shape), scratch_shapes=list(scratch_shapes),
                              input_output_aliases=aliases, compiler_params=params)(*inputs)
        return list(outs), []
    x_in = exchange.inputs()
    x_out = exchange.out_shapes()
    aliases.update(exchange.aliases(n_in, n_out))

    def body(*refs):
        a = n_in + len(x_in)
        b = a + n_out + len(x_out)
        core_in, job_in = refs[:n_in], refs[n_in:a]
        core_out, job_out = refs[a:a + n_out], refs[a + n_out:b]
        core_sc, job_sc = refs[b:b + n_sc], refs[b + n_sc:]
        first = functools.reduce(lambda p, q: p & q, [pl.program_id(ax) == 0 for ax in range(len(grid))])
        last = functools.reduce(lambda p, q: p & q, [pl.program_id(ax) == g - 1 for ax, g in enumerate(grid)])

        @pl.when(first)
        def _():
            exchange.start(job_in, job_out, job_sc)

        core(*core_in, *core_out, *core_sc)

        @pl.when(last)
        def _():
            exchange.wait(job_in, job_out, job_sc)

    outs = pl.pallas_call(
        body, name=name, grid=grid, in_specs=list(in_specs) + [ANY] * len(x_in),
        out_specs=tuple(out_specs) + (ANY,) * len(x_out), out_shape=tuple(out_shape) + tuple(x_out),
        scratch_shapes=list(scratch_shapes) + exchange.scratch(), input_output_aliases=aliases,
        compiler_params=params)(*inputs, *x_in)
    return list(outs[:n_out]), list(outs[n_out:])


def exchange_only(exchange, name):
    def body(*refs):
        n_in, n_out = len(exchange.inputs()), len(exchange.out_shapes())
        job_in, job_out, sems = refs[:n_in], refs[n_in:n_in + n_out], refs[n_in + n_out:]
        exchange.start(job_in, job_out, sems)
        exchange.wait(job_in, job_out, sems)

    x_in = exchange.inputs()
    outs = pl.pallas_call(
        body, name=name, in_specs=[ANY] * len(x_in), out_specs=(ANY,) * len(exchange.out_shapes()),
        out_shape=tuple(exchange.out_shapes()), scratch_shapes=exchange.scratch(),
        input_output_aliases=exchange.aliases(0, 0))(*x_in)
    return list(outs)


def all_gather(v, name):
    return exchange_only(Exchange(gathers=[v]), name)[0]


def gather_two_level(vs, name):
    nv = len(vs)
    per = N_DEV - 1

    def body(*refs):
        v_refs, o_refs = refs[:nv], refs[nv:2 * nv]
        send_sems, recv_sems, local_sems = refs[2 * nv:]
        x, y, c = lax.axis_index("x"), lax.axis_index("y"), lax.axis_index("c")
        me, sibling = (x, y, c), (x, y, 1 - c)
        chips = [(1 - x, y), (x, 1 - y), (1 - x, 1 - y)]

        def copy(a, k, block, to, src=None):
            dst = o_refs[a].at[4 * block[0] + 2 * block[1] + block[2]]
            return pltpu.make_async_remote_copy(
                src_ref=dst if src is None else src, dst_ref=dst, send_sem=send_sems.at[a * per + k],
                recv_sem=recv_sems.at[a * per + k], device_id=to, device_id_type=MESH)

        mine = [pltpu.make_async_copy(v_refs[a], o_refs[a].at[4 * x + 2 * y + c], local_sems.at[a]) for a in range(nv)]
        first = []
        for a in range(nv):
            first.append(copy(a, 0, me, sibling, src=v_refs[a]))
            first += [copy(a, 1 + j, me, (*chip, c), src=v_refs[a]) for j, chip in enumerate(chips)]
        for cp in mine + first:
            cp.start()
        passed = []
        for a in range(nv):
            for j, chip in enumerate(chips):
                copy(a, 1 + j, (*chip, c), me).wait_recv()
                passed.append(copy(a, 4 + j, (*chip, c), sibling))
                passed[-1].start()
        for a in range(nv):
            copy(a, 0, sibling, me).wait_recv()
            for j, chip in enumerate(chips):
                copy(a, 4 + j, (*chip, 1 - c), me).wait_recv()
        for cp in first + passed:
            cp.wait_send()
        for cp in mine:
            cp.wait()

    outs = pl.pallas_call(
        body, name=name, in_specs=[ANY] * nv, out_specs=(ANY,) * nv,
        out_shape=tuple(jax.ShapeDtypeStruct((N_DEV,) + v.shape, v.dtype) for v in vs),
        scratch_shapes=[pltpu.SemaphoreType.DMA((per * nv,)), pltpu.SemaphoreType.DMA((per * nv,)),
                        pltpu.SemaphoreType.DMA((nv,))])(*vs)
    return list(outs)


def _rms(xf):
    return lax.rsqrt(jnp.mean(xf * xf, axis=-1, keepdims=True) + RMS_EPS)


def _is_ctx(i, tm, s):
    return (i * tm + lax.broadcasted_iota(jnp.int32, (tm, 1), 0)) >= s


def _by_tile_kind(i, tm, s, fn):
    n_latent = s // tm

    @pl.when(i < n_latent)
    def _():
        fn(None)

    @pl.when(i >= n_latent)
    def _():
        fn(_is_ctx(i, tm, s))


def _mod_rows(mod_ref, k, is_ctx):
    if is_ctx is None:
        return mod_ref[0, k:k + 1, :]
    return jnp.where(is_ctx, mod_ref[1, k:k + 1, :], mod_ref[0, k:k + 1, :])


def _norm_mod(xf, g, mod_ref, k_shift, k_scale, is_ctx):
    nrm = xf * _rms(xf) * g
    return (nrm * (1.0 + _mod_rows(mod_ref, k_scale, is_ctx)) + _mod_rows(mod_ref, k_shift, is_ctx)).astype(BF16)


def _post(xf, ff, g, mod_ref, k_gate, coef, is_ctx):
    return xf + coef * _mod_rows(mod_ref, k_gate, is_ctx) * (ff * _rms(ff) * g)


def _red_add(red_ref, first, is_ctx, rows):
    @pl.when(first)
    def _():
        red_ref[...] = jnp.zeros_like(red_ref)

    for r, val in enumerate(rows):
        tot = jnp.sum(val, axis=0, keepdims=True)
        if is_ctx is None:
            red_ref[0, r:r + 1, :] += tot
        else:
            ctx = jnp.sum(jnp.where(is_ctx, val, 0.0), axis=0, keepdims=True)
            red_ref[0, r:r + 1, :] += tot - ctx
            red_ref[1, r:r + 1, :] += ctx


def _pre_bwd(xf, dh, dxo, g, mod_ref, k_scale, is_ctx):
    r = _rms(xf)
    xhat = xf * r
    dn = dh * (1.0 + _mod_rows(mod_ref, k_scale, is_ctx))
    dxhat = dn * g
    dx = dxo + r * (dxhat - xhat * jnp.mean(dxhat * xhat, axis=-1, keepdims=True))
    return dx, [dh, dh * xhat]


def _finish_pre(red_ref, g, mod_ref, k_scale):
    sums = red_ref[:, 1:2, :]
    red_ref[:, 2:3, :] = (1.0 + mod_ref[:, k_scale:k_scale + 1, :]) * sums
    red_ref[:, 1:2, :] = g * sums


def _vec_spec(d):
    return pl.BlockSpec((1, d), lambda *_: (0, 0))


def _mod_whole(d):
    return pl.BlockSpec((2, N_MOD, d), lambda *_: (0, 0, 0))


def _red_whole(d):
    return pl.BlockSpec((2, 8, d), lambda *_: (0, 0, 0))


def _token_tile(n):
    return _pick(n, (768, 640, 512, 384, 256))


def _ffn_tile(n):
    return _pick(n, (528, 384, 640, 256))


def _resident(shape):
    zeros = (0,) * len(shape)
    return pl.BlockSpec(shape, lambda i: zeros, pipeline_mode=pl.Buffered(1))


def ffn_up(x, g, mod, wgu, s, k0, name, exchange=None):
    n, d = x.shape
    nk, fq = wgu.shape[1], wgu.shape[-1]
    tm = _ffn_tile(n)

    def core(x_ref, g_ref, mod_ref, w_ref, hb_ref, gu_ref, a_ref):
        i = pl.program_id(0)

        def prologue(is_ctx):
            hb_ref[...] = _norm_mod(x_ref[...], g_ref[...], mod_ref, k0, k0 + 1, is_ctx)

        _by_tile_kind(i, tm, s, prologue)
        h = hb_ref[...]
        for k in range(nk):
            gg = _dot(h, w_ref[0, k])
            uu = _dot(h, w_ref[1, k])
            gu_ref[0, k] = gg.astype(BF16)
            gu_ref[1, k] = uu.astype(BF16)
            a_ref[k] = (_silu(gg) * uu).astype(BF16)

    outs, xo = carrier_call(
        core, name=name, grid=(n // tm,),
        in_specs=[pl.BlockSpec((tm, d), lambda i: (i, 0)), _vec_spec(d), _mod_whole(d), _resident(wgu.shape)],
        out_specs=[pl.BlockSpec((tm, d), lambda i: (i, 0)),
                   pl.BlockSpec((2, nk, tm, fq), lambda i: (0, 0, i, 0)),
                   pl.BlockSpec((nk, tm, fq), lambda i: (0, i, 0))],
        out_shape=[jax.ShapeDtypeStruct((n, d), BF16), jax.ShapeDtypeStruct((2, nk, n, fq), BF16),
                   jax.ShapeDtypeStruct((nk, n, fq), BF16)],
        inputs=[x, g, mod, wgu], exchange=exchange)
    return outs, xo


def ffn_down(a4, wd4, x, g, mod, s, k_gate, name, exchange=None):
    n, d = x.shape
    nk, fq = wd4.shape[0], wd4.shape[1]
    tm = _ffn_tile(n)

    def core(a_ref, w_ref, x_ref, g_ref, mod_ref, f_ref, xo_ref):
        i = pl.program_id(0)
        ff = _dot(a_ref[0], w_ref[0])
        for k in range(1, nk):
            ff = ff + _dot(a_ref[k], w_ref[k])
        f_ref[...] = ff

        def epilogue(is_ctx):
            xo_ref[...] = _post(x_ref[...], f_ref[...], g_ref[...], mod_ref, k_gate, 0.5, is_ctx)

        _by_tile_kind(i, tm, s, epilogue)

    tile = pl.BlockSpec((tm, d), lambda i: (i, 0))
    outs, xo = carrier_call(
        core, name=name, grid=(n // tm,),
        in_specs=[pl.BlockSpec((nk, tm, fq), lambda i: (0, i, 0)), _resident(wd4.shape), tile, _vec_spec(d),
                  _mod_whole(d)],
        out_specs=[tile, tile],
        out_shape=[jax.ShapeDtypeStruct((n, d), F32), jax.ShapeDtypeStruct((n, d), F32)],
        inputs=[a4, wd4, x, g, mod], exchange=exchange)
    return outs, xo


def ffn_da(df, wd4, gu, name, exchange=None):
    n, d = df.shape
    nk, fq = wd4.shape[0], wd4.shape[1]
    tm = _ffn_tile(n)

    def core(df_ref, w_ref, gu_ref, o_ref):
        dfv = df_ref[...]
        for k in range(nk):
            da = _dot(dfv, w_ref[k], NT).astype(BF16)
            gg = gu_ref[0, k]
            uu = gu_ref[1, k]
            sg = jax.nn.sigmoid(gg.astype(F32)).astype(BF16)
            o_ref[0, k] = da * (uu * (sg * (1 + gg * (1 - sg))))
            o_ref[1, k] = da * (gg * sg)

    gu_spec = pl.BlockSpec((2, nk, tm, fq), lambda i: (0, 0, i, 0))
    outs, xo = carrier_call(
        core, name=name, grid=(n // tm,),
        in_specs=[pl.BlockSpec((tm, d), lambda i: (i, 0)), _resident(wd4.shape), gu_spec],
        out_specs=[gu_spec], out_shape=[jax.ShapeDtypeStruct(gu.shape, BF16)],
        inputs=[df, wd4, gu], exchange=exchange)
    return outs[0], xo


def ffn_dh(dgu, wgu, x, dxo, g, mod, s, k0, name, exchange=None):
    n, d = x.shape
    nk, fq = wgu.shape[1], wgu.shape[-1]
    tm = _ffn_tile(n)

    def core(dgu_ref, w_ref, x_ref, dxo_ref, g_ref, mod_ref, dx_ref, red_ref, dh_s):
        i = pl.program_id(0)
        dh = _dot(dgu_ref[0, 0], w_ref[0, 0], NT) + _dot(dgu_ref[1, 0], w_ref[1, 0], NT)
        for k in range(1, nk):
            dh = dh + _dot(dgu_ref[0, k], w_ref[0, k], NT) + _dot(dgu_ref[1, k], w_ref[1, k], NT)
        dh_s[...] = dh

        def epilogue(is_ctx):
            dx, sums = _pre_bwd(x_ref[...], dh_s[...], dxo_ref[...], g_ref[...], mod_ref, k0 + 1, is_ctx)
            dx_ref[...] = dx
            _red_add(red_ref, i == 0, is_ctx, sums)

        _by_tile_kind(i, tm, s, epilogue)

        @pl.when(i == n // tm - 1)
        def _():
            _finish_pre(red_ref, g_ref[...], mod_ref, k0 + 1)

    tile = pl.BlockSpec((tm, d), lambda i: (i, 0))
    outs, xo = carrier_call(
        core, name=name, grid=(n // tm,),
        in_specs=[pl.BlockSpec((2, nk, tm, fq), lambda i: (0, 0, i, 0)), _resident(wgu.shape), tile, tile,
                  _vec_spec(d), _mod_whole(d)],
        out_specs=[tile, _red_whole(d)],
        out_shape=[jax.ShapeDtypeStruct((n, d), F32), jax.ShapeDtypeStruct((2, 8, d), F32)],
        scratch_shapes=[pltpu.VMEM((tm, d), F32)], inputs=[dgu, wgu, x, dxo, g, mod], exchange=exchange)
    return outs, xo


def grad_weight(a, b, name, a_lead=None, b_lead=None, b_cols=None, per_step=1, exchange=None):
    n = a.shape[-2]
    ka = a.shape[-1]
    kb = b_cols or b.shape[-1]
    nj = a_lead or b_lead or (b.shape[-1] // b_cols)
    tk = _pick(n, (1408, 1024, 768, 640, 512, 256))
    nk = n // tk
    ps = per_step
    assert nj % ps == 0

    def core(a_ref, b_ref, o_ref, acc):
        kk = pl.program_id(1)

        @pl.when(kk == 0)
        def _():
            acc[...] = jnp.zeros_like(acc)

        if b_cols:
            acc[...] += _dot(a_ref[...], b_ref[...], TN)
        else:
            for t in range(ps):
                acc[t] += _dot(a_ref[t] if a_lead else a_ref[...], b_ref[t] if b_lead else b_ref[...], TN)

        @pl.when(kk == nk - 1)
        def _():
            for t in range(ps):
                o_ref[t] = (acc[:, t * kb:(t + 1) * kb] if b_cols else acc[t]).astype(BF16)

    a_spec = (pl.BlockSpec((ps, tk, ka), lambda j, kk: (j, kk, 0)) if a_lead
              else pl.BlockSpec((tk, ka), lambda j, kk: (kk, 0)))
    if b_lead:
        b_spec = pl.BlockSpec((ps, tk, kb), lambda j, kk: (j, kk, 0))
    elif b_cols:
        b_spec = pl.BlockSpec((tk, ps * kb), lambda j, kk: (kk, j))
    else:
        b_spec = pl.BlockSpec((tk, kb), lambda j, kk: (kk, 0))
    outs, xo = carrier_call(
        core, name=name, grid=(nj // ps, nk), in_specs=[a_spec, b_spec],
        out_specs=[pl.BlockSpec((ps, ka, kb), lambda j, kk: (j, 0, 0))],
        out_shape=[jax.ShapeDtypeStruct((nj, ka, kb), BF16)],
        scratch_shapes=[pltpu.VMEM((ka, ps * kb) if b_cols else (ps, ka, kb), F32)], inputs=[a, b], exchange=exchange)
    return outs[0], xo


def post_bwd(f, dxo, g, mod, k_gate, coef, s, name):
    n, d = f.shape
    tm = _token_tile(n)

    def core(f_ref, dxo_ref, g_ref, mod_ref, df_ref, red_ref):
        i = pl.program_id(0)

        def body(is_ctx):
            ff = f_ref[...]
            dxo_ = dxo_ref[...]
            r = _rms(ff)
            fn = ff * r
            cg = (coef * _mod_rows(mod_ref, k_gate, is_ctx)) * g_ref[...]
            w = dxo_ * fn
            dfn = dxo_ * cg
            df_ref[...] = (r * (dfn - fn * jnp.mean(w * cg, axis=-1, keepdims=True))).astype(BF16)
            _red_add(red_ref, i == 0, is_ctx, [w])

        _by_tile_kind(i, tm, s, body)

        @pl.when(i == n // tm - 1)
        def _():
            sums = red_ref[:, 0:1, :]
            red_ref[:, 1:2, :] = (coef * mod_ref[:, k_gate:k_gate + 1, :]) * sums
            red_ref[:, 0:1, :] = (coef * g_ref[...]) * sums

    tile = pl.BlockSpec((tm, d), lambda i: (i, 0))
    outs, _ = carrier_call(
        core, name=name, grid=(n // tm,), in_specs=[tile, tile, _vec_spec(d), _mod_whole(d)],
        out_specs=[tile, _red_whole(d)],
        out_shape=[jax.ShapeDtypeStruct((n, d), BF16), jax.ShapeDtypeStruct((2, 8, d), F32)],
        inputs=[f, dxo, g, mod])
    return outs


def matmul_nt(a, b, name):
    m, k = a.shape
    n = b.shape[0]
    tm = _token_tile(m)

    def core(a_ref, b_ref, o_ref):
        o_ref[...] = _dot(a_ref[...], b_ref[...], NT)

    outs, _ = carrier_call(
        core, name=name, grid=(m // tm,),
        in_specs=[pl.BlockSpec((tm, k), lambda i: (i, 0)), pl.BlockSpec((n, k), lambda i: (0, 0))],
        out_specs=[pl.BlockSpec((tm, n), lambda i: (i, 0))], out_shape=[jax.ShapeDtypeStruct((m, n), F32)],
        inputs=[a, b])
    return outs[0]


def _rope_tables(s, n):
    t = jnp.arange(n)
    lane = jnp.arange(LANES)
    dd = lane % HEAD_DIM
    inv = ROPE_THETA ** (-(dd % ROPE_PAIRS).astype(F32) / ROPE_PAIRS)
    pos = jnp.where(dd[None, :] < HEAD_DIM // 2, (t // GRID_W)[:, None], (t % GRID_W)[:, None]).astype(F32)
    ang = pos * inv[None, :]
    live = (t < s)[:, None]
    first = ((dd % (2 * ROPE_PAIRS)) < ROPE_PAIRS)[None, :]
    cos = jnp.where(live, jnp.cos(ang), 1.0)
    sin = jnp.where(live, jnp.sin(ang), 0.0)
    sa = jnp.where(first, -sin, 0.0)
    sb = jnp.where(first, 0.0, sin)
    return cos.astype(F32), sa.astype(F32), sb.astype(F32)


def _rope(xv, cos, sa, sb):
    return (xv * cos + pltpu.roll(xv, LANES - ROPE_PAIRS, 1) * sa + pltpu.roll(xv, ROPE_PAIRS, 1) * sb)


def mix_in(x, g, mod, win8, tables, s, name, exchange=None):
    n, d = x.shape
    tm = _token_tile(n)
    nb = win8.shape[0]
    n_rope = 2 * NA_WIDTH // IN_BLOCK

    def core(x_ref, g_ref, mod_ref, w_ref, c_ref, sa_ref, sb_ref, hb_ref, qkv_ref, u_ref):
        i = pl.program_id(0)
        hb = _norm_mod(x_ref[...], g_ref[...], mod_ref, 3, 4, _is_ctx(i, tm, s))
        hb_ref[...] = hb
        cos, sa, sb = c_ref[...], sa_ref[...], sb_ref[...]
        for j in range(nb):
            y = _dot(hb, w_ref[j])
            if j < n_rope:
                for b in range(IN_BLOCK // LANES):
                    sl = slice(b * LANES, (b + 1) * LANES)
                    qkv_ref[:, j * IN_BLOCK + b * LANES:j * IN_BLOCK + (b + 1) * LANES] = (
                        _rope(y[:, sl], cos, sa, sb).astype(BF16))
            elif j < N_QKV_BLOCKS:
                qkv_ref[:, j * IN_BLOCK:(j + 1) * IN_BLOCK] = y.astype(BF16)
            else:
                u_ref[:, (j - N_QKV_BLOCKS) * IN_BLOCK:(j - N_QKV_BLOCKS + 1) * IN_BLOCK] = y

    tab = pl.BlockSpec((tm, LANES), lambda i: (i, 0))
    row = lambda w: pl.BlockSpec((tm, w), lambda i: (i, 0))
    outs, xo = carrier_call(
        core, name=name, grid=(n // tm,),
        in_specs=[row(d), _vec_spec(d), _mod_whole(d), pl.BlockSpec((nb, d, IN_BLOCK), lambda i: (0, 0, 0)),
                  tab, tab, tab],
        out_specs=[row(d), row(3 * NA_WIDTH), row(POOL_WIDTH)],
        out_shape=[jax.ShapeDtypeStruct((n, d), BF16), jax.ShapeDtypeStruct((n, 3 * NA_WIDTH), BF16),
                   jax.ShapeDtypeStruct((n, POOL_WIDTH), F32)],
        inputs=[x, g, mod, win8, *tables], exchange=exchange)
    return outs, xo


def qkv_bwd(dq, dk, dv, du, tables, name):
    n = dq.shape[0]
    w = NA_WIDTH

    def core(dq_ref, dk_ref, dv_ref, du_ref, c_ref, sa_ref, sb_ref, o_ref):
        cos, sa, sb = c_ref[...], -sa_ref[...], -sb_ref[...]
        for b in range(N_HEAD_BLOCKS):
            sl = slice(b * LANES, (b + 1) * LANES)
            o_ref[:, b * LANES:(b + 1) * LANES] = _rope(dq_ref[:, sl], cos, sa, sb).astype(BF16)
            o_ref[:, w + b * LANES:w + (b + 1) * LANES] = _rope(dk_ref[:, sl], cos, sa, sb).astype(BF16)
        o_ref[:, 2 * w:3 * w] = dv_ref[...].astype(BF16)
        o_ref[:, 3 * w:] = du_ref[...].astype(BF16)

    tm = _token_tile(n)
    tab = pl.BlockSpec((tm, LANES), lambda i: (i, 0))
    tile = pl.BlockSpec((tm, w), lambda i: (i, 0))
    outs, _ = carrier_call(
        core, name=name, grid=(n // tm,), in_specs=[tile, tile, tile, tile, tab, tab, tab],
        out_specs=[pl.BlockSpec((tm, 4 * w), lambda i: (i, 0))],
        out_shape=[jax.ShapeDtypeStruct((n, 4 * w), BF16)], inputs=[dq, dk, dv, du, *tables])
    return outs[0]


def mix_out(na, py, wout, x, g, mod, s, name):
    n, d = x.shape
    tm = _token_tile(n)
    half = na.shape[1]

    def core(na_ref, py_ref, w_ref, x_ref, g_ref, mod_ref, f_ref, xo_ref):
        i = pl.program_id(0)
        ff = _dot(na_ref[...], w_ref[:half, :]) + _dot(py_ref[...], w_ref[half:, :])
        f_ref[...] = ff
        xo_ref[...] = _post(x_ref[...], ff, g_ref[...], mod_ref, 5, 1.0, _is_ctx(i, tm, s))

    tile = pl.BlockSpec((tm, d), lambda i: (i, 0))
    htile = pl.BlockSpec((tm, half), lambda i: (i, 0))
    outs, _ = carrier_call(
        core, name=name, grid=(n // tm,),
        in_specs=[htile, htile, pl.BlockSpec((2 * half, d), lambda i: (0, 0)), tile, _vec_spec(d), _mod_whole(d)],
        out_specs=[tile, tile],
        out_shape=[jax.ShapeDtypeStruct((n, d), F32), jax.ShapeDtypeStruct((n, d), F32)],
        inputs=[na, py, wout, x, g, mod])
    return outs


def grad_wout(na, py, dfm, name):
    n, half = na.shape
    d = dfm.shape[1]
    tk = _pick(n, (1408, 1024, 768, 640, 512, 256))
    nk = n // tk

    def core(na_ref, py_ref, b_ref, o_ref, acc):
        hh, kk = pl.program_id(0), pl.program_id(1)

        @pl.when(kk == 0)
        def _():
            acc[...] = jnp.zeros_like(acc)

        @pl.when(hh == 0)
        def _():
            acc[...] += _dot(na_ref[...], b_ref[...], TN)

        @pl.when(hh == 1)
        def _():
            acc[...] += _dot(py_ref[...], b_ref[...], TN)

        @pl.when(kk == nk - 1)
        def _():
            o_ref[0] = acc[...].astype(BF16)

    htile = pl.BlockSpec((tk, half), lambda hh, kk: (kk, 0))
    outs, _ = carrier_call(
        core, name=name, grid=(2, nk), in_specs=[htile, htile, pl.BlockSpec((tk, d), lambda hh, kk: (kk, 0))],
        out_specs=[pl.BlockSpec((1, half, d), lambda hh, kk: (hh, 0, 0))],
        out_shape=[jax.ShapeDtypeStruct((2, half, d), BF16)],
        scratch_shapes=[pltpu.VMEM((half, d), F32)], inputs=[na, py, dfm])
    return outs[0]


def mix_dh(dqkvu, win8, x, dxo, g, mod, s, name, exchange=None):
    n, d = x.shape
    tm = _token_tile(n)
    nb = win8.shape[0]

    def core(dq_ref, w_ref, x_ref, dxo_ref, g_ref, mod_ref, dx_ref, red_ref):
        i = pl.program_id(0)
        dh = _dot(dq_ref[:, :IN_BLOCK], w_ref[0], NT)
        for j in range(1, nb):
            dh = dh + _dot(dq_ref[:, j * IN_BLOCK:(j + 1) * IN_BLOCK], w_ref[j], NT)
        is_ctx = _is_ctx(i, tm, s)
        dx, sums = _pre_bwd(x_ref[...], dh, dxo_ref[...], g_ref[...], mod_ref, 4, is_ctx)
        dx_ref[...] = dx
        _red_add(red_ref, i == 0, is_ctx, sums)

        @pl.when(i == n // tm - 1)
        def _():
            _finish_pre(red_ref, g_ref[...], mod_ref, 4)

    tile = pl.BlockSpec((tm, d), lambda i: (i, 0))
    outs, xo = carrier_call(
        core, name=name, grid=(n // tm,),
        in_specs=[pl.BlockSpec((tm, nb * IN_BLOCK), lambda i: (i, 0)),
                  pl.BlockSpec((nb, d, IN_BLOCK), lambda i: (0, 0, 0)), tile, tile, _vec_spec(d), _mod_whole(d)],
        out_specs=[tile, _red_whole(d)],
        out_shape=[jax.ShapeDtypeStruct((n, d), F32), jax.ShapeDtypeStruct((2, 8, d), F32)],
        inputs=[dqkvu, win8, x, dxo, g, mod], exchange=exchange)
    return outs, xo


def _na_consts():
    j = np.arange(GRID_W)
    col_start = np.clip(j - NA_KW // 2, 0, GRID_W - NA_KW)
    valid = (j[None, :] >= col_start[:, None]) & (j[None, :] < col_start[:, None] + NA_KW)
    dc = np.clip(j[None, :] - j[:, None] + NA_KW - 1, 0, 2 * NA_KW - 2)
    onehot = np.zeros((LANES, GRID_W, GRID_W), np.float32)
    for d in range(2 * NA_KW - 1):
        onehot[d] = ((dc == d) & valid).astype(np.float32)
    negmask = np.where(valid, 0.0, NEG_INF).astype(np.float32)
    return onehot.reshape(LANES, GRID_W * GRID_W), np.tile(negmask, (1, NA_KH))


def bias_tables(rpb, name):
    onehot, negmask = _na_consts()
    nj = 2 * NA_KH - 1
    rows = NA_HEADS * nj
    a = jnp.pad(rpb.reshape(rows, 2 * NA_KW - 1), ((0, 0), (0, LANES - (2 * NA_KW - 1))))

    def body(a_ref, e_ref, o_ref):
        o_ref[...] = _dot(a_ref[...], e_ref[...], precision=HIGHEST)

    t = pl.pallas_call(
        body, name=name, out_shape=jax.ShapeDtypeStruct((rows, GRID_W * GRID_W), F32),
        in_specs=[pl.BlockSpec(memory_space=pltpu.VMEM)] * 2,
        out_specs=pl.BlockSpec(memory_space=pltpu.VMEM),
    )(a, jnp.asarray(onehot))
    t = t.reshape(NA_HEADS, nj, GRID_W, GRID_W)
    tb = jnp.stack([t[:, j0:j0 + NA_KH] for j0 in range(NA_KH)])
    tb = tb.transpose(0, 1, 3, 2, 4).reshape(NA_KH, NA_HEADS, GRID_W, NA_KH * GRID_W)
    return tb + jnp.asarray(negmask)[None, None]


def bias_tables_bwd(dtb, name):
    onehot, _ = _na_consts()
    nj = 2 * NA_KH - 1
    d5 = dtb.reshape(NA_KH, NA_HEADS, GRID_W, NA_KH, GRID_W).transpose(0, 3, 1, 2, 4)
    d2 = d5.reshape(NA_KH * NA_KH * NA_HEADS, GRID_W * GRID_W)

    def body(d_ref, e_ref, o_ref):
        r = _dot(d_ref[...], e_ref[...], NT, precision=HIGHEST)
        for j in range(nj):
            acc = jnp.zeros((NA_HEADS, LANES), F32)
            for j0 in range(NA_KH):
                kk = j - j0
                if 0 <= kk < NA_KH:
                    base = (j0 * NA_KH + kk) * NA_HEADS
                    acc = acc + r[base:base + NA_HEADS, :]
            o_ref[j] = acc

    out = pl.pallas_call(
        body, name=name, out_shape=jax.ShapeDtypeStruct((nj, NA_HEADS, LANES), F32),
        in_specs=[pl.BlockSpec(memory_space=pltpu.VMEM)] * 2,
        out_specs=pl.BlockSpec(memory_space=pltpu.VMEM),
        compiler_params=pltpu.CompilerParams(vmem_limit_bytes=VMEM_LIMIT),
    )(d2, jnp.asarray(onehot))
    return out[:, :, :2 * NA_KW - 1].transpose(1, 0, 2)


def _head_masks():
    lane = lax.broadcasted_iota(jnp.int32, (1, LANES), 1)
    return [(lane >= h * HEAD_DIM) & (lane < (h + 1) * HEAD_DIM) for h in range(HEADS_PER_BLOCK)]


def _row_window(r, rows):
    rs = jnp.clip(r - NA_KH // 2, 0, rows - NA_KH)
    return rs - r + NA_KH - 1, pl.multiple_of(rs * GRID_W, GRID_W)


def _stack_heads(t, masks):
    return jnp.concatenate([jnp.where(mk, t, jnp.zeros_like(t)) for mk in masks], axis=0)


def _unstack_heads(t2, masks):
    out = t2[(HEADS_PER_BLOCK - 1) * GRID_W:, :]
    for h in reversed(range(HEADS_PER_BLOCK - 1)):
        out = jnp.where(masks[h], t2[h * GRID_W:(h + 1) * GRID_W, :], out)
    return out


NA_ROWS_PER_STEP = 4
NA_STEP = NA_ROWS_PER_STEP * GRID_W
SLAB = NA_KH * GRID_W
K_COL = N_HEAD_BLOCKS
V_COL = 2 * N_HEAD_BLOCKS


def na_fwd(qkv, tb, s, name, exchange=None):
    n = qkv.shape[0]
    l = n - s
    rows = s // GRID_W
    rr = NA_ROWS_PER_STEP
    x_steps = rows // rr

    def core(q_ref, k_ref, v_ref, kc_ref, vc_ref, tb_ref, o_ref, lse_ref):
        rb = pl.program_id(1)

        @pl.when(rb >= x_steps)
        def _():
            o_ref[...] = jnp.zeros_like(o_ref)
            lse_ref[...] = jnp.zeros_like(lse_ref)

        @pl.when(rb < x_steps)
        def _():
            masks = _head_masks()
            kcb, vcb = kc_ref[...], vc_ref[...]
            hq = HEADS_PER_BLOCK * GRID_W
            wins, q2s = [], []
            for t in range(rr):
                j0, off = _row_window(rb * rr + t, rows)
                wins.append((j0, off))
                q2s.append(_stack_heads(q_ref[t * GRID_W:(t + 1) * GRID_W, :] * ATT_SCALE, masks))
            s_ctx_all = _dot(jnp.concatenate(q2s, axis=0), kcb, NT)
            scores = []
            for t, (j0, off) in enumerate(wins):
                s_loc = _dot(q2s[t], k_ref[pl.ds(off, SLAB), :], NT) + tb_ref[j0].reshape(hq, SLAB)
                scores.append((s_loc, s_ctx_all[t * hq:(t + 1) * hq, :]))
            probs = []
            for s_loc, s_ctx in scores:
                m = jnp.maximum(jnp.max(s_loc, axis=-1, keepdims=True), jnp.max(s_ctx, axis=-1, keepdims=True))
                p_loc = jnp.exp(s_loc - m)
                p_ctx = jnp.exp(s_ctx - m)
                den = jnp.sum(p_loc, axis=-1, keepdims=True) + jnp.sum(p_ctx, axis=-1, keepdims=True)
                probs.append((p_loc.astype(BF16), p_ctx.astype(BF16), den, m + jnp.log(den)))
            o_ctx_all = _dot(jnp.concatenate([p[1] for p in probs], axis=0), vcb)
            for t, (p_loc, p_ctx, den, lse2) in enumerate(probs):
                o2 = (_dot(p_loc, v_ref[pl.ds(wins[t][1], SLAB), :]) + o_ctx_all[t * hq:(t + 1) * hq, :]) / den
                o_ref[t * GRID_W:(t + 1) * GRID_W, :] = _unstack_heads(o2, masks).astype(BF16)
                lse_ref[0, t * GRID_W:(t + 1) * GRID_W, :] = _unstack_heads(lse2, masks)

    cb = s // l
    outs, xo = carrier_call(
        core, name=name, grid=(N_HEAD_BLOCKS, n // NA_STEP),
        in_specs=[pl.BlockSpec((NA_STEP, LANES), lambda hb, rb: (jnp.minimum(rb, x_steps - 1), hb)),
                  pl.BlockSpec((s, LANES), lambda hb, rb: (0, K_COL + hb)),
                  pl.BlockSpec((s, LANES), lambda hb, rb: (0, V_COL + hb)),
                  pl.BlockSpec((l, LANES), lambda hb, rb: (cb, K_COL + hb)),
                  pl.BlockSpec((l, LANES), lambda hb, rb: (cb, V_COL + hb)),
                  pl.BlockSpec((NA_KH, HEADS_PER_BLOCK, GRID_W, SLAB), lambda hb, rb: (0, hb, 0, 0))],
        out_specs=[pl.BlockSpec((NA_STEP, LANES), lambda hb, rb: (rb, hb)),
                   pl.BlockSpec((1, NA_STEP, LANES), lambda hb, rb: (hb, rb, 0))],
        out_shape=[jax.ShapeDtypeStruct((n, NA_WIDTH), BF16), jax.ShapeDtypeStruct((N_HEAD_BLOCKS, n, LANES), F32)],
        inputs=[qkv, qkv, qkv, qkv, qkv, tb], exchange=exchange)
    return outs, xo


def na_bwd(qkv, tb, o, dmix, lse, s, name, exchange=None):
    n = qkv.shape[0]
    l = n - s
    rows = s // GRID_W
    rr = NA_ROWS_PER_STEP
    x_steps = rows // rr

    def core(q_ref, k_ref, v_ref, kc_ref, vc_ref, tb_ref, o_ref, do_ref, lse_ref, dq_ref, dk_ref, dv_ref, dtb_ref):
        rb = pl.program_id(1)

        @pl.when(rb == 0)
        def _():
            dk_ref[...] = jnp.zeros_like(dk_ref)
            dv_ref[...] = jnp.zeros_like(dv_ref)
            dtb_ref[...] = jnp.zeros_like(dtb_ref)

        @pl.when(rb >= x_steps)
        def _():
            dq_ref[...] = jnp.zeros_like(dq_ref)

        @pl.when(rb < x_steps)
        def _():
            masks = _head_masks()
            kcb, vcb = kc_ref[...], vc_ref[...]
            hq = HEADS_PER_BLOCK * GRID_W
            rows1 = []
            for t in range(rr):
                j0, off = _row_window(rb * rr + t, rows)
                sl = slice(t * GRID_W, (t + 1) * GRID_W)
                q2 = _stack_heads(q_ref[sl, :] * ATT_SCALE, masks)
                do_f = do_ref[sl, :]
                do2 = _stack_heads(do_f.astype(BF16), masks)
                dd = do_f * o_ref[sl, :].astype(F32)
                delta2 = jnp.concatenate(
                    [jnp.sum(jnp.where(mk, dd, 0.0), axis=-1, keepdims=True) for mk in masks], axis=0)
                lse_t = lse_ref[0, sl, :]
                lse2 = jnp.concatenate(
                    [lse_t[:, h * HEAD_DIM:h * HEAD_DIM + 1] for h in range(HEADS_PER_BLOCK)], axis=0)
                rows1.append((j0, off, q2, do2, lse2, delta2))
            q2_all = jnp.concatenate([r[2] for r in rows1], axis=0)
            do2_all = jnp.concatenate([r[3] for r in rows1], axis=0)
            lse2_all = jnp.concatenate([r[4] for r in rows1], axis=0)
            delta2_all = jnp.concatenate([r[5] for r in rows1], axis=0)
            s_ctx_all = _dot(q2_all, kcb, NT) - lse2_all
            dp_ctx_all = _dot(do2_all, vcb, NT) - delta2_all
            stage1 = []
            for j0, off, q2, do2, lse2, delta2 in rows1:
                kslab = k_ref[pl.ds(off, SLAB), :]
                vslab = v_ref[pl.ds(off, SLAB), :]
                s_loc = _dot(q2, kslab, NT) + tb_ref[j0].reshape(hq, SLAB) - lse2
                dp_loc = _dot(do2, vslab, NT) - delta2
                stage1.append((j0, off, q2, do2, s_loc, dp_loc))
            p_ctx_all = jnp.exp(s_ctx_all)
            ds_ctx_all = (p_ctx_all * dp_ctx_all).astype(BF16)
            p_ctx_all = p_ctx_all.astype(BF16)
            stage2 = []
            for j0, off, q2, do2, s_loc, dp_loc in stage1:
                p_loc = jnp.exp(s_loc)
                ds_loc = p_loc * dp_loc
                dtb_ref[j0] += ds_loc.reshape(HEADS_PER_BLOCK, GRID_W, SLAB)
                stage2.append((off, q2, do2, p_loc.astype(BF16), ds_loc.astype(BF16)))
            dq_ctx_all = _dot(ds_ctx_all, kcb)
            for t, (off, q2, do2, p_loc, ds_loc) in enumerate(stage2):
                dq2 = (_dot(ds_loc, k_ref[pl.ds(off, SLAB), :]) + dq_ctx_all[t * hq:(t + 1) * hq, :]) * ATT_SCALE
                dq_ref[t * GRID_W:(t + 1) * GRID_W, :] = _unstack_heads(dq2, masks)
                dk_ref[pl.ds(off, SLAB), :] += _dot(ds_loc, q2, TN)
                dv_ref[pl.ds(off, SLAB), :] += _dot(p_loc, do2, TN)
            dk_ref[s:, :] += _dot(ds_ctx_all, q2_all, TN)
            dv_ref[s:, :] += _dot(p_ctx_all, do2_all, TN)

    cb = s // l
    clamp = lambda hb, rb: (jnp.minimum(rb, x_steps - 1), hb)
    tile_in = pl.BlockSpec((NA_STEP, LANES), clamp)
    whole_out = pl.BlockSpec((n, LANES), lambda hb, rb: (0, hb))
    tbs = pl.BlockSpec((NA_KH, HEADS_PER_BLOCK, GRID_W, SLAB), lambda hb, rb: (0, hb, 0, 0))
    f32n = jax.ShapeDtypeStruct((n, NA_WIDTH), F32)
    outs, xo = carrier_call(
        core, name=name, grid=(N_HEAD_BLOCKS, n // NA_STEP),
        in_specs=[tile_in,
                  pl.BlockSpec((s, LANES), lambda hb, rb: (0, K_COL + hb)),
                  pl.BlockSpec((s, LANES), lambda hb, rb: (0, V_COL + hb)),
                  pl.BlockSpec((l, LANES), lambda hb, rb: (cb, K_COL + hb)),
                  pl.BlockSpec((l, LANES), lambda hb, rb: (cb, V_COL + hb)),
                  tbs, tile_in, tile_in,
                  pl.BlockSpec((1, NA_STEP, LANES), lambda hb, rb: (hb, jnp.minimum(rb, x_steps - 1), 0))],
        out_specs=[pl.BlockSpec((NA_STEP, LANES), lambda hb, rb: (rb, hb)), whole_out, whole_out, tbs],
        out_shape=[f32n, f32n, f32n, jax.ShapeDtypeStruct((NA_KH, NA_HEADS, GRID_W, SLAB), F32)],
        inputs=[qkv, qkv, qkv, qkv, qkv, tb, o, dmix, lse], exchange=exchange)
    return outs, xo


def ctx_attn_fwd(qkv, na, s, name):
    n = qkv.shape[0]
    l = n - s
    cb = s // l

    def core(q_ref, k_ref, v_ref, na_in, o_ref, lse_ref):
        masks = _head_masks()
        qt, kb, vb = q_ref[...], k_ref[...], v_ref[...]
        o_acc = jnp.zeros((l, LANES), F32)
        lse_acc = jnp.zeros((l, LANES), F32)
        for h in range(HEADS_PER_BLOCK):
            qh = jnp.where(masks[h], qt, jnp.zeros_like(qt))
            sc = _dot(qh, kb, NT) * ATT_SCALE
            m = jnp.max(sc, axis=-1, keepdims=True)
            p = jnp.exp(sc - m)
            den = jnp.sum(p, axis=-1, keepdims=True)
            o_acc = jnp.where(masks[h], _dot(p.astype(BF16), vb) / den, o_acc)
            lse_acc = jnp.where(masks[h], m + jnp.log(den), lse_acc)
        o_ref[...] = o_acc.astype(BF16)
        lse_ref[0] = lse_acc

    outs, _ = carrier_call(
        core, name=name, grid=(N_HEAD_BLOCKS,),
        in_specs=[pl.BlockSpec((l, LANES), lambda hb: (cb, hb)), pl.BlockSpec((l, LANES), lambda hb: (cb, K_COL + hb)),
                  pl.BlockSpec((l, LANES), lambda hb: (cb, V_COL + hb)), ANY],
        out_specs=[pl.BlockSpec((l, LANES), lambda hb: (cb, hb)), pl.BlockSpec((1, l, LANES), lambda hb: (hb, 0, 0))],
        out_shape=[jax.ShapeDtypeStruct(na.shape, BF16), jax.ShapeDtypeStruct((N_HEAD_BLOCKS, l, LANES), F32)],
        inputs=[qkv, qkv, qkv, na], aliases={3: 0})
    return outs


def ctx_attn_bwd(qkv, na, dmix, lse, dq, dk, dv, s, name):
    n = qkv.shape[0]
    l = n - s
    cb = s // l

    def core(q_ref, k_ref, v_ref, o_ref, do_ref, lse_ref, dq_in, dk_in, dv_in, dq_ref, dk_ref, dv_ref):
        masks = _head_masks()
        qt, kb, vb = q_ref[...], k_ref[...], v_ref[...]
        do_f = do_ref[...]
        dd = do_f * o_ref[...].astype(F32)
        do_b = do_f.astype(BF16)
        lse_t = lse_ref[0]
        dq_acc = jnp.zeros((l, LANES), F32)
        dk_acc = jnp.zeros((l, LANES), F32)
        dv_acc = jnp.zeros((l, LANES), F32)
        for h in range(HEADS_PER_BLOCK):
            qh = jnp.where(masks[h], qt, jnp.zeros_like(qt))
            doh = jnp.where(masks[h], do_b, jnp.zeros_like(do_b))
            delta = jnp.sum(jnp.where(masks[h], dd, 0.0), axis=-1, keepdims=True)
            p = jnp.exp(_dot(qh, kb, NT) * ATT_SCALE - lse_t[:, h * HEAD_DIM:h * HEAD_DIM + 1])
            ds = (p * (_dot(doh, vb, NT) - delta)).astype(BF16)
            dq_acc = jnp.where(masks[h], _dot(ds, kb) * ATT_SCALE, dq_acc)
            dk_acc = dk_acc + _dot(ds, qh, TN)
            dv_acc = dv_acc + _dot(p.astype(BF16), doh, TN)
        dq_ref[...] = dq_acc
        dk_ref[...] = dk_in[...] + dk_acc * ATT_SCALE
        dv_ref[...] = dv_in[...] + dv_acc

    blk = pl.BlockSpec((l, LANES), lambda hb: (cb, hb))
    f32n = jax.ShapeDtypeStruct((n, NA_WIDTH), F32)
    outs, _ = carrier_call(
        core, name=name, grid=(N_HEAD_BLOCKS,),
        in_specs=[blk, pl.BlockSpec((l, LANES), lambda hb: (cb, K_COL + hb)),
                  pl.BlockSpec((l, LANES), lambda hb: (cb, V_COL + hb)), blk, blk,
                  pl.BlockSpec((1, l, LANES), lambda hb: (hb, 0, 0)), ANY, blk, blk],
        out_specs=[blk, blk, blk], out_shape=[f32n, f32n, f32n],
        inputs=[qkv, qkv, qkv, na, dmix, lse, dq, dk, dv], aliases={6: 0, 7: 1, 8: 2})
    return outs


def _pool_consts(l):
    assert l == TM
    mem = np.zeros((2, POOL_GROUPS, TM, TM), np.float32)
    inv = np.zeros((2, POOL_GROUPS, TM, LANES), np.float32)
    for which, length in ((0, GRID_W), (1, l)):
        t = np.arange(length)
        for g, w in enumerate(POOL_WINDOWS):
            lo = np.clip(t - w // 2, 0, length)
            hi = np.clip(t - w // 2 + w, 0, length)
            blockm = ((t[None, :] >= lo[:, None]) & (t[None, :] < hi[:, None])).astype(np.float32)
            cnt = (hi - lo).astype(np.float32)
            for b in range(TM // length):
                mem[which, g, b * length:(b + 1) * length, b * length:(b + 1) * length] = blockm
                inv[which, g, b * length:(b + 1) * length, :] = (1.0 / cnt)[:, None]
    return mem, np.ascontiguousarray(mem.transpose(0, 1, 3, 2)), inv


def _split_dot(m01, val):
    hi = val.astype(BF16)
    lo = (val - hi.astype(F32)).astype(BF16)
    return _dot(m01, hi) + _dot(m01, lo)


def pool_fwd(u, mem, inv, wp, scale, nx_tiles, name):
    n = u.shape[0]
    tp = _pick(n, (3 * TM, TM))
    nsb = tp // TM

    def core(u_ref, m_ref, i_ref, wp_ref, s_ref, o_ref):
        i = pl.program_id(0)
        for sb in range(nsb):
            kind = ((i * nsb + sb) >= nx_tiles).astype(jnp.int32)
            rows = slice(sb * TM, (sb + 1) * TM)
            for g in range(POOL_GROUPS):
                sl = slice(g * POOL_CH, (g + 1) * POOL_CH)
                ug = u_ref[rows, sl]
                dg = _split_dot(m_ref[kind, g], ug) * i_ref[kind, g] - ug
                o_ref[rows, sl] = (_dot(dg.astype(BF16), wp_ref[g]) * s_ref[:, sl]).astype(BF16)

    whole4 = lambda i: (0, 0, 0, 0)
    outs, _ = carrier_call(
        core, name=name, grid=(n // tp,),
        in_specs=[pl.BlockSpec((tp, POOL_WIDTH), lambda i: (i, 0)),
                  pl.BlockSpec((2, POOL_GROUPS, TM, TM), whole4),
                  pl.BlockSpec((2, POOL_GROUPS, TM, LANES), whole4),
                  pl.BlockSpec((POOL_GROUPS, POOL_CH, POOL_CH), lambda i: (0, 0, 0)),
                  pl.BlockSpec((1, POOL_WIDTH), lambda i: (0, 0))],
        out_specs=[pl.BlockSpec((tp, POOL_WIDTH), lambda i: (i, 0))],
        out_shape=[jax.ShapeDtypeStruct((n, POOL_WIDTH), BF16)], inputs=[u, mem, inv, wp, scale])
    return outs[0]


def pool_bwd(dmix, u, mem, mem_t, inv, wp, scale, nx_tiles, name):
    n = u.shape[0]
    tp = _pick(n, (3 * TM, TM))
    nsb = tp // TM

    def core(dy_ref, u_ref, m_ref, mt_ref, i_ref, wp_ref, s_ref, du_ref, dwp_ref, dsc_ref):
        @pl.when(pl.program_id(0) == 0)
        def _():
            dwp_ref[...] = jnp.zeros_like(dwp_ref)
            dsc_ref[...] = jnp.zeros_like(dsc_ref)

        i = pl.program_id(0)
        for sb in range(nsb):
            kind = ((i * nsb + sb) >= nx_tiles).astype(jnp.int32)
            rows = slice(sb * TM, (sb + 1) * TM)
            for g in range(POOL_GROUPS):
                sl = slice(g * POOL_CH, (g + 1) * POOL_CH)
                ug = u_ref[rows, sl]
                dy = dy_ref[rows, sl]
                dg = (_split_dot(m_ref[kind, g], ug) * i_ref[kind, g] - ug).astype(BF16)
                z = _dot(dg, wp_ref[g])
                dsc_ref[0:1, sl] += jnp.sum(dy * z, axis=0, keepdims=True)
                dz = (dy * s_ref[:, sl]).astype(BF16)
                dwp_ref[g] += _dot(dg, dz, TN)
                dd = _dot(dz, wp_ref[g], NT)
                du_ref[rows, sl] = _split_dot(mt_ref[kind, g], dd * i_ref[kind, g]) - dd

    whole4 = lambda i: (0, 0, 0, 0)
    outs, _ = carrier_call(
        core, name=name, grid=(n // tp,),
        in_specs=[pl.BlockSpec((tp, POOL_WIDTH), lambda i: (i, 1)),
                  pl.BlockSpec((tp, POOL_WIDTH), lambda i: (i, 0)),
                  pl.BlockSpec((2, POOL_GROUPS, TM, TM), whole4),
                  pl.BlockSpec((2, POOL_GROUPS, TM, TM), whole4),
                  pl.BlockSpec((2, POOL_GROUPS, TM, LANES), whole4),
                  pl.BlockSpec((POOL_GROUPS, POOL_CH, POOL_CH), lambda i: (0, 0, 0)),
                  pl.BlockSpec((1, POOL_WIDTH), lambda i: (0, 0))],
        out_specs=[pl.BlockSpec((tp, POOL_WIDTH), lambda i: (i, 0)),
                   pl.BlockSpec((POOL_GROUPS, POOL_CH, POOL_CH), lambda i: (0, 0, 0)),
                   pl.BlockSpec((8, POOL_WIDTH), lambda i: (0, 0))],
        out_shape=[jax.ShapeDtypeStruct((n, POOL_WIDTH), F32),
                   jax.ShapeDtypeStruct((POOL_GROUPS, POOL_CH, POOL_CH), F32),
                   jax.ShapeDtypeStruct((8, POOL_WIDTH), F32)],
        inputs=[dmix, u, mem, mem_t, inv, wp, scale])
    return outs


MOD_ROWS = 16


def mod_fwd(cvecs, w, b, name):
    _, d = cvecs.shape
    cl = w.shape[2]
    tc = _pick(cl, (384, 128))

    def core(c_ref, w_ref, b_ref, o_ref):
        a = _silu(c_ref[...]).astype(BF16)
        o_ref[0] = _dot(a, w_ref[0].astype(BF16)) + b_ref[0]

    outs, _ = carrier_call(
        core, name=name, grid=(DEPTH, cl // tc),
        in_specs=[pl.BlockSpec((MOD_ROWS, d), lambda li, j: (0, 0)),
                  pl.BlockSpec((1, d, tc), lambda li, j: (li, 0, j)),
                  pl.BlockSpec((1, 1, tc), lambda li, j: (li, 0, j))],
        out_specs=[pl.BlockSpec((1, MOD_ROWS, tc), lambda li, j: (li, 0, j))],
        out_shape=[jax.ShapeDtypeStruct((DEPTH, MOD_ROWS, cl), F32)], inputs=[cvecs, w, b])
    return outs[0]


def mod_bwd(cvecs, dm, w, name):
    _, d = cvecs.shape
    cl = w.shape[2]
    tc = _pick(cl, (384, 128))

    def core(c_ref, dm_ref, w_ref, dw_ref, da_ref):
        @pl.when((pl.program_id(0) == 0) & (pl.program_id(1) == 0))
        def _():
            da_ref[...] = jnp.zeros_like(da_ref)

        a = _silu(c_ref[...]).astype(BF16)
        dmb = dm_ref[0].astype(BF16)
        dw_ref[0] = _dot(a, dmb, TN)
        da_ref[...] += _dot(dmb, w_ref[0].astype(BF16), NT)

    outs, _ = carrier_call(
        core, name=name, grid=(DEPTH, cl // tc),
        in_specs=[pl.BlockSpec((MOD_ROWS, d), lambda li, j: (0, 0)),
                  pl.BlockSpec((1, MOD_ROWS, tc), lambda li, j: (li, 0, j)),
                  pl.BlockSpec((1, d, tc), lambda li, j: (li, 0, j))],
        out_specs=[pl.BlockSpec((1, d, tc), lambda li, j: (li, 0, j)),
                   pl.BlockSpec((MOD_ROWS, d), lambda li, j: (0, 0))],
        out_shape=[jax.ShapeDtypeStruct((DEPTH, d, cl), F32), jax.ShapeDtypeStruct((MOD_ROWS, d), F32)],
        inputs=[cvecs, dm, w])
    return outs


def loss_head(y, target, name):
    n, d = y.shape
    s = target.shape[0]
    nt, nx = n // TM, s // TM

    def core(y_ref, t_ref, l_ref, dy_ref, acc_ref):
        i = pl.program_id(0)

        @pl.when(i == 0)
        def _():
            acc_ref[...] = jnp.zeros_like(acc_ref)

        @pl.when(i < nx)
        def _():
            e = y_ref[...] - t_ref[...]
            dy_ref[...] = e * (1.0 / d)
            acc_ref[...] += jnp.sum(e * e, axis=0, keepdims=True)

        @pl.when(i >= nx)
        def _():
            dy_ref[...] = jnp.zeros_like(dy_ref)

        @pl.when(i == nt - 1)
        def _():
            l_ref[...] = jnp.sum(acc_ref[...], axis=1, keepdims=True) * (0.5 / d)

    tile = pl.BlockSpec((TM, d), lambda i: (i, 0))
    outs, _ = carrier_call(
        core, name=name, grid=(nt,),
        in_specs=[tile, pl.BlockSpec((TM, d), lambda i: (jnp.minimum(i, nx - 1), 0))],
        out_specs=[pl.BlockSpec((1, 1), lambda i: (0, 0)), tile],
        out_shape=[jax.ShapeDtypeStruct((1, 1), F32), jax.ShapeDtypeStruct((n, d), F32)],
        scratch_shapes=[pltpu.VMEM((1, d), F32)], inputs=[y, target])
    return outs


def sum_devices(v, name):
    _, r, c = v.shape
    tr = _pick(r, (64, 8))

    def core(v_ref, o_ref):
        acc = v_ref[0]
        for p in range(1, N_DEV):
            acc = acc + v_ref[p]
        o_ref[...] = acc

    outs, _ = carrier_call(
        core, name=name, grid=(r // tr,), in_specs=[pl.BlockSpec((N_DEV, tr, c), lambda i: (0, i, 0))],
        out_specs=[pl.BlockSpec((tr, c), lambda i: (i, 0))], out_shape=[jax.ShapeDtypeStruct((r, c), F32)],
        inputs=[v])
    return outs[0]


def cctx_grad(parts, c_ctx, name):
    d = c_ctx.shape[1]

    def body(p_ref, c_ref, o_ref):
        acc = p_ref[0]
        for p in range(1, N_DEV):
            acc = acc + p_ref[p]
        o_ref[...] = acc[8:9, :] * _dsilu(c_ref[...])

    return pl.pallas_call(
        body, name=name, out_shape=jax.ShapeDtypeStruct((1, d), F32),
        in_specs=[pl.BlockSpec(memory_space=pltpu.VMEM)] * 2,
        out_specs=pl.BlockSpec(memory_space=pltpu.VMEM),
    )(parts, c_ctx)


def _adam_math(w, g, m, v):
    m2 = ADAM_B1 * m + (1.0 - ADAM_B1) * g
    v2 = ADAM_B2 * v + (1.0 - ADAM_B2) * (g * g)
    m_hat = m2 / (1.0 - ADAM_B1 ** ADAM_STEP)
    v_hat = v2 / (1.0 - ADAM_B2 ** ADAM_STEP)
    delta = -ADAM_LR * (m_hat / (jnp.sqrt(v_hat) + ADAM_EPS) + ADAM_WD * w)
    return delta, m2, v2


def adamw(w, g, m, v, name):
    r, c = w.shape
    tr = _pick(r, (256, 128, 64, 32, 16, 8, r))

    def core(w_ref, g_ref, m_ref, v_ref, d_ref, m2_ref, v2_ref):
        d_ref[...], m2_ref[...], v2_ref[...] = _adam_math(w_ref[...], g_ref[...], m_ref[...], v_ref[...])

    tile = pl.BlockSpec((tr, c), lambda i: (i, 0))
    out = jax.ShapeDtypeStruct((r, c), F32)
    outs, _ = carrier_call(core, name=name, grid=(r // tr,), in_specs=[tile] * 4, out_specs=[tile] * 3,
                           out_shape=[out] * 3, inputs=[w, g, m, v])
    return outs


def reduce_adamw(recv, w, m, v, name):
    r, c = w.shape
    tr = _pick(r, (256, 128, 64, 8))

    def core(recv_ref, w_ref, m_ref, v_ref, g_ref, d_ref, m2_ref, v2_ref):
        acc = recv_ref[0].astype(F32)
        for p in range(1, N_DEV):
            acc = acc + recv_ref[p].astype(F32)
        g_ref[...] = acc
        d_ref[...], m2_ref[...], v2_ref[...] = _adam_math(w_ref[...], acc, m_ref[...], v_ref[...])

    tile = pl.BlockSpec((tr, c), lambda i: (i, 0))
    out = jax.ShapeDtypeStruct((r, c), F32)
    outs, _ = carrier_call(
        core, name=name, grid=(r // tr,),
        in_specs=[pl.BlockSpec((N_DEV, tr, c), lambda i: (0, i, 0)), tile, tile, tile],
        out_specs=[tile] * 4, out_shape=[out] * 4, inputs=[recv, w, m, v])
    return outs


def kernel(x, c, ctx, c_ctx, w_mod, b_mod, norm_g, w_ffn_gate_up, w_ffn_down, w_in, w_out, na_rpb, w_pool, pool_scale, loss_target, m_c_ctx, m_w_mod, m_b_mod, m_norm_g, m_w_ffn_gate_up, m_w_ffn_down, m_w_in, m_w_out, m_na_rpb, m_w_pool, m_pool_scale, v_c_ctx, v_w_mod, v_b_mod, v_norm_g, v_w_ffn_gate_up, v_w_ffn_down, v_w_in, v_w_out, v_na_rpb, v_w_pool, v_pool_scale):
    s, d = x.shape[1], x.shape[2]
    l = ctx.shape[1]
    n = s + l
    nx = s // TM
    fq = w_ffn_gate_up.shape[-1]
    fr = w_ffn_down.shape[2]
    cl = w_mod.shape[2]
    dl = norm_g.shape[2]
    me = 4 * lax.axis_index("x") + 2 * lax.axis_index("y") + lax.axis_index("c")

    c_all = all_gather(c, "gather_c").reshape(N_DEV, d)
    cvecs = jnp.concatenate([c_all, c_ctx[None, :], jnp.zeros((MOD_ROWS - N_DEV - 1, d), F32)], axis=0)
    b_loc = lax.dynamic_slice(b_mod, (0, me * cl), (DEPTH, cl)).reshape(DEPTH, 1, cl)
    mod_loc = mod_fwd(cvecs, w_mod, b_loc, "mod_fwd")
    mod_all = all_gather(mod_loc.reshape(DEPTH * MOD_ROWS, cl), "gather_mod")
    mod_all = mod_all.reshape(N_DEV, DEPTH, MOD_ROWS, cl).transpose(1, 2, 0, 3).reshape(DEPTH, MOD_ROWS, N_DEV * cl)
    mine = lax.dynamic_slice(mod_all, (0, me, 0), (DEPTH, 1, N_DEV * cl))
    mods = jnp.concatenate([mine, mod_all[:, N_DEV:N_DEV + 1]], axis=1).reshape(DEPTH, 2, N_MOD, d)

    gu_b = w_ffn_gate_up.astype(BF16)
    dn_b = w_ffn_down.astype(BF16)
    wi_b = w_in.astype(BF16)
    wo_b = w_out.astype(BF16)
    wp_b = w_pool.astype(BF16)

    def ffn_shards(li, i):
        return [gu_b[li, i], dn_b[li, i]]

    def mix_shards(li):
        return [wi_b[li], wo_b[li]]

    def as_ffn_weights(gathered):
        return gathered[0].reshape(2, 4, d, fq), gathered[1].reshape(4, 2 * fr, d)

    def as_mix_weights(gathered):
        return gathered[0], gathered[1].reshape(N_DEV * wo_b.shape[1], d)

    tables = _rope_tables(s, n)
    mem_np, mem_t_np, inv_np = _pool_consts(l)
    mem, mem_t, inv = jnp.asarray(mem_np, BF16), jnp.asarray(mem_t_np, BF16), jnp.asarray(inv_np)
    first = gather_two_level([norm_g.reshape(DEPTH * 6, dl), gu_b[0, 0]], "gather_first")
    g_full = first[0].reshape(N_DEV, DEPTH, 6, dl).transpose(1, 2, 0, 3).reshape(DEPTH, 6, 1, N_DEV * dl)

    weights = {("ffn", 0, 0): (first[1].reshape(2, 4, d, fq), None)}
    saved = {}
    xcur = jnp.concatenate([x[0], ctx[0]], axis=0)
    for li in range(DEPTH):
        last = li == DEPTH - 1
        for i in range(2):
            tag = f"l{li}_ffn{i}"
            wgu, wd4 = weights[("ffn", li, i)]
            if i == 0:
                ex_up = Exchange(gathers=mix_shards(li) + ([dn_b[0, 0]] if wd4 is None else []))
                ex_dn = Exchange(gathers=[dn_b[li, 1]])
            elif not last:
                ex_up, ex_dn = Exchange(gathers=[gu_b[li + 1, 0]]), Exchange(gathers=[dn_b[li + 1, 0]])
            else:
                ex_up = ex_dn = None
            (hb, gu, a4), got_up = ffn_up(xcur, g_full[li, 4 * i], mods[li], wgu, s, 6 * i, tag + "_up", ex_up)
            if wd4 is None:
                wd4 = got_up.pop().reshape(4, 2 * fr, d)
                weights[("ffn", li, i)] = (wgu, wd4)
            (ff, xnext), got_dn = ffn_down(a4, wd4, xcur, g_full[li, 4 * i + 1], mods[li], s, 6 * i + 2, tag + "_down", ex_dn)
            saved[("ffn", li, i)] = (xcur, hb, gu, a4, ff)
            xcur = xnext
            if i == 0:
                weights[("mix", li)] = as_mix_weights(got_up)
                next_dn = got_dn
            elif not last:
                weights[("ffn", li + 1, 0)] = as_ffn_weights(got_up + got_dn)
            if i == 0:
                tag = f"l{li}_mix"
                win8, wout = weights[("mix", li)]
                (hb, qkv, u), _ = mix_in(xcur, g_full[li, 2], mods[li], win8, tables, s, tag + "_in")
                tb = bias_tables(na_rpb[li], tag + "_bias")
                (na, lse), got = na_fwd(qkv, tb, s, tag + "_na", Exchange(gathers=[gu_b[li, 1]]))
                weights[("ffn", li, 1)] = as_ffn_weights(got + next_dn)
                lse_c = None
                if not last:
                    na, lse_c = ctx_attn_fwd(qkv, na, s, tag + "_ctx_attn")
                py = pool_fwd(u, mem, inv, wp_b[li], pool_scale[li][None, :], nx, tag + "_pool")
                fm, xnext = mix_out(na, py, wout, xcur, g_full[li, 3], mods[li], s, tag + "_out")
                saved[("mix", li)] = (xcur, hb, qkv, u, tb, na, lse, lse_c, py, fm)
                xcur = xnext

    loss_local, dcur = loss_head(xcur, loss_target[0], "loss")
    loss = lax.psum(loss_local[0, 0], ("x", "y", "c"))

    recv = {"gu": lax.empty((N_DEV, 2 * DEPTH, d, fq), BF16), "dn": lax.empty((N_DEV, 2 * DEPTH, fr, d), BF16),
            "wi": lax.empty((N_DEV, DEPTH, d, IN_BLOCK), BF16), "wo": lax.empty((N_DEV, DEPTH, wo_b.shape[1], d), BF16)}
    pending = []

    def take(keys, gathers=()):
        nonlocal pending
        jobs = [(gr, recv[key], st) for key, gr, st in pending if key in keys]
        order = [key for key, _, _ in pending if key in keys]
        pending = [p for p in pending if p[0] not in keys]
        return Exchange(gathers=gathers, a2as=jobs), order

    def pad8(t):
        t = t.reshape(-1, d) if t.size % d == 0 else jnp.pad(t.reshape(-1), (0, -t.size % d)).reshape(-1, d)
        return jnp.pad(t, ((0, -t.shape[0] % 8), (0, 0)))

    def packed(parts):
        parts = [pad8(p) for p in parts]
        offs = np.cumsum([0] + [p.shape[0] for p in parts])
        return jnp.concatenate(parts + [jnp.zeros((-offs[-1] % 64, d), F32)], axis=0), offs

    def put(order, bufs):
        for key, buf in zip(order, bufs):
            recv[key] = buf

    d_rpb, d_wp, d_ps, d_mod, d_g = [], [], [], [], []
    for li in reversed(range(DEPTH)):
        reds = {}
        for i in (1, 0):
            tag = f"l{li}_ffn{i}"
            xin, hb, gu, a4, ff = saved[("ffn", li, i)]
            wgu, wd4 = weights[("ffn", li, i)]
            dff, red1 = post_bwd(ff, dcur, g_full[li, 4 * i + 1], mods[li], 6 * i + 2, 0.5, s, tag + "_post_bwd")
            early = []
            if (li, i) == (0, 0):
                early_small, early_offs = packed([jnp.stack(d_wp), jnp.stack(d_ps), jnp.stack(d_rpb)])
                early = [early_small]
            ex, _ = take((), early)
            g_dn, bufs = grad_weight(a4, dff, tag + "_dwdown", a_lead=4, per_step=4, exchange=ex)
            if early:
                early_sum = sum_devices(bufs[0], "sum_early_small_grads")
            pending += [("dn", g_dn.reshape(N_DEV, fr, d), 2 * li + i)]
            ex, order = take(("dn",))
            dgu, bufs = ffn_da(dff, wd4, gu, tag + "_da", ex)
            put(order, bufs)
            ex, order = take(("wi", "wo"))
            g_gu, bufs = grad_weight(hb, dgu.reshape(N_DEV, n, fq), tag + "_dwgu", b_lead=N_DEV, per_step=4, exchange=ex)
            put(order, bufs)
            pending += [("gu", g_gu, 2 * li + i)]
            ex, order = take(("gu",))
            (dcur, red2), bufs = ffn_dh(dgu, wgu, xin, dcur, g_full[li, 4 * i], mods[li], s, 6 * i, tag + "_dh", ex)
            put(order, bufs)
            reds[i] = (red1, red2)
            if i == 1:
                tag = f"l{li}_mix"
                xin, hb, qkv, u, tb, na, lse, lse_c, py, fm = saved[("mix", li)]
                win8, wout = weights[("mix", li)]
                dfm, redm1 = post_bwd(fm, dcur, g_full[li, 3], mods[li], 5, 1.0, s, tag + "_post_bwd")
                dmix = matmul_nt(dfm, wout, tag + "_dmix")
                g_wo = grad_wout(na, py, dfm, tag + "_dwout").reshape(N_DEV, wo_b.shape[1], d)
                du, gwp, gps = pool_bwd(dmix, u, mem, mem_t, inv, wp_b[li], pool_scale[li][None, :], nx, tag + "_pool_bwd")
                ex, order = take(("gu", "dn"))
                (dq, dk, dv, dtb), bufs = na_bwd(qkv, tb, na, dmix, lse, s, tag + "_na_bwd", ex)
                put(order, bufs)
                if li != DEPTH - 1:
                    dq, dk, dv = ctx_attn_bwd(qkv, na, dmix, lse_c, dq, dk, dv, s, tag + "_ctx_attn_bwd")
                grpb = bias_tables_bwd(dtb, tag + "_bias_bwd")
                dqkvu = qkv_bwd(dq, dk, dv, du, tables, tag + "_rope_bwd")
                (dcur, redm2), _ = mix_dh(dqkvu, win8, xin, dcur, g_full[li, 2], mods[li], s, tag + "_dh")
                g_wi, _ = grad_weight(hb, dqkvu, tag + "_dwin", b_cols=IN_BLOCK, per_step=4)
                pending += [("wi", g_wi, li), ("wo", g_wo, li)]
                d_rpb.insert(0, grpb)
                d_wp.insert(0, gwp)
                d_ps.insert(0, gps[0])
        (ra1, ra2), (rb1, rb2) = reds[0], reds[1]
        d_mod.insert(0, jnp.stack([ra2[:, 0], ra2[:, 1], ra1[:, 0], redm2[:, 0], redm2[:, 1], redm1[:, 0],
                                   rb2[:, 0], rb2[:, 1], rb1[:, 0]], axis=1))
        d_g.insert(0, jnp.stack([t[0] + t[1] for t in (ra2[:, 2], ra1[:, 1], redm2[:, 2], redm1[:, 1], rb2[:, 2], rb1[:, 1])]))
    grad_x = dcur[:s][None]

    small, offs = packed([jnp.stack(d_mod), jnp.stack(d_g)])
    ex, order = take(("gu", "dn", "wi", "wo"), [small])
    bufs = exchange_only(ex, "exchange_last")
    small_all = bufs[0]
    put(order, bufs[1:])
    small_sum = sum_devices(small_all, "sum_small_grads")

    n_mod_rows = DEPTH * 2 * N_MOD
    dmod_all = small_all[:, :n_mod_rows].reshape(N_DEV, DEPTH, 2, N_MOD * d)
    dmod_sum = small_sum[:n_mod_rows].reshape(DEPTH, 2, N_MOD * d)
    dm_rows = jnp.concatenate([dmod_all[:, :, 0].transpose(1, 0, 2), dmod_sum[:, 1:2],
                               jnp.zeros((DEPTH, MOD_ROWS - N_DEV - 1, N_MOD * d), F32)], axis=1)
    grad_b_mod = dmod_sum[:, 0] + dmod_sum[:, 1]
    dm_loc = lax.dynamic_slice(dm_rows, (0, 0, me * cl), (DEPTH, MOD_ROWS, cl))
    grad_w_mod, da_part = mod_bwd(cvecs, dm_loc, w_mod, "mod_bwd")
    da_all = all_gather(da_part, "gather_dcvec")
    grad_c_ctx = cctx_grad(da_all, c_ctx[None, :], "c_ctx_grad")[0]

    grad_norm_full = small_sum[offs[1]:offs[1] + DEPTH * 6].reshape(DEPTH, 6, d)
    grad_norm_g = lax.dynamic_slice(grad_norm_full, (0, 0, me * dl), (DEPTH, 6, dl))
    grad_w_pool = early_sum[early_offs[0]:early_offs[0] + w_pool.size // d].reshape(w_pool.shape)
    grad_pool_scale = early_sum[early_offs[1]:early_offs[1] + pool_scale.size // d].reshape(pool_scale.shape)
    grad_na_rpb = early_sum[early_offs[2]:early_offs[3]].reshape(-1)[:na_rpb.size].reshape(na_rpb.shape)

    def big_adam(key, w, m, v, name):
        shp = w.shape
        cols = shp[-1]
        outs = reduce_adamw(recv[key].reshape(N_DEV, -1, cols), w.reshape(-1, cols), m.reshape(-1, cols),
                            v.reshape(-1, cols), name)
        return tuple(t.reshape(shp) for t in outs)

    def small_adam(w, g, m, v, name):
        shp = w.shape
        cols = shp[-1]
        outs = adamw(w.reshape(-1, cols), g.reshape(-1, cols), m.reshape(-1, cols), v.reshape(-1, cols), name)
        return tuple(t.reshape(shp) for t in outs)

    b_gu = big_adam("gu", w_ffn_gate_up, m_w_ffn_gate_up, v_w_ffn_gate_up, "adam_gate_up")
    b_dn = big_adam("dn", w_ffn_down, m_w_ffn_down, v_w_ffn_down, "adam_down")
    b_wi = big_adam("wi", w_in, m_w_in, v_w_in, "adam_w_in")
    b_wo = big_adam("wo", w_out, m_w_out, v_w_out, "adam_w_out")
    a_cc = small_adam(c_ctx, grad_c_ctx, m_c_ctx, v_c_ctx, "adam_c_ctx")
    a_wm = small_adam(w_mod, grad_w_mod, m_w_mod, v_w_mod, "adam_w_mod")
    a_bm = small_adam(b_mod, grad_b_mod, m_b_mod, v_b_mod, "adam_b_mod")
    a_ng = small_adam(norm_g, grad_norm_g, m_norm_g, v_norm_g, "adam_norm_g")
    a_rp = small_adam(na_rpb, grad_na_rpb, m_na_rpb, v_na_rpb, "adam_na_rpb")
    a_wp = small_adam(w_pool, grad_w_pool, m_w_pool, v_w_pool, "adam_w_pool")
    a_ps = small_adam(pool_scale, grad_pool_scale, m_pool_scale, v_pool_scale, "adam_pool_scale")

    grads = (grad_c_ctx, grad_w_mod, grad_b_mod, grad_norm_g, b_gu[0], b_dn[0], b_wi[0], b_wo[0], grad_na_rpb, grad_w_pool, grad_pool_scale)
    deltas = (a_cc[0], a_wm[0], a_bm[0], a_ng[0], b_gu[1], b_dn[1], b_wi[1], b_wo[1], a_rp[0], a_wp[0], a_ps[0])
    new_m = (a_cc[1], a_wm[1], a_bm[1], a_ng[1], b_gu[2], b_dn[2], b_wi[2], b_wo[2], a_rp[1], a_wp[1], a_ps[1])
    new_v = (a_cc[2], a_wm[2], a_bm[2], a_ng[2], b_gu[3], b_dn[3], b_wi[3], b_wo[3], a_rp[2], a_wp[2], a_ps[2])
    return (loss, grad_x, *grads, *deltas, *new_m, *new_v)
```

```python
import functools
import math

import numpy as np
import jax
import jax.numpy as jnp
from jax import lax
from jax.experimental import pallas as pl
from jax.experimental.pallas import tpu as pltpu

F32 = jnp.float32
BF16 = jnp.bfloat16

N_DEV = 8
DEPTH = 2
GRID_W = 64
N_MOD = 9
NA_HEADS = 8
HEAD_DIM = 64
NA_WIDTH = NA_HEADS * HEAD_DIM
NA_KH = 8
NA_KW = 16
POOL_GROUPS = 4
POOL_CH = 128
POOL_WIDTH = POOL_GROUPS * POOL_CH
POOL_WINDOWS = (2, 4, 8, 16)
ROPE_THETA = 10000.0
ROPE_PAIRS = HEAD_DIM // 4
RMS_EPS = 1e-6
NEG_INF = -1e30
ATT_SCALE = HEAD_DIM ** -0.5

ADAM_LR = 0.001
ADAM_B1 = 0.9
ADAM_B2 = 0.999
ADAM_EPS = 1e-08
ADAM_WD = 0.01
ADAM_STEP = 10

TM = 256
LANES = 128
HEADS_PER_BLOCK = LANES // HEAD_DIM
N_HEAD_BLOCKS = NA_WIDTH // LANES
IN_BLOCK = 2 * LANES
N_QKV_BLOCKS = 3 * NA_WIDTH // IN_BLOCK
VMEM_LIMIT = 56 * 1024 * 1024
HIGHEST = lax.Precision.HIGHEST
MESH = pl.DeviceIdType.MESH
ANY = pl.BlockSpec(memory_space=pl.ANY)

NN = (((1,), (0,)), ((), ()))
NT = (((1,), (1,)), ((), ()))
TN = (((0,), (0,)), ((), ()))


def _pick(n, cands):
    for t in cands:
        if n % t == 0:
            return t
    raise ValueError(f"no tile for {n} among {cands}")


def _dot(a, b, dn=NN, precision=None):
    return lax.dot_general(a, b, dn, preferred_element_type=F32, precision=precision)


def _silu(x):
    return x * jax.nn.sigmoid(x)


def _dsilu(x):
    s = jax.nn.sigmoid(x)
    return s * (1.0 + x * (1.0 - s))


def _peer(mask):
    x, y, c = lax.axis_index("x"), lax.axis_index("y"), lax.axis_index("c")
    px = 1 - x if mask & 4 else x
    py = 1 - y if mask & 2 else y
    pc = 1 - c if mask & 1 else c
    return (px, py, pc), 4 * px + 2 * py + pc


class Exchange:
    def __init__(self, gathers=(), a2as=()):
        self.gathers = list(gathers)
        self.a2as = list(a2as)
        self.n_jobs = len(self.gathers) + len(self.a2as)

    def inputs(self):
        out = list(self.gathers)
        for v, buf, _ in self.a2as:
            out += [v, buf]
        return out

    def out_shapes(self):
        shapes = [jax.ShapeDtypeStruct((N_DEV,) + v.shape, v.dtype) for v in self.gathers]
        shapes += [jax.ShapeDtypeStruct(buf.shape, buf.dtype) for _, buf, _ in self.a2as]
        return shapes

    def aliases(self, n_in, n_out):
        ng = len(self.gathers)
        return {n_in + ng + 2 * k + 1: n_out + ng + k for k in range(len(self.a2as))}

    def scratch(self):
        per = N_DEV - 1
        return [pltpu.SemaphoreType.DMA((per * self.n_jobs,)), pltpu.SemaphoreType.DMA((per * self.n_jobs,)),
                pltpu.SemaphoreType.DMA((self.n_jobs,))]

    def _copies(self, in_refs, out_refs, sems, with_recvs):
        send_sems, recv_sems, local_sems = sems
        _, me = _peer(0)
        ng = len(self.gathers)
        local, sends, recvs = [], [], []
        for job in range(self.n_jobs):
            if job < ng:
                src_of = lambda pid, r=in_refs[job]: r
                dst_of = lambda pid, r=out_refs[job]: r.at[pid]
            else:
                k = job - ng
                stage = self.a2as[k][2]
                src_of = lambda pid, r=in_refs[ng + 2 * k]: r.at[pid]
                dst_of = lambda pid, r=out_refs[job], st=stage: r.at[pid, st]
            local.append(pltpu.make_async_copy(src_of(me), dst_of(me), local_sems.at[job]))
            for mask in range(1, N_DEV):
                peer, pid = _peer(mask)
                idx = job * (N_DEV - 1) + mask - 1
                sends.append(pltpu.make_async_remote_copy(
                    src_ref=src_of(pid), dst_ref=dst_of(me), send_sem=send_sems.at[idx],
                    recv_sem=recv_sems.at[idx], device_id=peer, device_id_type=MESH))
                if with_recvs:
                    recvs.append(pltpu.make_async_remote_copy(
                        src_ref=src_of(pid), dst_ref=dst_of(pid), send_sem=send_sems.at[idx],
                        recv_sem=recv_sems.at[idx], device_id=peer, device_id_type=MESH))
        return local, sends, recvs

    def start(self, in_refs, out_refs, sems):
        local, sends, _ = self._copies(in_refs, out_refs, sems, False)
        for cp in local + sends:
            cp.start()

    def wait(self, in_refs, out_refs, sems):
        local, sends, recvs = self._copies(in_refs, out_refs, sems, True)
        for cp in recvs:
            cp.wait_recv()
        for cp in sends:
            cp.wait_send()
        for cp in local:
            cp.wait()


def carrier_call(core, *, name, grid, in_specs, out_specs, out_shape, inputs, scratch_shapes=(), aliases=None,
                 exchange=None):
    aliases = dict(aliases or {})
    n_in, n_out, n_sc = len(in_specs), len(out_specs), len(scratch_shapes)
    sem = ("arbitrary",) * len(grid)
    params = pltpu.CompilerParams(dimension_semantics=sem, vmem_limit_bytes=VMEM_LIMIT)
    if exchange is None or exchange.n_jobs == 0:
        outs = pl.pallas_call(core, name=name, grid=grid, in_specs=list(in_specs), out_specs=tuple(out_specs),
                              out_shape=tuple(out_shape), scratch_shapes=list(scratch_shapes),
                              input_output_aliases=aliases, compiler_params=params)(*inputs)
        return list(outs), []
    x_in = exchange.inputs()
    x_out = exchange.out_shapes()
    aliases.update(exchange.aliases(n_in, n_out))

    def body(*refs):
        a = n_in + len(x_in)
        b = a + n_out + len(x_out)
        core_in, job_in = refs[:n_in], refs[n_in:a]
        core_out, job_out = refs[a:a + n_out], refs[a + n_out:b]
        core_sc, job_sc = refs[b:b + n_sc], refs[b + n_sc:]
        first = functools.reduce(lambda p, q: p & q, [pl.program_id(ax) == 0 for ax in range(len(grid))])
        last = functools.reduce(lambda p, q: p & q, [pl.program_id(ax) == g - 1 for ax, g in enumerate(grid)])

        @pl.when(first)
        def _():
            exchange.start(job_in, job_out, job_sc)

        core(*core_in, *core_out, *core_sc)

        @pl.when(last)
        def _():
            exchange.wait(job_in, job_out, job_sc)

    outs = pl.pallas_call(
        body, name=name, grid=grid, in_specs=list(in_specs) + [ANY] * len(x_in),
        out_specs=tuple(out_specs) + (ANY,) * len(x_out), out_shape=tuple(out_shape) + tuple(x_out),
        scratch_shapes=list(scratch_shapes) + exchange.scratch(), input_output_aliases=aliases,
        compiler_params=params)(*inputs, *x_in)
    return list(outs[:n_out]), list(outs[n_out:])


def exchange_only(exchange, name):
    def body(*refs):
        n_in, n_out = len(exchange.inputs()), len(exchange.out_shapes())
        job_in, job_out, sems = refs[:n_in], refs[n_in:n_in + n_out], refs[n_in + n_out:]
        exchange.start(job_in, job_out, sems)
        exchange.wait(job_in, job_out, sems)

    x_in = exchange.inputs()
    outs = pl.pallas_call(
        body, name=name, in_specs=[ANY] * len(x_in), out_specs=(ANY,) * len(exchange.out_shapes()),
        out_shape=tuple(exchange.out_shapes()), scratch_shapes=exchange.scratch(),
        input_output_aliases=exchange.aliases(0, 0))(*x_in)
    return list(outs)


def all_gather(v, name):
    return exchange_only(Exchange(gathers=[v]), name)[0]


def gather_two_level(vs, name):
    nv = len(vs)
    per = N_DEV - 1

    def body(*refs):
        v_refs, o_refs = refs[:nv], refs[nv:2 * nv]
        send_sems, recv_sems, local_sems = refs[2 * nv:]
        x, y, c = lax.axis_index("x"), lax.axis_index("y"), lax.axis_index("c")
        me, sibling = (x, y, c), (x, y, 1 - c)
        chips = [(1 - x, y), (x, 1 - y), (1 - x, 1 - y)]

        def copy(a, k, block, to, src=None):
            dst = o_refs[a].at[4 * block[0] + 2 * block[1] + block[2]]
            return pltpu.make_async_remote_copy(
                src_ref=dst if src is None else src, dst_ref=dst, send_sem=send_sems.at[a * per + k],
                recv_sem=recv_sems.at[a * per + k], device_id=to, device_id_type=MESH)

        mine = [pltpu.make_async_copy(v_refs[a], o_refs[a].at[4 * x + 2 * y + c], local_sems.at[a]) for a in range(nv)]
        first = []
        for a in range(nv):
            first.append(copy(a, 0, me, sibling, src=v_refs[a]))
            first += [copy(a, 1 + j, me, (*chip, c), src=v_refs[a]) for j, chip in enumerate(chips)]
        for cp in mine + first:
            cp.start()
        passed = []
        for a in range(nv):
            for j, chip in enumerate(chips):
                copy(a, 1 + j, (*chip, c), me).wait_recv()
                passed.append(copy(a, 4 + j, (*chip, c), sibling))
                passed[-1].start()
        for a in range(nv):
            copy(a, 0, sibling, me).wait_recv()
            for j, chip in enumerate(chips):
                copy(a, 4 + j, (*chip, 1 - c), me).wait_recv()
        for cp in first + passed:
            cp.wait_send()
        for cp in mine:
            cp.wait()

    outs = pl.pallas_call(
        body, name=name, in_specs=[ANY] * nv, out_specs=(ANY,) * nv,
        out_shape=tuple(jax.ShapeDtypeStruct((N_DEV,) + v.shape, v.dtype) for v in vs),
        scratch_shapes=[pltpu.SemaphoreType.DMA((per * nv,)), pltpu.SemaphoreType.DMA((per * nv,)),
                        pltpu.SemaphoreType.DMA((nv,))])(*vs)
    return list(outs)


def _rms(xf):
    return lax.rsqrt(jnp.mean(xf * xf, axis=-1, keepdims=True) + RMS_EPS)


def _is_ctx(i, tm, s):
    return (i * tm + lax.broadcasted_iota(jnp.int32, (tm, 1), 0)) >= s


def _by_tile_kind(i, tm, s, fn):
    n_latent = s // tm

    @pl.when(i < n_latent)
    def _():
        fn(None)

    @pl.when(i >= n_latent)
    def _():
        fn(_is_ctx(i, tm, s))


def _mod_rows(mod_ref, k, is_ctx):
    if is_ctx is None:
        return mod_ref[0, k:k + 1, :]
    return jnp.where(is_ctx, mod_ref[1, k:k + 1, :], mod_ref[0, k:k + 1, :])


def _norm_mod(xf, g, mod_ref, k_shift, k_scale, is_ctx):
    nrm = xf * _rms(xf) * g
    return (nrm * (1.0 + _mod_rows(mod_ref, k_scale, is_ctx)) + _mod_rows(mod_ref, k_shift, is_ctx)).astype(BF16)


def _post(xf, ff, g, mod_ref, k_gate, coef, is_ctx):
    return xf + coef * _mod_rows(mod_ref, k_gate, is_ctx) * (ff * _rms(ff) * g)


def _red_add(red_ref, first, is_ctx, rows):
    @pl.when(first)
    def _():
        red_ref[...] = jnp.zeros_like(red_ref)

    for r, val in enumerate(rows):
        tot = jnp.sum(val, axis=0, keepdims=True)
        if is_ctx is None:
            red_ref[0, r:r + 1, :] += tot
        else:
            ctx = jnp.sum(jnp.where(is_ctx, val, 0.0), axis=0, keepdims=True)
            red_ref[0, r:r + 1, :] += tot - ctx
            red_ref[1, r:r + 1, :] += ctx


def _pre_bwd(xf, dh, dxo, g, mod_ref, k_scale, is_ctx):
    r = _rms(xf)
    xhat = xf * r
    dn = dh * (1.0 + _mod_rows(mod_ref, k_scale, is_ctx))
    dxhat = dn * g
    dx = dxo + r * (dxhat - xhat * jnp.mean(dxhat * xhat, axis=-1, keepdims=True))
    return dx, [dh, dh * xhat]


def _finish_pre(red_ref, g, mod_ref, k_scale):
    sums = red_ref[:, 1:2, :]
    red_ref[:, 2:3, :] = (1.0 + mod_ref[:, k_scale:k_scale + 1, :]) * sums
    red_ref[:, 1:2, :] = g * sums


def _vec_spec(d):
    return pl.BlockSpec((1, d), lambda *_: (0, 0))


def _mod_whole(d):
    return pl.BlockSpec((2, N_MOD, d), lambda *_: (0, 0, 0))


def _red_whole(d):
    return pl.BlockSpec((2, 8, d), lambda *_: (0, 0, 0))


def _token_tile(n):
    return _pick(n, (768, 640, 512, 384, 256))


def _elementwise_tile(n):
    return _pick(n, (1056, 768, 640, 512, 384, 256))


def _ffn_tile(n):
    return _pick(n, (528, 384, 640, 256))


def _resident(shape):
    zeros = (0,) * len(shape)
    return pl.BlockSpec(shape, lambda i: zeros, pipeline_mode=pl.Buffered(1))


def ffn_up(x, g, mod, wgu, s, k0, name, exchange=None):
    n, d = x.shape
    nk, fq = wgu.shape[1], wgu.shape[-1]
    tm = _ffn_tile(n)

    def core(x_ref, g_ref, mod_ref, w_ref, hb_ref, gu_ref, a_ref):
        i = pl.program_id(0)

        def prologue(is_ctx):
            hb_ref[...] = _norm_mod(x_ref[...], g_ref[...], mod_ref, k0, k0 + 1, is_ctx)

        _by_tile_kind(i, tm, s, prologue)
        h = hb_ref[...]
        for k in range(nk):
            gg = _dot(h, w_ref[0, k])
            uu = _dot(h, w_ref[1, k])
            gu_ref[0, k] = gg.astype(BF16)
            gu_ref[1, k] = uu.astype(BF16)
            a_ref[k] = (_silu(gg) * uu).astype(BF16)

    outs, xo = carrier_call(
        core, name=name, grid=(n // tm,),
        in_specs=[pl.BlockSpec((tm, d), lambda i: (i, 0)), _vec_spec(d), _mod_whole(d), _resident(wgu.shape)],
        out_specs=[pl.BlockSpec((tm, d), lambda i: (i, 0)),
                   pl.BlockSpec((2, nk, tm, fq), lambda i: (0, 0, i, 0)),
                   pl.BlockSpec((nk, tm, fq), lambda i: (0, i, 0))],
        out_shape=[jax.ShapeDtypeStruct((n, d), BF16), jax.ShapeDtypeStruct((2, nk, n, fq), BF16),
                   jax.ShapeDtypeStruct((nk, n, fq), BF16)],
        inputs=[x, g, mod, wgu], exchange=exchange)
    return outs, xo


def ffn_down(a4, wd4, x, g, mod, s, k_gate, name, exchange=None):
    n, d = x.shape
    nk, fq = wd4.shape[0], wd4.shape[1]
    tm = _ffn_tile(n)

    def core(a_ref, w_ref, x_ref, g_ref, mod_ref, f_ref, xo_ref):
        i = pl.program_id(0)
        ff = _dot(a_ref[0], w_ref[0])
        for k in range(1, nk):
            ff = ff + _dot(a_ref[k], w_ref[k])
        f_ref[...] = ff

        def epilogue(is_ctx):
            xo_ref[...] = _post(x_ref[...], f_ref[...], g_ref[...], mod_ref, k_gate, 0.5, is_ctx)

        _by_tile_kind(i, tm, s, epilogue)

    tile = pl.BlockSpec((tm, d), lambda i: (i, 0))
    outs, xo = carrier_call(
        core, name=name, grid=(n // tm,),
        in_specs=[pl.BlockSpec((nk, tm, fq), lambda i: (0, i, 0)), _resident(wd4.shape), tile, _vec_spec(d),
                  _mod_whole(d)],
        out_specs=[tile, tile],
        out_shape=[jax.ShapeDtypeStruct((n, d), F32), jax.ShapeDtypeStruct((n, d), F32)],
        inputs=[a4, wd4, x, g, mod], exchange=exchange)
    return outs, xo


def ffn_da(df, wd4, gu, name, exchange=None):
    n, d = df.shape
    nk, fq = wd4.shape[0], wd4.shape[1]
    tm = _ffn_tile(n)

    def core(df_ref, w_ref, gu_ref, o_ref):
        dfv = df_ref[...]
        for k in range(nk):
            da = _dot(dfv, w_ref[k], NT).astype(BF16)
            gg = gu_ref[0, k]
            uu = gu_ref[1, k]
            sg = jax.nn.sigmoid(gg.astype(F32)).astype(BF16)
            o_ref[0, k] = da * (uu * (sg * (1 + gg * (1 - sg))))
            o_ref[1, k] = da * (gg * sg)

    gu_spec = pl.BlockSpec((2, nk, tm, fq), lambda i: (0, 0, i, 0))
    outs, xo = carrier_call(
        core, name=name, grid=(n // tm,),
        in_specs=[pl.BlockSpec((tm, d), lambda i: (i, 0)), _resident(wd4.shape), gu_spec],
        out_specs=[gu_spec], out_shape=[jax.ShapeDtypeStruct(gu.shape, BF16)],
        inputs=[df, wd4, gu], exchange=exchange)
    return outs[0], xo


def ffn_dh(dgu, wgu, x, dxo, g, mod, s, k0, name, exchange=None):
    n, d = x.shape
    nk, fq = wgu.shape[1], wgu.shape[-1]
    tm = _ffn_tile(n)

    def core(dgu_ref, w_ref, x_ref, dxo_ref, g_ref, mod_ref, dx_ref, red_ref, dh_s):
        i = pl.program_id(0)
        dh = _dot(dgu_ref[0, 0], w_ref[0, 0], NT) + _dot(dgu_ref[1, 0], w_ref[1, 0], NT)
        for k in range(1, nk):
            dh = dh + _dot(dgu_ref[0, k], w_ref[0, k], NT) + _dot(dgu_ref[1, k], w_ref[1, k], NT)
        dh_s[...] = dh

        def epilogue(is_ctx):
            dx, sums = _pre_bwd(x_ref[...], dh_s[...], dxo_ref[...], g_ref[...], mod_ref, k0 + 1, is_ctx)
            dx_ref[...] = dx
            _red_add(red_ref, i == 0, is_ctx, sums)

        _by_tile_kind(i, tm, s, epilogue)

        @pl.when(i == n // tm - 1)
        def _():
            _finish_pre(red_ref, g_ref[...], mod_ref, k0 + 1)

    tile = pl.BlockSpec((tm, d), lambda i: (i, 0))
    outs, xo = carrier_call(
        core, name=name, grid=(n // tm,),
        in_specs=[pl.BlockSpec((2, nk, tm, fq), lambda i: (0, 0, i, 0)), _resident(wgu.shape), tile, tile,
                  _vec_spec(d), _mod_whole(d)],
        out_specs=[tile, _red_whole(d)],
        out_shape=[jax.ShapeDtypeStruct((n, d), F32), jax.ShapeDtypeStruct((2, 8, d), F32)],
        scratch_shapes=[pltpu.VMEM((tm, d), F32)], inputs=[dgu, wgu, x, dxo, g, mod], exchange=exchange)
    return outs, xo


def grad_weight(a, b, name, a_lead=None, b_lead=None, b_cols=None, per_step=1, exchange=None):
    n = a.shape[-2]
    ka = a.shape[-1]
    kb = b_cols or b.shape[-1]
    nj = a_lead or b_lead or (b.shape[-1] // b_cols)
    tk = _pick(n, (1408, 1024, 768, 640, 512, 256))
    nk = n // tk
    ps = per_step
    assert nj % ps == 0

    def core(a_ref, b_ref, o_ref, acc):
        kk = pl.program_id(1)

        @pl.when(kk == 0)
        def _():
            acc[...] = jnp.zeros_like(acc)

        if b_cols:
            acc[...] += _dot(a_ref[...], b_ref[...], TN)
        else:
            for t in range(ps):
                acc[t] += _dot(a_ref[t] if a_lead else a_ref[...], b_ref[t] if b_lead else b_ref[...], TN)

        @pl.when(kk == nk - 1)
        def _():
            for t in range(ps):
                o_ref[t] = (acc[:, t * kb:(t + 1) * kb] if b_cols else acc[t]).astype(BF16)

    a_spec = (pl.BlockSpec((ps, tk, ka), lambda j, kk: (j, kk, 0)) if a_lead
              else pl.BlockSpec((tk, ka), lambda j, kk: (kk, 0)))
    if b_lead:
        b_spec = pl.BlockSpec((ps, tk, kb), lambda j, kk: (j, kk, 0))
    elif b_cols:
        b_spec = pl.BlockSpec((tk, ps * kb), lambda j, kk: (kk, j))
    else:
        b_spec = pl.BlockSpec((tk, kb), lambda j, kk: (kk, 0))
    outs, xo = carrier_call(
        core, name=name, grid=(nj // ps, nk), in_specs=[a_spec, b_spec],
        out_specs=[pl.BlockSpec((ps, ka, kb), lambda j, kk: (j, 0, 0))],
        out_shape=[jax.ShapeDtypeStruct((nj, ka, kb), BF16)],
        scratch_shapes=[pltpu.VMEM((ka, ps * kb) if b_cols else (ps, ka, kb), F32)], inputs=[a, b], exchange=exchange)
    return outs[0], xo


def post_bwd(f, dxo, g, mod, k_gate, coef, s, name):
    n, d = f.shape
    tm = _elementwise_tile(n)

    def core(f_ref, dxo_ref, g_ref, mod_ref, df_ref, red_ref):
        i = pl.program_id(0)

        def body(is_ctx):
            ff = f_ref[...]
            dxo_ = dxo_ref[...]
            r = _rms(ff)
            fn = ff * r
            cg = (coef * _mod_rows(mod_ref, k_gate, is_ctx)) * g_ref[...]
            w = dxo_ * fn
            dfn = dxo_ * cg
            df_ref[...] = (r * (dfn - fn * jnp.mean(w * cg, axis=-1, keepdims=True))).astype(BF16)
            _red_add(red_ref, i == 0, is_ctx, [w])

        _by_tile_kind(i, tm, s, body)

        @pl.when(i == n // tm - 1)
        def _():
            sums = red_ref[:, 0:1, :]
            red_ref[:, 1:2, :] = (coef * mod_ref[:, k_gate:k_gate + 1, :]) * sums
            red_ref[:, 0:1, :] = (coef * g_ref[...]) * sums

    tile = pl.BlockSpec((tm, d), lambda i: (i, 0))
    outs, _ = carrier_call(
        core, name=name, grid=(n // tm,), in_specs=[tile, tile, _vec_spec(d), _mod_whole(d)],
        out_specs=[tile, _red_whole(d)],
        out_shape=[jax.ShapeDtypeStruct((n, d), BF16), jax.ShapeDtypeStruct((2, 8, d), F32)],
        inputs=[f, dxo, g, mod])
    return outs


def matmul_nt(a, b, name):
    m, k = a.shape
    n = b.shape[0]
    tm = _token_tile(m)

    def core(a_ref, b_ref, o_ref):
        o_ref[...] = _dot(a_ref[...], b_ref[...], NT)

    outs, _ = carrier_call(
        core, name=name, grid=(m // tm,),
        in_specs=[pl.BlockSpec((tm, k), lambda i: (i, 0)), pl.BlockSpec((n, k), lambda i: (0, 0))],
        out_specs=[pl.BlockSpec((tm, n), lambda i: (i, 0))], out_shape=[jax.ShapeDtypeStruct((m, n), F32)],
        inputs=[a, b])
    return outs[0]


def _rope_tables(s, n):
    t = jnp.arange(n)
    lane = jnp.arange(LANES)
    dd = lane % HEAD_DIM
    inv = ROPE_THETA ** (-(dd % ROPE_PAIRS).astype(F32) / ROPE_PAIRS)
    pos = jnp.where(dd[None, :] < HEAD_DIM // 2, (t // GRID_W)[:, None], (t % GRID_W)[:, None]).astype(F32)
    ang = pos * inv[None, :]
    live = (t < s)[:, None]
    first = ((dd % (2 * ROPE_PAIRS)) < ROPE_PAIRS)[None, :]
    cos = jnp.where(live, jnp.cos(ang), 1.0)
    sin = jnp.where(live, jnp.sin(ang), 0.0)
    sa = jnp.where(first, -sin, 0.0)
    sb = jnp.where(first, 0.0, sin)
    return cos.astype(F32), sa.astype(F32), sb.astype(F32)


def _rope(xv, cos, sa, sb):
    return (xv * cos + pltpu.roll(xv, LANES - ROPE_PAIRS, 1) * sa + pltpu.roll(xv, ROPE_PAIRS, 1) * sb)


def mix_in(x, g, mod, win8, tables, s, name, exchange=None):
    n, d = x.shape
    tm = _token_tile(n)
    nb = win8.shape[0]
    n_rope = 2 * NA_WIDTH // IN_BLOCK

    def core(x_ref, g_ref, mod_ref, w_ref, c_ref, sa_ref, sb_ref, hb_ref, qkv_ref, u_ref):
        i = pl.program_id(0)
        hb = _norm_mod(x_ref[...], g_ref[...], mod_ref, 3, 4, _is_ctx(i, tm, s))
        hb_ref[...] = hb
        cos, sa, sb = c_ref[...], sa_ref[...], sb_ref[...]
        for j in range(nb):
            y = _dot(hb, w_ref[j])
            if j < n_rope:
                for b in range(IN_BLOCK // LANES):
                    sl = slice(b * LANES, (b + 1) * LANES)
                    qkv_ref[:, j * IN_BLOCK + b * LANES:j * IN_BLOCK + (b + 1) * LANES] = (
                        _rope(y[:, sl], cos, sa, sb).astype(BF16))
            elif j < N_QKV_BLOCKS:
                qkv_ref[:, j * IN_BLOCK:(j + 1) * IN_BLOCK] = y.astype(BF16)
            else:
                u_ref[:, (j - N_QKV_BLOCKS) * IN_BLOCK:(j - N_QKV_BLOCKS + 1) * IN_BLOCK] = y

    tab = pl.BlockSpec((tm, LANES), lambda i: (i, 0))
    row = lambda w: pl.BlockSpec((tm, w), lambda i: (i, 0))
    outs, xo = carrier_call(
        core, name=name, grid=(n // tm,),
        in_specs=[row(d), _vec_spec(d), _mod_whole(d), pl.BlockSpec((nb, d, IN_BLOCK), lambda i: (0, 0, 0)),
                  tab, tab, tab],
        out_specs=[row(d), row(3 * NA_WIDTH), row(POOL_WIDTH)],
        out_shape=[jax.ShapeDtypeStruct((n, d), BF16), jax.ShapeDtypeStruct((n, 3 * NA_WIDTH), BF16),
                   jax.ShapeDtypeStruct((n, POOL_WIDTH), F32)],
        inputs=[x, g, mod, win8, *tables], exchange=exchange)
    return outs, xo


def qkv_bwd(dq, dk, dv, du, tables, name):
    n = dq.shape[0]
    w = NA_WIDTH

    def core(dq_ref, dk_ref, dv_ref, du_ref, c_ref, sa_ref, sb_ref, o_ref):
        cos, sa, sb = c_ref[...], -sa_ref[...], -sb_ref[...]
        for b in range(N_HEAD_BLOCKS):
            sl = slice(b * LANES, (b + 1) * LANES)
            o_ref[:, b * LANES:(b + 1) * LANES] = _rope(dq_ref[:, sl], cos, sa, sb).astype(BF16)
            o_ref[:, w + b * LANES:w + (b + 1) * LANES] = _rope(dk_ref[:, sl], cos, sa, sb).astype(BF16)
        o_ref[:, 2 * w:3 * w] = dv_ref[...].astype(BF16)
        o_ref[:, 3 * w:] = du_ref[...].astype(BF16)

    tm = _elementwise_tile(n)
    tab = pl.BlockSpec((tm, LANES), lambda i: (i, 0))
    tile = pl.BlockSpec((tm, w), lambda i: (i, 0))
    outs, _ = carrier_call(
        core, name=name, grid=(n // tm,), in_specs=[tile, tile, tile, tile, tab, tab, tab],
        out_specs=[pl.BlockSpec((tm, 4 * w), lambda i: (i, 0))],
        out_shape=[jax.ShapeDtypeStruct((n, 4 * w), BF16)], inputs=[dq, dk, dv, du, *tables])
    return outs[0]


def mix_out(na, py, wout, x, g, mod, s, name):
    n, d = x.shape
    tm = _token_tile(n)
    half = na.shape[1]

    def core(na_ref, py_ref, w_ref, x_ref, g_ref, mod_ref, f_ref, xo_ref):
        i = pl.program_id(0)
        ff = _dot(na_ref[...], w_ref[:half, :]) + _dot(py_ref[...], w_ref[half:, :])
        f_ref[...] = ff
        xo_ref[...] = _post(x_ref[...], ff, g_ref[...], mod_ref, 5, 1.0, _is_ctx(i, tm, s))

    tile = pl.BlockSpec((tm, d), lambda i: (i, 0))
    htile = pl.BlockSpec((tm, half), lambda i: (i, 0))
    outs, _ = carrier_call(
        core, name=name, grid=(n // tm,),
        in_specs=[htile, htile, pl.BlockSpec((2 * half, d), lambda i: (0, 0)), tile, _vec_spec(d), _mod_whole(d)],
        out_specs=[tile, tile],
        out_shape=[jax.ShapeDtypeStruct((n, d), F32), jax.ShapeDtypeStruct((n, d), F32)],
        inputs=[na, py, wout, x, g, mod])
    return outs


def grad_wout(na, py, dfm, name):
    n, half = na.shape
    d = dfm.shape[1]
    tk = _pick(n, (1408, 1024, 768, 640, 512, 256))
    nk = n // tk

    def core(na_ref, py_ref, b_ref, o_ref, acc):
        hh, kk = pl.program_id(0), pl.program_id(1)

        @pl.when(kk == 0)
        def _():
            acc[...] = jnp.zeros_like(acc)

        @pl.when(hh == 0)
        def _():
            acc[...] += _dot(na_ref[...], b_ref[...], TN)

        @pl.when(hh == 1)
        def _():
            acc[...] += _dot(py_ref[...], b_ref[...], TN)

        @pl.when(kk == nk - 1)
        def _():
            o_ref[0] = acc[...].astype(BF16)

    htile = pl.BlockSpec((tk, half), lambda hh, kk: (kk, 0))
    outs, _ = carrier_call(
        core, name=name, grid=(2, nk), in_specs=[htile, htile, pl.BlockSpec((tk, d), lambda hh, kk: (kk, 0))],
        out_specs=[pl.BlockSpec((1, half, d), lambda hh, kk: (hh, 0, 0))],
        out_shape=[jax.ShapeDtypeStruct((2, half, d), BF16)],
        scratch_shapes=[pltpu.VMEM((half, d), F32)], inputs=[na, py, dfm])
    return outs[0]


def mix_dh(dqkvu, win8, x, dxo, g, mod, s, name, exchange=None):
    n, d = x.shape
    tm = _token_tile(n)
    nb = win8.shape[0]

    def core(dq_ref, w_ref, x_ref, dxo_ref, g_ref, mod_ref, dx_ref, red_ref):
        i = pl.program_id(0)
        dh = _dot(dq_ref[:, :IN_BLOCK], w_ref[0], NT)
        for j in range(1, nb):
            dh = dh + _dot(dq_ref[:, j * IN_BLOCK:(j + 1) * IN_BLOCK], w_ref[j], NT)
        is_ctx = _is_ctx(i, tm, s)
        dx, sums = _pre_bwd(x_ref[...], dh, dxo_ref[...], g_ref[...], mod_ref, 4, is_ctx)
        dx_ref[...] = dx
        _red_add(red_ref, i == 0, is_ctx, sums)

        @pl.when(i == n // tm - 1)
        def _():
            _finish_pre(red_ref, g_ref[...], mod_ref, 4)

    tile = pl.BlockSpec((tm, d), lambda i: (i, 0))
    outs, xo = carrier_call(
        core, name=name, grid=(n // tm,),
        in_specs=[pl.BlockSpec((tm, nb * IN_BLOCK), lambda i: (i, 0)),
                  pl.BlockSpec((nb, d, IN_BLOCK), lambda i: (0, 0, 0)), tile, tile, _vec_spec(d), _mod_whole(d)],
        out_specs=[tile, _red_whole(d)],
        out_shape=[jax.ShapeDtypeStruct((n, d), F32), jax.ShapeDtypeStruct((2, 8, d), F32)],
        inputs=[dqkvu, win8, x, dxo, g, mod], exchange=exchange)
    return outs, xo


def _na_consts():
    j = np.arange(GRID_W)
    col_start = np.clip(j - NA_KW // 2, 0, GRID_W - NA_KW)
    valid = (j[None, :] >= col_start[:, None]) & (j[None, :] < col_start[:, None] + NA_KW)
    dc = np.clip(j[None, :] - j[:, None] + NA_KW - 1, 0, 2 * NA_KW - 2)
    onehot = np.zeros((LANES, GRID_W, GRID_W), np.float32)
    for d in range(2 * NA_KW - 1):
        onehot[d] = ((dc == d) & valid).astype(np.float32)
    negmask = np.where(valid, 0.0, NEG_INF).astype(np.float32)
    return onehot.reshape(LANES, GRID_W * GRID_W), np.tile(negmask, (1, NA_KH))


def bias_tables(rpb, name):
    onehot, negmask = _na_consts()
    nj = 2 * NA_KH - 1
    rows = NA_HEADS * nj
    a = jnp.pad(rpb.reshape(rows, 2 * NA_KW - 1), ((0, 0), (0, LANES - (2 * NA_KW - 1))))

    def body(a_ref, e_ref, o_ref):
        o_ref[...] = _dot(a_ref[...], e_ref[...], precision=HIGHEST)

    t = pl.pallas_call(
        body, name=name, out_shape=jax.ShapeDtypeStruct((rows, GRID_W * GRID_W), F32),
        in_specs=[pl.BlockSpec(memory_space=pltpu.VMEM)] * 2,
        out_specs=pl.BlockSpec(memory_space=pltpu.VMEM),
    )(a, jnp.asarray(onehot))
    t = t.reshape(NA_HEADS, nj, GRID_W, GRID_W)
    tb = jnp.stack([t[:, j0:j0 + NA_KH] for j0 in range(NA_KH)])
    tb = tb.transpose(0, 1, 3, 2, 4).reshape(NA_KH, NA_HEADS, GRID_W, NA_KH * GRID_W)
    return tb + jnp.asarray(negmask)[None, None]


def bias_tables_bwd(dtb, name):
    onehot, _ = _na_consts()
    nj = 2 * NA_KH - 1
    d5 = dtb.reshape(NA_KH, NA_HEADS, GRID_W, NA_KH, GRID_W).transpose(0, 3, 1, 2, 4)
    d2 = d5.reshape(NA_KH * NA_KH * NA_HEADS, GRID_W * GRID_W)

    def body(d_ref, e_ref, o_ref):
        r = _dot(d_ref[...], e_ref[...], NT, precision=HIGHEST)
        for j in range(nj):
            acc = jnp.zeros((NA_HEADS, LANES), F32)
            for j0 in range(NA_KH):
                kk = j - j0
                if 0 <= kk < NA_KH:
                    base = (j0 * NA_KH + kk) * NA_HEADS
                    acc = acc + r[base:base + NA_HEADS, :]
            o_ref[j] = acc

    out = pl.pallas_call(
        body, name=name, out_shape=jax.ShapeDtypeStruct((nj, NA_HEADS, LANES), F32),
        in_specs=[pl.BlockSpec(memory_space=pltpu.VMEM)] * 2,
        out_specs=pl.BlockSpec(memory_space=pltpu.VMEM),
        compiler_params=pltpu.CompilerParams(vmem_limit_bytes=VMEM_LIMIT),
    )(d2, jnp.asarray(onehot))
    return out[:, :, :2 * NA_KW - 1].transpose(1, 0, 2)


def _head_masks():
    lane = lax.broadcasted_iota(jnp.int32, (1, LANES), 1)
    return [(lane >= h * HEAD_DIM) & (lane < (h + 1) * HEAD_DIM) for h in range(HEADS_PER_BLOCK)]


def _row_window(r, rows):
    rs = jnp.clip(r - NA_KH // 2, 0, rows - NA_KH)
    return rs - r + NA_KH - 1, pl.multiple_of(rs * GRID_W, GRID_W)


def _stack_heads(t, masks):
    return jnp.concatenate([jnp.where(mk, t, jnp.zeros_like(t)) for mk in masks], axis=0)


def _unstack_heads(t2, masks):
    out = t2[(HEADS_PER_BLOCK - 1) * GRID_W:, :]
    for h in reversed(range(HEADS_PER_BLOCK - 1)):
        out = jnp.where(masks[h], t2[h * GRID_W:(h + 1) * GRID_W, :], out)
    return out


NA_ROWS_PER_STEP = 4
NA_STEP = NA_ROWS_PER_STEP * GRID_W
SLAB = NA_KH * GRID_W
K_COL = N_HEAD_BLOCKS
V_COL = 2 * N_HEAD_BLOCKS


def na_fwd(qkv, tb, s, name, exchange=None):
    n = qkv.shape[0]
    l = n - s
    rows = s // GRID_W
    rr = NA_ROWS_PER_STEP
    x_steps = rows // rr

    def core(q_ref, k_ref, v_ref, kc_ref, vc_ref, tb_ref, o_ref, lse_ref):
        rb = pl.program_id(1)

        @pl.when(rb >= x_steps)
        def _():
            o_ref[...] = jnp.zeros_like(o_ref)
            lse_ref[...] = jnp.zeros_like(lse_ref)

        @pl.when(rb < x_steps)
        def _():
            masks = _head_masks()
            kcb, vcb = kc_ref[...], vc_ref[...]
            hq = HEADS_PER_BLOCK * GRID_W
            wins, q2s = [], []
            for t in range(rr):
                j0, off = _row_window(rb * rr + t, rows)
                wins.append((j0, off))
                q2s.append(_stack_heads(q_ref[t * GRID_W:(t + 1) * GRID_W, :] * ATT_SCALE, masks))
            s_ctx_all = _dot(jnp.concatenate(q2s, axis=0), kcb, NT)
            scores = []
            for t, (j0, off) in enumerate(wins):
                s_loc = _dot(q2s[t], k_ref[pl.ds(off, SLAB), :], NT) + tb_ref[j0].reshape(hq, SLAB)
                scores.append((s_loc, s_ctx_all[t * hq:(t + 1) * hq, :]))
            probs = []
            for s_loc, s_ctx in scores:
                m = jnp.maximum(jnp.max(s_loc, axis=-1, keepdims=True), jnp.max(s_ctx, axis=-1, keepdims=True))
                p_loc = jnp.exp(s_loc - m)
                p_ctx = jnp.exp(s_ctx - m)
                den = jnp.sum(p_loc, axis=-1, keepdims=True) + jnp.sum(p_ctx, axis=-1, keepdims=True)
                probs.append((p_loc.astype(BF16), p_ctx.astype(BF16), den, m + jnp.log(den)))
            o_ctx_all = _dot(jnp.concatenate([p[1] for p in probs], axis=0), vcb)
            for t, (p_loc, p_ctx, den, lse2) in enumerate(probs):
                o2 = (_dot(p_loc, v_ref[pl.ds(wins[t][1], SLAB), :]) + o_ctx_all[t * hq:(t + 1) * hq, :]) / den
                o_ref[t * GRID_W:(t + 1) * GRID_W, :] = _unstack_heads(o2, masks).astype(BF16)
                lse_ref[0, t * GRID_W:(t + 1) * GRID_W, :] = _unstack_heads(lse2, masks)

    cb = s // l
    outs, xo = carrier_call(
        core, name=name, grid=(N_HEAD_BLOCKS, n // NA_STEP),
        in_specs=[pl.BlockSpec((NA_STEP, LANES), lambda hb, rb: (jnp.minimum(rb, x_steps - 1), hb)),
                  pl.BlockSpec((s, LANES), lambda hb, rb: (0, K_COL + hb)),
                  pl.BlockSpec((s, LANES), lambda hb, rb: (0, V_COL + hb)),
                  pl.BlockSpec((l, LANES), lambda hb, rb: (cb, K_COL + hb)),
                  pl.BlockSpec((l, LANES), lambda hb, rb: (cb, V_COL + hb)),
                  pl.BlockSpec((NA_KH, HEADS_PER_BLOCK, GRID_W, SLAB), lambda hb, rb: (0, hb, 0, 0))],
        out_specs=[pl.BlockSpec((NA_STEP, LANES), lambda hb, rb: (rb, hb)),
                   pl.BlockSpec((1, NA_STEP, LANES), lambda hb, rb: (hb, rb, 0))],
        out_shape=[jax.ShapeDtypeStruct((n, NA_WIDTH), BF16), jax.ShapeDtypeStruct((N_HEAD_BLOCKS, n, LANES), F32)],
        inputs=[qkv, qkv, qkv, qkv, qkv, tb], exchange=exchange)
    return outs, xo


def na_bwd(qkv, tb, o, dmix, lse, s, name, exchange=None):
    n = qkv.shape[0]
    l = n - s
    rows = s // GRID_W
    rr = NA_ROWS_PER_STEP
    x_steps = rows // rr

    def core(q_ref, k_ref, v_ref, kc_ref, vc_ref, tb_ref, o_ref, do_ref, lse_ref, dq_ref, dk_ref, dv_ref, dtb_ref):
        rb = pl.program_id(1)

        @pl.when(rb == 0)
        def _():
            dk_ref[...] = jnp.zeros_like(dk_ref)
            dv_ref[...] = jnp.zeros_like(dv_ref)
            dtb_ref[...] = jnp.zeros_like(dtb_ref)

        @pl.when(rb >= x_steps)
        def _():
            dq_ref[...] = jnp.zeros_like(dq_ref)

        @pl.when(rb < x_steps)
        def _():
            masks = _head_masks()
            kcb, vcb = kc_ref[...], vc_ref[...]
            hq = HEADS_PER_BLOCK * GRID_W
            rows1 = []
            for t in range(rr):
                j0, off = _row_window(rb * rr + t, rows)
                sl = slice(t * GRID_W, (t + 1) * GRID_W)
                q2 = _stack_heads(q_ref[sl, :] * ATT_SCALE, masks)
                do_f = do_ref[sl, :]
                do2 = _stack_heads(do_f.astype(BF16), masks)
                dd = do_f * o_ref[sl, :].astype(F32)
                delta2 = jnp.concatenate(
                    [jnp.sum(jnp.where(mk, dd, 0.0), axis=-1, keepdims=True) for mk in masks], axis=0)
                lse_t = lse_ref[0, sl, :]
                lse2 = jnp.concatenate(
                    [lse_t[:, h * HEAD_DIM:h * HEAD_DIM + 1] for h in range(HEADS_PER_BLOCK)], axis=0)
                rows1.append((j0, off, q2, do2, lse2, delta2))
            q2_all = jnp.concatenate([r[2] for r in rows1], axis=0)
            do2_all = jnp.concatenate([r[3] for r in rows1], axis=0)
            lse2_all = jnp.concatenate([r[4] for r in rows1], axis=0)
            delta2_all = jnp.concatenate([r[5] for r in rows1], axis=0)
            s_ctx_all = _dot(q2_all, kcb, NT) - lse2_all
            dp_ctx_all = _dot(do2_all, vcb, NT) - delta2_all
            stage1 = []
            for j0, off, q2, do2, lse2, delta2 in rows1:
                kslab = k_ref[pl.ds(off, SLAB), :]
                vslab = v_ref[pl.ds(off, SLAB), :]
                s_loc = _dot(q2, kslab, NT) + tb_ref[j0].reshape(hq, SLAB) - lse2
                dp_loc = _dot(do2, vslab, NT) - delta2
                stage1.append((j0, off, q2, do2, s_loc, dp_loc))
            p_ctx_all = jnp.exp(s_ctx_all)
            ds_ctx_all = (p_ctx_all * dp_ctx_all).astype(BF16)
            p_ctx_all = p_ctx_all.astype(BF16)
            stage2 = []
            for j0, off, q2, do2, s_loc, dp_loc in stage1:
                p_loc = jnp.exp(s_loc)
                ds_loc = p_loc * dp_loc
                dtb_ref[j0] += ds_loc.reshape(HEADS_PER_BLOCK, GRID_W, SLAB)
                stage2.append((off, q2, do2, p_loc.astype(BF16), ds_loc.astype(BF16)))
            dq_ctx_all = _dot(ds_ctx_all, kcb)
            for t, (off, q2, do2, p_loc, ds_loc) in enumerate(stage2):
                dq2 = (_dot(ds_loc, k_ref[pl.ds(off, SLAB), :]) + dq_ctx_all[t * hq:(t + 1) * hq, :]) * ATT_SCALE
                dq_ref[t * GRID_W:(t + 1) * GRID_W, :] = _unstack_heads(dq2, masks)
                dk_ref[pl.ds(off, SLAB), :] += _dot(ds_loc, q2, TN)
                dv_ref[pl.ds(off, SLAB), :] += _dot(p_loc, do2, TN)
            dk_ref[s:, :] += _dot(ds_ctx_all, q2_all, TN)
            dv_ref[s:, :] += _dot(p_ctx_all, do2_all, TN)

    cb = s // l
    clamp = lambda hb, rb: (jnp.minimum(rb, x_steps - 1), hb)
    tile_in = pl.BlockSpec((NA_STEP, LANES), clamp)
    whole_out = pl.BlockSpec((n, LANES), lambda hb, rb: (0, hb))
    tbs = pl.BlockSpec((NA_KH, HEADS_PER_BLOCK, GRID_W, SLAB), lambda hb, rb: (0, hb, 0, 0))
    f32n = jax.ShapeDtypeStruct((n, NA_WIDTH), F32)
    outs, xo = carrier_call(
        core, name=name, grid=(N_HEAD_BLOCKS, n // NA_STEP),
        in_specs=[tile_in,
                  pl.BlockSpec((s, LANES), lambda hb, rb: (0, K_COL + hb)),
                  pl.BlockSpec((s, LANES), lambda hb, rb: (0, V_COL + hb)),
                  pl.BlockSpec((l, LANES), lambda hb, rb: (cb, K_COL + hb)),
                  pl.BlockSpec((l, LANES), lambda hb, rb: (cb, V_COL + hb)),
                  tbs, tile_in, tile_in,
                  pl.BlockSpec((1, NA_STEP, LANES), lambda hb, rb: (hb, jnp.minimum(rb, x_steps - 1), 0))],
        out_specs=[pl.BlockSpec((NA_STEP, LANES), lambda hb, rb: (rb, hb)), whole_out, whole_out, tbs],
        out_shape=[f32n, f32n, f32n, jax.ShapeDtypeStruct((NA_KH, NA_HEADS, GRID_W, SLAB), F32)],
        inputs=[qkv, qkv, qkv, qkv, qkv, tb, o, dmix, lse], exchange=exchange)
    return outs, xo


def ctx_attn_fwd(qkv, na, s, name):
    n = qkv.shape[0]
    l = n - s
    cb = s // l

    def core(q_ref, k_ref, v_ref, na_in, o_ref, lse_ref):
        masks = _head_masks()
        qt, kb, vb = q_ref[...], k_ref[...], v_ref[...]
        o_acc = jnp.zeros((l, LANES), F32)
        lse_acc = jnp.zeros((l, LANES), F32)
        for h in range(HEADS_PER_BLOCK):
            qh = jnp.where(masks[h], qt, jnp.zeros_like(qt))
            sc = _dot(qh, kb, NT) * ATT_SCALE
            m = jnp.max(sc, axis=-1, keepdims=True)
            p = jnp.exp(sc - m)
            den = jnp.sum(p, axis=-1, keepdims=True)
            o_acc = jnp.where(masks[h], _dot(p.astype(BF16), vb) / den, o_acc)
            lse_acc = jnp.where(masks[h], m + jnp.log(den), lse_acc)
        o_ref[...] = o_acc.astype(BF16)
        lse_ref[0] = lse_acc

    outs, _ = carrier_call(
        core, name=name, grid=(N_HEAD_BLOCKS,),
        in_specs=[pl.BlockSpec((l, LANES), lambda hb: (cb, hb)), pl.BlockSpec((l, LANES), lambda hb: (cb, K_COL + hb)),
                  pl.BlockSpec((l, LANES), lambda hb: (cb, V_COL + hb)), ANY],
        out_specs=[pl.BlockSpec((l, LANES), lambda hb: (cb, hb)), pl.BlockSpec((1, l, LANES), lambda hb: (hb, 0, 0))],
        out_shape=[jax.ShapeDtypeStruct(na.shape, BF16), jax.ShapeDtypeStruct((N_HEAD_BLOCKS, l, LANES), F32)],
        inputs=[qkv, qkv, qkv, na], aliases={3: 0})
    return outs


def ctx_attn_bwd(qkv, na, dmix, lse, dq, dk, dv, s, name):
    n = qkv.shape[0]
    l = n - s
    cb = s // l

    def core(q_ref, k_ref, v_ref, o_ref, do_ref, lse_ref, dq_in, dk_in, dv_in, dq_ref, dk_ref, dv_ref):
        masks = _head_masks()
        qt, kb, vb = q_ref[...], k_ref[...], v_ref[...]
        do_f = do_ref[...]
        dd = do_f * o_ref[...].astype(F32)
        do_b = do_f.astype(BF16)
        lse_t = lse_ref[0]
        dq_acc = jnp.zeros((l, LANES), F32)
        dk_acc = jnp.zeros((l, LANES), F32)
        dv_acc = jnp.zeros((l, LANES), F32)
        for h in range(HEADS_PER_BLOCK):
            qh = jnp.where(masks[h], qt, jnp.zeros_like(qt))
            doh = jnp.where(masks[h], do_b, jnp.zeros_like(do_b))
            delta = jnp.sum(jnp.where(masks[h], dd, 0.0), axis=-1, keepdims=True)
            p = jnp.exp(_dot(qh, kb, NT) * ATT_SCALE - lse_t[:, h * HEAD_DIM:h * HEAD_DIM + 1])
            ds = (p * (_dot(doh, vb, NT) - delta)).astype(BF16)
            dq_acc = jnp.where(masks[h], _dot(ds, kb) * ATT_SCALE, dq_acc)
            dk_acc = dk_acc + _dot(ds, qh, TN)
            dv_acc = dv_acc + _dot(p.astype(BF16), doh, TN)
        dq_ref[...] = dq_acc
        dk_ref[...] = dk_in[...] + dk_acc * ATT_SCALE
        dv_ref[...] = dv_in[...] + dv_acc

    blk = pl.BlockSpec((l, LANES), lambda hb: (cb, hb))
    f32n = jax.ShapeDtypeStruct((n, NA_WIDTH), F32)
    outs, _ = carrier_call(
        core, name=name, grid=(N_HEAD_BLOCKS,),
        in_specs=[blk, pl.BlockSpec((l, LANES), lambda hb: (cb, K_COL + hb)),
                  pl.BlockSpec((l, LANES), lambda hb: (cb, V_COL + hb)), blk, blk,
                  pl.BlockSpec((1, l, LANES), lambda hb: (hb, 0, 0)), ANY, blk, blk],
        out_specs=[blk, blk, blk], out_shape=[f32n, f32n, f32n],
        inputs=[qkv, qkv, qkv, na, dmix, lse, dq, dk, dv], aliases={6: 0, 7: 1, 8: 2})
    return outs


def _pool_consts(l):
    assert l == TM
    mem = np.zeros((2, POOL_GROUPS, TM, TM), np.float32)
    inv = np.zeros((2, POOL_GROUPS, TM, LANES), np.float32)
    for which, length in ((0, GRID_W), (1, l)):
        t = np.arange(length)
        for g, w in enumerate(POOL_WINDOWS):
            lo = np.clip(t - w // 2, 0, length)
            hi = np.clip(t - w // 2 + w, 0, length)
            blockm = ((t[None, :] >= lo[:, None]) & (t[None, :] < hi[:, None])).astype(np.float32)
            cnt = (hi - lo).astype(np.float32)
            for b in range(TM // length):
                mem[which, g, b * length:(b + 1) * length, b * length:(b + 1) * length] = blockm
                inv[which, g, b * length:(b + 1) * length, :] = (1.0 / cnt)[:, None]
    return mem, np.ascontiguousarray(mem.transpose(0, 1, 3, 2)), inv


def _split_dot(m01, val):
    hi = val.astype(BF16)
    lo = (val - hi.astype(F32)).astype(BF16)
    return _dot(m01, hi) + _dot(m01, lo)


def pool_fwd(u, mem, inv, wp, scale, nx_tiles, name):
    n = u.shape[0]
    tp = _pick(n, (3 * TM, TM))
    nsb = tp // TM

    def core(u_ref, m_ref, i_ref, wp_ref, s_ref, o_ref):
        i = pl.program_id(0)
        for sb in range(nsb):
            kind = ((i * nsb + sb) >= nx_tiles).astype(jnp.int32)
            rows = slice(sb * TM, (sb + 1) * TM)
            for g in range(POOL_GROUPS):
                sl = slice(g * POOL_CH, (g + 1) * POOL_CH)
                ug = u_ref[rows, sl]
                dg = _split_dot(m_ref[kind, g], ug) * i_ref[kind, g] - ug
                o_ref[rows, sl] = (_dot(dg.astype(BF16), wp_ref[g]) * s_ref[:, sl]).astype(BF16)

    whole4 = lambda i: (0, 0, 0, 0)
    outs, _ = carrier_call(
        core, name=name, grid=(n // tp,),
        in_specs=[pl.BlockSpec((tp, POOL_WIDTH), lambda i: (i, 0)),
                  pl.BlockSpec((2, POOL_GROUPS, TM, TM), whole4),
                  pl.BlockSpec((2, POOL_GROUPS, TM, LANES), whole4),
                  pl.BlockSpec((POOL_GROUPS, POOL_CH, POOL_CH), lambda i: (0, 0, 0)),
                  pl.BlockSpec((1, POOL_WIDTH), lambda i: (0, 0))],
        out_specs=[pl.BlockSpec((tp, POOL_WIDTH), lambda i: (i, 0))],
        out_shape=[jax.ShapeDtypeStruct((n, POOL_WIDTH), BF16)], inputs=[u, mem, inv, wp, scale])
    return outs[0]


def pool_bwd(dmix, u, mem, mem_t, inv, wp, scale, nx_tiles, name):
    n = u.shape[0]
    tp = _pick(n, (3 * TM, TM))
    nsb = tp // TM

    def core(dy_ref, u_ref, m_ref, mt_ref, i_ref, wp_ref, s_ref, du_ref, dwp_ref, dsc_ref):
        @pl.when(pl.program_id(0) == 0)
        def _():
            dwp_ref[...] = jnp.zeros_like(dwp_ref)
            dsc_ref[...] = jnp.zeros_like(dsc_ref)

        i = pl.program_id(0)
        for sb in range(nsb):
            kind = ((i * nsb + sb) >= nx_tiles).astype(jnp.int32)
            rows = slice(sb * TM, (sb + 1) * TM)
            for g in range(POOL_GROUPS):
                sl = slice(g * POOL_CH, (g + 1) * POOL_CH)
                ug = u_ref[rows, sl]
                dy = dy_ref[rows, sl]
                dg = (_split_dot(m_ref[kind, g], ug) * i_ref[kind, g] - ug).astype(BF16)
                z = _dot(dg, wp_ref[g])
                dsc_ref[0:1, sl] += jnp.sum(dy * z, axis=0, keepdims=True)
                dz = (dy * s_ref[:, sl]).astype(BF16)
                dwp_ref[g] += _dot(dg, dz, TN)
                dd = _dot(dz, wp_ref[g], NT)
                du_ref[rows, sl] = _split_dot(mt_ref[kind, g], dd * i_ref[kind, g]) - dd

    whole4 = lambda i: (0, 0, 0, 0)
    outs, _ = carrier_call(
        core, name=name, grid=(n // tp,),
        in_specs=[pl.BlockSpec((tp, POOL_WIDTH), lambda i: (i, 1)),
                  pl.BlockSpec((tp, POOL_WIDTH), lambda i: (i, 0)),
                  pl.BlockSpec((2, POOL_GROUPS, TM, TM), whole4),
                  pl.BlockSpec((2, POOL_GROUPS, TM, TM), whole4),
                  pl.BlockSpec((2, POOL_GROUPS, TM, LANES), whole4),
                  pl.BlockSpec((POOL_GROUPS, POOL_CH, POOL_CH), lambda i: (0, 0, 0)),
                  pl.BlockSpec((1, POOL_WIDTH), lambda i: (0, 0))],
        out_specs=[pl.BlockSpec((tp, POOL_WIDTH), lambda i: (i, 0)),
                   pl.BlockSpec((POOL_GROUPS, POOL_CH, POOL_CH), lambda i: (0, 0, 0)),
                   pl.BlockSpec((8, POOL_WIDTH), lambda i: (0, 0))],
        out_shape=[jax.ShapeDtypeStruct((n, POOL_WIDTH), F32),
                   jax.ShapeDtypeStruct((POOL_GROUPS, POOL_CH, POOL_CH), F32),
                   jax.ShapeDtypeStruct((8, POOL_WIDTH), F32)],
        inputs=[dmix, u, mem, mem_t, inv, wp, scale])
    return outs


MOD_ROWS = 16


def mod_fwd(cvecs, w, b, name):
    _, d = cvecs.shape
    cl = w.shape[2]
    tc = _pick(cl, (384, 128))

    def core(c_ref, w_ref, b_ref, o_ref):
        a = _silu(c_ref[...]).astype(BF16)
        o_ref[0] = _dot(a, w_ref[0].astype(BF16)) + b_ref[0]

    outs, _ = carrier_call(
        core, name=name, grid=(DEPTH, cl // tc),
        in_specs=[pl.BlockSpec((MOD_ROWS, d), lambda li, j: (0, 0)),
                  pl.BlockSpec((1, d, tc), lambda li, j: (li, 0, j)),
                  pl.BlockSpec((1, 1, tc), lambda li, j: (li, 0, j))],
        out_specs=[pl.BlockSpec((1, MOD_ROWS, tc), lambda li, j: (li, 0, j))],
        out_shape=[jax.ShapeDtypeStruct((DEPTH, MOD_ROWS, cl), F32)], inputs=[cvecs, w, b])
    return outs[0]


def mod_bwd(cvecs, dm, w, name):
    _, d = cvecs.shape
    cl = w.shape[2]
    tc = _pick(cl, (384, 128))

    def core(c_ref, dm_ref, w_ref, dw_ref, da_ref):
        @pl.when((pl.program_id(0) == 0) & (pl.program_id(1) == 0))
        def _():
            da_ref[...] = jnp.zeros_like(da_ref)

        a = _silu(c_ref[...]).astype(BF16)
        dmb = dm_ref[0].astype(BF16)
        dw_ref[0] = _dot(a, dmb, TN)
        da_ref[...] += _dot(dmb, w_ref[0].astype(BF16), NT)

    outs, _ = carrier_call(
        core, name=name, grid=(DEPTH, cl // tc),
        in_specs=[pl.BlockSpec((MOD_ROWS, d), lambda li, j: (0, 0)),
                  pl.BlockSpec((1, MOD_ROWS, tc), lambda li, j: (li, 0, j)),
                  pl.BlockSpec((1, d, tc), lambda li, j: (li, 0, j))],
        out_specs=[pl.BlockSpec((1, d, tc), lambda li, j: (li, 0, j)),
                   pl.BlockSpec((MOD_ROWS, d), lambda li, j: (0, 0))],
        out_shape=[jax.ShapeDtypeStruct((DEPTH, d, cl), F32), jax.ShapeDtypeStruct((MOD_ROWS, d), F32)],
        inputs=[cvecs, dm, w])
    return outs


def loss_head(y, target, name):
    n, d = y.shape
    s = target.shape[0]
    nt, nx = n // TM, s // TM

    def core(y_ref, t_ref, l_ref, dy_ref, acc_ref):
        i = pl.program_id(0)

        @pl.when(i == 0)
        def _():
            acc_ref[...] = jnp.zeros_like(acc_ref)

        @pl.when(i < nx)
        def _():
            e = y_ref[...] - t_ref[...]
            dy_ref[...] = e * (1.0 / d)
            acc_ref[...] += jnp.sum(e * e, axis=0, keepdims=True)

        @pl.when(i >= nx)
        def _():
            dy_ref[...] = jnp.zeros_like(dy_ref)

        @pl.when(i == nt - 1)
        def _():
            l_ref[...] = jnp.sum(acc_ref[...], axis=1, keepdims=True) * (0.5 / d)

    tile = pl.BlockSpec((TM, d), lambda i: (i, 0))
    outs, _ = carrier_call(
        core, name=name, grid=(nt,),
        in_specs=[tile, pl.BlockSpec((TM, d), lambda i: (jnp.minimum(i, nx - 1), 0))],
        out_specs=[pl.BlockSpec((1, 1), lambda i: (0, 0)), tile],
        out_shape=[jax.ShapeDtypeStruct((1, 1), F32), jax.ShapeDtypeStruct((n, d), F32)],
        scratch_shapes=[pltpu.VMEM((1, d), F32)], inputs=[y, target])
    return outs


def sum_devices(v, name):
    _, r, c = v.shape
    tr = _pick(r, (64, 8))

    def core(v_ref, o_ref):
        acc = v_ref[0]
        for p in range(1, N_DEV):
            acc = acc + v_ref[p]
        o_ref[...] = acc

    outs, _ = carrier_call(
        core, name=name, grid=(r // tr,), in_specs=[pl.BlockSpec((N_DEV, tr, c), lambda i: (0, i, 0))],
        out_specs=[pl.BlockSpec((tr, c), lambda i: (i, 0))], out_shape=[jax.ShapeDtypeStruct((r, c), F32)],
        inputs=[v])
    return outs[0]


def cctx_grad(parts, c_ctx, name):
    d = c_ctx.shape[1]

    def body(p_ref, c_ref, o_ref):
        acc = p_ref[0]
        for p in range(1, N_DEV):
            acc = acc + p_ref[p]
        o_ref[...] = acc[8:9, :] * _dsilu(c_ref[...])

    return pl.pallas_call(
        body, name=name, out_shape=jax.ShapeDtypeStruct((1, d), F32),
        in_specs=[pl.BlockSpec(memory_space=pltpu.VMEM)] * 2,
        out_specs=pl.BlockSpec(memory_space=pltpu.VMEM),
    )(parts, c_ctx)


def _adam_math(w, g, m, v):
    m2 = ADAM_B1 * m + (1.0 - ADAM_B1) * g
    v2 = ADAM_B2 * v + (1.0 - ADAM_B2) * (g * g)
    m_hat = m2 / (1.0 - ADAM_B1 ** ADAM_STEP)
    v_hat = v2 / (1.0 - ADAM_B2 ** ADAM_STEP)
    delta = -ADAM_LR * (m_hat / (jnp.sqrt(v_hat) + ADAM_EPS) + ADAM_WD * w)
    return delta, m2, v2


def adamw(w, g, m, v, name):
    r, c = w.shape
    tr = _pick(r, (256, 128, 64, 32, 16, 8, r))

    def core(w_ref, g_ref, m_ref, v_ref, d_ref, m2_ref, v2_ref):
        d_ref[...], m2_ref[...], v2_ref[...] = _adam_math(w_ref[...], g_ref[...], m_ref[...], v_ref[...])

    tile = pl.BlockSpec((tr, c), lambda i: (i, 0))
    out = jax.ShapeDtypeStruct((r, c), F32)
    outs, _ = carrier_call(core, name=name, grid=(r // tr,), in_specs=[tile] * 4, out_specs=[tile] * 3,
                           out_shape=[out] * 3, inputs=[w, g, m, v])
    return outs


def reduce_adamw(recv, w, m, v, name):
    r, c = w.shape
    tr = _pick(r, (256, 128, 64, 8))

    def core(recv_ref, w_ref, m_ref, v_ref, g_ref, d_ref, m2_ref, v2_ref):
        acc = recv_ref[0].astype(F32)
        for p in range(1, N_DEV):
            acc = acc + recv_ref[p].astype(F32)
        g_ref[...] = acc
        d_ref[...], m2_ref[...], v2_ref[...] = _adam_math(w_ref[...], acc, m_ref[...], v_ref[...])

    tile = pl.BlockSpec((tr, c), lambda i: (i, 0))
    out = jax.ShapeDtypeStruct((r, c), F32)
    outs, _ = carrier_call(
        core, name=name, grid=(r // tr,),
        in_specs=[pl.BlockSpec((N_DEV, tr, c), lambda i: (0, i, 0)), tile, tile, tile],
        out_specs=[tile] * 4, out_shape=[out] * 4, inputs=[recv, w, m, v])
    return outs


def kernel(x, c, ctx, c_ctx, w_mod, b_mod, norm_g, w_ffn_gate_up, w_ffn_down, w_in, w_out, na_rpb, w_pool, pool_scale, loss_target, m_c_ctx, m_w_mod, m_b_mod, m_norm_g, m_w_ffn_gate_up, m_w_ffn_down, m_w_in, m_w_out, m_na_rpb, m_w_pool, m_pool_scale, v_c_ctx, v_w_mod, v_b_mod, v_norm_g, v_w_ffn_gate_up, v_w_ffn_down, v_w_in, v_w_out, v_na_rpb, v_w_pool, v_pool_scale):
    s, d = x.shape[1], x.shape[2]
    l = ctx.shape[1]
    n = s + l
    nx = s // TM
    fq = w_ffn_gate_up.shape[-1]
    fr = w_ffn_down.shape[2]
    cl = w_mod.shape[2]
    dl = norm_g.shape[2]
    me = 4 * lax.axis_index("x") + 2 * lax.axis_index("y") + lax.axis_index("c")

    c_all = all_gather(c, "gather_c").reshape(N_DEV, d)
    cvecs = jnp.concatenate([c_all, c_ctx[None, :], jnp.zeros((MOD_ROWS - N_DEV - 1, d), F32)], axis=0)
    b_loc = lax.dynamic_slice(b_mod, (0, me * cl), (DEPTH, cl)).reshape(DEPTH, 1, cl)
    mod_loc = mod_fwd(cvecs, w_mod, b_loc, "mod_fwd")
    mod_all = all_gather(mod_loc.reshape(DEPTH * MOD_ROWS, cl), "gather_mod")
    mod_all = mod_all.reshape(N_DEV, DEPTH, MOD_ROWS, cl).transpose(1, 2, 0, 3).reshape(DEPTH, MOD_ROWS, N_DEV * cl)
    mine = lax.dynamic_slice(mod_all, (0, me, 0), (DEPTH, 1, N_DEV * cl))
    mods = jnp.concatenate([mine, mod_all[:, N_DEV:N_DEV + 1]], axis=1).reshape(DEPTH, 2, N_MOD, d)

    gu_b = w_ffn_gate_up.astype(BF16)
    dn_b = w_ffn_down.astype(BF16)
    wi_b = w_in.astype(BF16)
    wo_b = w_out.astype(BF16)
    wp_b = w_pool.astype(BF16)

    def ffn_shards(li, i):
        return [gu_b[li, i], dn_b[li, i]]

    def mix_shards(li):
        return [wi_b[li], wo_b[li]]

    def as_ffn_weights(gathered):
        return gathered[0].reshape(2, 4, d, fq), gathered[1].reshape(4, 2 * fr, d)

    def as_mix_weights(gathered):
        return gathered[0], gathered[1].reshape(N_DEV * wo_b.shape[1], d)

    tables = _rope_tables(s, n)
    mem_np, mem_t_np, inv_np = _pool_consts(l)
    mem, mem_t, inv = jnp.asarray(mem_np, BF16), jnp.asarray(mem_t_np, BF16), jnp.asarray(inv_np)
    first = gather_two_level([norm_g.reshape(DEPTH * 6, dl), gu_b[0, 0]], "gather_first")
    g_full = first[0].reshape(N_DEV, DEPTH, 6, dl).transpose(1, 2, 0, 3).reshape(DEPTH, 6, 1, N_DEV * dl)

    weights = {("ffn", 0, 0): (first[1].reshape(2, 4, d, fq), None)}
    saved = {}
    xcur = jnp.concatenate([x[0], ctx[0]], axis=0)
    for li in range(DEPTH):
        last = li == DEPTH - 1
        for i in range(2):
            tag = f"l{li}_ffn{i}"
            wgu, wd4 = weights[("ffn", li, i)]
            if i == 0:
                ex_up = Exchange(gathers=mix_shards(li) + ([dn_b[0, 0]] if wd4 is None else []))
                ex_dn = Exchange(gathers=[dn_b[li, 1]])
            elif not last:
                ex_up, ex_dn = Exchange(gathers=[gu_b[li + 1, 0]]), Exchange(gathers=[dn_b[li + 1, 0]])
            else:
                ex_up = ex_dn = None
            (hb, gu, a4), got_up = ffn_up(xcur, g_full[li, 4 * i], mods[li], wgu, s, 6 * i, tag + "_up", ex_up)
            if wd4 is None:
                wd4 = got_up.pop().reshape(4, 2 * fr, d)
                weights[("ffn", li, i)] = (wgu, wd4)
            (ff, xnext), got_dn = ffn_down(a4, wd4, xcur, g_full[li, 4 * i + 1], mods[li], s, 6 * i + 2, tag + "_down", ex_dn)
            saved[("ffn", li, i)] = (xcur, hb, gu, a4, ff)
            xcur = xnext
            if i == 0:
                weights[("mix", li)] = as_mix_weights(got_up)
                next_dn = got_dn
            elif not last:
                weights[("ffn", li + 1, 0)] = as_ffn_weights(got_up + got_dn)
            if i == 0:
                tag = f"l{li}_mix"
                win8, wout = weights[("mix", li)]
                (hb, qkv, u), _ = mix_in(xcur, g_full[li, 2], mods[li], win8, tables, s, tag + "_in")
                tb = bias_tables(na_rpb[li], tag + "_bias")
                (na, lse), got = na_fwd(qkv, tb, s, tag + "_na", Exchange(gathers=[gu_b[li, 1]]))
                weights[("ffn", li, 1)] = as_ffn_weights(got + next_dn)
                lse_c = None
                if not last:
                    na, lse_c = ctx_attn_fwd(qkv, na, s, tag + "_ctx_attn")
                py = pool_fwd(u, mem, inv, wp_b[li], pool_scale[li][None, :], nx, tag + "_pool")
                fm, xnext = mix_out(na, py, wout, xcur, g_full[li, 3], mods[li], s, tag + "_out")
                saved[("mix", li)] = (xcur, hb, qkv, u, tb, na, lse, lse_c, py, fm)
                xcur = xnext

    loss_local, dcur = loss_head(xcur, loss_target[0], "loss")
    loss = lax.psum(loss_local[0, 0], ("x", "y", "c"))

    recv = {"gu": lax.empty((N_DEV, 2 * DEPTH, d, fq), BF16), "dn": lax.empty((N_DEV, 2 * DEPTH, fr, d), BF16),
            "wi": lax.empty((N_DEV, DEPTH, d, IN_BLOCK), BF16), "wo": lax.empty((N_DEV, DEPTH, wo_b.shape[1], d), BF16)}
    pending = []

    def take(keys, gathers=()):
        nonlocal pending
        jobs = [(gr, recv[key], st) for key, gr, st in pending if key in keys]
        order = [key for key, _, _ in pending if key in keys]
        pending = [p for p in pending if p[0] not in keys]
        return Exchange(gathers=gathers, a2as=jobs), order

    def pad8(t):
        t = t.reshape(-1, d) if t.size % d == 0 else jnp.pad(t.reshape(-1), (0, -t.size % d)).reshape(-1, d)
        return jnp.pad(t, ((0, -t.shape[0] % 8), (0, 0)))

    def packed(parts):
        parts = [pad8(p) for p in parts]
        offs = np.cumsum([0] + [p.shape[0] for p in parts])
        return jnp.concatenate(parts + [jnp.zeros((-offs[-1] % 64, d), F32)], axis=0), offs

    def put(order, bufs):
        for key, buf in zip(order, bufs):
            recv[key] = buf

    d_rpb, d_wp, d_ps, d_mod, d_g = [], [], [], [], []
    for li in reversed(range(DEPTH)):
        reds = {}
        for i in (1, 0):
            tag = f"l{li}_ffn{i}"
            xin, hb, gu, a4, ff = saved[("ffn", li, i)]
            wgu, wd4 = weights[("ffn", li, i)]
            dff, red1 = post_bwd(ff, dcur, g_full[li, 4 * i + 1], mods[li], 6 * i + 2, 0.5, s, tag + "_post_bwd")
            early = []
            if (li, i) == (0, 0):
                early_small, early_offs = packed([jnp.stack(d_wp), jnp.stack(d_ps), jnp.stack(d_rpb)])
                early = [early_small]
            ex, _ = take((), early)
            g_dn, bufs = grad_weight(a4, dff, tag + "_dwdown", a_lead=4, per_step=4, exchange=ex)
            if early:
                early_sum = sum_devices(bufs[0], "sum_early_small_grads")
            pending += [("dn", g_dn.reshape(N_DEV, fr, d), 2 * li + i)]
            ex, order = take(("dn",))
            dgu, bufs = ffn_da(dff, wd4, gu, tag + "_da", ex)
            put(order, bufs)
            ex, order = take(("wi", "wo"))
            g_gu, bufs = grad_weight(hb, dgu.reshape(N_DEV, n, fq), tag + "_dwgu", b_lead=N_DEV, per_step=4, exchange=ex)
            put(order, bufs)
            pending += [("gu", g_gu, 2 * li + i)]
            ex, order = take(("gu",))
            (dcur, red2), bufs = ffn_dh(dgu, wgu, xin, dcur, g_full[li, 4 * i], mods[li], s, 6 * i, tag + "_dh", ex)
            put(order, bufs)
            reds[i] = (red1, red2)
            if i == 1:
                tag = f"l{li}_mix"
                xin, hb, qkv, u, tb, na, lse, lse_c, py, fm = saved[("mix", li)]
                win8, wout = weights[("mix", li)]
                dfm, redm1 = post_bwd(fm, dcur, g_full[li, 3], mods[li], 5, 1.0, s, tag + "_post_bwd")
                dmix = matmul_nt(dfm, wout, tag + "_dmix")
                g_wo = grad_wout(na, py, dfm, tag + "_dwout").reshape(N_DEV, wo_b.shape[1], d)
                du, gwp, gps = pool_bwd(dmix, u, mem, mem_t, inv, wp_b[li], pool_scale[li][None, :], nx, tag + "_pool_bwd")
                ex, order = take(("gu", "dn"))
                (dq, dk, dv, dtb), bufs = na_bwd(qkv, tb, na, dmix, lse, s, tag + "_na_bwd", ex)
                put(order, bufs)
                if li != DEPTH - 1:
                    dq, dk, dv = ctx_attn_bwd(qkv, na, dmix, lse_c, dq, dk, dv, s, tag + "_ctx_attn_bwd")
                grpb = bias_tables_bwd(dtb, tag + "_bias_bwd")
                dqkvu = qkv_bwd(dq, dk, dv, du, tables, tag + "_rope_bwd")
                (dcur, redm2), _ = mix_dh(dqkvu, win8, xin, dcur, g_full[li, 2], mods[li], s, tag + "_dh")
                g_wi, _ = grad_weight(hb, dqkvu, tag + "_dwin", b_cols=IN_BLOCK, per_step=4)
                pending += [("wi", g_wi, li), ("wo", g_wo, li)]
                d_rpb.insert(0, grpb)
                d_wp.insert(0, gwp)
                d_ps.insert(0, gps[0])
        (ra1, ra2), (rb1, rb2) = reds[0], reds[1]
        d_mod.insert(0, jnp.stack([ra2[:, 0], ra2[:, 1], ra1[:, 0], redm2[:, 0], redm2[:, 1], redm1[:, 0],
                                   rb2[:, 0], rb2[:, 1], rb1[:, 0]], axis=1))
        d_g.insert(0, jnp.stack([t[0] + t[1] for t in (ra2[:, 2], ra1[:, 1], redm2[:, 2], redm1[:, 1], rb2[:, 2], rb1[:, 1])]))
    grad_x = dcur[:s][None]

    small, offs = packed([jnp.stack(d_mod), jnp.stack(d_g)])
    ex, order = take(("gu", "dn", "wi", "wo"), [small])
    bufs = exchange_only(ex, "exchange_last")
    small_all = bufs[0]
    put(order, bufs[1:])
    small_sum = sum_devices(small_all, "sum_small_grads")

    n_mod_rows = DEPTH * 2 * N_MOD
    dmod_all = small_all[:, :n_mod_rows].reshape(N_DEV, DEPTH, 2, N_MOD * d)
    dmod_sum = small_sum[:n_mod_rows].reshape(DEPTH, 2, N_MOD * d)
    dm_rows = jnp.concatenate([dmod_all[:, :, 0].transpose(1, 0, 2), dmod_sum[:, 1:2],
                               jnp.zeros((DEPTH, MOD_ROWS - N_DEV - 1, N_MOD * d), F32)], axis=1)
    grad_b_mod = dmod_sum[:, 0] + dmod_sum[:, 1]
    dm_loc = lax.dynamic_slice(dm_rows, (0, 0, me * cl), (DEPTH, MOD_ROWS, cl))
    grad_w_mod, da_part = mod_bwd(cvecs, dm_loc, w_mod, "mod_bwd")
    da_all = all_gather(da_part, "gather_dcvec")
    grad_c_ctx = cctx_grad(da_all, c_ctx[None, :], "c_ctx_grad")[0]

    grad_norm_full = small_sum[offs[1]:offs[1] + DEPTH * 6].reshape(DEPTH, 6, d)
    grad_norm_g = lax.dynamic_slice(grad_norm_full, (0, 0, me * dl), (DEPTH, 6, dl))
    grad_w_pool = early_sum[early_offs[0]:early_offs[0] + w_pool.size // d].reshape(w_pool.shape)
    grad_pool_scale = early_sum[early_offs[1]:early_offs[1] + pool_scale.size // d].reshape(pool_scale.shape)
    grad_na_rpb = early_sum[early_offs[2]:early_offs[3]].reshape(-1)[:na_rpb.size].reshape(na_rpb.shape)

    def big_adam(key, w, m, v, name):
        shp = w.shape
        cols = shp[-1]
        outs = reduce_adamw(recv[key].reshape(N_DEV, -1, cols), w.reshape(-1, cols), m.reshape(-1, cols),
                            v.reshape(-1, cols), name)
        return tuple(t.reshape(shp) for t in outs)

    def small_adam(w, g, m, v, name):
        shp = w.shape
        cols = shp[-1]
        outs = adamw(w.reshape(-1, cols), g.reshape(-1, cols), m.reshape(-1, cols), v.reshape(-1, cols), name)
        return tuple(t.reshape(shp) for t in outs)

    b_gu = big_adam("gu", w_ffn_gate_up, m_w_ffn_gate_up, v_w_ffn_gate_up, "adam_gate_up")
    b_dn = big_adam("dn", w_ffn_down, m_w_ffn_down, v_w_ffn_down, "adam_down")
    b_wi = big_adam("wi", w_in, m_w_in, v_w_in, "adam_w_in")
    b_wo = big_adam("wo", w_out, m_w_out, v_w_out, "adam_w_out")
    a_cc = small_adam(c_ctx, grad_c_ctx, m_c_ctx, v_c_ctx, "adam_c_ctx")
    a_wm = small_adam(w_mod, grad_w_mod, m_w_mod, v_w_mod, "adam_w_mod")
    a_bm = small_adam(b_mod, grad_b_mod, m_b_mod, v_b_mod, "adam_b_mod")
    a_ng = small_adam(norm_g, grad_norm_g, m_norm_g, v_norm_g, "adam_norm_g")
    a_rp = small_adam(na_rpb, grad_na_rpb, m_na_rpb, v_na_rpb, "adam_na_rpb")
    a_wp = small_adam(w_pool, grad_w_pool, m_w_pool, v_w_pool, "adam_w_pool")
    a_ps = small_adam(pool_scale, grad_pool_scale, m_pool_scale, v_pool_scale, "adam_pool_scale")

    grads = (grad_c_ctx, grad_w_mod, grad_b_mod, grad_norm_g, b_gu[0], b_dn[0], b_wi[0], b_wo[0], grad_na_rpb, grad_w_pool, grad_pool_scale)
    deltas = (a_cc[0], a_wm[0], a_bm[0], a_ng[0], b_gu[1], b_dn[1], b_wi[1], b_wo[1], a_rp[0], a_wp[0], a_ps[0])
    new_m = (a_cc[1], a_wm[1], a_bm[1], a_ng[1], b_gu[2], b_dn[2], b_wi[2], b_wo[2], a_rp[1], a_wp[1], a_ps[1])
    new_v = (a_cc[2], a_wm[2], a_bm[2], a_ng[2], b_gu[3], b_dn[3], b_wi[3], b_wo[3], a_rp[2], a_wp[2], a_ps[2])
    return (loss, grad_x, *grads, *deltas, *new_m, *new_v)
```
